```python
import jax, jax.numpy as jnp
from jax import lax
import numpy as np

D_MODEL = 1024
BATCH = 8
SEQ = 4096
DEPTH = 1

HEAD_DIM = 64
N_ATTN_HEADS = 16
ATTN_WIDTH = N_ATTN_HEADS * HEAD_DIM
ROPE_DIM = HEAD_DIM // 4
ROPE_THETA = 500000.0
DILATED_PATTERNS = ((128, 1), (512, 4), (2048, 16))

D_INNER = 1024
SSM_HEAD_DIM = 64
N_SSM_HEADS = D_INNER // SSM_HEAD_DIM
N_SSM_GROUPS = 4
HEADS_PER_GROUP = N_SSM_HEADS // N_SSM_GROUPS
D_STATE = 128
SSM_CONV = 3
CHUNK = 128
XBC_WIDTH = D_INNER + 2 * N_SSM_GROUPS * D_STATE

MIX_WIDTH = ATTN_WIDTH + D_INNER
IN_WIDTH = 3 * ATTN_WIDTH + D_INNER + XBC_WIDTH + 2 * N_SSM_HEADS

D_FF = 2816
FFN_CONV = 3
EPS = 1e-6

kernel_name = "hymba_dilated_ssd_convffn_encoder"


def rmsnorm(x, w):
    xf = x.astype(jnp.float32)
    y = xf * lax.rsqrt(jnp.mean(xf * xf, axis=-1, keepdims=True) + EPS)
    return (y * w.astype(jnp.float32)).astype(x.dtype)


def dwconv_centered(x, w, b):
    k = w.shape[1]
    rhs = w.T[:, None, :].astype(x.dtype)
    y = lax.conv_general_dilated(x, rhs, window_strides=(1,), padding=[(k // 2, k // 2)],
                                 dimension_numbers=("NWC", "WIO", "NWC"),
                                 feature_group_count=x.shape[-1])
    return y + b.astype(x.dtype)


def partial_rope(t, pos):
    half = ROPE_DIM // 2
    inv_freq = jnp.power(ROPE_THETA, -jnp.arange(half, dtype=jnp.float32) * 2.0 / ROPE_DIM)
    ang = pos[:, None] * inv_freq[None, :]
    cos = jnp.cos(ang)[None, :, None, :]
    sin = jnp.sin(ang)[None, :, None, :]
    tf = t.astype(jnp.float32)
    x1 = tf[..., :half]
    x2 = tf[..., half:ROPE_DIM]
    out = jnp.concatenate([x1 * cos - x2 * sin, x2 * cos + x1 * sin, tf[..., ROPE_DIM:]], axis=-1)
    return out.astype(t.dtype)


def band_attention(q, k, v, half):
    n, length, h, dh = q.shape
    blk = half
    nb = -(-length // blk)
    lp = nb * blk
    qb = jnp.pad(q, [(0, 0), (0, lp - length), (0, 0), (0, 0)]).reshape(n, nb, blk, h, dh)
    pad_kv = [(0, 0), (blk, lp - length + blk), (0, 0), (0, 0)]
    kb = jnp.pad(k, pad_kv).reshape(n, nb + 2, blk, h, dh)
    vb = jnp.pad(v, pad_kv).reshape(n, nb + 2, blk, h, dh)
    kw = jnp.concatenate([kb[:, :-2], kb[:, 1:-1], kb[:, 2:]], axis=2)
    vw = jnp.concatenate([vb[:, :-2], vb[:, 1:-1], vb[:, 2:]], axis=2)
    s = jnp.einsum("nbqhd,nbkhd->nbhqk", qb, kw, preferred_element_type=jnp.float32) * (dh ** -0.5)
    qpos = jnp.arange(nb)[:, None] * blk + jnp.arange(blk)[None, :]
    kpos = jnp.arange(nb)[:, None] * blk - blk + jnp.arange(3 * blk)[None, :]
    valid = ((jnp.abs(kpos[:, None, :] - qpos[:, :, None]) <= half)
             & (kpos >= 0)[:, None, :] & (kpos < length)[:, None, :])
    s = jnp.where(valid[None, :, None], s, -jnp.inf)
    m = jnp.max(s, axis=-1, keepdims=True)
    p = jnp.exp(s - m)
    den = jnp.sum(p, axis=-1)
    o = jnp.einsum("nbhqk,nbkhd->nbqhd", p, vw.astype(jnp.float32))
    o = o / jnp.transpose(den, (0, 1, 3, 2))[..., None]
    lse = jnp.transpose(m[..., 0] + jnp.log(den), (0, 1, 3, 2))
    o = o.reshape(n, lp, h, dh)[:, :length]
    lse = lse.reshape(n, lp, h)[:, :length]
    return o, lse


def dilated_attention(q, k, v):
    b, s, h, dh = q.shape
    outs, lses = [], []
    for window, dil in DILATED_PATTERNS:
        half = (window // 2) // dil
        length = s // dil

        def gather(t):
            t = t.reshape(b, length, dil, h, dh).transpose(0, 2, 1, 3, 4)
            return t.reshape(b * dil, length, h, dh)

        o, lse = band_attention(gather(q), gather(k), gather(v), half)
        o = o.reshape(b, dil, length, h, dh).transpose(0, 2, 1, 3, 4).reshape(b, s, h, dh)
        lse = lse.reshape(b, dil, length, h).transpose(0, 2, 1, 3).reshape(b, s, h)
        outs.append(o)
        lses.append(lse)
    wts = jax.nn.softmax(jnp.stack(lses, axis=0), axis=0)[..., None]
    out = jnp.sum(wts * jnp.stack(outs, axis=0), axis=0)
    return out.astype(q.dtype)


def segsum(a):
    t = a.shape[-1]
    a_rep = jnp.broadcast_to(a[..., :, None], a.shape + (t,))
    a_rep = jnp.where(jnp.tril(jnp.ones((t, t), dtype=bool), -1), a_rep, 0.0)
    cs = jnp.cumsum(a_rep, axis=-2)
    return jnp.where(jnp.tril(jnp.ones((t, t), dtype=bool)), cs, -jnp.inf)


def ssd_scan(x, dt, a_head, bm, cm):
    b, l, g, r, p = x.shape
    c = l // CHUNK
    xc = (x * dt[..., None]).reshape(b, c, CHUNK, g, r, p)
    a = (dt * a_head).reshape(b, c, CHUNK, g, r).transpose(0, 3, 4, 1, 2)
    bc = bm.reshape(b, c, CHUNK, g, -1)
    cc = cm.reshape(b, c, CHUNK, g, -1)
    a_cs = jnp.cumsum(a, axis=-1)
    lmat = jnp.exp(segsum(a))
    cb = jnp.einsum("bclgn,bcsgn->bcgls", cc, bc)
    y_diag = jnp.einsum("bcgls,bgrcls,bcsgrp->bclgrp", cb, lmat, xc)
    decay_states = jnp.exp(a_cs[..., -1:] - a_cs)
    states = jnp.einsum("bclgn,bgrcl,bclgrp->bcgrpn", bc, decay_states, xc)
    states = jnp.concatenate([jnp.zeros_like(states[:, :1]), states], axis=1)
    chunk_tot = jnp.pad(a_cs[..., -1], [(0, 0), (0, 0), (0, 0), (1, 0)])
    decay_chunk = jnp.exp(segsum(chunk_tot))
    states = jnp.einsum("bgrzc,bcgrpn->bzgrpn", decay_chunk, states)[:, :-1]
    y_off = jnp.einsum("bclgn,bcgrpn,bgrcl->bclgrp", cc, states, jnp.exp(a_cs))
    return (y_diag + y_off).reshape(b, l, g, r, p)


def ssm_mixer(z, xbc, dt_f_raw, dt_b_raw, conv_w, conv_b, a_log_f, a_log_b,
              dt_bias_f, dt_bias_b, d_skip, norm_w):
    b, l, _ = z.shape
    xbc = jax.nn.silu(dwconv_centered(xbc, conv_w, conv_b))
    gn = N_SSM_GROUPS * D_STATE
    xs = xbc[..., :D_INNER].astype(jnp.float32).reshape(b, l, N_SSM_GROUPS, HEADS_PER_GROUP, SSM_HEAD_DIM)
    bm = xbc[..., D_INNER:D_INNER + gn].astype(jnp.float32).reshape(b, l, N_SSM_GROUPS, D_STATE)
    cm = xbc[..., D_INNER + gn:].astype(jnp.float32).reshape(b, l, N_SSM_GROUPS, D_STATE)

    def direction(xs_, bm_, cm_, dt_raw, a_log, dt_bias):
        dt = jax.nn.softplus(dt_raw.astype(jnp.float32) + dt_bias.astype(jnp.float32))
        dt = dt.reshape(b, l, N_SSM_GROUPS, HEADS_PER_GROUP)
        a_head = -jnp.exp(a_log.astype(jnp.float32)).reshape(N_SSM_GROUPS, HEADS_PER_GROUP)
        return ssd_scan(xs_, dt, a_head, bm_, cm_)

    flip = lambda t: jnp.flip(t, axis=1)
    y_f = direction(xs, bm, cm, dt_f_raw, a_log_f, dt_bias_f)
    y_b = flip(direction(flip(xs), flip(bm), flip(cm), flip(dt_b_raw), a_log_b, dt_bias_b))
    d = d_skip.astype(jnp.float32).reshape(N_SSM_GROUPS, HEADS_PER_GROUP)[..., None]
    y = (y_f + y_b + d * xs).reshape(b, l, D_INNER)
    gated = (y * jax.nn.silu(z.astype(jnp.float32))).reshape(b, l, N_SSM_GROUPS, D_INNER // N_SSM_GROUPS)
    gated = gated * lax.rsqrt(jnp.mean(gated * gated, axis=-1, keepdims=True) + EPS)
    return (gated.reshape(b, l, D_INNER) * norm_w.astype(jnp.float32)).astype(z.dtype)


def conv_gated_mlp(h, w_up, conv_w, conv_b, w_down):
    u = dwconv_centered(h @ w_up, conv_w, conv_b)
    gate = u[..., :D_FF]
    up = u[..., D_FF:]
    return (jax.nn.silu(gate) * up) @ w_down


def _fwd_setup_inputs(seed: int = 0) -> dict:
    key = jax.random.key(seed)
    ks = jax.random.split(key, 20)
    f32 = jnp.float32

    def gain(k, shape):
        return 1.0 + 0.02 * jax.random.normal(k, shape, f32)

    def dt_bias(k):
        dt = jnp.exp(jax.random.uniform(k, (DEPTH, N_SSM_HEADS), f32, np.log(1e-3), np.log(1e-1)))
        return dt + jnp.log(-jnp.expm1(-dt))

    return {
        "x": jax.random.normal(ks[0], (BATCH, SEQ, D_MODEL), f32),
        "norm1_w": gain(ks[1], (DEPTH, D_MODEL)),
        "w_in": jax.random.normal(ks[2], (DEPTH, D_MODEL, IN_WIDTH), f32) * D_MODEL ** -0.5,
        "ssm_conv_w": jax.random.normal(ks[3], (DEPTH, XBC_WIDTH, SSM_CONV), f32) * SSM_CONV ** -0.5,
        "ssm_conv_b": 0.02 * jax.random.normal(ks[4], (DEPTH, XBC_WIDTH), f32),
        "a_log_f": jnp.log(jax.random.uniform(ks[5], (DEPTH, N_SSM_HEADS), f32, 1.0, 16.0)),
        "a_log_b": jnp.log(jax.random.uniform(ks[6], (DEPTH, N_SSM_HEADS), f32, 1.0, 16.0)),
        "dt_bias_f": dt_bias(ks[7]),
        "dt_bias_b": dt_bias(ks[8]),
        "d_skip": gain(ks[9], (DEPTH, N_SSM_HEADS)),
        "ssm_norm_w": gain(ks[10], (DEPTH, D_INNER)),
        "w_out": jax.random.normal(ks[11], (DEPTH, MIX_WIDTH, D_MODEL), f32) * MIX_WIDTH ** -0.5,
        "norm2_w": gain(ks[12], (DEPTH, D_MODEL)),
        "w_up": jax.random.normal(ks[13], (DEPTH, D_MODEL, 2 * D_FF), f32) * D_MODEL ** -0.5,
        "ffn_conv_w": jax.random.normal(ks[14], (DEPTH, 2 * D_FF, FFN_CONV), f32) * FFN_CONV ** -0.5,
        "ffn_conv_b": 0.02 * jax.random.normal(ks[15], (DEPTH, 2 * D_FF), f32),
        "w_down": jax.random.normal(ks[16], (DEPTH, D_FF, D_MODEL), f32) * D_FF ** -0.5,
        "final_norm_w": gain(ks[17], (D_MODEL,)),
    }


def _fwd_reference(x, norm1_w, w_in, ssm_conv_w, ssm_conv_b, a_log_f, a_log_b, dt_bias_f, dt_bias_b,
              d_skip, ssm_norm_w, w_out, norm2_w, w_up, ffn_conv_w, ffn_conv_b, w_down, final_norm_w):
    b, s, _ = x.shape
    pos = jnp.arange(s, dtype=jnp.float32)
    sizes = [ATTN_WIDTH, ATTN_WIDTH, ATTN_WIDTH, D_INNER, XBC_WIDTH, N_SSM_HEADS, N_SSM_HEADS]
    splits = [int(v) for v in np.cumsum(sizes)[:-1]]
    for layer in range(DEPTH):
        h = rmsnorm(x, norm1_w[layer])
        proj = h @ w_in[layer]
        q, k, v, z, xbc, dt_f, dt_b = jnp.split(proj, splits, axis=-1)
        q = partial_rope(q.reshape(b, s, N_ATTN_HEADS, HEAD_DIM), pos)
        k = partial_rope(k.reshape(b, s, N_ATTN_HEADS, HEAD_DIM), pos)
        v = v.reshape(b, s, N_ATTN_HEADS, HEAD_DIM)
        attn = dilated_attention(q, k, v).reshape(b, s, ATTN_WIDTH)
        ssm = ssm_mixer(z, xbc, dt_f, dt_b, ssm_conv_w[layer], ssm_conv_b[layer], a_log_f[layer],
                        a_log_b[layer], dt_bias_f[layer], dt_bias_b[layer], d_skip[layer], ssm_norm_w[layer])
        x = x + jnp.concatenate([attn, ssm], axis=-1) @ w_out[layer]
        h = rmsnorm(x, norm2_w[layer])
        x = x + conv_gated_mlp(h, w_up[layer], ffn_conv_w[layer], ffn_conv_b[layer], w_down[layer])
    return rmsnorm(x, final_norm_w)


import jax as _jax
import jax.numpy as _jnp

TWIN_FORMAT = 'train_step'
FWD_PARAMS = ['x', 'norm1_w', 'w_in', 'ssm_conv_w', 'ssm_conv_b', 'a_log_f', 'a_log_b', 'dt_bias_f', 'dt_bias_b', 'd_skip', 'ssm_norm_w', 'w_out', 'norm2_w', 'w_up', 'ffn_conv_w', 'ffn_conv_b', 'w_down', 'final_norm_w']
TWIN_WEIGHTS = ['norm1_w', 'w_in', 'ssm_conv_w', 'ssm_conv_b', 'a_log_f', 'a_log_b', 'dt_bias_f', 'dt_bias_b', 'd_skip', 'ssm_norm_w', 'w_out', 'norm2_w', 'w_up', 'ffn_conv_w', 'ffn_conv_b', 'w_down', 'final_norm_w']
TWIN_DIFF_INPUT = 'x'
TWIN_INPUTS = ['x', 'norm1_w', 'w_in', 'ssm_conv_w', 'ssm_conv_b', 'a_log_f', 'a_log_b', 'dt_bias_f', 'dt_bias_b', 'd_skip', 'ssm_norm_w', 'w_out', 'norm2_w', 'w_up', 'ffn_conv_w', 'ffn_conv_b', 'w_down', 'final_norm_w', 'loss_target', 'm_norm1_w', 'm_w_in', 'm_ssm_conv_w', 'm_ssm_conv_b', 'm_a_log_f', 'm_a_log_b', 'm_dt_bias_f', 'm_dt_bias_b', 'm_d_skip', 'm_ssm_norm_w', 'm_w_out', 'm_norm2_w', 'm_w_up', 'm_ffn_conv_w', 'm_ffn_conv_b', 'm_w_down', 'm_final_norm_w', 'v_norm1_w', 'v_w_in', 'v_ssm_conv_w', 'v_ssm_conv_b', 'v_a_log_f', 'v_a_log_b', 'v_dt_bias_f', 'v_dt_bias_b', 'v_d_skip', 'v_ssm_norm_w', 'v_w_out', 'v_norm2_w', 'v_w_up', 'v_ffn_conv_w', 'v_ffn_conv_b', 'v_w_down', 'v_final_norm_w']
TWIN_OUTPUTS = ['loss', 'grad_x', 'grad_norm1_w', 'grad_w_in', 'grad_ssm_conv_w', 'grad_ssm_conv_b', 'grad_a_log_f', 'grad_a_log_b', 'grad_dt_bias_f', 'grad_dt_bias_b', 'grad_d_skip', 'grad_ssm_norm_w', 'grad_w_out', 'grad_norm2_w', 'grad_w_up', 'grad_ffn_conv_w', 'grad_ffn_conv_b', 'grad_w_down', 'grad_final_norm_w', 'delta_norm1_w', 'delta_w_in', 'delta_ssm_conv_w', 'delta_ssm_conv_b', 'delta_a_log_f', 'delta_a_log_b', 'delta_dt_bias_f', 'delta_dt_bias_b', 'delta_d_skip', 'delta_ssm_norm_w', 'delta_w_out', 'delta_norm2_w', 'delta_w_up', 'delta_ffn_conv_w', 'delta_ffn_conv_b', 'delta_w_down', 'delta_final_norm_w', 'new_m_norm1_w', 'new_m_w_in', 'new_m_ssm_conv_w', 'new_m_ssm_conv_b', 'new_m_a_log_f', 'new_m_a_log_b', 'new_m_dt_bias_f', 'new_m_dt_bias_b', 'new_m_d_skip', 'new_m_ssm_norm_w', 'new_m_w_out', 'new_m_norm2_w', 'new_m_w_up', 'new_m_ffn_conv_w', 'new_m_ffn_conv_b', 'new_m_w_down', 'new_m_final_norm_w', 'new_v_norm1_w', 'new_v_w_in', 'new_v_ssm_conv_w', 'new_v_ssm_conv_b', 'new_v_a_log_f', 'new_v_a_log_b', 'new_v_dt_bias_f', 'new_v_dt_bias_b', 'new_v_d_skip', 'new_v_ssm_norm_w', 'new_v_w_out', 'new_v_norm2_w', 'new_v_w_up', 'new_v_ffn_conv_w', 'new_v_ffn_conv_b', 'new_v_w_down', 'new_v_final_norm_w']
TWIN_LEAF_KINDS = {'loss': 'loss', 'grad_x': 'grad_x', 'grad_norm1_w': 'grad_w', 'grad_w_in': 'grad_w', 'grad_ssm_conv_w': 'grad_w', 'grad_ssm_conv_b': 'grad_w', 'grad_a_log_f': 'grad_w', 'grad_a_log_b': 'grad_w', 'grad_dt_bias_f': 'grad_w', 'grad_dt_bias_b': 'grad_w', 'grad_d_skip': 'grad_w', 'grad_ssm_norm_w': 'grad_w', 'grad_w_out': 'grad_w', 'grad_norm2_w': 'grad_w', 'grad_w_up': 'grad_w', 'grad_ffn_conv_w': 'grad_w', 'grad_ffn_conv_b': 'grad_w', 'grad_w_down': 'grad_w', 'grad_final_norm_w': 'grad_w', 'delta_norm1_w': 'delta_w', 'delta_w_in': 'delta_w', 'delta_ssm_conv_w': 'delta_w', 'delta_ssm_conv_b': 'delta_w', 'delta_a_log_f': 'delta_w', 'delta_a_log_b': 'delta_w', 'delta_dt_bias_f': 'delta_w', 'delta_dt_bias_b': 'delta_w', 'delta_d_skip': 'delta_w', 'delta_ssm_norm_w': 'delta_w', 'delta_w_out': 'delta_w', 'delta_norm2_w': 'delta_w', 'delta_w_up': 'delta_w', 'delta_ffn_conv_w': 'delta_w', 'delta_ffn_conv_b': 'delta_w', 'delta_w_down': 'delta_w', 'delta_final_norm_w': 'delta_w', 'new_m_norm1_w': 'new_m', 'new_m_w_in': 'new_m', 'new_m_ssm_conv_w': 'new_m', 'new_m_ssm_conv_b': 'new_m', 'new_m_a_log_f': 'new_m', 'new_m_a_log_b': 'new_m', 'new_m_dt_bias_f': 'new_m', 'new_m_dt_bias_b': 'new_m', 'new_m_d_skip': 'new_m', 'new_m_ssm_norm_w': 'new_m', 'new_m_w_out': 'new_m', 'new_m_norm2_w': 'new_m', 'new_m_w_up': 'new_m', 'new_m_ffn_conv_w': 'new_m', 'new_m_ffn_conv_b': 'new_m', 'new_m_w_down': 'new_m', 'new_m_final_norm_w': 'new_m', 'new_v_norm1_w': 'new_v', 'new_v_w_in': 'new_v', 'new_v_ssm_conv_w': 'new_v', 'new_v_ssm_conv_b': 'new_v', 'new_v_a_log_f': 'new_v', 'new_v_a_log_b': 'new_v', 'new_v_dt_bias_f': 'new_v', 'new_v_dt_bias_b': 'new_v', 'new_v_d_skip': 'new_v', 'new_v_ssm_norm_w': 'new_v', 'new_v_w_out': 'new_v', 'new_v_norm2_w': 'new_v', 'new_v_w_up': 'new_v', 'new_v_ffn_conv_w': 'new_v', 'new_v_ffn_conv_b': 'new_v', 'new_v_w_down': 'new_v', 'new_v_final_norm_w': 'new_v'}


def _forward(args):
    return _fwd_reference(*[args[k] for k in FWD_PARAMS])


def _output_shape():
    def fwd():
        inp = _fwd_setup_inputs(0)
        return _fwd_reference(*[inp[k] for k in FWD_PARAMS])
    out = _jax.eval_shape(fwd)
    return out.shape, out.dtype

N_MICROBATCH = 1
ADAM_LR = 0.001
ADAM_B1 = 0.9
ADAM_B2 = 0.999
ADAM_EPS = 1e-08
ADAM_WD = 0.01
ADAM_STEP = 10
PER_EXAMPLE_BATCH_AXIS = {'x': 0, 'loss_target': 0}
SHARED_INPUTS = []
_WEIGHT_DTYPES = {'norm1_w': _jnp.float32, 'w_in': _jnp.float32, 'ssm_conv_w': _jnp.float32, 'ssm_conv_b': _jnp.float32, 'a_log_f': _jnp.float32, 'a_log_b': _jnp.float32, 'dt_bias_f': _jnp.float32, 'dt_bias_b': _jnp.float32, 'd_skip': _jnp.float32, 'ssm_norm_w': _jnp.float32, 'w_out': _jnp.float32, 'norm2_w': _jnp.float32, 'w_up': _jnp.float32, 'ffn_conv_w': _jnp.float32, 'ffn_conv_b': _jnp.float32, 'w_down': _jnp.float32, 'final_norm_w': _jnp.float32}
MOMENT_SCALE = {'norm1_w': 1.736827e-01, 'w_in': 6.850165e-02, 'ssm_conv_w': 8.632274e-02, 'ssm_conv_b': 1.397924e-01, 'a_log_f': 2.469332e-01, 'a_log_b': 1.778909e-01, 'dt_bias_f': 1.912718e-01, 'dt_bias_b': 2.159083e-01, 'd_skip': 4.505979e-01, 'ssm_norm_w': 1.179545e-01, 'w_out': 1.138475e-01, 'norm2_w': 1.203999e-01, 'w_up': 4.805085e-02, 'ffn_conv_w': 4.968851e-02, 'ffn_conv_b': 4.810365e-02, 'w_down': 7.875202e-02, 'final_norm_w': 3.203970e+01}


def _to_microbatches(a, axis):
    t = _jnp.moveaxis(a, axis, 0)
    t = t.reshape((N_MICROBATCH, t.shape[0] // N_MICROBATCH) + t.shape[1:])
    return _jnp.moveaxis(t, 1, axis + 1)


def setup_inputs(seed: int = 0) -> dict:
    inp = _fwd_setup_inputs(seed)
    key = _jax.random.fold_in(_jax.random.key(seed), 7919)
    shape, _ = _output_shape()
    out = dict(inp)
    out["loss_target"] = _jax.random.normal(_jax.random.fold_in(key, 0), shape, _jnp.float32)
    for i, name in enumerate(TWIN_WEIGHTS):
        w = inp[name].astype(_jnp.float32)
        if MOMENT_SCALE is None:
            s = _jnp.sqrt(_jnp.mean(_jnp.square(w)) + 1e-30)
        else:
            s = MOMENT_SCALE[name]
        km, kv = _jax.random.split(_jax.random.fold_in(key, i + 1))
        out[name] = w
        out["m_" + name] = s * _jax.random.normal(km, w.shape, _jnp.float32)
        out["v_" + name] = (s * s) * _jax.random.uniform(kv, w.shape, _jnp.float32, 0.5, 1.5)
    if N_MICROBATCH > 1:
        for name, axis in PER_EXAMPLE_BATCH_AXIS.items():
            out[name] = _to_microbatches(out[name], axis)
    return {'x': out['x'], 'norm1_w': out['norm1_w'], 'w_in': out['w_in'], 'ssm_conv_w': out['ssm_conv_w'], 'ssm_conv_b': out['ssm_conv_b'], 'a_log_f': out['a_log_f'], 'a_log_b': out['a_log_b'], 'dt_bias_f': out['dt_bias_f'], 'dt_bias_b': out['dt_bias_b'], 'd_skip': out['d_skip'], 'ssm_norm_w': out['ssm_norm_w'], 'w_out': out['w_out'], 'norm2_w': out['norm2_w'], 'w_up': out['w_up'], 'ffn_conv_w': out['ffn_conv_w'], 'ffn_conv_b': out['ffn_conv_b'], 'w_down': out['w_down'], 'final_norm_w': out['final_norm_w'], 'loss_target': out['loss_target'], 'm_norm1_w': out['m_norm1_w'], 'm_w_in': out['m_w_in'], 'm_ssm_conv_w': out['m_ssm_conv_w'], 'm_ssm_conv_b': out['m_ssm_conv_b'], 'm_a_log_f': out['m_a_log_f'], 'm_a_log_b': out['m_a_log_b'], 'm_dt_bias_f': out['m_dt_bias_f'], 'm_dt_bias_b': out['m_dt_bias_b'], 'm_d_skip': out['m_d_skip'], 'm_ssm_norm_w': out['m_ssm_norm_w'], 'm_w_out': out['m_w_out'], 'm_norm2_w': out['m_norm2_w'], 'm_w_up': out['m_w_up'], 'm_ffn_conv_w': out['m_ffn_conv_w'], 'm_ffn_conv_b': out['m_ffn_conv_b'], 'm_w_down': out['m_w_down'], 'm_final_norm_w': out['m_final_norm_w'], 'v_norm1_w': out['v_norm1_w'], 'v_w_in': out['v_w_in'], 'v_ssm_conv_w': out['v_ssm_conv_w'], 'v_ssm_conv_b': out['v_ssm_conv_b'], 'v_a_log_f': out['v_a_log_f'], 'v_a_log_b': out['v_a_log_b'], 'v_dt_bias_f': out['v_dt_bias_f'], 'v_dt_bias_b': out['v_dt_bias_b'], 'v_d_skip': out['v_d_skip'], 'v_ssm_norm_w': out['v_ssm_norm_w'], 'v_w_out': out['v_w_out'], 'v_norm2_w': out['v_norm2_w'], 'v_w_up': out['v_w_up'], 'v_ffn_conv_w': out['v_ffn_conv_w'], 'v_ffn_conv_b': out['v_ffn_conv_b'], 'v_w_down': out['v_w_down'], 'v_final_norm_w': out['v_final_norm_w']}


def _loss(weights, diff, rest, loss_target):
    with _jax.named_scope("forward"):
        args = {**rest, TWIN_DIFF_INPUT: diff, **{k: w.astype(_WEIGHT_DTYPES[k]) for k, w in weights.items()}}
        y = _forward(args)
    with _jax.named_scope("loss_head"):
        err = _jnp.square(y.astype(_jnp.float32) - loss_target)
        return 0.5 * _jnp.sum(_jnp.mean(err, axis=-1)) if err.ndim else 0.5 * err


def _adamw(w, g, m, v):
    m = ADAM_B1 * m + (1.0 - ADAM_B1) * g
    v = ADAM_B2 * v + (1.0 - ADAM_B2) * _jnp.square(g)
    m_hat = m / (1.0 - ADAM_B1 ** ADAM_STEP)
    v_hat = v / (1.0 - ADAM_B2 ** ADAM_STEP)
    delta = -ADAM_LR * (m_hat / (_jnp.sqrt(v_hat) + ADAM_EPS) + ADAM_WD * w)
    return delta, m, v


def reference(x, norm1_w, w_in, ssm_conv_w, ssm_conv_b, a_log_f, a_log_b, dt_bias_f, dt_bias_b, d_skip, ssm_norm_w, w_out, norm2_w, w_up, ffn_conv_w, ffn_conv_b, w_down, final_norm_w, loss_target, m_norm1_w, m_w_in, m_ssm_conv_w, m_ssm_conv_b, m_a_log_f, m_a_log_b, m_dt_bias_f, m_dt_bias_b, m_d_skip, m_ssm_norm_w, m_w_out, m_norm2_w, m_w_up, m_ffn_conv_w, m_ffn_conv_b, m_w_down, m_final_norm_w, v_norm1_w, v_w_in, v_ssm_conv_w, v_ssm_conv_b, v_a_log_f, v_a_log_b, v_dt_bias_f, v_dt_bias_b, v_d_skip, v_ssm_norm_w, v_w_out, v_norm2_w, v_w_up, v_ffn_conv_w, v_ffn_conv_b, v_w_down, v_final_norm_w):
    given = dict(x=x, norm1_w=norm1_w, w_in=w_in, ssm_conv_w=ssm_conv_w, ssm_conv_b=ssm_conv_b, a_log_f=a_log_f, a_log_b=a_log_b, dt_bias_f=dt_bias_f, dt_bias_b=dt_bias_b, d_skip=d_skip, ssm_norm_w=ssm_norm_w, w_out=w_out, norm2_w=norm2_w, w_up=w_up, ffn_conv_w=ffn_conv_w, ffn_conv_b=ffn_conv_b, w_down=w_down, final_norm_w=final_norm_w, loss_target=loss_target, m_norm1_w=m_norm1_w, m_w_in=m_w_in, m_ssm_conv_w=m_ssm_conv_w, m_ssm_conv_b=m_ssm_conv_b, m_a_log_f=m_a_log_f, m_a_log_b=m_a_log_b, m_dt_bias_f=m_dt_bias_f, m_dt_bias_b=m_dt_bias_b, m_d_skip=m_d_skip, m_ssm_norm_w=m_ssm_norm_w, m_w_out=m_w_out, m_norm2_w=m_norm2_w, m_w_up=m_w_up, m_ffn_conv_w=m_ffn_conv_w, m_ffn_conv_b=m_ffn_conv_b, m_w_down=m_w_down, m_final_norm_w=m_final_norm_w, v_norm1_w=v_norm1_w, v_w_in=v_w_in, v_ssm_conv_w=v_ssm_conv_w, v_ssm_conv_b=v_ssm_conv_b, v_a_log_f=v_a_log_f, v_a_log_b=v_a_log_b, v_dt_bias_f=v_dt_bias_f, v_dt_bias_b=v_dt_bias_b, v_d_skip=v_d_skip, v_ssm_norm_w=v_ssm_norm_w, v_w_out=v_w_out, v_norm2_w=v_norm2_w, v_w_up=v_w_up, v_ffn_conv_w=v_ffn_conv_w, v_ffn_conv_b=v_ffn_conv_b, v_w_down=v_w_down, v_final_norm_w=v_final_norm_w)
    weights = {n: given[n] for n in TWIN_WEIGHTS}
    shared = {n: given[n] for n in SHARED_INPUTS}
    per_example = {n: given[n] for n in ['x']}
    grad_fn = _jax.value_and_grad(_loss, argnums=(0, 1))

    def one_microbatch(ex, loss_target):
        ex = dict(ex)
        diff = ex.pop(TWIN_DIFF_INPUT)
        return grad_fn(weights, diff, {**shared, **ex}, loss_target)

    if N_MICROBATCH == 1:
        loss, (grad_w, grad_x) = one_microbatch(per_example, given["loss_target"])
    else:
        def body(carry, xs):
            loss_sum, grad_sum = carry
            l_k, (gw_k, gx_k) = one_microbatch(xs[0], xs[1])
            with _jax.named_scope("update"):
                return (loss_sum + l_k, _jax.tree.map(_jnp.add, grad_sum, gw_k)), gx_k

        init = (_jnp.zeros((), _jnp.float32), _jax.tree.map(_jnp.zeros_like, weights))
        (loss, grad_w), grad_x = _jax.lax.scan(body, init, (per_example, given["loss_target"]))
    with _jax.named_scope("update"):
        delta_w, new_m, new_v = {}, {}, {}
        for n in TWIN_WEIGHTS:
            delta_w[n], new_m[n], new_v[n] = _adamw(weights[n], grad_w[n], given["m_" + n], given["v_" + n])
    return (loss, grad_x, *[grad_w[n] for n in TWIN_WEIGHTS], *[delta_w[n] for n in TWIN_WEIGHTS],
            *[new_m[n] for n in TWIN_WEIGHTS], *[new_v[n] for n in TWIN_WEIGHTS])
```

```python
import numpy as np
import jax
import jax.numpy as jnp
from jax import lax
from jax.experimental import pallas as pl
from jax.experimental.pallas import tpu as pltpu

F32, BF16 = jnp.float32, jnp.bfloat16
MESH = pl.DeviceIdType.MESH
V7X_VMEM_LIMIT = 56 * 1024 * 1024

D = 1024
HD = 64
EPS = 1e-6
CHUNK = 128
D_FF = 2816
ROPE_DIM = 16
ROPE_THETA = 500000.0
PATTERN_DILATIONS = (1, 4, 16)
BAND = 64
PACK_ROWS = 4224
SMALL_ROWS = 280
ADAM_LR, ADAM_B1, ADAM_B2, ADAM_EPS, ADAM_WD, ADAM_STEP = 0.001, 0.9, 0.999, 1e-08, 0.01, 10

NN = (((1,), (0,)), ((), ()))
NT = (((1,), (1,)), ((), ()))
TN = (((0,), (0,)), ((), ()))


def _pcall(body, **kw):
    return pl.pallas_call(body, **kw)


def _cparams(sem=None):
    return pltpu.CompilerParams(dimension_semantics=sem, vmem_limit_bytes=V7X_VMEM_LIMIT)


def _dot(a, b, dims=NN):
    return lax.dot_general(a, b, dims, preferred_element_type=F32)


def _pick(n, cap):
    if n <= cap:
        return n
    best = 0
    for t in range(128, cap + 1, 128):
        if n % t == 0:
            best = t
    assert best, (n, cap)
    return best


def _iota(shape, dim):
    return lax.broadcasted_iota(jnp.int32, shape, dim)


def _parts(x, n):
    out, r = [], x
    for _ in range(n):
        h = r.astype(BF16)
        out.append(h)
        r = r - h.astype(F32)
    return out


def _sigmoid(x):
    return 1.0 / (1.0 + jnp.exp(-x))


def _silu(x):
    return x * _sigmoid(x)


def _dsilu(x):
    s = _sigmoid(x)
    return s * (1.0 + x * (1.0 - s))


def matmul(a, b, mode, name, out_dtype=F32):
    if mode == "nn":
        (m, k), (_, n) = a.shape, b.shape
    elif mode == "nt":
        (m, k), (n, _) = a.shape, b.shape
    else:
        (k, m), (_, n) = a.shape, b.shape
    tm, tn, tk = _pick(m, 512), _pick(n, 1408), _pick(k, 1024)
    nk = k // tk
    dims = {"nn": NN, "nt": NT, "tn": TN}[mode]
    a_spec = pl.BlockSpec((tk, tm), lambda i, j, kk: (kk, i)) if mode == "tn" else pl.BlockSpec((tm, tk), lambda i, j, kk: (i, kk))
    b_spec = pl.BlockSpec((tn, tk), lambda i, j, kk: (j, kk)) if mode == "nt" else pl.BlockSpec((tk, tn), lambda i, j, kk: (kk, j))

    def body(a_ref, b_ref, o_ref, acc_ref):
        kk = pl.program_id(2)

        @pl.when(kk == 0)
        def _():
            acc_ref[...] = jnp.zeros_like(acc_ref)

        acc_ref[...] += _dot(a_ref[...].astype(BF16), b_ref[...].astype(BF16), dims)

        @pl.when(kk == nk - 1)
        def _():
            o_ref[...] = acc_ref[...].astype(o_ref.dtype)

    return _pcall(
        body, name=name, grid=(m // tm, n // tn, nk), in_specs=[a_spec, b_spec],
        out_specs=pl.BlockSpec((tm, tn), lambda i, j, kk: (i, j)),
        out_shape=jax.ShapeDtypeStruct((m, n), out_dtype),
        scratch_shapes=[pltpu.VMEM((tm, tn), F32)],
        compiler_params=_cparams(("parallel", "parallel", "arbitrary")),
    )(a, b)


def ew(fn, name, rows, tm, ncol, ins, outs, accs=()):
    nrow = rows // tm
    r8 = tm // 8
    in_specs, arrays = [], []
    for ent in ins:
        arr, kind, w, off = ent[:4]
        roff = ent[4] if len(ent) > 4 else 0
        if kind == "row":
            spec = pl.BlockSpec((tm, w), lambda j, i, off=off, roff=roff: (i + roff, j + off))
        elif kind == "const":
            spec = pl.BlockSpec((arr.shape[0], w), lambda j, i, off=off: (0, j + off))
        elif kind == "prev":
            spec = pl.BlockSpec((8, w), lambda j, i, off=off: (jnp.maximum(i * r8 - 1, 0), j + off))
        else:
            spec = pl.BlockSpec((8, w), lambda j, i, off=off: (jnp.minimum((i + 1) * r8, rows // 8 - 1), j + off))
        in_specs.append(spec)
        arrays.append(arr)
    out_specs = [pl.BlockSpec((tm, w), lambda j, i: (i, j)) for (_, _, w) in outs]
    out_shape = [jax.ShapeDtypeStruct((rows, c), dt) for (c, dt, _) in outs]
    out_specs += [pl.BlockSpec((1, w), lambda j, i: (0, j)) for (_, w) in accs]
    out_shape += [jax.ShapeDtypeStruct((1, c), F32) for (c, _) in accs]
    nin, nout = len(ins), len(outs)

    def body(*refs):
        i = pl.program_id(1)
        res = fn(i, nrow, *[r[...] for r in refs[:nin]])
        if not isinstance(res, (tuple, list)):
            res = (res,)
        for r, v in zip(refs[nin:nin + nout], res[:nout]):
            r[...] = v.astype(r.dtype)
        if accs:
            acc_refs = refs[nin + nout:]

            @pl.when(i == 0)
            def _():
                for r in acc_refs:
                    r[...] = jnp.zeros_like(r)

            for r, v in zip(acc_refs, res[nout:]):
                r[...] += v

    res = _pcall(
        body, name=name, grid=(ncol, nrow), in_specs=in_specs, out_specs=out_specs, out_shape=out_shape,
        compiler_params=_cparams(("parallel", "arbitrary")),
    )(*arrays)
    return res


def _shift_down(x, prev8, i):
    first = jnp.where(i == 0, 0.0, prev8[7:8, :])
    return jnp.where(_iota(x.shape, 0) == 0, first, pltpu.roll(x, 1, 0))


def _shift_up(x, next8, i, nrow):
    last = jnp.where(i == nrow - 1, 0.0, next8[0:1, :])
    return jnp.where(_iota(x.shape, 0) == x.shape[0] - 1, last, pltpu.roll(x, x.shape[0] - 1, 0))


def _colsum(x):
    return jnp.sum(x, axis=0, keepdims=True)


def _rms_fwd(x, w):
    r = lax.rsqrt(jnp.mean(x * x, axis=-1, keepdims=True) + EPS)
    return x * r * w


def _rms_bwd(dy, x, w):
    r = lax.rsqrt(jnp.mean(x * x, axis=-1, keepdims=True) + EPS)
    xh = x * r
    dxh = dy * w
    dx = r * (dxh - xh * jnp.mean(dxh * xh, axis=-1, keepdims=True))
    return dx, _colsum(dy * xh)


def _gather(t, d):
    s, c = t.shape
    return t if d == 1 else t.reshape(s // d, d, c).transpose(1, 0, 2).reshape(s, c)


def _ungather(t, d):
    s, c = t.shape
    return t if d == 1 else t.reshape(d, s // d, c).transpose(1, 0, 2).reshape(s, c)


def _attn_window(seq_len, tq, win):
    bps = seq_len // tq
    qb = pl.program_id(1)
    seq, t = qb // bps, qb % bps
    kloc = jnp.clip(t * tq - BAND, 0, seq_len - win)
    kstart = pl.multiple_of(seq * seq_len + kloc, BAND)
    qpos = t * tq + _iota((tq, win), 0)
    kpos = kloc + _iota((tq, win), 1)
    return kstart, jnp.abs(kpos - qpos) <= BAND


def _attn_tiles(s, seq_len):
    tq = min(256, seq_len)
    return tq, min(seq_len, tq + 2 * BAND)


def attn_fwd(q, k, v, seq_len, name):
    s = q.shape[0]
    tq, win = _attn_tiles(s, seq_len)

    def body(q_ref, k_ref, v_ref, o_ref, lse_ref):
        kstart, valid = _attn_window(seq_len, tq, win)
        qv = q_ref[...]
        kw = k_ref[pl.ds(kstart, win), :]
        vw = v_ref[pl.ds(kstart, win), :]
        head0 = _iota((tq, 128), 1) < HD
        o, lse = [], []
        for h in range(2):
            qh = jnp.where(head0 if h == 0 else ~head0, qv, jnp.zeros_like(qv))
            sc = jnp.where(valid, _dot(qh, kw, NT), -1e30)
            m = jnp.max(sc, axis=1, keepdims=True)
            p = jnp.exp(sc - m)
            den = jnp.sum(p, axis=1, keepdims=True)
            o.append(_dot(p.astype(BF16), vw) / den)
            lse.append(m + jnp.log(den))
        o_ref[...] = jnp.where(head0, o[0], o[1])
        lse_ref[...] = jnp.where(head0, lse[0], lse[1])

    qspec = pl.BlockSpec((tq, 128), lambda p, i: (i, p))
    kspec = pl.BlockSpec((s, 128), lambda p, i: (0, p))
    return _pcall(
        body, name=name, grid=(D // 128, s // tq), in_specs=[qspec, kspec, kspec], out_specs=[qspec, qspec],
        out_shape=[jax.ShapeDtypeStruct((s, D), F32)] * 2, compiler_params=_cparams(("parallel", "arbitrary")),
    )(q, k, v)


def attn_bwd(q, k, v, do, o, lse, seq_len, name):
    s = q.shape[0]
    tq, win = _attn_tiles(s, seq_len)

    def body(q_ref, k_ref, v_ref, do_ref, o_ref, lse_ref, dq_ref, dk_ref, dv_ref):
        @pl.when(pl.program_id(1) == 0)
        def _():
            dk_ref[...] = jnp.zeros_like(dk_ref)
            dv_ref[...] = jnp.zeros_like(dv_ref)

        kstart, valid = _attn_window(seq_len, tq, win)
        qv = q_ref[...]
        kw = k_ref[pl.ds(kstart, win), :]
        vw = v_ref[pl.ds(kstart, win), :]
        dov = do_ref[...]
        dob = dov.astype(BF16)
        prod = dov * o_ref[...]
        lsev = lse_ref[...]
        head0 = _iota((tq, 128), 1) < HD
        dq, dk, dv = [], 0.0, 0.0
        for h in range(2):
            hm = head0 if h == 0 else ~head0
            qh = jnp.where(hm, qv, jnp.zeros_like(qv))
            doh = jnp.where(hm, dob, jnp.zeros_like(dob))
            delta = jnp.sum(jnp.where(hm, prod, 0.0), axis=1, keepdims=True)
            sc = _dot(qh, kw, NT)
            p = jnp.where(valid, jnp.exp(sc - lsev[:, HD * h:HD * h + 1]), 0.0)
            ds = (p * (_dot(doh, vw, NT) - delta)).astype(BF16)
            dq.append(_dot(ds, kw))
            dk = dk + _dot(ds, qh, TN)
            dv = dv + _dot(p.astype(BF16), doh, TN)
        dq_ref[...] = jnp.where(head0, dq[0], dq[1])
        dk_ref[pl.ds(kstart, win), :] += dk
        dv_ref[pl.ds(kstart, win), :] += dv

    qspec = pl.BlockSpec((tq, 128), lambda p, i: (i, p))
    kspec = pl.BlockSpec((s, 128), lambda p, i: (0, p))
    return _pcall(
        body, name=name, grid=(D // 128, s // tq), in_specs=[qspec, kspec, kspec, qspec, qspec, qspec],
        out_specs=[qspec, kspec, kspec], out_shape=[jax.ShapeDtypeStruct((s, D), F32)] * 3,
        compiler_params=_cparams(("parallel", "arbitrary")),
    )(q, k, v, do, o, lse)


def _rope_tables(s):
    half = ROPE_DIM // 2
    inv_freq = jnp.power(ROPE_THETA, -jnp.arange(half, dtype=F32) * 2.0 / ROPE_DIM)
    ang = jnp.arange(s, dtype=F32)[:, None] * inv_freq[None, :]
    cos, sin = jnp.cos(ang), jnp.sin(ang)
    one, zero = jnp.ones((s, HD - ROPE_DIM), F32), jnp.zeros((s, HD - ROPE_DIM), F32)
    z8 = jnp.zeros((s, half), F32)
    c = jnp.concatenate([cos, cos, one], axis=1)
    sa = jnp.concatenate([-sin, z8, zero], axis=1)
    sb = jnp.concatenate([z8, sin, zero], axis=1)
    return [jnp.tile(t, (1, 2)) for t in (c, sa, sb)]


def _rope(x, c, sa, sb):
    return x * c + pltpu.roll(x, D - 8, 1) * sa + pltpu.roll(x, 8, 1) * sb


def _rope_t(dy, c, sa, sb):
    return dy * c + pltpu.roll(dy * sa, 8, 1) + pltpu.roll(dy * sb, D - 8, 1)


def _ssd_common(x_ref, b_ref, c_ref, dt_ref, dtt_ref, a_ref, ar_ref, rev):
    ii, jj = _iota((CHUNK, CHUNK), 0), _iota((CHUNK, CHUNK), 1)
    low = jj >= ii if rev else jj <= ii
    x, dtx = x_ref[...], dt_ref[...]
    bm, cm = b_ref[...].astype(BF16), c_ref[...].astype(BF16)
    a = dtx * a_ref[...]
    arow = dtt_ref[0] * ar_ref[0]
    lowb = low.astype(BF16)
    cs = sum(_dot(lowb, p) for p in _parts(a, 3))
    csr = sum(_dot(p, lowb, NT) for p in _parts(arow, 3))
    last = 0 if rev else CHUNK - 1
    tot = cs[last:last + 1, :]
    xdt = x * dtx
    cb = _dot(cm, bm, NT)
    lmats = [jnp.exp(jnp.where(low, cs[:, HD * h:HD * h + 1] - csr[h:h + 1, :], -1e30)) for h in range(2)]
    return dict(x=x, dtx=dtx, bm=bm, cm=cm, a=a, cs=cs, tot=tot, xdt=xdt, cb=cb, lmats=lmats, low=low, last=last)


def _ssd_specs(s, rev_order):
    nck = s // CHUNK
    ci = (lambda c: nck - 1 - c) if rev_order else (lambda c: c)
    tile = lambda off, div: pl.BlockSpec((CHUNK, 128), lambda p, c: (ci(c), off + p // div))
    common = [tile(0, 1), tile(8, 2), tile(12, 2), tile(0, 1),
              pl.BlockSpec((1, 8, CHUNK), lambda p, c: (p, 0, ci(c))),
              pl.BlockSpec((1, 128), lambda p, c: (0, p)),
              pl.BlockSpec((1, 8, 128), lambda p, c: (p, 0, 0))]
    hs = pl.BlockSpec((1, 1, CHUNK, 128), lambda p, c: (p, ci(c), 0, 0))
    return nck, common, tile(0, 1), hs


def ssd_fwd(xbc, dt_exp, dtt, a_exp, a_rows, rev, name):
    s = xbc.shape[0]
    nck, common, tile, hs_spec = _ssd_specs(s, rev)

    def body(x_ref, b_ref, c_ref, dt_ref, dtt_ref, a_ref, ar_ref, y_ref, hs_ref, h_scr):
        @pl.when(pl.program_id(1) == 0)
        def _():
            h_scr[...] = jnp.zeros_like(h_scr)

        v = _ssd_common(x_ref, b_ref, c_ref, dt_ref, dtt_ref, a_ref, ar_ref, rev)
        xdtb = v["xdt"].astype(BF16)
        yd = [_dot((v["cb"] * v["lmats"][h]).astype(BF16), xdtb) for h in range(2)]
        h_in = h_scr[...]
        hs_ref[0, 0] = h_in
        y_off = _dot(v["cm"], h_in.astype(BF16)) * jnp.exp(v["cs"])
        y_ref[...] = jnp.where(_iota((CHUNK, 128), 1) < HD, yd[0], yd[1]) + y_off
        decay = jnp.exp(v["tot"] - v["cs"])
        h_scr[...] = jnp.exp(v["tot"]) * h_in + _dot(v["bm"], (v["xdt"] * decay).astype(BF16), TN)

    return _pcall(
        body, name=name, grid=(8, nck), in_specs=common, out_specs=[tile, hs_spec],
        out_shape=[jax.ShapeDtypeStruct((s, D), F32), jax.ShapeDtypeStruct((8, nck, CHUNK, 128), F32)],
        scratch_shapes=[pltpu.VMEM((CHUNK, 128), F32)], compiler_params=_cparams(("parallel", "arbitrary")),
    )(xbc, xbc, xbc, dt_exp, dtt, a_exp, a_rows)


def ssd_bwd(xbc, dt_exp, dtt, a_exp, a_rows, hs, dy, rev, name):
    s = xbc.shape[0]
    nck, common, tile, hs_spec = _ssd_specs(s, not rev)

    def body(x_ref, b_ref, c_ref, dt_ref, dtt_ref, a_ref, ar_ref, hs_ref, dy_ref,
             dx_ref, ddt_ref, db_ref, dc_ref, dal_ref, dh_scr):
        @pl.when(pl.program_id(1) == 0)
        def _():
            dh_scr[...] = jnp.zeros_like(dh_scr)
            dal_ref[...] = jnp.zeros_like(dal_ref)

        v = _ssd_common(x_ref, b_ref, c_ref, dt_ref, dtt_ref, a_ref, ar_ref, rev)
        bm, cm, cs, tot, xdt = v["bm"], v["cm"], v["cs"], v["tot"], v["xdt"]
        h_in, dh = hs_ref[0, 0], dh_scr[...]
        dyv = dy_ref[...]
        dyb = dyv.astype(BF16)
        etot, decay, ecs = jnp.exp(tot), jnp.exp(tot - cs), jnp.exp(cs)
        xdtb = xdt.astype(BF16)
        xdec = xdt * decay
        dch = (dyv * ecs).astype(BF16)
        hb, dhb = h_in.astype(BF16), dh.astype(BF16)
        y_off = _dot(cm, hb) * ecs
        dc = _dot(dch, hb, NT)
        dh_y = _dot(cm, dch, TN)
        dxdec = _dot(bm, dhb)
        db = _dot(xdec.astype(BF16), dhb, NT)
        state_term = xdec * dxdec
        dtot = _colsum(dh * h_in) * etot + _colsum(state_term)
        head0 = _iota((CHUNK, 128), 1) < HD
        ii, jj = _iota((CHUNK, CHUNK), 0), _iota((CHUNK, CHUNK), 1)
        low_t = jj <= ii if rev else jj >= ii
        not_low_t = (~low_t).astype(BF16)
        dcb, dxd, da_l = 0.0, [], []
        for h in range(2):
            dyh = jnp.where(head0 if h == 0 else ~head0, dyb, jnp.zeros_like(dyb))
            gl = _dot(dyh, xdtb, NT) * v["lmats"][h]
            dcb = dcb + gl
            dxd.append(_dot((v["cb"] * v["lmats"][h]).astype(BF16), dyb, TN))
            w = (gl * v["cb"]).astype(BF16)
            da_l.append(jnp.sum(jnp.where(low_t, _dot(not_low_t, w, NT), 0.0), axis=1, keepdims=True))
        dxd = jnp.where(head0, dxd[0], dxd[1])
        dxdt = dxdec * decay + dxd
        dcbb = dcb.astype(BF16)
        dc_ref[...] = dc + _dot(dcbb, bm)
        db_ref[...] = db + _dot(dcbb, cm, TN)
        dcs = dyv * y_off - state_term + jnp.where(_iota((CHUNK, 128), 0) == v["last"], dtot, 0.0)
        lowb = v["low"].astype(BF16)
        da = sum(_dot(lowb, p, TN) for p in _parts(dcs, 2))
        seg = ((ii < HD) == (jj < HD)).astype(BF16)
        da = sum(_dot(p, seg) for p in _parts(da, 2)) + jnp.where(head0, da_l[0], da_l[1])
        ddt_x = sum(_dot(p, seg) for p in _parts(dxdt * v["x"], 2))
        dx_ref[...] = dxdt * v["dtx"]
        ddt_ref[...] = ddt_x + da * a_ref[...]
        dal_ref[0] += _colsum(da * v["a"])
        dh_scr[...] = etot * dh + dh_y

    return _pcall(
        body, name=name, grid=(8, nck), in_specs=common + [hs_spec, tile],
        out_specs=[tile, tile, tile, tile, pl.BlockSpec((1, 8, 128), lambda p, c: (p, 0, 0))],
        out_shape=[jax.ShapeDtypeStruct((s, D), F32)] * 4 + [jax.ShapeDtypeStruct((8, 8, 128), F32)],
        scratch_shapes=[pltpu.VMEM((CHUNK, 128), F32)], compiler_params=_cparams(("parallel", "arbitrary")),
    )(xbc, xbc, xbc, dt_exp, dtt, a_exp, a_rows, hs, dy)


def _group_norm_stats(g):
    r = [lax.rsqrt(jnp.mean(g[:, 256 * k:256 * k + 256] ** 2, axis=-1, keepdims=True) + EPS) for k in range(4)]
    grp = _iota(g.shape, 1) // 256
    return jnp.where(grp == 0, r[0], jnp.where(grp == 1, r[1], jnp.where(grp == 2, r[2], r[3])))


def _group_mean(t):
    m = [jnp.mean(t[:, 256 * k:256 * k + 256], axis=-1, keepdims=True) for k in range(4)]
    grp = _iota(t.shape, 1) // 256
    return jnp.where(grp == 0, m[0], jnp.where(grp == 1, m[1], jnp.where(grp == 2, m[2], m[3])))


def _mesh_pos():
    return lax.axis_index("x"), lax.axis_index("y"), lax.axis_index("c")


def gather_weights(pack, small):
    hbm = pl.BlockSpec(memory_space=pl.ANY)

    def body(w_ref, s_ref, wg_ref, sg_ref, send_sems, recv_sems, local_sems):
        x, y, c = _mesh_pos()
        me = 2 * x + y
        chips = [(1 - x, y), (x, 1 - y), (1 - x, 1 - y)]
        local = [pltpu.make_async_copy(w_ref, wg_ref.at[me], local_sems.at[0]),
                 pltpu.make_async_copy(s_ref, sg_ref.at[me], local_sems.at[1])]
        for cp in local:
            cp.start()
        sends = []
        for j, (px, py) in enumerate(chips):
            for k, (src, dst) in enumerate(((w_ref, wg_ref), (s_ref, sg_ref))):
                sends.append(pltpu.make_async_remote_copy(
                    src_ref=src, dst_ref=dst.at[me], send_sem=send_sems.at[2 * j + k], recv_sem=recv_sems.at[2 * j + k],
                    device_id=(px, py, c), device_id_type=MESH))
        for cp in sends:
            cp.start()
        for j, (px, py) in enumerate(chips):
            for k, (src, dst) in enumerate(((w_ref, wg_ref), (s_ref, sg_ref))):
                pltpu.make_async_remote_copy(
                    src_ref=src, dst_ref=dst.at[2 * px + py], send_sem=send_sems.at[2 * j + k],
                    recv_sem=recv_sems.at[2 * j + k], device_id=(px, py, c), device_id_type=MESH).wait_recv()
        for cp in sends:
            cp.wait_send()
        for cp in local:
            cp.wait()

    return _pcall(
        body, name="gather_weights", in_specs=[hbm, hbm], out_specs=[hbm, hbm],
        out_shape=[jax.ShapeDtypeStruct((4,) + pack.shape, pack.dtype), jax.ShapeDtypeStruct((4,) + small.shape, small.dtype)],
        scratch_shapes=[pltpu.SemaphoreType.DMA((6,)), pltpu.SemaphoreType.DMA((6,)), pltpu.SemaphoreType.DMA((2,))],
    )(pack, small)


def scatter_grads(pieces, small):
    hbm = pl.BlockSpec(memory_space=pl.ANY)

    def body(p_ref, s_ref, pr_ref, sr_ref, send_sems, recv_sems, local_sems):
        x, y, c = _mesh_pos()
        me = 4 * x + 2 * y + c
        local = [pltpu.make_async_copy(p_ref.at[2 * x + y, c], pr_ref.at[0], local_sems.at[0]),
                 pltpu.make_async_copy(s_ref, sr_ref.at[me], local_sems.at[1])]
        for cp in local:
            cp.start()
        sends = []
        for m in range(1, 8):
            mx, my, mc = m >> 2, (m >> 1) & 1, m & 1
            px, py, pc = x ^ mx, y ^ my, c ^ mc
            sends.append(pltpu.make_async_remote_copy(
                src_ref=p_ref.at[2 * px + py, pc], dst_ref=pr_ref.at[m], send_sem=send_sems.at[2 * m],
                recv_sem=recv_sems.at[2 * m], device_id=(px, py, pc), device_id_type=MESH))
            sends.append(pltpu.make_async_remote_copy(
                src_ref=s_ref, dst_ref=sr_ref.at[me], send_sem=send_sems.at[2 * m + 1],
                recv_sem=recv_sems.at[2 * m + 1], device_id=(px, py, pc), device_id_type=MESH))
        for cp in sends:
            cp.start()
        for m in range(1, 8):
            mx, my, mc = m >> 2, (m >> 1) & 1, m & 1
            px, py, pc = x ^ mx, y ^ my, c ^ mc
            peer = 4 * px + 2 * py + pc
            pltpu.make_async_remote_copy(
                src_ref=p_ref.at[0, 0], dst_ref=pr_ref.at[m], send_sem=send_sems.at[2 * m],
                recv_sem=recv_sems.at[2 * m], device_id=(px, py, pc), device_id_type=MESH).wait_recv()
            pltpu.make_async_remote_copy(
                src_ref=s_ref, dst_ref=sr_ref.at[peer], send_sem=send_sems.at[2 * m + 1],
                recv_sem=recv_sems.at[2 * m + 1], device_id=(px, py, pc), device_id_type=MESH).wait_recv()
        for cp in sends:
            cp.wait_send()
        for cp in local:
            cp.wait()

    return _pcall(
        body, name="scatter_grads", in_specs=[hbm, hbm], out_specs=[hbm, hbm],
        out_shape=[jax.ShapeDtypeStruct((8,) + pieces.shape[2:], pieces.dtype),
                   jax.ShapeDtypeStruct((8,) + small.shape, small.dtype)],
        scratch_shapes=[pltpu.SemaphoreType.DMA((16,)), pltpu.SemaphoreType.DMA((16,)), pltpu.SemaphoreType.DMA((2,))],
    )(pieces, small)


def swap_halves(piece):
    hbm = pl.BlockSpec(memory_space=pl.ANY)

    def body(p_ref, o_ref, send_sem, recv_sem, local_sem):
        x, y, c = _mesh_pos()
        local = pltpu.make_async_copy(p_ref, o_ref.at[c], local_sem)
        local.start()
        send = pltpu.make_async_remote_copy(src_ref=p_ref, dst_ref=o_ref.at[c], send_sem=send_sem, recv_sem=recv_sem,
                                            device_id=(x, y, 1 - c), device_id_type=MESH)
        send.start()
        pltpu.make_async_remote_copy(src_ref=p_ref, dst_ref=o_ref.at[1 - c], send_sem=send_sem, recv_sem=recv_sem,
                                     device_id=(x, y, 1 - c), device_id_type=MESH).wait_recv()
        send.wait_send()
        local.wait()

    return _pcall(
        body, name="swap_halves", in_specs=[hbm], out_specs=hbm,
        out_shape=jax.ShapeDtypeStruct((2,) + piece.shape, piece.dtype),
        scratch_shapes=[pltpu.SemaphoreType.DMA, pltpu.SemaphoreType.DMA, pltpu.SemaphoreType.DMA],
    )(piece)


def adamw(w, g, m, v, name):
    rows, cols = w.shape
    tm = rows
    for t in (256, 352, 128, 144, 64, 32, 16, 8):
        if rows % t == 0:
            tm = t
            break

    def fn(i, nrow, wv, gv, mv, vv):
        mn = ADAM_B1 * mv + (1.0 - ADAM_B1) * gv
        vn = ADAM_B2 * vv + (1.0 - ADAM_B2) * (gv * gv)
        m_hat = mn / (1.0 - ADAM_B1 ** ADAM_STEP)
        v_hat = vn / (1.0 - ADAM_B2 ** ADAM_STEP)
        delta = -ADAM_LR * (m_hat / (jnp.sqrt(v_hat) + ADAM_EPS) + ADAM_WD * wv)
        return delta, mn, vn

    return ew(fn, name, rows, tm, 1, [(t, "row", cols, 0) for t in (w, g, m, v)], [(cols, F32, cols)] * 3)


BIG = ("w_in", "w_out", "w_up", "w_down")
BIG_ROWS = (1544, 512, 1408, 704)
SMALL = ("norm1_w", "ssm_conv_w", "ssm_conv_b", "a_log_f", "a_log_b", "dt_bias_f", "dt_bias_b", "d_skip",
         "ssm_norm_w", "norm2_w", "ffn_conv_w", "ffn_conv_b", "final_norm_w")
WEIGHTS = ("norm1_w", "w_in", "ssm_conv_w", "ssm_conv_b", "a_log_f", "a_log_b", "dt_bias_f", "dt_bias_b", "d_skip",
           "ssm_norm_w", "w_out", "norm2_w", "w_up", "ffn_conv_w", "ffn_conv_b", "w_down", "final_norm_w")
INPUTS = ("x",) + WEIGHTS + ("loss_target",) + tuple("m_" + n for n in WEIGHTS) + tuple("v_" + n for n in WEIGHTS)


def _pad_rows(t, rows):
    return jnp.pad(t, ((0, rows - t.shape[0]), (0, 0)))


def _flat_rows(parts, width, rows):
    flat = jnp.concatenate([p.reshape(-1) for p in parts])
    return jnp.pad(flat, (0, rows * width - flat.shape[0])).reshape(rows, width)


def _split_flat(flat, shapes):
    out, pos = [], 0
    flat = flat.reshape(-1)
    for shp in shapes:
        n = int(np.prod(shp))
        out.append(flat[pos:pos + n].reshape(shp))
        pos += n
    return out


def _col_shards(t, nshard):
    r, c = t.shape
    return t.reshape(r, nshard, c // nshard).transpose(1, 0, 2).reshape(nshard, -1, D)


def _row_shards(t, nshard):
    r, c = t.shape
    return t.reshape(nshard, -1, D)


def kernel(x, norm1_w, w_in, ssm_conv_w, ssm_conv_b, a_log_f, a_log_b, dt_bias_f, dt_bias_b, d_skip, ssm_norm_w, w_out, norm2_w, w_up, ffn_conv_w, ffn_conv_b, w_down, final_norm_w, loss_target, m_norm1_w, m_w_in, m_ssm_conv_w, m_ssm_conv_b, m_a_log_f, m_a_log_b, m_dt_bias_f, m_dt_bias_b, m_d_skip, m_ssm_norm_w, m_w_out, m_norm2_w, m_w_up, m_ffn_conv_w, m_ffn_conv_b, m_w_down, m_final_norm_w, v_norm1_w, v_w_in, v_ssm_conv_w, v_ssm_conv_b, v_a_log_f, v_a_log_b, v_dt_bias_f, v_dt_bias_b, v_d_skip, v_ssm_norm_w, v_w_out, v_norm2_w, v_w_up, v_ffn_conv_w, v_ffn_conv_b, v_w_down, v_final_norm_w):
    p = dict(zip(INPUTS, (x, norm1_w, w_in, ssm_conv_w, ssm_conv_b, a_log_f, a_log_b, dt_bias_f, dt_bias_b, d_skip, ssm_norm_w, w_out, norm2_w, w_up, ffn_conv_w, ffn_conv_b, w_down, final_norm_w, loss_target, m_norm1_w, m_w_in, m_ssm_conv_w, m_ssm_conv_b, m_a_log_f, m_a_log_b, m_dt_bias_f, m_dt_bias_b, m_d_skip, m_ssm_norm_w, m_w_out, m_norm2_w, m_w_up, m_ffn_conv_w, m_ffn_conv_b, m_w_down, m_final_norm_w, v_norm1_w, v_w_in, v_ssm_conv_w, v_ssm_conv_b, v_a_log_f, v_a_log_b, v_dt_bias_f, v_dt_bias_b, v_d_skip, v_ssm_norm_w, v_w_out, v_norm2_w, v_w_up, v_ffn_conv_w, v_ffn_conv_b, v_w_down, v_final_norm_w)))
    x = p["x"][0]
    tgt = p["loss_target"][0]
    s = x.shape[0]
    chip = 2 * lax.axis_index("x") + lax.axis_index("y")

    pack = _pad_rows(jnp.concatenate([p[n][0].reshape(-1, D) for n in BIG], axis=0).astype(BF16), PACK_ROWS)
    small_w = _flat_rows([p["ssm_conv_w"][0], p["ffn_conv_w"][0]], 128, 48)
    wg, sg = gather_weights(pack, small_w)
    o = np.cumsum((0,) + BIG_ROWS)
    w_in = wg[:, o[0]:o[1]].reshape(4, D, -1).transpose(1, 0, 2).reshape(D, -1)
    w_out = wg[:, o[1]:o[2]].reshape(-1, D)
    w_up = wg[:, o[2]:o[3]].reshape(4, D, -1).transpose(1, 0, 2).reshape(D, -1)
    w_down = wg[:, o[3]:o[4]].reshape(-1, D)
    n_in = w_in.shape[1]
    n_main = 6 * D
    w_main = w_in[:, :n_main]
    w_dt = jnp.pad(w_in[:, n_main:], ((0, 0), (0, 128 - (n_in - n_main))))
    sgf = sg.reshape(4, -1)
    n_sc, n_fc = p["ssm_conv_w"].shape[1], p["ffn_conv_w"].shape[1]
    ssm_cw = sgf[:, :n_sc * 3].reshape(-1, 3).T
    ffn_cw = sgf[:, n_sc * 3:(n_sc + n_fc) * 3].reshape(-1, 3).T
    ssm_cb, ffn_cb = p["ssm_conv_b"], p["ffn_conv_b"]
    n1w, n2w, snw, fnw = p["norm1_w"], p["norm2_w"], p["ssm_norm_w"], p["final_norm_w"].reshape(1, D)

    h1, = ew(lambda i, n, xv, w: _rms_fwd(xv, w), "rms1", s, 256, 1,
             [(x, "row", D, 0), (n1w, "const", D, 0)], [(D, BF16, D)])
    proj = matmul(h1, w_main, "nn", "in_proj")
    proj_dt = matmul(h1, w_dt, "nn", "in_proj_dt")
    tabs = _rope_tables(s)

    def rope_fn(i, n, qv, kv, vv, c, sa, sb):
        c, sa, sb = (jnp.tile(t, (1, D // 128)) for t in (c, sa, sb))
        return _rope(qv, c, sa, sb) * (HD ** -0.5), _rope(kv, c, sa, sb), vv

    qr, kr, vb = ew(rope_fn, "rope", s, 256, 1,
                    [(proj, "row", D, 0), (proj, "row", D, 1), (proj, "row", D, 2)] + [(t, "row", 128, 0) for t in tabs],
                    [(D, BF16, D)] * 3)
    qkv_g, o_p, lse_p = [], [], []
    for d in PATTERN_DILATIONS:
        g3 = [_gather(t, d) for t in (qr, kr, vb)]
        qkv_g.append(g3)
        od, ld = attn_fwd(*g3, s // d, "attn_fwd_d%d" % d)
        o_p.append(_ungather(od, d))
        lse_p.append(_ungather(ld, d))

    def combine_fn(i, n, o1, o2, o3, l1, l2, l3):
        m = jnp.maximum(jnp.maximum(l1, l2), l3)
        e1, e2, e3 = jnp.exp(l1 - m), jnp.exp(l2 - m), jnp.exp(l3 - m)
        den = e1 + e2 + e3
        return (e1 * o1 + e2 * o2 + e3 * o3) / den, m + jnp.log(den)

    attn, lse = ew(combine_fn, "attn_combine", s, 256, 1, [(t, "row", D, 0) for t in o_p + lse_p], [(D, F32, D)] * 2)

    def conv_silu_fn(i, n, xv, xp, xn, w, b):
        return _silu(w[0:1] * _shift_down(xv, xp, i) + w[1:2] * xv + w[2:3] * _shift_up(xv, xn, i, n) + b)

    xbc_act, = ew(conv_silu_fn, "ssm_conv", s, 512, 4,
                  [(proj, "row", 512, 8), (proj, "prev", 512, 8), (proj, "next", 512, 8),
                   (ssm_cw, "const", 512, 0), (ssm_cb, "const", 512, 0)], [(2 * D, F32, 512)])
    dt_bias = jnp.pad(jnp.concatenate([p["dt_bias_f"], p["dt_bias_b"]], axis=1), ((0, 0), (0, 96)))

    def softplus_fn(i, n, r, b):
        t = r + b
        return jnp.maximum(t, 0.0) + jnp.log(1.0 + jnp.exp(-jnp.abs(t)))

    dt, = ew(softplus_fn, "dt_softplus", s, 512, 1, [(proj_dt, "row", 128, 0), (dt_bias, "const", 128, 0)], [(128, F32, 128)])
    d_exp = jnp.repeat(p["d_skip"], HD, axis=1)
    ssd = []
    for k, (a_log, rev) in enumerate(((p["a_log_f"], False), (p["a_log_b"], True))):
        dt_k = dt[:, 16 * k:16 * k + 16]
        a_head = -jnp.exp(a_log)
        dt_exp = jnp.repeat(dt_k, HD, axis=1)
        dtt = jnp.pad(dt_k.T.reshape(8, 2, s), ((0, 0), (0, 6), (0, 0)))
        a_exp = jnp.repeat(a_head, HD, axis=1)
        a_rows = jnp.broadcast_to(jnp.pad(a_head.reshape(8, 2), ((0, 0), (0, 6)))[:, :, None], (8, 8, 128))
        y_k, hs_k = ssd_fwd(xbc_act, dt_exp, dtt, a_exp, a_rows, rev, "ssd_fwd_%d" % k)
        ssd.append(dict(dt_exp=dt_exp, dtt=dtt, a_exp=a_exp, a_rows=a_rows, y=y_k, hs=hs_k, rev=rev))

    def gate_fn(i, n, yf, yb, xs, z, dsk, w):
        g = (yf + yb + dsk * xs) * _silu(z)
        return g * _group_norm_stats(g) * w

    ssm_out, = ew(gate_fn, "ssm_gate_norm", s, 256, 1,
                  [(ssd[0]["y"], "row", D, 0), (ssd[1]["y"], "row", D, 0), (xbc_act, "row", D, 0), (proj, "row", D, 3),
                   (d_exp, "const", D, 0), (snw, "const", D, 0)], [(D, F32, D)])
    mix = jnp.concatenate([attn, ssm_out], axis=1).astype(BF16)
    mix_w = matmul(mix, w_out, "nn", "out_proj")

    def res_rms_fn(i, n, xv, mw, w):
        x1v = xv + mw
        return x1v, _rms_fwd(x1v, w)

    x1, h2 = ew(res_rms_fn, "res_rms2", s, 256, 1, [(x, "row", D, 0), (mix_w, "row", D, 0), (n2w, "const", D, 0)],
                [(D, F32, D), (D, BF16, D)])
    hw = matmul(h2, w_up, "nn", "ffn_up")
    nfb = D_FF // 256
    ffn_conv_ins = [(hw, "row", 256, 0), (hw, "prev", 256, 0), (hw, "next", 256, 0),
                    (hw, "row", 256, nfb), (hw, "prev", 256, nfb), (hw, "next", 256, nfb),
                    (ffn_cw, "const", 256, 0), (ffn_cw, "const", 256, nfb), (ffn_cb, "const", 256, 0), (ffn_cb, "const", 256, nfb)]

    def ffn_conv(i, n, g, gp, gn, u, up_, un, wg_, wu, bg, bu):
        gs = (_shift_down(g, gp, i), g, _shift_up(g, gn, i, n))
        us = (_shift_down(u, up_, i), u, _shift_up(u, un, i, n))
        gate = wg_[0:1] * gs[0] + wg_[1:2] * gs[1] + wg_[2:3] * gs[2] + bg
        upv = wu[0:1] * us[0] + wu[1:2] * us[1] + wu[2:3] * us[2] + bu
        return gate, upv, gs, us

    def glu_fn(i, n, *blocks):
        gate, upv, _, _ = ffn_conv(i, n, *blocks)
        return _silu(gate) * upv

    act, = ew(glu_fn, "ffn_conv_glu", s, 512, nfb, ffn_conv_ins, [(D_FF, BF16, 256)])
    ffn = matmul(act, w_down, "nn", "ffn_down")

    def head_fn(i, n, x1v, fv, tv, w):
        x2 = x1v + fv
        r = lax.rsqrt(jnp.mean(x2 * x2, axis=-1, keepdims=True) + EPS)
        xh = x2 * r
        diff = xh * w - tv
        loss = 0.5 * jnp.sum(jnp.mean(diff * diff, axis=-1, keepdims=True), axis=0, keepdims=True)
        dout = diff * (1.0 / D)
        dxh = dout * w
        dx2 = r * (dxh - xh * jnp.mean(dxh * xh, axis=-1, keepdims=True))
        return dx2, jnp.broadcast_to(loss, (1, 128)), _colsum(dout * xh)

    dx2, loss_acc, g_fnw = ew(head_fn, "loss_head", s, 256, 1,
                              [(x1, "row", D, 0), (ffn, "row", D, 0), (tgt, "row", D, 0), (fnw, "const", D, 0)],
                              [(D, F32, D)], [(128, 128), (D, D)])
    loss = lax.psum(loss_acc[0, 0], ("x", "y", "c"))

    g_w_down = matmul(act, dx2, "tn", "d_w_down")
    dact = matmul(dx2, w_down, "nt", "d_act")

    def glu_bwd_fn(i, n, *blocks):
        gate, upv, gs, us = ffn_conv(i, n, *blocks[:-1])
        da = blocks[-1]
        dg = da * upv * _dsilu(gate)
        du = da * _silu(gate)
        return (dg, du) + tuple(_colsum(dg * t) for t in gs) + tuple(_colsum(du * t) for t in us) + (_colsum(dg), _colsum(du))

    res = ew(glu_bwd_fn, "ffn_glu_bwd", s, 512, nfb, ffn_conv_ins + [(dact, "row", 256, 0)],
             [(D_FF, F32, 256)] * 2, [(D_FF, 256)] * 8)
    du_g, du_u = res[0], res[1]
    g_ffn_cw = jnp.concatenate([jnp.concatenate(res[2:5], axis=0), jnp.concatenate(res[5:8], axis=0)], axis=1).T
    g_ffn_cb = jnp.concatenate([res[8], res[9]], axis=1)

    def conv_t_fn(i, n, dv, dp, dn, w):
        return w[0:1] * _shift_up(dv, dn, i, n) + w[1:2] * dv + w[2:3] * _shift_down(dv, dp, i)

    def conv_t(du, cw, off, width, ncol, name):
        return ew(conv_t_fn, name, s, 512, ncol,
                  [(du, "row", width, 0), (du, "prev", width, 0), (du, "next", width, 0), (cw, "const", width, off)],
                  [(du.shape[1], F32, width)])[0]

    dhw_g = conv_t(du_g, ffn_cw, 0, 256, nfb, "ffn_conv_t_gate")
    dhw_u = conv_t(du_u, ffn_cw, nfb, 256, nfb, "ffn_conv_t_up")
    g_w_up = jnp.concatenate([matmul(h2, dhw_g, "tn", "d_w_up_gate"), matmul(h2, dhw_u, "tn", "d_w_up_up")], axis=1)
    dh2_a = matmul(dhw_g, w_up[:, :D_FF], "nt", "d_h2_gate")
    dh2_b = matmul(dhw_u, w_up[:, D_FF:], "nt", "d_h2_up")

    def res_rms_bwd_fn(i, n, dres, da, db, xin, w):
        dx, dw = _rms_bwd(da + db, xin, w)
        return dres + dx, dw

    dx1, g_n2w = ew(res_rms_bwd_fn, "res_rms2_bwd", s, 256, 1,
                    [(dx2, "row", D, 0), (dh2_a, "row", D, 0), (dh2_b, "row", D, 0), (x1, "row", D, 0), (n2w, "const", D, 0)],
                    [(D, F32, D)], [(D, D)])

    g_w_out = matmul(mix, dx1, "tn", "d_w_out")
    dmix = matmul(dx1, w_out, "nt", "d_mix")
    ii, jj = np.arange(D)[:, None] // HD, np.arange(D)[None, :] // HD
    seg = jnp.asarray(ii == jj, BF16)

    def gate_bwd_fn(i, n, dout, yf, yb, xs, z, dsk, w, segm):
        yt = yf + yb + dsk * xs
        sz = _silu(z)
        g = yt * sz
        r = _group_norm_stats(g)
        gh = g * r
        dn = dout * w
        dg = r * (dn - gh * _group_mean(dn * gh))
        dy = dg * sz
        dsk_lane = jnp.broadcast_to(_colsum(dy * xs), (8, D))
        return dy, dg * yt * _dsilu(z), _colsum(dout * gh), sum(_dot(q, segm) for q in _parts(dsk_lane, 2))[0:1]

    dy, dz, g_snw, g_dskip_l = ew(
        gate_bwd_fn, "ssm_gate_norm_bwd", s, 256, 1,
        [(dmix, "row", D, 1), (ssd[0]["y"], "row", D, 0), (ssd[1]["y"], "row", D, 0), (xbc_act, "row", D, 0),
         (proj, "row", D, 3), (d_exp, "const", D, 0), (snw, "const", D, 0), (seg, "const", D, 0)],
        [(D, F32, D)] * 2, [(D, D)] * 2)
    sb = [ssd_bwd(xbc_act, t["dt_exp"], t["dtt"], t["a_exp"], t["a_rows"], t["hs"], dy, t["rev"], "ssd_bwd_%d" % k)
          for k, t in enumerate(ssd)]

    def dxbc_act_fn(i, n, dxf, dxb, dyv, dsk, dbf, dbb, dcf, dcb_):
        db, dc = dbf + dbb, dcf + dcb_
        db = [db[:, 256 * g:256 * g + 128] + db[:, 256 * g + 128:256 * g + 256] for g in range(4)]
        dc = [dc[:, 256 * g:256 * g + 128] + dc[:, 256 * g + 128:256 * g + 256] for g in range(4)]
        return jnp.concatenate([dxf + dxb + dyv * dsk] + db + dc, axis=1)

    dxbc_act, = ew(dxbc_act_fn, "d_xbc_act", s, 256, 1,
                   [(sb[0][0], "row", D, 0), (sb[1][0], "row", D, 0), (dy, "row", D, 0), (d_exp, "const", D, 0),
                    (sb[0][2], "row", D, 0), (sb[1][2], "row", D, 0), (sb[0][3], "row", D, 0), (sb[1][3], "row", D, 0)],
                   [(2 * D, F32, 2 * D)])

    def silu_bwd_fn(i, n, xv, xp, xn, w, b, da):
        xs3 = (_shift_down(xv, xp, i), xv, _shift_up(xv, xn, i, n))
        du = da * _dsilu(w[0:1] * xs3[0] + w[1:2] * xs3[1] + w[2:3] * xs3[2] + b)
        return (du,) + tuple(_colsum(du * t) for t in xs3) + (_colsum(du),)

    res = ew(silu_bwd_fn, "ssm_conv_bwd", s, 512, 4,
             [(proj, "row", 512, 8), (proj, "prev", 512, 8), (proj, "next", 512, 8), (ssm_cw, "const", 512, 0),
              (ssm_cb, "const", 512, 0), (dxbc_act, "row", 512, 0)], [(2 * D, F32, 512)], [(2 * D, 512)] * 4)
    g_ssm_cw = jnp.concatenate(res[1:4], axis=0).T
    g_ssm_cb = res[4]
    dxbc = conv_t(res[0], ssm_cw, 0, 512, 4, "ssm_conv_t")
    ddt = jnp.pad(jnp.concatenate([sb[0][1][:, ::HD], sb[1][1][:, ::HD]], axis=1), ((0, 0), (0, 96)))

    def dt_bwd_fn(i, n, dd, r, b):
        dr = dd * _sigmoid(r + b)
        return dr, _colsum(dr)

    dproj_dt, g_dt_bias = ew(dt_bwd_fn, "dt_softplus_bwd", s, 512, 1,
                             [(ddt, "row", 128, 0), (proj_dt, "row", 128, 0), (dt_bias, "const", 128, 0)],
                             [(128, F32, 128)], [(128, 128)])
    g_a_log = [t[4][:, 0, ::HD].reshape(1, 16) for t in sb]

    do_attn = dmix[:, :D]
    dqkv = []
    for d, g3 in zip(PATTERN_DILATIONS, qkv_g):
        r3 = attn_bwd(*g3, _gather(do_attn, d), _gather(attn, d), _gather(lse, d), s // d, "attn_bwd_d%d" % d)
        dqkv.append([_ungather(t, d) for t in r3])

    def rope_bwd_fn(i, n, q1, q2, q3, k1, k2, k3, v1, v2, v3, c, sa, sb_):
        c, sa, sb_ = (jnp.tile(t, (1, D // 128)) for t in (c, sa, sb_))
        return _rope_t((q1 + q2 + q3) * (HD ** -0.5), c, sa, sb_), _rope_t(k1 + k2 + k3, c, sa, sb_), v1 + v2 + v3

    dq, dk, dv = ew(rope_bwd_fn, "rope_bwd", s, 256, 1,
                    [(dqkv[j][k], "row", D, 0) for k in range(3) for j in range(3)] + [(t, "row", 128, 0) for t in tabs],
                    [(D, F32, D)] * 3)

    dproj = jnp.concatenate([dq, dk, dv, dz, dxbc], axis=1).astype(BF16)
    g_w_in = jnp.concatenate([matmul(h1, dproj, "tn", "d_w_in"), matmul(h1, dproj_dt, "tn", "d_w_in_dt")[:, :n_in - n_main]], axis=1)
    dh1_a = matmul(dproj, w_main, "nt", "d_h1")
    dh1_b = matmul(dproj_dt, w_dt, "nt", "d_h1_dt")
    grad_x, g_n1w = ew(res_rms_bwd_fn, "rms1_bwd", s, 256, 1,
                       [(dx1, "row", D, 0), (dh1_a, "row", D, 0), (dh1_b, "row", D, 0), (x, "row", D, 0), (n1w, "const", D, 0)],
                       [(D, F32, D)], [(D, D)])

    shards = [_col_shards(g_w_in, 4), _row_shards(g_w_out, 4), _col_shards(g_w_up, 4), _row_shards(g_w_down, 4)]
    gpack = jnp.pad(jnp.concatenate(shards, axis=1), ((0, 0), (0, PACK_ROWS - o[4]), (0, 0))).astype(BF16)
    pieces = gpack.reshape(4, PACK_ROWS, 2, 512).transpose(0, 2, 1, 3)
    small_g = {"norm1_w": g_n1w, "ssm_conv_w": g_ssm_cw, "ssm_conv_b": g_ssm_cb, "a_log_f": g_a_log[0], "a_log_b": g_a_log[1],
               "dt_bias_f": g_dt_bias[:, :16], "dt_bias_b": g_dt_bias[:, 16:32], "d_skip": g_dskip_l[:, ::HD],
               "ssm_norm_w": g_snw, "norm2_w": g_n2w, "ffn_conv_w": g_ffn_cw, "ffn_conv_b": g_ffn_cb, "final_norm_w": g_fnw}
    small_shapes = [small_g[n].shape for n in SMALL]
    got_p, got_s = scatter_grads(pieces, _flat_rows([small_g[n] for n in SMALL], 128, SMALL_ROWS))

    def sum8_fn(i, n, *v):
        t = v[0].astype(F32)
        for u in v[1:]:
            t = t + u.astype(F32)
        return t

    piece, = ew(sum8_fn, "sum_pieces", PACK_ROWS, 384, 1,
                [(got_p.reshape(8 * PACK_ROWS, 512), "row", 512, 0, k * (PACK_ROWS // 384)) for k in range(8)], [(512, F32, 512)])
    small_sum, = ew(sum8_fn, "sum_small", SMALL_ROWS, SMALL_ROWS, 1,
                    [(got_s.reshape(8 * SMALL_ROWS, 128), "row", 128, 0, k) for k in range(8)], [(128, F32, 128)])
    g_shard = swap_halves(piece).transpose(1, 0, 2).reshape(PACK_ROWS, D)
    grads = {n: g_shard[o[k]:o[k + 1]].reshape(p[n].shape) for k, n in enumerate(BIG)}
    for n, g in zip(SMALL, _split_flat(small_sum, small_shapes)):
        if n in ("ssm_conv_w", "ffn_conv_w"):
            rows = p[n].shape[1]
            g = lax.dynamic_slice_in_dim(g, chip * rows, rows, axis=0)
        grads[n] = g.reshape(p[n].shape)

    delta, new_m, new_v = {}, {}, {}
    for n in BIG:
        shp = p[n].shape
        r = [t.reshape(shp[1:]) for t in (p[n], grads[n], p["m_" + n], p["v_" + n])]
        delta[n], new_m[n], new_v[n] = [t.reshape(shp) for t in adamw(*r, "adamw_" + n)]
    shapes = [p[n].shape for n in SMALL]
    total = sum(int(np.prod(sh)) for sh in shapes)
    rows = -(-total // 1024) * 8
    packs = [_flat_rows([t[n] for n in SMALL], 128, rows)
             for t in (p, grads, {n: p["m_" + n] for n in SMALL}, {n: p["v_" + n] for n in SMALL})]
    for dst, t in zip((delta, new_m, new_v), adamw(*packs, "adamw_small")):
        for n, u in zip(SMALL, _split_flat(t, shapes)):
            dst[n] = u
    return (loss, grad_x[None], *[grads[n] for n in WEIGHTS], *[delta[n] for n in WEIGHTS],
            *[new_m[n] for n in WEIGHTS], *[new_v[n] for n in WEIGHTS])
```

```python
import numpy as np
import jax
import jax.numpy as jnp
from jax import lax
from jax.experimental import pallas as pl
from jax.experimental.pallas import tpu as pltpu

F32, BF16 = jnp.float32, jnp.bfloat16
MESH = pl.DeviceIdType.MESH
V7X_VMEM_LIMIT = 56 * 1024 * 1024

D = 1024
HD = 64
EPS = 1e-6
CHUNK = 128
D_FF = 2816
ROPE_DIM = 16
ROPE_THETA = 500000.0
PATTERN_DILATIONS = (1, 4, 16)
BAND = 64
SMALL_ROWS = 280
ADAM_LR, ADAM_B1, ADAM_B2, ADAM_EPS, ADAM_WD, ADAM_STEP = 0.001, 0.9, 0.999, 1e-08, 0.01, 10

NN = (((1,), (0,)), ((), ()))
NT = (((1,), (1,)), ((), ()))
TN = (((0,), (0,)), ((), ()))


def _pcall(body, **kw):
    return pl.pallas_call(body, **kw)


def _cparams(sem=None):
    return pltpu.CompilerParams(dimension_semantics=sem, vmem_limit_bytes=V7X_VMEM_LIMIT)


def _dot(a, b, dims=NN):
    return lax.dot_general(a, b, dims, preferred_element_type=F32)


def _pick(n, cap):
    if n <= cap:
        return n
    best = 0
    for t in range(128, cap + 1, 128):
        if n % t == 0:
            best = t
    assert best, (n, cap)
    return best


def _iota(shape, dim):
    return lax.broadcasted_iota(jnp.int32, shape, dim)


def _parts(x, n):
    out, r = [], x
    for _ in range(n):
        h = r.astype(BF16)
        out.append(h)
        r = r - h.astype(F32)
    return out


def _sigmoid(x):
    return 1.0 / (1.0 + jnp.exp(-x))


def _silu(x):
    return x * _sigmoid(x)


def _dsilu(x):
    s = _sigmoid(x)
    return s * (1.0 + x * (1.0 - s))


def matmul(a, b, mode, name, out_dtype=F32):
    if mode == "nn":
        (m, k), (_, n) = a.shape, b.shape
    elif mode == "nt":
        (m, k), (n, _) = a.shape, b.shape
    else:
        (k, m), (_, n) = a.shape, b.shape
    tm, tn, tk = _pick(m, 512), _pick(n, 1408), _pick(k, 1024)
    nk = k // tk
    dims = {"nn": NN, "nt": NT, "tn": TN}[mode]
    a_spec = pl.BlockSpec((tk, tm), lambda i, j, kk: (kk, i)) if mode == "tn" else pl.BlockSpec((tm, tk), lambda i, j, kk: (i, kk))
    b_spec = pl.BlockSpec((tn, tk), lambda i, j, kk: (j, kk)) if mode == "nt" else pl.BlockSpec((tk, tn), lambda i, j, kk: (kk, j))

    def body(a_ref, b_ref, o_ref, *acc):
        part = _dot(a_ref[...].astype(BF16), b_ref[...].astype(BF16), dims)
        if nk == 1:
            o_ref[...] = part.astype(o_ref.dtype)
            return
        acc_ref, kk = acc[0], pl.program_id(2)

        @pl.when(kk == 0)
        def _():
            acc_ref[...] = part

        @pl.when((kk > 0) & (kk < nk - 1))
        def _():
            acc_ref[...] += part

        @pl.when(kk == nk - 1)
        def _():
            o_ref[...] = (acc_ref[...] + part).astype(o_ref.dtype)

    return _pcall(
        body, name=name, grid=(m // tm, n // tn, nk), in_specs=[a_spec, b_spec],
        out_specs=pl.BlockSpec((tm, tn), lambda i, j, kk: (i, j)),
        out_shape=jax.ShapeDtypeStruct((m, n), out_dtype),
        scratch_shapes=[pltpu.VMEM((tm, tn), F32)] if nk > 1 else [],
        compiler_params=_cparams(("parallel", "parallel", "arbitrary")),
    )(a, b)


def ew(fn, name, rows, tm, ncol, ins, outs, accs=()):
    nrow = rows // tm
    r8 = tm // 8
    in_specs, arrays = [], []
    for ent in ins:
        arr, kind, w, off = ent[:4]
        roff = ent[4] if len(ent) > 4 else 0
        if kind == "row":
            spec = pl.BlockSpec((tm, w), lambda j, i, off=off, roff=roff: (i + roff, j + off))
        elif kind == "const":
            spec = pl.BlockSpec((arr.shape[0], w), lambda j, i, off=off: (0, j + off))
        elif kind == "prev":
            spec = pl.BlockSpec((8, w), lambda j, i, off=off: (jnp.maximum(i * r8 - 1, 0), j + off))
        else:
            spec = pl.BlockSpec((8, w), lambda j, i, off=off: (jnp.minimum((i + 1) * r8, rows // 8 - 1), j + off))
        in_specs.append(spec)
        arrays.append(arr)
    out_specs = [pl.BlockSpec((tm, w), lambda j, i: (i, j)) for (_, _, w) in outs]
    out_shape = [jax.ShapeDtypeStruct((rows, c), dt) for (c, dt, _) in outs]
    out_specs += [pl.BlockSpec((1, w), lambda j, i: (0, j)) for (_, w) in accs]
    out_shape += [jax.ShapeDtypeStruct((1, c), F32) for (c, _) in accs]
    nin, nout = len(ins), len(outs)

    def body(*refs):
        i = pl.program_id(1)
        res = fn(i, nrow, *[r[...] for r in refs[:nin]])
        if not isinstance(res, (tuple, list)):
            res = (res,)
        for r, v in zip(refs[nin:nin + nout], res[:nout]):
            r[...] = v.astype(r.dtype)
        if accs:
            acc_refs = refs[nin + nout:]

            @pl.when(i == 0)
            def _():
                for r in acc_refs:
                    r[...] = jnp.zeros_like(r)

            for r, v in zip(acc_refs, res[nout:]):
                r[...] += v

    res = _pcall(
        body, name=name, grid=(ncol, nrow), in_specs=in_specs, out_specs=out_specs, out_shape=out_shape,
        compiler_params=_cparams(("parallel", "arbitrary")),
    )(*arrays)
    return res


def _shift_down(x, prev8, i):
    first = jnp.where(i == 0, 0.0, prev8[7:8, :])
    return jnp.where(_iota(x.shape, 0) == 0, first, pltpu.roll(x, 1, 0))


def _shift_up(x, next8, i, nrow):
    last = jnp.where(i == nrow - 1, 0.0, next8[0:1, :])
    return jnp.where(_iota(x.shape, 0) == x.shape[0] - 1, last, pltpu.roll(x, x.shape[0] - 1, 0))


def _colsum(x):
    return jnp.sum(x, axis=0, keepdims=True)


def _rms_fwd(x, w):
    r = lax.rsqrt(jnp.mean(x * x, axis=-1, keepdims=True) + EPS)
    return x * r * w


def _rms_bwd(dy, x, w):
    r = lax.rsqrt(jnp.mean(x * x, axis=-1, keepdims=True) + EPS)
    xh = x * r
    dxh = dy * w
    dx = r * (dxh - xh * jnp.mean(dxh * xh, axis=-1, keepdims=True))
    return dx, _colsum(dy * xh)


def _gather(t, d):
    s, c = t.shape
    return t if d == 1 else t.reshape(s // d, d, c).transpose(1, 0, 2).reshape(s, c)


def _ungather(t, d):
    s, c = t.shape
    return t if d == 1 else t.reshape(d, s // d, c).transpose(1, 0, 2).reshape(s, c)


def _attn_window(seq_len, tq, win):
    bps = seq_len // tq
    qb = pl.program_id(1)
    seq, t = qb // bps, qb % bps
    kloc = jnp.clip(t * tq - BAND, 0, seq_len - win)
    kstart = pl.multiple_of(seq * seq_len + kloc, BAND)
    qpos = t * tq + _iota((tq, win), 0)
    kpos = kloc + _iota((tq, win), 1)
    return kstart, jnp.abs(kpos - qpos) <= BAND


def _attn_tiles(s, seq_len):
    tq = min(256, seq_len)
    return tq, min(seq_len, tq + 2 * BAND)


def attn_fwd(q, k, v, seq_len, name):
    s = q.shape[0]
    tq, win = _attn_tiles(s, seq_len)

    def body(q_ref, k_ref, v_ref, o_ref, lse_ref):
        kstart, valid = _attn_window(seq_len, tq, win)
        qv = q_ref[...]
        kw = k_ref[pl.ds(kstart, win), :]
        vw = v_ref[pl.ds(kstart, win), :]
        head0 = _iota((tq, 128), 1) < HD
        o, lse = [], []
        for h in range(2):
            qh = jnp.where(head0 if h == 0 else ~head0, qv, jnp.zeros_like(qv))
            sc = jnp.where(valid, _dot(qh, kw, NT), -1e30)
            m = jnp.max(sc, axis=1, keepdims=True)
            p = jnp.exp(sc - m)
            den = jnp.sum(p, axis=1, keepdims=True)
            o.append(_dot(p.astype(BF16), vw) / den)
            lse.append(m + jnp.log(den))
        o_ref[...] = jnp.where(head0, o[0], o[1])
        lse_ref[...] = jnp.where(head0, lse[0], lse[1])

    qspec = pl.BlockSpec((tq, 128), lambda p, i: (i, p))
    kspec = pl.BlockSpec((s, 128), lambda p, i: (0, p))
    return _pcall(
        body, name=name, grid=(D // 128, s // tq), in_specs=[qspec, kspec, kspec], out_specs=[qspec, qspec],
        out_shape=[jax.ShapeDtypeStruct((s, D), F32)] * 2, compiler_params=_cparams(("parallel", "arbitrary")),
    )(q, k, v)


def attn_bwd(q, k, v, do, o, lse, seq_len, name):
    s = q.shape[0]
    tq, win = _attn_tiles(s, seq_len)

    def body(q_ref, k_ref, v_ref, do_ref, o_ref, lse_ref, dq_ref, dk_ref, dv_ref):
        @pl.when(pl.program_id(1) == 0)
        def _():
            dk_ref[...] = jnp.zeros_like(dk_ref)
            dv_ref[...] = jnp.zeros_like(dv_ref)

        kstart, valid = _attn_window(seq_len, tq, win)
        qv = q_ref[...]
        kw = k_ref[pl.ds(kstart, win), :]
        vw = v_ref[pl.ds(kstart, win), :]
        dov = do_ref[...]
        dob = dov.astype(BF16)
        prod = dov * o_ref[...]
        lsev = lse_ref[...]
        head0 = _iota((tq, 128), 1) < HD
        dq, dk, dv = [], 0.0, 0.0
        for h in range(2):
            hm = head0 if h == 0 else ~head0
            qh = jnp.where(hm, qv, jnp.zeros_like(qv))
            doh = jnp.where(hm, dob, jnp.zeros_like(dob))
            delta = jnp.sum(jnp.where(hm, prod, 0.0), axis=1, keepdims=True)
            sc = _dot(qh, kw, NT)
            p = jnp.where(valid, jnp.exp(sc - lsev[:, HD * h:HD * h + 1]), 0.0)
            ds = (p * (_dot(doh, vw, NT) - delta)).astype(BF16)
            dq.append(_dot(ds, kw))
            dk = dk + _dot(ds, qh, TN)
            dv = dv + _dot(p.astype(BF16), doh, TN)
        dq_ref[...] = jnp.where(head0, dq[0], dq[1])
        dk_ref[pl.ds(kstart, win), :] += dk
        dv_ref[pl.ds(kstart, win), :] += dv

    qspec = pl.BlockSpec((tq, 128), lambda p, i: (i, p))
    kspec = pl.BlockSpec((s, 128), lambda p, i: (0, p))
    return _pcall(
        body, name=name, grid=(D // 128, s // tq), in_specs=[qspec, kspec, kspec, qspec, qspec, qspec],
        out_specs=[qspec, kspec, kspec], out_shape=[jax.ShapeDtypeStruct((s, D), F32)] * 3,
        compiler_params=_cparams(("parallel", "arbitrary")),
    )(q, k, v, do, o, lse)


def _rope_tables(s):
    half = ROPE_DIM // 2
    inv_freq = jnp.power(ROPE_THETA, -jnp.arange(half, dtype=F32) * 2.0 / ROPE_DIM)
    ang = jnp.arange(s, dtype=F32)[:, None] * inv_freq[None, :]
    cos, sin = jnp.cos(ang), jnp.sin(ang)
    one, zero = jnp.ones((s, HD - ROPE_DIM), F32), jnp.zeros((s, HD - ROPE_DIM), F32)
    z8 = jnp.zeros((s, half), F32)
    c = jnp.concatenate([cos, cos, one], axis=1)
    sa = jnp.concatenate([-sin, z8, zero], axis=1)
    sb = jnp.concatenate([z8, sin, zero], axis=1)
    return [jnp.tile(t, (1, 2)) for t in (c, sa, sb)]


def _rope(x, c, sa, sb):
    return x * c + pltpu.roll(x, D - 8, 1) * sa + pltpu.roll(x, 8, 1) * sb


def _rope_t(dy, c, sa, sb):
    return dy * c + pltpu.roll(dy * sa, 8, 1) + pltpu.roll(dy * sb, D - 8, 1)


def _ssd_common(x_ref, b_ref, c_ref, dt_ref, dtt_ref, a_ref, ar_ref, rev):
    ii, jj = _iota((CHUNK, CHUNK), 0), _iota((CHUNK, CHUNK), 1)
    low = jj >= ii if rev else jj <= ii
    x, dtx = x_ref[...], dt_ref[...]
    bm, cm = b_ref[...].astype(BF16), c_ref[...].astype(BF16)
    a = dtx * a_ref[...]
    arow = dtt_ref[0] * ar_ref[0]
    lowb = low.astype(BF16)
    cs = sum(_dot(lowb, p) for p in _parts(a, 3))
    csr = sum(_dot(p, lowb, NT) for p in _parts(arow, 3))
    last = 0 if rev else CHUNK - 1
    tot = cs[last:last + 1, :]
    xdt = x * dtx
    cb = _dot(cm, bm, NT)
    lmats = [jnp.exp(jnp.where(low, cs[:, HD * h:HD * h + 1] - csr[h:h + 1, :], -1e30)) for h in range(2)]
    return dict(x=x, dtx=dtx, bm=bm, cm=cm, a=a, cs=cs, tot=tot, xdt=xdt, cb=cb, lmats=lmats, low=low, last=last)


def _ssd_specs(s, rev_order):
    nck = s // CHUNK
    ci = (lambda c: nck - 1 - c) if rev_order else (lambda c: c)
    tile = lambda off, div: pl.BlockSpec((CHUNK, 128), lambda p, c: (ci(c), off + p // div))
    common = [tile(0, 1), tile(8, 2), tile(12, 2), tile(0, 1),
              pl.BlockSpec((1, 8, CHUNK), lambda p, c: (p, 0, ci(c))),
              pl.BlockSpec((1, 128), lambda p, c: (0, p)),
              pl.BlockSpec((1, 8, 128), lambda p, c: (p, 0, 0))]
    hs = pl.BlockSpec((1, 1, CHUNK, 128), lambda p, c: (p, ci(c), 0, 0))
    return nck, common, tile(0, 1), hs


def ssd_fwd(xbc, dt_exp, dtt, a_exp, a_rows, rev, name):
    s = xbc.shape[0]
    nck, common, tile, hs_spec = _ssd_specs(s, rev)

    def body(x_ref, b_ref, c_ref, dt_ref, dtt_ref, a_ref, ar_ref, y_ref, hs_ref, h_scr):
        @pl.when(pl.program_id(1) == 0)
        def _():
            h_scr[...] = jnp.zeros_like(h_scr)

        v = _ssd_common(x_ref, b_ref, c_ref, dt_ref, dtt_ref, a_ref, ar_ref, rev)
        xdtb = v["xdt"].astype(BF16)
        yd = [_dot((v["cb"] * v["lmats"][h]).astype(BF16), xdtb) for h in range(2)]
        h_in = h_scr[...]
        hs_ref[0, 0] = h_in
        y_off = _dot(v["cm"], h_in.astype(BF16)) * jnp.exp(v["cs"])
        y_ref[...] = jnp.where(_iota((CHUNK, 128), 1) < HD, yd[0], yd[1]) + y_off
        decay = jnp.exp(v["tot"] - v["cs"])
        h_scr[...] = jnp.exp(v["tot"]) * h_in + _dot(v["bm"], (v["xdt"] * decay).astype(BF16), TN)

    return _pcall(
        body, name=name, grid=(8, nck), in_specs=common, out_specs=[tile, hs_spec],
        out_shape=[jax.ShapeDtypeStruct((s, D), F32), jax.ShapeDtypeStruct((8, nck, CHUNK, 128), F32)],
        scratch_shapes=[pltpu.VMEM((CHUNK, 128), F32)], compiler_params=_cparams(("parallel", "arbitrary")),
    )(xbc, xbc, xbc, dt_exp, dtt, a_exp, a_rows)


def ssd_bwd(xbc, dt_exp, dtt, a_exp, a_rows, hs, dy, rev, name):
    s = xbc.shape[0]
    nck, common, tile, hs_spec = _ssd_specs(s, not rev)

    def body(x_ref, b_ref, c_ref, dt_ref, dtt_ref, a_ref, ar_ref, hs_ref, dy_ref,
             dx_ref, ddt_ref, db_ref, dc_ref, dal_ref, dh_scr):
        @pl.when(pl.program_id(1) == 0)
        def _():
            dh_scr[...] = jnp.zeros_like(dh_scr)
            dal_ref[...] = jnp.zeros_like(dal_ref)

        v = _ssd_common(x_ref, b_ref, c_ref, dt_ref, dtt_ref, a_ref, ar_ref, rev)
        bm, cm, cs, tot, xdt = v["bm"], v["cm"], v["cs"], v["tot"], v["xdt"]
        h_in, dh = hs_ref[0, 0], dh_scr[...]
        dyv = dy_ref[...]
        dyb = dyv.astype(BF16)
        etot, decay, ecs = jnp.exp(tot), jnp.exp(tot - cs), jnp.exp(cs)
        xdtb = xdt.astype(BF16)
        xdec = xdt * decay
        dch = (dyv * ecs).astype(BF16)
        hb, dhb = h_in.astype(BF16), dh.astype(BF16)
        y_off = _dot(cm, hb) * ecs
        dc = _dot(dch, hb, NT)
        dh_y = _dot(cm, dch, TN)
        dxdec = _dot(bm, dhb)
        db = _dot(xdec.astype(BF16), dhb, NT)
        state_term = xdec * dxdec
        dtot = _colsum(dh * h_in) * etot + _colsum(state_term)
        head0 = _iota((CHUNK, 128), 1) < HD
        ii, jj = _iota((CHUNK, CHUNK), 0), _iota((CHUNK, CHUNK), 1)
        low_t = jj <= ii if rev else jj >= ii
        not_low_t = (~low_t).astype(BF16)
        dcb, dxd, da_l = 0.0, [], []
        for h in range(2):
            dyh = jnp.where(head0 if h == 0 else ~head0, dyb, jnp.zeros_like(dyb))
            gl = _dot(dyh, xdtb, NT) * v["lmats"][h]
            dcb = dcb + gl
            dxd.append(_dot((v["cb"] * v["lmats"][h]).astype(BF16), dyb, TN))
            w = (gl * v["cb"]).astype(BF16)
            da_l.append(jnp.sum(jnp.where(low_t, _dot(not_low_t, w, NT), 0.0), axis=1, keepdims=True))
        dxd = jnp.where(head0, dxd[0], dxd[1])
        dxdt = dxdec * decay + dxd
        dcbb = dcb.astype(BF16)
        dc_ref[...] = dc + _dot(dcbb, bm)
        db_ref[...] = db + _dot(dcbb, cm, TN)
        dcs = dyv * y_off - state_term + jnp.where(_iota((CHUNK, 128), 0) == v["last"], dtot, 0.0)
        lowb = v["low"].astype(BF16)
        da = sum(_dot(lowb, p, TN) for p in _parts(dcs, 2))
        seg = ((ii < HD) == (jj < HD)).astype(BF16)
        da = sum(_dot(p, seg) for p in _parts(da, 2)) + jnp.where(head0, da_l[0], da_l[1])
        ddt_x = sum(_dot(p, seg) for p in _parts(dxdt * v["x"], 2))
        dx_ref[...] = dxdt * v["dtx"]
        ddt_ref[...] = ddt_x + da * a_ref[...]
        dal_ref[0] += _colsum(da * v["a"])
        dh_scr[...] = etot * dh + dh_y

    return _pcall(
        body, name=name, grid=(8, nck), in_specs=common + [hs_spec, tile],
        out_specs=[tile, tile, tile, tile, pl.BlockSpec((1, 8, 128), lambda p, c: (p, 0, 0))],
        out_shape=[jax.ShapeDtypeStruct((s, D), F32)] * 4 + [jax.ShapeDtypeStruct((8, 8, 128), F32)],
        scratch_shapes=[pltpu.VMEM((CHUNK, 128), F32)], compiler_params=_cparams(("parallel", "arbitrary")),
    )(xbc, xbc, xbc, dt_exp, dtt, a_exp, a_rows, hs, dy)


def _group_norm_stats(g):
    r = [lax.rsqrt(jnp.mean(g[:, 256 * k:256 * k + 256] ** 2, axis=-1, keepdims=True) + EPS) for k in range(4)]
    grp = _iota(g.shape, 1) // 256
    return jnp.where(grp == 0, r[0], jnp.where(grp == 1, r[1], jnp.where(grp == 2, r[2], r[3])))


def _group_mean(t):
    m = [jnp.mean(t[:, 256 * k:256 * k + 256], axis=-1, keepdims=True) for k in range(4)]
    grp = _iota(t.shape, 1) // 256
    return jnp.where(grp == 0, m[0], jnp.where(grp == 1, m[1], jnp.where(grp == 2, m[2], m[3])))


def _mesh_pos():
    return lax.axis_index("x"), lax.axis_index("y"), lax.axis_index("c")


HBM = pl.BlockSpec(memory_space=pltpu.HBM)
SEM = pl.BlockSpec(memory_space=pltpu.SEMAPHORE)
EFFECT = pltpu.SideEffectType.DATAFLOW_SIDE_EFFECTING
SWAP_CHUNKS = 18


def _hbm(t):
    return pltpu.with_memory_space_constraint(t, pltpu.HBM)


def _other_chips(x, y):
    return [(1 - x, y), (x, 1 - y), (1 - x, 1 - y)]


def _peer(x, y, c, m):
    return x ^ (m >> 2), y ^ ((m >> 1) & 1), c ^ (m & 1)


def gather_start(srcs_a, srcs_b):
    srcs = [_hbm(t) for t in list(srcs_a) + list(srcs_b)]
    n, na = len(srcs), len(srcs_a)
    lands = [_hbm(lax.empty((4,) + t.shape, t.dtype)) for t in srcs]

    def body(*refs):
        src, land = refs[:n], refs[n:2 * n]
        sems = refs[2 * n:2 * n + 4]
        x, y, c = _mesh_pos()
        for j, (px, py) in enumerate(_other_chips(x, y)):
            for k in range(n):
                send, recv, idx = (sems[0], sems[1], 3 * k + j) if k < na else (sems[2], sems[3], 3 * (k - na) + j)
                pltpu.make_async_remote_copy(src_ref=src[k], dst_ref=land[k].at[2 * x + y], send_sem=send.at[idx],
                                             recv_sem=recv.at[idx], device_id=(px, py, c), device_id_type=MESH).start()

    sem_a, sem_b = pltpu.SemaphoreType.DMA((3 * na,)), pltpu.SemaphoreType.DMA((3 * (n - na),))
    res = _pcall(
        body, name="gather_start", in_specs=[HBM] * (2 * n), out_specs=[SEM] * 4 + [HBM] * (2 * n),
        out_shape=[sem_a, sem_a, sem_b, sem_b] + [pltpu.HBM(t.shape, t.dtype) for t in srcs + lands],
        input_output_aliases={i: 4 + i for i in range(2 * n)},
        compiler_params=pltpu.CompilerParams(has_side_effects=EFFECT),
    )(*srcs, *lands)
    thru_src, thru_land = res[4:4 + n], res[4 + n:]
    return ((res[0], res[1], thru_src[:na], thru_land[:na]), (res[2], res[3], thru_src[na:], thru_land[na:]))


def gather_wait(group, name, after=None):
    send, recv, srcs, lands = group
    n = len(srcs)

    def body(*refs):
        src, land, send_ref, recv_ref = refs[:n], refs[n:2 * n], refs[2 * n], refs[2 * n + 1]
        x, y, c = _mesh_pos()
        for j, (px, py) in enumerate(_other_chips(x, y)):
            for k in range(n):
                cp = pltpu.make_async_remote_copy(src_ref=src[k], dst_ref=land[k].at[2 * px + py], send_sem=send_ref.at[3 * k + j],
                                                  recv_sem=recv_ref.at[3 * k + j], device_id=(px, py, c), device_id_type=MESH)
                cp.wait_send()
                cp.wait_recv()

    extra = [] if after is None else [after]
    res = _pcall(
        body, name=name, in_specs=[HBM] * (2 * n) + [SEM, SEM] + [pl.BlockSpec(memory_space=pl.ANY)] * len(extra),
        out_specs=[HBM] * (2 * n), out_shape=[pltpu.HBM(t.shape, t.dtype) for t in list(srcs) + list(lands)],
        input_output_aliases={i: i for i in range(2 * n)}, compiler_params=pltpu.CompilerParams(has_side_effects=EFFECT),
    )(*srcs, *lands, send, recv, *extra)
    return res[:n], res[n:]


def scatter_start(pieces, smalls, name):
    srcs = [_hbm(t) for t in list(pieces) + list(smalls)]
    n, npc = len(srcs), len(pieces)
    lands = [_hbm(lax.empty((8,) + (t.shape[2:] if k < npc else t.shape), t.dtype)) for k, t in enumerate(srcs)]

    def body(*refs):
        src, land, send, recv = refs[:n], refs[n:2 * n], refs[2 * n], refs[2 * n + 1]
        token = refs[-1]
        x, y, c = _mesh_pos()
        for m in range(1, 8):
            px, py, pc = _peer(x, y, c, m)
            for k in range(n):
                s_ref = src[k].at[2 * px + py, pc] if k < npc else src[k]
                d_ref = land[k].at[m] if k < npc else land[k].at[4 * x + 2 * y + c]
                pltpu.make_async_remote_copy(src_ref=s_ref, dst_ref=d_ref, send_sem=send.at[7 * k + m - 1], recv_sem=recv.at[7 * k + m - 1],
                                             device_id=(px, py, pc), device_id_type=MESH).start()
        token[...] = jnp.zeros_like(token)

    sem = pltpu.SemaphoreType.DMA((7 * n,))
    res = _pcall(
        body, name=name, in_specs=[HBM] * (2 * n),
        out_specs=[SEM, SEM] + [HBM] * (2 * n) + [pl.BlockSpec(memory_space=pltpu.VMEM)],
        out_shape=[sem, sem] + [pltpu.HBM(t.shape, t.dtype) for t in srcs + lands] + [jax.ShapeDtypeStruct((8, 128), F32)],
        input_output_aliases={i: 2 + i for i in range(2 * n)},
        compiler_params=pltpu.CompilerParams(has_side_effects=EFFECT),
    )(*srcs, *lands)
    return (res[0], res[1], res[2:2 + n], res[2 + n:2 + 2 * n], npc), res[-1]


def scatter_wait(group, name):
    send, recv, srcs, lands, npc = group
    n = len(srcs)

    def body(*refs):
        src, land, send_ref, recv_ref = refs[:n], refs[n:2 * n], refs[2 * n], refs[2 * n + 1]
        x, y, c = _mesh_pos()
        for m in range(1, 8):
            px, py, pc = _peer(x, y, c, m)
            for k in range(n):
                s_ref = src[k].at[0, 0] if k < npc else src[k]
                d_ref = land[k].at[m] if k < npc else land[k].at[4 * px + 2 * py + pc]
                cp = pltpu.make_async_remote_copy(src_ref=s_ref, dst_ref=d_ref, send_sem=send_ref.at[7 * k + m - 1],
                                                  recv_sem=recv_ref.at[7 * k + m - 1], device_id=(px, py, pc), device_id_type=MESH)
                cp.wait_send()
                cp.wait_recv()

    res = _pcall(
        body, name=name, in_specs=[HBM] * (2 * n) + [SEM, SEM], out_specs=[HBM] * (2 * n),
        out_shape=[pltpu.HBM(t.shape, t.dtype) for t in list(srcs) + list(lands)],
        input_output_aliases={i: i for i in range(2 * n)}, compiler_params=pltpu.CompilerParams(has_side_effects=EFFECT),
    )(*srcs, *lands, send, recv)
    return res[:n], res[n:]


def swap_halves(piece):
    any_space = pl.BlockSpec(memory_space=pl.ANY)
    rows = piece.shape[0] // SWAP_CHUNKS
    assert rows * SWAP_CHUNKS == piece.shape[0] and rows % 8 == 0

    def body(p_ref, o_ref, send_sems, recv_sems, local_sem):
        x, y, c = _mesh_pos()
        local = pltpu.make_async_copy(p_ref, o_ref.at[c], local_sem)
        local.start()

        def chunk(k, slot):
            return pltpu.make_async_remote_copy(
                src_ref=p_ref.at[pl.ds(k * rows, rows)], dst_ref=o_ref.at[slot, pl.ds(k * rows, rows)], send_sem=send_sems.at[k],
                recv_sem=recv_sems.at[k], device_id=(x, y, 1 - c), device_id_type=MESH)

        for k in range(SWAP_CHUNKS):
            chunk(k, c).start()
        for k in range(SWAP_CHUNKS):
            chunk(k, 1 - c).wait_recv()
        for k in range(SWAP_CHUNKS):
            chunk(k, c).wait_send()
        local.wait()

    return _pcall(
        body, name="swap_halves", in_specs=[any_space], out_specs=any_space,
        out_shape=jax.ShapeDtypeStruct((2,) + piece.shape, piece.dtype),
        scratch_shapes=[pltpu.SemaphoreType.DMA((SWAP_CHUNKS,)), pltpu.SemaphoreType.DMA((SWAP_CHUNKS,)), pltpu.SemaphoreType.DMA]
    )(piece)


def adamw(w, g, m, v, name):
    rows, cols = w.shape
    tm = rows
    for t in (256, 352, 128, 144, 64, 32, 16, 8):
        if rows % t == 0:
            tm = t
            break

    def fn(i, nrow, wv, gv, mv, vv):
        mn = ADAM_B1 * mv + (1.0 - ADAM_B1) * gv
        vn = ADAM_B2 * vv + (1.0 - ADAM_B2) * (gv * gv)
        m_hat = mn / (1.0 - ADAM_B1 ** ADAM_STEP)
        v_hat = vn / (1.0 - ADAM_B2 ** ADAM_STEP)
        delta = -ADAM_LR * (m_hat / (jnp.sqrt(v_hat) + ADAM_EPS) + ADAM_WD * wv)
        return delta, mn, vn

    return ew(fn, name, rows, tm, 1, [(t, "row", cols, 0) for t in (w, g, m, v)], [(cols, F32, cols)] * 3)


BIG = ("w_in", "w_out", "w_up", "w_down")
REST = ("w_out", "w_up", "w_down")
REST_ROWS = (512, 1408, 704)
W_IN_ROWS, W_IN_ROWS_PADDED = 1544, 1552
SMALL = ("norm1_w", "ssm_conv_w", "ssm_conv_b", "a_log_f", "a_log_b", "dt_bias_f", "dt_bias_b", "d_skip",
         "ssm_norm_w", "norm2_w", "ffn_conv_w", "ffn_conv_b", "final_norm_w")
WEIGHTS = ("norm1_w", "w_in", "ssm_conv_w", "ssm_conv_b", "a_log_f", "a_log_b", "dt_bias_f", "dt_bias_b", "d_skip",
           "ssm_norm_w", "w_out", "norm2_w", "w_up", "ffn_conv_w", "ffn_conv_b", "w_down", "final_norm_w")
INPUTS = ("x",) + WEIGHTS + ("loss_target",) + tuple("m_" + n for n in WEIGHTS) + tuple("v_" + n for n in WEIGHTS)


def _flat_rows(parts, width, rows):
    flat = jnp.concatenate([p.reshape(-1) for p in parts])
    return jnp.pad(flat, (0, rows * width - flat.shape[0])).reshape(rows, width)


def _split_flat(flat, shapes):
    out, pos = [], 0
    flat = flat.reshape(-1)
    for shp in shapes:
        n = int(np.prod(shp))
        out.append(flat[pos:pos + n].reshape(shp))
        pos += n
    return out


def _col_shards(t, nshard):
    r, c = t.shape
    return t.reshape(r, nshard, c // nshard).transpose(1, 0, 2).reshape(nshard, -1, D)


def _row_shards(t, nshard):
    r, c = t.shape
    return t.reshape(nshard, -1, D)


def kernel(x, norm1_w, w_in, ssm_conv_w, ssm_conv_b, a_log_f, a_log_b, dt_bias_f, dt_bias_b, d_skip, ssm_norm_w, w_out, norm2_w, w_up, ffn_conv_w, ffn_conv_b, w_down, final_norm_w, loss_target, m_norm1_w, m_w_in, m_ssm_conv_w, m_ssm_conv_b, m_a_log_f, m_a_log_b, m_dt_bias_f, m_dt_bias_b, m_d_skip, m_ssm_norm_w, m_w_out, m_norm2_w, m_w_up, m_ffn_conv_w, m_ffn_conv_b, m_w_down, m_final_norm_w, v_norm1_w, v_w_in, v_ssm_conv_w, v_ssm_conv_b, v_a_log_f, v_a_log_b, v_dt_bias_f, v_dt_bias_b, v_d_skip, v_ssm_norm_w, v_w_out, v_norm2_w, v_w_up, v_ffn_conv_w, v_ffn_conv_b, v_w_down, v_final_norm_w):
    p = dict(zip(INPUTS, (x, norm1_w, w_in, ssm_conv_w, ssm_conv_b, a_log_f, a_log_b, dt_bias_f, dt_bias_b, d_skip, ssm_norm_w, w_out, norm2_w, w_up, ffn_conv_w, ffn_conv_b, w_down, final_norm_w, loss_target, m_norm1_w, m_w_in, m_ssm_conv_w, m_ssm_conv_b, m_a_log_f, m_a_log_b, m_dt_bias_f, m_dt_bias_b, m_d_skip, m_ssm_norm_w, m_w_out, m_norm2_w, m_w_up, m_ffn_conv_w, m_ffn_conv_b, m_w_down, m_final_norm_w, v_norm1_w, v_w_in, v_ssm_conv_w, v_ssm_conv_b, v_a_log_f, v_a_log_b, v_dt_bias_f, v_dt_bias_b, v_d_skip, v_ssm_norm_w, v_w_out, v_norm2_w, v_w_up, v_ffn_conv_w, v_ffn_conv_b, v_w_down, v_final_norm_w)))
    x = p["x"][0]
    tgt = p["loss_target"][0]
    s = x.shape[0]
    chip = 2 * lax.axis_index("x") + lax.axis_index("y")

    own_slot = lambda land, mine, slot: lax.dynamic_update_slice_in_dim(land, mine[None], slot, axis=0)
    src_in = p["w_in"][0].reshape(-1, D).astype(BF16)
    src_rest = jnp.concatenate([p[n][0].reshape(-1, D) for n in REST], axis=0).astype(BF16)
    small_w = _flat_rows([p["ssm_conv_w"][0], p["ffn_conv_w"][0]], 128, 48)
    gather_in, gather_rest = gather_start([src_in, small_w], [src_rest])
    (src_in, small_w), (wg_in, sg) = gather_wait(gather_in, "gather_wait_in")
    w_in = own_slot(wg_in, src_in, chip).reshape(4, D, -1).transpose(1, 0, 2).reshape(D, -1)
    sg = own_slot(sg, small_w, chip)
    o = np.cumsum((0,) + REST_ROWS)
    n_in = w_in.shape[1]
    n_main = 6 * D
    w_main = w_in[:, :n_main]
    w_dt = jnp.pad(w_in[:, n_main:], ((0, 0), (0, 128 - (n_in - n_main))))
    sgf = sg.reshape(4, -1)
    n_sc, n_fc = p["ssm_conv_w"].shape[1], p["ffn_conv_w"].shape[1]
    ssm_cw = sgf[:, :n_sc * 3].reshape(-1, 3).T
    ffn_cw = sgf[:, n_sc * 3:(n_sc + n_fc) * 3].reshape(-1, 3).T
    ssm_cb, ffn_cb = p["ssm_conv_b"], p["ffn_conv_b"]
    n1w, n2w, snw, fnw = p["norm1_w"], p["norm2_w"], p["ssm_norm_w"], p["final_norm_w"].reshape(1, D)

    h1, = ew(lambda i, n, xv, w: _rms_fwd(xv, w), "rms1", s, 256, 1,
             [(x, "row", D, 0), (n1w, "const", D, 0)], [(D, BF16, D)])
    proj = matmul(h1, w_main, "nn", "in_proj")
    proj_dt = matmul(h1, w_dt, "nn", "in_proj_dt")
    tabs = _rope_tables(s)

    def rope_fn(i, n, qv, kv, vv, c, sa, sb):
        c, sa, sb = (jnp.tile(t, (1, D // 128)) for t in (c, sa, sb))
        return _rope(qv, c, sa, sb) * (HD ** -0.5), _rope(kv, c, sa, sb), vv

    qr, kr, vb = ew(rope_fn, "rope", s, 256, 1,
                    [(proj, "row", D, 0), (proj, "row", D, 1), (proj, "row", D, 2)] + [(t, "row", 128, 0) for t in tabs],
                    [(D, BF16, D)] * 3)
    qkv_g, o_p, lse_p = [], [], []
    for d in PATTERN_DILATIONS:
        g3 = [_gather(t, d) for t in (qr, kr, vb)]
        qkv_g.append(g3)
        od, ld = attn_fwd(*g3, s // d, "attn_fwd_d%d" % d)
        o_p.append(_ungather(od, d))
        lse_p.append(_ungather(ld, d))

    def combine_fn(i, n, o1, o2, o3, l1, l2, l3):
        m = jnp.maximum(jnp.maximum(l1, l2), l3)
        e1, e2, e3 = jnp.exp(l1 - m), jnp.exp(l2 - m), jnp.exp(l3 - m)
        den = e1 + e2 + e3
        return (e1 * o1 + e2 * o2 + e3 * o3) / den, m + jnp.log(den)

    attn, lse = ew(combine_fn, "attn_combine", s, 256, 1, [(t, "row", D, 0) for t in o_p + lse_p], [(D, F32, D)] * 2)

    def conv_silu_fn(i, n, xv, xp, xn, w, b):
        return _silu(w[0:1] * _shift_down(xv, xp, i) + w[1:2] * xv + w[2:3] * _shift_up(xv, xn, i, n) + b)

    xbc_act, = ew(conv_silu_fn, "ssm_conv", s, 512, 4,
                  [(proj, "row", 512, 8), (proj, "prev", 512, 8), (proj, "next", 512, 8),
                   (ssm_cw, "const", 512, 0), (ssm_cb, "const", 512, 0)], [(2 * D, F32, 512)])
    dt_bias = jnp.pad(jnp.concatenate([p["dt_bias_f"], p["dt_bias_b"]], axis=1), ((0, 0), (0, 96)))

    def softplus_fn(i, n, r, b):
        t = r + b
        return jnp.maximum(t, 0.0) + jnp.log(1.0 + jnp.exp(-jnp.abs(t)))

    dt, = ew(softplus_fn, "dt_softplus", s, 512, 1, [(proj_dt, "row", 128, 0), (dt_bias, "const", 128, 0)], [(128, F32, 128)])
    d_exp = jnp.repeat(p["d_skip"], HD, axis=1)
    ssd = []
    for k, (a_log, rev) in enumerate(((p["a_log_f"], False), (p["a_log_b"], True))):
        dt_k = dt[:, 16 * k:16 * k + 16]
        a_head = -jnp.exp(a_log)
        dt_exp = jnp.repeat(dt_k, HD, axis=1)
        dtt = jnp.pad(dt_k.T.reshape(8, 2, s), ((0, 0), (0, 6), (0, 0)))
        a_exp = jnp.repeat(a_head, HD, axis=1)
        a_rows = jnp.broadcast_to(jnp.pad(a_head.reshape(8, 2), ((0, 0), (0, 6)))[:, :, None], (8, 8, 128))
        y_k, hs_k = ssd_fwd(xbc_act, dt_exp, dtt, a_exp, a_rows, rev, "ssd_fwd_%d" % k)
        ssd.append(dict(dt_exp=dt_exp, dtt=dtt, a_exp=a_exp, a_rows=a_rows, y=y_k, hs=hs_k, rev=rev))

    def gate_fn(i, n, yf, yb, xs, z, dsk, w):
        g = (yf + yb + dsk * xs) * _silu(z)
        return g * _group_norm_stats(g) * w

    ssm_out, = ew(gate_fn, "ssm_gate_norm", s, 256, 1,
                  [(ssd[0]["y"], "row", D, 0), (ssd[1]["y"], "row", D, 0), (xbc_act, "row", D, 0), (proj, "row", D, 3),
                   (d_exp, "const", D, 0), (snw, "const", D, 0)], [(D, F32, D)])
    mix = jnp.concatenate([attn, ssm_out], axis=1).astype(BF16)
    (src_rest,), (wg_rest,) = gather_wait(gather_rest, "gather_wait_rest", after=mix)
    wg_rest = own_slot(wg_rest, src_rest, chip)
    w_out = wg_rest[:, o[0]:o[1]].reshape(-1, D)
    w_up = wg_rest[:, o[1]:o[2]].reshape(4, D, -1).transpose(1, 0, 2).reshape(D, -1)
    w_down = wg_rest[:, o[2]:o[3]].reshape(-1, D)
    mix_w = matmul(mix, w_out, "nn", "out_proj")

    def res_rms_fn(i, n, xv, mw, w):
        x1v = xv + mw
        return x1v, _rms_fwd(x1v, w)

    x1, h2 = ew(res_rms_fn, "res_rms2", s, 256, 1, [(x, "row", D, 0), (mix_w, "row", D, 0), (n2w, "const", D, 0)],
                [(D, F32, D), (D, BF16, D)])
    hw = matmul(h2, w_up, "nn", "ffn_up")
    nfb = D_FF // 256
    ffn_conv_ins = [(hw, "row", 256, 0), (hw, "prev", 256, 0), (hw, "next", 256, 0),
                    (hw, "row", 256, nfb), (hw, "prev", 256, nfb), (hw, "next", 256, nfb),
                    (ffn_cw, "const", 256, 0), (ffn_cw, "const", 256, nfb), (ffn_cb, "const", 256, 0), (ffn_cb, "const", 256, nfb)]

    def ffn_conv(i, n, g, gp, gn, u, up_, un, wg_, wu, bg, bu):
        gs = (_shift_down(g, gp, i), g, _shift_up(g, gn, i, n))
        us = (_shift_down(u, up_, i), u, _shift_up(u, un, i, n))
        gate = wg_[0:1] * gs[0] + wg_[1:2] * gs[1] + wg_[2:3] * gs[2] + bg
        upv = wu[0:1] * us[0] + wu[1:2] * us[1] + wu[2:3] * us[2] + bu
        return gate, upv, gs, us

    def glu_fn(i, n, *blocks):
        gate, upv, _, _ = ffn_conv(i, n, *blocks)
        return _silu(gate) * upv

    act, = ew(glu_fn, "ffn_conv_glu", s, 512, nfb, ffn_conv_ins, [(D_FF, BF16, 256)])
    ffn = matmul(act, w_down, "nn", "ffn_down")

    def head_fn(i, n, x1v, fv, tv, w):
        x2 = x1v + fv
        r = lax.rsqrt(jnp.mean(x2 * x2, axis=-1, keepdims=True) + EPS)
        xh = x2 * r
        diff = xh * w - tv
        loss = 0.5 * jnp.sum(jnp.mean(diff * diff, axis=-1, keepdims=True), axis=0, keepdims=True)
        dout = diff * (1.0 / D)
        dxh = dout * w
        dx2 = r * (dxh - xh * jnp.mean(dxh * xh, axis=-1, keepdims=True))
        return dx2, jnp.broadcast_to(loss, (1, 128)), _colsum(dout * xh)

    dx2, loss_acc, g_fnw = ew(head_fn, "loss_head", s, 256, 1,
                              [(x1, "row", D, 0), (ffn, "row", D, 0), (tgt, "row", D, 0), (fnw, "const", D, 0)],
                              [(D, F32, D)], [(128, 128), (D, D)])
    loss = lax.psum(loss_acc[0, 0], ("x", "y", "c"))

    g_w_down = matmul(act, dx2, "tn", "d_w_down")
    dact = matmul(dx2, w_down, "nt", "d_act")

    def glu_bwd_fn(i, n, *blocks):
        gate, upv, gs, us = ffn_conv(i, n, *blocks[:-1])
        da = blocks[-1]
        dg = da * upv * _dsilu(gate)
        du = da * _silu(gate)
        return (dg, du) + tuple(_colsum(dg * t) for t in gs) + tuple(_colsum(du * t) for t in us) + (_colsum(dg), _colsum(du))

    res = ew(glu_bwd_fn, "ffn_glu_bwd", s, 512, nfb, ffn_conv_ins + [(dact, "row", 256, 0)],
             [(D_FF, F32, 256)] * 2, [(D_FF, 256)] * 8)
    du_g, du_u = res[0], res[1]
    g_ffn_cw = jnp.concatenate([jnp.concatenate(res[2:5], axis=0), jnp.concatenate(res[5:8], axis=0)], axis=1).T
    g_ffn_cb = jnp.concatenate([res[8], res[9]], axis=1)

    def conv_t_fn(i, n, dv, dp, dn, w):
        return w[0:1] * _shift_up(dv, dn, i, n) + w[1:2] * dv + w[2:3] * _shift_down(dv, dp, i)

    def conv_t(du, cw, off, width, ncol, name):
        return ew(conv_t_fn, name, s, 512, ncol,
                  [(du, "row", width, 0), (du, "prev", width, 0), (du, "next", width, 0), (cw, "const", width, off)],
                  [(du.shape[1], F32, width)])[0]

    dhw_g = conv_t(du_g, ffn_cw, 0, 256, nfb, "ffn_conv_t_gate")
    dhw_u = conv_t(du_u, ffn_cw, nfb, 256, nfb, "ffn_conv_t_up")
    g_w_up = jnp.concatenate([matmul(h2, dhw_g, "tn", "d_w_up_gate"), matmul(h2, dhw_u, "tn", "d_w_up_up")], axis=1)
    dh2_a = matmul(dhw_g, w_up[:, :D_FF], "nt", "d_h2_gate")
    dh2_b = matmul(dhw_u, w_up[:, D_FF:], "nt", "d_h2_up")

    def res_rms_bwd_fn(i, n, dres, da, db, xin, w):
        dx, dw = _rms_bwd(da + db, xin, w)
        return dres + dx, dw

    dx1, g_n2w = ew(res_rms_bwd_fn, "res_rms2_bwd", s, 256, 1,
                    [(dx2, "row", D, 0), (dh2_a, "row", D, 0), (dh2_b, "row", D, 0), (x1, "row", D, 0), (n2w, "const", D, 0)],
                    [(D, F32, D)], [(D, D)])

    g_w_out = matmul(mix, dx1, "tn", "d_w_out")
    to_pieces = lambda t: t.astype(BF16).reshape(4, -1, 2, 512).transpose(0, 2, 1, 3)
    shards_rest = jnp.concatenate([_row_shards(g_w_out, 4), _col_shards(g_w_up, 4), _row_shards(g_w_down, 4)], axis=1)
    scatter_rest, token = scatter_start([to_pieces(shards_rest)], [], "scatter_start_rest")
    w_out_after = w_out + token[0:1, 0:1].astype(BF16)
    dmix = matmul(dx1, w_out_after, "nt", "d_mix")
    ii, jj = np.arange(D)[:, None] // HD, np.arange(D)[None, :] // HD
    seg = jnp.asarray(ii == jj, BF16)

    def gate_bwd_fn(i, n, dout, yf, yb, xs, z, dsk, w, segm):
        yt = yf + yb + dsk * xs
        sz = _silu(z)
        g = yt * sz
        r = _group_norm_stats(g)
        gh = g * r
        dn = dout * w
        dg = r * (dn - gh * _group_mean(dn * gh))
        dy = dg * sz
        dsk_lane = jnp.broadcast_to(_colsum(dy * xs), (8, D))
        return dy, dg * yt * _dsilu(z), _colsum(dout * gh), sum(_dot(q, segm) for q in _parts(dsk_lane, 2))[0:1]

    dy, dz, g_snw, g_dskip_l = ew(
        gate_bwd_fn, "ssm_gate_norm_bwd", s, 256, 1,
        [(dmix, "row", D, 1), (ssd[0]["y"], "row", D, 0), (ssd[1]["y"], "row", D, 0), (xbc_act, "row", D, 0),
         (proj, "row", D, 3), (d_exp, "const", D, 0), (snw, "const", D, 0), (seg, "const", D, 0)],
        [(D, F32, D)] * 2, [(D, D)] * 2)
    sb = [ssd_bwd(xbc_act, t["dt_exp"], t["dtt"], t["a_exp"], t["a_rows"], t["hs"], dy, t["rev"], "ssd_bwd_%d" % k)
          for k, t in enumerate(ssd)]

    def dxbc_act_fn(i, n, dxf, dxb, dyv, dsk, dbf, dbb, dcf, dcb_):
        db, dc = dbf + dbb, dcf + dcb_
        db = [db[:, 256 * g:256 * g + 128] + db[:, 256 * g + 128:256 * g + 256] for g in range(4)]
        dc = [dc[:, 256 * g:256 * g + 128] + dc[:, 256 * g + 128:256 * g + 256] for g in range(4)]
        return jnp.concatenate([dxf + dxb + dyv * dsk] + db + dc, axis=1)

    dxbc_act, = ew(dxbc_act_fn, "d_xbc_act", s, 256, 1,
                   [(sb[0][0], "row", D, 0), (sb[1][0], "row", D, 0), (dy, "row", D, 0), (d_exp, "const", D, 0),
                    (sb[0][2], "row", D, 0), (sb[1][2], "row", D, 0), (sb[0][3], "row", D, 0), (sb[1][3], "row", D, 0)],
                   [(2 * D, F32, 2 * D)])

    def silu_bwd_fn(i, n, xv, xp, xn, w, b, da):
        xs3 = (_shift_down(xv, xp, i), xv, _shift_up(xv, xn, i, n))
        du = da * _dsilu(w[0:1] * xs3[0] + w[1:2] * xs3[1] + w[2:3] * xs3[2] + b)
        return (du,) + tuple(_colsum(du * t) for t in xs3) + (_colsum(du),)

    res = ew(silu_bwd_fn, "ssm_conv_bwd", s, 512, 4,
             [(proj, "row", 512, 8), (proj, "prev", 512, 8), (proj, "next", 512, 8), (ssm_cw, "const", 512, 0),
              (ssm_cb, "const", 512, 0), (dxbc_act, "row", 512, 0)], [(2 * D, F32, 512)], [(2 * D, 512)] * 4)
    g_ssm_cw = jnp.concatenate(res[1:4], axis=0).T
    g_ssm_cb = res[4]
    dxbc = conv_t(res[0], ssm_cw, 0, 512, 4, "ssm_conv_t")
    ddt = jnp.pad(jnp.concatenate([sb[0][1][:, ::HD], sb[1][1][:, ::HD]], axis=1), ((0, 0), (0, 96)))

    def dt_bwd_fn(i, n, dd, r, b):
        dr = dd * _sigmoid(r + b)
        return dr, _colsum(dr)

    dproj_dt, g_dt_bias = ew(dt_bwd_fn, "dt_softplus_bwd", s, 512, 1,
                             [(ddt, "row", 128, 0), (proj_dt, "row", 128, 0), (dt_bias, "const", 128, 0)],
                             [(128, F32, 128)], [(128, 128)])
    g_a_log = [t[4][:, 0, ::HD].reshape(1, 16) for t in sb]

    do_attn = dmix[:, :D]
    dqkv = []
    for d, g3 in zip(PATTERN_DILATIONS, qkv_g):
        r3 = attn_bwd(*g3, _gather(do_attn, d), _gather(attn, d), _gather(lse, d), s // d, "attn_bwd_d%d" % d)
        dqkv.append([_ungather(t, d) for t in r3])

    def rope_bwd_fn(i, n, q1, q2, q3, k1, k2, k3, v1, v2, v3, c, sa, sb_):
        c, sa, sb_ = (jnp.tile(t, (1, D // 128)) for t in (c, sa, sb_))
        return _rope_t((q1 + q2 + q3) * (HD ** -0.5), c, sa, sb_), _rope_t(k1 + k2 + k3, c, sa, sb_), v1 + v2 + v3

    dq, dk, dv = ew(rope_bwd_fn, "rope_bwd", s, 256, 1,
                    [(dqkv[j][k], "row", D, 0) for k in range(3) for j in range(3)] + [(t, "row", 128, 0) for t in tabs],
                    [(D, F32, D)] * 3)

    dproj = jnp.concatenate([dq, dk, dv, dz, dxbc], axis=1).astype(BF16)
    g_w_in = jnp.concatenate([matmul(h1, dproj, "tn", "d_w_in"), matmul(h1, dproj_dt, "tn", "d_w_in_dt")[:, :n_in - n_main]], axis=1)
    shards_in = jnp.pad(_col_shards(g_w_in, 4), ((0, 0), (0, W_IN_ROWS_PADDED - W_IN_ROWS), (0, 0)))
    scatter_in, token = scatter_start([to_pieces(shards_in)], [], "scatter_start_in")
    w_main_after = w_main + token[0:1, 0:1].astype(BF16)
    dh1_a = matmul(dproj, w_main_after, "nt", "d_h1")
    dh1_b = matmul(dproj_dt, w_dt, "nt", "d_h1_dt")
    grad_x, g_n1w = ew(res_rms_bwd_fn, "rms1_bwd", s, 256, 1,
                       [(dx1, "row", D, 0), (dh1_a, "row", D, 0), (dh1_b, "row", D, 0), (x, "row", D, 0), (n1w, "const", D, 0)],
                       [(D, F32, D)], [(D, D)])

    small_g = {"norm1_w": g_n1w, "ssm_conv_w": g_ssm_cw, "ssm_conv_b": g_ssm_cb, "a_log_f": g_a_log[0], "a_log_b": g_a_log[1],
               "dt_bias_f": g_dt_bias[:, :16], "dt_bias_b": g_dt_bias[:, 16:32], "d_skip": g_dskip_l[:, ::HD],
               "ssm_norm_w": g_snw, "norm2_w": g_n2w, "ffn_conv_w": g_ffn_cw, "ffn_conv_b": g_ffn_cb, "final_norm_w": g_fnw}
    small_shapes = [small_g[n].shape for n in SMALL]
    scatter_small, _ = scatter_start([], [_flat_rows([small_g[n] for n in SMALL], 128, SMALL_ROWS)], "scatter_start_small")
    (sent_rest,), (got_rest,) = scatter_wait(scatter_rest, "scatter_wait_rest")
    (sent_in,), (got_in,) = scatter_wait(scatter_in, "scatter_wait_in")
    (sent_small,), (got_small,) = scatter_wait(scatter_small, "scatter_wait_small")
    core = lax.axis_index("c")

    def sum8_fn(i, n, *v):
        t = v[0].astype(F32)
        for u in v[1:]:
            t = t + u.astype(F32)
        return t

    def sum_pieces(sent, got, tm, ncol, name):
        rows = got.shape[1]
        mine = lax.dynamic_slice(sent, (chip, core, 0, 0), (1, 1, rows, 512)).reshape(rows, 512)
        w = 512 // ncol
        ins = [(mine, "row", w, 0)] + [(got.reshape(8 * rows, 512), "row", w, 0, k * (rows // tm)) for k in range(1, 8)]
        return ew(sum8_fn, name, rows, tm, ncol, ins, [(512, F32, w)])[0]

    rows_rest = int(o[3])
    piece = jnp.concatenate([sum_pieces(sent_rest, got_rest, rows_rest // 4, 1, "sum_pieces_rest"),
                             sum_pieces(sent_in, got_in, W_IN_ROWS_PADDED, 2, "sum_pieces_in")], axis=0)
    got_small = own_slot(got_small, sent_small, 2 * chip + core)
    small_sum, = ew(sum8_fn, "sum_small", SMALL_ROWS, SMALL_ROWS, 1,
                    [(got_small.reshape(8 * SMALL_ROWS, 128), "row", 128, 0, k) for k in range(8)], [(128, F32, 128)])
    g_shard = swap_halves(piece).transpose(1, 0, 2).reshape(-1, D)
    grads = {n: g_shard[o[k]:o[k + 1]].reshape(p[n].shape) for k, n in enumerate(REST)}
    grads["w_in"] = g_shard[rows_rest:rows_rest + W_IN_ROWS].reshape(p["w_in"].shape)
    for n, g in zip(SMALL, _split_flat(small_sum, small_shapes)):
        if n in ("ssm_conv_w", "ffn_conv_w"):
            rows = p[n].shape[1]
            g = lax.dynamic_slice_in_dim(g, chip * rows, rows, axis=0)
        grads[n] = g.reshape(p[n].shape)

    delta, new_m, new_v = {}, {}, {}
    for n in BIG:
        shp = p[n].shape
        r = [t.reshape(shp[1:]) for t in (p[n], grads[n], p["m_" + n], p["v_" + n])]
        delta[n], new_m[n], new_v[n] = [t.reshape(shp) for t in adamw(*r, "adamw_" + n)]
    shapes = [p[n].shape for n in SMALL]
    total = sum(int(np.prod(sh)) for sh in shapes)
    rows = -(-total // 1024) * 8
    packs = [_flat_rows([t[n] for n in SMALL], 128, rows)
             for t in (p, grads, {n: p["m_" + n] for n in SMALL}, {n: p["v_" + n] for n in SMALL})]
    for dst, t in zip((delta, new_m, new_v), adamw(*packs, "adamw_small")):
        for n, u in zip(SMALL, _split_flat(t, shapes)):
            dst[n] = u
    return (loss, grad_x[None], *[grads[n] for n in WEIGHTS], *[delta[n] for n in WEIGHTS],
            *[new_m[n] for n in WEIGHTS], *[new_v[n] for n in WEIGHTS])
```

```python
import numpy as np
import jax
import jax.numpy as jnp
from jax import lax
from jax.experimental import pallas as pl
from jax.experimental.pallas import tpu as pltpu

F32, BF16 = jnp.float32, jnp.bfloat16
MESH = pl.DeviceIdType.MESH
V7X_VMEM_LIMIT = 56 * 1024 * 1024

D = 1024
HD = 64
EPS = 1e-6
CHUNK = 128
D_FF = 2816
ROPE_DIM = 16
ROPE_THETA = 500000.0
PATTERN_DILATIONS = (1, 4, 16)
BAND = 64
SMALL_ROWS = 280
ADAM_LR, ADAM_B1, ADAM_B2, ADAM_EPS, ADAM_WD, ADAM_STEP = 0.001, 0.9, 0.999, 1e-08, 0.01, 10

NN = (((1,), (0,)), ((), ()))
NT = (((1,), (1,)), ((), ()))
TN = (((0,), (0,)), ((), ()))


def _pcall(body, **kw):
    return pl.pallas_call(body, **kw)


def _cparams(sem=None):
    return pltpu.CompilerParams(dimension_semantics=sem, vmem_limit_bytes=V7X_VMEM_LIMIT)


def _dot(a, b, dims=NN):
    return lax.dot_general(a, b, dims, preferred_element_type=F32)


def _pick(n, cap):
    if n <= cap:
        return n
    best = 0
    for t in range(128, cap + 1, 128):
        if n % t == 0:
            best = t
    assert best, (n, cap)
    return best


def _iota(shape, dim):
    return lax.broadcasted_iota(jnp.int32, shape, dim)


def _parts(x, n):
    out, r = [], x
    for _ in range(n):
        h = r.astype(BF16)
        out.append(h)
        r = r - h.astype(F32)
    return out


def _sigmoid(x):
    return 1.0 / (1.0 + jnp.exp(-x))


def _silu(x):
    return x * _sigmoid(x)


def _dsilu(x):
    s = _sigmoid(x)
    return s * (1.0 + x * (1.0 - s))


def matmul(a, b, mode, name, out_dtype=F32):
    if mode == "nn":
        (m, k), (_, n) = a.shape, b.shape
    elif mode == "nt":
        (m, k), (n, _) = a.shape, b.shape
    else:
        (k, m), (_, n) = a.shape, b.shape
    tm, tn, tk = _pick(m, 512), _pick(n, 1408), _pick(k, 1024)
    nk = k // tk
    dims = {"nn": NN, "nt": NT, "tn": TN}[mode]
    a_spec = pl.BlockSpec((tk, tm), lambda i, j, kk: (kk, i)) if mode == "tn" else pl.BlockSpec((tm, tk), lambda i, j, kk: (i, kk))
    b_spec = pl.BlockSpec((tn, tk), lambda i, j, kk: (j, kk)) if mode == "nt" else pl.BlockSpec((tk, tn), lambda i, j, kk: (kk, j))

    def body(a_ref, b_ref, o_ref, *acc):
        part = _dot(a_ref[...].astype(BF16), b_ref[...].astype(BF16), dims)
        if nk == 1:
            o_ref[...] = part.astype(o_ref.dtype)
            return
        acc_ref, kk = acc[0], pl.program_id(2)

        @pl.when(kk == 0)
        def _():
            acc_ref[...] = part

        @pl.when((kk > 0) & (kk < nk - 1))
        def _():
            acc_ref[...] += part

        @pl.when(kk == nk - 1)
        def _():
            o_ref[...] = (acc_ref[...] + part).astype(o_ref.dtype)

    return _pcall(
        body, name=name, grid=(m // tm, n // tn, nk), in_specs=[a_spec, b_spec],
        out_specs=pl.BlockSpec((tm, tn), lambda i, j, kk: (i, j)),
        out_shape=jax.ShapeDtypeStruct((m, n), out_dtype),
        scratch_shapes=[pltpu.VMEM((tm, tn), F32)] if nk > 1 else [],
        compiler_params=_cparams(("parallel", "parallel", "arbitrary")),
    )(a, b)


def ew(fn, name, rows, tm, ncol, ins, outs, accs=()):
    nrow = rows // tm
    r8 = tm // 8
    in_specs, arrays = [], []
    for ent in ins:
        arr, kind, w, off = ent[:4]
        roff = ent[4] if len(ent) > 4 else 0
        if kind == "row":
            spec = pl.BlockSpec((tm, w), lambda j, i, off=off, roff=roff: (i + roff, j + off))
        elif kind == "const":
            spec = pl.BlockSpec((arr.shape[0], w), lambda j, i, off=off: (0, j + off))
        elif kind == "prev":
            spec = pl.BlockSpec((8, w), lambda j, i, off=off: (jnp.maximum(i * r8 - 1, 0), j + off))
        else:
            spec = pl.BlockSpec((8, w), lambda j, i, off=off: (jnp.minimum((i + 1) * r8, rows // 8 - 1), j + off))
        in_specs.append(spec)
        arrays.append(arr)
    out_specs = [pl.BlockSpec((tm, w), lambda j, i: (i, j)) for (_, _, w) in outs]
    out_shape = [jax.ShapeDtypeStruct((rows, c), dt) for (c, dt, _) in outs]
    out_specs += [pl.BlockSpec((1, w), lambda j, i: (0, j)) for (_, w) in accs]
    out_shape += [jax.ShapeDtypeStruct((1, c), F32) for (c, _) in accs]
    nin, nout = len(ins), len(outs)

    def body(*refs):
        i = pl.program_id(1)
        res = fn(i, nrow, *[r[...] for r in refs[:nin]])
        if not isinstance(res, (tuple, list)):
            res = (res,)
        for r, v in zip(refs[nin:nin + nout], res[:nout]):
            r[...] = v.astype(r.dtype)
        if accs:
            acc_refs = refs[nin + nout:]

            @pl.when(i == 0)
            def _():
                for r in acc_refs:
                    r[...] = jnp.zeros_like(r)

            for r, v in zip(acc_refs, res[nout:]):
                r[...] += v

    res = _pcall(
        body, name=name, grid=(ncol, nrow), in_specs=in_specs, out_specs=out_specs, out_shape=out_shape,
        compiler_params=_cparams(("parallel", "arbitrary")),
    )(*arrays)
    return res


def _shift_down(x, prev8, i):
    first = jnp.where(i == 0, 0.0, prev8[7:8, :])
    return jnp.where(_iota(x.shape, 0) == 0, first, pltpu.roll(x, 1, 0))


def _shift_up(x, next8, i, nrow):
    last = jnp.where(i == nrow - 1, 0.0, next8[0:1, :])
    return jnp.where(_iota(x.shape, 0) == x.shape[0] - 1, last, pltpu.roll(x, x.shape[0] - 1, 0))


def _colsum(x):
    return jnp.sum(x, axis=0, keepdims=True)


def _rms_fwd(x, w):
    r = lax.rsqrt(jnp.mean(x * x, axis=-1, keepdims=True) + EPS)
    return x * r * w


def _rms_bwd(dy, x, w):
    r = lax.rsqrt(jnp.mean(x * x, axis=-1, keepdims=True) + EPS)
    xh = x * r
    dxh = dy * w
    dx = r * (dxh - xh * jnp.mean(dxh * xh, axis=-1, keepdims=True))
    return dx, _colsum(dy * xh)


def _rope_tables(s):
    half = ROPE_DIM // 2
    inv_freq = jnp.power(ROPE_THETA, -jnp.arange(half, dtype=F32) * 2.0 / ROPE_DIM)
    ang = jnp.arange(s, dtype=F32)[:, None] * inv_freq[None, :]
    cos, sin = jnp.cos(ang), jnp.sin(ang)
    one, zero = jnp.ones((s, HD - ROPE_DIM), F32), jnp.zeros((s, HD - ROPE_DIM), F32)
    z8 = jnp.zeros((s, half), F32)
    c = jnp.concatenate([cos, cos, one], axis=1)
    sa = jnp.concatenate([-sin, z8, zero], axis=1)
    sb = jnp.concatenate([z8, sin, zero], axis=1)
    return [jnp.tile(t, (1, 2)) for t in (c, sa, sb)]


ATTN_CHUNK = 1024


def _attn_plan(s):
    plan = []
    for d in PATTERN_DILATIONS:
        per_res = ATTN_CHUNK // d
        tq = min(256, per_res)
        plan.append((d, tq, min(s // d, tq + 2 * BAND), per_res // tq, s // d))
    return plan


def _rows(start, size, d):
    return pl.ds(start, size) if d == 1 else pl.ds(start, size, stride=d)


def _attn_sub(chunk, pat, r, b):
    d, tq, win, nblk, seq_len = pat
    t0 = chunk * (ATTN_CHUNK // d) + b * tq
    kloc = jnp.clip(t0 - BAND, 0, seq_len - win)
    qrow, krow = r + d * b * tq, r + d * kloc
    if d == 1:
        qrow, krow = b * tq, pl.multiple_of(kloc, BAND)
    valid = jnp.abs(kloc + _iota((tq, win), 1) - (t0 + _iota((tq, win), 0))) <= BAND
    return qrow, krow, valid


def _for_tiles(pat, fn):
    d, tq, win, nblk, seq_len = pat
    if d == 1:
        for b in range(nblk):
            fn(0, b)
    else:
        def step(r, carry):
            fn(r, 0)
            return carry
        lax.fori_loop(0, d, step, 0)


def _rope_pair(x, c, sa, sb):
    n = x.shape[1]
    return x * c + pltpu.roll(x, n - 8, 1) * sa + pltpu.roll(x, 8, 1) * sb


def _rope_pair_t(dy, c, sa, sb):
    n = dy.shape[1]
    return dy * c + pltpu.roll(dy * sa, 8, 1) + pltpu.roll(dy * sb, n - 8, 1)


def _attn_specs(s):
    whole = lambda off: pl.BlockSpec((s, 128), lambda p, c: (0, off + p))
    table = pl.BlockSpec((s, 128), lambda p, c: (0, 0))
    chunk = pl.BlockSpec((ATTN_CHUNK, 128), lambda p, c: (c, p))
    return whole, table, chunk


def attn_fwd_all(proj, tabs, name):
    s = proj.shape[0]
    plan = _attn_plan(s)
    whole, table, chunk_spec = _attn_specs(s)

    def body(q_ref, k_ref, v_ref, c_ref, sa_ref, sb_ref, o_ref, lse_ref, qs, ks, acc_s, m_s, l_s):
        chunk = pl.program_id(1)

        @pl.when(chunk == 0)
        def _():
            qs[...] = _rope_pair(q_ref[...], c_ref[...], sa_ref[...], sb_ref[...]) * (HD ** -0.5)
            ks[...] = _rope_pair(k_ref[...], c_ref[...], sa_ref[...], sb_ref[...])

        base = pl.multiple_of(chunk * ATTN_CHUNK, ATTN_CHUNK)
        for pi, pat in enumerate(plan):
            d, tq, win = pat[:3]
            head0 = _iota((tq, 128), 1) < HD

            def tile(r, b, pi=pi, pat=pat, d=d, tq=tq, win=win, head0=head0):
                qrow, krow, valid = _attn_sub(chunk, pat, r, b)
                qv = qs[_rows(base + qrow, tq, d), :].astype(BF16)
                kw = ks[_rows(krow, win, d), :].astype(BF16)
                vw = v_ref[_rows(krow, win, d), :].astype(BF16)
                acc, m, den = [], [], []
                for h in range(2):
                    qh = jnp.where(head0 if h == 0 else ~head0, qv, jnp.zeros_like(qv))
                    sc = jnp.where(valid, _dot(qh, kw, NT), -1e30)
                    mh = jnp.max(sc, axis=1, keepdims=True)
                    p = jnp.exp(sc - mh)
                    m.append(mh)
                    den.append(jnp.sum(p, axis=1, keepdims=True))
                    acc.append(_dot(p.astype(BF16), vw))
                acc_s[pi, _rows(qrow, tq, d), :] = jnp.where(head0, acc[0], acc[1])
                m_s[pi, _rows(qrow, tq, d), :] = jnp.where(head0, m[0], m[1])
                l_s[pi, _rows(qrow, tq, d), :] = jnp.where(head0, den[0], den[1])

            _for_tiles(pat, tile)
        m_all = jnp.maximum(jnp.maximum(m_s[0], m_s[1]), m_s[2])
        e = [jnp.exp(m_s[k] - m_all) for k in range(3)]
        den = e[0] * l_s[0] + e[1] * l_s[1] + e[2] * l_s[2]
        o_ref[...] = (e[0] * acc_s[0] + e[1] * acc_s[1] + e[2] * acc_s[2]) / den
        lse_ref[...] = m_all + jnp.log(den)

    stat = pltpu.VMEM((3, ATTN_CHUNK, 128), F32)
    return _pcall(
        body, name=name, grid=(D // 128, s // ATTN_CHUNK),
        in_specs=[whole(0), whole(8), whole(16), table, table, table], out_specs=[chunk_spec, chunk_spec],
        out_shape=[jax.ShapeDtypeStruct((s, D), F32)] * 2,
        scratch_shapes=[pltpu.VMEM((s, 128), F32), pltpu.VMEM((s, 128), F32), stat, stat, stat],
        compiler_params=_cparams(("parallel", "arbitrary")),
    )(proj, proj, proj, *tabs)


def attn_bwd_all(proj, tabs, dmix, o, lse, name):
    s = proj.shape[0]
    plan = _attn_plan(s)
    whole, table, chunk_spec = _attn_specs(s)
    nchunk = s // ATTN_CHUNK

    def body(q_ref, k_ref, v_ref, c_ref, sa_ref, sb_ref, do_ref, o_ref, lse_ref, dq_ref, dk_ref, dv_ref, qs, ks, delta_s):
        chunk = pl.program_id(1)

        @pl.when(chunk == 0)
        def _():
            qs[...] = _rope_pair(q_ref[...], c_ref[...], sa_ref[...], sb_ref[...]) * (HD ** -0.5)
            ks[...] = _rope_pair(k_ref[...], c_ref[...], sa_ref[...], sb_ref[...])
            dk_ref[...] = jnp.zeros_like(dk_ref)
            dv_ref[...] = jnp.zeros_like(dv_ref)

        base = pl.multiple_of(chunk * ATTN_CHUNK, ATTN_CHUNK)
        prod = do_ref[...] * o_ref[...]
        first = _iota(prod.shape, 1) < HD
        delta_s[...] = jnp.where(first, jnp.sum(jnp.where(first, prod, 0.0), axis=1, keepdims=True),
                                 jnp.sum(jnp.where(first, 0.0, prod), axis=1, keepdims=True))
        for pi, pat in enumerate(plan):
            d, tq, win = pat[:3]
            head0 = _iota((tq, 128), 1) < HD

            def tile(r, b, pi=pi, pat=pat, d=d, tq=tq, win=win, head0=head0):
                qrow, krow, valid = _attn_sub(chunk, pat, r, b)
                qv = qs[_rows(base + qrow, tq, d), :].astype(BF16)
                kw = ks[_rows(krow, win, d), :].astype(BF16)
                vw = v_ref[_rows(krow, win, d), :].astype(BF16)
                dob = do_ref[_rows(qrow, tq, d), :].astype(BF16)
                lsev = lse_ref[_rows(qrow, tq, d), :]
                delta = delta_s[_rows(qrow, tq, d), :]
                dq, dk, dv = [], 0.0, 0.0
                for h in range(2):
                    hm = head0 if h == 0 else ~head0
                    qh = jnp.where(hm, qv, jnp.zeros_like(qv))
                    doh = jnp.where(hm, dob, jnp.zeros_like(dob))
                    col = slice(HD * h, HD * h + 1)
                    p = jnp.where(valid, jnp.exp(_dot(qh, kw, NT) - lsev[:, col]), 0.0)
                    ds = (p * (_dot(doh, vw, NT) - delta[:, col])).astype(BF16)
                    dq.append(_dot(ds, kw))
                    dk = dk + _dot(ds, qh, TN)
                    dv = dv + _dot(p.astype(BF16), doh, TN)
                dqv = jnp.where(head0, dq[0], dq[1])
                if pi == 0:
                    dq_ref[_rows(qrow, tq, d), :] = dqv
                else:
                    dq_ref[_rows(qrow, tq, d), :] += dqv
                dk_ref[_rows(krow, win, d), :] += dk
                dv_ref[_rows(krow, win, d), :] += dv

            _for_tiles(pat, tile)
        tab = [t[pl.ds(base, ATTN_CHUNK), :] for t in (c_ref, sa_ref, sb_ref)]
        dq_ref[...] = _rope_pair_t(dq_ref[...] * (HD ** -0.5), *tab)

        @pl.when(chunk == nchunk - 1)
        def _():
            dk_ref[...] = _rope_pair_t(dk_ref[...], c_ref[...], sa_ref[...], sb_ref[...])

    return _pcall(
        body, name=name, grid=(D // 128, nchunk),
        in_specs=[whole(0), whole(8), whole(16), table, table, table, chunk_spec, chunk_spec, chunk_spec],
        out_specs=[chunk_spec, whole(0), whole(0)], out_shape=[jax.ShapeDtypeStruct((s, D), F32)] * 3,
        scratch_shapes=[pltpu.VMEM((s, 128), F32), pltpu.VMEM((s, 128), F32), pltpu.VMEM((ATTN_CHUNK, 128), F32)],
        compiler_params=_cparams(("parallel", "arbitrary")),
    )(proj, proj, proj, *tabs, dmix, o, lse)


def _ssd_common(x_ref, b_ref, c_ref, dt_ref, dtt_ref, a_ref, ar_ref, rev):
    ii, jj = _iota((CHUNK, CHUNK), 0), _iota((CHUNK, CHUNK), 1)
    low = jj >= ii if rev else jj <= ii
    x, dtx = x_ref[...], dt_ref[...]
    bm, cm = b_ref[...].astype(BF16), c_ref[...].astype(BF16)
    a = dtx * a_ref[...]
    arow = dtt_ref[0] * ar_ref[0]
    lowb = low.astype(BF16)
    cs = sum(_dot(lowb, p) for p in _parts(a, 3))
    csr = sum(_dot(p, lowb, NT) for p in _parts(arow, 3))
    last = 0 if rev else CHUNK - 1
    tot = cs[last:last + 1, :]
    xdt = x * dtx
    cb = _dot(cm, bm, NT)
    lmats = [jnp.exp(jnp.where(low, cs[:, HD * h:HD * h + 1] - csr[h:h + 1, :], -1e30)) for h in range(2)]
    return dict(x=x, dtx=dtx, bm=bm, cm=cm, a=a, cs=cs, tot=tot, xdt=xdt, cb=cb, lmats=lmats, low=low, last=last)


def _ssd_specs(s, rev_order):
    nck = s // CHUNK
    ci = (lambda c: nck - 1 - c) if rev_order else (lambda c: c)
    tile = lambda off, div: pl.BlockSpec((CHUNK, 128), lambda p, c: (ci(c), off + p // div))
    common = [tile(0, 1), tile(8, 2), tile(12, 2), tile(0, 1),
              pl.BlockSpec((1, 8, CHUNK), lambda p, c: (p, 0, ci(c))),
              pl.BlockSpec((1, 128), lambda p, c: (0, p)),
              pl.BlockSpec((1, 8, 128), lambda p, c: (p, 0, 0))]
    hs = pl.BlockSpec((1, 1, CHUNK, 128), lambda p, c: (p, ci(c), 0, 0))
    return nck, common, tile(0, 1), hs


def ssd_fwd(xbc, dt_exp, dtt, a_exp, a_rows, rev, name):
    s = xbc.shape[0]
    nck, common, tile, hs_spec = _ssd_specs(s, rev)

    def body(x_ref, b_ref, c_ref, dt_ref, dtt_ref, a_ref, ar_ref, y_ref, hs_ref, h_scr):
        @pl.when(pl.program_id(1) == 0)
        def _():
            h_scr[...] = jnp.zeros_like(h_scr)

        v = _ssd_common(x_ref, b_ref, c_ref, dt_ref, dtt_ref, a_ref, ar_ref, rev)
        xdtb = v["xdt"].astype(BF16)
        yd = [_dot((v["cb"] * v["lmats"][h]).astype(BF16), xdtb) for h in range(2)]
        h_in = h_scr[...]
        hs_ref[0, 0] = h_in
        y_off = _dot(v["cm"], h_in.astype(BF16)) * jnp.exp(v["cs"])
        y_ref[...] = jnp.where(_iota((CHUNK, 128), 1) < HD, yd[0], yd[1]) + y_off
        decay = jnp.exp(v["tot"] - v["cs"])
        h_scr[...] = jnp.exp(v["tot"]) * h_in + _dot(v["bm"], (v["xdt"] * decay).astype(BF16), TN)

    return _pcall(
        body, name=name, grid=(8, nck), in_specs=common, out_specs=[tile, hs_spec],
        out_shape=[jax.ShapeDtypeStruct((s, D), F32), jax.ShapeDtypeStruct((8, nck, CHUNK, 128), F32)],
        scratch_shapes=[pltpu.VMEM((CHUNK, 128), F32)], compiler_params=_cparams(("parallel", "arbitrary")),
    )(xbc, xbc, xbc, dt_exp, dtt, a_exp, a_rows)


def ssd_bwd(xbc, dt_exp, dtt, a_exp, a_rows, hs, dy, rev, name):
    s = xbc.shape[0]
    nck, common, tile, hs_spec = _ssd_specs(s, not rev)

    def body(x_ref, b_ref, c_ref, dt_ref, dtt_ref, a_ref, ar_ref, hs_ref, dy_ref,
             dx_ref, ddt_ref, db_ref, dc_ref, dal_ref, dh_scr):
        @pl.when(pl.program_id(1) == 0)
        def _():
            dh_scr[...] = jnp.zeros_like(dh_scr)
            dal_ref[...] = jnp.zeros_like(dal_ref)

        v = _ssd_common(x_ref, b_ref, c_ref, dt_ref, dtt_ref, a_ref, ar_ref, rev)
        bm, cm, cs, tot, xdt = v["bm"], v["cm"], v["cs"], v["tot"], v["xdt"]
        h_in, dh = hs_ref[0, 0], dh_scr[...]
        dyv = dy_ref[...]
        dyb = dyv.astype(BF16)
        etot, decay, ecs = jnp.exp(tot), jnp.exp(tot - cs), jnp.exp(cs)
        xdtb = xdt.astype(BF16)
        xdec = xdt * decay
        dch = (dyv * ecs).astype(BF16)
        hb, dhb = h_in.astype(BF16), dh.astype(BF16)
        y_off = _dot(cm, hb) * ecs
        dc = _dot(dch, hb, NT)
        dh_y = _dot(cm, dch, TN)
        dxdec = _dot(bm, dhb)
        db = _dot(xdec.astype(BF16), dhb, NT)
        state_term = xdec * dxdec
        dtot = _colsum(dh * h_in) * etot + _colsum(state_term)
        head0 = _iota((CHUNK, 128), 1) < HD
        ii, jj = _iota((CHUNK, CHUNK), 0), _iota((CHUNK, CHUNK), 1)
        low_t = jj <= ii if rev else jj >= ii
        not_low_t = (~low_t).astype(BF16)
        dcb, dxd, da_l = 0.0, [], []
        for h in range(2):
            dyh = jnp.where(head0 if h == 0 else ~head0, dyb, jnp.zeros_like(dyb))
            gl = _dot(dyh, xdtb, NT) * v["lmats"][h]
            dcb = dcb + gl
            dxd.append(_dot((v["cb"] * v["lmats"][h]).astype(BF16), dyb, TN))
            w = (gl * v["cb"]).astype(BF16)
            da_l.append(jnp.sum(jnp.where(low_t, _dot(not_low_t, w, NT), 0.0), axis=1, keepdims=True))
        dxd = jnp.where(head0, dxd[0], dxd[1])
        dxdt = dxdec * decay + dxd
        dcbb = dcb.astype(BF16)
        dc_ref[...] = dc + _dot(dcbb, bm)
        db_ref[...] = db + _dot(dcbb, cm, TN)
        dcs = dyv * y_off - state_term + jnp.where(_iota((CHUNK, 128), 0) == v["last"], dtot, 0.0)
        lowb = v["low"].astype(BF16)
        da = sum(_dot(lowb, p, TN) for p in _parts(dcs, 2))
        seg = ((ii < HD) == (jj < HD)).astype(BF16)
        da = sum(_dot(p, seg) for p in _parts(da, 2)) + jnp.where(head0, da_l[0], da_l[1])
        ddt_x = sum(_dot(p, seg) for p in _parts(dxdt * v["x"], 2))
        dx_ref[...] = dxdt * v["dtx"]
        ddt_ref[...] = ddt_x + da * a_ref[...]
        dal_ref[0] += _colsum(da * v["a"])
        dh_scr[...] = etot * dh + dh_y

    return _pcall(
        body, name=name, grid=(8, nck), in_specs=common + [hs_spec, tile],
        out_specs=[tile, tile, tile, tile, pl.BlockSpec((1, 8, 128), lambda p, c: (p, 0, 0))],
        out_shape=[jax.ShapeDtypeStruct((s, D), F32)] * 4 + [jax.ShapeDtypeStruct((8, 8, 128), F32)],
        scratch_shapes=[pltpu.VMEM((CHUNK, 128), F32)], compiler_params=_cparams(("parallel", "arbitrary")),
    )(xbc, xbc, xbc, dt_exp, dtt, a_exp, a_rows, hs, dy)


def _group_norm_stats(g):
    r = [lax.rsqrt(jnp.mean(g[:, 256 * k:256 * k + 256] ** 2, axis=-1, keepdims=True) + EPS) for k in range(4)]
    grp = _iota(g.shape, 1) // 256
    return jnp.where(grp == 0, r[0], jnp.where(grp == 1, r[1], jnp.where(grp == 2, r[2], r[3])))


def _group_mean(t):
    m = [jnp.mean(t[:, 256 * k:256 * k + 256], axis=-1, keepdims=True) for k in range(4)]
    grp = _iota(t.shape, 1) // 256
    return jnp.where(grp == 0, m[0], jnp.where(grp == 1, m[1], jnp.where(grp == 2, m[2], m[3])))


def _mesh_pos():
    return lax.axis_index("x"), lax.axis_index("y"), lax.axis_index("c")


HBM = pl.BlockSpec(memory_space=pltpu.HBM)
SEM = pl.BlockSpec(memory_space=pltpu.SEMAPHORE)
EFFECT = pltpu.SideEffectType.DATAFLOW_SIDE_EFFECTING
SWAP_CHUNKS = 18


def _hbm(t):
    return pltpu.with_memory_space_constraint(t, pltpu.HBM)


def _other_chips(x, y):
    return [(1 - x, y), (x, 1 - y), (1 - x, 1 - y)]


def _peer(x, y, c, m):
    return x ^ (m >> 2), y ^ ((m >> 1) & 1), c ^ (m & 1)


def gather_start(srcs_a, srcs_b):
    srcs = [_hbm(t) for t in list(srcs_a) + list(srcs_b)]
    n, na = len(srcs), len(srcs_a)
    lands = [_hbm(lax.empty((4,) + t.shape, t.dtype)) for t in srcs]

    def body(*refs):
        src, land = refs[:n], refs[n:2 * n]
        sems = refs[2 * n:2 * n + 4]
        x, y, c = _mesh_pos()
        for k in range(n):
            for j, (px, py) in enumerate(_other_chips(x, y)):
                send, recv, idx = (sems[0], sems[1], 3 * k + j) if k < na else (sems[2], sems[3], 3 * (k - na) + j)
                pltpu.make_async_remote_copy(src_ref=src[k], dst_ref=land[k].at[2 * x + y], send_sem=send.at[idx],
                                             recv_sem=recv.at[idx], device_id=(px, py, c), device_id_type=MESH).start()

    sem_a, sem_b = pltpu.SemaphoreType.DMA((3 * na,)), pltpu.SemaphoreType.DMA((3 * (n - na),))
    res = _pcall(
        body, name="gather_start", in_specs=[HBM] * (2 * n), out_specs=[SEM] * 4 + [HBM] * (2 * n),
        out_shape=[sem_a, sem_a, sem_b, sem_b] + [pltpu.HBM(t.shape, t.dtype) for t in srcs + lands],
        input_output_aliases={i: 4 + i for i in range(2 * n)},
        compiler_params=pltpu.CompilerParams(has_side_effects=EFFECT),
    )(*srcs, *lands)
    thru_src, thru_land = res[4:4 + n], res[4 + n:]
    return ((res[0], res[1], thru_src[:na], thru_land[:na]), (res[2], res[3], thru_src[na:], thru_land[na:]))


def gather_wait(group, name, after=None):
    send, recv, srcs, lands = group
    n = len(srcs)

    def body(*refs):
        src, land, send_ref, recv_ref = refs[:n], refs[n:2 * n], refs[2 * n], refs[2 * n + 1]
        x, y, c = _mesh_pos()
        for j, (px, py) in enumerate(_other_chips(x, y)):
            for k in range(n):
                cp = pltpu.make_async_remote_copy(src_ref=src[k], dst_ref=land[k].at[2 * px + py], send_sem=send_ref.at[3 * k + j],
                                                  recv_sem=recv_ref.at[3 * k + j], device_id=(px, py, c), device_id_type=MESH)
                cp.wait_send()
                cp.wait_recv()

    extra = [] if after is None else [after]
    res = _pcall(
        body, name=name, in_specs=[HBM] * (2 * n) + [SEM, SEM] + [pl.BlockSpec(memory_space=pl.ANY)] * len(extra),
        out_specs=[HBM] * (2 * n), out_shape=[pltpu.HBM(t.shape, t.dtype) for t in list(srcs) + list(lands)],
        input_output_aliases={i: i for i in range(2 * n)}, compiler_params=pltpu.CompilerParams(has_side_effects=EFFECT),
    )(*srcs, *lands, send, recv, *extra)
    return res[:n], res[n:]


def scatter_start(pieces, smalls, name):
    srcs = [_hbm(t) for t in list(pieces) + list(smalls)]
    n, npc = len(srcs), len(pieces)
    lands = [_hbm(lax.empty((8,) + (t.shape[2:] if k < npc else t.shape), t.dtype)) for k, t in enumerate(srcs)]

    def body(*refs):
        src, land, send, recv = refs[:n], refs[n:2 * n], refs[2 * n], refs[2 * n + 1]
        token = refs[-1]
        x, y, c = _mesh_pos()
        for m in range(1, 8):
            px, py, pc = _peer(x, y, c, m)
            for k in range(n):
                s_ref = src[k].at[2 * px + py, pc] if k < npc else src[k]
                d_ref = land[k].at[m] if k < npc else land[k].at[4 * x + 2 * y + c]
                pltpu.make_async_remote_copy(src_ref=s_ref, dst_ref=d_ref, send_sem=send.at[7 * k + m - 1], recv_sem=recv.at[7 * k + m - 1],
                                             device_id=(px, py, pc), device_id_type=MESH).start()
        token[...] = jnp.zeros_like(token)

    sem = pltpu.SemaphoreType.DMA((7 * n,))
    res = _pcall(
        body, name=name, in_specs=[HBM] * (2 * n),
        out_specs=[SEM, SEM] + [HBM] * (2 * n) + [pl.BlockSpec(memory_space=pltpu.VMEM)],
        out_shape=[sem, sem] + [pltpu.HBM(t.shape, t.dtype) for t in srcs + lands] + [jax.ShapeDtypeStruct((8, 128), F32)],
        input_output_aliases={i: 2 + i for i in range(2 * n)},
        compiler_params=pltpu.CompilerParams(has_side_effects=EFFECT),
    )(*srcs, *lands)
    return (res[0], res[1], res[2:2 + n], res[2 + n:2 + 2 * n], npc), res[-1]


def scatter_wait(group, name):
    send, recv, srcs, lands, npc = group
    n = len(srcs)

    def body(*refs):
        src, land, send_ref, recv_ref = refs[:n], refs[n:2 * n], refs[2 * n], refs[2 * n + 1]
        x, y, c = _mesh_pos()
        for m in range(1, 8):
            px, py, pc = _peer(x, y, c, m)
            for k in range(n):
                s_ref = src[k].at[0, 0] if k < npc else src[k]
                d_ref = land[k].at[m] if k < npc else land[k].at[4 * px + 2 * py + pc]
                cp = pltpu.make_async_remote_copy(src_ref=s_ref, dst_ref=d_ref, send_sem=send_ref.at[7 * k + m - 1],
                                                  recv_sem=recv_ref.at[7 * k + m - 1], device_id=(px, py, pc), device_id_type=MESH)
                cp.wait_send()
                cp.wait_recv()

    res = _pcall(
        body, name=name, in_specs=[HBM] * (2 * n) + [SEM, SEM], out_specs=[HBM] * (2 * n),
        out_shape=[pltpu.HBM(t.shape, t.dtype) for t in list(srcs) + list(lands)],
        input_output_aliases={i: i for i in range(2 * n)}, compiler_params=pltpu.CompilerParams(has_side_effects=EFFECT),
    )(*srcs, *lands, send, recv)
    return res[:n], res[n:]


def swap_halves(piece):
    any_space = pl.BlockSpec(memory_space=pltpu.VMEM)
    rows = piece.shape[0] // SWAP_CHUNKS
    assert rows * SWAP_CHUNKS == piece.shape[0] and rows % 8 == 0

    def body(p_ref, o_ref, send_sems, recv_sems, local_sem):
        x, y, c = _mesh_pos()
        local = pltpu.make_async_copy(p_ref, o_ref.at[c], local_sem)
        local.start()

        def chunk(k, slot):
            return pltpu.make_async_remote_copy(
                src_ref=p_ref.at[pl.ds(k * rows, rows)], dst_ref=o_ref.at[slot, pl.ds(k * rows, rows)], send_sem=send_sems.at[k],
                recv_sem=recv_sems.at[k], device_id=(x, y, 1 - c), device_id_type=MESH)

        for k in range(SWAP_CHUNKS):
            chunk(k, c).start()
        for k in range(SWAP_CHUNKS):
            chunk(k, 1 - c).wait_recv()
        for k in range(SWAP_CHUNKS):
            chunk(k, c).wait_send()
        local.wait()

    return _pcall(
        body, name="swap_halves", in_specs=[any_space], out_specs=any_space,
        out_shape=jax.ShapeDtypeStruct((2,) + piece.shape, piece.dtype),
        scratch_shapes=[pltpu.SemaphoreType.DMA((SWAP_CHUNKS,)), pltpu.SemaphoreType.DMA((SWAP_CHUNKS,)), pltpu.SemaphoreType.DMA],
        compiler_params=_cparams(),
    )(piece)


def adamw(w, g, m, v, name):
    rows, cols = w.shape
    tm = rows
    for t in (256, 352, 128, 144, 64, 32, 16, 8):
        if rows % t == 0:
            tm = t
            break

    def fn(i, nrow, wv, gv, mv, vv):
        mn = ADAM_B1 * mv + (1.0 - ADAM_B1) * gv
        vn = ADAM_B2 * vv + (1.0 - ADAM_B2) * (gv * gv)
        m_hat = mn / (1.0 - ADAM_B1 ** ADAM_STEP)
        v_hat = vn / (1.0 - ADAM_B2 ** ADAM_STEP)
        delta = -ADAM_LR * (m_hat / (jnp.sqrt(v_hat) + ADAM_EPS) + ADAM_WD * wv)
        return delta, mn, vn

    return ew(fn, name, rows, tm, 1, [(t, "row", cols, 0) for t in (w, g, m, v)], [(cols, F32, cols)] * 3)


BIG = ("w_in", "w_out", "w_up", "w_down")
REST = ("w_out", "w_up", "w_down")
REST_ROWS = (512, 1408, 704)
W_IN_ROWS, W_IN_ROWS_PADDED = 1544, 1552
SMALL = ("norm1_w", "ssm_conv_w", "ssm_conv_b", "a_log_f", "a_log_b", "dt_bias_f", "dt_bias_b", "d_skip",
         "ssm_norm_w", "norm2_w", "ffn_conv_w", "ffn_conv_b", "final_norm_w")
WEIGHTS = ("norm1_w", "w_in", "ssm_conv_w", "ssm_conv_b", "a_log_f", "a_log_b", "dt_bias_f", "dt_bias_b", "d_skip",
           "ssm_norm_w", "w_out", "norm2_w", "w_up", "ffn_conv_w", "ffn_conv_b", "w_down", "final_norm_w")
INPUTS = ("x",) + WEIGHTS + ("loss_target",) + tuple("m_" + n for n in WEIGHTS) + tuple("v_" + n for n in WEIGHTS)


def _flat_rows(parts, width, rows):
    flat = jnp.concatenate([p.reshape(-1) for p in parts])
    return jnp.pad(flat, (0, rows * width - flat.shape[0])).reshape(rows, width)


def _split_flat(flat, shapes):
    out, pos = [], 0
    flat = flat.reshape(-1)
    for shp in shapes:
        n = int(np.prod(shp))
        out.append(flat[pos:pos + n].reshape(shp))
        pos += n
    return out


def _col_shards(t, nshard):
    r, c = t.shape
    return t.reshape(r, nshard, c // nshard).transpose(1, 0, 2).reshape(nshard, -1, D)


def _row_shards(t, nshard):
    r, c = t.shape
    return t.reshape(nshard, -1, D)


def kernel(x, norm1_w, w_in, ssm_conv_w, ssm_conv_b, a_log_f, a_log_b, dt_bias_f, dt_bias_b, d_skip, ssm_norm_w, w_out, norm2_w, w_up, ffn_conv_w, ffn_conv_b, w_down, final_norm_w, loss_target, m_norm1_w, m_w_in, m_ssm_conv_w, m_ssm_conv_b, m_a_log_f, m_a_log_b, m_dt_bias_f, m_dt_bias_b, m_d_skip, m_ssm_norm_w, m_w_out, m_norm2_w, m_w_up, m_ffn_conv_w, m_ffn_conv_b, m_w_down, m_final_norm_w, v_norm1_w, v_w_in, v_ssm_conv_w, v_ssm_conv_b, v_a_log_f, v_a_log_b, v_dt_bias_f, v_dt_bias_b, v_d_skip, v_ssm_norm_w, v_w_out, v_norm2_w, v_w_up, v_ffn_conv_w, v_ffn_conv_b, v_w_down, v_final_norm_w):
    p = dict(zip(INPUTS, (x, norm1_w, w_in, ssm_conv_w, ssm_conv_b, a_log_f, a_log_b, dt_bias_f, dt_bias_b, d_skip, ssm_norm_w, w_out, norm2_w, w_up, ffn_conv_w, ffn_conv_b, w_down, final_norm_w, loss_target, m_norm1_w, m_w_in, m_ssm_conv_w, m_ssm_conv_b, m_a_log_f, m_a_log_b, m_dt_bias_f, m_dt_bias_b, m_d_skip, m_ssm_norm_w, m_w_out, m_norm2_w, m_w_up, m_ffn_conv_w, m_ffn_conv_b, m_w_down, m_final_norm_w, v_norm1_w, v_w_in, v_ssm_conv_w, v_ssm_conv_b, v_a_log_f, v_a_log_b, v_dt_bias_f, v_dt_bias_b, v_d_skip, v_ssm_norm_w, v_w_out, v_norm2_w, v_w_up, v_ffn_conv_w, v_ffn_conv_b, v_w_down, v_final_norm_w)))
    x = p["x"][0]
    tgt = p["loss_target"][0]
    s = x.shape[0]
    chip = 2 * lax.axis_index("x") + lax.axis_index("y")

    own_slot = lambda land, mine, slot: lax.dynamic_update_slice_in_dim(land, mine[None], slot, axis=0)
    src_in = p["w_in"][0].reshape(-1, D).astype(BF16)
    src_rest = jnp.concatenate([p[n][0].reshape(-1, D) for n in REST], axis=0).astype(BF16)
    small_w = _flat_rows([p["ssm_conv_w"][0], p["ffn_conv_w"][0]], 128, 48)
    gather_in, gather_rest = gather_start([src_in, small_w], [src_rest])
    (src_in, small_w), (wg_in, sg) = gather_wait(gather_in, "gather_wait_in")
    w_in = own_slot(wg_in, src_in, chip).reshape(4, D, -1).transpose(1, 0, 2).reshape(D, -1)
    sg = own_slot(sg, small_w, chip)
    o = np.cumsum((0,) + REST_ROWS)
    n_in = w_in.shape[1]
    n_main = 6 * D
    w_main = w_in[:, :n_main]
    w_dt = jnp.pad(w_in[:, n_main:], ((0, 0), (0, 128 - (n_in - n_main))))
    sgf = sg.reshape(4, -1)
    n_sc, n_fc = p["ssm_conv_w"].shape[1], p["ffn_conv_w"].shape[1]
    ssm_cw = sgf[:, :n_sc * 3].reshape(-1, 3).T
    ffn_cw = sgf[:, n_sc * 3:(n_sc + n_fc) * 3].reshape(-1, 3).T
    ssm_cb, ffn_cb = p["ssm_conv_b"], p["ffn_conv_b"]
    n1w, n2w, snw, fnw = p["norm1_w"], p["norm2_w"], p["ssm_norm_w"], p["final_norm_w"].reshape(1, D)

    h1, = ew(lambda i, n, xv, w: _rms_fwd(xv, w), "rms1", s, 256, 1,
             [(x, "row", D, 0), (n1w, "const", D, 0)], [(D, BF16, D)])
    proj = matmul(h1, w_main, "nn", "in_proj")
    proj_dt = matmul(h1, w_dt, "nn", "in_proj_dt")
    tabs = _rope_tables(s)
    attn, lse = attn_fwd_all(proj, tabs, "attn_fwd")

    def conv_silu_fn(i, n, xv, xp, xn, w, b):
        return _silu(w[0:1] * _shift_down(xv, xp, i) + w[1:2] * xv + w[2:3] * _shift_up(xv, xn, i, n) + b)

    xbc_act, = ew(conv_silu_fn, "ssm_conv", s, 512, 4,
                  [(proj, "row", 512, 8), (proj, "prev", 512, 8), (proj, "next", 512, 8),
                   (ssm_cw, "const", 512, 0), (ssm_cb, "const", 512, 0)], [(2 * D, F32, 512)])
    dt_bias = jnp.pad(jnp.concatenate([p["dt_bias_f"], p["dt_bias_b"]], axis=1), ((0, 0), (0, 96)))

    def softplus_fn(i, n, r, b):
        t = r + b
        return jnp.maximum(t, 0.0) + jnp.log(1.0 + jnp.exp(-jnp.abs(t)))

    dt, = ew(softplus_fn, "dt_softplus", s, 512, 1, [(proj_dt, "row", 128, 0), (dt_bias, "const", 128, 0)], [(128, F32, 128)])
    d_exp = jnp.repeat(p["d_skip"], HD, axis=1)
    ssd = []
    for k, (a_log, rev) in enumerate(((p["a_log_f"], False), (p["a_log_b"], True))):
        dt_k = dt[:, 16 * k:16 * k + 16]
        a_head = -jnp.exp(a_log)
        dt_exp = jnp.repeat(dt_k, HD, axis=1)
        dtt = jnp.pad(dt_k.T.reshape(8, 2, s), ((0, 0), (0, 6), (0, 0)))
        a_exp = jnp.repeat(a_head, HD, axis=1)
        a_rows = jnp.broadcast_to(jnp.pad(a_head.reshape(8, 2), ((0, 0), (0, 6)))[:, :, None], (8, 8, 128))
        y_k, hs_k = ssd_fwd(xbc_act, dt_exp, dtt, a_exp, a_rows, rev, "ssd_fwd_%d" % k)
        ssd.append(dict(dt_exp=dt_exp, dtt=dtt, a_exp=a_exp, a_rows=a_rows, y=y_k, hs=hs_k, rev=rev))

    def gate_fn(i, n, yf, yb, xs, z, dsk, w):
        g = (yf + yb + dsk * xs) * _silu(z)
        return g * _group_norm_stats(g) * w

    ssm_out, = ew(gate_fn, "ssm_gate_norm", s, 256, 1,
                  [(ssd[0]["y"], "row", D, 0), (ssd[1]["y"], "row", D, 0), (xbc_act, "row", D, 0), (proj, "row", D, 3),
                   (d_exp, "const", D, 0), (snw, "const", D, 0)], [(D, F32, D)])
    mix = jnp.concatenate([attn, ssm_out], axis=1).astype(BF16)
    (src_rest,), (wg_rest,) = gather_wait(gather_rest, "gather_wait_rest", after=mix)
    wg_rest = own_slot(wg_rest, src_rest, chip)
    w_out = wg_rest[:, o[0]:o[1]].reshape(-1, D)
    w_up = wg_rest[:, o[1]:o[2]].reshape(4, D, -1).transpose(1, 0, 2).reshape(D, -1)
    w_down = wg_rest[:, o[2]:o[3]].reshape(-1, D)
    mix_w = matmul(mix, w_out, "nn", "out_proj")

    def res_rms_fn(i, n, xv, mw, w):
        x1v = xv + mw
        return x1v, _rms_fwd(x1v, w)

    x1, h2 = ew(res_rms_fn, "res_rms2", s, 256, 1, [(x, "row", D, 0), (mix_w, "row", D, 0), (n2w, "const", D, 0)],
                [(D, F32, D), (D, BF16, D)])
    hw = matmul(h2, w_up, "nn", "ffn_up")
    nfb = D_FF // 256
    ffn_conv_ins = [(hw, "row", 256, 0), (hw, "prev", 256, 0), (hw, "next", 256, 0),
                    (hw, "row", 256, nfb), (hw, "prev", 256, nfb), (hw, "next", 256, nfb),
                    (ffn_cw, "const", 256, 0), (ffn_cw, "const", 256, nfb), (ffn_cb, "const", 256, 0), (ffn_cb, "const", 256, nfb)]

    def ffn_conv(i, n, g, gp, gn, u, up_, un, wg_, wu, bg, bu):
        gs = (_shift_down(g, gp, i), g, _shift_up(g, gn, i, n))
        us = (_shift_down(u, up_, i), u, _shift_up(u, un, i, n))
        gate = wg_[0:1] * gs[0] + wg_[1:2] * gs[1] + wg_[2:3] * gs[2] + bg
        upv = wu[0:1] * us[0] + wu[1:2] * us[1] + wu[2:3] * us[2] + bu
        return gate, upv, gs, us

    def glu_fn(i, n, *blocks):
        gate, upv, _, _ = ffn_conv(i, n, *blocks)
        return _silu(gate) * upv

    act, = ew(glu_fn, "ffn_conv_glu", s, 512, nfb, ffn_conv_ins, [(D_FF, BF16, 256)])
    ffn = matmul(act, w_down, "nn", "ffn_down")

    def head_fn(i, n, x1v, fv, tv, w):
        x2 = x1v + fv
        r = lax.rsqrt(jnp.mean(x2 * x2, axis=-1, keepdims=True) + EPS)
        xh = x2 * r
        diff = xh * w - tv
        loss = 0.5 * jnp.sum(jnp.mean(diff * diff, axis=-1, keepdims=True), axis=0, keepdims=True)
        dout = diff * (1.0 / D)
        dxh = dout * w
        dx2 = r * (dxh - xh * jnp.mean(dxh * xh, axis=-1, keepdims=True))
        return dx2, jnp.broadcast_to(loss, (1, 128)), _colsum(dout * xh)

    dx2, loss_acc, g_fnw = ew(head_fn, "loss_head", s, 256, 1,
                              [(x1, "row", D, 0), (ffn, "row", D, 0), (tgt, "row", D, 0), (fnw, "const", D, 0)],
                              [(D, F32, D)], [(128, 128), (D, D)])
    loss = lax.psum(loss_acc[0, 0], ("x", "y", "c"))

    g_w_down = matmul(act, dx2, "tn", "d_w_down")
    dact = matmul(dx2, w_down, "nt", "d_act")

    def glu_bwd_fn(i, n, *blocks):
        gate, upv, gs, us = ffn_conv(i, n, *blocks[:-1])
        da = blocks[-1]
        dg = da * upv * _dsilu(gate)
        du = da * _silu(gate)
        return (dg, du) + tuple(_colsum(dg * t) for t in gs) + tuple(_colsum(du * t) for t in us) + (_colsum(dg), _colsum(du))

    res = ew(glu_bwd_fn, "ffn_glu_bwd", s, 512, nfb, ffn_conv_ins + [(dact, "row", 256, 0)],
             [(D_FF, F32, 256)] * 2, [(D_FF, 256)] * 8)
    du_g, du_u = res[0], res[1]
    g_ffn_cw = jnp.concatenate([jnp.concatenate(res[2:5], axis=0), jnp.concatenate(res[5:8], axis=0)], axis=1).T
    g_ffn_cb = jnp.concatenate([res[8], res[9]], axis=1)

    def conv_t_fn(i, n, dv, dp, dn, w):
        return w[0:1] * _shift_up(dv, dn, i, n) + w[1:2] * dv + w[2:3] * _shift_down(dv, dp, i)

    def conv_t(du, cw, off, width, ncol, name):
        return ew(conv_t_fn, name, s, 512, ncol,
                  [(du, "row", width, 0), (du, "prev", width, 0), (du, "next", width, 0), (cw, "const", width, off)],
                  [(du.shape[1], F32, width)])[0]

    dhw_g = conv_t(du_g, ffn_cw, 0, 256, nfb, "ffn_conv_t_gate")
    dhw_u = conv_t(du_u, ffn_cw, nfb, 256, nfb, "ffn_conv_t_up")
    g_w_up = jnp.concatenate([matmul(h2, dhw_g, "tn", "d_w_up_gate"), matmul(h2, dhw_u, "tn", "d_w_up_up")], axis=1)
    dh2_a = matmul(dhw_g, w_up[:, :D_FF], "nt", "d_h2_gate")
    dh2_b = matmul(dhw_u, w_up[:, D_FF:], "nt", "d_h2_up")

    def res_rms_bwd_fn(i, n, dres, da, db, xin, w):
        dx, dw = _rms_bwd(da + db, xin, w)
        return dres + dx, dw

    dx1, g_n2w = ew(res_rms_bwd_fn, "res_rms2_bwd", s, 256, 1,
                    [(dx2, "row", D, 0), (dh2_a, "row", D, 0), (dh2_b, "row", D, 0), (x1, "row", D, 0), (n2w, "const", D, 0)],
                    [(D, F32, D)], [(D, D)])

    g_w_out = matmul(mix, dx1, "tn", "d_w_out")
    to_pieces = lambda t: t.astype(BF16).reshape(4, -1, 2, 512).transpose(0, 2, 1, 3)
    shards_rest = jnp.concatenate([_row_shards(g_w_out, 4), _col_shards(g_w_up, 4), _row_shards(g_w_down, 4)], axis=1)
    scatter_rest, token = scatter_start([to_pieces(shards_rest)], [], "scatter_start_rest")
    w_out_after = w_out + token[0:1, 0:1].astype(BF16)
    dmix = matmul(dx1, w_out_after, "nt", "d_mix")
    ii, jj = np.arange(D)[:, None] // HD, np.arange(D)[None, :] // HD
    seg = jnp.asarray(ii == jj, BF16)

    def gate_bwd_fn(i, n, dout, yf, yb, xs, z, dsk, w, segm):
        yt = yf + yb + dsk * xs
        sz = _silu(z)
        g = yt * sz
        r = _group_norm_stats(g)
        gh = g * r
        dn = dout * w
        dg = r * (dn - gh * _group_mean(dn * gh))
        dy = dg * sz
        dsk_lane = jnp.broadcast_to(_colsum(dy * xs), (8, D))
        return dy, dg * yt * _dsilu(z), _colsum(dout * gh), sum(_dot(q, segm) for q in _parts(dsk_lane, 2))[0:1]

    dy, dz, g_snw, g_dskip_l = ew(
        gate_bwd_fn, "ssm_gate_norm_bwd", s, 256, 1,
        [(dmix, "row", D, 1), (ssd[0]["y"], "row", D, 0), (ssd[1]["y"], "row", D, 0), (xbc_act, "row", D, 0),
         (proj, "row", D, 3), (d_exp, "const", D, 0), (snw, "const", D, 0), (seg, "const", D, 0)],
        [(D, F32, D)] * 2, [(D, D)] * 2)
    sb = [ssd_bwd(xbc_act, t["dt_exp"], t["dtt"], t["a_exp"], t["a_rows"], t["hs"], dy, t["rev"], "ssd_bwd_%d" % k)
          for k, t in enumerate(ssd)]

    def dxbc_act_fn(i, n, dxf, dxb, dyv, dsk, dbf, dbb, dcf, dcb_):
        db, dc = dbf + dbb, dcf + dcb_
        db = [db[:, 256 * g:256 * g + 128] + db[:, 256 * g + 128:256 * g + 256] for g in range(4)]
        dc = [dc[:, 256 * g:256 * g + 128] + dc[:, 256 * g + 128:256 * g + 256] for g in range(4)]
        return jnp.concatenate([dxf + dxb + dyv * dsk] + db + dc, axis=1)

    dxbc_act, = ew(dxbc_act_fn, "d_xbc_act", s, 256, 1,
                   [(sb[0][0], "row", D, 0), (sb[1][0], "row", D, 0), (dy, "row", D, 0), (d_exp, "const", D, 0),
                    (sb[0][2], "row", D, 0), (sb[1][2], "row", D, 0), (sb[0][3], "row", D, 0), (sb[1][3], "row", D, 0)],
                   [(2 * D, F32, 2 * D)])

    def silu_bwd_fn(i, n, xv, xp, xn, w, b, da):
        xs3 = (_shift_down(xv, xp, i), xv, _shift_up(xv, xn, i, n))
        du = da * _dsilu(w[0:1] * xs3[0] + w[1:2] * xs3[1] + w[2:3] * xs3[2] + b)
        return (du,) + tuple(_colsum(du * t) for t in xs3) + (_colsum(du),)

    res = ew(silu_bwd_fn, "ssm_conv_bwd", s, 512, 4,
             [(proj, "row", 512, 8), (proj, "prev", 512, 8), (proj, "next", 512, 8), (ssm_cw, "const", 512, 0),
              (ssm_cb, "const", 512, 0), (dxbc_act, "row", 512, 0)], [(2 * D, F32, 512)], [(2 * D, 512)] * 4)
    g_ssm_cw = jnp.concatenate(res[1:4], axis=0).T
    g_ssm_cb = res[4]
    dxbc = conv_t(res[0], ssm_cw, 0, 512, 4, "ssm_conv_t")
    ddt = jnp.pad(jnp.concatenate([sb[0][1][:, ::HD], sb[1][1][:, ::HD]], axis=1), ((0, 0), (0, 96)))

    def dt_bwd_fn(i, n, dd, r, b):
        dr = dd * _sigmoid(r + b)
        return dr, _colsum(dr)

    dproj_dt, g_dt_bias = ew(dt_bwd_fn, "dt_softplus_bwd", s, 512, 1,
                             [(ddt, "row", 128, 0), (proj_dt, "row", 128, 0), (dt_bias, "const", 128, 0)],
                             [(128, F32, 128)], [(128, 128)])
    g_a_log = [t[4][:, 0, ::HD].reshape(1, 16) for t in sb]

    dq, dk, dv = attn_bwd_all(proj, tabs, dmix, attn, lse, "attn_bwd")

    dproj = jnp.concatenate([dq, dk, dv, dz, dxbc], axis=1).astype(BF16)
    g_w_in = jnp.concatenate([matmul(h1, dproj, "tn", "d_w_in"), matmul(h1, dproj_dt, "tn", "d_w_in_dt")[:, :n_in - n_main]], axis=1)
    shards_in = jnp.pad(_col_shards(g_w_in, 4), ((0, 0), (0, W_IN_ROWS_PADDED - W_IN_ROWS), (0, 0)))
    scatter_in, token = scatter_start([to_pieces(shards_in)], [], "scatter_start_in")
    w_main_after = w_main + token[0:1, 0:1].astype(BF16)
    dh1_a = matmul(dproj, w_main_after, "nt", "d_h1")
    dh1_b = matmul(dproj_dt, w_dt, "nt", "d_h1_dt")
    grad_x, g_n1w = ew(res_rms_bwd_fn, "rms1_bwd", s, 256, 1,
                       [(dx1, "row", D, 0), (dh1_a, "row", D, 0), (dh1_b, "row", D, 0), (x, "row", D, 0), (n1w, "const", D, 0)],
                       [(D, F32, D)], [(D, D)])

    small_g = {"norm1_w": g_n1w, "ssm_conv_w": g_ssm_cw, "ssm_conv_b": g_ssm_cb, "a_log_f": g_a_log[0], "a_log_b": g_a_log[1],
               "dt_bias_f": g_dt_bias[:, :16], "dt_bias_b": g_dt_bias[:, 16:32], "d_skip": g_dskip_l[:, ::HD],
               "ssm_norm_w": g_snw, "norm2_w": g_n2w, "ffn_conv_w": g_ffn_cw, "ffn_conv_b": g_ffn_cb, "final_norm_w": g_fnw}
    small_shapes = [small_g[n].shape for n in SMALL]
    scatter_small, _ = scatter_start([], [_flat_rows([small_g[n] for n in SMALL], 128, SMALL_ROWS)], "scatter_start_small")
    (sent_rest,), (got_rest,) = scatter_wait(scatter_rest, "scatter_wait_rest")
    (sent_in,), (got_in,) = scatter_wait(scatter_in, "scatter_wait_in")
    (sent_small,), (got_small,) = scatter_wait(scatter_small, "scatter_wait_small")
    core = lax.axis_index("c")

    def sum8_fn(i, n, *v):
        t = v[0].astype(F32)
        for u in v[1:]:
            t = t + u.astype(F32)
        return t

    def sum_pieces(sent, got, tm, ncol, name):
        rows = got.shape[1]
        mine = lax.dynamic_slice(sent, (chip, core, 0, 0), (1, 1, rows, 512)).reshape(rows, 512)
        w = 512 // ncol
        ins = [(mine, "row", w, 0)] + [(got.reshape(8 * rows, 512), "row", w, 0, k * (rows // tm)) for k in range(1, 8)]
        return ew(sum8_fn, name, rows, tm, ncol, ins, [(512, F32, w)])[0]

    rows_rest = int(o[3])
    piece = jnp.concatenate([sum_pieces(sent_rest, got_rest, rows_rest // 4, 1, "sum_pieces_rest"),
                             sum_pieces(sent_in, got_in, W_IN_ROWS_PADDED, 2, "sum_pieces_in")], axis=0)
    got_small = own_slot(got_small, sent_small, 2 * chip + core)
    small_sum, = ew(sum8_fn, "sum_small", SMALL_ROWS, SMALL_ROWS, 1,
                    [(got_small.reshape(8 * SMALL_ROWS, 128), "row", 128, 0, k) for k in range(8)], [(128, F32, 128)])
    g_shard = swap_halves(piece).transpose(1, 0, 2).reshape(-1, D)
    grads = {n: g_shard[o[k]:o[k + 1]].reshape(p[n].shape) for k, n in enumerate(REST)}
    grads["w_in"] = g_shard[rows_rest:rows_rest + W_IN_ROWS].reshape(p["w_in"].shape)
    for n, g in zip(SMALL, _split_flat(small_sum, small_shapes)):
        if n in ("ssm_conv_w", "ffn_conv_w"):
            rows = p[n].shape[1]
            g = lax.dynamic_slice_in_dim(g, chip * rows, rows, axis=0)
        grads[n] = g.reshape(p[n].shape)

    delta, new_m, new_v = {}, {}, {}
    for n in BIG:
        shp = p[n].shape
        r = [t.reshape(shp[1:]) for t in (p[n], grads[n], p["m_" + n], p["v_" + n])]
        delta[n], new_m[n], new_v[n] = [t.reshape(shp) for t in adamw(*r, "adamw_" + n)]
    shapes = [p[n].shape for n in SMALL]
    total = sum(int(np.prod(sh)) for sh in shapes)
    rows = -(-total // 1024) * 8
    packs = [_flat_rows([t[n] for n in SMALL], 128, rows)
             for t in (p, grads, {n: p["m_" + n] for n in SMALL}, {n: p["v_" + n] for n in SMALL})]
    for dst, t in zip((delta, new_m, new_v), adamw(*packs, "adamw_small")):
        for n, u in zip(SMALL, _split_flat(t, shapes)):
            dst[n] = u
    return (loss, grad_x[None], *[grads[n] for n in WEIGHTS], *[delta[n] for n in WEIGHTS],
            *[new_m[n] for n in WEIGHTS], *[new_v[n] for n in WEIGHTS])
```

```python
import numpy as np
import jax
import jax.numpy as jnp
from jax import lax
from jax.experimental import pallas as pl
from jax.experimental.pallas import tpu as pltpu

F32, BF16 = jnp.float32, jnp.bfloat16
MESH = pl.DeviceIdType.MESH
V7X_VMEM_LIMIT = 56 * 1024 * 1024

D = 1024
HD = 64
EPS = 1e-6
CHUNK = 128
D_FF = 2816
ROPE_DIM = 16
ROPE_THETA = 500000.0
PATTERN_DILATIONS = (1, 4, 16)
BAND = 64
SMALL_ROWS = 280
ADAM_LR, ADAM_B1, ADAM_B2, ADAM_EPS, ADAM_WD, ADAM_STEP = 0.001, 0.9, 0.999, 1e-08, 0.01, 10

NN = (((1,), (0,)), ((), ()))
NT = (((1,), (1,)), ((), ()))
TN = (((0,), (0,)), ((), ()))


def _pcall(body, **kw):
    return pl.pallas_call(body, **kw)


def _cparams(sem=None):
    return pltpu.CompilerParams(dimension_semantics=sem, vmem_limit_bytes=V7X_VMEM_LIMIT)


def _dot(a, b, dims=NN):
    return lax.dot_general(a, b, dims, preferred_element_type=F32)


def _pick(n, cap):
    if n <= cap:
        return n
    best = 0
    for t in range(128, cap + 1, 128):
        if n % t == 0:
            best = t
    assert best, (n, cap)
    return best


def _iota(shape, dim):
    return lax.broadcasted_iota(jnp.int32, shape, dim)


def _parts(x, n):
    out, r = [], x
    for _ in range(n):
        h = r.astype(BF16)
        out.append(h)
        r = r - h.astype(F32)
    return out


def _sigmoid(x):
    return 1.0 / (1.0 + jnp.exp(-x))


def _silu(x):
    return x * _sigmoid(x)


def _dsilu(x):
    s = _sigmoid(x)
    return s * (1.0 + x * (1.0 - s))


def matmul(a, b, mode, name, out_dtype=F32):
    if mode == "nn":
        (m, k), (_, n) = a.shape, b.shape
    elif mode == "nt":
        (m, k), (n, _) = a.shape, b.shape
    else:
        (k, m), (_, n) = a.shape, b.shape
    tm, tn, tk = _pick(m, 1408), _pick(n, 1408), _pick(k, 1408)
    nk = k // tk
    dims = {"nn": NN, "nt": NT, "tn": TN}[mode]
    a_spec = pl.BlockSpec((tk, tm), lambda i, j, kk: (kk, i)) if mode == "tn" else pl.BlockSpec((tm, tk), lambda i, j, kk: (i, kk))
    b_spec = pl.BlockSpec((tn, tk), lambda i, j, kk: (j, kk)) if mode == "nt" else pl.BlockSpec((tk, tn), lambda i, j, kk: (kk, j))

    def body(a_ref, b_ref, o_ref, *acc):
        part = _dot(a_ref[...].astype(BF16), b_ref[...].astype(BF16), dims)
        if nk == 1:
            o_ref[...] = part.astype(o_ref.dtype)
            return
        acc_ref, kk = acc[0], pl.program_id(2)

        @pl.when(kk == 0)
        def _():
            acc_ref[...] = part

        @pl.when((kk > 0) & (kk < nk - 1))
        def _():
            acc_ref[...] += part

        @pl.when(kk == nk - 1)
        def _():
            o_ref[...] = (acc_ref[...] + part).astype(o_ref.dtype)

    return _pcall(
        body, name=name, grid=(m // tm, n // tn, nk), in_specs=[a_spec, b_spec],
        out_specs=pl.BlockSpec((tm, tn), lambda i, j, kk: (i, j)),
        out_shape=jax.ShapeDtypeStruct((m, n), out_dtype),
        scratch_shapes=[pltpu.VMEM((tm, tn), F32)] if nk > 1 else [],
        compiler_params=_cparams(("parallel", "parallel", "arbitrary")),
    )(a, b)


def ew(fn, name, rows, tm, ncol, ins, outs, accs=()):
    nrow = rows // tm
    r8 = tm // 8
    in_specs, arrays = [], []
    for ent in ins:
        arr, kind, w, off = ent[:4]
        roff = ent[4] if len(ent) > 4 else 0
        if kind == "row":
            spec = pl.BlockSpec((tm, w), lambda j, i, off=off, roff=roff: (i + roff, j + off))
        elif kind == "const":
            spec = pl.BlockSpec((arr.shape[0], w), lambda j, i, off=off: (0, j + off))
        elif kind == "prev":
            spec = pl.BlockSpec((8, w), lambda j, i, off=off: (jnp.maximum(i * r8 - 1, 0), j + off))
        else:
            spec = pl.BlockSpec((8, w), lambda j, i, off=off: (jnp.minimum((i + 1) * r8, rows // 8 - 1), j + off))
        in_specs.append(spec)
        arrays.append(arr)
    out_specs = [pl.BlockSpec((tm, w), lambda j, i: (i, j)) for (_, _, w) in outs]
    out_shape = [jax.ShapeDtypeStruct((rows, c), dt) for (c, dt, _) in outs]
    out_specs += [pl.BlockSpec((1, w), lambda j, i: (0, j)) for (_, w) in accs]
    out_shape += [jax.ShapeDtypeStruct((1, c), F32) for (c, _) in accs]
    nin, nout = len(ins), len(outs)

    def body(*refs):
        i = pl.program_id(1)
        res = fn(i, nrow, *[r[...] for r in refs[:nin]])
        if not isinstance(res, (tuple, list)):
            res = (res,)
        for r, v in zip(refs[nin:nin + nout], res[:nout]):
            r[...] = v.astype(r.dtype)
        if accs:
            acc_refs = refs[nin + nout:]

            @pl.when(i == 0)
            def _():
                for r in acc_refs:
                    r[...] = jnp.zeros_like(r)

            for r, v in zip(acc_refs, res[nout:]):
                r[...] += v

    res = _pcall(
        body, name=name, grid=(ncol, nrow), in_specs=in_specs, out_specs=out_specs, out_shape=out_shape,
        compiler_params=_cparams(("parallel", "arbitrary")),
    )(*arrays)
    return res


def _shift_down(x, prev8, i):
    first = jnp.where(i == 0, 0.0, prev8[7:8, :])
    return jnp.where(_iota(x.shape, 0) == 0, first, pltpu.roll(x, 1, 0))


def _shift_up(x, next8, i, nrow):
    last = jnp.where(i == nrow - 1, 0.0, next8[0:1, :])
    return jnp.where(_iota(x.shape, 0) == x.shape[0] - 1, last, pltpu.roll(x, x.shape[0] - 1, 0))


def _colsum(x):
    return jnp.sum(x, axis=0, keepdims=True)


def _rms_fwd(x, w):
    r = lax.rsqrt(jnp.mean(x * x, axis=-1, keepdims=True) + EPS)
    return x * r * w


def _rms_bwd(dy, x, w):
    r = lax.rsqrt(jnp.mean(x * x, axis=-1, keepdims=True) + EPS)
    xh = x * r
    dxh = dy * w
    dx = r * (dxh - xh * jnp.mean(dxh * xh, axis=-1, keepdims=True))
    return dx, _colsum(dy * xh)


def _rope_tables(s):
    half = ROPE_DIM // 2
    inv_freq = jnp.power(ROPE_THETA, -jnp.arange(half, dtype=F32) * 2.0 / ROPE_DIM)
    ang = jnp.arange(s, dtype=F32)[:, None] * inv_freq[None, :]
    cos, sin = jnp.cos(ang), jnp.sin(ang)
    one, zero = jnp.ones((s, HD - ROPE_DIM), F32), jnp.zeros((s, HD - ROPE_DIM), F32)
    z8 = jnp.zeros((s, half), F32)
    c = jnp.concatenate([cos, cos, one], axis=1)
    sa = jnp.concatenate([-sin, z8, zero], axis=1)
    sb = jnp.concatenate([z8, sin, zero], axis=1)
    return [jnp.tile(t, (1, 2)) for t in (c, sa, sb)]


ATTN_CHUNK = 1024


def _attn_plan(s):
    plan = []
    for d in PATTERN_DILATIONS:
        per_res = ATTN_CHUNK // d
        tq = min(128, per_res)
        plan.append((d, tq, min(s // d, tq + 2 * BAND), per_res // tq, s // d))
    return plan


def _rows(start, size, d):
    return pl.ds(start, size) if d == 1 else pl.ds(start, size, stride=d)


def _for_tiles(chunk, pat, fn):
    d, tq, win, nblk, seq_len = pat
    for b in range(nblk):
        t0 = chunk * (ATTN_CHUNK // d) + b * tq
        kloc = jnp.clip(t0 - BAND, 0, seq_len - win)
        valid = jnp.abs(kloc + _iota((tq, win), 1) - (t0 + _iota((tq, win), 0))) <= BAND
        if d == 1:
            fn(b * tq, pl.multiple_of(kloc, BAND), valid)
        else:
            def step(r, carry, qoff=d * b * tq, koff=d * kloc, valid=valid):
                fn(qoff + r, koff + r, valid)
                return carry
            lax.fori_loop(0, d, step, 0, unroll=min(d, 4))


def _rope_pair(x, c, sa, sb):
    n = x.shape[1]
    return x * c + pltpu.roll(x, n - 8, 1) * sa + pltpu.roll(x, 8, 1) * sb


def _rope_pair_t(dy, c, sa, sb):
    n = dy.shape[1]
    return dy * c + pltpu.roll(dy * sa, 8, 1) + pltpu.roll(dy * sb, n - 8, 1)


def _attn_specs(s):
    whole = lambda off: pl.BlockSpec((s, 128), lambda p, c: (0, off + p))
    table = pl.BlockSpec((s, 128), lambda p, c: (0, 0))
    chunk = pl.BlockSpec((ATTN_CHUNK, 128), lambda p, c: (c, p))
    return whole, table, chunk


def attn_fwd_all(proj, tabs, name):
    s = proj.shape[0]
    plan = _attn_plan(s)
    whole, table, chunk_spec = _attn_specs(s)

    def body(q_ref, k_ref, v_ref, c_ref, sa_ref, sb_ref, o_ref, lse_ref, qs, ks, acc_s, m_s, l_s):
        chunk = pl.program_id(1)

        @pl.when(chunk == 0)
        def _():
            qs[...] = _rope_pair(q_ref[...], c_ref[...], sa_ref[...], sb_ref[...]) * (HD ** -0.5)
            ks[...] = _rope_pair(k_ref[...], c_ref[...], sa_ref[...], sb_ref[...])

        base = pl.multiple_of(chunk * ATTN_CHUNK, ATTN_CHUNK)
        for pi, pat in enumerate(plan):
            d, tq, win = pat[:3]
            head0 = _iota((tq, 128), 1) < HD

            def tile(qrow, krow, valid, pi=pi, d=d, tq=tq, win=win, head0=head0):
                qv = qs[_rows(base + qrow, tq, d), :].astype(BF16)
                kw = ks[_rows(krow, win, d), :].astype(BF16)
                vw = v_ref[_rows(krow, win, d), :].astype(BF16)
                acc, m, den = [], [], []
                for h in range(2):
                    qh = jnp.where(head0 if h == 0 else ~head0, qv, jnp.zeros_like(qv))
                    sc = jnp.where(valid, _dot(qh, kw, NT), -1e30)
                    mh = jnp.max(sc, axis=1, keepdims=True)
                    p = jnp.exp(sc - mh)
                    m.append(mh)
                    den.append(jnp.sum(p, axis=1, keepdims=True))
                    acc.append(_dot(p.astype(BF16), vw))
                acc_s[pi, _rows(qrow, tq, d), :] = jnp.where(head0, acc[0], acc[1])
                m_s[pi, _rows(qrow, tq, d), :] = jnp.where(head0, m[0], m[1])
                l_s[pi, _rows(qrow, tq, d), :] = jnp.where(head0, den[0], den[1])

            _for_tiles(chunk, pat, tile)
        m_all = jnp.maximum(jnp.maximum(m_s[0], m_s[1]), m_s[2])
        e = [jnp.exp(m_s[k] - m_all) for k in range(3)]
        den = e[0] * l_s[0] + e[1] * l_s[1] + e[2] * l_s[2]
        o_ref[...] = (e[0] * acc_s[0] + e[1] * acc_s[1] + e[2] * acc_s[2]) / den
        lse_ref[...] = m_all + jnp.log(den)

    stat = pltpu.VMEM((3, ATTN_CHUNK, 128), F32)
    return _pcall(
        body, name=name, grid=(D // 128, s // ATTN_CHUNK),
        in_specs=[whole(0), whole(8), whole(16), table, table, table], out_specs=[chunk_spec, chunk_spec],
        out_shape=[jax.ShapeDtypeStruct((s, D), F32)] * 2,
        scratch_shapes=[pltpu.VMEM((s, 128), F32), pltpu.VMEM((s, 128), F32), stat, stat, stat],
        compiler_params=_cparams(("parallel", "arbitrary")),
    )(proj, proj, proj, *tabs)


def attn_bwd_all(proj, tabs, dmix, o, lse, name):
    s = proj.shape[0]
    plan = _attn_plan(s)
    whole, table, chunk_spec = _attn_specs(s)
    nchunk = s // ATTN_CHUNK

    def body(q_ref, k_ref, v_ref, c_ref, sa_ref, sb_ref, do_ref, o_ref, lse_ref, dq_ref, dk_ref, dv_ref, qs, ks, delta_s):
        chunk = pl.program_id(1)

        @pl.when(chunk == 0)
        def _():
            qs[...] = _rope_pair(q_ref[...], c_ref[...], sa_ref[...], sb_ref[...]) * (HD ** -0.5)
            ks[...] = _rope_pair(k_ref[...], c_ref[...], sa_ref[...], sb_ref[...])
            dk_ref[...] = jnp.zeros_like(dk_ref)
            dv_ref[...] = jnp.zeros_like(dv_ref)

        base = pl.multiple_of(chunk * ATTN_CHUNK, ATTN_CHUNK)
        prod = do_ref[...] * o_ref[...]
        first = _iota(prod.shape, 1) < HD
        delta_s[...] = jnp.where(first, jnp.sum(jnp.where(first, prod, 0.0), axis=1, keepdims=True),
                                 jnp.sum(jnp.where(first, 0.0, prod), axis=1, keepdims=True))
        for pi, pat in enumerate(plan):
            d, tq, win = pat[:3]
            head0 = _iota((tq, 128), 1) < HD

            def tile(qrow, krow, valid, pi=pi, d=d, tq=tq, win=win, head0=head0):
                qv = qs[_rows(base + qrow, tq, d), :].astype(BF16)
                kw = ks[_rows(krow, win, d), :].astype(BF16)
                vw = v_ref[_rows(krow, win, d), :].astype(BF16)
                dob = do_ref[_rows(qrow, tq, d), :].astype(BF16)
                lsev = lse_ref[_rows(qrow, tq, d), :]
                delta = delta_s[_rows(qrow, tq, d), :]
                dq, dk, dv = [], 0.0, 0.0
                for h in range(2):
                    hm = head0 if h == 0 else ~head0
                    qh = jnp.where(hm, qv, jnp.zeros_like(qv))
                    doh = jnp.where(hm, dob, jnp.zeros_like(dob))
                    col = slice(HD * h, HD * h + 1)
                    p = jnp.where(valid, jnp.exp(_dot(qh, kw, NT) - lsev[:, col]), 0.0)
                    ds = (p * (_dot(doh, vw, NT) - delta[:, col])).astype(BF16)
                    dq.append(_dot(ds, kw))
                    dk = dk + _dot(ds, qh, TN)
                    dv = dv + _dot(p.astype(BF16), doh, TN)
                dqv = jnp.where(head0, dq[0], dq[1])
                if pi == 0:
                    dq_ref[_rows(qrow, tq, d), :] = dqv
                else:
                    dq_ref[_rows(qrow, tq, d), :] += dqv
                dk_ref[_rows(krow, win, d), :] += dk
                dv_ref[_rows(krow, win, d), :] += dv

            _for_tiles(chunk, pat, tile)
        tab = [t[pl.ds(base, ATTN_CHUNK), :] for t in (c_ref, sa_ref, sb_ref)]
        dq_ref[...] = _rope_pair_t(dq_ref[...] * (HD ** -0.5), *tab)

        @pl.when(chunk == nchunk - 1)
        def _():
            dk_ref[...] = _rope_pair_t(dk_ref[...], c_ref[...], sa_ref[...], sb_ref[...])

    return _pcall(
        body, name=name, grid=(D // 128, nchunk),
        in_specs=[whole(0), whole(8), whole(16), table, table, table, chunk_spec, chunk_spec, chunk_spec],
        out_specs=[chunk_spec, whole(0), whole(0)], out_shape=[jax.ShapeDtypeStruct((s, D), F32)] * 3,
        scratch_shapes=[pltpu.VMEM((s, 128), F32), pltpu.VMEM((s, 128), F32), pltpu.VMEM((ATTN_CHUNK, 128), F32)],
        compiler_params=_cparams(("parallel", "arbitrary")),
    )(proj, proj, proj, *tabs, dmix, o, lse)


def _ssd_common(x_ref, b_ref, c_ref, dt_ref, dtt_ref, a_ref, ar_ref, rev):
    ii, jj = _iota((CHUNK, CHUNK), 0), _iota((CHUNK, CHUNK), 1)
    low = jj >= ii if rev else jj <= ii
    x, dtx = x_ref[...], dt_ref[...]
    bm, cm = b_ref[...].astype(BF16), c_ref[...].astype(BF16)
    a = dtx * a_ref[...]
    arow = dtt_ref[0] * ar_ref[0]
    lowb = low.astype(BF16)
    cs = sum(_dot(lowb, p) for p in _parts(a, 3))
    csr = sum(_dot(p, lowb, NT) for p in _parts(arow, 3))
    last = 0 if rev else CHUNK - 1
    tot = cs[last:last + 1, :]
    xdt = x * dtx
    cb = _dot(cm, bm, NT)
    lmats = [jnp.exp(jnp.where(low, cs[:, HD * h:HD * h + 1] - csr[h:h + 1, :], -1e30)) for h in range(2)]
    return dict(x=x, dtx=dtx, bm=bm, cm=cm, a=a, cs=cs, tot=tot, xdt=xdt, cb=cb, lmats=lmats, low=low, last=last)


def _ssd_specs(s, rev_order):
    nck = s // CHUNK
    ci = (lambda c: nck - 1 - c) if rev_order else (lambda c: c)
    tile = lambda off, div: pl.BlockSpec((CHUNK, 128), lambda p, c: (ci(c), off + p // div))
    common = [tile(0, 1), tile(8, 2), tile(12, 2), tile(0, 1),
              pl.BlockSpec((1, 8, CHUNK), lambda p, c: (p, 0, ci(c))),
              pl.BlockSpec((1, 128), lambda p, c: (0, p)),
              pl.BlockSpec((1, 8, 128), lambda p, c: (p, 0, 0))]
    hs = pl.BlockSpec((1, 1, CHUNK, 128), lambda p, c: (p, ci(c), 0, 0))
    return nck, common, tile(0, 1), hs


def ssd_fwd(xbc, dt_exp, dtt, a_exp, a_rows, rev, name):
    s = xbc.shape[0]
    nck, common, tile, hs_spec = _ssd_specs(s, rev)

    def body(x_ref, b_ref, c_ref, dt_ref, dtt_ref, a_ref, ar_ref, y_ref, hs_ref, h_scr):
        @pl.when(pl.program_id(1) == 0)
        def _():
            h_scr[...] = jnp.zeros_like(h_scr)

        v = _ssd_common(x_ref, b_ref, c_ref, dt_ref, dtt_ref, a_ref, ar_ref, rev)
        xdtb = v["xdt"].astype(BF16)
        yd = [_dot((v["cb"] * v["lmats"][h]).astype(BF16), xdtb) for h in range(2)]
        h_in = h_scr[...]
        hs_ref[0, 0] = h_in
        y_off = _dot(v["cm"], h_in.astype(BF16)) * jnp.exp(v["cs"])
        y_ref[...] = jnp.where(_iota((CHUNK, 128), 1) < HD, yd[0], yd[1]) + y_off
        decay = jnp.exp(v["tot"] - v["cs"])
        h_scr[...] = jnp.exp(v["tot"]) * h_in + _dot(v["bm"], (v["xdt"] * decay).astype(BF16), TN)

    return _pcall(
        body, name=name, grid=(8, nck), in_specs=common, out_specs=[tile, hs_spec],
        out_shape=[jax.ShapeDtypeStruct((s, D), F32), jax.ShapeDtypeStruct((8, nck, CHUNK, 128), F32)],
        scratch_shapes=[pltpu.VMEM((CHUNK, 128), F32)], compiler_params=_cparams(("parallel", "arbitrary")),
    )(xbc, xbc, xbc, dt_exp, dtt, a_exp, a_rows)


def ssd_bwd(xbc, dt_exp, dtt, a_exp, a_rows, hs, dy, rev, name):
    s = xbc.shape[0]
    nck, common, tile, hs_spec = _ssd_specs(s, not rev)

    def body(x_ref, b_ref, c_ref, dt_ref, dtt_ref, a_ref, ar_ref, hs_ref, dy_ref,
             dx_ref, ddt_ref, db_ref, dc_ref, dal_ref, dh_scr):
        @pl.when(pl.program_id(1) == 0)
        def _():
            dh_scr[...] = jnp.zeros_like(dh_scr)
            dal_ref[...] = jnp.zeros_like(dal_ref)

        v = _ssd_common(x_ref, b_ref, c_ref, dt_ref, dtt_ref, a_ref, ar_ref, rev)
        bm, cm, cs, tot, xdt = v["bm"], v["cm"], v["cs"], v["tot"], v["xdt"]
        h_in, dh = hs_ref[0, 0], dh_scr[...]
        dyv = dy_ref[...]
        dyb = dyv.astype(BF16)
        etot, decay, ecs = jnp.exp(tot), jnp.exp(tot - cs), jnp.exp(cs)
        xdtb = xdt.astype(BF16)
        xdec = xdt * decay
        dch = (dyv * ecs).astype(BF16)
        hb, dhb = h_in.astype(BF16), dh.astype(BF16)
        y_off = _dot(cm, hb) * ecs
        dc = _dot(dch, hb, NT)
        dh_y = _dot(cm, dch, TN)
        dxdec = _dot(bm, dhb)
        db = _dot(xdec.astype(BF16), dhb, NT)
        state_term = xdec * dxdec
        dtot = _colsum(dh * h_in) * etot + _colsum(state_term)
        head0 = _iota((CHUNK, 128), 1) < HD
        ii, jj = _iota((CHUNK, CHUNK), 0), _iota((CHUNK, CHUNK), 1)
        low_t = jj <= ii if rev else jj >= ii
        not_low_t = (~low_t).astype(BF16)
        dcb, dxd, da_l = 0.0, [], []
        for h in range(2):
            dyh = jnp.where(head0 if h == 0 else ~head0, dyb, jnp.zeros_like(dyb))
            gl = _dot(dyh, xdtb, NT) * v["lmats"][h]
            dcb = dcb + gl
            dxd.append(_dot((v["cb"] * v["lmats"][h]).astype(BF16), dyb, TN))
            w = (gl * v["cb"]).astype(BF16)
            da_l.append(jnp.sum(jnp.where(low_t, _dot(not_low_t, w, NT), 0.0), axis=1, keepdims=True))
        dxd = jnp.where(head0, dxd[0], dxd[1])
        dxdt = dxdec * decay + dxd
        dcbb = dcb.astype(BF16)
        dc_ref[...] = dc + _dot(dcbb, bm)
        db_ref[...] = db + _dot(dcbb, cm, TN)
        dcs = dyv * y_off - state_term + jnp.where(_iota((CHUNK, 128), 0) == v["last"], dtot, 0.0)
        lowb = v["low"].astype(BF16)
        da = sum(_dot(lowb, p, TN) for p in _parts(dcs, 2))
        seg = ((ii < HD) == (jj < HD)).astype(BF16)
        da = sum(_dot(p, seg) for p in _parts(da, 2)) + jnp.where(head0, da_l[0], da_l[1])
        ddt_x = sum(_dot(p, seg) for p in _parts(dxdt * v["x"], 2))
        dx_ref[...] = dxdt * v["dtx"]
        ddt_ref[...] = ddt_x + da * a_ref[...]
        dal_ref[0] += _colsum(da * v["a"])
        dh_scr[...] = etot * dh + dh_y

    return _pcall(
        body, name=name, grid=(8, nck), in_specs=common + [hs_spec, tile],
        out_specs=[tile, tile, tile, tile, pl.BlockSpec((1, 8, 128), lambda p, c: (p, 0, 0))],
        out_shape=[jax.ShapeDtypeStruct((s, D), F32)] * 4 + [jax.ShapeDtypeStruct((8, 8, 128), F32)],
        scratch_shapes=[pltpu.VMEM((CHUNK, 128), F32)], compiler_params=_cparams(("parallel", "arbitrary")),
    )(xbc, xbc, xbc, dt_exp, dtt, a_exp, a_rows, hs, dy)


def _group_norm_stats(g):
    r = [lax.rsqrt(jnp.mean(g[:, 256 * k:256 * k + 256] ** 2, axis=-1, keepdims=True) + EPS) for k in range(4)]
    grp = _iota(g.shape, 1) // 256
    return jnp.where(grp == 0, r[0], jnp.where(grp == 1, r[1], jnp.where(grp == 2, r[2], r[3])))


def _group_mean(t):
    m = [jnp.mean(t[:, 256 * k:256 * k + 256], axis=-1, keepdims=True) for k in range(4)]
    grp = _iota(t.shape, 1) // 256
    return jnp.where(grp == 0, m[0], jnp.where(grp == 1, m[1], jnp.where(grp == 2, m[2], m[3])))


def _mesh_pos():
    return lax.axis_index("x"), lax.axis_index("y"), lax.axis_index("c")


HBM = pl.BlockSpec(memory_space=pltpu.HBM)
SEM = pl.BlockSpec(memory_space=pltpu.SEMAPHORE)
EFFECT = pltpu.SideEffectType.DATAFLOW_SIDE_EFFECTING
SWAP_CHUNKS = 18


def _hbm(t):
    return pltpu.with_memory_space_constraint(t, pltpu.HBM)


def _other_chips(x, y):
    return [(1 - x, y), (x, 1 - y), (1 - x, 1 - y)]


def _peer(x, y, c, m):
    return x ^ (m >> 2), y ^ ((m >> 1) & 1), c ^ (m & 1)


def gather_start(srcs_a, srcs_b):
    srcs = [_hbm(t) for t in list(srcs_a) + list(srcs_b)]
    n, na = len(srcs), len(srcs_a)
    lands = [_hbm(lax.empty((4,) + t.shape, t.dtype)) for t in srcs]

    def body(*refs):
        src, land = refs[:n], refs[n:2 * n]
        sems = refs[2 * n:2 * n + 4]
        x, y, c = _mesh_pos()
        for k in range(n):
            for j, (px, py) in enumerate(_other_chips(x, y)):
                send, recv, idx = (sems[0], sems[1], 3 * k + j) if k < na else (sems[2], sems[3], 3 * (k - na) + j)
                pltpu.make_async_remote_copy(src_ref=src[k], dst_ref=land[k].at[2 * x + y], send_sem=send.at[idx],
                                             recv_sem=recv.at[idx], device_id=(px, py, c), device_id_type=MESH).start()

    sem_a, sem_b = pltpu.SemaphoreType.DMA((3 * na,)), pltpu.SemaphoreType.DMA((3 * (n - na),))
    res = _pcall(
        body, name="gather_start", in_specs=[HBM] * (2 * n), out_specs=[SEM] * 4 + [HBM] * (2 * n),
        out_shape=[sem_a, sem_a, sem_b, sem_b] + [pltpu.HBM(t.shape, t.dtype) for t in srcs + lands],
        input_output_aliases={i: 4 + i for i in range(2 * n)},
        compiler_params=pltpu.CompilerParams(has_side_effects=EFFECT),
    )(*srcs, *lands)
    thru_src, thru_land = res[4:4 + n], res[4 + n:]
    return ((res[0], res[1], thru_src[:na], thru_land[:na]), (res[2], res[3], thru_src[na:], thru_land[na:]))


def gather_wait(group, name, after=None):
    send, recv, srcs, lands = group
    n = len(srcs)

    def body(*refs):
        src, land, send_ref, recv_ref = refs[:n], refs[n:2 * n], refs[2 * n], refs[2 * n + 1]
        x, y, c = _mesh_pos()
        for j, (px, py) in enumerate(_other_chips(x, y)):
            for k in range(n):
                cp = pltpu.make_async_remote_copy(src_ref=src[k], dst_ref=land[k].at[2 * px + py], send_sem=send_ref.at[3 * k + j],
                                                  recv_sem=recv_ref.at[3 * k + j], device_id=(px, py, c), device_id_type=MESH)
                cp.wait_send()
                cp.wait_recv()

    extra = [] if after is None else [after]
    res = _pcall(
        body, name=name, in_specs=[HBM] * (2 * n) + [SEM, SEM] + [pl.BlockSpec(memory_space=pl.ANY)] * len(extra),
        out_specs=[HBM] * (2 * n), out_shape=[pltpu.HBM(t.shape, t.dtype) for t in list(srcs) + list(lands)],
        input_output_aliases={i: i for i in range(2 * n)}, compiler_params=pltpu.CompilerParams(has_side_effects=EFFECT),
    )(*srcs, *lands, send, recv, *extra)
    return res[:n], res[n:]


def scatter_start(pieces, smalls, name):
    srcs = [_hbm(t) for t in list(pieces) + list(smalls)]
    n, npc = len(srcs), len(pieces)
    lands = [_hbm(lax.empty((8,) + (t.shape[2:] if k < npc else t.shape), t.dtype)) for k, t in enumerate(srcs)]

    def body(*refs):
        src, land, send, recv = refs[:n], refs[n:2 * n], refs[2 * n], refs[2 * n + 1]
        token = refs[-1]
        x, y, c = _mesh_pos()
        for m in range(1, 8):
            px, py, pc = _peer(x, y, c, m)
            for k in range(n):
                s_ref = src[k].at[2 * px + py, pc] if k < npc else src[k]
                d_ref = land[k].at[m] if k < npc else land[k].at[4 * x + 2 * y + c]
                pltpu.make_async_remote_copy(src_ref=s_ref, dst_ref=d_ref, send_sem=send.at[7 * k + m - 1], recv_sem=recv.at[7 * k + m - 1],
                                             device_id=(px, py, pc), device_id_type=MESH).start()
        token[...] = jnp.zeros_like(token)

    sem = pltpu.SemaphoreType.DMA((7 * n,))
    res = _pcall(
        body, name=name, in_specs=[HBM] * (2 * n),
        out_specs=[SEM, SEM] + [HBM] * (2 * n) + [pl.BlockSpec(memory_space=pltpu.VMEM)],
        out_shape=[sem, sem] + [pltpu.HBM(t.shape, t.dtype) for t in srcs + lands] + [jax.ShapeDtypeStruct((8, 128), F32)],
        input_output_aliases={i: 2 + i for i in range(2 * n)},
        compiler_params=pltpu.CompilerParams(has_side_effects=EFFECT),
    )(*srcs, *lands)
    return (res[0], res[1], res[2:2 + n], res[2 + n:2 + 2 * n], npc), res[-1]


def scatter_wait(group, name):
    send, recv, srcs, lands, npc = group
    n = len(srcs)

    def body(*refs):
        src, land, send_ref, recv_ref = refs[:n], refs[n:2 * n], refs[2 * n], refs[2 * n + 1]
        x, y, c = _mesh_pos()
        for m in range(1, 8):
            px, py, pc = _peer(x, y, c, m)
            for k in range(n):
                s_ref = src[k].at[0, 0] if k < npc else src[k]
                d_ref = land[k].at[m] if k < npc else land[k].at[4 * px + 2 * py + pc]
                cp = pltpu.make_async_remote_copy(src_ref=s_ref, dst_ref=d_ref, send_sem=send_ref.at[7 * k + m - 1],
                                                  recv_sem=recv_ref.at[7 * k + m - 1], device_id=(px, py, pc), device_id_type=MESH)
                cp.wait_send()
                cp.wait_recv()

    res = _pcall(
        body, name=name, in_specs=[HBM] * (2 * n) + [SEM, SEM], out_specs=[HBM] * (2 * n),
        out_shape=[pltpu.HBM(t.shape, t.dtype) for t in list(srcs) + list(lands)],
        input_output_aliases={i: i for i in range(2 * n)}, compiler_params=pltpu.CompilerParams(has_side_effects=EFFECT),
    )(*srcs, *lands, send, recv)
    return res[:n], res[n:]


def swap_halves(piece):
    any_space = pl.BlockSpec(memory_space=pltpu.VMEM)
    rows = piece.shape[0] // SWAP_CHUNKS
    assert rows * SWAP_CHUNKS == piece.shape[0] and rows % 8 == 0

    def body(p_ref, o_ref, send_sems, recv_sems, local_sem):
        x, y, c = _mesh_pos()
        local = pltpu.make_async_copy(p_ref, o_ref.at[c], local_sem)
        local.start()

        def chunk(k, slot):
            return pltpu.make_async_remote_copy(
                src_ref=p_ref.at[pl.ds(k * rows, rows)], dst_ref=o_ref.at[slot, pl.ds(k * rows, rows)], send_sem=send_sems.at[k],
                recv_sem=recv_sems.at[k], device_id=(x, y, 1 - c), device_id_type=MESH)

        for k in range(SWAP_CHUNKS):
            chunk(k, c).start()
        for k in range(SWAP_CHUNKS):
            chunk(k, 1 - c).wait_recv()
        for k in range(SWAP_CHUNKS):
            chunk(k, c).wait_send()
        local.wait()

    return _pcall(
        body, name="swap_halves", in_specs=[any_space], out_specs=any_space,
        out_shape=jax.ShapeDtypeStruct((2,) + piece.shape, piece.dtype),
        scratch_shapes=[pltpu.SemaphoreType.DMA((SWAP_CHUNKS,)), pltpu.SemaphoreType.DMA((SWAP_CHUNKS,)), pltpu.SemaphoreType.DMA],
        compiler_params=_cparams(),
    )(piece)


def adamw(w, g, m, v, name):
    rows, cols = w.shape
    tm = rows
    for t in (256, 352, 128, 144, 64, 32, 16, 8):
        if rows % t == 0:
            tm = t
            break

    def fn(i, nrow, wv, gv, mv, vv):
        mn = ADAM_B1 * mv + (1.0 - ADAM_B1) * gv
        vn = ADAM_B2 * vv + (1.0 - ADAM_B2) * (gv * gv)
        m_hat = mn / (1.0 - ADAM_B1 ** ADAM_STEP)
        v_hat = vn / (1.0 - ADAM_B2 ** ADAM_STEP)
        delta = -ADAM_LR * (m_hat / (jnp.sqrt(v_hat) + ADAM_EPS) + ADAM_WD * wv)
        return delta, mn, vn

    return ew(fn, name, rows, tm, 1, [(t, "row", cols, 0) for t in (w, g, m, v)], [(cols, F32, cols)] * 3)


BIG = ("w_in", "w_out", "w_up", "w_down")
REST = ("w_out", "w_up", "w_down")
REST_ROWS = (512, 1408, 704)
W_IN_ROWS, W_IN_ROWS_PADDED = 1544, 1552
SMALL = ("norm1_w", "ssm_conv_w", "ssm_conv_b", "a_log_f", "a_log_b", "dt_bias_f", "dt_bias_b", "d_skip",
         "ssm_norm_w", "norm2_w", "ffn_conv_w", "ffn_conv_b", "final_norm_w")
WEIGHTS = ("norm1_w", "w_in", "ssm_conv_w", "ssm_conv_b", "a_log_f", "a_log_b", "dt_bias_f", "dt_bias_b", "d_skip",
           "ssm_norm_w", "w_out", "norm2_w", "w_up", "ffn_conv_w", "ffn_conv_b", "w_down", "final_norm_w")
INPUTS = ("x",) + WEIGHTS + ("loss_target",) + tuple("m_" + n for n in WEIGHTS) + tuple("v_" + n for n in WEIGHTS)


def _flat_rows(parts, width, rows):
    flat = jnp.concatenate([p.reshape(-1) for p in parts])
    return jnp.pad(flat, (0, rows * width - flat.shape[0])).reshape(rows, width)


def _split_flat(flat, shapes):
    out, pos = [], 0
    flat = flat.reshape(-1)
    for shp in shapes:
        n = int(np.prod(shp))
        out.append(flat[pos:pos + n].reshape(shp))
        pos += n
    return out


def _col_shards(t, nshard):
    r, c = t.shape
    return t.reshape(r, nshard, c // nshard).transpose(1, 0, 2).reshape(nshard, -1, D)


def _row_shards(t, nshard):
    r, c = t.shape
    return t.reshape(nshard, -1, D)


def kernel(x, norm1_w, w_in, ssm_conv_w, ssm_conv_b, a_log_f, a_log_b, dt_bias_f, dt_bias_b, d_skip, ssm_norm_w, w_out, norm2_w, w_up, ffn_conv_w, ffn_conv_b, w_down, final_norm_w, loss_target, m_norm1_w, m_w_in, m_ssm_conv_w, m_ssm_conv_b, m_a_log_f, m_a_log_b, m_dt_bias_f, m_dt_bias_b, m_d_skip, m_ssm_norm_w, m_w_out, m_norm2_w, m_w_up, m_ffn_conv_w, m_ffn_conv_b, m_w_down, m_final_norm_w, v_norm1_w, v_w_in, v_ssm_conv_w, v_ssm_conv_b, v_a_log_f, v_a_log_b, v_dt_bias_f, v_dt_bias_b, v_d_skip, v_ssm_norm_w, v_w_out, v_norm2_w, v_w_up, v_ffn_conv_w, v_ffn_conv_b, v_w_down, v_final_norm_w):
    p = dict(zip(INPUTS, (x, norm1_w, w_in, ssm_conv_w, ssm_conv_b, a_log_f, a_log_b, dt_bias_f, dt_bias_b, d_skip, ssm_norm_w, w_out, norm2_w, w_up, ffn_conv_w, ffn_conv_b, w_down, final_norm_w, loss_target, m_norm1_w, m_w_in, m_ssm_conv_w, m_ssm_conv_b, m_a_log_f, m_a_log_b, m_dt_bias_f, m_dt_bias_b, m_d_skip, m_ssm_norm_w, m_w_out, m_norm2_w, m_w_up, m_ffn_conv_w, m_ffn_conv_b, m_w_down, m_final_norm_w, v_norm1_w, v_w_in, v_ssm_conv_w, v_ssm_conv_b, v_a_log_f, v_a_log_b, v_dt_bias_f, v_dt_bias_b, v_d_skip, v_ssm_norm_w, v_w_out, v_norm2_w, v_w_up, v_ffn_conv_w, v_ffn_conv_b, v_w_down, v_final_norm_w)))
    x = p["x"][0]
    tgt = p["loss_target"][0]
    s = x.shape[0]
    chip = 2 * lax.axis_index("x") + lax.axis_index("y")

    own_slot = lambda land, mine, slot: lax.dynamic_update_slice_in_dim(land, mine[None], slot, axis=0)
    src_in = p["w_in"][0].reshape(-1, D).astype(BF16)
    src_rest = jnp.concatenate([p[n][0].reshape(-1, D) for n in REST], axis=0).astype(BF16)
    small_w = _flat_rows([p["ssm_conv_w"][0], p["ffn_conv_w"][0]], 128, 48)
    gather_in, gather_rest = gather_start([src_in, small_w], [src_rest])
    (src_in, small_w), (wg_in, sg) = gather_wait(gather_in, "gather_wait_in")
    w_in = own_slot(wg_in, src_in, chip).reshape(4, D, -1).transpose(1, 0, 2).reshape(D, -1)
    sg = own_slot(sg, small_w, chip)
    o = np.cumsum((0,) + REST_ROWS)
    n_in = w_in.shape[1]
    n_main = 6 * D
    w_main = w_in[:, :n_main]
    w_dt = jnp.pad(w_in[:, n_main:], ((0, 0), (0, 128 - (n_in - n_main))))
    sgf = sg.reshape(4, -1)
    n_sc, n_fc = p["ssm_conv_w"].shape[1], p["ffn_conv_w"].shape[1]
    ssm_cw = sgf[:, :n_sc * 3].reshape(-1, 3).T
    ffn_cw = sgf[:, n_sc * 3:(n_sc + n_fc) * 3].reshape(-1, 3).T
    ssm_cb, ffn_cb = p["ssm_conv_b"], p["ffn_conv_b"]
    n1w, n2w, snw, fnw = p["norm1_w"], p["norm2_w"], p["ssm_norm_w"], p["final_norm_w"].reshape(1, D)

    h1, = ew(lambda i, n, xv, w: _rms_fwd(xv, w), "rms1", s, 256, 1,
             [(x, "row", D, 0), (n1w, "const", D, 0)], [(D, BF16, D)])
    proj = matmul(h1, w_main, "nn", "in_proj")
    proj_dt = matmul(h1, w_dt, "nn", "in_proj_dt")
    tabs = _rope_tables(s)
    attn, lse = attn_fwd_all(proj, tabs, "attn_fwd")

    def conv_silu_fn(i, n, xv, xp, xn, w, b):
        return _silu(w[0:1] * _shift_down(xv, xp, i) + w[1:2] * xv + w[2:3] * _shift_up(xv, xn, i, n) + b)

    xbc_act, = ew(conv_silu_fn, "ssm_conv", s, 256, 2,
                  [(proj, "row", D, 4), (proj, "prev", D, 4), (proj, "next", D, 4),
                   (ssm_cw, "const", D, 0), (ssm_cb, "const", D, 0)], [(2 * D, F32, D)])
    dt_bias = jnp.pad(jnp.concatenate([p["dt_bias_f"], p["dt_bias_b"]], axis=1), ((0, 0), (0, 96)))

    def softplus_fn(i, n, r, b):
        t = r + b
        return jnp.maximum(t, 0.0) + jnp.log(1.0 + jnp.exp(-jnp.abs(t)))

    dt, = ew(softplus_fn, "dt_softplus", s, 512, 1, [(proj_dt, "row", 128, 0), (dt_bias, "const", 128, 0)], [(128, F32, 128)])
    d_exp = jnp.repeat(p["d_skip"], HD, axis=1)
    ssd = []
    for k, (a_log, rev) in enumerate(((p["a_log_f"], False), (p["a_log_b"], True))):
        dt_k = dt[:, 16 * k:16 * k + 16]
        a_head = -jnp.exp(a_log)
        dt_exp = jnp.repeat(dt_k, HD, axis=1)
        dtt = jnp.pad(dt_k.T.reshape(8, 2, s), ((0, 0), (0, 6), (0, 0)))
        a_exp = jnp.repeat(a_head, HD, axis=1)
        a_rows = jnp.broadcast_to(jnp.pad(a_head.reshape(8, 2), ((0, 0), (0, 6)))[:, :, None], (8, 8, 128))
        y_k, hs_k = ssd_fwd(xbc_act, dt_exp, dtt, a_exp, a_rows, rev, "ssd_fwd_%d" % k)
        ssd.append(dict(dt_exp=dt_exp, dtt=dtt, a_exp=a_exp, a_rows=a_rows, y=y_k, hs=hs_k, rev=rev))

    def gate_fn(i, n, yf, yb, xs, z, dsk, w):
        g = (yf + yb + dsk * xs) * _silu(z)
        return g * _group_norm_stats(g) * w

    ssm_out, = ew(gate_fn, "ssm_gate_norm", s, 256, 1,
                  [(ssd[0]["y"], "row", D, 0), (ssd[1]["y"], "row", D, 0), (xbc_act, "row", D, 0), (proj, "row", D, 3),
                   (d_exp, "const", D, 0), (snw, "const", D, 0)], [(D, F32, D)])
    mix = jnp.concatenate([attn, ssm_out], axis=1).astype(BF16)
    (src_rest,), (wg_rest,) = gather_wait(gather_rest, "gather_wait_rest", after=mix)
    wg_rest = own_slot(wg_rest, src_rest, chip)
    w_out = wg_rest[:, o[0]:o[1]].reshape(-1, D)
    w_up = wg_rest[:, o[1]:o[2]].reshape(4, D, -1).transpose(1, 0, 2).reshape(D, -1)
    w_down = wg_rest[:, o[2]:o[3]].reshape(-1, D)
    mix_w = matmul(mix, w_out, "nn", "out_proj")

    def res_rms_fn(i, n, xv, mw, w):
        x1v = xv + mw
        return x1v, _rms_fwd(x1v, w)

    x1, h2 = ew(res_rms_fn, "res_rms2", s, 256, 1, [(x, "row", D, 0), (mix_w, "row", D, 0), (n2w, "const", D, 0)],
                [(D, F32, D), (D, BF16, D)])
    hw = matmul(h2, w_up, "nn", "ffn_up")
    fw = D_FF // 2
    nfb = D_FF // fw
    ffn_conv_ins = [(hw, "row", fw, 0), (hw, "prev", fw, 0), (hw, "next", fw, 0),
                    (hw, "row", fw, nfb), (hw, "prev", fw, nfb), (hw, "next", fw, nfb),
                    (ffn_cw, "const", fw, 0), (ffn_cw, "const", fw, nfb), (ffn_cb, "const", fw, 0), (ffn_cb, "const", fw, nfb)]

    def ffn_conv(i, n, g, gp, gn, u, up_, un, wg_, wu, bg, bu):
        gs = (_shift_down(g, gp, i), g, _shift_up(g, gn, i, n))
        us = (_shift_down(u, up_, i), u, _shift_up(u, un, i, n))
        gate = wg_[0:1] * gs[0] + wg_[1:2] * gs[1] + wg_[2:3] * gs[2] + bg
        upv = wu[0:1] * us[0] + wu[1:2] * us[1] + wu[2:3] * us[2] + bu
        return gate, upv, gs, us

    def glu_fn(i, n, *blocks):
        gate, upv, _, _ = ffn_conv(i, n, *blocks)
        return _silu(gate) * upv

    act, = ew(glu_fn, "ffn_conv_glu", s, 256, nfb, ffn_conv_ins, [(D_FF, BF16, fw)])
    ffn = matmul(act, w_down, "nn", "ffn_down")

    def head_fn(i, n, x1v, fv, tv, w):
        x2 = x1v + fv
        r = lax.rsqrt(jnp.mean(x2 * x2, axis=-1, keepdims=True) + EPS)
        xh = x2 * r
        diff = xh * w - tv
        loss = 0.5 * jnp.sum(jnp.mean(diff * diff, axis=-1, keepdims=True), axis=0, keepdims=True)
        dout = diff * (1.0 / D)
        dxh = dout * w
        dx2 = r * (dxh - xh * jnp.mean(dxh * xh, axis=-1, keepdims=True))
        return dx2, jnp.broadcast_to(loss, (1, 128)), _colsum(dout * xh)

    dx2, loss_acc, g_fnw = ew(head_fn, "loss_head", s, 256, 1,
                              [(x1, "row", D, 0), (ffn, "row", D, 0), (tgt, "row", D, 0), (fnw, "const", D, 0)],
                              [(D, F32, D)], [(128, 128), (D, D)])
    loss = lax.psum(loss_acc[0, 0], ("x", "y", "c"))

    g_w_down = matmul(act, dx2, "tn", "d_w_down")
    dact = matmul(dx2, w_down, "nt", "d_act")

    def glu_bwd_fn(i, n, *blocks):
        gate, upv, gs, us = ffn_conv(i, n, *blocks[:-1])
        da = blocks[-1]
        dg = da * upv * _dsilu(gate)
        du = da * _silu(gate)
        return (dg, du) + tuple(_colsum(dg * t) for t in gs) + tuple(_colsum(du * t) for t in us) + (_colsum(dg), _colsum(du))

    res = ew(glu_bwd_fn, "ffn_glu_bwd", s, 256, nfb, ffn_conv_ins + [(dact, "row", fw, 0)],
             [(D_FF, F32, fw)] * 2, [(D_FF, fw)] * 8)
    du_g, du_u = res[0], res[1]
    g_ffn_cw = jnp.concatenate([jnp.concatenate(res[2:5], axis=0), jnp.concatenate(res[5:8], axis=0)], axis=1).T
    g_ffn_cb = jnp.concatenate([res[8], res[9]], axis=1)

    def conv_t_fn(i, n, dv, dp, dn, w):
        return w[0:1] * _shift_up(dv, dn, i, n) + w[1:2] * dv + w[2:3] * _shift_down(dv, dp, i)

    def conv_t(du, cw, off, width, ncol, name):
        return ew(conv_t_fn, name, s, 256, ncol,
                  [(du, "row", width, 0), (du, "prev", width, 0), (du, "next", width, 0), (cw, "const", width, off)],
                  [(du.shape[1], F32, width)])[0]

    dhw_g = conv_t(du_g, ffn_cw, 0, fw, nfb, "ffn_conv_t_gate")
    dhw_u = conv_t(du_u, ffn_cw, nfb, fw, nfb, "ffn_conv_t_up")
    g_w_up = jnp.concatenate([matmul(h2, dhw_g, "tn", "d_w_up_gate"), matmul(h2, dhw_u, "tn", "d_w_up_up")], axis=1)
    dh2_a = matmul(dhw_g, w_up[:, :D_FF], "nt", "d_h2_gate")
    dh2_b = matmul(dhw_u, w_up[:, D_FF:], "nt", "d_h2_up")

    def res_rms_bwd_fn(i, n, dres, da, db, xin, w):
        dx, dw = _rms_bwd(da + db, xin, w)
        return dres + dx, dw

    dx1, g_n2w = ew(res_rms_bwd_fn, "res_rms2_bwd", s, 256, 1,
                    [(dx2, "row", D, 0), (dh2_a, "row", D, 0), (dh2_b, "row", D, 0), (x1, "row", D, 0), (n2w, "const", D, 0)],
                    [(D, F32, D)], [(D, D)])

    g_w_out = matmul(mix, dx1, "tn", "d_w_out")
    to_pieces = lambda t: t.astype(BF16).reshape(4, -1, 2, 512).transpose(0, 2, 1, 3)
    shards_rest = jnp.concatenate([_row_shards(g_w_out, 4), _col_shards(g_w_up, 4), _row_shards(g_w_down, 4)], axis=1)
    scatter_rest, token = scatter_start([to_pieces(shards_rest)], [], "scatter_start_rest")
    w_out_after = w_out + token[0:1, 0:1].astype(BF16)
    dmix = matmul(dx1, w_out_after, "nt", "d_mix")
    ii, jj = np.arange(D)[:, None] // HD, np.arange(D)[None, :] // HD
    seg = jnp.asarray(ii == jj, BF16)

    def gate_bwd_fn(i, n, dout, yf, yb, xs, z, dsk, w, segm):
        yt = yf + yb + dsk * xs
        sz = _silu(z)
        g = yt * sz
        r = _group_norm_stats(g)
        gh = g * r
        dn = dout * w
        dg = r * (dn - gh * _group_mean(dn * gh))
        dy = dg * sz
        dsk_lane = jnp.broadcast_to(_colsum(dy * xs), (8, D))
        return dy, dg * yt * _dsilu(z), _colsum(dout * gh), sum(_dot(q, segm) for q in _parts(dsk_lane, 2))[0:1]

    dy, dz, g_snw, g_dskip_l = ew(
        gate_bwd_fn, "ssm_gate_norm_bwd", s, 256, 1,
        [(dmix, "row", D, 1), (ssd[0]["y"], "row", D, 0), (ssd[1]["y"], "row", D, 0), (xbc_act, "row", D, 0),
         (proj, "row", D, 3), (d_exp, "const", D, 0), (snw, "const", D, 0), (seg, "const", D, 0)],
        [(D, F32, D)] * 2, [(D, D)] * 2)
    sb = [ssd_bwd(xbc_act, t["dt_exp"], t["dtt"], t["a_exp"], t["a_rows"], t["hs"], dy, t["rev"], "ssd_bwd_%d" % k)
          for k, t in enumerate(ssd)]

    def dxbc_act_fn(i, n, dxf, dxb, dyv, dsk, dbf, dbb, dcf, dcb_):
        db, dc = dbf + dbb, dcf + dcb_
        db = [db[:, 256 * g:256 * g + 128] + db[:, 256 * g + 128:256 * g + 256] for g in range(4)]
        dc = [dc[:, 256 * g:256 * g + 128] + dc[:, 256 * g + 128:256 * g + 256] for g in range(4)]
        return jnp.concatenate([dxf + dxb + dyv * dsk] + db + dc, axis=1)

    dxbc_act, = ew(dxbc_act_fn, "d_xbc_act", s, 256, 1,
                   [(sb[0][0], "row", D, 0), (sb[1][0], "row", D, 0), (dy, "row", D, 0), (d_exp, "const", D, 0),
                    (sb[0][2], "row", D, 0), (sb[1][2], "row", D, 0), (sb[0][3], "row", D, 0), (sb[1][3], "row", D, 0)],
                   [(2 * D, F32, 2 * D)])

    def silu_bwd_fn(i, n, xv, xp, xn, w, b, da):
        xs3 = (_shift_down(xv, xp, i), xv, _shift_up(xv, xn, i, n))
        du = da * _dsilu(w[0:1] * xs3[0] + w[1:2] * xs3[1] + w[2:3] * xs3[2] + b)
        return (du,) + tuple(_colsum(du * t) for t in xs3) + (_colsum(du),)

    res = ew(silu_bwd_fn, "ssm_conv_bwd", s, 256, 2,
             [(proj, "row", D, 4), (proj, "prev", D, 4), (proj, "next", D, 4), (ssm_cw, "const", D, 0),
              (ssm_cb, "const", D, 0), (dxbc_act, "row", D, 0)], [(2 * D, F32, D)], [(2 * D, D)] * 4)
    g_ssm_cw = jnp.concatenate(res[1:4], axis=0).T
    g_ssm_cb = res[4]
    dxbc = conv_t(res[0], ssm_cw, 0, D, 2, "ssm_conv_t")
    ddt = jnp.pad(jnp.concatenate([sb[0][1][:, ::HD], sb[1][1][:, ::HD]], axis=1), ((0, 0), (0, 96)))

    def dt_bwd_fn(i, n, dd, r, b):
        dr = dd * _sigmoid(r + b)
        return dr, _colsum(dr)

    dproj_dt, g_dt_bias = ew(dt_bwd_fn, "dt_softplus_bwd", s, 512, 1,
                             [(ddt, "row", 128, 0), (proj_dt, "row", 128, 0), (dt_bias, "const", 128, 0)],
                             [(128, F32, 128)], [(128, 128)])
    g_a_log = [t[4][:, 0, ::HD].reshape(1, 16) for t in sb]

    dq, dk, dv = attn_bwd_all(proj, tabs, dmix, attn, lse, "attn_bwd")

    dproj = jnp.concatenate([dq, dk, dv, dz, dxbc], axis=1).astype(BF16)
    g_w_in = jnp.concatenate([matmul(h1, dproj, "tn", "d_w_in"), matmul(h1, dproj_dt, "tn", "d_w_in_dt")[:, :n_in - n_main]], axis=1)
    shards_in = jnp.pad(_col_shards(g_w_in, 4), ((0, 0), (0, W_IN_ROWS_PADDED - W_IN_ROWS), (0, 0)))
    scatter_in, token = scatter_start([to_pieces(shards_in)], [], "scatter_start_in")
    w_main_after = w_main + token[0:1, 0:1].astype(BF16)
    dh1_a = matmul(dproj, w_main_after, "nt", "d_h1")
    dh1_b = matmul(dproj_dt, w_dt, "nt", "d_h1_dt")
    grad_x, g_n1w = ew(res_rms_bwd_fn, "rms1_bwd", s, 256, 1,
                       [(dx1, "row", D, 0), (dh1_a, "row", D, 0), (dh1_b, "row", D, 0), (x, "row", D, 0), (n1w, "const", D, 0)],
                       [(D, F32, D)], [(D, D)])

    small_g = {"norm1_w": g_n1w, "ssm_conv_w": g_ssm_cw, "ssm_conv_b": g_ssm_cb, "a_log_f": g_a_log[0], "a_log_b": g_a_log[1],
               "dt_bias_f": g_dt_bias[:, :16], "dt_bias_b": g_dt_bias[:, 16:32], "d_skip": g_dskip_l[:, ::HD],
               "ssm_norm_w": g_snw, "norm2_w": g_n2w, "ffn_conv_w": g_ffn_cw, "ffn_conv_b": g_ffn_cb, "final_norm_w": g_fnw}
    small_shapes = [small_g[n].shape for n in SMALL]
    scatter_small, _ = scatter_start([], [_flat_rows([small_g[n] for n in SMALL], 128, SMALL_ROWS)], "scatter_start_small")
    (sent_rest,), (got_rest,) = scatter_wait(scatter_rest, "scatter_wait_rest")
    (sent_in,), (got_in,) = scatter_wait(scatter_in, "scatter_wait_in")
    (sent_small,), (got_small,) = scatter_wait(scatter_small, "scatter_wait_small")
    core = lax.axis_index("c")

    def sum8_fn(i, n, *v):
        t = v[0].astype(F32)
        for u in v[1:]:
            t = t + u.astype(F32)
        return t

    def sum_pieces(sent, got, tm, ncol, name):
        rows = got.shape[1]
        mine = lax.dynamic_slice(sent, (chip, core, 0, 0), (1, 1, rows, 512)).reshape(rows, 512)
        w = 512 // ncol
        ins = [(mine, "row", w, 0)] + [(got.reshape(8 * rows, 512), "row", w, 0, k * (rows // tm)) for k in range(1, 8)]
        return ew(sum8_fn, name, rows, tm, ncol, ins, [(512, F32, w)])[0]

    rows_rest = int(o[3])
    piece = jnp.concatenate([sum_pieces(sent_rest, got_rest, rows_rest // 4, 1, "sum_pieces_rest"),
                             sum_pieces(sent_in, got_in, W_IN_ROWS_PADDED, 2, "sum_pieces_in")], axis=0)
    got_small = own_slot(got_small, sent_small, 2 * chip + core)
    small_sum, = ew(sum8_fn, "sum_small", SMALL_ROWS, SMALL_ROWS, 1,
                    [(got_small.reshape(8 * SMALL_ROWS, 128), "row", 128, 0, k) for k in range(8)], [(128, F32, 128)])
    g_shard = swap_halves(piece).transpose(1, 0, 2).reshape(-1, D)
    grads = {n: g_shard[o[k]:o[k + 1]].reshape(p[n].shape) for k, n in enumerate(REST)}
    grads["w_in"] = g_shard[rows_rest:rows_rest + W_IN_ROWS].reshape(p["w_in"].shape)
    for n, g in zip(SMALL, _split_flat(small_sum, small_shapes)):
        if n in ("ssm_conv_w", "ffn_conv_w"):
            rows = p[n].shape[1]
            g = lax.dynamic_slice_in_dim(g, chip * rows, rows, axis=0)
        grads[n] = g.reshape(p[n].shape)

    delta, new_m, new_v = {}, {}, {}
    for n in BIG:
        shp = p[n].shape
        r = [t.reshape(shp[1:]) for t in (p[n], grads[n], p["m_" + n], p["v_" + n])]
        delta[n], new_m[n], new_v[n] = [t.reshape(shp) for t in adamw(*r, "adamw_" + n)]
    shapes = [p[n].shape for n in SMALL]
    total = sum(int(np.prod(sh)) for sh in shapes)
    rows = -(-total // 1024) * 8
    packs = [_flat_rows([t[n] for n in SMALL], 128, rows)
             for t in (p, grads, {n: p["m_" + n] for n in SMALL}, {n: p["v_" + n] for n in SMALL})]
    for dst, t in zip((delta, new_m, new_v), adamw(*packs, "adamw_small")):
        for n, u in zip(SMALL, _split_flat(t, shapes)):
            dst[n] = u
    return (loss, grad_x[None], *[grads[n] for n in WEIGHTS], *[delta[n] for n in WEIGHTS],
            *[new_m[n] for n in WEIGHTS], *[new_v[n] for n in WEIGHTS])
```

```python
import numpy as np
import jax
import jax.numpy as jnp
from jax import lax
from jax.experimental import pallas as pl
from jax.experimental.pallas import tpu as pltpu

F32, BF16 = jnp.float32, jnp.bfloat16
MESH = pl.DeviceIdType.MESH
V7X_VMEM_LIMIT = 56 * 1024 * 1024

D = 1024
HD = 64
EPS = 1e-6
CHUNK = 128
D_FF = 2816
ROPE_DIM = 16
ROPE_THETA = 500000.0
PATTERN_DILATIONS = (1, 4, 16)
BAND = 64
SMALL_ROWS = 280
ADAM_LR, ADAM_B1, ADAM_B2, ADAM_EPS, ADAM_WD, ADAM_STEP = 0.001, 0.9, 0.999, 1e-08, 0.01, 10

NN = (((1,), (0,)), ((), ()))
NT = (((1,), (1,)), ((), ()))
TN = (((0,), (0,)), ((), ()))


def _pcall(body, **kw):
    return pl.pallas_call(body, **kw)


def _cparams(sem=None):
    return pltpu.CompilerParams(dimension_semantics=sem, vmem_limit_bytes=V7X_VMEM_LIMIT)


def _dot(a, b, dims=NN):
    return lax.dot_general(a, b, dims, preferred_element_type=F32)


def _pick(n, cap):
    if n <= cap:
        return n
    best = 0
    for t in range(128, cap + 1, 128):
        if n % t == 0:
            best = t
    assert best, (n, cap)
    return best


def _iota(shape, dim):
    return lax.broadcasted_iota(jnp.int32, shape, dim)


def _parts(x, n):
    out, r = [], x
    for _ in range(n):
        h = r.astype(BF16)
        out.append(h)
        r = r - h.astype(F32)
    return out


def _sigmoid(x):
    return 1.0 / (1.0 + jnp.exp(-x))


def _silu(x):
    return x * _sigmoid(x)


def _dsilu(x):
    s = _sigmoid(x)
    return s * (1.0 + x * (1.0 - s))


def matmul(a, b, mode, name, out_dtype=F32):
    if mode == "nn":
        (m, k), (_, n) = a.shape, b.shape
    elif mode == "nt":
        (m, k), (n, _) = a.shape, b.shape
    else:
        (k, m), (_, n) = a.shape, b.shape
    tm, tn, tk = _pick(m, 1408), _pick(n, 1408), _pick(k, 1408)
    nk = k // tk
    dims = {"nn": NN, "nt": NT, "tn": TN}[mode]
    a_spec = pl.BlockSpec((tk, tm), lambda i, j, kk: (kk, i)) if mode == "tn" else pl.BlockSpec((tm, tk), lambda i, j, kk: (i, kk))
    b_spec = pl.BlockSpec((tn, tk), lambda i, j, kk: (j, kk)) if mode == "nt" else pl.BlockSpec((tk, tn), lambda i, j, kk: (kk, j))

    def body(a_ref, b_ref, o_ref, *acc):
        part = _dot(a_ref[...].astype(BF16), b_ref[...].astype(BF16), dims)
        if nk == 1:
            o_ref[...] = part.astype(o_ref.dtype)
            return
        acc_ref, kk = acc[0], pl.program_id(2)

        @pl.when(kk == 0)
        def _():
            acc_ref[...] = part

        @pl.when((kk > 0) & (kk < nk - 1))
        def _():
            acc_ref[...] += part

        @pl.when(kk == nk - 1)
        def _():
            o_ref[...] = (acc_ref[...] + part).astype(o_ref.dtype)

    return _pcall(
        body, name=name, grid=(m // tm, n // tn, nk), in_specs=[a_spec, b_spec],
        out_specs=pl.BlockSpec((tm, tn), lambda i, j, kk: (i, j)),
        out_shape=jax.ShapeDtypeStruct((m, n), out_dtype),
        scratch_shapes=[pltpu.VMEM((tm, tn), F32)] if nk > 1 else [],
        compiler_params=_cparams(("parallel", "parallel", "arbitrary")),
    )(a, b)


def ew(fn, name, rows, tm, ncol, ins, outs, accs=()):
    nrow = rows // tm
    r8 = tm // 8
    in_specs, arrays = [], []
    for ent in ins:
        arr, kind, w, off = ent[:4]
        roff = ent[4] if len(ent) > 4 else 0
        if kind == "row":
            spec = pl.BlockSpec((tm, w), lambda j, i, off=off, roff=roff: (i + roff, j + off))
        elif kind == "const":
            spec = pl.BlockSpec((arr.shape[0], w), lambda j, i, off=off: (0, j + off))
        elif kind == "prev":
            spec = pl.BlockSpec((8, w), lambda j, i, off=off: (jnp.maximum(i * r8 - 1, 0), j + off))
        else:
            spec = pl.BlockSpec((8, w), lambda j, i, off=off: (jnp.minimum((i + 1) * r8, rows // 8 - 1), j + off))
        in_specs.append(spec)
        arrays.append(arr)
    out_specs = [pl.BlockSpec((tm, w), lambda j, i: (i, j)) for (_, _, w) in outs]
    out_shape = [jax.ShapeDtypeStruct((rows, c), dt) for (c, dt, _) in outs]
    out_specs += [pl.BlockSpec((1, w), lambda j, i: (0, j)) for (_, w) in accs]
    out_shape += [jax.ShapeDtypeStruct((1, c), F32) for (c, _) in accs]
    nin, nout = len(ins), len(outs)

    def body(*refs):
        i = pl.program_id(1)
        res = fn(i, nrow, *[r[...] for r in refs[:nin]])
        if not isinstance(res, (tuple, list)):
            res = (res,)
        for r, v in zip(refs[nin:nin + nout], res[:nout]):
            r[...] = v.astype(r.dtype)
        if accs:
            acc_refs = refs[nin + nout:]

            @pl.when(i == 0)
            def _():
                for r in acc_refs:
                    r[...] = jnp.zeros_like(r)

            for r, v in zip(acc_refs, res[nout:]):
                r[...] += v

    res = _pcall(
        body, name=name, grid=(ncol, nrow), in_specs=in_specs, out_specs=out_specs, out_shape=out_shape,
        compiler_params=_cparams(("parallel", "arbitrary")),
    )(*arrays)
    return res


def _shift_down(x, prev8, i):
    first = jnp.where(i == 0, 0.0, prev8[7:8, :])
    return jnp.where(_iota(x.shape, 0) == 0, first, pltpu.roll(x, 1, 0))


def _shift_up(x, next8, i, nrow):
    last = jnp.where(i == nrow - 1, 0.0, next8[0:1, :])
    return jnp.where(_iota(x.shape, 0) == x.shape[0] - 1, last, pltpu.roll(x, x.shape[0] - 1, 0))


def _colsum(x):
    return jnp.sum(x, axis=0, keepdims=True)


def _rms_fwd(x, w):
    r = lax.rsqrt(jnp.mean(x * x, axis=-1, keepdims=True) + EPS)
    return x * r * w


def _rms_bwd(dy, x, w):
    r = lax.rsqrt(jnp.mean(x * x, axis=-1, keepdims=True) + EPS)
    xh = x * r
    dxh = dy * w
    dx = r * (dxh - xh * jnp.mean(dxh * xh, axis=-1, keepdims=True))
    return dx, _colsum(dy * xh)


def _rope_tables(s):
    half = ROPE_DIM // 2
    inv_freq = jnp.power(ROPE_THETA, -jnp.arange(half, dtype=F32) * 2.0 / ROPE_DIM)
    ang = jnp.arange(s, dtype=F32)[:, None] * inv_freq[None, :]
    cos, sin = jnp.cos(ang), jnp.sin(ang)
    one, zero = jnp.ones((s, HD - ROPE_DIM), F32), jnp.zeros((s, HD - ROPE_DIM), F32)
    z8 = jnp.zeros((s, half), F32)
    c = jnp.concatenate([cos, cos, one], axis=1)
    sa = jnp.concatenate([-sin, z8, zero], axis=1)
    sb = jnp.concatenate([z8, sin, zero], axis=1)
    return [jnp.tile(t, (1, 2)) for t in (c, sa, sb)]


ATTN_CHUNK = 1024


def _attn_plan(s):
    plan = []
    for d in PATTERN_DILATIONS:
        per_res = ATTN_CHUNK // d
        tq = min(128, per_res)
        plan.append((d, tq, min(s // d, tq + 2 * BAND), per_res // tq, s // d))
    return plan


def _rows(start, size, d):
    return pl.ds(start, size) if d == 1 else pl.ds(start, size, stride=d)


def _for_tiles(chunk, pat, fn):
    d, tq, win, nblk, seq_len = pat
    for b in range(nblk):
        t0 = chunk * (ATTN_CHUNK // d) + b * tq
        kloc = jnp.clip(t0 - BAND, 0, seq_len - win)
        valid = jnp.abs(kloc + _iota((tq, win), 1) - (t0 + _iota((tq, win), 0))) <= BAND
        if d == 1:
            fn(b * tq, pl.multiple_of(kloc, BAND), valid)
        else:
            def step(r, carry, qoff=d * b * tq, koff=d * kloc, valid=valid):
                fn(qoff + r, koff + r, valid)
                return carry
            lax.fori_loop(0, d, step, 0, unroll=min(d, 4))


def _rope_pair(x, c, sa, sb):
    n = x.shape[1]
    return x * c + pltpu.roll(x, n - 8, 1) * sa + pltpu.roll(x, 8, 1) * sb


def _rope_pair_t(dy, c, sa, sb):
    n = dy.shape[1]
    return dy * c + pltpu.roll(dy * sa, 8, 1) + pltpu.roll(dy * sb, n - 8, 1)


def _attn_specs(s):
    whole = lambda off: pl.BlockSpec((s, 128), lambda p, c: (0, off + p))
    table = pl.BlockSpec((s, 128), lambda p, c: (0, 0))
    chunk = pl.BlockSpec((ATTN_CHUNK, 128), lambda p, c: (c, p))
    return whole, table, chunk


def attn_fwd_all(proj, tabs, name):
    s = proj.shape[0]
    plan = _attn_plan(s)
    whole, table, chunk_spec = _attn_specs(s)

    def body(q_ref, k_ref, v_ref, c_ref, sa_ref, sb_ref, o_ref, lse_ref, qs, ks, acc_s, m_s, l_s):
        chunk = pl.program_id(1)

        @pl.when(chunk == 0)
        def _():
            qs[...] = _rope_pair(q_ref[...], c_ref[...], sa_ref[...], sb_ref[...]) * (HD ** -0.5)
            ks[...] = _rope_pair(k_ref[...], c_ref[...], sa_ref[...], sb_ref[...])

        base = pl.multiple_of(chunk * ATTN_CHUNK, ATTN_CHUNK)
        for pi, pat in enumerate(plan):
            d, tq, win = pat[:3]
            head0 = _iota((tq, 128), 1) < HD

            def tile(qrow, krow, valid, pi=pi, d=d, tq=tq, win=win, head0=head0):
                qv = qs[_rows(base + qrow, tq, d), :].astype(BF16)
                kw = ks[_rows(krow, win, d), :].astype(BF16)
                vw = v_ref[_rows(krow, win, d), :].astype(BF16)
                v_ones = jnp.concatenate([vw, jnp.ones_like(vw)], axis=1)
                acc, m, den = [], [], []
                for h in range(2):
                    qh = jnp.where(head0 if h == 0 else ~head0, qv, jnp.zeros_like(qv))
                    sc = jnp.where(valid, _dot(qh, kw, NT), -1e30)
                    mh = jnp.max(sc, axis=1, keepdims=True)
                    pv = _dot(jnp.exp(sc - mh).astype(BF16), v_ones)
                    m.append(mh)
                    den.append(pv[:, 128:])
                    acc.append(pv[:, :128])
                acc_s[pi, _rows(qrow, tq, d), :] = jnp.where(head0, acc[0], acc[1])
                m_s[pi, _rows(qrow, tq, d), :] = jnp.where(head0, m[0], m[1])
                l_s[pi, _rows(qrow, tq, d), :] = jnp.where(head0, den[0], den[1])

            _for_tiles(chunk, pat, tile)
        m_all = jnp.maximum(jnp.maximum(m_s[0], m_s[1]), m_s[2])
        e = [jnp.exp(m_s[k] - m_all) for k in range(3)]
        den = e[0] * l_s[0] + e[1] * l_s[1] + e[2] * l_s[2]
        o_ref[...] = (e[0] * acc_s[0] + e[1] * acc_s[1] + e[2] * acc_s[2]) / den
        lse_ref[...] = m_all + jnp.log(den)

    stat = pltpu.VMEM((3, ATTN_CHUNK, 128), F32)
    return _pcall(
        body, name=name, grid=(D // 128, s // ATTN_CHUNK),
        in_specs=[whole(0), whole(8), whole(16), table, table, table], out_specs=[chunk_spec, chunk_spec],
        out_shape=[jax.ShapeDtypeStruct((s, D), F32)] * 2,
        scratch_shapes=[pltpu.VMEM((s, 128), F32), pltpu.VMEM((s, 128), F32), stat, stat, stat],
        compiler_params=_cparams(("parallel", "arbitrary")),
    )(proj, proj, proj, *tabs)


def attn_bwd_all(proj, tabs, dmix, o, lse, name):
    s = proj.shape[0]
    plan = _attn_plan(s)
    whole, table, chunk_spec = _attn_specs(s)
    nchunk = s // ATTN_CHUNK

    def body(q_ref, k_ref, v_ref, c_ref, sa_ref, sb_ref, do_ref, o_ref, lse_ref, dq_ref, dk_ref, dv_ref, qs, ks, lse_s, delta_s):
        chunk = pl.program_id(1)

        @pl.when(chunk == 0)
        def _():
            qs[...] = _rope_pair(q_ref[...], c_ref[...], sa_ref[...], sb_ref[...]) * (HD ** -0.5)
            ks[...] = _rope_pair(k_ref[...], c_ref[...], sa_ref[...], sb_ref[...])
            dk_ref[...] = jnp.zeros_like(dk_ref)
            dv_ref[...] = jnp.zeros_like(dv_ref)

        base = pl.multiple_of(chunk * ATTN_CHUNK, ATTN_CHUNK)
        prod = do_ref[...] * o_ref[...]
        first = _iota(prod.shape, 1) < HD
        delta = jnp.where(first, jnp.sum(jnp.where(first, prod, 0.0), axis=1, keepdims=True),
                          jnp.sum(jnp.where(first, 0.0, prod), axis=1, keepdims=True))
        lane = _iota(prod.shape, 1) % HD

        def as_lanes(t):
            parts = [u.astype(F32) for u in _parts(pltpu.roll(t, HD, 1), 3)]
            return jnp.where(lane == 0, parts[0], jnp.where(lane == 1, parts[1], jnp.where(lane == 2, parts[2], 0.0)))

        lse_s[...] = as_lanes(lse_ref[...])
        delta_s[...] = as_lanes(delta)
        for pi, pat in enumerate(plan):
            d, tq, win = pat[:3]
            head0 = _iota((tq, 128), 1) < HD

            def tile(qrow, krow, valid, pi=pi, d=d, tq=tq, win=win, head0=head0):
                qv = qs[_rows(base + qrow, tq, d), :].astype(BF16)
                kw = ks[_rows(krow, win, d), :].astype(BF16)
                vw = v_ref[_rows(krow, win, d), :].astype(BF16)
                dob = do_ref[_rows(qrow, tq, d), :].astype(BF16)
                lse_l = lse_s[_rows(qrow, tq, d), :].astype(BF16)
                delta_l = delta_s[_rows(qrow, tq, d), :].astype(BF16)
                klane = _iota((win, 128), 1)
                dq, dk, dv = [], 0.0, 0.0
                for h in range(2):
                    hm = head0 if h == 0 else ~head0
                    qh = jnp.where(hm, qv, jnp.zeros_like(qv))
                    doh = jnp.where(hm, dob, jnp.zeros_like(dob))
                    minus = (klane >= HD * (1 - h)) & (klane < HD * (1 - h) + 3)
                    neg = jnp.full((win, 128), -1.0, BF16)
                    p = jnp.where(valid, jnp.exp(_dot(jnp.where(hm, qv, lse_l), jnp.where(minus, neg, kw), NT)), 0.0)
                    ds = (p * _dot(jnp.where(hm, dob, delta_l), jnp.where(minus, neg, vw), NT)).astype(BF16)
                    dq.append(_dot(ds, kw))
                    dk = dk + _dot(ds, qh, TN)
                    dv = dv + _dot(p.astype(BF16), doh, TN)
                dqv = jnp.where(head0, dq[0], dq[1])
                if pi == 0:
                    dq_ref[_rows(qrow, tq, d), :] = dqv
                else:
                    dq_ref[_rows(qrow, tq, d), :] += dqv
                dk_ref[_rows(krow, win, d), :] += dk
                dv_ref[_rows(krow, win, d), :] += dv

            _for_tiles(chunk, pat, tile)
        tab = [t[pl.ds(base, ATTN_CHUNK), :] for t in (c_ref, sa_ref, sb_ref)]
        dq_ref[...] = _rope_pair_t(dq_ref[...] * (HD ** -0.5), *tab)

        @pl.when(chunk == nchunk - 1)
        def _():
            dk_ref[...] = _rope_pair_t(dk_ref[...], c_ref[...], sa_ref[...], sb_ref[...])

    return _pcall(
        body, name=name, grid=(D // 128, nchunk),
        in_specs=[whole(0), whole(8), whole(16), table, table, table, chunk_spec, chunk_spec, chunk_spec],
        out_specs=[chunk_spec, whole(0), whole(0)], out_shape=[jax.ShapeDtypeStruct((s, D), F32)] * 3,
        scratch_shapes=[pltpu.VMEM((s, 128), F32), pltpu.VMEM((s, 128), F32)] + [pltpu.VMEM((ATTN_CHUNK, 128), F32)] * 2,
        compiler_params=_cparams(("parallel", "arbitrary")),
    )(proj, proj, proj, *tabs, dmix, o, lse)


def _ssd_common(x_ref, b_ref, c_ref, dt_ref, dtt_ref, a_ref, ar_ref, rev):
    ii, jj = _iota((CHUNK, CHUNK), 0), _iota((CHUNK, CHUNK), 1)
    low = jj >= ii if rev else jj <= ii
    x, dtx = x_ref[...], dt_ref[...]
    bm, cm = b_ref[...].astype(BF16), c_ref[...].astype(BF16)
    a = dtx * a_ref[...]
    arow = dtt_ref[0] * ar_ref[0]
    lowb = low.astype(BF16)
    cs = sum(_dot(lowb, p) for p in _parts(a, 3))
    csr = sum(_dot(p, lowb, NT) for p in _parts(arow, 3))
    last = 0 if rev else CHUNK - 1
    tot = cs[last:last + 1, :]
    xdt = x * dtx
    cb = _dot(cm, bm, NT)
    lmats = [jnp.exp(jnp.where(low, cs[:, HD * h:HD * h + 1] - csr[h:h + 1, :], -1e30)) for h in range(2)]
    return dict(x=x, dtx=dtx, bm=bm, cm=cm, a=a, cs=cs, tot=tot, xdt=xdt, cb=cb, lmats=lmats, low=low, last=last)


def _ssd_specs(s, rev_order):
    nck = s // CHUNK
    ci = (lambda c: nck - 1 - c) if rev_order else (lambda c: c)
    tile = lambda off, div: pl.BlockSpec((CHUNK, 128), lambda p, c: (ci(c), off + p // div))
    common = [tile(0, 1), tile(8, 2), tile(12, 2), tile(0, 1),
              pl.BlockSpec((1, 8, CHUNK), lambda p, c: (p, 0, ci(c))),
              pl.BlockSpec((1, 128), lambda p, c: (0, p)),
              pl.BlockSpec((1, 8, 128), lambda p, c: (p, 0, 0))]
    hs = pl.BlockSpec((1, 1, CHUNK, 128), lambda p, c: (p, ci(c), 0, 0))
    return nck, common, tile(0, 1), hs


def _ssd_args(xbc, t):
    return [xbc, xbc, xbc, t["dt_exp"], t["dtt"], t["a_exp"], t["a_rows"]]


def ssd_fwd(xbc, dirs, name):
    s = xbc.shape[0]
    nd = len(dirs)
    specs = [_ssd_specs(s, t["rev"]) for t in dirs]
    nck = specs[0][0]

    def one(rev, x_ref, b_ref, c_ref, dt_ref, dtt_ref, a_ref, ar_ref, y_ref, hs_ref, h_scr):
        @pl.when(pl.program_id(1) == 0)
        def _():
            h_scr[...] = jnp.zeros_like(h_scr)

        v = _ssd_common(x_ref, b_ref, c_ref, dt_ref, dtt_ref, a_ref, ar_ref, rev)
        xdtb = v["xdt"].astype(BF16)
        yd = [_dot((v["cb"] * v["lmats"][h]).astype(BF16), xdtb) for h in range(2)]
        h_in = h_scr[...]
        hs_ref[0, 0] = h_in
        y_off = _dot(v["cm"], h_in.astype(BF16)) * jnp.exp(v["cs"])
        y_ref[...] = jnp.where(_iota((CHUNK, 128), 1) < HD, yd[0], yd[1]) + y_off
        decay = jnp.exp(v["tot"] - v["cs"])
        h_scr[...] = jnp.exp(v["tot"]) * h_in + _dot(v["bm"], (v["xdt"] * decay).astype(BF16), TN)

    def body(*refs):
        for k, t in enumerate(dirs):
            one(t["rev"], *refs[7 * k:7 * k + 7], *refs[7 * nd + 2 * k:7 * nd + 2 * k + 2], refs[9 * nd + k])

    res = _pcall(
        body, name=name, grid=(8, nck), in_specs=[sp for t in specs for sp in t[1]],
        out_specs=[sp for t in specs for sp in (t[2], t[3])],
        out_shape=[jax.ShapeDtypeStruct((s, D), F32), jax.ShapeDtypeStruct((8, nck, CHUNK, 128), F32)] * nd,
        scratch_shapes=[pltpu.VMEM((CHUNK, 128), F32)] * nd, compiler_params=_cparams(("parallel", "arbitrary")),
    )(*[a for t in dirs for a in _ssd_args(xbc, t)])
    return [(res[2 * k], res[2 * k + 1]) for k in range(nd)]


def ssd_bwd(xbc, dirs, dy, name):
    s = xbc.shape[0]
    nd = len(dirs)
    specs = [_ssd_specs(s, not t["rev"]) for t in dirs]
    nck = specs[0][0]

    def one(rev, x_ref, b_ref, c_ref, dt_ref, dtt_ref, a_ref, ar_ref, hs_ref, dy_ref,
            dx_ref, ddt_ref, db_ref, dc_ref, dal_ref, dh_scr):
        @pl.when(pl.program_id(1) == 0)
        def _():
            dh_scr[...] = jnp.zeros_like(dh_scr)
            dal_ref[...] = jnp.zeros_like(dal_ref)

        v = _ssd_common(x_ref, b_ref, c_ref, dt_ref, dtt_ref, a_ref, ar_ref, rev)
        bm, cm, cs, tot, xdt = v["bm"], v["cm"], v["cs"], v["tot"], v["xdt"]
        h_in, dh = hs_ref[0, 0], dh_scr[...]
        dyv = dy_ref[...]
        dyb = dyv.astype(BF16)
        etot, decay, ecs = jnp.exp(tot), jnp.exp(tot - cs), jnp.exp(cs)
        xdtb = xdt.astype(BF16)
        xdec = xdt * decay
        dch = (dyv * ecs).astype(BF16)
        hb, dhb = h_in.astype(BF16), dh.astype(BF16)
        y_off = _dot(cm, hb) * ecs
        dc = _dot(dch, hb, NT)
        dh_y = _dot(cm, dch, TN)
        dxdec = _dot(bm, dhb)
        db = _dot(xdec.astype(BF16), dhb, NT)
        state_term = xdec * dxdec
        dtot = _colsum(dh * h_in) * etot + _colsum(state_term)
        head0 = _iota((CHUNK, 128), 1) < HD
        ii, jj = _iota((CHUNK, CHUNK), 0), _iota((CHUNK, CHUNK), 1)
        low_t = jj <= ii if rev else jj >= ii
        not_low_t = (~low_t).astype(BF16)
        dcb, dxd, da_l = 0.0, [], []
        for h in range(2):
            dyh = jnp.where(head0 if h == 0 else ~head0, dyb, jnp.zeros_like(dyb))
            gl = _dot(dyh, xdtb, NT) * v["lmats"][h]
            dcb = dcb + gl
            dxd.append(_dot((v["cb"] * v["lmats"][h]).astype(BF16), dyb, TN))
            w = (gl * v["cb"]).astype(BF16)
            da_l.append(jnp.sum(jnp.where(low_t, _dot(not_low_t, w, NT), 0.0), axis=1, keepdims=True))
        dxd = jnp.where(head0, dxd[0], dxd[1])
        dxdt = dxdec * decay + dxd
        dcbb = dcb.astype(BF16)
        dc_ref[...] = dc + _dot(dcbb, bm)
        db_ref[...] = db + _dot(dcbb, cm, TN)
        dcs = dyv * y_off - state_term + jnp.where(_iota((CHUNK, 128), 0) == v["last"], dtot, 0.0)
        lowb = v["low"].astype(BF16)
        da = sum(_dot(lowb, p, TN) for p in _parts(dcs, 2))
        seg = ((ii < HD) == (jj < HD)).astype(BF16)
        da = sum(_dot(p, seg) for p in _parts(da, 2)) + jnp.where(head0, da_l[0], da_l[1])
        ddt_x = sum(_dot(p, seg) for p in _parts(dxdt * v["x"], 2))
        dx_ref[...] = dxdt * v["dtx"]
        ddt_ref[...] = ddt_x + da * a_ref[...]
        dal_ref[0] += _colsum(da * v["a"])
        dh_scr[...] = etot * dh + dh_y

    def body(*refs):
        for k, t in enumerate(dirs):
            one(t["rev"], *refs[9 * k:9 * k + 9], *refs[9 * nd + 5 * k:9 * nd + 5 * k + 5], refs[14 * nd + k])

    acc_spec = pl.BlockSpec((1, 8, 128), lambda p, c: (p, 0, 0))
    res = _pcall(
        body, name=name, grid=(8, nck), in_specs=[sp for t in specs for sp in t[1] + [t[3], t[2]]],
        out_specs=[sp for t in specs for sp in [t[2]] * 4 + [acc_spec]],
        out_shape=([jax.ShapeDtypeStruct((s, D), F32)] * 4 + [jax.ShapeDtypeStruct((8, 8, 128), F32)]) * nd,
        scratch_shapes=[pltpu.VMEM((CHUNK, 128), F32)] * nd, compiler_params=_cparams(("parallel", "arbitrary")),
    )(*[a for t in dirs for a in _ssd_args(xbc, t) + [t["hs"], dy]])
    return [res[5 * k:5 * k + 5] for k in range(nd)]


def _group_norm_stats(g):
    r = [lax.rsqrt(jnp.mean(g[:, 256 * k:256 * k + 256] ** 2, axis=-1, keepdims=True) + EPS) for k in range(4)]
    grp = _iota(g.shape, 1) // 256
    return jnp.where(grp == 0, r[0], jnp.where(grp == 1, r[1], jnp.where(grp == 2, r[2], r[3])))


def _group_mean(t):
    m = [jnp.mean(t[:, 256 * k:256 * k + 256], axis=-1, keepdims=True) for k in range(4)]
    grp = _iota(t.shape, 1) // 256
    return jnp.where(grp == 0, m[0], jnp.where(grp == 1, m[1], jnp.where(grp == 2, m[2], m[3])))


def _mesh_pos():
    return lax.axis_index("x"), lax.axis_index("y"), lax.axis_index("c")


HBM = pl.BlockSpec(memory_space=pltpu.HBM)
SEM = pl.BlockSpec(memory_space=pltpu.SEMAPHORE)
EFFECT = pltpu.SideEffectType.DATAFLOW_SIDE_EFFECTING


def _hbm(t):
    return pltpu.with_memory_space_constraint(t, pltpu.HBM)


def _other_chips(x, y):
    return [(1 - x, y), (x, 1 - y), (1 - x, 1 - y)]


def _peer(x, y, c, m):
    return x ^ (m >> 2), y ^ ((m >> 1) & 1), c ^ (m & 1)


def gather_start(srcs_a, srcs_b):
    srcs = [_hbm(t) for t in list(srcs_a) + list(srcs_b)]
    n, na = len(srcs), len(srcs_a)
    lands = [_hbm(lax.empty((4,) + t.shape, t.dtype)) for t in srcs]

    def body(*refs):
        src, land = refs[:n], refs[n:2 * n]
        sems = refs[2 * n:2 * n + 4]
        x, y, c = _mesh_pos()
        for k in range(n):
            for j, (px, py) in enumerate(_other_chips(x, y)):
                send, recv, idx = (sems[0], sems[1], 3 * k + j) if k < na else (sems[2], sems[3], 3 * (k - na) + j)
                pltpu.make_async_remote_copy(src_ref=src[k], dst_ref=land[k].at[2 * x + y], send_sem=send.at[idx],
                                             recv_sem=recv.at[idx], device_id=(px, py, c), device_id_type=MESH).start()

    sem_a, sem_b = pltpu.SemaphoreType.DMA((3 * na,)), pltpu.SemaphoreType.DMA((3 * (n - na),))
    res = _pcall(
        body, name="gather_start", in_specs=[HBM] * (2 * n), out_specs=[SEM] * 4 + [HBM] * (2 * n),
        out_shape=[sem_a, sem_a, sem_b, sem_b] + [pltpu.HBM(t.shape, t.dtype) for t in srcs + lands],
        input_output_aliases={i: 4 + i for i in range(2 * n)},
        compiler_params=pltpu.CompilerParams(has_side_effects=EFFECT),
    )(*srcs, *lands)
    thru_src, thru_land = res[4:4 + n], res[4 + n:]
    return ((res[0], res[1], thru_src[:na], thru_land[:na]), (res[2], res[3], thru_src[na:], thru_land[na:]))


def gather_wait(group, name, after=None):
    send, recv, srcs, lands = group
    n = len(srcs)

    def body(*refs):
        src, land, send_ref, recv_ref = refs[:n], refs[n:2 * n], refs[2 * n], refs[2 * n + 1]
        x, y, c = _mesh_pos()
        for j, (px, py) in enumerate(_other_chips(x, y)):
            for k in range(n):
                cp = pltpu.make_async_remote_copy(src_ref=src[k], dst_ref=land[k].at[2 * px + py], send_sem=send_ref.at[3 * k + j],
                                                  recv_sem=recv_ref.at[3 * k + j], device_id=(px, py, c), device_id_type=MESH)
                cp.wait_send()
                cp.wait_recv()

    extra = [] if after is None else [after]
    res = _pcall(
        body, name=name, in_specs=[HBM] * (2 * n) + [SEM, SEM] + [pl.BlockSpec(memory_space=pl.ANY)] * len(extra),
        out_specs=[HBM] * (2 * n), out_shape=[pltpu.HBM(t.shape, t.dtype) for t in list(srcs) + list(lands)],
        input_output_aliases={i: i for i in range(2 * n)}, compiler_params=pltpu.CompilerParams(has_side_effects=EFFECT),
    )(*srcs, *lands, send, recv, *extra)
    return res[:n], res[n:]


def scatter_start(pieces, smalls, name):
    srcs = [_hbm(t) for t in list(pieces) + list(smalls)]
    n, npc = len(srcs), len(pieces)
    lands = [_hbm(lax.empty((8,) + (t.shape[2:] if k < npc else t.shape), t.dtype)) for k, t in enumerate(srcs)]

    def body(*refs):
        src, land, send, recv = refs[:n], refs[n:2 * n], refs[2 * n], refs[2 * n + 1]
        token = refs[-1]
        x, y, c = _mesh_pos()
        for m in range(1, 8):
            px, py, pc = _peer(x, y, c, m)
            for k in range(n):
                s_ref = src[k].at[2 * px + py, pc] if k < npc else src[k]
                d_ref = land[k].at[m] if k < npc else land[k].at[4 * x + 2 * y + c]
                pltpu.make_async_remote_copy(src_ref=s_ref, dst_ref=d_ref, send_sem=send.at[7 * k + m - 1], recv_sem=recv.at[7 * k + m - 1],
                                             device_id=(px, py, pc), device_id_type=MESH).start()
        token[...] = jnp.zeros_like(token)

    sem = pltpu.SemaphoreType.DMA((7 * n,))
    res = _pcall(
        body, name=name, in_specs=[HBM] * (2 * n),
        out_specs=[SEM, SEM] + [HBM] * (2 * n) + [pl.BlockSpec(memory_space=pltpu.VMEM)],
        out_shape=[sem, sem] + [pltpu.HBM(t.shape, t.dtype) for t in srcs + lands] + [jax.ShapeDtypeStruct((8, 128), F32)],
        input_output_aliases={i: 2 + i for i in range(2 * n)},
        compiler_params=pltpu.CompilerParams(has_side_effects=EFFECT),
    )(*srcs, *lands)
    return (res[0], res[1], res[2:2 + n], res[2 + n:2 + 2 * n], npc), res[-1]


def scatter_wait(group, name):
    send, recv, srcs, lands, npc = group
    n = len(srcs)

    def body(*refs):
        src, land, send_ref, recv_ref = refs[:n], refs[n:2 * n], refs[2 * n], refs[2 * n + 1]
        x, y, c = _mesh_pos()
        for m in range(1, 8):
            px, py, pc = _peer(x, y, c, m)
            for k in range(n):
                s_ref = src[k].at[0, 0] if k < npc else src[k]
                d_ref = land[k].at[m] if k < npc else land[k].at[4 * px + 2 * py + pc]
                cp = pltpu.make_async_remote_copy(src_ref=s_ref, dst_ref=d_ref, send_sem=send_ref.at[7 * k + m - 1],
                                                  recv_sem=recv_ref.at[7 * k + m - 1], device_id=(px, py, pc), device_id_type=MESH)
                cp.wait_send()
                cp.wait_recv()

    res = _pcall(
        body, name=name, in_specs=[HBM] * (2 * n) + [SEM, SEM], out_specs=[HBM] * (2 * n),
        out_shape=[pltpu.HBM(t.shape, t.dtype) for t in list(srcs) + list(lands)],
        input_output_aliases={i: i for i in range(2 * n)}, compiler_params=pltpu.CompilerParams(has_side_effects=EFFECT),
    )(*srcs, *lands, send, recv)
    return res[:n], res[n:]


def swap_halves(pieces):
    n = len(pieces)
    whole = pl.BlockSpec(memory_space=pltpu.VMEM)

    def body(*refs):
        p_refs, o_refs, send_sems, recv_sems, local_sems = refs[:n], refs[n:2 * n], refs[2 * n], refs[2 * n + 1], refs[2 * n + 2]
        x, y, c = _mesh_pos()
        local = [pltpu.make_async_copy(p_refs[k], o_refs[k].at[c], local_sems.at[k]) for k in range(n)]
        for cp in local:
            cp.start()

        def copy(k, slot):
            return pltpu.make_async_remote_copy(src_ref=p_refs[k], dst_ref=o_refs[k].at[slot], send_sem=send_sems.at[k],
                                                recv_sem=recv_sems.at[k], device_id=(x, y, 1 - c), device_id_type=MESH)

        for k in range(n):
            copy(k, c).start()
        for k in range(n):
            copy(k, 1 - c).wait_recv()
        for k in range(n):
            copy(k, c).wait_send()
        for cp in local:
            cp.wait()

    return _pcall(
        body, name="swap_halves", in_specs=[whole] * n, out_specs=[whole] * n,
        out_shape=[jax.ShapeDtypeStruct((2,) + t.shape, t.dtype) for t in pieces],
        scratch_shapes=[pltpu.SemaphoreType.DMA((n,)), pltpu.SemaphoreType.DMA((n,)), pltpu.SemaphoreType.DMA((n,))],
        compiler_params=_cparams(),
    )(*pieces)


def adamw(w, g, m, v, name):
    rows, cols = w.shape
    tm = rows
    for t in (256, 352, 128, 144, 64, 32, 16, 8):
        if rows % t == 0:
            tm = t
            break

    def fn(i, nrow, wv, gv, mv, vv):
        mn = ADAM_B1 * mv + (1.0 - ADAM_B1) * gv
        vn = ADAM_B2 * vv + (1.0 - ADAM_B2) * (gv * gv)
        m_hat = mn / (1.0 - ADAM_B1 ** ADAM_STEP)
        v_hat = vn / (1.0 - ADAM_B2 ** ADAM_STEP)
        delta = -ADAM_LR * (m_hat / (jnp.sqrt(v_hat) + ADAM_EPS) + ADAM_WD * wv)
        return delta, mn, vn

    return ew(fn, name, rows, tm, 1, [(t, "row", cols, 0) for t in (w, g, m, v)], [(cols, F32, cols)] * 3)


BIG = ("w_in", "w_out", "w_up", "w_down")
REST = ("w_out", "w_up", "w_down")
REST_ROWS = (512, 1408, 704)
W_IN_ROWS, W_IN_ROWS_PADDED = 1544, 1568
SMALL = ("norm1_w", "ssm_conv_w", "ssm_conv_b", "a_log_f", "a_log_b", "dt_bias_f", "dt_bias_b", "d_skip",
         "ssm_norm_w", "norm2_w", "ffn_conv_w", "ffn_conv_b", "final_norm_w")
WEIGHTS = ("norm1_w", "w_in", "ssm_conv_w", "ssm_conv_b", "a_log_f", "a_log_b", "dt_bias_f", "dt_bias_b", "d_skip",
           "ssm_norm_w", "w_out", "norm2_w", "w_up", "ffn_conv_w", "ffn_conv_b", "w_down", "final_norm_w")
INPUTS = ("x",) + WEIGHTS + ("loss_target",) + tuple("m_" + n for n in WEIGHTS) + tuple("v_" + n for n in WEIGHTS)


def _flat_rows(parts, width, rows):
    flat = jnp.concatenate([p.reshape(-1) for p in parts])
    return jnp.pad(flat, (0, rows * width - flat.shape[0])).reshape(rows, width)


def _split_flat(flat, shapes):
    out, pos = [], 0
    flat = flat.reshape(-1)
    for shp in shapes:
        n = int(np.prod(shp))
        out.append(flat[pos:pos + n].reshape(shp))
        pos += n
    return out


def _col_shards(t, nshard):
    r, c = t.shape
    return t.reshape(r, nshard, c // nshard).transpose(1, 0, 2).reshape(nshard, -1, D)


def _row_shards(t, nshard):
    r, c = t.shape
    return t.reshape(nshard, -1, D)


def kernel(x, norm1_w, w_in, ssm_conv_w, ssm_conv_b, a_log_f, a_log_b, dt_bias_f, dt_bias_b, d_skip, ssm_norm_w, w_out, norm2_w, w_up, ffn_conv_w, ffn_conv_b, w_down, final_norm_w, loss_target, m_norm1_w, m_w_in, m_ssm_conv_w, m_ssm_conv_b, m_a_log_f, m_a_log_b, m_dt_bias_f, m_dt_bias_b, m_d_skip, m_ssm_norm_w, m_w_out, m_norm2_w, m_w_up, m_ffn_conv_w, m_ffn_conv_b, m_w_down, m_final_norm_w, v_norm1_w, v_w_in, v_ssm_conv_w, v_ssm_conv_b, v_a_log_f, v_a_log_b, v_dt_bias_f, v_dt_bias_b, v_d_skip, v_ssm_norm_w, v_w_out, v_norm2_w, v_w_up, v_ffn_conv_w, v_ffn_conv_b, v_w_down, v_final_norm_w):
    p = dict(zip(INPUTS, (x, norm1_w, w_in, ssm_conv_w, ssm_conv_b, a_log_f, a_log_b, dt_bias_f, dt_bias_b, d_skip, ssm_norm_w, w_out, norm2_w, w_up, ffn_conv_w, ffn_conv_b, w_down, final_norm_w, loss_target, m_norm1_w, m_w_in, m_ssm_conv_w, m_ssm_conv_b, m_a_log_f, m_a_log_b, m_dt_bias_f, m_dt_bias_b, m_d_skip, m_ssm_norm_w, m_w_out, m_norm2_w, m_w_up, m_ffn_conv_w, m_ffn_conv_b, m_w_down, m_final_norm_w, v_norm1_w, v_w_in, v_ssm_conv_w, v_ssm_conv_b, v_a_log_f, v_a_log_b, v_dt_bias_f, v_dt_bias_b, v_d_skip, v_ssm_norm_w, v_w_out, v_norm2_w, v_w_up, v_ffn_conv_w, v_ffn_conv_b, v_w_down, v_final_norm_w)))
    x = p["x"][0]
    tgt = p["loss_target"][0]
    s = x.shape[0]
    chip = 2 * lax.axis_index("x") + lax.axis_index("y")

    own_slot = lambda land, mine, slot: lax.dynamic_update_slice_in_dim(land, mine[None], slot, axis=0)
    src_in = p["w_in"][0].reshape(-1, D).astype(BF16)
    src_rest = jnp.concatenate([p[n][0].reshape(-1, D) for n in REST], axis=0).astype(BF16)
    small_w = _flat_rows([p["ssm_conv_w"][0], p["ffn_conv_w"][0]], 128, 48)
    gather_in, gather_rest = gather_start([src_in, small_w], [src_rest])
    (src_in, small_w), (wg_in, sg) = gather_wait(gather_in, "gather_wait_in")
    w_in = own_slot(wg_in, src_in, chip).reshape(4, D, -1).transpose(1, 0, 2).reshape(D, -1)
    sg = own_slot(sg, small_w, chip)
    o = np.cumsum((0,) + REST_ROWS)
    n_in = w_in.shape[1]
    n_main = 6 * D
    w_main = w_in[:, :n_main]
    w_dt = jnp.pad(w_in[:, n_main:], ((0, 0), (0, 128 - (n_in - n_main))))
    sgf = sg.reshape(4, -1)
    n_sc, n_fc = p["ssm_conv_w"].shape[1], p["ffn_conv_w"].shape[1]
    ssm_cw = sgf[:, :n_sc * 3].reshape(-1, 3).T
    ffn_cw = sgf[:, n_sc * 3:(n_sc + n_fc) * 3].reshape(-1, 3).T
    ssm_cb, ffn_cb = p["ssm_conv_b"], p["ffn_conv_b"]
    n1w, n2w, snw, fnw = p["norm1_w"], p["norm2_w"], p["ssm_norm_w"], p["final_norm_w"].reshape(1, D)

    h1, = ew(lambda i, n, xv, w: _rms_fwd(xv, w), "rms1", s, 256, 1,
             [(x, "row", D, 0), (n1w, "const", D, 0)], [(D, BF16, D)])
    proj = matmul(h1, w_main, "nn", "in_proj")
    proj_dt = matmul(h1, w_dt, "nn", "in_proj_dt")
    tabs = _rope_tables(s)
    attn, lse = attn_fwd_all(proj, tabs, "attn_fwd")

    def conv_silu_fn(i, n, xv, xp, xn, w, b):
        return _silu(w[0:1] * _shift_down(xv, xp, i) + w[1:2] * xv + w[2:3] * _shift_up(xv, xn, i, n) + b)

    xbc_act, = ew(conv_silu_fn, "ssm_conv", s, 256, 2,
                  [(proj, "row", D, 4), (proj, "prev", D, 4), (proj, "next", D, 4),
                   (ssm_cw, "const", D, 0), (ssm_cb, "const", D, 0)], [(2 * D, F32, D)])
    dt_bias = jnp.pad(jnp.concatenate([p["dt_bias_f"], p["dt_bias_b"]], axis=1), ((0, 0), (0, 96)))

    def softplus_fn(i, n, r, b):
        t = r + b
        return jnp.maximum(t, 0.0) + jnp.log(1.0 + jnp.exp(-jnp.abs(t)))

    dt, = ew(softplus_fn, "dt_softplus", s, 512, 1, [(proj_dt, "row", 128, 0), (dt_bias, "const", 128, 0)], [(128, F32, 128)])
    d_exp = jnp.repeat(p["d_skip"], HD, axis=1)
    ssd = []
    for k, (a_log, rev) in enumerate(((p["a_log_f"], False), (p["a_log_b"], True))):
        dt_k = dt[:, 16 * k:16 * k + 16]
        a_head = -jnp.exp(a_log)
        dt_exp = jnp.repeat(dt_k, HD, axis=1)
        dtt = jnp.pad(dt_k.T.reshape(8, 2, s), ((0, 0), (0, 6), (0, 0)))
        a_exp = jnp.repeat(a_head, HD, axis=1)
        a_rows = jnp.broadcast_to(jnp.pad(a_head.reshape(8, 2), ((0, 0), (0, 6)))[:, :, None], (8, 8, 128))
        ssd.append(dict(dt_exp=dt_exp, dtt=dtt, a_exp=a_exp, a_rows=a_rows, rev=rev))
    for t, (y_k, hs_k) in zip(ssd, ssd_fwd(xbc_act, ssd, "ssd_fwd")):
        t["y"], t["hs"] = y_k, hs_k

    def gate_fn(i, n, yf, yb, xs, z, dsk, w):
        g = (yf + yb + dsk * xs) * _silu(z)
        return g * _group_norm_stats(g) * w

    ssm_out, = ew(gate_fn, "ssm_gate_norm", s, 256, 1,
                  [(ssd[0]["y"], "row", D, 0), (ssd[1]["y"], "row", D, 0), (xbc_act, "row", D, 0), (proj, "row", D, 3),
                   (d_exp, "const", D, 0), (snw, "const", D, 0)], [(D, F32, D)])
    mix = jnp.concatenate([attn, ssm_out], axis=1).astype(BF16)
    (src_rest,), (wg_rest,) = gather_wait(gather_rest, "gather_wait_rest", after=mix)
    wg_rest = own_slot(wg_rest, src_rest, chip)
    w_out = wg_rest[:, o[0]:o[1]].reshape(-1, D)
    w_up = wg_rest[:, o[1]:o[2]].reshape(4, D, -1).transpose(1, 0, 2).reshape(D, -1)
    w_down = wg_rest[:, o[2]:o[3]].reshape(-1, D)
    mix_w = matmul(mix, w_out, "nn", "out_proj")

    def res_rms_fn(i, n, xv, mw, w):
        x1v = xv + mw
        return x1v, _rms_fwd(x1v, w)

    x1, h2 = ew(res_rms_fn, "res_rms2", s, 256, 1, [(x, "row", D, 0), (mix_w, "row", D, 0), (n2w, "const", D, 0)],
                [(D, F32, D), (D, BF16, D)])
    hw = matmul(h2, w_up, "nn", "ffn_up")
    fw = D_FF // 2
    nfb = D_FF // fw
    ffn_conv_ins = [(hw, "row", fw, 0), (hw, "prev", fw, 0), (hw, "next", fw, 0),
                    (hw, "row", fw, nfb), (hw, "prev", fw, nfb), (hw, "next", fw, nfb),
                    (ffn_cw, "const", fw, 0), (ffn_cw, "const", fw, nfb), (ffn_cb, "const", fw, 0), (ffn_cb, "const", fw, nfb)]

    def ffn_conv(i, n, g, gp, gn, u, up_, un, wg_, wu, bg, bu):
        gs = (_shift_down(g, gp, i), g, _shift_up(g, gn, i, n))
        us = (_shift_down(u, up_, i), u, _shift_up(u, un, i, n))
        gate = wg_[0:1] * gs[0] + wg_[1:2] * gs[1] + wg_[2:3] * gs[2] + bg
        upv = wu[0:1] * us[0] + wu[1:2] * us[1] + wu[2:3] * us[2] + bu
        return gate, upv, gs, us

    def glu_fn(i, n, *blocks):
        gate, upv, _, _ = ffn_conv(i, n, *blocks)
        return _silu(gate) * upv

    act, = ew(glu_fn, "ffn_conv_glu", s, 256, nfb, ffn_conv_ins, [(D_FF, BF16, fw)])
    ffn = matmul(act, w_down, "nn", "ffn_down")

    def head_fn(i, n, x1v, fv, tv, w):
        x2 = x1v + fv
        r = lax.rsqrt(jnp.mean(x2 * x2, axis=-1, keepdims=True) + EPS)
        xh = x2 * r
        diff = xh * w - tv
        loss = 0.5 * jnp.sum(jnp.mean(diff * diff, axis=-1, keepdims=True), axis=0, keepdims=True)
        dout = diff * (1.0 / D)
        dxh = dout * w
        dx2 = r * (dxh - xh * jnp.mean(dxh * xh, axis=-1, keepdims=True))
        return dx2, jnp.broadcast_to(loss, (1, 128)), _colsum(dout * xh)

    dx2, loss_acc, g_fnw = ew(head_fn, "loss_head", s, 256, 1,
                              [(x1, "row", D, 0), (ffn, "row", D, 0), (tgt, "row", D, 0), (fnw, "const", D, 0)],
                              [(D, F32, D)], [(128, 128), (D, D)])
    loss = lax.psum(loss_acc[0, 0], ("x", "y", "c"))

    g_w_down = matmul(act, dx2, "tn", "d_w_down")
    dact = matmul(dx2, w_down, "nt", "d_act")

    def glu_bwd_fn(i, n, *blocks):
        gate, upv, gs, us = ffn_conv(i, n, *blocks[:-1])
        da = blocks[-1]
        dg = da * upv * _dsilu(gate)
        du = da * _silu(gate)
        return (dg, du) + tuple(_colsum(dg * t) for t in gs) + tuple(_colsum(du * t) for t in us) + (_colsum(dg), _colsum(du))

    res = ew(glu_bwd_fn, "ffn_glu_bwd", s, 256, nfb, ffn_conv_ins + [(dact, "row", fw, 0)],
             [(D_FF, F32, fw)] * 2, [(D_FF, fw)] * 8)
    du_g, du_u = res[0], res[1]
    g_ffn_cw = jnp.concatenate([jnp.concatenate(res[2:5], axis=0), jnp.concatenate(res[5:8], axis=0)], axis=1).T
    g_ffn_cb = jnp.concatenate([res[8], res[9]], axis=1)

    def conv_t_fn(i, n, dv, dp, dn, w):
        return w[0:1] * _shift_up(dv, dn, i, n) + w[1:2] * dv + w[2:3] * _shift_down(dv, dp, i)

    def conv_t(du, cw, off, width, ncol, name):
        return ew(conv_t_fn, name, s, 256, ncol,
                  [(du, "row", width, 0), (du, "prev", width, 0), (du, "next", width, 0), (cw, "const", width, off)],
                  [(du.shape[1], F32, width)])[0]

    dhw_g = conv_t(du_g, ffn_cw, 0, fw, nfb, "ffn_conv_t_gate")
    dhw_u = conv_t(du_u, ffn_cw, nfb, fw, nfb, "ffn_conv_t_up")
    g_w_up = jnp.concatenate([matmul(h2, dhw_g, "tn", "d_w_up_gate"), matmul(h2, dhw_u, "tn", "d_w_up_up")], axis=1)
    dh2_a = matmul(dhw_g, w_up[:, :D_FF], "nt", "d_h2_gate")
    dh2_b = matmul(dhw_u, w_up[:, D_FF:], "nt", "d_h2_up")

    def res_rms_bwd_fn(i, n, dres, da, db, xin, w):
        dx, dw = _rms_bwd(da + db, xin, w)
        return dres + dx, dw

    dx1, g_n2w = ew(res_rms_bwd_fn, "res_rms2_bwd", s, 256, 1,
                    [(dx2, "row", D, 0), (dh2_a, "row", D, 0), (dh2_b, "row", D, 0), (x1, "row", D, 0), (n2w, "const", D, 0)],
                    [(D, F32, D)], [(D, D)])

    g_w_out = matmul(mix, dx1, "tn", "d_w_out")
    to_pieces = lambda t: t.astype(BF16).reshape(4, 2, -1, D)
    shards_rest = jnp.concatenate([_row_shards(g_w_out, 4), _col_shards(g_w_up, 4), _row_shards(g_w_down, 4)], axis=1)
    scatter_rest, token = scatter_start([to_pieces(shards_rest)], [], "scatter_start_rest")
    w_out_after = w_out + token[0:1, 0:1].astype(BF16)
    dmix = matmul(dx1, w_out_after, "nt", "d_mix")
    ii, jj = np.arange(D)[:, None] // HD, np.arange(D)[None, :] // HD
    seg = jnp.asarray(ii == jj, BF16)

    def gate_bwd_fn(i, n, dout, yf, yb, xs, z, dsk, w, segm):
        yt = yf + yb + dsk * xs
        sz = _silu(z)
        g = yt * sz
        r = _group_norm_stats(g)
        gh = g * r
        dn = dout * w
        dg = r * (dn - gh * _group_mean(dn * gh))
        dy = dg * sz
        dsk_lane = jnp.broadcast_to(_colsum(dy * xs), (8, D))
        return dy, dg * yt * _dsilu(z), _colsum(dout * gh), sum(_dot(q, segm) for q in _parts(dsk_lane, 2))[0:1]

    dy, dz, g_snw, g_dskip_l = ew(
        gate_bwd_fn, "ssm_gate_norm_bwd", s, 256, 1,
        [(dmix, "row", D, 1), (ssd[0]["y"], "row", D, 0), (ssd[1]["y"], "row", D, 0), (xbc_act, "row", D, 0),
         (proj, "row", D, 3), (d_exp, "const", D, 0), (snw, "const", D, 0), (seg, "const", D, 0)],
        [(D, F32, D)] * 2, [(D, D)] * 2)
    sb = ssd_bwd(xbc_act, ssd, dy, "ssd_bwd")

    def dxbc_act_fn(i, n, dxf, dxb, dyv, dsk, dbf, dbb, dcf, dcb_):
        db, dc = dbf + dbb, dcf + dcb_
        db = [db[:, 256 * g:256 * g + 128] + db[:, 256 * g + 128:256 * g + 256] for g in range(4)]
        dc = [dc[:, 256 * g:256 * g + 128] + dc[:, 256 * g + 128:256 * g + 256] for g in range(4)]
        return jnp.concatenate([dxf + dxb + dyv * dsk] + db + dc, axis=1)

    dxbc_act, = ew(dxbc_act_fn, "d_xbc_act", s, 256, 1,
                   [(sb[0][0], "row", D, 0), (sb[1][0], "row", D, 0), (dy, "row", D, 0), (d_exp, "const", D, 0),
                    (sb[0][2], "row", D, 0), (sb[1][2], "row", D, 0), (sb[0][3], "row", D, 0), (sb[1][3], "row", D, 0)],
                   [(2 * D, F32, 2 * D)])

    def silu_bwd_fn(i, n, xv, xp, xn, w, b, da):
        xs3 = (_shift_down(xv, xp, i), xv, _shift_up(xv, xn, i, n))
        du = da * _dsilu(w[0:1] * xs3[0] + w[1:2] * xs3[1] + w[2:3] * xs3[2] + b)
        return (du,) + tuple(_colsum(du * t) for t in xs3) + (_colsum(du),)

    res = ew(silu_bwd_fn, "ssm_conv_bwd", s, 256, 2,
             [(proj, "row", D, 4), (proj, "prev", D, 4), (proj, "next", D, 4), (ssm_cw, "const", D, 0),
              (ssm_cb, "const", D, 0), (dxbc_act, "row", D, 0)], [(2 * D, F32, D)], [(2 * D, D)] * 4)
    g_ssm_cw = jnp.concatenate(res[1:4], axis=0).T
    g_ssm_cb = res[4]
    dxbc = conv_t(res[0], ssm_cw, 0, D, 2, "ssm_conv_t")
    ddt = jnp.pad(jnp.concatenate([sb[0][1][:, ::HD], sb[1][1][:, ::HD]], axis=1), ((0, 0), (0, 96)))

    def dt_bwd_fn(i, n, dd, r, b):
        dr = dd * _sigmoid(r + b)
        return dr, _colsum(dr)

    dproj_dt, g_dt_bias = ew(dt_bwd_fn, "dt_softplus_bwd", s, 512, 1,
                             [(ddt, "row", 128, 0), (proj_dt, "row", 128, 0), (dt_bias, "const", 128, 0)],
                             [(128, F32, 128)], [(128, 128)])
    g_a_log = [t[4][:, 0, ::HD].reshape(1, 16) for t in sb]

    dq, dk, dv = attn_bwd_all(proj, tabs, dmix, attn, lse, "attn_bwd")

    dproj = jnp.concatenate([dq, dk, dv, dz, dxbc], axis=1).astype(BF16)
    g_w_in = jnp.concatenate([matmul(h1, dproj, "tn", "d_w_in"), matmul(h1, dproj_dt, "tn", "d_w_in_dt")[:, :n_in - n_main]], axis=1)
    shards_in = jnp.pad(_col_shards(g_w_in, 4), ((0, 0), (0, W_IN_ROWS_PADDED - W_IN_ROWS), (0, 0)))
    scatter_in, token = scatter_start([to_pieces(shards_in)], [], "scatter_start_in")
    w_main_after = w_main + token[0:1, 0:1].astype(BF16)
    dh1_a = matmul(dproj, w_main_after, "nt", "d_h1")
    dh1_b = matmul(dproj_dt, w_dt, "nt", "d_h1_dt")
    grad_x, g_n1w = ew(res_rms_bwd_fn, "rms1_bwd", s, 256, 1,
                       [(dx1, "row", D, 0), (dh1_a, "row", D, 0), (dh1_b, "row", D, 0), (x, "row", D, 0), (n1w, "const", D, 0)],
                       [(D, F32, D)], [(D, D)])

    small_g = {"norm1_w": g_n1w, "ssm_conv_w": g_ssm_cw, "ssm_conv_b": g_ssm_cb, "a_log_f": g_a_log[0], "a_log_b": g_a_log[1],
               "dt_bias_f": g_dt_bias[:, :16], "dt_bias_b": g_dt_bias[:, 16:32], "d_skip": g_dskip_l[:, ::HD],
               "ssm_norm_w": g_snw, "norm2_w": g_n2w, "ffn_conv_w": g_ffn_cw, "ffn_conv_b": g_ffn_cb, "final_norm_w": g_fnw}
    small_shapes = [small_g[n].shape for n in SMALL]
    scatter_small, _ = scatter_start([], [_flat_rows([small_g[n] for n in SMALL], 128, SMALL_ROWS)], "scatter_start_small")
    (sent_rest,), (got_rest,) = scatter_wait(scatter_rest, "scatter_wait_rest")
    (sent_in,), (got_in,) = scatter_wait(scatter_in, "scatter_wait_in")
    (sent_small,), (got_small,) = scatter_wait(scatter_small, "scatter_wait_small")
    core = lax.axis_index("c")

    def sum8_fn(i, n, *v):
        t = v[0].astype(F32)
        for u in v[1:]:
            t = t + u.astype(F32)
        return t

    def sum_pieces(sent, got, tm, ncol, name):
        rows = got.shape[1]
        mine = lax.dynamic_slice(sent, (chip, core, 0, 0), (1, 1, rows, D)).reshape(rows, D)
        w = D // ncol
        ins = [(mine, "row", w, 0)] + [(got.reshape(8 * rows, D), "row", w, 0, k * (rows // tm)) for k in range(1, 8)]
        return ew(sum8_fn, name, rows, tm, ncol, ins, [(D, F32, w)])[0]

    rows_rest = int(o[3])
    got_small = own_slot(got_small, sent_small, 2 * chip + core)
    small_sum, = ew(sum8_fn, "sum_small", SMALL_ROWS, SMALL_ROWS, 1,
                    [(got_small.reshape(8 * SMALL_ROWS, 128), "row", 128, 0, k) for k in range(8)], [(128, F32, 128)])
    g_rest, g_in = swap_halves([sum_pieces(sent_rest, got_rest, rows_rest // 4, 1, "sum_pieces_rest"),
                                sum_pieces(sent_in, got_in, W_IN_ROWS_PADDED // 2, 2, "sum_pieces_in")])
    g_rest = g_rest.reshape(-1, D)
    grads = {n: g_rest[o[k]:o[k + 1]].reshape(p[n].shape) for k, n in enumerate(REST)}
    grads["w_in"] = g_in.reshape(-1, D)[:W_IN_ROWS].reshape(p["w_in"].shape)
    for n, g in zip(SMALL, _split_flat(small_sum, small_shapes)):
        if n in ("ssm_conv_w", "ffn_conv_w"):
            rows = p[n].shape[1]
            g = lax.dynamic_slice_in_dim(g, chip * rows, rows, axis=0)
        grads[n] = g.reshape(p[n].shape)

    delta, new_m, new_v = {}, {}, {}
    for n in BIG:
        shp = p[n].shape
        r = [t.reshape(shp[1:]) for t in (p[n], grads[n], p["m_" + n], p["v_" + n])]
        delta[n], new_m[n], new_v[n] = [t.reshape(shp) for t in adamw(*r, "adamw_" + n)]
    shapes = [p[n].shape for n in SMALL]
    total = sum(int(np.prod(sh)) for sh in shapes)
    rows = -(-total // 1024) * 8
    packs = [_flat_rows([t[n] for n in SMALL], 128, rows)
             for t in (p, grads, {n: p["m_" + n] for n in SMALL}, {n: p["v_" + n] for n in SMALL})]
    for dst, t in zip((delta, new_m, new_v), adamw(*packs, "adamw_small")):
        for n, u in zip(SMALL, _split_flat(t, shapes)):
            dst[n] = u
    return (loss, grad_x[None], *[grads[n] for n in WEIGHTS], *[delta[n] for n in WEIGHTS],
            *[new_m[n] for n in WEIGHTS], *[new_v[n] for n in WEIGHTS])
```

```python
import numpy as np
import jax
import jax.numpy as jnp
from jax import lax
from jax.experimental import pallas as pl
from jax.experimental.pallas import tpu as pltpu

F32, BF16 = jnp.float32, jnp.bfloat16
MESH = pl.DeviceIdType.MESH
V7X_VMEM_LIMIT = 56 * 1024 * 1024

D = 1024
HD = 64
EPS = 1e-6
CHUNK = 128
D_FF = 2816
ROPE_DIM = 16
ROPE_THETA = 500000.0
PATTERN_DILATIONS = (1, 4, 16)
BAND = 64
SMALL_ROWS = 280
ADAM_LR, ADAM_B1, ADAM_B2, ADAM_EPS, ADAM_WD, ADAM_STEP = 0.001, 0.9, 0.999, 1e-08, 0.01, 10

NN = (((1,), (0,)), ((), ()))
NT = (((1,), (1,)), ((), ()))
TN = (((0,), (0,)), ((), ()))


def _pcall(body, **kw):
    return pl.pallas_call(body, **kw)


def _cparams(sem=None):
    return pltpu.CompilerParams(dimension_semantics=sem, vmem_limit_bytes=V7X_VMEM_LIMIT)


def _dot(a, b, dims=NN):
    return lax.dot_general(a, b, dims, preferred_element_type=F32)


def _pick(n, cap):
    if n <= cap:
        return n
    best = 0
    for t in range(128, cap + 1, 128):
        if n % t == 0:
            best = t
    assert best, (n, cap)
    return best


def _iota(shape, dim):
    return lax.broadcasted_iota(jnp.int32, shape, dim)


def _parts(x, n):
    out, r = [], x
    for _ in range(n):
        h = r.astype(BF16)
        out.append(h)
        r = r - h.astype(F32)
    return out


def _sigmoid(x):
    return 1.0 / (1.0 + jnp.exp(-x))


def _silu(x):
    return x * _sigmoid(x)


def _dsilu(x):
    s = _sigmoid(x)
    return s * (1.0 + x * (1.0 - s))


def matmul(a, b, mode, name, out_dtype=F32):
    if mode == "nn":
        (m, k), (_, n) = a.shape, b.shape
    elif mode == "nt":
        (m, k), (n, _) = a.shape, b.shape
    else:
        (k, m), (_, n) = a.shape, b.shape
    tm, tn, tk = _pick(m, 1408), _pick(n, 1408), _pick(k, 1408)
    nk = k // tk
    dims = {"nn": NN, "nt": NT, "tn": TN}[mode]
    a_spec = pl.BlockSpec((tk, tm), lambda i, j, kk: (kk, i)) if mode == "tn" else pl.BlockSpec((tm, tk), lambda i, j, kk: (i, kk))
    b_spec = pl.BlockSpec((tn, tk), lambda i, j, kk: (j, kk)) if mode == "nt" else pl.BlockSpec((tk, tn), lambda i, j, kk: (kk, j))

    def body(a_ref, b_ref, o_ref, *acc):
        part = _dot(a_ref[...].astype(BF16), b_ref[...].astype(BF16), dims)
        if nk == 1:
            o_ref[...] = part.astype(o_ref.dtype)
            return
        acc_ref, kk = acc[0], pl.program_id(2)

        @pl.when(kk == 0)
        def _():
            acc_ref[...] = part

        @pl.when((kk > 0) & (kk < nk - 1))
        def _():
            acc_ref[...] += part

        @pl.when(kk == nk - 1)
        def _():
            o_ref[...] = (acc_ref[...] + part).astype(o_ref.dtype)

    return _pcall(
        body, name=name, grid=(m // tm, n // tn, nk), in_specs=[a_spec, b_spec],
        out_specs=pl.BlockSpec((tm, tn), lambda i, j, kk: (i, j)),
        out_shape=jax.ShapeDtypeStruct((m, n), out_dtype),
        scratch_shapes=[pltpu.VMEM((tm, tn), F32)] if nk > 1 else [],
        compiler_params=_cparams(("parallel", "parallel", "arbitrary")),
    )(a, b)


def ew(fn, name, rows, tm, ncol, ins, outs, accs=()):
    nrow = rows // tm
    r8 = tm // 8
    in_specs, arrays = [], []
    for ent in ins:
        arr, kind, w, off = ent[:4]
        roff = ent[4] if len(ent) > 4 else 0
        if kind == "row":
            spec = pl.BlockSpec((tm, w), lambda j, i, off=off, roff=roff: (i + roff, j + off))
        elif kind == "const":
            spec = pl.BlockSpec((arr.shape[0], w), lambda j, i, off=off: (0, j + off))
        elif kind == "prev":
            spec = pl.BlockSpec((8, w), lambda j, i, off=off: (jnp.maximum(i * r8 - 1, 0), j + off))
        else:
            spec = pl.BlockSpec((8, w), lambda j, i, off=off: (jnp.minimum((i + 1) * r8, rows // 8 - 1), j + off))
        in_specs.append(spec)
        arrays.append(arr)
    out_specs = [pl.BlockSpec((tm, w), lambda j, i: (i, j)) for (_, _, w) in outs]
    out_shape = [jax.ShapeDtypeStruct((rows, c), dt) for (c, dt, _) in outs]
    out_specs += [pl.BlockSpec((1, w), lambda j, i: (0, j)) for (_, w) in accs]
    out_shape += [jax.ShapeDtypeStruct((1, c), F32) for (c, _) in accs]
    nin, nout = len(ins), len(outs)

    def body(*refs):
        i = pl.program_id(1)
        res = fn(i, nrow, *[r[...] for r in refs[:nin]])
        if not isinstance(res, (tuple, list)):
            res = (res,)
        for r, v in zip(refs[nin:nin + nout], res[:nout]):
            r[...] = v.astype(r.dtype)
        if accs:
            acc_refs = refs[nin + nout:]

            @pl.when(i == 0)
            def _():
                for r in acc_refs:
                    r[...] = jnp.zeros_like(r)

            for r, v in zip(acc_refs, res[nout:]):
                r[...] += v

    res = _pcall(
        body, name=name, grid=(ncol, nrow), in_specs=in_specs, out_specs=out_specs, out_shape=out_shape,
        compiler_params=_cparams(("parallel", "arbitrary")),
    )(*arrays)
    return res


def _shift_down(x, prev8, i):
    first = jnp.where(i == 0, 0.0, prev8[7:8, :])
    return jnp.where(_iota(x.shape, 0) == 0, first, pltpu.roll(x, 1, 0))


def _shift_up(x, next8, i, nrow):
    last = jnp.where(i == nrow - 1, 0.0, next8[0:1, :])
    return jnp.where(_iota(x.shape, 0) == x.shape[0] - 1, last, pltpu.roll(x, x.shape[0] - 1, 0))


def _colsum(x):
    return jnp.sum(x, axis=0, keepdims=True)


def _rms_fwd(x, w):
    r = lax.rsqrt(jnp.mean(x * x, axis=-1, keepdims=True) + EPS)
    return x * r * w


def _rms_bwd(dy, x, w):
    r = lax.rsqrt(jnp.mean(x * x, axis=-1, keepdims=True) + EPS)
    xh = x * r
    dxh = dy * w
    dx = r * (dxh - xh * jnp.mean(dxh * xh, axis=-1, keepdims=True))
    return dx, _colsum(dy * xh)


def _rope_tables(s):
    half = ROPE_DIM // 2
    inv_freq = jnp.power(ROPE_THETA, -jnp.arange(half, dtype=F32) * 2.0 / ROPE_DIM)
    ang = jnp.arange(s, dtype=F32)[:, None] * inv_freq[None, :]
    cos, sin = jnp.cos(ang), jnp.sin(ang)
    one, zero = jnp.ones((s, HD - ROPE_DIM), F32), jnp.zeros((s, HD - ROPE_DIM), F32)
    z8 = jnp.zeros((s, half), F32)
    c = jnp.concatenate([cos, cos, one], axis=1)
    sa = jnp.concatenate([-sin, z8, zero], axis=1)
    sb = jnp.concatenate([z8, sin, zero], axis=1)
    return [jnp.tile(t, (1, 2)) for t in (c, sa, sb)]


ATTN_CHUNK = 1024


def _attn_plan(s):
    plan = []
    for d in PATTERN_DILATIONS:
        per_res = ATTN_CHUNK // d
        tq = min(128, per_res)
        plan.append((d, tq, min(s // d, tq + 2 * BAND), per_res // tq, s // d))
    return plan


def _rows(start, size, d):
    return pl.ds(start, size) if d == 1 else pl.ds(start, size, stride=d)


def _for_tiles(chunk, pat, fn):
    d, tq, win, nblk, seq_len = pat
    for b in range(nblk):
        t0 = chunk * (ATTN_CHUNK // d) + b * tq
        kloc = jnp.clip(t0 - BAND, 0, seq_len - win)
        valid = jnp.abs(kloc + _iota((tq, win), 1) - (t0 + _iota((tq, win), 0))) <= BAND
        valid = jnp.concatenate([valid, valid], axis=0)
        if d == 1:
            fn(b * tq, pl.multiple_of(kloc, BAND), valid)
        else:
            def step(r, carry, qoff=d * b * tq, koff=d * kloc, valid=valid):
                fn(qoff + r, koff + r, valid)
                return carry
            lax.fori_loop(0, d, step, 0, unroll=min(d, 4))


def _stack_heads(x, head0):
    zero = jnp.zeros_like(x)
    return jnp.concatenate([jnp.where(head0, x, zero), jnp.where(head0, zero, x)], axis=0)


def _rope_pair(x, c, sa, sb):
    n = x.shape[1]
    return x * c + pltpu.roll(x, n - 8, 1) * sa + pltpu.roll(x, 8, 1) * sb


def _rope_pair_t(dy, c, sa, sb):
    n = dy.shape[1]
    return dy * c + pltpu.roll(dy * sa, 8, 1) + pltpu.roll(dy * sb, n - 8, 1)


def _attn_specs(s):
    whole = lambda off: pl.BlockSpec((s, 128), lambda p, c: (0, off + p))
    table = pl.BlockSpec((s, 128), lambda p, c: (0, 0))
    chunk = pl.BlockSpec((ATTN_CHUNK, 128), lambda p, c: (c, p))
    return whole, table, chunk


def attn_fwd_all(proj, tabs, name):
    s = proj.shape[0]
    plan = _attn_plan(s)
    whole, table, chunk_spec = _attn_specs(s)

    def body(q_ref, k_ref, v_ref, c_ref, sa_ref, sb_ref, o_ref, lse_ref, qs, ks, acc_s, m_s, l_s):
        chunk = pl.program_id(1)

        @pl.when(chunk == 0)
        def _():
            qs[...] = _rope_pair(q_ref[...], c_ref[...], sa_ref[...], sb_ref[...]) * (HD ** -0.5)
            ks[...] = _rope_pair(k_ref[...], c_ref[...], sa_ref[...], sb_ref[...])

        base = pl.multiple_of(chunk * ATTN_CHUNK, ATTN_CHUNK)
        for pi, pat in enumerate(plan):
            d, tq, win = pat[:3]
            head0 = _iota((tq, 128), 1) < HD

            def tile(qrow, krow, valid, pi=pi, d=d, tq=tq, win=win, head0=head0):
                qv = qs[_rows(base + qrow, tq, d), :].astype(BF16)
                kw = ks[_rows(krow, win, d), :].astype(BF16)
                vw = v_ref[_rows(krow, win, d), :].astype(BF16)
                v_ones = jnp.concatenate([vw, jnp.ones_like(vw)], axis=1)
                sc = jnp.where(valid, _dot(_stack_heads(qv, head0), kw, NT), -1e30)
                mh = jnp.max(sc, axis=1, keepdims=True)
                pv = _dot(jnp.exp(sc - mh).astype(BF16), v_ones)
                acc_s[pi, _rows(qrow, tq, d), :] = jnp.where(head0, pv[:tq, :128], pv[tq:, :128])
                m_s[pi, _rows(qrow, tq, d), :] = jnp.where(head0, mh[:tq], mh[tq:])
                l_s[pi, _rows(qrow, tq, d), :] = jnp.where(head0, pv[:tq, 128:], pv[tq:, 128:])

            _for_tiles(chunk, pat, tile)
        m_all = jnp.maximum(jnp.maximum(m_s[0], m_s[1]), m_s[2])
        e = [jnp.exp(m_s[k] - m_all) for k in range(3)]
        den = e[0] * l_s[0] + e[1] * l_s[1] + e[2] * l_s[2]
        o_ref[...] = (e[0] * acc_s[0] + e[1] * acc_s[1] + e[2] * acc_s[2]) / den
        lse_ref[...] = m_all + jnp.log(den)

    stat = pltpu.VMEM((3, ATTN_CHUNK, 128), F32)
    return _pcall(
        body, name=name, grid=(D // 128, s // ATTN_CHUNK),
        in_specs=[whole(0), whole(8), whole(16), table, table, table], out_specs=[chunk_spec, chunk_spec],
        out_shape=[jax.ShapeDtypeStruct((s, D), F32)] * 2,
        scratch_shapes=[pltpu.VMEM((s, 128), F32), pltpu.VMEM((s, 128), F32), stat, stat, stat],
        compiler_params=_cparams(("parallel", "arbitrary")),
    )(proj, proj, proj, *tabs)


def attn_bwd_all(proj, tabs, dmix, o, lse, name):
    s = proj.shape[0]
    plan = _attn_plan(s)
    whole, table, chunk_spec = _attn_specs(s)
    nchunk = s // ATTN_CHUNK

    def body(q_ref, k_ref, v_ref, c_ref, sa_ref, sb_ref, do_ref, o_ref, lse_ref, dq_ref, dk_ref, dv_ref, qs, ks, aug0_s, aug1_s):
        chunk = pl.program_id(1)

        @pl.when(chunk == 0)
        def _():
            qs[...] = _rope_pair(q_ref[...], c_ref[...], sa_ref[...], sb_ref[...]) * (HD ** -0.5)
            ks[...] = _rope_pair(k_ref[...], c_ref[...], sa_ref[...], sb_ref[...])
            dk_ref[...] = jnp.zeros_like(dk_ref)
            dv_ref[...] = jnp.zeros_like(dv_ref)

        base = pl.multiple_of(chunk * ATTN_CHUNK, ATTN_CHUNK)
        prod = do_ref[...] * o_ref[...]
        first = _iota(prod.shape, 1) < HD
        delta = jnp.where(first, jnp.sum(jnp.where(first, prod, 0.0), axis=1, keepdims=True),
                          jnp.sum(jnp.where(first, 0.0, prod), axis=1, keepdims=True))
        lane = _iota(prod.shape, 1)

        def as_lanes(lse_h, delta_h):
            a, b = [u.astype(F32) for u in _parts(lse_h, 3)], [u.astype(F32) for u in _parts(delta_h, 3)]
            out = jnp.zeros_like(lse_h)
            for k, u in enumerate(a + b):
                out = jnp.where(lane == k, u, out)
            return out

        lsev = lse_ref[...]
        aug0_s[...] = as_lanes(lsev, delta)
        aug1_s[...] = as_lanes(pltpu.roll(lsev, HD, 1), pltpu.roll(delta, HD, 1))
        for pi, pat in enumerate(plan):
            d, tq, win = pat[:3]
            head0 = _iota((tq, 128), 1) < HD

            def tile(qrow, krow, valid, pi=pi, d=d, tq=tq, win=win, head0=head0):
                qv = qs[_rows(base + qrow, tq, d), :].astype(BF16)
                kw = ks[_rows(krow, win, d), :].astype(BF16)
                vw = v_ref[_rows(krow, win, d), :].astype(BF16)
                dob = do_ref[_rows(qrow, tq, d), :].astype(BF16)
                aug = jnp.concatenate([aug0_s[_rows(qrow, tq, d), :], aug1_s[_rows(qrow, tq, d), :]], axis=0).astype(BF16)
                klane = _iota((win, 128), 1)
                minus_lse = jnp.where(klane < 3, -1.0, 0.0).astype(BF16)
                minus_delta = jnp.where((klane >= 3) & (klane < 6), -1.0, 0.0).astype(BF16)
                q2, do2 = _stack_heads(qv, head0), _stack_heads(dob, head0)
                s_lse = _dot(jnp.concatenate([q2, aug], axis=1), jnp.concatenate([kw, minus_lse], axis=1), NT)
                dp_delta = _dot(jnp.concatenate([do2, aug], axis=1), jnp.concatenate([vw, minus_delta], axis=1), NT)
                p = jnp.where(valid, jnp.exp(s_lse), 0.0)
                ds = (p * dp_delta).astype(BF16)
                dq2 = _dot(ds, kw)
                dk = _dot(ds, q2, TN)
                dv = _dot(p.astype(BF16), do2, TN)
                dqv = jnp.where(head0, dq2[:tq], dq2[tq:])
                if pi == 0:
                    dq_ref[_rows(qrow, tq, d), :] = dqv
                else:
                    dq_ref[_rows(qrow, tq, d), :] += dqv
                dk_ref[_rows(krow, win, d), :] += dk
                dv_ref[_rows(krow, win, d), :] += dv

            _for_tiles(chunk, pat, tile)
        tab = [t[pl.ds(base, ATTN_CHUNK), :] for t in (c_ref, sa_ref, sb_ref)]
        dq_ref[...] = _rope_pair_t(dq_ref[...] * (HD ** -0.5), *tab)

        @pl.when(chunk == nchunk - 1)
        def _():
            dk_ref[...] = _rope_pair_t(dk_ref[...], c_ref[...], sa_ref[...], sb_ref[...])

    return _pcall(
        body, name=name, grid=(D // 128, nchunk),
        in_specs=[whole(0), whole(8), whole(16), table, table, table, chunk_spec, chunk_spec, chunk_spec],
        out_specs=[chunk_spec, whole(0), whole(0)], out_shape=[jax.ShapeDtypeStruct((s, D), F32)] * 3,
        scratch_shapes=[pltpu.VMEM((s, 128), F32), pltpu.VMEM((s, 128), F32)] + [pltpu.VMEM((ATTN_CHUNK, 128), F32)] * 2,
        compiler_params=_cparams(("parallel", "arbitrary")),
    )(proj, proj, proj, *tabs, dmix, o, lse)


def _ssd_common(x_ref, b_ref, c_ref, dt_ref, dtt_ref, a_ref, ar_ref, rev):
    ii, jj = _iota((CHUNK, CHUNK), 0), _iota((CHUNK, CHUNK), 1)
    low = jj >= ii if rev else jj <= ii
    x, dtx = x_ref[...], dt_ref[...]
    bm, cm = b_ref[...].astype(BF16), c_ref[...].astype(BF16)
    a = dtx * a_ref[...]
    arow = dtt_ref[0] * ar_ref[0]
    lowb = low.astype(BF16)
    cs = _dot(lowb, jnp.concatenate(_parts(a, 3), axis=1))
    cs = cs[:, :128] + cs[:, 128:256] + cs[:, 256:]
    csr = _dot(jnp.concatenate([p.astype(F32) for p in _parts(arow, 3)], axis=0).astype(BF16), lowb, NT)
    csr = csr[0:8] + csr[8:16] + csr[16:24]
    last = 0 if rev else CHUNK - 1
    tot = cs[last:last + 1, :]
    xdt = x * dtx
    cb = _dot(cm, bm, NT)
    lmats = [jnp.exp(jnp.where(low, cs[:, HD * h:HD * h + 1] - csr[h:h + 1, :], -1e30)) for h in range(2)]
    return dict(x=x, dtx=dtx, bm=bm, cm=cm, a=a, cs=cs, tot=tot, xdt=xdt, cb=cb, lmats=lmats, low=low, last=last)


def _ssd_specs(s, rev_order):
    nck = s // CHUNK
    ci = (lambda c: nck - 1 - c) if rev_order else (lambda c: c)
    tile = lambda off, div: pl.BlockSpec((CHUNK, 128), lambda p, c: (ci(c), off + p // div))
    common = [tile(0, 1), tile(8, 2), tile(12, 2), tile(0, 1),
              pl.BlockSpec((1, 8, CHUNK), lambda p, c: (p, 0, ci(c))),
              pl.BlockSpec((1, 128), lambda p, c: (0, p)),
              pl.BlockSpec((1, 8, 128), lambda p, c: (p, 0, 0))]
    hs = pl.BlockSpec((1, 1, CHUNK, 128), lambda p, c: (p, ci(c), 0, 0))
    return nck, common, tile(0, 1), hs


def _ssd_args(xbc, t):
    return [xbc, xbc, xbc, t["dt_exp"], t["dtt"], t["a_exp"], t["a_rows"]]


def ssd_fwd(xbc, dirs, name):
    s = xbc.shape[0]
    nd = len(dirs)
    specs = [_ssd_specs(s, t["rev"]) for t in dirs]
    nck = specs[0][0]

    def one(rev, x_ref, b_ref, c_ref, dt_ref, dtt_ref, a_ref, ar_ref, y_ref, hs_ref, h_scr):
        @pl.when(pl.program_id(1) == 0)
        def _():
            h_scr[...] = jnp.zeros_like(h_scr)

        v = _ssd_common(x_ref, b_ref, c_ref, dt_ref, dtt_ref, a_ref, ar_ref, rev)
        xdtb = v["xdt"].astype(BF16)
        yd = _dot(jnp.concatenate([v["cb"] * v["lmats"][h] for h in range(2)], axis=0).astype(BF16), xdtb)
        h_in = h_scr[...]
        hs_ref[0, 0] = h_in
        y_off = _dot(v["cm"], h_in.astype(BF16)) * jnp.exp(v["cs"])
        y_ref[...] = jnp.where(_iota((CHUNK, 128), 1) < HD, yd[:CHUNK], yd[CHUNK:]) + y_off
        decay = jnp.exp(v["tot"] - v["cs"])
        h_scr[...] = jnp.exp(v["tot"]) * h_in + _dot(v["bm"], (v["xdt"] * decay).astype(BF16), TN)

    def body(*refs):
        for k, t in enumerate(dirs):
            one(t["rev"], *refs[7 * k:7 * k + 7], *refs[7 * nd + 2 * k:7 * nd + 2 * k + 2], refs[9 * nd + k])

    res = _pcall(
        body, name=name, grid=(8, nck), in_specs=[sp for t in specs for sp in t[1]],
        out_specs=[sp for t in specs for sp in (t[2], t[3])],
        out_shape=[jax.ShapeDtypeStruct((s, D), F32), jax.ShapeDtypeStruct((8, nck, CHUNK, 128), F32)] * nd,
        scratch_shapes=[pltpu.VMEM((CHUNK, 128), F32)] * nd, compiler_params=_cparams(("parallel", "arbitrary")),
    )(*[a for t in dirs for a in _ssd_args(xbc, t)])
    return [(res[2 * k], res[2 * k + 1]) for k in range(nd)]


def ssd_bwd(xbc, dirs, dy, name):
    s = xbc.shape[0]
    nd = len(dirs)
    specs = [_ssd_specs(s, not t["rev"]) for t in dirs]
    nck = specs[0][0]

    def one(rev, x_ref, b_ref, c_ref, dt_ref, dtt_ref, a_ref, ar_ref, hs_ref, dy_ref,
            dx_ref, ddt_ref, db_ref, dc_ref, dal_ref, dh_scr):
        @pl.when(pl.program_id(1) == 0)
        def _():
            dh_scr[...] = jnp.zeros_like(dh_scr)
            dal_ref[...] = jnp.zeros_like(dal_ref)

        v = _ssd_common(x_ref, b_ref, c_ref, dt_ref, dtt_ref, a_ref, ar_ref, rev)
        bm, cm, cs, tot, xdt = v["bm"], v["cm"], v["cs"], v["tot"], v["xdt"]
        h_in, dh = hs_ref[0, 0], dh_scr[...]
        dyv = dy_ref[...]
        dyb = dyv.astype(BF16)
        etot, decay, ecs = jnp.exp(tot), jnp.exp(tot - cs), jnp.exp(cs)
        xdtb = xdt.astype(BF16)
        xdec = xdt * decay
        dch = (dyv * ecs).astype(BF16)
        hb, dhb = h_in.astype(BF16), dh.astype(BF16)
        y_off = _dot(cm, hb) * ecs
        dc = _dot(dch, hb, NT)
        dh_y = _dot(cm, dch, TN)
        dxdec = _dot(bm, dhb)
        db = _dot(xdec.astype(BF16), dhb, NT)
        state_term = xdec * dxdec
        dtot = _colsum(dh * h_in) * etot + _colsum(state_term)
        head0 = _iota((CHUNK, 128), 1) < HD
        ii, jj = _iota((CHUNK, CHUNK), 0), _iota((CHUNK, CHUNK), 1)
        low_t = jj <= ii if rev else jj >= ii
        not_low_t = (~low_t).astype(BF16)
        g = _dot(_stack_heads(dyb, head0), xdtb, NT)
        gl = [g[:CHUNK] * v["lmats"][0], g[CHUNK:] * v["lmats"][1]]
        dcb = gl[0] + gl[1]
        dxd = _dot(jnp.concatenate([v["cb"] * v["lmats"][h] for h in range(2)], axis=1).astype(BF16), dyb, TN)
        dxd = jnp.where(head0, dxd[:CHUNK], dxd[CHUNK:])
        w = _dot(not_low_t, jnp.concatenate([gl[h] * v["cb"] for h in range(2)], axis=0).astype(BF16), NT)
        da_l = [jnp.sum(jnp.where(low_t, w[:, CHUNK * h:CHUNK * h + CHUNK], 0.0), axis=1, keepdims=True) for h in range(2)]
        dxdt = dxdec * decay + dxd
        dcbb = dcb.astype(BF16)
        dc_ref[...] = dc + _dot(dcbb, bm)
        db_ref[...] = db + _dot(dcbb, cm, TN)
        dcs = dyv * y_off - state_term + jnp.where(_iota((CHUNK, 128), 0) == v["last"], dtot, 0.0)
        lowb = v["low"].astype(BF16)
        da = _dot(lowb, jnp.concatenate(_parts(dcs, 2), axis=1), TN)
        da = da[:, :128] + da[:, 128:]
        seg = ((ii < HD) == (jj < HD)).astype(BF16)
        sums = _dot(jnp.concatenate(_parts(da, 2) + _parts(dxdt * v["x"], 2), axis=0), seg)
        da = sums[:CHUNK] + sums[CHUNK:2 * CHUNK] + jnp.where(head0, da_l[0], da_l[1])
        ddt_x = sums[2 * CHUNK:3 * CHUNK] + sums[3 * CHUNK:]
        dx_ref[...] = dxdt * v["dtx"]
        ddt_ref[...] = ddt_x + da * a_ref[...]
        dal_ref[0] += _colsum(da * v["a"])
        dh_scr[...] = etot * dh + dh_y

    def body(*refs):
        for k, t in enumerate(dirs):
            one(t["rev"], *refs[9 * k:9 * k + 9], *refs[9 * nd + 5 * k:9 * nd + 5 * k + 5], refs[14 * nd + k])

    acc_spec = pl.BlockSpec((1, 8, 128), lambda p, c: (p, 0, 0))
    res = _pcall(
        body, name=name, grid=(8, nck), in_specs=[sp for t in specs for sp in t[1] + [t[3], t[2]]],
        out_specs=[sp for t in specs for sp in [t[2]] * 4 + [acc_spec]],
        out_shape=([jax.ShapeDtypeStruct((s, D), F32)] * 4 + [jax.ShapeDtypeStruct((8, 8, 128), F32)]) * nd,
        scratch_shapes=[pltpu.VMEM((CHUNK, 128), F32)] * nd, compiler_params=_cparams(("parallel", "arbitrary")),
    )(*[a for t in dirs for a in _ssd_args(xbc, t) + [t["hs"], dy]])
    return [res[5 * k:5 * k + 5] for k in range(nd)]


def _group_norm_stats(g):
    r = [lax.rsqrt(jnp.mean(g[:, 256 * k:256 * k + 256] ** 2, axis=-1, keepdims=True) + EPS) for k in range(4)]
    grp = _iota(g.shape, 1) // 256
    return jnp.where(grp == 0, r[0], jnp.where(grp == 1, r[1], jnp.where(grp == 2, r[2], r[3])))


def _group_mean(t):
    m = [jnp.mean(t[:, 256 * k:256 * k + 256], axis=-1, keepdims=True) for k in range(4)]
    grp = _iota(t.shape, 1) // 256
    return jnp.where(grp == 0, m[0], jnp.where(grp == 1, m[1], jnp.where(grp == 2, m[2], m[3])))


def _mesh_pos():
    return lax.axis_index("x"), lax.axis_index("y"), lax.axis_index("c")


HBM = pl.BlockSpec(memory_space=pltpu.HBM)
SEM = pl.BlockSpec(memory_space=pltpu.SEMAPHORE)
EFFECT = pltpu.SideEffectType.DATAFLOW_SIDE_EFFECTING


def _hbm(t):
    return pltpu.with_memory_space_constraint(t, pltpu.HBM)


def _other_chips(x, y):
    return [(1 - x, y), (x, 1 - y), (1 - x, 1 - y)]


def _peer(x, y, c, m):
    return x ^ (m >> 2), y ^ ((m >> 1) & 1), c ^ (m & 1)


def gather_start(srcs_a, srcs_b):
    srcs = [_hbm(t) for t in list(srcs_a) + list(srcs_b)]
    n, na = len(srcs), len(srcs_a)
    lands = [_hbm(lax.empty((4,) + t.shape, t.dtype)) for t in srcs]

    def body(*refs):
        src, land = refs[:n], refs[n:2 * n]
        sems = refs[2 * n:2 * n + 4]
        x, y, c = _mesh_pos()
        for k in range(n):
            for j, (px, py) in enumerate(_other_chips(x, y)):
                send, recv, idx = (sems[0], sems[1], 3 * k + j) if k < na else (sems[2], sems[3], 3 * (k - na) + j)
                pltpu.make_async_remote_copy(src_ref=src[k], dst_ref=land[k].at[2 * x + y], send_sem=send.at[idx],
                                             recv_sem=recv.at[idx], device_id=(px, py, c), device_id_type=MESH).start()

    sem_a, sem_b = pltpu.SemaphoreType.DMA((3 * na,)), pltpu.SemaphoreType.DMA((3 * (n - na),))
    res = _pcall(
        body, name="gather_start", in_specs=[HBM] * (2 * n), out_specs=[SEM] * 4 + [HBM] * (2 * n),
        out_shape=[sem_a, sem_a, sem_b, sem_b] + [pltpu.HBM(t.shape, t.dtype) for t in srcs + lands],
        input_output_aliases={i: 4 + i for i in range(2 * n)},
        compiler_params=pltpu.CompilerParams(has_side_effects=EFFECT),
    )(*srcs, *lands)
    thru_src, thru_land = res[4:4 + n], res[4 + n:]
    return ((res[0], res[1], thru_src[:na], thru_land[:na]), (res[2], res[3], thru_src[na:], thru_land[na:]))


def gather_wait(group, name, after=None):
    send, recv, srcs, lands = group
    n = len(srcs)

    def body(*refs):
        src, land, send_ref, recv_ref = refs[:n], refs[n:2 * n], refs[2 * n], refs[2 * n + 1]
        x, y, c = _mesh_pos()
        for j, (px, py) in enumerate(_other_chips(x, y)):
            for k in range(n):
                cp = pltpu.make_async_remote_copy(src_ref=src[k], dst_ref=land[k].at[2 * px + py], send_sem=send_ref.at[3 * k + j],
                                                  recv_sem=recv_ref.at[3 * k + j], device_id=(px, py, c), device_id_type=MESH)
                cp.wait_send()
                cp.wait_recv()

    extra = [] if after is None else [after]
    res = _pcall(
        body, name=name, in_specs=[HBM] * (2 * n) + [SEM, SEM] + [pl.BlockSpec(memory_space=pl.ANY)] * len(extra),
        out_specs=[HBM] * (2 * n), out_shape=[pltpu.HBM(t.shape, t.dtype) for t in list(srcs) + list(lands)],
        input_output_aliases={i: i for i in range(2 * n)}, compiler_params=pltpu.CompilerParams(has_side_effects=EFFECT),
    )(*srcs, *lands, send, recv, *extra)
    return res[:n], res[n:]


def scatter_start(pieces, smalls, name):
    srcs = [_hbm(t) for t in list(pieces) + list(smalls)]
    n, npc = len(srcs), len(pieces)
    lands = [_hbm(lax.empty((8,) + (t.shape[2:] if k < npc else t.shape), t.dtype)) for k, t in enumerate(srcs)]

    def body(*refs):
        src, land, send, recv = refs[:n], refs[n:2 * n], refs[2 * n], refs[2 * n + 1]
        token = refs[-1]
        x, y, c = _mesh_pos()
        for m in range(1, 8):
            px, py, pc = _peer(x, y, c, m)
            for k in range(n):
                s_ref = src[k].at[2 * px + py, pc] if k < npc else src[k]
                d_ref = land[k].at[m] if k < npc else land[k].at[4 * x + 2 * y + c]
                pltpu.make_async_remote_copy(src_ref=s_ref, dst_ref=d_ref, send_sem=send.at[7 * k + m - 1], recv_sem=recv.at[7 * k + m - 1],
                                             device_id=(px, py, pc), device_id_type=MESH).start()
        token[...] = jnp.zeros_like(token)

    sem = pltpu.SemaphoreType.DMA((7 * n,))
    res = _pcall(
        body, name=name, in_specs=[HBM] * (2 * n),
        out_specs=[SEM, SEM] + [HBM] * (2 * n) + [pl.BlockSpec(memory_space=pltpu.VMEM)],
        out_shape=[sem, sem] + [pltpu.HBM(t.shape, t.dtype) for t in srcs + lands] + [jax.ShapeDtypeStruct((8, 128), F32)],
        input_output_aliases={i: 2 + i for i in range(2 * n)},
        compiler_params=pltpu.CompilerParams(has_side_effects=EFFECT),
    )(*srcs, *lands)
    return (res[0], res[1], res[2:2 + n], res[2 + n:2 + 2 * n], npc), res[-1]


def scatter_wait(group, name):
    send, recv, srcs, lands, npc = group
    n = len(srcs)

    def body(*refs):
        src, land, send_ref, recv_ref = refs[:n], refs[n:2 * n], refs[2 * n], refs[2 * n + 1]
        x, y, c = _mesh_pos()
        for m in range(1, 8):
            px, py, pc = _peer(x, y, c, m)
            for k in range(n):
                s_ref = src[k].at[0, 0] if k < npc else src[k]
                d_ref = land[k].at[m] if k < npc else land[k].at[4 * px + 2 * py + pc]
                cp = pltpu.make_async_remote_copy(src_ref=s_ref, dst_ref=d_ref, send_sem=send_ref.at[7 * k + m - 1],
                                                  recv_sem=recv_ref.at[7 * k + m - 1], device_id=(px, py, pc), device_id_type=MESH)
                cp.wait_send()
                cp.wait_recv()

    res = _pcall(
        body, name=name, in_specs=[HBM] * (2 * n) + [SEM, SEM], out_specs=[HBM] * (2 * n),
        out_shape=[pltpu.HBM(t.shape, t.dtype) for t in list(srcs) + list(lands)],
        input_output_aliases={i: i for i in range(2 * n)}, compiler_params=pltpu.CompilerParams(has_side_effects=EFFECT),
    )(*srcs, *lands, send, recv)
    return res[:n], res[n:]


def swap_halves(pieces):
    n = len(pieces)
    whole = pl.BlockSpec(memory_space=pltpu.VMEM)

    def body(*refs):
        p_refs, o_refs, send_sems, recv_sems, local_sems = refs[:n], refs[n:2 * n], refs[2 * n], refs[2 * n + 1], refs[2 * n + 2]
        x, y, c = _mesh_pos()
        local = [pltpu.make_async_copy(p_refs[k], o_refs[k].at[c], local_sems.at[k]) for k in range(n)]
        for cp in local:
            cp.start()

        def copy(k, slot):
            return pltpu.make_async_remote_copy(src_ref=p_refs[k], dst_ref=o_refs[k].at[slot], send_sem=send_sems.at[k],
                                                recv_sem=recv_sems.at[k], device_id=(x, y, 1 - c), device_id_type=MESH)

        for k in range(n):
            copy(k, c).start()
        for k in range(n):
            copy(k, 1 - c).wait_recv()
        for k in range(n):
            copy(k, c).wait_send()
        for cp in local:
            cp.wait()

    return _pcall(
        body, name="swap_halves", in_specs=[whole] * n, out_specs=[whole] * n,
        out_shape=[jax.ShapeDtypeStruct((2,) + t.shape, t.dtype) for t in pieces],
        scratch_shapes=[pltpu.SemaphoreType.DMA((n,)), pltpu.SemaphoreType.DMA((n,)), pltpu.SemaphoreType.DMA((n,))],
        compiler_params=_cparams(),
    )(*pieces)


def adamw(w, g, m, v, name):
    rows, cols = w.shape
    tm = rows
    for t in (256, 352, 128, 144, 64, 32, 16, 8):
        if rows % t == 0:
            tm = t
            break

    def fn(i, nrow, wv, gv, mv, vv):
        mn = ADAM_B1 * mv + (1.0 - ADAM_B1) * gv
        vn = ADAM_B2 * vv + (1.0 - ADAM_B2) * (gv * gv)
        m_hat = mn / (1.0 - ADAM_B1 ** ADAM_STEP)
        v_hat = vn / (1.0 - ADAM_B2 ** ADAM_STEP)
        delta = -ADAM_LR * (m_hat / (jnp.sqrt(v_hat) + ADAM_EPS) + ADAM_WD * wv)
        return delta, mn, vn

    return ew(fn, name, rows, tm, 1, [(t, "row", cols, 0) for t in (w, g, m, v)], [(cols, F32, cols)] * 3)


BIG = ("w_in", "w_out", "w_up", "w_down")
REST = ("w_out", "w_up", "w_down")
REST_ROWS = (512, 1408, 704)
W_IN_ROWS, W_IN_ROWS_PADDED = 1544, 1568
SMALL = ("norm1_w", "ssm_conv_w", "ssm_conv_b", "a_log_f", "a_log_b", "dt_bias_f", "dt_bias_b", "d_skip",
         "ssm_norm_w", "norm2_w", "ffn_conv_w", "ffn_conv_b", "final_norm_w")
WEIGHTS = ("norm1_w", "w_in", "ssm_conv_w", "ssm_conv_b", "a_log_f", "a_log_b", "dt_bias_f", "dt_bias_b", "d_skip",
           "ssm_norm_w", "w_out", "norm2_w", "w_up", "ffn_conv_w", "ffn_conv_b", "w_down", "final_norm_w")
INPUTS = ("x",) + WEIGHTS + ("loss_target",) + tuple("m_" + n for n in WEIGHTS) + tuple("v_" + n for n in WEIGHTS)


def _flat_rows(parts, width, rows):
    flat = jnp.concatenate([p.reshape(-1) for p in parts])
    return jnp.pad(flat, (0, rows * width - flat.shape[0])).reshape(rows, width)


def _split_flat(flat, shapes):
    out, pos = [], 0
    flat = flat.reshape(-1)
    for shp in shapes:
        n = int(np.prod(shp))
        out.append(flat[pos:pos + n].reshape(shp))
        pos += n
    return out


def _col_shards(t, nshard):
    r, c = t.shape
    return t.reshape(r, nshard, c // nshard).transpose(1, 0, 2).reshape(nshard, -1, D)


def _row_shards(t, nshard):
    r, c = t.shape
    return t.reshape(nshard, -1, D)


def kernel(x, norm1_w, w_in, ssm_conv_w, ssm_conv_b, a_log_f, a_log_b, dt_bias_f, dt_bias_b, d_skip, ssm_norm_w, w_out, norm2_w, w_up, ffn_conv_w, ffn_conv_b, w_down, final_norm_w, loss_target, m_norm1_w, m_w_in, m_ssm_conv_w, m_ssm_conv_b, m_a_log_f, m_a_log_b, m_dt_bias_f, m_dt_bias_b, m_d_skip, m_ssm_norm_w, m_w_out, m_norm2_w, m_w_up, m_ffn_conv_w, m_ffn_conv_b, m_w_down, m_final_norm_w, v_norm1_w, v_w_in, v_ssm_conv_w, v_ssm_conv_b, v_a_log_f, v_a_log_b, v_dt_bias_f, v_dt_bias_b, v_d_skip, v_ssm_norm_w, v_w_out, v_norm2_w, v_w_up, v_ffn_conv_w, v_ffn_conv_b, v_w_down, v_final_norm_w):
    p = dict(zip(INPUTS, (x, norm1_w, w_in, ssm_conv_w, ssm_conv_b, a_log_f, a_log_b, dt_bias_f, dt_bias_b, d_skip, ssm_norm_w, w_out, norm2_w, w_up, ffn_conv_w, ffn_conv_b, w_down, final_norm_w, loss_target, m_norm1_w, m_w_in, m_ssm_conv_w, m_ssm_conv_b, m_a_log_f, m_a_log_b, m_dt_bias_f, m_dt_bias_b, m_d_skip, m_ssm_norm_w, m_w_out, m_norm2_w, m_w_up, m_ffn_conv_w, m_ffn_conv_b, m_w_down, m_final_norm_w, v_norm1_w, v_w_in, v_ssm_conv_w, v_ssm_conv_b, v_a_log_f, v_a_log_b, v_dt_bias_f, v_dt_bias_b, v_d_skip, v_ssm_norm_w, v_w_out, v_norm2_w, v_w_up, v_ffn_conv_w, v_ffn_conv_b, v_w_down, v_final_norm_w)))
    x = p["x"][0]
    tgt = p["loss_target"][0]
    s = x.shape[0]
    chip = 2 * lax.axis_index("x") + lax.axis_index("y")

    own_slot = lambda land, mine, slot: lax.dynamic_update_slice_in_dim(land, mine[None], slot, axis=0)
    src_in = p["w_in"][0].reshape(-1, D).astype(BF16)
    src_rest = jnp.concatenate([p[n][0].reshape(-1, D) for n in REST], axis=0).astype(BF16)
    small_w = _flat_rows([p["ssm_conv_w"][0], p["ffn_conv_w"][0]], 128, 48)
    gather_in, gather_rest = gather_start([src_in, small_w], [src_rest])
    (src_in, small_w), (wg_in, sg) = gather_wait(gather_in, "gather_wait_in")
    w_in = own_slot(wg_in, src_in, chip).reshape(4, D, -1).transpose(1, 0, 2).reshape(D, -1)
    sg = own_slot(sg, small_w, chip)
    o = np.cumsum((0,) + REST_ROWS)
    n_in = w_in.shape[1]
    n_main = 6 * D
    w_main = w_in[:, :n_main]
    w_dt = jnp.pad(w_in[:, n_main:], ((0, 0), (0, 128 - (n_in - n_main))))
    sgf = sg.reshape(4, -1)
    n_sc, n_fc = p["ssm_conv_w"].shape[1], p["ffn_conv_w"].shape[1]
    ssm_cw = sgf[:, :n_sc * 3].reshape(-1, 3).T
    ffn_cw = sgf[:, n_sc * 3:(n_sc + n_fc) * 3].reshape(-1, 3).T
    ssm_cb, ffn_cb = p["ssm_conv_b"], p["ffn_conv_b"]
    n1w, n2w, snw, fnw = p["norm1_w"], p["norm2_w"], p["ssm_norm_w"], p["final_norm_w"].reshape(1, D)

    h1, = ew(lambda i, n, xv, w: _rms_fwd(xv, w), "rms1", s, 256, 1,
             [(x, "row", D, 0), (n1w, "const", D, 0)], [(D, BF16, D)])
    proj = matmul(h1, w_main, "nn", "in_proj")
    proj_dt = matmul(h1, w_dt, "nn", "in_proj_dt")
    tabs = _rope_tables(s)
    attn, lse = attn_fwd_all(proj, tabs, "attn_fwd")

    def conv_silu_fn(i, n, xv, xp, xn, w, b):
        return _silu(w[0:1] * _shift_down(xv, xp, i) + w[1:2] * xv + w[2:3] * _shift_up(xv, xn, i, n) + b)

    xbc_act, = ew(conv_silu_fn, "ssm_conv", s, 256, 2,
                  [(proj, "row", D, 4), (proj, "prev", D, 4), (proj, "next", D, 4),
                   (ssm_cw, "const", D, 0), (ssm_cb, "const", D, 0)], [(2 * D, F32, D)])
    dt_bias = jnp.pad(jnp.concatenate([p["dt_bias_f"], p["dt_bias_b"]], axis=1), ((0, 0), (0, 96)))

    def softplus_fn(i, n, r, b):
        t = r + b
        return jnp.maximum(t, 0.0) + jnp.log(1.0 + jnp.exp(-jnp.abs(t)))

    dt, = ew(softplus_fn, "dt_softplus", s, 512, 1, [(proj_dt, "row", 128, 0), (dt_bias, "const", 128, 0)], [(128, F32, 128)])
    d_exp = jnp.repeat(p["d_skip"], HD, axis=1)
    ssd = []
    for k, (a_log, rev) in enumerate(((p["a_log_f"], False), (p["a_log_b"], True))):
        dt_k = dt[:, 16 * k:16 * k + 16]
        a_head = -jnp.exp(a_log)
        dt_exp = jnp.repeat(dt_k, HD, axis=1)
        dtt = jnp.pad(dt_k.T.reshape(8, 2, s), ((0, 0), (0, 6), (0, 0)))
        a_exp = jnp.repeat(a_head, HD, axis=1)
        a_rows = jnp.broadcast_to(jnp.pad(a_head.reshape(8, 2), ((0, 0), (0, 6)))[:, :, None], (8, 8, 128))
        ssd.append(dict(dt_exp=dt_exp, dtt=dtt, a_exp=a_exp, a_rows=a_rows, rev=rev))
    for t, (y_k, hs_k) in zip(ssd, ssd_fwd(xbc_act, ssd, "ssd_fwd")):
        t["y"], t["hs"] = y_k, hs_k

    def gate_fn(i, n, yf, yb, xs, z, dsk, w):
        g = (yf + yb + dsk * xs) * _silu(z)
        return g * _group_norm_stats(g) * w

    ssm_out, = ew(gate_fn, "ssm_gate_norm", s, 256, 1,
                  [(ssd[0]["y"], "row", D, 0), (ssd[1]["y"], "row", D, 0), (xbc_act, "row", D, 0), (proj, "row", D, 3),
                   (d_exp, "const", D, 0), (snw, "const", D, 0)], [(D, F32, D)])
    mix = jnp.concatenate([attn, ssm_out], axis=1).astype(BF16)
    (src_rest,), (wg_rest,) = gather_wait(gather_rest, "gather_wait_rest", after=mix)
    wg_rest = own_slot(wg_rest, src_rest, chip)
    w_out = wg_rest[:, o[0]:o[1]].reshape(-1, D)
    w_up = wg_rest[:, o[1]:o[2]].reshape(4, D, -1).transpose(1, 0, 2).reshape(D, -1)
    w_down = wg_rest[:, o[2]:o[3]].reshape(-1, D)
    mix_w = matmul(mix, w_out, "nn", "out_proj")

    def res_rms_fn(i, n, xv, mw, w):
        x1v = xv + mw
        return x1v, _rms_fwd(x1v, w)

    x1, h2 = ew(res_rms_fn, "res_rms2", s, 256, 1, [(x, "row", D, 0), (mix_w, "row", D, 0), (n2w, "const", D, 0)],
                [(D, F32, D), (D, BF16, D)])
    hw = matmul(h2, w_up, "nn", "ffn_up")
    fw = D_FF // 2
    nfb = D_FF // fw
    ffn_conv_ins = [(hw, "row", fw, 0), (hw, "prev", fw, 0), (hw, "next", fw, 0),
                    (hw, "row", fw, nfb), (hw, "prev", fw, nfb), (hw, "next", fw, nfb),
                    (ffn_cw, "const", fw, 0), (ffn_cw, "const", fw, nfb), (ffn_cb, "const", fw, 0), (ffn_cb, "const", fw, nfb)]

    def ffn_conv(i, n, g, gp, gn, u, up_, un, wg_, wu, bg, bu):
        gs = (_shift_down(g, gp, i), g, _shift_up(g, gn, i, n))
        us = (_shift_down(u, up_, i), u, _shift_up(u, un, i, n))
        gate = wg_[0:1] * gs[0] + wg_[1:2] * gs[1] + wg_[2:3] * gs[2] + bg
        upv = wu[0:1] * us[0] + wu[1:2] * us[1] + wu[2:3] * us[2] + bu
        return gate, upv, gs, us

    def glu_fn(i, n, *blocks):
        gate, upv, _, _ = ffn_conv(i, n, *blocks)
        return _silu(gate) * upv

    act, = ew(glu_fn, "ffn_conv_glu", s, 256, nfb, ffn_conv_ins, [(D_FF, BF16, fw)])
    ffn = matmul(act, w_down, "nn", "ffn_down")

    def head_fn(i, n, x1v, fv, tv, w):
        x2 = x1v + fv
        r = lax.rsqrt(jnp.mean(x2 * x2, axis=-1, keepdims=True) + EPS)
        xh = x2 * r
        diff = xh * w - tv
        loss = 0.5 * jnp.sum(jnp.mean(diff * diff, axis=-1, keepdims=True), axis=0, keepdims=True)
        dout = diff * (1.0 / D)
        dxh = dout * w
        dx2 = r * (dxh - xh * jnp.mean(dxh * xh, axis=-1, keepdims=True))
        return dx2, jnp.broadcast_to(loss, (1, 128)), _colsum(dout * xh)

    dx2, loss_acc, g_fnw = ew(head_fn, "loss_head", s, 256, 1,
                              [(x1, "row", D, 0), (ffn, "row", D, 0), (tgt, "row", D, 0), (fnw, "const", D, 0)],
                              [(D, F32, D)], [(128, 128), (D, D)])
    loss = lax.psum(loss_acc[0, 0], ("x", "y", "c"))

    g_w_down = matmul(act, dx2, "tn", "d_w_down")
    dact = matmul(dx2, w_down, "nt", "d_act")

    def glu_bwd_fn(i, n, *blocks):
        gate, upv, gs, us = ffn_conv(i, n, *blocks[:-1])
        da = blocks[-1]
        dg = da * upv * _dsilu(gate)
        du = da * _silu(gate)
        return (dg, du) + tuple(_colsum(dg * t) for t in gs) + tuple(_colsum(du * t) for t in us) + (_colsum(dg), _colsum(du))

    res = ew(glu_bwd_fn, "ffn_glu_bwd", s, 256, nfb, ffn_conv_ins + [(dact, "row", fw, 0)],
             [(D_FF, F32, fw)] * 2, [(D_FF, fw)] * 8)
    du_g, du_u = res[0], res[1]
    g_ffn_cw = jnp.concatenate([jnp.concatenate(res[2:5], axis=0), jnp.concatenate(res[5:8], axis=0)], axis=1).T
    g_ffn_cb = jnp.concatenate([res[8], res[9]], axis=1)

    def conv_t_fn(i, n, dv, dp, dn, w):
        return w[0:1] * _shift_up(dv, dn, i, n) + w[1:2] * dv + w[2:3] * _shift_down(dv, dp, i)

    def conv_t(du, cw, off, width, ncol, name):
        return ew(conv_t_fn, name, s, 256, ncol,
                  [(du, "row", width, 0), (du, "prev", width, 0), (du, "next", width, 0), (cw, "const", width, off)],
                  [(du.shape[1], F32, width)])[0]

    dhw_g = conv_t(du_g, ffn_cw, 0, fw, nfb, "ffn_conv_t_gate")
    dhw_u = conv_t(du_u, ffn_cw, nfb, fw, nfb, "ffn_conv_t_up")
    g_w_up = jnp.concatenate([matmul(h2, dhw_g, "tn", "d_w_up_gate"), matmul(h2, dhw_u, "tn", "d_w_up_up")], axis=1)
    dh2_a = matmul(dhw_g, w_up[:, :D_FF], "nt", "d_h2_gate")
    dh2_b = matmul(dhw_u, w_up[:, D_FF:], "nt", "d_h2_up")

    def res_rms_bwd_fn(i, n, dres, da, db, xin, w):
        dx, dw = _rms_bwd(da + db, xin, w)
        return dres + dx, dw

    dx1, g_n2w = ew(res_rms_bwd_fn, "res_rms2_bwd", s, 256, 1,
                    [(dx2, "row", D, 0), (dh2_a, "row", D, 0), (dh2_b, "row", D, 0), (x1, "row", D, 0), (n2w, "const", D, 0)],
                    [(D, F32, D)], [(D, D)])

    g_w_out = matmul(mix, dx1, "tn", "d_w_out")
    to_pieces = lambda t: t.astype(BF16).reshape(4, 2, -1, D)
    shards_rest = jnp.concatenate([_row_shards(g_w_out, 4), _col_shards(g_w_up, 4), _row_shards(g_w_down, 4)], axis=1)
    scatter_rest, token = scatter_start([to_pieces(shards_rest)], [], "scatter_start_rest")
    w_out_after = w_out + token[0:1, 0:1].astype(BF16)
    dmix = matmul(dx1, w_out_after, "nt", "d_mix")
    ii, jj = np.arange(D)[:, None] // HD, np.arange(D)[None, :] // HD
    seg = jnp.asarray(ii == jj, BF16)

    def gate_bwd_fn(i, n, dout, yf, yb, xs, z, dsk, w, segm):
        yt = yf + yb + dsk * xs
        sz = _silu(z)
        g = yt * sz
        r = _group_norm_stats(g)
        gh = g * r
        dn = dout * w
        dg = r * (dn - gh * _group_mean(dn * gh))
        dy = dg * sz
        dsk_lane = jnp.broadcast_to(_colsum(dy * xs), (8, D))
        return dy, dg * yt * _dsilu(z), _colsum(dout * gh), sum(_dot(q, segm) for q in _parts(dsk_lane, 2))[0:1]

    dy, dz, g_snw, g_dskip_l = ew(
        gate_bwd_fn, "ssm_gate_norm_bwd", s, 256, 1,
        [(dmix, "row", D, 1), (ssd[0]["y"], "row", D, 0), (ssd[1]["y"], "row", D, 0), (xbc_act, "row", D, 0),
         (proj, "row", D, 3), (d_exp, "const", D, 0), (snw, "const", D, 0), (seg, "const", D, 0)],
        [(D, F32, D)] * 2, [(D, D)] * 2)
    sb = ssd_bwd(xbc_act, ssd, dy, "ssd_bwd")

    def dxbc_act_fn(i, n, dxf, dxb, dyv, dsk, dbf, dbb, dcf, dcb_):
        db, dc = dbf + dbb, dcf + dcb_
        db = [db[:, 256 * g:256 * g + 128] + db[:, 256 * g + 128:256 * g + 256] for g in range(4)]
        dc = [dc[:, 256 * g:256 * g + 128] + dc[:, 256 * g + 128:256 * g + 256] for g in range(4)]
        return jnp.concatenate([dxf + dxb + dyv * dsk] + db + dc, axis=1)

    dxbc_act, = ew(dxbc_act_fn, "d_xbc_act", s, 256, 1,
                   [(sb[0][0], "row", D, 0), (sb[1][0], "row", D, 0), (dy, "row", D, 0), (d_exp, "const", D, 0),
                    (sb[0][2], "row", D, 0), (sb[1][2], "row", D, 0), (sb[0][3], "row", D, 0), (sb[1][3], "row", D, 0)],
                   [(2 * D, F32, 2 * D)])

    def silu_bwd_fn(i, n, xv, xp, xn, w, b, da):
        xs3 = (_shift_down(xv, xp, i), xv, _shift_up(xv, xn, i, n))
        du = da * _dsilu(w[0:1] * xs3[0] + w[1:2] * xs3[1] + w[2:3] * xs3[2] + b)
        return (du,) + tuple(_colsum(du * t) for t in xs3) + (_colsum(du),)

    res = ew(silu_bwd_fn, "ssm_conv_bwd", s, 256, 2,
             [(proj, "row", D, 4), (proj, "prev", D, 4), (proj, "next", D, 4), (ssm_cw, "const", D, 0),
              (ssm_cb, "const", D, 0), (dxbc_act, "row", D, 0)], [(2 * D, F32, D)], [(2 * D, D)] * 4)
    g_ssm_cw = jnp.concatenate(res[1:4], axis=0).T
    g_ssm_cb = res[4]
    dxbc = conv_t(res[0], ssm_cw, 0, D, 2, "ssm_conv_t")
    ddt = jnp.pad(jnp.concatenate([sb[0][1][:, ::HD], sb[1][1][:, ::HD]], axis=1), ((0, 0), (0, 96)))

    def dt_bwd_fn(i, n, dd, r, b):
        dr = dd * _sigmoid(r + b)
        return dr, _colsum(dr)

    dproj_dt, g_dt_bias = ew(dt_bwd_fn, "dt_softplus_bwd", s, 512, 1,
                             [(ddt, "row", 128, 0), (proj_dt, "row", 128, 0), (dt_bias, "const", 128, 0)],
                             [(128, F32, 128)], [(128, 128)])
    g_a_log = [t[4][:, 0, ::HD].reshape(1, 16) for t in sb]

    dq, dk, dv = attn_bwd_all(proj, tabs, dmix, attn, lse, "attn_bwd")

    dproj = jnp.concatenate([dq, dk, dv, dz, dxbc], axis=1).astype(BF16)
    g_w_in = jnp.concatenate([matmul(h1, dproj, "tn", "d_w_in"), matmul(h1, dproj_dt, "tn", "d_w_in_dt")[:, :n_in - n_main]], axis=1)
    shards_in = jnp.pad(_col_shards(g_w_in, 4), ((0, 0), (0, W_IN_ROWS_PADDED - W_IN_ROWS), (0, 0)))
    scatter_in, token = scatter_start([to_pieces(shards_in)], [], "scatter_start_in")
    w_main_after = w_main + token[0:1, 0:1].astype(BF16)
    dh1_a = matmul(dproj, w_main_after, "nt", "d_h1")
    dh1_b = matmul(dproj_dt, w_dt, "nt", "d_h1_dt")
    grad_x, g_n1w = ew(res_rms_bwd_fn, "rms1_bwd", s, 256, 1,
                       [(dx1, "row", D, 0), (dh1_a, "row", D, 0), (dh1_b, "row", D, 0), (x, "row", D, 0), (n1w, "const", D, 0)],
                       [(D, F32, D)], [(D, D)])

    small_g = {"norm1_w": g_n1w, "ssm_conv_w": g_ssm_cw, "ssm_conv_b": g_ssm_cb, "a_log_f": g_a_log[0], "a_log_b": g_a_log[1],
               "dt_bias_f": g_dt_bias[:, :16], "dt_bias_b": g_dt_bias[:, 16:32], "d_skip": g_dskip_l[:, ::HD],
               "ssm_norm_w": g_snw, "norm2_w": g_n2w, "ffn_conv_w": g_ffn_cw, "ffn_conv_b": g_ffn_cb, "final_norm_w": g_fnw}
    small_shapes = [small_g[n].shape for n in SMALL]
    scatter_small, _ = scatter_start([], [_flat_rows([small_g[n] for n in SMALL], 128, SMALL_ROWS)], "scatter_start_small")
    (sent_rest,), (got_rest,) = scatter_wait(scatter_rest, "scatter_wait_rest")
    (sent_in,), (got_in,) = scatter_wait(scatter_in, "scatter_wait_in")
    (sent_small,), (got_small,) = scatter_wait(scatter_small, "scatter_wait_small")
    core = lax.axis_index("c")

    def sum8_fn(i, n, *v):
        t = v[0].astype(F32)
        for u in v[1:]:
            t = t + u.astype(F32)
        return t

    def sum_pieces(sent, got, tm, ncol, name):
        rows = got.shape[1]
        mine = lax.dynamic_slice(sent, (chip, core, 0, 0), (1, 1, rows, D)).reshape(rows, D)
        w = D // ncol
        ins = [(mine, "row", w, 0)] + [(got.reshape(8 * rows, D), "row", w, 0, k * (rows // tm)) for k in range(1, 8)]
        return ew(sum8_fn, name, rows, tm, ncol, ins, [(D, F32, w)])[0]

    rows_rest = int(o[3])
    got_small = own_slot(got_small, sent_small, 2 * chip + core)
    small_sum, = ew(sum8_fn, "sum_small", SMALL_ROWS, SMALL_ROWS, 1,
                    [(got_small.reshape(8 * SMALL_ROWS, 128), "row", 128, 0, k) for k in range(8)], [(128, F32, 128)])
    g_rest, g_in = swap_halves([sum_pieces(sent_rest, got_rest, rows_rest // 4, 1, "sum_pieces_rest"),
                                sum_pieces(sent_in, got_in, W_IN_ROWS_PADDED // 2, 2, "sum_pieces_in")])
    g_rest = g_rest.reshape(-1, D)
    grads = {n: g_rest[o[k]:o[k + 1]].reshape(p[n].shape) for k, n in enumerate(REST)}
    grads["w_in"] = g_in.reshape(-1, D)[:W_IN_ROWS].reshape(p["w_in"].shape)
    for n, g in zip(SMALL, _split_flat(small_sum, small_shapes)):
        if n in ("ssm_conv_w", "ffn_conv_w"):
            rows = p[n].shape[1]
            g = lax.dynamic_slice_in_dim(g, chip * rows, rows, axis=0)
        grads[n] = g.reshape(p[n].shape)

    delta, new_m, new_v = {}, {}, {}
    for n in BIG:
        shp = p[n].shape
        r = [t.reshape(shp[1:]) for t in (p[n], grads[n], p["m_" + n], p["v_" + n])]
        delta[n], new_m[n], new_v[n] = [t.reshape(shp) for t in adamw(*r, "adamw_" + n)]
    shapes = [p[n].shape for n in SMALL]
    total = sum(int(np.prod(sh)) for sh in shapes)
    rows = -(-total // 1024) * 8
    packs = [_flat_rows([t[n] for n in SMALL], 128, rows)
             for t in (p, grads, {n: p["m_" + n] for n in SMALL}, {n: p["v_" + n] for n in SMALL})]
    for dst, t in zip((delta, new_m, new_v), adamw(*packs, "adamw_small")):
        for n, u in zip(SMALL, _split_flat(t, shapes)):
            dst[n] = u
    return (loss, grad_x[None], *[grads[n] for n in WEIGHTS], *[delta[n] for n in WEIGHTS],
            *[new_m[n] for n in WEIGHTS], *[new_v[n] for n in WEIGHTS])
```

```python
import numpy as np
import jax
import jax.numpy as jnp
from jax import lax
from jax.experimental import pallas as pl
from jax.experimental.pallas import tpu as pltpu

F32, BF16 = jnp.float32, jnp.bfloat16
MESH = pl.DeviceIdType.MESH
V7X_VMEM_LIMIT = 56 * 1024 * 1024

D = 1024
HD = 64
EPS = 1e-6
CHUNK = 128
D_FF = 2816
ROPE_DIM = 16
ROPE_THETA = 500000.0
PATTERN_DILATIONS = (1, 4, 16)
BAND = 64
SMALL_ROWS = 280
ADAM_LR, ADAM_B1, ADAM_B2, ADAM_EPS, ADAM_WD, ADAM_STEP = 0.001, 0.9, 0.999, 1e-08, 0.01, 10

NN = (((1,), (0,)), ((), ()))
NT = (((1,), (1,)), ((), ()))
TN = (((0,), (0,)), ((), ()))


def _pcall(body, **kw):
    return pl.pallas_call(body, **kw)


def _cparams(sem=None):
    return pltpu.CompilerParams(dimension_semantics=sem, vmem_limit_bytes=V7X_VMEM_LIMIT)


def _dot(a, b, dims=NN):
    return lax.dot_general(a, b, dims, preferred_element_type=F32)


def _pick(n, cap):
    if n <= cap:
        return n
    best = 0
    for t in range(128, cap + 1, 128):
        if n % t == 0:
            best = t
    assert best, (n, cap)
    return best


def _iota(shape, dim):
    return lax.broadcasted_iota(jnp.int32, shape, dim)


def _parts(x, n):
    out, r = [], x
    for _ in range(n):
        h = r.astype(BF16)
        out.append(h)
        r = r - h.astype(F32)
    return out


def _sigmoid(x):
    return 1.0 / (1.0 + jnp.exp(-x))


def _silu(x):
    return x * _sigmoid(x)


def _dsilu(x):
    s = _sigmoid(x)
    return s * (1.0 + x * (1.0 - s))


def matmul(a, b, mode, name, out_dtype=F32, after=None, b_k_off=0):
    if mode == "nn":
        (m, k), (_, n) = a.shape, b.shape
    elif mode == "nt":
        (m, k), (n, _) = a.shape, b.shape
    else:
        (k, m), (_, n) = a.shape, b.shape
    tm, tn, tk = _pick(m, 1408), _pick(n, 1408), _pick(k, 1408)
    nk = k // tk
    dims = {"nn": NN, "nt": NT, "tn": TN}[mode]
    a_spec = pl.BlockSpec((tk, tm), lambda i, j, kk: (kk, i)) if mode == "tn" else pl.BlockSpec((tm, tk), lambda i, j, kk: (i, kk))
    b_spec = pl.BlockSpec((tn, tk), lambda i, j, kk: (j, kk + b_k_off)) if mode == "nt" else pl.BlockSpec((tk, tn), lambda i, j, kk: (kk, j))
    extra = [] if after is None else [after]

    def body(a_ref, b_ref, *rest):
        o_ref, acc = rest[len(extra)], rest[len(extra) + 1:]
        part = _dot(a_ref[...].astype(BF16), b_ref[...].astype(BF16), dims)
        if nk == 1:
            o_ref[...] = part.astype(o_ref.dtype)
            return
        acc_ref, kk = acc[0], pl.program_id(2)

        @pl.when(kk == 0)
        def _():
            acc_ref[...] = part

        @pl.when((kk > 0) & (kk < nk - 1))
        def _():
            acc_ref[...] += part

        @pl.when(kk == nk - 1)
        def _():
            o_ref[...] = (acc_ref[...] + part).astype(o_ref.dtype)

    return _pcall(
        body, name=name, grid=(m // tm, n // tn, nk), in_specs=[a_spec, b_spec] + [pl.BlockSpec(memory_space=pl.ANY)] * len(extra),
        out_specs=pl.BlockSpec((tm, tn), lambda i, j, kk: (i, j)),
        out_shape=jax.ShapeDtypeStruct((m, n), out_dtype),
        scratch_shapes=[pltpu.VMEM((tm, tn), F32)] if nk > 1 else [],
        compiler_params=_cparams(("parallel", "parallel", "arbitrary")),
    )(a, b, *extra)


def ew(fn, name, rows, tm, ncol, ins, outs, accs=()):
    nrow = rows // tm
    r8 = tm // 8
    in_specs, arrays = [], []
    for ent in ins:
        arr, kind, w, off = ent[:4]
        roff = ent[4] if len(ent) > 4 else 0
        if kind == "row":
            spec = pl.BlockSpec((tm, w), lambda j, i, off=off, roff=roff: (i + roff, j + off))
        elif kind == "const":
            spec = pl.BlockSpec((arr.shape[0], w), lambda j, i, off=off: (0, j + off))
        elif kind == "prev":
            spec = pl.BlockSpec((8, w), lambda j, i, off=off: (jnp.maximum(i * r8 - 1, 0), j + off))
        else:
            spec = pl.BlockSpec((8, w), lambda j, i, off=off: (jnp.minimum((i + 1) * r8, rows // 8 - 1), j + off))
        in_specs.append(spec)
        arrays.append(arr)
    out_specs = [pl.BlockSpec((tm, w), lambda j, i: (i, j)) for (_, _, w) in outs]
    out_shape = [jax.ShapeDtypeStruct((rows, c), dt) for (c, dt, _) in outs]
    out_specs += [pl.BlockSpec((1, w), lambda j, i: (0, j)) for (_, w) in accs]
    out_shape += [jax.ShapeDtypeStruct((1, c), F32) for (c, _) in accs]
    nin, nout = len(ins), len(outs)

    def body(*refs):
        i = pl.program_id(1)
        res = fn(i, nrow, *[r[...] for r in refs[:nin]])
        if not isinstance(res, (tuple, list)):
            res = (res,)
        for r, v in zip(refs[nin:nin + nout], res[:nout]):
            r[...] = v.astype(r.dtype)
        if accs:
            acc_refs = refs[nin + nout:]

            @pl.when(i == 0)
            def _():
                for r in acc_refs:
                    r[...] = jnp.zeros_like(r)

            for r, v in zip(acc_refs, res[nout:]):
                r[...] += v

    res = _pcall(
        body, name=name, grid=(ncol, nrow), in_specs=in_specs, out_specs=out_specs, out_shape=out_shape,
        compiler_params=_cparams(("parallel", "arbitrary")),
    )(*arrays)
    return res


def _shift_down(x, prev8, i):
    first = jnp.where(i == 0, 0.0, prev8[7:8, :])
    return jnp.where(_iota(x.shape, 0) == 0, first, pltpu.roll(x, 1, 0))


def _shift_up(x, next8, i, nrow):
    last = jnp.where(i == nrow - 1, 0.0, next8[0:1, :])
    return jnp.where(_iota(x.shape, 0) == x.shape[0] - 1, last, pltpu.roll(x, x.shape[0] - 1, 0))


def _colsum(x):
    return jnp.sum(x, axis=0, keepdims=True)


def _rms_fwd(x, w):
    r = lax.rsqrt(jnp.mean(x * x, axis=-1, keepdims=True) + EPS)
    return x * r * w


def _rms_bwd(dy, x, w):
    r = lax.rsqrt(jnp.mean(x * x, axis=-1, keepdims=True) + EPS)
    xh = x * r
    dxh = dy * w
    dx = r * (dxh - xh * jnp.mean(dxh * xh, axis=-1, keepdims=True))
    return dx, _colsum(dy * xh)


def _rope_tables(s):
    half = ROPE_DIM // 2
    inv_freq = jnp.power(ROPE_THETA, -jnp.arange(half, dtype=F32) * 2.0 / ROPE_DIM)
    ang = jnp.arange(s, dtype=F32)[:, None] * inv_freq[None, :]
    cos, sin = jnp.cos(ang), jnp.sin(ang)
    one, zero = jnp.ones((s, HD - ROPE_DIM), F32), jnp.zeros((s, HD - ROPE_DIM), F32)
    z8 = jnp.zeros((s, half), F32)
    c = jnp.concatenate([cos, cos, one], axis=1)
    sa = jnp.concatenate([-sin, z8, zero], axis=1)
    sb = jnp.concatenate([z8, sin, zero], axis=1)
    return [jnp.tile(t, (1, 2)) for t in (c, sa, sb)]


ATTN_CHUNK = 1024


def _attn_plan(s):
    plan = []
    for d in PATTERN_DILATIONS:
        per_res = ATTN_CHUNK // d
        tq = min(128, per_res)
        plan.append((d, tq, min(s // d, tq + 2 * BAND), per_res // tq, s // d))
    return plan


def _rows(start, size, d):
    return pl.ds(start, size) if d == 1 else pl.ds(start, size, stride=d)


def _for_tiles(chunk, pat, fn):
    d, tq, win, nblk, seq_len = pat
    for b in range(nblk):
        t0 = chunk * (ATTN_CHUNK // d) + b * tq
        kloc = jnp.clip(t0 - BAND, 0, seq_len - win)
        valid = jnp.abs(kloc + _iota((tq, win), 1) - (t0 + _iota((tq, win), 0))) <= BAND
        valid = jnp.concatenate([valid, valid], axis=0)
        if d == 1:
            fn(b * tq, pl.multiple_of(kloc, BAND), valid)
        else:
            def step(r, carry, qoff=d * b * tq, koff=d * kloc, valid=valid):
                fn(qoff + r, koff + r, valid)
                return carry
            lax.fori_loop(0, d, step, 0, unroll=min(d, 4))


def _stack_heads(x, head0):
    zero = jnp.zeros_like(x)
    return jnp.concatenate([jnp.where(head0, x, zero), jnp.where(head0, zero, x)], axis=0)


def _rope_pair(x, c, sa, sb):
    n = x.shape[1]
    return x * c + pltpu.roll(x, n - 8, 1) * sa + pltpu.roll(x, 8, 1) * sb


def _rope_pair_t(dy, c, sa, sb):
    n = dy.shape[1]
    return dy * c + pltpu.roll(dy * sa, 8, 1) + pltpu.roll(dy * sb, n - 8, 1)


def _attn_specs(s):
    whole = lambda off: pl.BlockSpec((s, 128), lambda p, c: (0, off + p))
    table = pl.BlockSpec((s, 128), lambda p, c: (0, 0))
    chunk = pl.BlockSpec((ATTN_CHUNK, 128), lambda p, c: (c, p))
    return whole, table, chunk


def attn_fwd_all(proj, tabs, name):
    s = proj.shape[0]
    plan = _attn_plan(s)
    whole, table, chunk_spec = _attn_specs(s)

    def body(q_ref, k_ref, v_ref, c_ref, sa_ref, sb_ref, o_ref, lse_ref, qs, ks, acc_s, m_s, l_s):
        chunk = pl.program_id(1)

        @pl.when(chunk == 0)
        def _():
            qs[...] = _rope_pair(q_ref[...], c_ref[...], sa_ref[...], sb_ref[...]) * (HD ** -0.5)
            ks[...] = _rope_pair(k_ref[...], c_ref[...], sa_ref[...], sb_ref[...])

        base = pl.multiple_of(chunk * ATTN_CHUNK, ATTN_CHUNK)
        for pi, pat in enumerate(plan):
            d, tq, win = pat[:3]
            head0 = _iota((tq, 128), 1) < HD

            def tile(qrow, krow, valid, pi=pi, d=d, tq=tq, win=win, head0=head0):
                qv = qs[_rows(base + qrow, tq, d), :].astype(BF16)
                kw = ks[_rows(krow, win, d), :].astype(BF16)
                vw = v_ref[_rows(krow, win, d), :].astype(BF16)
                v_ones = jnp.concatenate([vw, jnp.ones_like(vw)], axis=1)
                sc = jnp.where(valid, _dot(_stack_heads(qv, head0), kw, NT), -1e30)
                mh = jnp.max(sc, axis=1, keepdims=True)
                pv = _dot(jnp.exp(sc - mh).astype(BF16), v_ones)
                acc_s[pi, _rows(qrow, tq, d), :] = jnp.where(head0, pv[:tq, :128], pv[tq:, :128])
                m_s[pi, _rows(qrow, tq, d), :] = jnp.where(head0, mh[:tq], mh[tq:])
                l_s[pi, _rows(qrow, tq, d), :] = jnp.where(head0, pv[:tq, 128:], pv[tq:, 128:])

            _for_tiles(chunk, pat, tile)
        m_all = jnp.maximum(jnp.maximum(m_s[0], m_s[1]), m_s[2])
        e = [jnp.exp(m_s[k] - m_all) for k in range(3)]
        den = e[0] * l_s[0] + e[1] * l_s[1] + e[2] * l_s[2]
        o_ref[...] = (e[0] * acc_s[0] + e[1] * acc_s[1] + e[2] * acc_s[2]) / den
        lse_ref[...] = m_all + jnp.log(den)

    stat = pltpu.VMEM((3, ATTN_CHUNK, 128), F32)
    return _pcall(
        body, name=name, grid=(D // 128, s // ATTN_CHUNK),
        in_specs=[whole(0), whole(8), whole(16), table, table, table], out_specs=[chunk_spec, chunk_spec],
        out_shape=[jax.ShapeDtypeStruct((s, D), F32)] * 2,
        scratch_shapes=[pltpu.VMEM((s, 128), F32), pltpu.VMEM((s, 128), F32), stat, stat, stat],
        compiler_params=_cparams(("parallel", "arbitrary")),
    )(proj, proj, proj, *tabs)


def attn_bwd_all(proj, tabs, dmix, o, lse, name):
    s = proj.shape[0]
    plan = _attn_plan(s)
    whole, table, chunk_spec = _attn_specs(s)
    nchunk = s // ATTN_CHUNK

    def body(q_ref, k_ref, v_ref, c_ref, sa_ref, sb_ref, do_ref, o_ref, lse_ref, dq_ref, dk_ref, dv_ref, qs, ks, aug0_s, aug1_s):
        chunk = pl.program_id(1)

        @pl.when(chunk == 0)
        def _():
            qs[...] = _rope_pair(q_ref[...], c_ref[...], sa_ref[...], sb_ref[...]) * (HD ** -0.5)
            ks[...] = _rope_pair(k_ref[...], c_ref[...], sa_ref[...], sb_ref[...])
            dk_ref[...] = jnp.zeros_like(dk_ref)
            dv_ref[...] = jnp.zeros_like(dv_ref)

        base = pl.multiple_of(chunk * ATTN_CHUNK, ATTN_CHUNK)
        prod = do_ref[...] * o_ref[...]
        first = _iota(prod.shape, 1) < HD
        delta = jnp.where(first, jnp.sum(jnp.where(first, prod, 0.0), axis=1, keepdims=True),
                          jnp.sum(jnp.where(first, 0.0, prod), axis=1, keepdims=True))
        lane = _iota(prod.shape, 1)

        def as_lanes(lse_h, delta_h):
            a, b = [u.astype(F32) for u in _parts(lse_h, 3)], [u.astype(F32) for u in _parts(delta_h, 3)]
            out = jnp.zeros_like(lse_h)
            for k, u in enumerate(a + b):
                out = jnp.where(lane == k, u, out)
            return out

        lsev = lse_ref[...]
        aug0_s[...] = as_lanes(lsev, delta)
        aug1_s[...] = as_lanes(pltpu.roll(lsev, HD, 1), pltpu.roll(delta, HD, 1))
        for pi, pat in enumerate(plan):
            d, tq, win = pat[:3]
            head0 = _iota((tq, 128), 1) < HD

            def tile(qrow, krow, valid, pi=pi, d=d, tq=tq, win=win, head0=head0):
                qv = qs[_rows(base + qrow, tq, d), :].astype(BF16)
                kw = ks[_rows(krow, win, d), :].astype(BF16)
                vw = v_ref[_rows(krow, win, d), :].astype(BF16)
                dob = do_ref[_rows(qrow, tq, d), :].astype(BF16)
                aug = jnp.concatenate([aug0_s[_rows(qrow, tq, d), :], aug1_s[_rows(qrow, tq, d), :]], axis=0).astype(BF16)
                klane = _iota((win, 128), 1)
                minus_lse = jnp.where(klane < 3, -1.0, 0.0).astype(BF16)
                minus_delta = jnp.where((klane >= 3) & (klane < 6), -1.0, 0.0).astype(BF16)
                q2, do2 = _stack_heads(qv, head0), _stack_heads(dob, head0)
                s_lse = _dot(jnp.concatenate([q2, aug], axis=1), jnp.concatenate([kw, minus_lse], axis=1), NT)
                dp_delta = _dot(jnp.concatenate([do2, aug], axis=1), jnp.concatenate([vw, minus_delta], axis=1), NT)
                p = jnp.where(valid, jnp.exp(s_lse), 0.0)
                ds = (p * dp_delta).astype(BF16)
                dq2 = _dot(ds, kw)
                dk = _dot(ds, q2, TN)
                dv = _dot(p.astype(BF16), do2, TN)
                dqv = jnp.where(head0, dq2[:tq], dq2[tq:])
                if pi == 0:
                    dq_ref[_rows(qrow, tq, d), :] = dqv
                else:
                    dq_ref[_rows(qrow, tq, d), :] += dqv
                dk_ref[_rows(krow, win, d), :] += dk
                dv_ref[_rows(krow, win, d), :] += dv

            _for_tiles(chunk, pat, tile)
        tab = [t[pl.ds(base, ATTN_CHUNK), :] for t in (c_ref, sa_ref, sb_ref)]
        dq_ref[...] = _rope_pair_t(dq_ref[...] * (HD ** -0.5), *tab)

        @pl.when(chunk == nchunk - 1)
        def _():
            dk_ref[...] = _rope_pair_t(dk_ref[...], c_ref[...], sa_ref[...], sb_ref[...])

    return _pcall(
        body, name=name, grid=(D // 128, nchunk),
        in_specs=[whole(0), whole(8), whole(16), table, table, table, chunk_spec, chunk_spec, chunk_spec],
        out_specs=[chunk_spec, whole(0), whole(0)], out_shape=[jax.ShapeDtypeStruct((s, D), F32)] * 3,
        scratch_shapes=[pltpu.VMEM((s, 128), F32), pltpu.VMEM((s, 128), F32)] + [pltpu.VMEM((ATTN_CHUNK, 128), F32)] * 2,
        compiler_params=_cparams(("parallel", "arbitrary")),
    )(proj, proj, proj, *tabs, dmix, o, lse)


def _ssd_common(x_ref, b_ref, c_ref, dt_ref, dtt_ref, a_ref, ar_ref, rev):
    ii, jj = _iota((CHUNK, CHUNK), 0), _iota((CHUNK, CHUNK), 1)
    low = jj >= ii if rev else jj <= ii
    x, dtx = x_ref[...], dt_ref[...]
    bm, cm = b_ref[...].astype(BF16), c_ref[...].astype(BF16)
    a = dtx * a_ref[...]
    arow = dtt_ref[0] * ar_ref[0]
    lowb = low.astype(BF16)
    cs = _dot(lowb, jnp.concatenate(_parts(a, 3), axis=1))
    cs = cs[:, :128] + cs[:, 128:256] + cs[:, 256:]
    csr = _dot(jnp.concatenate([p.astype(F32) for p in _parts(arow, 3)], axis=0).astype(BF16), lowb, NT)
    csr = csr[0:8] + csr[8:16] + csr[16:24]
    last = 0 if rev else CHUNK - 1
    tot = cs[last:last + 1, :]
    xdt = x * dtx
    cb = _dot(cm, bm, NT)
    lmats = [jnp.exp(jnp.where(low, cs[:, HD * h:HD * h + 1] - csr[h:h + 1, :], -1e30)) for h in range(2)]
    return dict(x=x, dtx=dtx, bm=bm, cm=cm, a=a, cs=cs, tot=tot, xdt=xdt, cb=cb, lmats=lmats, low=low, last=last)


SSD_SUB = 4


def _ssd_specs(s, rev_order):
    nblk, rows = s // (SSD_SUB * CHUNK), SSD_SUB * CHUNK
    ci = (lambda c: nblk - 1 - c) if rev_order else (lambda c: c)
    tile = lambda off, div: pl.BlockSpec((rows, 128), lambda p, c: (ci(c), off + p // div))
    common = [tile(0, 1), tile(8, 2), tile(12, 2), tile(0, 1),
              pl.BlockSpec((1, 8, rows), lambda p, c: (p, 0, ci(c))),
              pl.BlockSpec((1, 128), lambda p, c: (0, p)),
              pl.BlockSpec((1, 8, 128), lambda p, c: (p, 0, 0))]
    hs = pl.BlockSpec((1, SSD_SUB, CHUNK, 128), lambda p, c: (p, ci(c), 0, 0))
    return nblk, common, tile(0, 1), hs


def _chunk_rows(ref, j):
    return ref.at[pl.ds(j * CHUNK, CHUNK), :]


def _ssd_chunk(refs, j):
    return [_chunk_rows(r, j) for r in refs[:4]] + [refs[4].at[:, :, pl.ds(j * CHUNK, CHUNK)], refs[5], refs[6]]


def _ssd_args(xbc, t):
    return [xbc, xbc, xbc, t["dt_exp"], t["dtt"], t["a_exp"], t["a_rows"]]


def ssd_fwd(xbc, dirs, name):
    s = xbc.shape[0]
    nd = len(dirs)
    specs = [_ssd_specs(s, t["rev"]) for t in dirs]
    nck = specs[0][0]

    def one(rev, x_ref, b_ref, c_ref, dt_ref, dtt_ref, a_ref, ar_ref, y_ref, hs_ref, h_scr):
        v = _ssd_common(x_ref, b_ref, c_ref, dt_ref, dtt_ref, a_ref, ar_ref, rev)
        xdtb = v["xdt"].astype(BF16)
        yd = _dot(jnp.concatenate([v["cb"] * v["lmats"][h] for h in range(2)], axis=0).astype(BF16), xdtb)
        h_in = h_scr[...]
        hs_ref[0, 0] = h_in
        y_off = _dot(v["cm"], h_in.astype(BF16)) * jnp.exp(v["cs"])
        y_ref[...] = jnp.where(_iota((CHUNK, 128), 1) < HD, yd[:CHUNK], yd[CHUNK:]) + y_off
        decay = jnp.exp(v["tot"] - v["cs"])
        h_scr[...] = jnp.exp(v["tot"]) * h_in + _dot(v["bm"], (v["xdt"] * decay).astype(BF16), TN)

    def body(*refs):
        @pl.when(pl.program_id(1) == 0)
        def _():
            for k in range(nd):
                refs[9 * nd + k][...] = jnp.zeros((CHUNK, 128), F32)

        for k, t in enumerate(dirs):
            y_ref, hs_ref = refs[7 * nd + 2 * k:7 * nd + 2 * k + 2]
            for j in (range(SSD_SUB)[::-1] if t["rev"] else range(SSD_SUB)):
                one(t["rev"], *_ssd_chunk(refs[7 * k:7 * k + 7], j), _chunk_rows(y_ref, j), hs_ref.at[:, pl.ds(j, 1)], refs[9 * nd + k])

    res = _pcall(
        body, name=name, grid=(8, nck), in_specs=[sp for t in specs for sp in t[1]],
        out_specs=[sp for t in specs for sp in (t[2], t[3])],
        out_shape=[jax.ShapeDtypeStruct((s, D), F32), jax.ShapeDtypeStruct((8, s // CHUNK, CHUNK, 128), F32)] * nd,
        scratch_shapes=[pltpu.VMEM((CHUNK, 128), F32)] * nd, compiler_params=_cparams(("parallel", "arbitrary")),
    )(*[a for t in dirs for a in _ssd_args(xbc, t)])
    return [(res[2 * k], res[2 * k + 1]) for k in range(nd)]


def ssd_bwd(xbc, dirs, dy, name):
    s = xbc.shape[0]
    nd = len(dirs)
    specs = [_ssd_specs(s, not t["rev"]) for t in dirs]
    nck = specs[0][0]

    def one(rev, x_ref, b_ref, c_ref, dt_ref, dtt_ref, a_ref, ar_ref, hs_ref, dy_ref,
            dx_ref, ddt_ref, db_ref, dc_ref, dal_ref, dh_scr):
        v = _ssd_common(x_ref, b_ref, c_ref, dt_ref, dtt_ref, a_ref, ar_ref, rev)
        bm, cm, cs, tot, xdt = v["bm"], v["cm"], v["cs"], v["tot"], v["xdt"]
        h_in, dh = hs_ref[0, 0], dh_scr[...]
        dyv = dy_ref[...]
        dyb = dyv.astype(BF16)
        etot, decay, ecs = jnp.exp(tot), jnp.exp(tot - cs), jnp.exp(cs)
        xdtb = xdt.astype(BF16)
        xdec = xdt * decay
        dch = (dyv * ecs).astype(BF16)
        hb, dhb = h_in.astype(BF16), dh.astype(BF16)
        y_off = _dot(cm, hb) * ecs
        dc = _dot(dch, hb, NT)
        dh_y = _dot(cm, dch, TN)
        dxdec = _dot(bm, dhb)
        db = _dot(xdec.astype(BF16), dhb, NT)
        state_term = xdec * dxdec
        dtot = _colsum(dh * h_in) * etot + _colsum(state_term)
        head0 = _iota((CHUNK, 128), 1) < HD
        ii, jj = _iota((CHUNK, CHUNK), 0), _iota((CHUNK, CHUNK), 1)
        low_t = jj <= ii if rev else jj >= ii
        not_low_t = (~low_t).astype(BF16)
        g = _dot(_stack_heads(dyb, head0), xdtb, NT)
        gl = [g[:CHUNK] * v["lmats"][0], g[CHUNK:] * v["lmats"][1]]
        dcb = gl[0] + gl[1]
        dxd = _dot(jnp.concatenate([v["cb"] * v["lmats"][h] for h in range(2)], axis=1).astype(BF16), dyb, TN)
        dxd = jnp.where(head0, dxd[:CHUNK], dxd[CHUNK:])
        w = _dot(not_low_t, jnp.concatenate([gl[h] * v["cb"] for h in range(2)], axis=0).astype(BF16), NT)
        da_l = [jnp.sum(jnp.where(low_t, w[:, CHUNK * h:CHUNK * h + CHUNK], 0.0), axis=1, keepdims=True) for h in range(2)]
        dxdt = dxdec * decay + dxd
        dcbb = dcb.astype(BF16)
        dc_ref[...] = dc + _dot(dcbb, bm)
        db_ref[...] = db + _dot(dcbb, cm, TN)
        dcs = dyv * y_off - state_term + jnp.where(_iota((CHUNK, 128), 0) == v["last"], dtot, 0.0)
        lowb = v["low"].astype(BF16)
        da = _dot(lowb, jnp.concatenate(_parts(dcs, 2), axis=1), TN)
        da = da[:, :128] + da[:, 128:]
        seg = ((ii < HD) == (jj < HD)).astype(BF16)
        sums = _dot(jnp.concatenate(_parts(da, 2) + _parts(dxdt * v["x"], 2), axis=0), seg)
        da = sums[:CHUNK] + sums[CHUNK:2 * CHUNK] + jnp.where(head0, da_l[0], da_l[1])
        ddt_x = sums[2 * CHUNK:3 * CHUNK] + sums[3 * CHUNK:]
        dx_ref[...] = dxdt * v["dtx"]
        ddt_ref[...] = ddt_x + da * a_ref[...]
        dal_ref[0] += _colsum(da * v["a"])
        dh_scr[...] = etot * dh + dh_y

    def body(*refs):
        @pl.when(pl.program_id(1) == 0)
        def _():
            for k in range(nd):
                refs[14 * nd + k][...] = jnp.zeros((CHUNK, 128), F32)
                refs[9 * nd + 5 * k + 4][...] = jnp.zeros((1, 8, 128), F32)

        for k, t in enumerate(dirs):
            ins, outs = refs[9 * k:9 * k + 9], refs[9 * nd + 5 * k:9 * nd + 5 * k + 5]
            for j in (range(SSD_SUB) if t["rev"] else range(SSD_SUB)[::-1]):
                one(t["rev"], *_ssd_chunk(ins[:7], j), ins[7].at[:, pl.ds(j, 1)], _chunk_rows(ins[8], j),
                    *[_chunk_rows(r, j) for r in outs[:4]], outs[4], refs[14 * nd + k])

    acc_spec = pl.BlockSpec((1, 8, 128), lambda p, c: (p, 0, 0))
    res = _pcall(
        body, name=name, grid=(8, nck), in_specs=[sp for t in specs for sp in t[1] + [t[3], t[2]]],
        out_specs=[sp for t in specs for sp in [t[2]] * 4 + [acc_spec]],
        out_shape=([jax.ShapeDtypeStruct((s, D), F32)] * 4 + [jax.ShapeDtypeStruct((8, 8, 128), F32)]) * nd,
        scratch_shapes=[pltpu.VMEM((CHUNK, 128), F32)] * nd, compiler_params=_cparams(("parallel", "arbitrary")),
    )(*[a for t in dirs for a in _ssd_args(xbc, t) + [t["hs"], dy]])
    return [res[5 * k:5 * k + 5] for k in range(nd)]


def _group_norm_stats(g):
    r = [lax.rsqrt(jnp.mean(g[:, 256 * k:256 * k + 256] ** 2, axis=-1, keepdims=True) + EPS) for k in range(4)]
    grp = _iota(g.shape, 1) // 256
    return jnp.where(grp == 0, r[0], jnp.where(grp == 1, r[1], jnp.where(grp == 2, r[2], r[3])))


def _group_mean(t):
    m = [jnp.mean(t[:, 256 * k:256 * k + 256], axis=-1, keepdims=True) for k in range(4)]
    grp = _iota(t.shape, 1) // 256
    return jnp.where(grp == 0, m[0], jnp.where(grp == 1, m[1], jnp.where(grp == 2, m[2], m[3])))


def _mesh_pos():
    return lax.axis_index("x"), lax.axis_index("y"), lax.axis_index("c")


HBM = pl.BlockSpec(memory_space=pltpu.HBM)
SEM = pl.BlockSpec(memory_space=pltpu.SEMAPHORE)
EFFECT = pltpu.SideEffectType.DATAFLOW_SIDE_EFFECTING


def _hbm(t):
    return pltpu.with_memory_space_constraint(t, pltpu.HBM)


def _other_chips(x, y):
    return [(1 - x, y), (x, 1 - y), (1 - x, 1 - y)]


def _peer(x, y, c, m):
    return x ^ (m >> 2), y ^ ((m >> 1) & 1), c ^ (m & 1)


def gather_start(srcs_a, srcs_b):
    srcs = [_hbm(t) for t in list(srcs_a) + list(srcs_b)]
    n, na = len(srcs), len(srcs_a)
    lands = [_hbm(lax.empty((4,) + t.shape, t.dtype)) for t in srcs]

    def body(*refs):
        src, land = refs[:n], refs[n:2 * n]
        sems = refs[2 * n:2 * n + 4]
        x, y, c = _mesh_pos()
        for k in range(n):
            for j, (px, py) in enumerate(_other_chips(x, y)):
                send, recv, idx = (sems[0], sems[1], 3 * k + j) if k < na else (sems[2], sems[3], 3 * (k - na) + j)
                pltpu.make_async_remote_copy(src_ref=src[k], dst_ref=land[k].at[2 * x + y], send_sem=send.at[idx],
                                             recv_sem=recv.at[idx], device_id=(px, py, c), device_id_type=MESH).start()

    sem_a, sem_b = pltpu.SemaphoreType.DMA((3 * na,)), pltpu.SemaphoreType.DMA((3 * (n - na),))
    res = _pcall(
        body, name="gather_start", in_specs=[HBM] * (2 * n), out_specs=[SEM] * 4 + [HBM] * (2 * n),
        out_shape=[sem_a, sem_a, sem_b, sem_b] + [pltpu.HBM(t.shape, t.dtype) for t in srcs + lands],
        input_output_aliases={i: 4 + i for i in range(2 * n)},
        compiler_params=pltpu.CompilerParams(has_side_effects=EFFECT),
    )(*srcs, *lands)
    thru_src, thru_land = res[4:4 + n], res[4 + n:]
    return ((res[0], res[1], thru_src[:na], thru_land[:na]), (res[2], res[3], thru_src[na:], thru_land[na:]))


def gather_wait(group, name, after=None):
    send, recv, srcs, lands = group
    n = len(srcs)

    def body(*refs):
        src, land, send_ref, recv_ref = refs[:n], refs[n:2 * n], refs[2 * n], refs[2 * n + 1]
        x, y, c = _mesh_pos()
        for j, (px, py) in enumerate(_other_chips(x, y)):
            for k in range(n):
                cp = pltpu.make_async_remote_copy(src_ref=src[k], dst_ref=land[k].at[2 * px + py], send_sem=send_ref.at[3 * k + j],
                                                  recv_sem=recv_ref.at[3 * k + j], device_id=(px, py, c), device_id_type=MESH)
                cp.wait_send()
                cp.wait_recv()

    extra = [] if after is None else [after]
    res = _pcall(
        body, name=name, in_specs=[HBM] * (2 * n) + [SEM, SEM] + [pl.BlockSpec(memory_space=pl.ANY)] * len(extra),
        out_specs=[HBM] * (2 * n), out_shape=[pltpu.HBM(t.shape, t.dtype) for t in list(srcs) + list(lands)],
        input_output_aliases={i: i for i in range(2 * n)}, compiler_params=pltpu.CompilerParams(has_side_effects=EFFECT),
    )(*srcs, *lands, send, recv, *extra)
    return res[:n], res[n:]


def scatter_start(pieces, smalls, name):
    srcs = [_hbm(t) for t in list(pieces) + list(smalls)]
    n, npc = len(srcs), len(pieces)
    lands = [_hbm(lax.empty((8,) + (t.shape[2:] if k < npc else t.shape), t.dtype)) for k, t in enumerate(srcs)]

    def body(*refs):
        src, land, send, recv = refs[:n], refs[n:2 * n], refs[2 * n], refs[2 * n + 1]
        token = refs[-1]
        x, y, c = _mesh_pos()
        for m in range(1, 8):
            px, py, pc = _peer(x, y, c, m)
            for k in range(n):
                s_ref = src[k].at[2 * px + py, pc] if k < npc else src[k]
                d_ref = land[k].at[m] if k < npc else land[k].at[4 * x + 2 * y + c]
                pltpu.make_async_remote_copy(src_ref=s_ref, dst_ref=d_ref, send_sem=send.at[7 * k + m - 1], recv_sem=recv.at[7 * k + m - 1],
                                             device_id=(px, py, pc), device_id_type=MESH).start()
        token[...] = jnp.zeros_like(token)

    sem = pltpu.SemaphoreType.DMA((7 * n,))
    res = _pcall(
        body, name=name, in_specs=[HBM] * (2 * n),
        out_specs=[SEM, SEM] + [HBM] * (2 * n) + [pl.BlockSpec(memory_space=pltpu.VMEM)],
        out_shape=[sem, sem] + [pltpu.HBM(t.shape, t.dtype) for t in srcs + lands] + [jax.ShapeDtypeStruct((8, 128), F32)],
        input_output_aliases={i: 2 + i for i in range(2 * n)},
        compiler_params=pltpu.CompilerParams(has_side_effects=EFFECT),
    )(*srcs, *lands)
    return (res[0], res[1], res[2:2 + n], res[2 + n:2 + 2 * n], npc), res[-1]


def scatter_wait(group, name):
    send, recv, srcs, lands, npc = group
    n = len(srcs)

    def body(*refs):
        src, land, send_ref, recv_ref = refs[:n], refs[n:2 * n], refs[2 * n], refs[2 * n + 1]
        x, y, c = _mesh_pos()
        for m in range(1, 8):
            px, py, pc = _peer(x, y, c, m)
            for k in range(n):
                s_ref = src[k].at[0, 0] if k < npc else src[k]
                d_ref = land[k].at[m] if k < npc else land[k].at[4 * px + 2 * py + pc]
                cp = pltpu.make_async_remote_copy(src_ref=s_ref, dst_ref=d_ref, send_sem=send_ref.at[7 * k + m - 1],
                                                  recv_sem=recv_ref.at[7 * k + m - 1], device_id=(px, py, pc), device_id_type=MESH)
                cp.wait_send()
                cp.wait_recv()

    res = _pcall(
        body, name=name, in_specs=[HBM] * (2 * n) + [SEM, SEM], out_specs=[HBM] * (2 * n),
        out_shape=[pltpu.HBM(t.shape, t.dtype) for t in list(srcs) + list(lands)],
        input_output_aliases={i: i for i in range(2 * n)}, compiler_params=pltpu.CompilerParams(has_side_effects=EFFECT),
    )(*srcs, *lands, send, recv)
    return res[:n], res[n:]


def swap_halves(pieces):
    n = len(pieces)
    whole = pl.BlockSpec(memory_space=pltpu.VMEM)

    def body(*refs):
        p_refs, o_refs, send_sems, recv_sems, local_sems = refs[:n], refs[n:2 * n], refs[2 * n], refs[2 * n + 1], refs[2 * n + 2]
        x, y, c = _mesh_pos()
        local = [pltpu.make_async_copy(p_refs[k], o_refs[k].at[c], local_sems.at[k]) for k in range(n)]
        for cp in local:
            cp.start()

        def copy(k, slot):
            return pltpu.make_async_remote_copy(src_ref=p_refs[k], dst_ref=o_refs[k].at[slot], send_sem=send_sems.at[k],
                                                recv_sem=recv_sems.at[k], device_id=(x, y, 1 - c), device_id_type=MESH)

        for k in range(n):
            copy(k, c).start()
        for k in range(n):
            copy(k, 1 - c).wait_recv()
        for k in range(n):
            copy(k, c).wait_send()
        for cp in local:
            cp.wait()

    return _pcall(
        body, name="swap_halves", in_specs=[whole] * n, out_specs=[whole] * n,
        out_shape=[jax.ShapeDtypeStruct((2,) + t.shape, t.dtype) for t in pieces],
        scratch_shapes=[pltpu.SemaphoreType.DMA((n,)), pltpu.SemaphoreType.DMA((n,)), pltpu.SemaphoreType.DMA((n,))],
        compiler_params=_cparams(),
    )(*pieces)


def adamw(w, g, m, v, name):
    rows, cols = w.shape
    tm = rows
    for t in (256, 352, 128, 144, 64, 32, 16, 8):
        if rows % t == 0:
            tm = t
            break

    def fn(i, nrow, wv, gv, mv, vv):
        mn = ADAM_B1 * mv + (1.0 - ADAM_B1) * gv
        vn = ADAM_B2 * vv + (1.0 - ADAM_B2) * (gv * gv)
        m_hat = mn / (1.0 - ADAM_B1 ** ADAM_STEP)
        v_hat = vn / (1.0 - ADAM_B2 ** ADAM_STEP)
        delta = -ADAM_LR * (m_hat / (jnp.sqrt(v_hat) + ADAM_EPS) + ADAM_WD * wv)
        return delta, mn, vn

    return ew(fn, name, rows, tm, 1, [(t, "row", cols, 0) for t in (w, g, m, v)], [(cols, F32, cols)] * 3)


BIG = ("w_in", "w_out", "w_up", "w_down")
REST = ("w_out", "w_up", "w_down")
SMALL = ("norm1_w", "ssm_conv_w", "ssm_conv_b", "a_log_f", "a_log_b", "dt_bias_f", "dt_bias_b", "d_skip",
         "ssm_norm_w", "norm2_w", "ffn_conv_w", "ffn_conv_b", "final_norm_w")
WEIGHTS = ("norm1_w", "w_in", "ssm_conv_w", "ssm_conv_b", "a_log_f", "a_log_b", "dt_bias_f", "dt_bias_b", "d_skip",
           "ssm_norm_w", "w_out", "norm2_w", "w_up", "ffn_conv_w", "ffn_conv_b", "w_down", "final_norm_w")
INPUTS = ("x",) + WEIGHTS + ("loss_target",) + tuple("m_" + n for n in WEIGHTS) + tuple("v_" + n for n in WEIGHTS)


def _flat_rows(parts, width, rows):
    flat = jnp.concatenate([p.reshape(-1) for p in parts])
    return jnp.pad(flat, (0, rows * width - flat.shape[0])).reshape(rows, width)


def _split_flat(flat, shapes):
    out, pos = [], 0
    flat = flat.reshape(-1)
    for shp in shapes:
        n = int(np.prod(shp))
        out.append(flat[pos:pos + n].reshape(shp))
        pos += n
    return out


def _col_shards(t, nshard):
    r, c = t.shape
    return t.reshape(r, nshard, c // nshard).transpose(1, 0, 2)


def _row_shards(t, nshard):
    r, c = t.shape
    return t.reshape(nshard, r // nshard, c)


def kernel(x, norm1_w, w_in, ssm_conv_w, ssm_conv_b, a_log_f, a_log_b, dt_bias_f, dt_bias_b, d_skip, ssm_norm_w, w_out, norm2_w, w_up, ffn_conv_w, ffn_conv_b, w_down, final_norm_w, loss_target, m_norm1_w, m_w_in, m_ssm_conv_w, m_ssm_conv_b, m_a_log_f, m_a_log_b, m_dt_bias_f, m_dt_bias_b, m_d_skip, m_ssm_norm_w, m_w_out, m_norm2_w, m_w_up, m_ffn_conv_w, m_ffn_conv_b, m_w_down, m_final_norm_w, v_norm1_w, v_w_in, v_ssm_conv_w, v_ssm_conv_b, v_a_log_f, v_a_log_b, v_dt_bias_f, v_dt_bias_b, v_d_skip, v_ssm_norm_w, v_w_out, v_norm2_w, v_w_up, v_ffn_conv_w, v_ffn_conv_b, v_w_down, v_final_norm_w):
    p = dict(zip(INPUTS, (x, norm1_w, w_in, ssm_conv_w, ssm_conv_b, a_log_f, a_log_b, dt_bias_f, dt_bias_b, d_skip, ssm_norm_w, w_out, norm2_w, w_up, ffn_conv_w, ffn_conv_b, w_down, final_norm_w, loss_target, m_norm1_w, m_w_in, m_ssm_conv_w, m_ssm_conv_b, m_a_log_f, m_a_log_b, m_dt_bias_f, m_dt_bias_b, m_d_skip, m_ssm_norm_w, m_w_out, m_norm2_w, m_w_up, m_ffn_conv_w, m_ffn_conv_b, m_w_down, m_final_norm_w, v_norm1_w, v_w_in, v_ssm_conv_w, v_ssm_conv_b, v_a_log_f, v_a_log_b, v_dt_bias_f, v_dt_bias_b, v_d_skip, v_ssm_norm_w, v_w_out, v_norm2_w, v_w_up, v_ffn_conv_w, v_ffn_conv_b, v_w_down, v_final_norm_w)))
    x = p["x"][0]
    tgt = p["loss_target"][0]
    s = x.shape[0]
    chip = 2 * lax.axis_index("x") + lax.axis_index("y")

    own_slot = lambda land, mine, slot: lax.dynamic_update_slice_in_dim(land, mine[None], slot, axis=0)
    src_in = p["w_in"][0].astype(BF16)
    src_rest = [p[n][0].astype(BF16) for n in REST]
    small_w = _flat_rows([p["ssm_conv_w"][0], p["ffn_conv_w"][0]], 128, 48)
    gather_in, gather_rest = gather_start([src_in, small_w], src_rest)
    (src_in, small_w), (wg_in, sg) = gather_wait(gather_in, "gather_wait_in")
    w_in = own_slot(wg_in, src_in, chip).transpose(1, 0, 2).reshape(D, -1)
    sg = own_slot(sg, small_w, chip)
    n_in = w_in.shape[1]
    n_main = 6 * D
    w_main = w_in[:, :n_main]
    w_dt = jnp.pad(w_in[:, n_main:], ((0, 0), (0, 128 - (n_in - n_main))))
    sgf = sg.reshape(4, -1)
    n_sc, n_fc = p["ssm_conv_w"].shape[1], p["ffn_conv_w"].shape[1]
    ssm_cw = sgf[:, :n_sc * 3].reshape(-1, 3).T
    ffn_cw = sgf[:, n_sc * 3:(n_sc + n_fc) * 3].reshape(-1, 3).T
    ssm_cb, ffn_cb = p["ssm_conv_b"], p["ffn_conv_b"]
    n1w, n2w, snw, fnw = p["norm1_w"], p["norm2_w"], p["ssm_norm_w"], p["final_norm_w"].reshape(1, D)

    h1, = ew(lambda i, n, xv, w: _rms_fwd(xv, w), "rms1", s, 256, 1,
             [(x, "row", D, 0), (n1w, "const", D, 0)], [(D, BF16, D)])
    proj = matmul(h1, w_main, "nn", "in_proj")
    proj_dt = matmul(h1, w_dt, "nn", "in_proj_dt")
    tabs = _rope_tables(s)
    attn, lse = attn_fwd_all(proj, tabs, "attn_fwd")

    def conv_silu_fn(i, n, xv, xp, xn, w, b):
        return _silu(w[0:1] * _shift_down(xv, xp, i) + w[1:2] * xv + w[2:3] * _shift_up(xv, xn, i, n) + b)

    xbc_act, = ew(conv_silu_fn, "ssm_conv", s, 256, 2,
                  [(proj, "row", D, 4), (proj, "prev", D, 4), (proj, "next", D, 4),
                   (ssm_cw, "const", D, 0), (ssm_cb, "const", D, 0)], [(2 * D, F32, D)])
    dt_bias = jnp.pad(jnp.concatenate([p["dt_bias_f"], p["dt_bias_b"]], axis=1), ((0, 0), (0, 96)))

    lanes_of = np.arange(128)[:, None] == np.arange(D)[None, :] // HD
    spread = [jnp.asarray(np.roll(lanes_of, 16 * k, axis=0), BF16) for k in range(2)]

    def softplus_fn(i, n, r, b, ef, eb):
        t = r + b
        dtv = jnp.maximum(t, 0.0) + jnp.log(1.0 + jnp.exp(-jnp.abs(t)))
        parts = _parts(dtv, 3)
        return dtv, sum(_dot(q, ef) for q in parts), sum(_dot(q, eb) for q in parts)

    dt, dt_exp_f, dt_exp_b = ew(softplus_fn, "dt_softplus", s, 512, 1,
                                [(proj_dt, "row", 128, 0), (dt_bias, "const", 128, 0), (spread[0], "const", D, 0), (spread[1], "const", D, 0)],
                                [(128, F32, 128), (D, F32, D), (D, F32, D)])
    d_exp = jnp.repeat(p["d_skip"], HD, axis=1)
    ssd = []
    for k, (a_log, rev) in enumerate(((p["a_log_f"], False), (p["a_log_b"], True))):
        dt_k = dt[:, 16 * k:16 * k + 16]
        a_head = -jnp.exp(a_log)
        dt_exp = (dt_exp_f, dt_exp_b)[k]
        dtt = jnp.pad(dt_k.T.reshape(8, 2, s), ((0, 0), (0, 6), (0, 0)))
        a_exp = jnp.repeat(a_head, HD, axis=1)
        a_rows = jnp.broadcast_to(jnp.pad(a_head.reshape(8, 2), ((0, 0), (0, 6)))[:, :, None], (8, 8, 128))
        ssd.append(dict(dt_exp=dt_exp, dtt=dtt, a_exp=a_exp, a_rows=a_rows, rev=rev))
    for t, (y_k, hs_k) in zip(ssd, ssd_fwd(xbc_act, ssd, "ssd_fwd")):
        t["y"], t["hs"] = y_k, hs_k

    def gate_fn(i, n, yf, yb, xs, z, dsk, w):
        g = (yf + yb + dsk * xs) * _silu(z)
        return g * _group_norm_stats(g) * w

    ssm_out, = ew(gate_fn, "ssm_gate_norm", s, 256, 1,
                  [(ssd[0]["y"], "row", D, 0), (ssd[1]["y"], "row", D, 0), (xbc_act, "row", D, 0), (proj, "row", D, 3),
                   (d_exp, "const", D, 0), (snw, "const", D, 0)], [(D, F32, D)])
    mix = jnp.concatenate([attn, ssm_out], axis=1).astype(BF16)
    src_rest, wg_rest = gather_wait(gather_rest, "gather_wait_rest", after=mix)
    wg_rest = [own_slot(land, mine, chip) for land, mine in zip(wg_rest, src_rest)]
    w_out = wg_rest[0].reshape(-1, D)
    w_up = wg_rest[1].transpose(1, 0, 2).reshape(D, -1)
    w_down = wg_rest[2].reshape(-1, D)
    mix_w = matmul(mix, w_out, "nn", "out_proj")

    def res_rms_fn(i, n, xv, mw, w):
        x1v = xv + mw
        return x1v, _rms_fwd(x1v, w)

    x1, h2 = ew(res_rms_fn, "res_rms2", s, 256, 1, [(x, "row", D, 0), (mix_w, "row", D, 0), (n2w, "const", D, 0)],
                [(D, F32, D), (D, BF16, D)])
    hw = matmul(h2, w_up, "nn", "ffn_up")
    fw = D_FF // 2
    nfb = D_FF // fw
    ffn_conv_ins = [(hw, "row", fw, 0), (hw, "prev", fw, 0), (hw, "next", fw, 0),
                    (hw, "row", fw, nfb), (hw, "prev", fw, nfb), (hw, "next", fw, nfb),
                    (ffn_cw, "const", fw, 0), (ffn_cw, "const", fw, nfb), (ffn_cb, "const", fw, 0), (ffn_cb, "const", fw, nfb)]

    def ffn_conv(i, n, g, gp, gn, u, up_, un, wg_, wu, bg, bu):
        gs = (_shift_down(g, gp, i), g, _shift_up(g, gn, i, n))
        us = (_shift_down(u, up_, i), u, _shift_up(u, un, i, n))
        gate = wg_[0:1] * gs[0] + wg_[1:2] * gs[1] + wg_[2:3] * gs[2] + bg
        upv = wu[0:1] * us[0] + wu[1:2] * us[1] + wu[2:3] * us[2] + bu
        return gate, upv, gs, us

    def glu_fn(i, n, *blocks):
        gate, upv, _, _ = ffn_conv(i, n, *blocks)
        return _silu(gate) * upv

    act, = ew(glu_fn, "ffn_conv_glu", s, 256, nfb, ffn_conv_ins, [(D_FF, BF16, fw)])
    ffn = matmul(act, w_down, "nn", "ffn_down")

    def head_fn(i, n, x1v, fv, tv, w):
        x2 = x1v + fv
        r = lax.rsqrt(jnp.mean(x2 * x2, axis=-1, keepdims=True) + EPS)
        xh = x2 * r
        diff = xh * w - tv
        loss = 0.5 * jnp.sum(jnp.mean(diff * diff, axis=-1, keepdims=True), axis=0, keepdims=True)
        dout = diff * (1.0 / D)
        dxh = dout * w
        dx2 = r * (dxh - xh * jnp.mean(dxh * xh, axis=-1, keepdims=True))
        return dx2, jnp.broadcast_to(loss, (1, 128)), _colsum(dout * xh)

    dx2, loss_acc, g_fnw = ew(head_fn, "loss_head", s, 256, 1,
                              [(x1, "row", D, 0), (ffn, "row", D, 0), (tgt, "row", D, 0), (fnw, "const", D, 0)],
                              [(D, F32, D)], [(128, 128), (D, D)])
    loss = lax.psum(loss_acc[0, 0], ("x", "y", "c"))

    g_w_down = matmul(act, dx2, "tn", "d_w_down")
    dact = matmul(dx2, w_down, "nt", "d_act")

    def glu_bwd_fn(i, n, *blocks):
        gate, upv, gs, us = ffn_conv(i, n, *blocks[:-1])
        da = blocks[-1]
        dg = da * upv * _dsilu(gate)
        du = da * _silu(gate)
        return (dg, du) + tuple(_colsum(dg * t) for t in gs) + tuple(_colsum(du * t) for t in us) + (_colsum(dg), _colsum(du))

    res = ew(glu_bwd_fn, "ffn_glu_bwd", s, 256, nfb, ffn_conv_ins + [(dact, "row", fw, 0)],
             [(D_FF, F32, fw)] * 2, [(D_FF, fw)] * 8)
    du_g, du_u = res[0], res[1]
    g_ffn_cw = jnp.concatenate([jnp.concatenate(res[2:5], axis=0), jnp.concatenate(res[5:8], axis=0)], axis=1).T
    g_ffn_cb = jnp.concatenate([res[8], res[9]], axis=1)

    def conv_t_fn(i, n, dv, dp, dn, w):
        return w[0:1] * _shift_up(dv, dn, i, n) + w[1:2] * dv + w[2:3] * _shift_down(dv, dp, i)

    def conv_t(du, cw, off, width, ncol, name):
        return ew(conv_t_fn, name, s, 256, ncol,
                  [(du, "row", width, 0), (du, "prev", width, 0), (du, "next", width, 0), (cw, "const", width, off)],
                  [(du.shape[1], F32, width)])[0]

    dhw_g = conv_t(du_g, ffn_cw, 0, fw, nfb, "ffn_conv_t_gate")
    dhw_u = conv_t(du_u, ffn_cw, nfb, fw, nfb, "ffn_conv_t_up")
    g_w_up = jnp.concatenate([matmul(h2, dhw_g, "tn", "d_w_up_gate"), matmul(h2, dhw_u, "tn", "d_w_up_up")], axis=1)
    dh2_a = matmul(dhw_g, w_up, "nt", "d_h2_gate")
    dh2_b = matmul(dhw_u, w_up, "nt", "d_h2_up", b_k_off=D_FF // _pick(D_FF, 1408))

    def res_rms_bwd_fn(i, n, dres, da, db, xin, w):
        dx, dw = _rms_bwd(da + db, xin, w)
        return dres + dx, dw

    dx1, g_n2w = ew(res_rms_bwd_fn, "res_rms2_bwd", s, 256, 1,
                    [(dx2, "row", D, 0), (dh2_a, "row", D, 0), (dh2_b, "row", D, 0), (x1, "row", D, 0), (n2w, "const", D, 0)],
                    [(D, F32, D)], [(D, D)])

    g_w_out = matmul(mix, dx1, "tn", "d_w_out")
    to_pieces = lambda t: t.astype(BF16).reshape(4, 2, t.shape[1] // 2, t.shape[2])
    shards_rest = [_row_shards(g_w_out, 4), _col_shards(g_w_up, 4), _row_shards(g_w_down, 4)]
    scatter_rest, token = scatter_start([to_pieces(t) for t in shards_rest], [], "scatter_start_rest")
    dmix = matmul(dx1, w_out, "nt", "d_mix", after=token)
    ii, jj = np.arange(D)[:, None] // HD, np.arange(D)[None, :] // HD
    seg = jnp.asarray(ii == jj, BF16)

    def gate_bwd_fn(i, n, dout, yf, yb, xs, z, dsk, w, segm):
        yt = yf + yb + dsk * xs
        sz = _silu(z)
        g = yt * sz
        r = _group_norm_stats(g)
        gh = g * r
        dn = dout * w
        dg = r * (dn - gh * _group_mean(dn * gh))
        dy = dg * sz
        dsk_lane = jnp.broadcast_to(_colsum(dy * xs), (8, D))
        return dy, dg * yt * _dsilu(z), _colsum(dout * gh), sum(_dot(q, segm) for q in _parts(dsk_lane, 2))[0:1]

    dy, dz, g_snw, g_dskip_l = ew(
        gate_bwd_fn, "ssm_gate_norm_bwd", s, 256, 1,
        [(dmix, "row", D, 1), (ssd[0]["y"], "row", D, 0), (ssd[1]["y"], "row", D, 0), (xbc_act, "row", D, 0),
         (proj, "row", D, 3), (d_exp, "const", D, 0), (snw, "const", D, 0), (seg, "const", D, 0)],
        [(D, F32, D)] * 2, [(D, D)] * 2)
    sb = ssd_bwd(xbc_act, ssd, dy, "ssd_bwd")

    def dxbc_act_fn(i, n, dxf, dxb, dyv, dsk, dbf, dbb, dcf, dcb_):
        db, dc = dbf + dbb, dcf + dcb_
        db = [db[:, 256 * g:256 * g + 128] + db[:, 256 * g + 128:256 * g + 256] for g in range(4)]
        dc = [dc[:, 256 * g:256 * g + 128] + dc[:, 256 * g + 128:256 * g + 256] for g in range(4)]
        return jnp.concatenate([dxf + dxb + dyv * dsk] + db + dc, axis=1)

    dxbc_act, = ew(dxbc_act_fn, "d_xbc_act", s, 256, 1,
                   [(sb[0][0], "row", D, 0), (sb[1][0], "row", D, 0), (dy, "row", D, 0), (d_exp, "const", D, 0),
                    (sb[0][2], "row", D, 0), (sb[1][2], "row", D, 0), (sb[0][3], "row", D, 0), (sb[1][3], "row", D, 0)],
                   [(2 * D, F32, 2 * D)])

    def silu_bwd_fn(i, n, xv, xp, xn, w, b, da):
        xs3 = (_shift_down(xv, xp, i), xv, _shift_up(xv, xn, i, n))
        du = da * _dsilu(w[0:1] * xs3[0] + w[1:2] * xs3[1] + w[2:3] * xs3[2] + b)
        return (du,) + tuple(_colsum(du * t) for t in xs3) + (_colsum(du),)

    res = ew(silu_bwd_fn, "ssm_conv_bwd", s, 256, 2,
             [(proj, "row", D, 4), (proj, "prev", D, 4), (proj, "next", D, 4), (ssm_cw, "const", D, 0),
              (ssm_cb, "const", D, 0), (dxbc_act, "row", D, 0)], [(2 * D, F32, D)], [(2 * D, D)] * 4)
    g_ssm_cw = jnp.concatenate(res[1:4], axis=0).T
    g_ssm_cb = res[4]
    dxbc = conv_t(res[0], ssm_cw, 0, D, 2, "ssm_conv_t")
    ddt = jnp.pad(jnp.concatenate([sb[0][1][:, ::HD], sb[1][1][:, ::HD]], axis=1), ((0, 0), (0, 96)))

    def dt_bwd_fn(i, n, dd, r, b):
        dr = dd * _sigmoid(r + b)
        return dr, _colsum(dr)

    dproj_dt, g_dt_bias = ew(dt_bwd_fn, "dt_softplus_bwd", s, 512, 1,
                             [(ddt, "row", 128, 0), (proj_dt, "row", 128, 0), (dt_bias, "const", 128, 0)],
                             [(128, F32, 128)], [(128, 128)])
    g_a_log = [t[4][:, 0, ::HD].reshape(1, 16) for t in sb]

    dq, dk, dv = attn_bwd_all(proj, tabs, dmix, attn, lse, "attn_bwd")

    dproj = jnp.concatenate([dq, dk, dv, dz, dxbc], axis=1).astype(BF16)
    g_w_in = jnp.concatenate([matmul(h1, dproj, "tn", "d_w_in"), matmul(h1, dproj_dt, "tn", "d_w_in_dt")[:, :n_in - n_main]], axis=1)
    scatter_in, token = scatter_start([to_pieces(_col_shards(g_w_in, 4))], [], "scatter_start_in")
    dh1_a = matmul(dproj, w_main, "nt", "d_h1", after=token)
    dh1_b = matmul(dproj_dt, w_dt, "nt", "d_h1_dt")
    grad_x, g_n1w = ew(res_rms_bwd_fn, "rms1_bwd", s, 256, 1,
                       [(dx1, "row", D, 0), (dh1_a, "row", D, 0), (dh1_b, "row", D, 0), (x, "row", D, 0), (n1w, "const", D, 0)],
                       [(D, F32, D)], [(D, D)])

    small_g = {"norm1_w": g_n1w, "ssm_conv_w": g_ssm_cw, "ssm_conv_b": g_ssm_cb, "a_log_f": g_a_log[0], "a_log_b": g_a_log[1],
               "dt_bias_f": g_dt_bias[:, :16], "dt_bias_b": g_dt_bias[:, 16:32], "d_skip": g_dskip_l[:, ::HD],
               "ssm_norm_w": g_snw, "norm2_w": g_n2w, "ffn_conv_w": g_ffn_cw, "ffn_conv_b": g_ffn_cb, "final_norm_w": g_fnw}
    small_shapes = [small_g[n].shape for n in SMALL]
    scatter_small, _ = scatter_start([], [_flat_rows([small_g[n] for n in SMALL], 128, SMALL_ROWS)], "scatter_start_small")
    sent_rest, got_rest = scatter_wait(scatter_rest, "scatter_wait_rest")
    (sent_in,), (got_in,) = scatter_wait(scatter_in, "scatter_wait_in")
    (sent_small,), (got_small,) = scatter_wait(scatter_small, "scatter_wait_small")
    core = lax.axis_index("c")

    def sum8_fn(i, n, *v):
        t = v[0].astype(F32)
        for u in v[1:]:
            t = t + u.astype(F32)
        return t

    def sum_pieces(sent, got, name):
        rows, w = got.shape[1:]
        tm = 256 if rows % 256 == 0 else rows
        mine = lax.dynamic_slice(sent, (chip, core, 0, 0), (1, 1, rows, w)).reshape(rows, w)
        ins = [(mine, "row", w, 0)] + [(got.reshape(8 * rows, w), "row", w, 0, k * (rows // tm)) for k in range(1, 8)]
        return ew(sum8_fn, name, rows, tm, 1, ins, [(w, F32, w)])[0]

    got_small = own_slot(got_small, sent_small, 2 * chip + core)
    small_sum, = ew(sum8_fn, "sum_small", SMALL_ROWS, SMALL_ROWS, 1,
                    [(got_small.reshape(8 * SMALL_ROWS, 128), "row", 128, 0, k) for k in range(8)], [(128, F32, 128)])
    summed = swap_halves([sum_pieces(a, b, "sum_pieces_" + n) for a, b, n in zip(sent_rest, got_rest, REST)]
                         + [sum_pieces(sent_in, got_in, "sum_pieces_w_in")])
    grads = {n: t.reshape(p[n].shape) for n, t in zip(REST + ("w_in",), summed)}
    for n, g in zip(SMALL, _split_flat(small_sum, small_shapes)):
        if n in ("ssm_conv_w", "ffn_conv_w"):
            rows = p[n].shape[1]
            g = lax.dynamic_slice_in_dim(g, chip * rows, rows, axis=0)
        grads[n] = g.reshape(p[n].shape)

    delta, new_m, new_v = {}, {}, {}
    for n in BIG:
        shp = p[n].shape
        r = [t.reshape(shp[1:]) for t in (p[n], grads[n], p["m_" + n], p["v_" + n])]
        delta[n], new_m[n], new_v[n] = [t.reshape(shp) for t in adamw(*r, "adamw_" + n)]
    shapes = [p[n].shape for n in SMALL]
    total = sum(int(np.prod(sh)) for sh in shapes)
    rows = -(-total // 1024) * 8
    packs = [_flat_rows([t[n] for n in SMALL], 128, rows)
             for t in (p, grads, {n: p["m_" + n] for n in SMALL}, {n: p["v_" + n] for n in SMALL})]
    for dst, t in zip((delta, new_m, new_v), adamw(*packs, "adamw_small")):
        for n, u in zip(SMALL, _split_flat(t, shapes)):
            dst[n] = u
    return (loss, grad_x[None], *[grads[n] for n in WEIGHTS], *[delta[n] for n in WEIGHTS],
            *[new_m[n] for n in WEIGHTS], *[new_v[n] for n in WEIGHTS])
```

```python
import numpy as np
import jax
import jax.numpy as jnp
from jax import lax
from jax.experimental import pallas as pl
from jax.experimental.pallas import tpu as pltpu

F32, BF16 = jnp.float32, jnp.bfloat16
MESH = pl.DeviceIdType.MESH
V7X_VMEM_LIMIT = 56 * 1024 * 1024

D = 1024
HD = 64
EPS = 1e-6
CHUNK = 128
D_FF = 2816
ROPE_DIM = 16
ROPE_THETA = 500000.0
PATTERN_DILATIONS = (1, 4, 16)
BAND = 64
SMALL_ROWS = 280
ADAM_LR, ADAM_B1, ADAM_B2, ADAM_EPS, ADAM_WD, ADAM_STEP = 0.001, 0.9, 0.999, 1e-08, 0.01, 10

NN = (((1,), (0,)), ((), ()))
NT = (((1,), (1,)), ((), ()))
TN = (((0,), (0,)), ((), ()))


def _pcall(body, **kw):
    return pl.pallas_call(body, **kw)


def _cparams(sem=None):
    return pltpu.CompilerParams(dimension_semantics=sem, vmem_limit_bytes=V7X_VMEM_LIMIT)


def _dot(a, b, dims=NN):
    return lax.dot_general(a, b, dims, preferred_element_type=F32)


def _pick(n, cap):
    if n <= cap:
        return n
    best = 0
    for t in range(128, cap + 1, 128):
        if n % t == 0:
            best = t
    assert best, (n, cap)
    return best


def _iota(shape, dim):
    return lax.broadcasted_iota(jnp.int32, shape, dim)


def _parts(x, n):
    out, r = [], x
    for _ in range(n):
        h = r.astype(BF16)
        out.append(h)
        r = r - h.astype(F32)
    return out


def _sigmoid(x):
    return 1.0 / (1.0 + jnp.exp(-x))


def _silu(x):
    return x * _sigmoid(x)


def _dsilu(x):
    s = _sigmoid(x)
    return s * (1.0 + x * (1.0 - s))


def matmul(a, b, mode, name, out_dtype=F32, after=None, b_k_off=0):
    if mode == "nn":
        (m, k), (_, n) = a.shape, b.shape
    elif mode == "nt":
        (m, k), (n, _) = a.shape, b.shape
    else:
        (k, m), (_, n) = a.shape, b.shape
    tm, tn, tk = _pick(m, 1408), _pick(n, 1408), _pick(k, 1408)
    nk = k // tk
    dims = {"nn": NN, "nt": NT, "tn": TN}[mode]
    a_spec = pl.BlockSpec((tk, tm), lambda i, j, kk: (kk, i)) if mode == "tn" else pl.BlockSpec((tm, tk), lambda i, j, kk: (i, kk))
    b_spec = pl.BlockSpec((tn, tk), lambda i, j, kk: (j, kk + b_k_off)) if mode == "nt" else pl.BlockSpec((tk, tn), lambda i, j, kk: (kk, j))
    extra = [] if after is None else [after]

    def body(a_ref, b_ref, *rest):
        o_ref, acc = rest[len(extra)], rest[len(extra) + 1:]
        part = _dot(a_ref[...].astype(BF16), b_ref[...].astype(BF16), dims)
        if nk == 1:
            o_ref[...] = part.astype(o_ref.dtype)
            return
        acc_ref, kk = acc[0], pl.program_id(2)

        @pl.when(kk == 0)
        def _():
            acc_ref[...] = part

        @pl.when((kk > 0) & (kk < nk - 1))
        def _():
            acc_ref[...] += part

        @pl.when(kk == nk - 1)
        def _():
            o_ref[...] = (acc_ref[...] + part).astype(o_ref.dtype)

    return _pcall(
        body, name=name, grid=(m // tm, n // tn, nk), in_specs=[a_spec, b_spec] + [pl.BlockSpec(memory_space=pl.ANY)] * len(extra),
        out_specs=pl.BlockSpec((tm, tn), lambda i, j, kk: (i, j)),
        out_shape=jax.ShapeDtypeStruct((m, n), out_dtype),
        scratch_shapes=[pltpu.VMEM((tm, tn), F32)] if nk > 1 else [],
        compiler_params=_cparams(("parallel", "parallel", "arbitrary")),
    )(a, b, *extra)


def ew(fn, name, rows, tm, ncol, ins, outs, accs=()):
    nrow = rows // tm
    r8 = tm // 8
    in_specs, arrays = [], []
    for ent in ins:
        arr, kind, w, off = ent[:4]
        roff = ent[4] if len(ent) > 4 else 0
        if kind == "row":
            spec = pl.BlockSpec((tm, w), lambda j, i, off=off, roff=roff: (i + roff, j + off))
        elif kind == "const":
            spec = pl.BlockSpec((arr.shape[0], w), lambda j, i, off=off: (0, j + off))
        elif kind == "prev":
            spec = pl.BlockSpec((8, w), lambda j, i, off=off: (jnp.maximum(i * r8 - 1, 0), j + off))
        else:
            spec = pl.BlockSpec((8, w), lambda j, i, off=off: (jnp.minimum((i + 1) * r8, rows // 8 - 1), j + off))
        in_specs.append(spec)
        arrays.append(arr)
    out_specs = [pl.BlockSpec((tm, w), lambda j, i: (i, j)) for (_, _, w) in outs]
    out_shape = [jax.ShapeDtypeStruct((rows, c), dt) for (c, dt, _) in outs]
    out_specs += [pl.BlockSpec((1, w), lambda j, i: (0, j)) for (_, w) in accs]
    out_shape += [jax.ShapeDtypeStruct((1, c), F32) for (c, _) in accs]
    nin, nout = len(ins), len(outs)

    def body(*refs):
        i = pl.program_id(1)
        res = fn(i, nrow, *[r[...] for r in refs[:nin]])
        if not isinstance(res, (tuple, list)):
            res = (res,)
        for r, v in zip(refs[nin:nin + nout], res[:nout]):
            r[...] = v.astype(r.dtype)
        if accs:
            acc_refs = refs[nin + nout:]

            @pl.when(i == 0)
            def _():
                for r in acc_refs:
                    r[...] = jnp.zeros_like(r)

            for r, v in zip(acc_refs, res[nout:]):
                r[...] += v

    res = _pcall(
        body, name=name, grid=(ncol, nrow), in_specs=in_specs, out_specs=out_specs, out_shape=out_shape,
        compiler_params=_cparams(("parallel", "arbitrary")),
    )(*arrays)
    return res


def _shift_down(x, prev8, i):
    first = jnp.where(i == 0, 0.0, prev8[7:8, :])
    return jnp.where(_iota(x.shape, 0) == 0, first, pltpu.roll(x, 1, 0))


def _shift_up(x, next8, i, nrow):
    last = jnp.where(i == nrow - 1, 0.0, next8[0:1, :])
    return jnp.where(_iota(x.shape, 0) == x.shape[0] - 1, last, pltpu.roll(x, x.shape[0] - 1, 0))


def _colsum(x):
    return jnp.sum(x, axis=0, keepdims=True)


def _extend(x, prev8, next8, i, nrow):
    return jnp.concatenate([jnp.where(i == 0, 0.0, prev8), x, jnp.where(i == nrow - 1, 0.0, next8)], axis=0)


def _taps(xe):
    return pltpu.roll(xe, 1, 0), xe, pltpu.roll(xe, xe.shape[0] - 1, 0)


def _mid(xe):
    return xe[8:xe.shape[0] - 8]


def _conv3(w, b, taps):
    return w[0:1] * taps[0] + w[1:2] * taps[1] + w[2:3] * taps[2] + b


def _conv3_t(w, d_ext):
    t = _taps(d_ext)
    return _mid(w[0:1] * t[2] + w[1:2] * t[1] + w[2:3] * t[0])


def ffn_conv_bwd_fn(i, n, g, gp, gn, u, up_, un, wg, wu, bg, bu, da, dap, dan):
    gt, ut = _taps(_extend(g, gp, gn, i, n)), _taps(_extend(u, up_, un, i, n))
    dae = _extend(da, dap, dan, i, n)
    gate, upv = _conv3(wg, bg, gt), _conv3(wu, bu, ut)
    dg, du = dae * upv * _dsilu(gate), dae * _silu(gate)
    dgm, dum = _mid(dg), _mid(du)
    sums = [_colsum(dgm * _mid(t)) for t in gt] + [_colsum(dum * _mid(t)) for t in ut] + [_colsum(dgm), _colsum(dum)]
    return (_conv3_t(wg, dg), _conv3_t(wu, du)) + tuple(sums)


def silu_conv_bwd_fn(i, n, xv, xp, xn, w, b, da, dap, dan):
    xt = _taps(_extend(xv, xp, xn, i, n))
    du = _extend(da, dap, dan, i, n) * _dsilu(_conv3(w, b, xt))
    dum = _mid(du)
    return (_conv3_t(w, du),) + tuple(_colsum(dum * _mid(t)) for t in xt) + (_colsum(dum),)


def _rms_fwd(x, w):
    r = lax.rsqrt(jnp.mean(x * x, axis=-1, keepdims=True) + EPS)
    return x * r * w


def _rms_bwd(dy, x, w):
    r = lax.rsqrt(jnp.mean(x * x, axis=-1, keepdims=True) + EPS)
    xh = x * r
    dxh = dy * w
    dx = r * (dxh - xh * jnp.mean(dxh * xh, axis=-1, keepdims=True))
    return dx, _colsum(dy * xh)


def _rope_tables(s):
    half = ROPE_DIM // 2
    inv_freq = jnp.power(ROPE_THETA, -jnp.arange(half, dtype=F32) * 2.0 / ROPE_DIM)
    ang = jnp.arange(s, dtype=F32)[:, None] * inv_freq[None, :]
    cos, sin = jnp.cos(ang), jnp.sin(ang)
    one, zero = jnp.ones((s, HD - ROPE_DIM), F32), jnp.zeros((s, HD - ROPE_DIM), F32)
    z8 = jnp.zeros((s, half), F32)
    c = jnp.concatenate([cos, cos, one], axis=1)
    sa = jnp.concatenate([-sin, z8, zero], axis=1)
    sb = jnp.concatenate([z8, sin, zero], axis=1)
    return [jnp.tile(t, (1, 2)) for t in (c, sa, sb)]


ATTN_CHUNK = 1024


def _attn_plan(s):
    plan = []
    for d in PATTERN_DILATIONS:
        per_res = ATTN_CHUNK // d
        tq = min(128, per_res)
        plan.append((d, tq, min(s // d, tq + 2 * BAND), per_res // tq, s // d))
    return plan


def _rows(start, size, d):
    return pl.ds(start, size) if d == 1 else pl.ds(start, size, stride=d)


def _for_tiles(chunk, pat, fn):
    d, tq, win, nblk, seq_len = pat
    for b in range(nblk):
        t0 = chunk * (ATTN_CHUNK // d) + b * tq
        kloc = jnp.clip(t0 - BAND, 0, seq_len - win)
        valid = jnp.abs(kloc + _iota((tq, win), 1) - (t0 + _iota((tq, win), 0))) <= BAND
        valid = jnp.concatenate([valid, valid], axis=0)
        if d == 1:
            fn(b * tq, pl.multiple_of(kloc, BAND), valid)
        else:
            def step(r, carry, qoff=d * b * tq, koff=d * kloc, valid=valid):
                fn(qoff + r, koff + r, valid)
                return carry
            lax.fori_loop(0, d, step, 0, unroll=min(d, 4))


def _stack_heads(x, head0):
    zero = jnp.zeros_like(x)
    return jnp.concatenate([jnp.where(head0, x, zero), jnp.where(head0, zero, x)], axis=0)


def _rope_pair(x, c, sa, sb):
    n = x.shape[1]
    return x * c + pltpu.roll(x, n - 8, 1) * sa + pltpu.roll(x, 8, 1) * sb


def _rope_pair_t(dy, c, sa, sb):
    n = dy.shape[1]
    return dy * c + pltpu.roll(dy * sa, 8, 1) + pltpu.roll(dy * sb, n - 8, 1)


def _attn_specs(s):
    whole = lambda off: pl.BlockSpec((s, 128), lambda p, c: (0, off + p))
    table = pl.BlockSpec((s, 128), lambda p, c: (0, 0))
    chunk = pl.BlockSpec((ATTN_CHUNK, 128), lambda p, c: (c, p))
    return whole, table, chunk


def attn_fwd_all(proj, tabs, name):
    s = proj.shape[0]
    plan = _attn_plan(s)
    whole, table, chunk_spec = _attn_specs(s)

    def body(q_ref, k_ref, v_ref, c_ref, sa_ref, sb_ref, o_ref, lse_ref, qs, ks, acc_s, m_s, l_s):
        chunk = pl.program_id(1)

        @pl.when(chunk == 0)
        def _():
            qs[...] = _rope_pair(q_ref[...], c_ref[...], sa_ref[...], sb_ref[...]) * (HD ** -0.5)
            ks[...] = _rope_pair(k_ref[...], c_ref[...], sa_ref[...], sb_ref[...])

        base = pl.multiple_of(chunk * ATTN_CHUNK, ATTN_CHUNK)
        for pi, pat in enumerate(plan):
            d, tq, win = pat[:3]
            head0 = _iota((tq, 128), 1) < HD

            def tile(qrow, krow, valid, pi=pi, d=d, tq=tq, win=win, head0=head0):
                qv = qs[_rows(base + qrow, tq, d), :].astype(BF16)
                kw = ks[_rows(krow, win, d), :].astype(BF16)
                vw = v_ref[_rows(krow, win, d), :].astype(BF16)
                v_ones = jnp.concatenate([vw, jnp.ones_like(vw)], axis=1)
                sc = jnp.where(valid, _dot(_stack_heads(qv, head0), kw, NT), -1e30)
                mh = jnp.max(sc, axis=1, keepdims=True)
                pv = _dot(jnp.exp(sc - mh).astype(BF16), v_ones)
                acc_s[pi, _rows(qrow, tq, d), :] = jnp.where(head0, pv[:tq, :128], pv[tq:, :128])
                m_s[pi, _rows(qrow, tq, d), :] = jnp.where(head0, mh[:tq], mh[tq:])
                l_s[pi, _rows(qrow, tq, d), :] = jnp.where(head0, pv[:tq, 128:], pv[tq:, 128:])

            _for_tiles(chunk, pat, tile)
        m_all = jnp.maximum(jnp.maximum(m_s[0], m_s[1]), m_s[2])
        e = [jnp.exp(m_s[k] - m_all) for k in range(3)]
        den = e[0] * l_s[0] + e[1] * l_s[1] + e[2] * l_s[2]
        o_ref[...] = (e[0] * acc_s[0] + e[1] * acc_s[1] + e[2] * acc_s[2]) / den
        lse_ref[...] = m_all + jnp.log(den)

    stat = pltpu.VMEM((3, ATTN_CHUNK, 128), F32)
    return _pcall(
        body, name=name, grid=(D // 128, s // ATTN_CHUNK),
        in_specs=[whole(0), whole(8), whole(16), table, table, table], out_specs=[chunk_spec, chunk_spec],
        out_shape=[jax.ShapeDtypeStruct((s, D), F32)] * 2,
        scratch_shapes=[pltpu.VMEM((s, 128), F32), pltpu.VMEM((s, 128), F32), stat, stat, stat],
        compiler_params=_cparams(("parallel", "arbitrary")),
    )(proj, proj, proj, *tabs)


def attn_bwd_all(proj, tabs, dmix, o, lse, name):
    s = proj.shape[0]
    plan = _attn_plan(s)
    whole, table, chunk_spec = _attn_specs(s)
    nchunk = s // ATTN_CHUNK

    def body(q_ref, k_ref, v_ref, c_ref, sa_ref, sb_ref, do_ref, o_ref, lse_ref, dq_ref, dk_ref, dv_ref, qs, ks, aug0_s, aug1_s):
        chunk = pl.program_id(1)

        @pl.when(chunk == 0)
        def _():
            qs[...] = _rope_pair(q_ref[...], c_ref[...], sa_ref[...], sb_ref[...]) * (HD ** -0.5)
            ks[...] = _rope_pair(k_ref[...], c_ref[...], sa_ref[...], sb_ref[...])
            dk_ref[...] = jnp.zeros_like(dk_ref)
            dv_ref[...] = jnp.zeros_like(dv_ref)

        base = pl.multiple_of(chunk * ATTN_CHUNK, ATTN_CHUNK)
        prod = do_ref[...] * o_ref[...]
        first = _iota(prod.shape, 1) < HD
        delta = jnp.where(first, jnp.sum(jnp.where(first, prod, 0.0), axis=1, keepdims=True),
                          jnp.sum(jnp.where(first, 0.0, prod), axis=1, keepdims=True))
        lane = _iota(prod.shape, 1)

        def as_lanes(lse_h, delta_h):
            a, b = [u.astype(F32) for u in _parts(lse_h, 3)], [u.astype(F32) for u in _parts(delta_h, 3)]
            out = jnp.zeros_like(lse_h)
            for k, u in enumerate(a + b):
                out = jnp.where(lane == k, u, out)
            return out

        lsev = lse_ref[...]
        aug0_s[...] = as_lanes(lsev, delta)
        aug1_s[...] = as_lanes(pltpu.roll(lsev, HD, 1), pltpu.roll(delta, HD, 1))
        for pi, pat in enumerate(plan):
            d, tq, win = pat[:3]
            head0 = _iota((tq, 128), 1) < HD

            def tile(qrow, krow, valid, pi=pi, d=d, tq=tq, win=win, head0=head0):
                qv = qs[_rows(base + qrow, tq, d), :].astype(BF16)
                kw = ks[_rows(krow, win, d), :].astype(BF16)
                vw = v_ref[_rows(krow, win, d), :].astype(BF16)
                dob = do_ref[_rows(qrow, tq, d), :].astype(BF16)
                aug = jnp.concatenate([aug0_s[_rows(qrow, tq, d), :], aug1_s[_rows(qrow, tq, d), :]], axis=0).astype(BF16)
                klane = _iota((win, 128), 1)
                minus_lse = jnp.where(klane < 3, -1.0, 0.0).astype(BF16)
                minus_delta = jnp.where((klane >= 3) & (klane < 6), -1.0, 0.0).astype(BF16)
                q2, do2 = _stack_heads(qv, head0), _stack_heads(dob, head0)
                s_lse = _dot(jnp.concatenate([q2, aug], axis=1), jnp.concatenate([kw, minus_lse], axis=1), NT)
                dp_delta = _dot(jnp.concatenate([do2, aug], axis=1), jnp.concatenate([vw, minus_delta], axis=1), NT)
                p = jnp.where(valid, jnp.exp(s_lse), 0.0)
                ds = (p * dp_delta).astype(BF16)
                dq2 = _dot(ds, kw)
                dk = _dot(ds, q2, TN)
                dv = _dot(p.astype(BF16), do2, TN)
                dqv = jnp.where(head0, dq2[:tq], dq2[tq:])
                if pi == 0:
                    dq_ref[_rows(qrow, tq, d), :] = dqv
                else:
                    dq_ref[_rows(qrow, tq, d), :] += dqv
                dk_ref[_rows(krow, win, d), :] += dk
                dv_ref[_rows(krow, win, d), :] += dv

            _for_tiles(chunk, pat, tile)
        tab = [t[pl.ds(base, ATTN_CHUNK), :] for t in (c_ref, sa_ref, sb_ref)]
        dq_ref[...] = _rope_pair_t(dq_ref[...] * (HD ** -0.5), *tab)

        @pl.when(chunk == nchunk - 1)
        def _():
            dk_ref[...] = _rope_pair_t(dk_ref[...], c_ref[...], sa_ref[...], sb_ref[...])

    return _pcall(
        body, name=name, grid=(D // 128, nchunk),
        in_specs=[whole(0), whole(8), whole(16), table, table, table, chunk_spec, chunk_spec, chunk_spec],
        out_specs=[chunk_spec, whole(0), whole(0)], out_shape=[jax.ShapeDtypeStruct((s, D), F32)] * 3,
        scratch_shapes=[pltpu.VMEM((s, 128), F32), pltpu.VMEM((s, 128), F32)] + [pltpu.VMEM((ATTN_CHUNK, 128), F32)] * 2,
        compiler_params=_cparams(("parallel", "arbitrary")),
    )(proj, proj, proj, *tabs, dmix, o, lse)


def _ssd_common(x_ref, b_ref, c_ref, dt_ref, dtt_ref, a_ref, ar_ref, rev):
    ii, jj = _iota((CHUNK, CHUNK), 0), _iota((CHUNK, CHUNK), 1)
    low = jj >= ii if rev else jj <= ii
    x, dtx = x_ref[...], dt_ref[...]
    bm, cm = b_ref[...].astype(BF16), c_ref[...].astype(BF16)
    a = dtx * a_ref[...]
    arow = dtt_ref[0] * ar_ref[0]
    lowb = low.astype(BF16)
    cs = _dot(lowb, jnp.concatenate(_parts(a, 3), axis=1))
    cs = cs[:, :128] + cs[:, 128:256] + cs[:, 256:]
    csr = _dot(jnp.concatenate([p.astype(F32) for p in _parts(arow, 3)], axis=0).astype(BF16), lowb, NT)
    csr = csr[0:8] + csr[8:16] + csr[16:24]
    last = 0 if rev else CHUNK - 1
    tot = cs[last:last + 1, :]
    xdt = x * dtx
    cb = _dot(cm, bm, NT)
    lmats = [jnp.exp(jnp.where(low, cs[:, HD * h:HD * h + 1] - csr[h:h + 1, :], -1e30)) for h in range(2)]
    return dict(x=x, dtx=dtx, bm=bm, cm=cm, a=a, cs=cs, tot=tot, xdt=xdt, cb=cb, lmats=lmats, low=low, last=last)


SSD_SUB = 4


def _ssd_specs(s, rev_order):
    nblk, rows = s // (SSD_SUB * CHUNK), SSD_SUB * CHUNK
    ci = (lambda c: nblk - 1 - c) if rev_order else (lambda c: c)
    tile = lambda off, div: pl.BlockSpec((rows, 128), lambda p, c: (ci(c), off + p // div))
    common = [tile(0, 1), tile(8, 2), tile(12, 2), tile(0, 1),
              pl.BlockSpec((1, 8, rows), lambda p, c: (p, 0, ci(c))),
              pl.BlockSpec((1, 128), lambda p, c: (0, p)),
              pl.BlockSpec((1, 8, 128), lambda p, c: (p, 0, 0))]
    hs = pl.BlockSpec((1, SSD_SUB, CHUNK, 128), lambda p, c: (p, ci(c), 0, 0))
    return nblk, common, tile(0, 1), hs


def _chunk_rows(ref, j):
    return ref.at[pl.ds(j * CHUNK, CHUNK), :]


def _ssd_chunk(refs, j):
    return [_chunk_rows(r, j) for r in refs[:4]] + [refs[4].at[:, :, pl.ds(j * CHUNK, CHUNK)], refs[5], refs[6]]


def _ssd_args(xbc, t):
    return [xbc, xbc, xbc, t["dt_exp"], t["dtt"], t["a_exp"], t["a_rows"]]


def ssd_fwd(xbc, dirs, name):
    s = xbc.shape[0]
    nd = len(dirs)
    specs = [_ssd_specs(s, t["rev"]) for t in dirs]
    nck = specs[0][0]

    def one(rev, x_ref, b_ref, c_ref, dt_ref, dtt_ref, a_ref, ar_ref, y_ref, hs_ref, h_scr):
        v = _ssd_common(x_ref, b_ref, c_ref, dt_ref, dtt_ref, a_ref, ar_ref, rev)
        xdtb = v["xdt"].astype(BF16)
        yd = _dot(jnp.concatenate([v["cb"] * v["lmats"][h] for h in range(2)], axis=0).astype(BF16), xdtb)
        h_in = h_scr[...]
        hs_ref[0, 0] = h_in
        y_off = _dot(v["cm"], h_in.astype(BF16)) * jnp.exp(v["cs"])
        y_ref[...] = jnp.where(_iota((CHUNK, 128), 1) < HD, yd[:CHUNK], yd[CHUNK:]) + y_off
        decay = jnp.exp(v["tot"] - v["cs"])
        h_scr[...] = jnp.exp(v["tot"]) * h_in + _dot(v["bm"], (v["xdt"] * decay).astype(BF16), TN)

    def body(*refs):
        @pl.when(pl.program_id(1) == 0)
        def _():
            for k in range(nd):
                refs[9 * nd + k][...] = jnp.zeros((CHUNK, 128), F32)

        for k, t in enumerate(dirs):
            y_ref, hs_ref = refs[7 * nd + 2 * k:7 * nd + 2 * k + 2]
            for j in (range(SSD_SUB)[::-1] if t["rev"] else range(SSD_SUB)):
                one(t["rev"], *_ssd_chunk(refs[7 * k:7 * k + 7], j), _chunk_rows(y_ref, j), hs_ref.at[:, pl.ds(j, 1)], refs[9 * nd + k])

    res = _pcall(
        body, name=name, grid=(8, nck), in_specs=[sp for t in specs for sp in t[1]],
        out_specs=[sp for t in specs for sp in (t[2], t[3])],
        out_shape=[jax.ShapeDtypeStruct((s, D), F32), jax.ShapeDtypeStruct((8, s // CHUNK, CHUNK, 128), F32)] * nd,
        scratch_shapes=[pltpu.VMEM((CHUNK, 128), F32)] * nd, compiler_params=_cparams(("parallel", "arbitrary")),
    )(*[a for t in dirs for a in _ssd_args(xbc, t)])
    return [(res[2 * k], res[2 * k + 1]) for k in range(nd)]


def ssd_bwd(xbc, dirs, dy, name):
    s = xbc.shape[0]
    nd = len(dirs)
    specs = [_ssd_specs(s, not t["rev"]) for t in dirs]
    nck = specs[0][0]

    def one(rev, x_ref, b_ref, c_ref, dt_ref, dtt_ref, a_ref, ar_ref, hs_ref, dy_ref,
            dx_ref, ddt_ref, db_ref, dc_ref, dal_ref, dh_scr):
        v = _ssd_common(x_ref, b_ref, c_ref, dt_ref, dtt_ref, a_ref, ar_ref, rev)
        bm, cm, cs, tot, xdt = v["bm"], v["cm"], v["cs"], v["tot"], v["xdt"]
        h_in, dh = hs_ref[0, 0], dh_scr[...]
        dyv = dy_ref[...]
        dyb = dyv.astype(BF16)
        etot, decay, ecs = jnp.exp(tot), jnp.exp(tot - cs), jnp.exp(cs)
        xdtb = xdt.astype(BF16)
        xdec = xdt * decay
        dch = (dyv * ecs).astype(BF16)
        hb, dhb = h_in.astype(BF16), dh.astype(BF16)
        y_off = _dot(cm, hb) * ecs
        dc = _dot(dch, hb, NT)
        dh_y = _dot(cm, dch, TN)
        dxdec = _dot(bm, dhb)
        db = _dot(xdec.astype(BF16), dhb, NT)
        state_term = xdec * dxdec
        dtot = _colsum(dh * h_in) * etot + _colsum(state_term)
        head0 = _iota((CHUNK, 128), 1) < HD
        ii, jj = _iota((CHUNK, CHUNK), 0), _iota((CHUNK, CHUNK), 1)
        low_t = jj <= ii if rev else jj >= ii
        not_low_t = (~low_t).astype(BF16)
        g = _dot(_stack_heads(dyb, head0), xdtb, NT)
        gl = [g[:CHUNK] * v["lmats"][0], g[CHUNK:] * v["lmats"][1]]
        dcb = gl[0] + gl[1]
        dxd = _dot(jnp.concatenate([v["cb"] * v["lmats"][h] for h in range(2)], axis=1).astype(BF16), dyb, TN)
        dxd = jnp.where(head0, dxd[:CHUNK], dxd[CHUNK:])
        w = _dot(not_low_t, jnp.concatenate([gl[h] * v["cb"] for h in range(2)], axis=0).astype(BF16), NT)
        da_l = [jnp.sum(jnp.where(low_t, w[:, CHUNK * h:CHUNK * h + CHUNK], 0.0), axis=1, keepdims=True) for h in range(2)]
        dxdt = dxdec * decay + dxd
        dcbb = dcb.astype(BF16)
        dc_ref[...] = dc + _dot(dcbb, bm)
        db_ref[...] = db + _dot(dcbb, cm, TN)
        dcs = dyv * y_off - state_term + jnp.where(_iota((CHUNK, 128), 0) == v["last"], dtot, 0.0)
        lowb = v["low"].astype(BF16)
        da = _dot(lowb, jnp.concatenate(_parts(dcs, 2), axis=1), TN)
        da = da[:, :128] + da[:, 128:]
        seg = ((ii < HD) == (jj < HD)).astype(BF16)
        sums = _dot(jnp.concatenate(_parts(da, 2) + _parts(dxdt * v["x"], 2), axis=0), seg)
        da = sums[:CHUNK] + sums[CHUNK:2 * CHUNK] + jnp.where(head0, da_l[0], da_l[1])
        ddt_x = sums[2 * CHUNK:3 * CHUNK] + sums[3 * CHUNK:]
        dx_ref[...] = dxdt * v["dtx"]
        ddt_ref[...] = ddt_x + da * a_ref[...]
        dal_ref[0] += _colsum(da * v["a"])
        dh_scr[...] = etot * dh + dh_y

    def body(*refs):
        @pl.when(pl.program_id(1) == 0)
        def _():
            for k in range(nd):
                refs[14 * nd + k][...] = jnp.zeros((CHUNK, 128), F32)
                refs[9 * nd + 5 * k + 4][...] = jnp.zeros((1, 8, 128), F32)

        for k, t in enumerate(dirs):
            ins, outs = refs[9 * k:9 * k + 9], refs[9 * nd + 5 * k:9 * nd + 5 * k + 5]
            for j in (range(SSD_SUB) if t["rev"] else range(SSD_SUB)[::-1]):
                one(t["rev"], *_ssd_chunk(ins[:7], j), ins[7].at[:, pl.ds(j, 1)], _chunk_rows(ins[8], j),
                    *[_chunk_rows(r, j) for r in outs[:4]], outs[4], refs[14 * nd + k])

    acc_spec = pl.BlockSpec((1, 8, 128), lambda p, c: (p, 0, 0))
    res = _pcall(
        body, name=name, grid=(8, nck), in_specs=[sp for t in specs for sp in t[1] + [t[3], t[2]]],
        out_specs=[sp for t in specs for sp in [t[2]] * 4 + [acc_spec]],
        out_shape=([jax.ShapeDtypeStruct((s, D), F32)] * 4 + [jax.ShapeDtypeStruct((8, 8, 128), F32)]) * nd,
        scratch_shapes=[pltpu.VMEM((CHUNK, 128), F32)] * nd, compiler_params=_cparams(("parallel", "arbitrary")),
    )(*[a for t in dirs for a in _ssd_args(xbc, t) + [t["hs"], dy]])
    return [res[5 * k:5 * k + 5] for k in range(nd)]


def _group_norm_stats(g):
    r = [lax.rsqrt(jnp.mean(g[:, 256 * k:256 * k + 256] ** 2, axis=-1, keepdims=True) + EPS) for k in range(4)]
    grp = _iota(g.shape, 1) // 256
    return jnp.where(grp == 0, r[0], jnp.where(grp == 1, r[1], jnp.where(grp == 2, r[2], r[3])))


def _group_mean(t):
    m = [jnp.mean(t[:, 256 * k:256 * k + 256], axis=-1, keepdims=True) for k in range(4)]
    grp = _iota(t.shape, 1) // 256
    return jnp.where(grp == 0, m[0], jnp.where(grp == 1, m[1], jnp.where(grp == 2, m[2], m[3])))


def _mesh_pos():
    return lax.axis_index("x"), lax.axis_index("y"), lax.axis_index("c")


HBM = pl.BlockSpec(memory_space=pltpu.HBM)
SEM = pl.BlockSpec(memory_space=pltpu.SEMAPHORE)
EFFECT = pltpu.SideEffectType.DATAFLOW_SIDE_EFFECTING


def _hbm(t):
    return pltpu.with_memory_space_constraint(t, pltpu.HBM)


def _other_chips(x, y):
    return [(1 - x, y), (x, 1 - y), (1 - x, 1 - y)]


def _peer(x, y, c, m):
    return x ^ (m >> 2), y ^ ((m >> 1) & 1), c ^ (m & 1)


def gather_start(srcs_a, srcs_b):
    srcs = [_hbm(t) for t in list(srcs_a) + list(srcs_b)]
    n, na = len(srcs), len(srcs_a)
    lands = [_hbm(lax.empty((4,) + t.shape, t.dtype)) for t in srcs]

    def body(*refs):
        src, land = refs[:n], refs[n:2 * n]
        sems = refs[2 * n:2 * n + 4]
        x, y, c = _mesh_pos()
        for k in range(n):
            for j, (px, py) in enumerate(_other_chips(x, y)):
                send, recv, idx = (sems[0], sems[1], 3 * k + j) if k < na else (sems[2], sems[3], 3 * (k - na) + j)
                pltpu.make_async_remote_copy(src_ref=src[k], dst_ref=land[k].at[2 * x + y], send_sem=send.at[idx],
                                             recv_sem=recv.at[idx], device_id=(px, py, c), device_id_type=MESH).start()

    sem_a, sem_b = pltpu.SemaphoreType.DMA((3 * na,)), pltpu.SemaphoreType.DMA((3 * (n - na),))
    res = _pcall(
        body, name="gather_start", in_specs=[HBM] * (2 * n), out_specs=[SEM] * 4 + [HBM] * (2 * n),
        out_shape=[sem_a, sem_a, sem_b, sem_b] + [pltpu.HBM(t.shape, t.dtype) for t in srcs + lands],
        input_output_aliases={i: 4 + i for i in range(2 * n)},
        compiler_params=pltpu.CompilerParams(has_side_effects=EFFECT),
    )(*srcs, *lands)
    thru_src, thru_land = res[4:4 + n], res[4 + n:]
    return ((res[0], res[1], thru_src[:na], thru_land[:na]), (res[2], res[3], thru_src[na:], thru_land[na:]))


def gather_wait(group, name, after=None):
    send, recv, srcs, lands = group
    n = len(srcs)

    def body(*refs):
        src, land, send_ref, recv_ref = refs[:n], refs[n:2 * n], refs[2 * n], refs[2 * n + 1]
        x, y, c = _mesh_pos()
        for j, (px, py) in enumerate(_other_chips(x, y)):
            for k in range(n):
                cp = pltpu.make_async_remote_copy(src_ref=src[k], dst_ref=land[k].at[2 * px + py], send_sem=send_ref.at[3 * k + j],
                                                  recv_sem=recv_ref.at[3 * k + j], device_id=(px, py, c), device_id_type=MESH)
                cp.wait_send()
                cp.wait_recv()

    extra = [] if after is None else [after]
    res = _pcall(
        body, name=name, in_specs=[HBM] * (2 * n) + [SEM, SEM] + [pl.BlockSpec(memory_space=pl.ANY)] * len(extra),
        out_specs=[HBM] * (2 * n), out_shape=[pltpu.HBM(t.shape, t.dtype) for t in list(srcs) + list(lands)],
        input_output_aliases={i: i for i in range(2 * n)}, compiler_params=pltpu.CompilerParams(has_side_effects=EFFECT),
    )(*srcs, *lands, send, recv, *extra)
    return res[:n], res[n:]


def scatter_start(pieces, smalls, name):
    srcs = [_hbm(t) for t in list(pieces) + list(smalls)]
    n, npc = len(srcs), len(pieces)
    lands = [_hbm(lax.empty((8,) + (t.shape[2:] if k < npc else t.shape), t.dtype)) for k, t in enumerate(srcs)]

    def body(*refs):
        src, land, send, recv = refs[:n], refs[n:2 * n], refs[2 * n], refs[2 * n + 1]
        token = refs[-1]
        x, y, c = _mesh_pos()
        for m in range(1, 8):
            px, py, pc = _peer(x, y, c, m)
            for k in range(n):
                s_ref = src[k].at[2 * px + py, pc] if k < npc else src[k]
                d_ref = land[k].at[m] if k < npc else land[k].at[4 * x + 2 * y + c]
                pltpu.make_async_remote_copy(src_ref=s_ref, dst_ref=d_ref, send_sem=send.at[7 * k + m - 1], recv_sem=recv.at[7 * k + m - 1],
                                             device_id=(px, py, pc), device_id_type=MESH).start()
        token[...] = jnp.zeros_like(token)

    sem = pltpu.SemaphoreType.DMA((7 * n,))
    res = _pcall(
        body, name=name, in_specs=[HBM] * (2 * n),
        out_specs=[SEM, SEM] + [HBM] * (2 * n) + [pl.BlockSpec(memory_space=pltpu.VMEM)],
        out_shape=[sem, sem] + [pltpu.HBM(t.shape, t.dtype) for t in srcs + lands] + [jax.ShapeDtypeStruct((8, 128), F32)],
        input_output_aliases={i: 2 + i for i in range(2 * n)},
        compiler_params=pltpu.CompilerParams(has_side_effects=EFFECT),
    )(*srcs, *lands)
    return (res[0], res[1], res[2:2 + n], res[2 + n:2 + 2 * n], npc), res[-1]


def scatter_wait(group, name, after=None):
    send, recv, srcs, lands, npc = group
    n = len(srcs)

    def body(*refs):
        src, land, send_ref, recv_ref = refs[:n], refs[n:2 * n], refs[2 * n], refs[2 * n + 1]
        x, y, c = _mesh_pos()
        for m in range(1, 8):
            px, py, pc = _peer(x, y, c, m)
            for k in range(n):
                s_ref = src[k].at[0, 0] if k < npc else src[k]
                d_ref = land[k].at[m] if k < npc else land[k].at[4 * px + 2 * py + pc]
                cp = pltpu.make_async_remote_copy(src_ref=s_ref, dst_ref=d_ref, send_sem=send_ref.at[7 * k + m - 1],
                                                  recv_sem=recv_ref.at[7 * k + m - 1], device_id=(px, py, pc), device_id_type=MESH)
                cp.wait_send()
                cp.wait_recv()

    extra = [] if after is None else [after]
    res = _pcall(
        body, name=name, in_specs=[HBM] * (2 * n) + [SEM, SEM] + [pl.BlockSpec(memory_space=pl.ANY)] * len(extra),
        out_specs=[HBM] * (2 * n), out_shape=[pltpu.HBM(t.shape, t.dtype) for t in list(srcs) + list(lands)],
        input_output_aliases={i: i for i in range(2 * n)}, compiler_params=pltpu.CompilerParams(has_side_effects=EFFECT),
    )(*srcs, *lands, send, recv, *extra)
    return res[:n], res[n:]


def swap_halves(pieces, name):
    n = len(pieces)
    whole = pl.BlockSpec(memory_space=pltpu.VMEM)

    def body(*refs):
        p_refs, o_refs, send_sems, recv_sems, local_sems = refs[:n], refs[n:2 * n], refs[2 * n], refs[2 * n + 1], refs[2 * n + 2]
        x, y, c = _mesh_pos()
        local = [pltpu.make_async_copy(p_refs[k], o_refs[k].at[c], local_sems.at[k]) for k in range(n)]
        for cp in local:
            cp.start()

        def copy(k, slot):
            return pltpu.make_async_remote_copy(src_ref=p_refs[k], dst_ref=o_refs[k].at[slot], send_sem=send_sems.at[k],
                                                recv_sem=recv_sems.at[k], device_id=(x, y, 1 - c), device_id_type=MESH)

        for k in range(n):
            copy(k, c).start()
        for k in range(n):
            copy(k, 1 - c).wait_recv()
        for k in range(n):
            copy(k, c).wait_send()
        for cp in local:
            cp.wait()

    return _pcall(
        body, name=name, in_specs=[whole] * n, out_specs=[whole] * n,
        out_shape=[jax.ShapeDtypeStruct((2,) + t.shape, t.dtype) for t in pieces],
        scratch_shapes=[pltpu.SemaphoreType.DMA((n,)), pltpu.SemaphoreType.DMA((n,)), pltpu.SemaphoreType.DMA((n,))],
        compiler_params=_cparams(),
    )(*pieces)


def adamw(w, g, m, v, name):
    rows, cols = w.shape
    tm = rows
    for t in (256, 352, 128, 144, 64, 32, 16, 8):
        if rows % t == 0:
            tm = t
            break

    def fn(i, nrow, wv, gv, mv, vv):
        mn = ADAM_B1 * mv + (1.0 - ADAM_B1) * gv
        vn = ADAM_B2 * vv + (1.0 - ADAM_B2) * (gv * gv)
        m_hat = mn / (1.0 - ADAM_B1 ** ADAM_STEP)
        v_hat = vn / (1.0 - ADAM_B2 ** ADAM_STEP)
        delta = -ADAM_LR * (m_hat / (jnp.sqrt(v_hat) + ADAM_EPS) + ADAM_WD * wv)
        return delta, mn, vn

    return ew(fn, name, rows, tm, 1, [(t, "row", cols, 0) for t in (w, g, m, v)], [(cols, F32, cols)] * 3)


REST = ("w_out", "w_up", "w_down")
SMALL = ("norm1_w", "ssm_conv_w", "ssm_conv_b", "a_log_f", "a_log_b", "dt_bias_f", "dt_bias_b", "d_skip",
         "ssm_norm_w", "norm2_w", "ffn_conv_w", "ffn_conv_b", "final_norm_w")
WEIGHTS = ("norm1_w", "w_in", "ssm_conv_w", "ssm_conv_b", "a_log_f", "a_log_b", "dt_bias_f", "dt_bias_b", "d_skip",
           "ssm_norm_w", "w_out", "norm2_w", "w_up", "ffn_conv_w", "ffn_conv_b", "w_down", "final_norm_w")
INPUTS = ("x",) + WEIGHTS + ("loss_target",) + tuple("m_" + n for n in WEIGHTS) + tuple("v_" + n for n in WEIGHTS)


def _flat_rows(parts, width, rows):
    flat = jnp.concatenate([p.reshape(-1) for p in parts])
    return jnp.pad(flat, (0, rows * width - flat.shape[0])).reshape(rows, width)


def _split_flat(flat, shapes):
    out, pos = [], 0
    flat = flat.reshape(-1)
    for shp in shapes:
        n = int(np.prod(shp))
        out.append(flat[pos:pos + n].reshape(shp))
        pos += n
    return out


def _col_shards(t, nshard):
    r, c = t.shape
    return t.reshape(r, nshard, c // nshard).transpose(1, 0, 2)


def _row_shards(t, nshard):
    r, c = t.shape
    return t.reshape(nshard, r // nshard, c)


def kernel(x, norm1_w, w_in, ssm_conv_w, ssm_conv_b, a_log_f, a_log_b, dt_bias_f, dt_bias_b, d_skip, ssm_norm_w, w_out, norm2_w, w_up, ffn_conv_w, ffn_conv_b, w_down, final_norm_w, loss_target, m_norm1_w, m_w_in, m_ssm_conv_w, m_ssm_conv_b, m_a_log_f, m_a_log_b, m_dt_bias_f, m_dt_bias_b, m_d_skip, m_ssm_norm_w, m_w_out, m_norm2_w, m_w_up, m_ffn_conv_w, m_ffn_conv_b, m_w_down, m_final_norm_w, v_norm1_w, v_w_in, v_ssm_conv_w, v_ssm_conv_b, v_a_log_f, v_a_log_b, v_dt_bias_f, v_dt_bias_b, v_d_skip, v_ssm_norm_w, v_w_out, v_norm2_w, v_w_up, v_ffn_conv_w, v_ffn_conv_b, v_w_down, v_final_norm_w):
    p = dict(zip(INPUTS, (x, norm1_w, w_in, ssm_conv_w, ssm_conv_b, a_log_f, a_log_b, dt_bias_f, dt_bias_b, d_skip, ssm_norm_w, w_out, norm2_w, w_up, ffn_conv_w, ffn_conv_b, w_down, final_norm_w, loss_target, m_norm1_w, m_w_in, m_ssm_conv_w, m_ssm_conv_b, m_a_log_f, m_a_log_b, m_dt_bias_f, m_dt_bias_b, m_d_skip, m_ssm_norm_w, m_w_out, m_norm2_w, m_w_up, m_ffn_conv_w, m_ffn_conv_b, m_w_down, m_final_norm_w, v_norm1_w, v_w_in, v_ssm_conv_w, v_ssm_conv_b, v_a_log_f, v_a_log_b, v_dt_bias_f, v_dt_bias_b, v_d_skip, v_ssm_norm_w, v_w_out, v_norm2_w, v_w_up, v_ffn_conv_w, v_ffn_conv_b, v_w_down, v_final_norm_w)))
    x = p["x"][0]
    tgt = p["loss_target"][0]
    s = x.shape[0]
    chip = 2 * lax.axis_index("x") + lax.axis_index("y")

    own_slot = lambda land, mine, slot: lax.dynamic_update_slice_in_dim(land, mine[None], slot, axis=0)
    src_in = p["w_in"][0].astype(BF16)
    src_rest = [p[n][0].astype(BF16) for n in REST]
    small_w = _flat_rows([p["ssm_conv_w"][0], p["ffn_conv_w"][0]], 128, 48)
    gather_in, gather_rest = gather_start([src_in, small_w], src_rest)
    (src_in, small_w), (wg_in, sg) = gather_wait(gather_in, "gather_wait_in")
    w_in = own_slot(wg_in, src_in, chip).transpose(1, 0, 2).reshape(D, -1)
    sg = own_slot(sg, small_w, chip)
    n_in = w_in.shape[1]
    n_main = 6 * D
    w_main = w_in[:, :n_main]
    w_dt = jnp.pad(w_in[:, n_main:], ((0, 0), (0, 128 - (n_in - n_main))))
    sgf = sg.reshape(4, -1)
    n_sc, n_fc = p["ssm_conv_w"].shape[1], p["ffn_conv_w"].shape[1]
    ssm_cw = sgf[:, :n_sc * 3].reshape(-1, 3).T
    ffn_cw = sgf[:, n_sc * 3:(n_sc + n_fc) * 3].reshape(-1, 3).T
    ssm_cb, ffn_cb = p["ssm_conv_b"], p["ffn_conv_b"]
    n1w, n2w, snw, fnw = p["norm1_w"], p["norm2_w"], p["ssm_norm_w"], p["final_norm_w"].reshape(1, D)

    h1, = ew(lambda i, n, xv, w: _rms_fwd(xv, w), "rms1", s, 256, 1,
             [(x, "row", D, 0), (n1w, "const", D, 0)], [(D, BF16, D)])
    proj = matmul(h1, w_main, "nn", "in_proj")
    proj_dt = matmul(h1, w_dt, "nn", "in_proj_dt")
    tabs = _rope_tables(s)
    attn, lse = attn_fwd_all(proj, tabs, "attn_fwd")

    def conv_silu_fn(i, n, xv, xp, xn, w, b):
        return _silu(w[0:1] * _shift_down(xv, xp, i) + w[1:2] * xv + w[2:3] * _shift_up(xv, xn, i, n) + b)

    xbc_act, = ew(conv_silu_fn, "ssm_conv", s, 256, 2,
                  [(proj, "row", D, 4), (proj, "prev", D, 4), (proj, "next", D, 4),
                   (ssm_cw, "const", D, 0), (ssm_cb, "const", D, 0)], [(2 * D, F32, D)])
    dt_bias = jnp.pad(jnp.concatenate([p["dt_bias_f"], p["dt_bias_b"]], axis=1), ((0, 0), (0, 96)))

    lanes_of = np.arange(128)[:, None] == np.arange(D)[None, :] // HD
    spread = [jnp.asarray(np.roll(lanes_of, 16 * k, axis=0), BF16) for k in range(2)]

    def softplus_fn(i, n, r, b, ef, eb):
        t = r + b
        dtv = jnp.maximum(t, 0.0) + jnp.log(1.0 + jnp.exp(-jnp.abs(t)))
        parts = _parts(dtv, 3)
        return dtv, sum(_dot(q, ef) for q in parts), sum(_dot(q, eb) for q in parts)

    dt, dt_exp_f, dt_exp_b = ew(softplus_fn, "dt_softplus", s, 512, 1,
                                [(proj_dt, "row", 128, 0), (dt_bias, "const", 128, 0), (spread[0], "const", D, 0), (spread[1], "const", D, 0)],
                                [(128, F32, 128), (D, F32, D), (D, F32, D)])
    d_exp = jnp.repeat(p["d_skip"], HD, axis=1)
    ssd = []
    for k, (a_log, rev) in enumerate(((p["a_log_f"], False), (p["a_log_b"], True))):
        dt_k = dt[:, 16 * k:16 * k + 16]
        a_head = -jnp.exp(a_log)
        dt_exp = (dt_exp_f, dt_exp_b)[k]
        dtt = jnp.pad(dt_k.T.reshape(8, 2, s), ((0, 0), (0, 6), (0, 0)))
        a_exp = jnp.repeat(a_head, HD, axis=1)
        a_rows = jnp.broadcast_to(jnp.pad(a_head.reshape(8, 2), ((0, 0), (0, 6)))[:, :, None], (8, 8, 128))
        ssd.append(dict(dt_exp=dt_exp, dtt=dtt, a_exp=a_exp, a_rows=a_rows, rev=rev))
    for t, (y_k, hs_k) in zip(ssd, ssd_fwd(xbc_act, ssd, "ssd_fwd")):
        t["y"], t["hs"] = y_k, hs_k

    def gate_fn(i, n, yf, yb, xs, z, dsk, w):
        g = (yf + yb + dsk * xs) * _silu(z)
        return g * _group_norm_stats(g) * w

    ssm_out, = ew(gate_fn, "ssm_gate_norm", s, 256, 1,
                  [(ssd[0]["y"], "row", D, 0), (ssd[1]["y"], "row", D, 0), (xbc_act, "row", D, 0), (proj, "row", D, 3),
                   (d_exp, "const", D, 0), (snw, "const", D, 0)], [(D, F32, D)])
    mix = jnp.concatenate([attn, ssm_out], axis=1).astype(BF16)
    src_rest, wg_rest = gather_wait(gather_rest, "gather_wait_rest", after=mix)
    wg_rest = [own_slot(land, mine, chip) for land, mine in zip(wg_rest, src_rest)]
    w_out = wg_rest[0].reshape(-1, D)
    w_up = wg_rest[1].transpose(1, 0, 2).reshape(D, -1)
    w_down = wg_rest[2].reshape(-1, D)
    mix_w = matmul(mix, w_out, "nn", "out_proj")

    def res_rms_fn(i, n, xv, mw, w):
        x1v = xv + mw
        return x1v, _rms_fwd(x1v, w)

    x1, h2 = ew(res_rms_fn, "res_rms2", s, 256, 1, [(x, "row", D, 0), (mix_w, "row", D, 0), (n2w, "const", D, 0)],
                [(D, F32, D), (D, BF16, D)])
    hw = matmul(h2, w_up, "nn", "ffn_up")
    fw = D_FF // 2
    nfb = D_FF // fw
    ffn_conv_ins = [(hw, "row", fw, 0), (hw, "prev", fw, 0), (hw, "next", fw, 0),
                    (hw, "row", fw, nfb), (hw, "prev", fw, nfb), (hw, "next", fw, nfb),
                    (ffn_cw, "const", fw, 0), (ffn_cw, "const", fw, nfb), (ffn_cb, "const", fw, 0), (ffn_cb, "const", fw, nfb)]

    def ffn_conv(i, n, g, gp, gn, u, up_, un, wg_, wu, bg, bu):
        gs = (_shift_down(g, gp, i), g, _shift_up(g, gn, i, n))
        us = (_shift_down(u, up_, i), u, _shift_up(u, un, i, n))
        gate = wg_[0:1] * gs[0] + wg_[1:2] * gs[1] + wg_[2:3] * gs[2] + bg
        upv = wu[0:1] * us[0] + wu[1:2] * us[1] + wu[2:3] * us[2] + bu
        return gate, upv, gs, us

    def glu_fn(i, n, *blocks):
        gate, upv, _, _ = ffn_conv(i, n, *blocks)
        return _silu(gate) * upv

    act, = ew(glu_fn, "ffn_conv_glu", s, 256, nfb, ffn_conv_ins, [(D_FF, BF16, fw)])
    ffn = matmul(act, w_down, "nn", "ffn_down")

    def head_fn(i, n, x1v, fv, tv, w):
        x2 = x1v + fv
        r = lax.rsqrt(jnp.mean(x2 * x2, axis=-1, keepdims=True) + EPS)
        xh = x2 * r
        diff = xh * w - tv
        loss = 0.5 * jnp.sum(jnp.mean(diff * diff, axis=-1, keepdims=True), axis=0, keepdims=True)
        dout = diff * (1.0 / D)
        dxh = dout * w
        dx2 = r * (dxh - xh * jnp.mean(dxh * xh, axis=-1, keepdims=True))
        return dx2, jnp.broadcast_to(loss, (1, 128)), _colsum(dout * xh)

    dx2, loss_acc, g_fnw = ew(head_fn, "loss_head", s, 256, 1,
                              [(x1, "row", D, 0), (ffn, "row", D, 0), (tgt, "row", D, 0), (fnw, "const", D, 0)],
                              [(D, F32, D)], [(128, 128), (D, D)])
    loss = lax.psum(loss_acc[0, 0], ("x", "y", "c"))

    g_w_down = matmul(act, dx2, "tn", "d_w_down")
    dact = matmul(dx2, w_down, "nt", "d_act")

    res = ew(ffn_conv_bwd_fn, "ffn_conv_glu_bwd", s, 256, nfb,
             ffn_conv_ins + [(dact, "row", fw, 0), (dact, "prev", fw, 0), (dact, "next", fw, 0)],
             [(D_FF, F32, fw)] * 2, [(D_FF, fw)] * 8)
    dhw_g, dhw_u = res[0], res[1]
    g_ffn_cw = jnp.concatenate([jnp.concatenate(res[2:5], axis=0), jnp.concatenate(res[5:8], axis=0)], axis=1).T
    g_ffn_cb = jnp.concatenate([res[8], res[9]], axis=1)

    g_w_up = jnp.concatenate([matmul(h2, dhw_g, "tn", "d_w_up_gate"), matmul(h2, dhw_u, "tn", "d_w_up_up")], axis=1)
    dh2_a = matmul(dhw_g, w_up, "nt", "d_h2_gate")
    dh2_b = matmul(dhw_u, w_up, "nt", "d_h2_up", b_k_off=D_FF // _pick(D_FF, 1408))

    def res_rms_bwd_fn(i, n, dres, da, db, xin, w):
        dx, dw = _rms_bwd(da + db, xin, w)
        return dres + dx, dw

    dx1, g_n2w = ew(res_rms_bwd_fn, "res_rms2_bwd", s, 256, 1,
                    [(dx2, "row", D, 0), (dh2_a, "row", D, 0), (dh2_b, "row", D, 0), (x1, "row", D, 0), (n2w, "const", D, 0)],
                    [(D, F32, D)], [(D, D)])

    g_w_out = matmul(mix, dx1, "tn", "d_w_out")
    to_pieces = lambda t: t.astype(BF16).reshape(4, 2, t.shape[1] // 2, t.shape[2])
    shards_rest = [_row_shards(g_w_out, 4), _col_shards(g_w_up, 4), _row_shards(g_w_down, 4)]
    scatter_rest, token = scatter_start([to_pieces(t) for t in shards_rest], [], "scatter_start_rest")
    dmix = matmul(dx1, w_out, "nt", "d_mix", after=token)
    ii, jj = np.arange(D)[:, None] // HD, np.arange(D)[None, :] // HD
    seg = jnp.asarray(ii == jj, BF16)

    def gate_bwd_fn(i, n, dout, yf, yb, xs, z, dsk, w, segm):
        yt = yf + yb + dsk * xs
        sz = _silu(z)
        g = yt * sz
        r = _group_norm_stats(g)
        gh = g * r
        dn = dout * w
        dg = r * (dn - gh * _group_mean(dn * gh))
        dy = dg * sz
        dsk_lane = jnp.broadcast_to(_colsum(dy * xs), (8, D))
        return dy, dg * yt * _dsilu(z), _colsum(dout * gh), sum(_dot(q, segm) for q in _parts(dsk_lane, 2))[0:1]

    dy, dz, g_snw, g_dskip_l = ew(
        gate_bwd_fn, "ssm_gate_norm_bwd", s, 256, 1,
        [(dmix, "row", D, 1), (ssd[0]["y"], "row", D, 0), (ssd[1]["y"], "row", D, 0), (xbc_act, "row", D, 0),
         (proj, "row", D, 3), (d_exp, "const", D, 0), (snw, "const", D, 0), (seg, "const", D, 0)],
        [(D, F32, D)] * 2, [(D, D)] * 2)
    sb = ssd_bwd(xbc_act, ssd, dy, "ssd_bwd")

    def dxbc_act_fn(i, n, dxf, dxb, dyv, dsk, dbf, dbb, dcf, dcb_):
        db, dc = dbf + dbb, dcf + dcb_
        db = [db[:, 256 * g:256 * g + 128] + db[:, 256 * g + 128:256 * g + 256] for g in range(4)]
        dc = [dc[:, 256 * g:256 * g + 128] + dc[:, 256 * g + 128:256 * g + 256] for g in range(4)]
        return jnp.concatenate([dxf + dxb + dyv * dsk] + db + dc, axis=1)

    dxbc_act, = ew(dxbc_act_fn, "d_xbc_act", s, 256, 1,
                   [(sb[0][0], "row", D, 0), (sb[1][0], "row", D, 0), (dy, "row", D, 0), (d_exp, "const", D, 0),
                    (sb[0][2], "row", D, 0), (sb[1][2], "row", D, 0), (sb[0][3], "row", D, 0), (sb[1][3], "row", D, 0)],
                   [(2 * D, F32, 2 * D)])

    res = ew(silu_conv_bwd_fn, "ssm_conv_bwd", s, 256, 2,
             [(proj, "row", D, 4), (proj, "prev", D, 4), (proj, "next", D, 4), (ssm_cw, "const", D, 0), (ssm_cb, "const", D, 0),
              (dxbc_act, "row", D, 0), (dxbc_act, "prev", D, 0), (dxbc_act, "next", D, 0)], [(2 * D, F32, D)], [(2 * D, D)] * 4)
    dxbc = res[0]
    g_ssm_cw = jnp.concatenate(res[1:4], axis=0).T
    g_ssm_cb = res[4]
    ddt = jnp.pad(jnp.concatenate([sb[0][1][:, ::HD], sb[1][1][:, ::HD]], axis=1), ((0, 0), (0, 96)))

    def dt_bwd_fn(i, n, dd, r, b):
        dr = dd * _sigmoid(r + b)
        return dr, _colsum(dr)

    dproj_dt, g_dt_bias = ew(dt_bwd_fn, "dt_softplus_bwd", s, 512, 1,
                             [(ddt, "row", 128, 0), (proj_dt, "row", 128, 0), (dt_bias, "const", 128, 0)],
                             [(128, F32, 128)], [(128, 128)])
    g_a_log = [t[4][:, 0, ::HD].reshape(1, 16) for t in sb]

    dq, dk, dv = attn_bwd_all(proj, tabs, dmix, attn, lse, "attn_bwd")

    dproj = jnp.concatenate([dq, dk, dv, dz, dxbc], axis=1).astype(BF16)
    g_w_in = jnp.concatenate([matmul(h1, dproj, "tn", "d_w_in"), matmul(h1, dproj_dt, "tn", "d_w_in_dt")[:, :n_in - n_main]], axis=1)
    scatter_in, token = scatter_start([to_pieces(_col_shards(g_w_in, 4))], [], "scatter_start_in")
    dh1_a = matmul(dproj, w_main, "nt", "d_h1", after=token)
    dh1_b = matmul(dproj_dt, w_dt, "nt", "d_h1_dt")
    grad_x, g_n1w = ew(res_rms_bwd_fn, "rms1_bwd", s, 256, 1,
                       [(dx1, "row", D, 0), (dh1_a, "row", D, 0), (dh1_b, "row", D, 0), (x, "row", D, 0), (n1w, "const", D, 0)],
                       [(D, F32, D)], [(D, D)])

    small_g = {"norm1_w": g_n1w, "ssm_conv_w": g_ssm_cw, "ssm_conv_b": g_ssm_cb, "a_log_f": g_a_log[0], "a_log_b": g_a_log[1],
               "dt_bias_f": g_dt_bias[:, :16], "dt_bias_b": g_dt_bias[:, 16:32], "d_skip": g_dskip_l[:, ::HD],
               "ssm_norm_w": g_snw, "norm2_w": g_n2w, "ffn_conv_w": g_ffn_cw, "ffn_conv_b": g_ffn_cb, "final_norm_w": g_fnw}
    small_shapes = [small_g[n].shape for n in SMALL]
    scatter_small, _ = scatter_start([], [_flat_rows([small_g[n] for n in SMALL], 128, SMALL_ROWS)], "scatter_start_small")
    core = lax.axis_index("c")

    def sum8_fn(i, n, *v):
        t = v[0].astype(F32)
        for u in v[1:]:
            t = t + u.astype(F32)
        return t

    def sum_pieces(sent, got, name):
        rows, w = got.shape[1:]
        tm = 256 if rows % 256 == 0 else rows
        mine = lax.dynamic_slice(sent, (chip, core, 0, 0), (1, 1, rows, w)).reshape(rows, w)
        ins = [(mine, "row", w, 0)] + [(got.reshape(8 * rows, w), "row", w, 0, k * (rows // tm)) for k in range(1, 8)]
        return ew(sum8_fn, name, rows, tm, 1, ins, [(w, F32, w)])[0]

    grads, delta, new_m, new_v = {}, {}, {}, {}

    def finish(names, sent, got, tag):
        summed = swap_halves([sum_pieces(a, b, "sum_pieces_" + n) for a, b, n in zip(sent, got, names)], "swap_halves_" + tag)
        for n, t in zip(names, summed):
            shp = p[n].shape
            grads[n] = t.reshape(shp)
            r = [u.reshape(shp[1:]) for u in (p[n], grads[n], p["m_" + n], p["v_" + n])]
            delta[n], new_m[n], new_v[n] = [u.reshape(shp) for u in adamw(*r, "adamw_" + n)]

    finish(REST, *scatter_wait(scatter_rest, "scatter_wait_rest"), "rest")
    finish(("w_in",), *scatter_wait(scatter_in, "scatter_wait_in", after=new_v[REST[-1]]), "w_in")
    (sent_small,), (got_small,) = scatter_wait(scatter_small, "scatter_wait_small", after=new_v["w_in"])
    got_small = own_slot(got_small, sent_small, 2 * chip + core)
    small_sum, = ew(sum8_fn, "sum_small", SMALL_ROWS, SMALL_ROWS, 1,
                    [(got_small.reshape(8 * SMALL_ROWS, 128), "row", 128, 0, k) for k in range(8)], [(128, F32, 128)])
    for n, g in zip(SMALL, _split_flat(small_sum, small_shapes)):
        if n in ("ssm_conv_w", "ffn_conv_w"):
            rows = p[n].shape[1]
            g = lax.dynamic_slice_in_dim(g, chip * rows, rows, axis=0)
        grads[n] = g.reshape(p[n].shape)

    shapes = [p[n].shape for n in SMALL]
    total = sum(int(np.prod(sh)) for sh in shapes)
    rows = -(-total // 1024) * 8
    packs = [_flat_rows([t[n] for n in SMALL], 128, rows)
             for t in (p, grads, {n: p["m_" + n] for n in SMALL}, {n: p["v_" + n] for n in SMALL})]
    for dst, t in zip((delta, new_m, new_v), adamw(*packs, "adamw_small")):
        for n, u in zip(SMALL, _split_flat(t, shapes)):
            dst[n] = u
    return (loss, grad_x[None], *[grads[n] for n in WEIGHTS], *[delta[n] for n in WEIGHTS],
            *[new_m[n] for n in WEIGHTS], *[new_v[n] for n in WEIGHTS])
```

```python
import numpy as np
import jax
import jax.numpy as jnp
from jax import lax
from jax.experimental import pallas as pl
from jax.experimental.pallas import tpu as pltpu

F32, BF16 = jnp.float32, jnp.bfloat16
MESH = pl.DeviceIdType.MESH
V7X_VMEM_LIMIT = 56 * 1024 * 1024

D = 1024
HD = 64
EPS = 1e-6
CHUNK = 128
D_FF = 2816
ROPE_DIM = 16
ROPE_THETA = 500000.0
PATTERN_DILATIONS = (1, 4, 16)
BAND = 64
SMALL_ROWS = 280
ADAM_LR, ADAM_B1, ADAM_B2, ADAM_EPS, ADAM_WD, ADAM_STEP = 0.001, 0.9, 0.999, 1e-08, 0.01, 10

NN = (((1,), (0,)), ((), ()))
NT = (((1,), (1,)), ((), ()))
TN = (((0,), (0,)), ((), ()))


def _pcall(body, **kw):
    return pl.pallas_call(body, **kw)


def _cparams(sem=None):
    return pltpu.CompilerParams(dimension_semantics=sem, vmem_limit_bytes=V7X_VMEM_LIMIT)


def _dot(a, b, dims=NN):
    return lax.dot_general(a, b, dims, preferred_element_type=F32)


def _pick(n, cap):
    if n <= cap:
        return n
    best = 0
    for t in range(128, cap + 1, 128):
        if n % t == 0:
            best = t
    assert best, (n, cap)
    return best


def _iota(shape, dim):
    return lax.broadcasted_iota(jnp.int32, shape, dim)


def _parts(x, n):
    out, r = [], x
    for _ in range(n):
        h = r.astype(BF16)
        out.append(h)
        r = r - h.astype(F32)
    return out


def _sigmoid(x):
    return 1.0 / (1.0 + jnp.exp(-x))


def _silu(x):
    return x * _sigmoid(x)


def _dsilu(x):
    s = _sigmoid(x)
    return s * (1.0 + x * (1.0 - s))


def matmul(a, b, mode, name, out_dtype=F32, after=None, b_k_off=0):
    if mode == "nn":
        (m, k), (_, n) = a.shape, b.shape
    elif mode == "nt":
        (m, k), (n, _) = a.shape, b.shape
    else:
        (k, m), (_, n) = a.shape, b.shape
    tm, tn, tk = _pick(m, 1408), _pick(n, 1408), _pick(k, 1408)
    nk = k // tk
    dims = {"nn": NN, "nt": NT, "tn": TN}[mode]
    a_spec = pl.BlockSpec((tk, tm), lambda i, j, kk: (kk, i)) if mode == "tn" else pl.BlockSpec((tm, tk), lambda i, j, kk: (i, kk))
    b_spec = pl.BlockSpec((tn, tk), lambda i, j, kk: (j, kk + b_k_off)) if mode == "nt" else pl.BlockSpec((tk, tn), lambda i, j, kk: (kk, j))
    extra = [] if after is None else [after]

    def body(a_ref, b_ref, *rest):
        o_ref, acc = rest[len(extra)], rest[len(extra) + 1:]
        part = _dot(a_ref[...].astype(BF16), b_ref[...].astype(BF16), dims)
        if nk == 1:
            o_ref[...] = part.astype(o_ref.dtype)
            return
        acc_ref, kk = acc[0], pl.program_id(2)

        @pl.when(kk == 0)
        def _():
            acc_ref[...] = part

        @pl.when((kk > 0) & (kk < nk - 1))
        def _():
            acc_ref[...] += part

        @pl.when(kk == nk - 1)
        def _():
            o_ref[...] = (acc_ref[...] + part).astype(o_ref.dtype)

    return _pcall(
        body, name=name, grid=(m // tm, n // tn, nk), in_specs=[a_spec, b_spec] + [pl.BlockSpec(memory_space=pl.ANY)] * len(extra),
        out_specs=pl.BlockSpec((tm, tn), lambda i, j, kk: (i, j)),
        out_shape=jax.ShapeDtypeStruct((m, n), out_dtype),
        scratch_shapes=[pltpu.VMEM((tm, tn), F32)] if nk > 1 else [],
        compiler_params=_cparams(("parallel", "parallel", "arbitrary")),
    )(a, b, *extra)


def ew(fn, name, rows, tm, ncol, ins, outs, accs=()):
    nrow = rows // tm
    r8 = tm // 8
    in_specs, arrays = [], []
    for ent in ins:
        arr, kind, w, off = ent[:4]
        roff = ent[4] if len(ent) > 4 else 0
        if kind == "row":
            spec = pl.BlockSpec((tm, w), lambda j, i, off=off, roff=roff: (i + roff, j + off))
        elif kind == "const":
            spec = pl.BlockSpec((arr.shape[0], w), lambda j, i, off=off: (0, j + off))
        elif kind == "prev":
            spec = pl.BlockSpec((8, w), lambda j, i, off=off: (jnp.maximum(i * r8 - 1, 0), j + off))
        else:
            spec = pl.BlockSpec((8, w), lambda j, i, off=off: (jnp.minimum((i + 1) * r8, rows // 8 - 1), j + off))
        in_specs.append(spec)
        arrays.append(arr)
    out_specs = [pl.BlockSpec((tm, w), lambda j, i: (i, j)) for (_, _, w) in outs]
    out_shape = [jax.ShapeDtypeStruct((rows, c), dt) for (c, dt, _) in outs]
    out_specs += [pl.BlockSpec((1, w), lambda j, i: (0, j)) for (_, w) in accs]
    out_shape += [jax.ShapeDtypeStruct((1, c), F32) for (c, _) in accs]
    nin, nout = len(ins), len(outs)

    def body(*refs):
        i = pl.program_id(1)
        res = fn(i, nrow, *[r[...] for r in refs[:nin]])
        if not isinstance(res, (tuple, list)):
            res = (res,)
        for r, v in zip(refs[nin:nin + nout], res[:nout]):
            r[...] = v.astype(r.dtype)
        if accs:
            acc_refs = refs[nin + nout:]

            @pl.when(i == 0)
            def _():
                for r in acc_refs:
                    r[...] = jnp.zeros_like(r)

            for r, v in zip(acc_refs, res[nout:]):
                r[...] += v

    res = _pcall(
        body, name=name, grid=(ncol, nrow), in_specs=in_specs, out_specs=out_specs, out_shape=out_shape,
        compiler_params=_cparams(("parallel", "arbitrary")),
    )(*arrays)
    return res


def _shift_down(x, prev8, i):
    first = jnp.where(i == 0, 0.0, prev8[7:8, :])
    return jnp.where(_iota(x.shape, 0) == 0, first, pltpu.roll(x, 1, 0))


def _shift_up(x, next8, i, nrow):
    last = jnp.where(i == nrow - 1, 0.0, next8[0:1, :])
    return jnp.where(_iota(x.shape, 0) == x.shape[0] - 1, last, pltpu.roll(x, x.shape[0] - 1, 0))


def _colsum(x):
    return jnp.sum(x, axis=0, keepdims=True)


def _extend(x, prev8, next8, i, nrow):
    return jnp.concatenate([jnp.where(i == 0, 0.0, prev8), x, jnp.where(i == nrow - 1, 0.0, next8)], axis=0)


def _taps(xe):
    return pltpu.roll(xe, 1, 0), xe, pltpu.roll(xe, xe.shape[0] - 1, 0)


def _mid(xe):
    return xe[8:xe.shape[0] - 8]


def _conv3(w, b, taps):
    return w[0:1] * taps[0] + w[1:2] * taps[1] + w[2:3] * taps[2] + b


def _conv3_t(w, d_ext):
    t = _taps(d_ext)
    return _mid(w[0:1] * t[2] + w[1:2] * t[1] + w[2:3] * t[0])


def ffn_conv_bwd_fn(i, n, g, gp, gn, u, up_, un, wg, wu, bg, bu, da, dap, dan):
    gt, ut = _taps(_extend(g, gp, gn, i, n)), _taps(_extend(u, up_, un, i, n))
    dae = _extend(da, dap, dan, i, n)
    gate, upv = _conv3(wg, bg, gt), _conv3(wu, bu, ut)
    dg, du = dae * upv * _dsilu(gate), dae * _silu(gate)
    dgm, dum = _mid(dg), _mid(du)
    sums = [_colsum(dgm * _mid(t)) for t in gt] + [_colsum(dum * _mid(t)) for t in ut] + [_colsum(dgm), _colsum(dum)]
    return (_conv3_t(wg, dg), _conv3_t(wu, du)) + tuple(sums)


def silu_conv_bwd_fn(i, n, xv, xp, xn, w, b, da, dap, dan):
    xt = _taps(_extend(xv, xp, xn, i, n))
    du = _extend(da, dap, dan, i, n) * _dsilu(_conv3(w, b, xt))
    dum = _mid(du)
    return (_conv3_t(w, du),) + tuple(_colsum(dum * _mid(t)) for t in xt) + (_colsum(dum),)


def _rms_fwd(x, w):
    r = lax.rsqrt(jnp.mean(x * x, axis=-1, keepdims=True) + EPS)
    return x * r * w


def _rms_bwd(dy, x, w):
    r = lax.rsqrt(jnp.mean(x * x, axis=-1, keepdims=True) + EPS)
    xh = x * r
    dxh = dy * w
    dx = r * (dxh - xh * jnp.mean(dxh * xh, axis=-1, keepdims=True))
    return dx, _colsum(dy * xh)


def _rope_tables(s):
    half = ROPE_DIM // 2
    inv_freq = jnp.power(ROPE_THETA, -jnp.arange(half, dtype=F32) * 2.0 / ROPE_DIM)
    ang = jnp.arange(s, dtype=F32)[:, None] * inv_freq[None, :]
    cos, sin = jnp.cos(ang), jnp.sin(ang)
    one, zero = jnp.ones((s, HD - ROPE_DIM), F32), jnp.zeros((s, HD - ROPE_DIM), F32)
    z8 = jnp.zeros((s, half), F32)
    c = jnp.concatenate([cos, cos, one], axis=1)
    sa = jnp.concatenate([-sin, z8, zero], axis=1)
    sb = jnp.concatenate([z8, sin, zero], axis=1)
    return [jnp.tile(t, (1, 2)) for t in (c, sa, sb)]


ATTN_CHUNK = 1024


def _attn_plan(s):
    plan = []
    for d in PATTERN_DILATIONS:
        per_res = ATTN_CHUNK // d
        tq = min(128, per_res)
        plan.append((d, tq, min(s // d, tq + 2 * BAND), per_res // tq, s // d))
    return plan


def _rows(start, size, d):
    return pl.ds(start, size) if d == 1 else pl.ds(start, size, stride=d)


def _for_tiles(chunk, pat, fn):
    d, tq, win, nblk, seq_len = pat
    for b in range(nblk):
        t0 = chunk * (ATTN_CHUNK // d) + b * tq
        kloc = jnp.clip(t0 - BAND, 0, seq_len - win)
        valid = jnp.abs(kloc + _iota((tq, win), 1) - (t0 + _iota((tq, win), 0))) <= BAND
        valid = jnp.concatenate([valid, valid], axis=0)
        if d == 1:
            fn(b * tq, pl.multiple_of(kloc, BAND), valid)
        else:
            def step(r, carry, qoff=d * b * tq, koff=d * kloc, valid=valid):
                fn(qoff + r, koff + r, valid)
                return carry
            lax.fori_loop(0, d, step, 0, unroll=min(d, 8))


def _stack_heads(x, head0):
    zero = jnp.zeros_like(x)
    return jnp.concatenate([jnp.where(head0, x, zero), jnp.where(head0, zero, x)], axis=0)


def _rope_pair(x, c, sa, sb):
    n = x.shape[1]
    return x * c + pltpu.roll(x, n - 8, 1) * sa + pltpu.roll(x, 8, 1) * sb


def _rope_pair_t(dy, c, sa, sb):
    n = dy.shape[1]
    return dy * c + pltpu.roll(dy * sa, 8, 1) + pltpu.roll(dy * sb, n - 8, 1)


def _attn_specs(s):
    whole = lambda off: pl.BlockSpec((s, 128), lambda p, c: (0, off + p))
    table = pl.BlockSpec((s, 128), lambda p, c: (0, 0))
    chunk = pl.BlockSpec((ATTN_CHUNK, 128), lambda p, c: (c, p))
    return whole, table, chunk


def attn_fwd_all(proj, tabs, name):
    s = proj.shape[0]
    plan = _attn_plan(s)
    whole, table, chunk_spec = _attn_specs(s)

    def body(q_ref, k_ref, v_ref, c_ref, sa_ref, sb_ref, o_ref, lse_ref, qs, ks, acc_s, m_s, l_s):
        chunk = pl.program_id(1)

        @pl.when(chunk == 0)
        def _():
            qs[...] = _rope_pair(q_ref[...], c_ref[...], sa_ref[...], sb_ref[...]) * (HD ** -0.5)
            ks[...] = _rope_pair(k_ref[...], c_ref[...], sa_ref[...], sb_ref[...])

        base = pl.multiple_of(chunk * ATTN_CHUNK, ATTN_CHUNK)
        for pi, pat in enumerate(plan):
            d, tq, win = pat[:3]
            head0 = _iota((tq, 128), 1) < HD

            def tile(qrow, krow, valid, pi=pi, d=d, tq=tq, win=win, head0=head0):
                qv = qs[_rows(base + qrow, tq, d), :].astype(BF16)
                kw = ks[_rows(krow, win, d), :].astype(BF16)
                vw = v_ref[_rows(krow, win, d), :].astype(BF16)
                v_ones = jnp.concatenate([vw, jnp.ones_like(vw)], axis=1)
                sc = jnp.where(valid, _dot(_stack_heads(qv, head0), kw, NT), -1e30)
                mh = jnp.max(sc, axis=1, keepdims=True)
                pv = _dot(jnp.exp(sc - mh).astype(BF16), v_ones)
                acc_s[pi, _rows(qrow, tq, d), :] = jnp.where(head0, pv[:tq, :128], pv[tq:, :128])
                m_s[pi, _rows(qrow, tq, d), :] = jnp.where(head0, mh[:tq], mh[tq:])
                l_s[pi, _rows(qrow, tq, d), :] = jnp.where(head0, pv[:tq, 128:], pv[tq:, 128:])

            _for_tiles(chunk, pat, tile)
        m_all = jnp.maximum(jnp.maximum(m_s[0], m_s[1]), m_s[2])
        e = [jnp.exp(m_s[k] - m_all) for k in range(3)]
        den = e[0] * l_s[0] + e[1] * l_s[1] + e[2] * l_s[2]
        o_ref[...] = (e[0] * acc_s[0] + e[1] * acc_s[1] + e[2] * acc_s[2]) / den
        lse_ref[...] = m_all + jnp.log(den)

    stat = pltpu.VMEM((3, ATTN_CHUNK, 128), F32)
    return _pcall(
        body, name=name, grid=(D // 128, s // ATTN_CHUNK),
        in_specs=[whole(0), whole(8), whole(16), table, table, table], out_specs=[chunk_spec, chunk_spec],
        out_shape=[jax.ShapeDtypeStruct((s, D), F32)] * 2,
        scratch_shapes=[pltpu.VMEM((s, 128), F32), pltpu.VMEM((s, 128), F32), stat, stat, stat],
        compiler_params=_cparams(("parallel", "arbitrary")),
    )(proj, proj, proj, *tabs)


def attn_bwd_all(proj, tabs, dmix, o, lse, name):
    s = proj.shape[0]
    plan = _attn_plan(s)
    whole, table, chunk_spec = _attn_specs(s)
    nchunk = s // ATTN_CHUNK

    def body(q_ref, k_ref, v_ref, c_ref, sa_ref, sb_ref, do_ref, o_ref, lse_ref, dq_ref, dk_ref, dv_ref, qs, ks, aug0_s, aug1_s):
        chunk = pl.program_id(1)

        @pl.when(chunk == 0)
        def _():
            qs[...] = _rope_pair(q_ref[...], c_ref[...], sa_ref[...], sb_ref[...]) * (HD ** -0.5)
            ks[...] = _rope_pair(k_ref[...], c_ref[...], sa_ref[...], sb_ref[...])
            dk_ref[...] = jnp.zeros_like(dk_ref)
            dv_ref[...] = jnp.zeros_like(dv_ref)

        base = pl.multiple_of(chunk * ATTN_CHUNK, ATTN_CHUNK)
        prod = do_ref[...] * o_ref[...]
        first = _iota(prod.shape, 1) < HD
        delta = jnp.where(first, jnp.sum(jnp.where(first, prod, 0.0), axis=1, keepdims=True),
                          jnp.sum(jnp.where(first, 0.0, prod), axis=1, keepdims=True))
        lane = _iota(prod.shape, 1)

        def as_lanes(lse_h, delta_h):
            a, b = [u.astype(F32) for u in _parts(lse_h, 3)], [u.astype(F32) for u in _parts(delta_h, 3)]
            out = jnp.zeros_like(lse_h)
            for k, u in enumerate(a + b):
                out = jnp.where(lane == k, u, out)
            return out

        lsev = lse_ref[...]
        aug0_s[...] = as_lanes(lsev, delta)
        aug1_s[...] = as_lanes(pltpu.roll(lsev, HD, 1), pltpu.roll(delta, HD, 1))
        for pi, pat in enumerate(plan):
            d, tq, win = pat[:3]
            head0 = _iota((tq, 128), 1) < HD

            def tile(qrow, krow, valid, pi=pi, d=d, tq=tq, win=win, head0=head0):
                qv = qs[_rows(base + qrow, tq, d), :].astype(BF16)
                kw = ks[_rows(krow, win, d), :].astype(BF16)
                vw = v_ref[_rows(krow, win, d), :].astype(BF16)
                dob = do_ref[_rows(qrow, tq, d), :].astype(BF16)
                aug = jnp.concatenate([aug0_s[_rows(qrow, tq, d), :], aug1_s[_rows(qrow, tq, d), :]], axis=0).astype(BF16)
                klane = _iota((win, 128), 1)
                minus_lse = jnp.where(klane < 3, -1.0, 0.0).astype(BF16)
                minus_delta = jnp.where((klane >= 3) & (klane < 6), -1.0, 0.0).astype(BF16)
                q2, do2 = _stack_heads(qv, head0), _stack_heads(dob, head0)
                s_lse = _dot(jnp.concatenate([q2, aug], axis=1), jnp.concatenate([kw, minus_lse], axis=1), NT)
                dp_delta = _dot(jnp.concatenate([do2, aug], axis=1), jnp.concatenate([vw, minus_delta], axis=1), NT)
                p = jnp.where(valid, jnp.exp(s_lse), 0.0)
                ds = (p * dp_delta).astype(BF16)
                dq2 = _dot(ds, kw)
                dk = _dot(ds, q2, TN)
                dv = _dot(p.astype(BF16), do2, TN)
                dqv = jnp.where(head0, dq2[:tq], dq2[tq:])
                if pi == 0:
                    dq_ref[_rows(qrow, tq, d), :] = dqv
                else:
                    dq_ref[_rows(qrow, tq, d), :] += dqv
                dk_ref[_rows(krow, win, d), :] += dk
                dv_ref[_rows(krow, win, d), :] += dv

            _for_tiles(chunk, pat, tile)
        tab = [t[pl.ds(base, ATTN_CHUNK), :] for t in (c_ref, sa_ref, sb_ref)]
        dq_ref[...] = _rope_pair_t(dq_ref[...] * (HD ** -0.5), *tab)

        @pl.when(chunk == nchunk - 1)
        def _():
            dk_ref[...] = _rope_pair_t(dk_ref[...], c_ref[...], sa_ref[...], sb_ref[...])

    return _pcall(
        body, name=name, grid=(D // 128, nchunk),
        in_specs=[whole(0), whole(8), whole(16), table, table, table, chunk_spec, chunk_spec, chunk_spec],
        out_specs=[chunk_spec, whole(0), whole(0)], out_shape=[jax.ShapeDtypeStruct((s, D), F32)] * 3,
        scratch_shapes=[pltpu.VMEM((s, 128), F32), pltpu.VMEM((s, 128), F32)] + [pltpu.VMEM((ATTN_CHUNK, 128), F32)] * 2,
        compiler_params=_cparams(("parallel", "arbitrary")),
    )(proj, proj, proj, *tabs, dmix, o, lse)


def _ssd_common(x_ref, b_ref, c_ref, dt_ref, dtt_ref, a_ref, ar_ref, rev):
    ii, jj = _iota((CHUNK, CHUNK), 0), _iota((CHUNK, CHUNK), 1)
    low = jj >= ii if rev else jj <= ii
    x, dtx = x_ref[...], dt_ref[...]
    bm, cm = b_ref[...].astype(BF16), c_ref[...].astype(BF16)
    a = dtx * a_ref[...]
    arow = dtt_ref[0] * ar_ref[0]
    lowb = low.astype(BF16)
    cs = _dot(lowb, jnp.concatenate(_parts(a, 3), axis=1))
    cs = cs[:, :128] + cs[:, 128:256] + cs[:, 256:]
    csr = _dot(jnp.concatenate([p.astype(F32) for p in _parts(arow, 3)], axis=0).astype(BF16), lowb, NT)
    csr = csr[0:8] + csr[8:16] + csr[16:24]
    last = 0 if rev else CHUNK - 1
    tot = cs[last:last + 1, :]
    xdt = x * dtx
    cb = _dot(cm, bm, NT)
    lmats = [jnp.exp(jnp.where(low, cs[:, HD * h:HD * h + 1] - csr[h:h + 1, :], -1e30)) for h in range(2)]
    return dict(x=x, dtx=dtx, bm=bm, cm=cm, a=a, cs=cs, tot=tot, xdt=xdt, cb=cb, lmats=lmats, low=low, last=last)


SSD_SUB = 4


def _ssd_specs(s, rev_order):
    nblk, rows = s // (SSD_SUB * CHUNK), SSD_SUB * CHUNK
    ci = (lambda c: nblk - 1 - c) if rev_order else (lambda c: c)
    tile = lambda off, div: pl.BlockSpec((rows, 128), lambda p, c: (ci(c), off + p // div))
    common = [tile(0, 1), tile(8, 2), tile(12, 2), tile(0, 1),
              pl.BlockSpec((1, 8, rows), lambda p, c: (p, 0, ci(c))),
              pl.BlockSpec((1, 128), lambda p, c: (0, p)),
              pl.BlockSpec((1, 8, 128), lambda p, c: (p, 0, 0))]
    hs = pl.BlockSpec((1, SSD_SUB, CHUNK, 128), lambda p, c: (p, ci(c), 0, 0))
    return nblk, common, tile(0, 1), hs


def _chunk_rows(ref, j):
    return ref.at[pl.ds(j * CHUNK, CHUNK), :]


def _ssd_chunk(refs, j):
    return [_chunk_rows(r, j) for r in refs[:4]] + [refs[4].at[:, :, pl.ds(j * CHUNK, CHUNK)], refs[5], refs[6]]


def _ssd_args(xbc, t):
    return [xbc, xbc, xbc, t["dt_exp"], t["dtt"], t["a_exp"], t["a_rows"]]


def ssd_fwd(xbc, dirs, name):
    s = xbc.shape[0]
    nd = len(dirs)
    specs = [_ssd_specs(s, t["rev"]) for t in dirs]
    nck = specs[0][0]

    def one(rev, x_ref, b_ref, c_ref, dt_ref, dtt_ref, a_ref, ar_ref, y_ref, hs_ref, h_scr):
        v = _ssd_common(x_ref, b_ref, c_ref, dt_ref, dtt_ref, a_ref, ar_ref, rev)
        xdtb = v["xdt"].astype(BF16)
        yd = _dot(jnp.concatenate([v["cb"] * v["lmats"][h] for h in range(2)], axis=0).astype(BF16), xdtb)
        h_in = h_scr[...]
        hs_ref[0, 0] = h_in
        y_off = _dot(v["cm"], h_in.astype(BF16)) * jnp.exp(v["cs"])
        y_ref[...] = jnp.where(_iota((CHUNK, 128), 1) < HD, yd[:CHUNK], yd[CHUNK:]) + y_off
        decay = jnp.exp(v["tot"] - v["cs"])
        h_scr[...] = jnp.exp(v["tot"]) * h_in + _dot(v["bm"], (v["xdt"] * decay).astype(BF16), TN)

    def body(*refs):
        @pl.when(pl.program_id(1) == 0)
        def _():
            for k in range(nd):
                refs[9 * nd + k][...] = jnp.zeros((CHUNK, 128), F32)

        for k, t in enumerate(dirs):
            y_ref, hs_ref = refs[7 * nd + 2 * k:7 * nd + 2 * k + 2]
            for j in (range(SSD_SUB)[::-1] if t["rev"] else range(SSD_SUB)):
                one(t["rev"], *_ssd_chunk(refs[7 * k:7 * k + 7], j), _chunk_rows(y_ref, j), hs_ref.at[:, pl.ds(j, 1)], refs[9 * nd + k])

    res = _pcall(
        body, name=name, grid=(8, nck), in_specs=[sp for t in specs for sp in t[1]],
        out_specs=[sp for t in specs for sp in (t[2], t[3])],
        out_shape=[jax.ShapeDtypeStruct((s, D), F32), jax.ShapeDtypeStruct((8, s // CHUNK, CHUNK, 128), F32)] * nd,
        scratch_shapes=[pltpu.VMEM((CHUNK, 128), F32)] * nd, compiler_params=_cparams(("parallel", "arbitrary")),
    )(*[a for t in dirs for a in _ssd_args(xbc, t)])
    return [(res[2 * k], res[2 * k + 1]) for k in range(nd)]


def ssd_bwd(xbc, dirs, dy, name):
    s = xbc.shape[0]
    nd = len(dirs)
    specs = [_ssd_specs(s, not t["rev"]) for t in dirs]
    nck = specs[0][0]

    def one(rev, x_ref, b_ref, c_ref, dt_ref, dtt_ref, a_ref, ar_ref, hs_ref, dy_ref,
            dx_ref, ddt_ref, db_ref, dc_ref, dal_ref, dh_scr):
        v = _ssd_common(x_ref, b_ref, c_ref, dt_ref, dtt_ref, a_ref, ar_ref, rev)
        bm, cm, cs, tot, xdt = v["bm"], v["cm"], v["cs"], v["tot"], v["xdt"]
        h_in, dh = hs_ref[0, 0], dh_scr[...]
        dyv = dy_ref[...]
        dyb = dyv.astype(BF16)
        etot, decay, ecs = jnp.exp(tot), jnp.exp(tot - cs), jnp.exp(cs)
        xdtb = xdt.astype(BF16)
        xdec = xdt * decay
        dch = (dyv * ecs).astype(BF16)
        hb, dhb = h_in.astype(BF16), dh.astype(BF16)
        y_off = _dot(cm, hb) * ecs
        dc = _dot(dch, hb, NT)
        dh_y = _dot(cm, dch, TN)
        dxdec = _dot(bm, dhb)
        db = _dot(xdec.astype(BF16), dhb, NT)
        state_term = xdec * dxdec
        dtot = _colsum(dh * h_in) * etot + _colsum(state_term)
        head0 = _iota((CHUNK, 128), 1) < HD
        ii, jj = _iota((CHUNK, CHUNK), 0), _iota((CHUNK, CHUNK), 1)
        low_t = jj <= ii if rev else jj >= ii
        not_low_t = (~low_t).astype(BF16)
        g = _dot(_stack_heads(dyb, head0), xdtb, NT)
        gl = [g[:CHUNK] * v["lmats"][0], g[CHUNK:] * v["lmats"][1]]
        dcb = gl[0] + gl[1]
        dxd = _dot(jnp.concatenate([v["cb"] * v["lmats"][h] for h in range(2)], axis=1).astype(BF16), dyb, TN)
        dxd = jnp.where(head0, dxd[:CHUNK], dxd[CHUNK:])
        w = _dot(not_low_t, jnp.concatenate([gl[h] * v["cb"] for h in range(2)], axis=0).astype(BF16), NT)
        da_l = [jnp.sum(jnp.where(low_t, w[:, CHUNK * h:CHUNK * h + CHUNK], 0.0), axis=1, keepdims=True) for h in range(2)]
        dxdt = dxdec * decay + dxd
        dcbb = dcb.astype(BF16)
        dc_ref[...] = dc + _dot(dcbb, bm)
        db_ref[...] = db + _dot(dcbb, cm, TN)
        dcs = dyv * y_off - state_term + jnp.where(_iota((CHUNK, 128), 0) == v["last"], dtot, 0.0)
        lowb = v["low"].astype(BF16)
        da = _dot(lowb, jnp.concatenate(_parts(dcs, 2), axis=1), TN)
        da = da[:, :128] + da[:, 128:]
        seg = ((ii < HD) == (jj < HD)).astype(BF16)
        sums = _dot(jnp.concatenate(_parts(da, 2) + _parts(dxdt * v["x"], 2), axis=0), seg)
        da = sums[:CHUNK] + sums[CHUNK:2 * CHUNK] + jnp.where(head0, da_l[0], da_l[1])
        ddt_x = sums[2 * CHUNK:3 * CHUNK] + sums[3 * CHUNK:]
        dx_ref[...] = dxdt * v["dtx"]
        ddt_ref[...] = ddt_x + da * a_ref[...]
        dal_ref[0] += _colsum(da * v["a"])
        dh_scr[...] = etot * dh + dh_y

    def body(*refs):
        @pl.when(pl.program_id(1) == 0)
        def _():
            for k in range(nd):
                refs[14 * nd + k][...] = jnp.zeros((CHUNK, 128), F32)
                refs[9 * nd + 5 * k + 4][...] = jnp.zeros((1, 8, 128), F32)

        for k, t in enumerate(dirs):
            ins, outs = refs[9 * k:9 * k + 9], refs[9 * nd + 5 * k:9 * nd + 5 * k + 5]
            for j in (range(SSD_SUB) if t["rev"] else range(SSD_SUB)[::-1]):
                one(t["rev"], *_ssd_chunk(ins[:7], j), ins[7].at[:, pl.ds(j, 1)], _chunk_rows(ins[8], j),
                    *[_chunk_rows(r, j) for r in outs[:4]], outs[4], refs[14 * nd + k])

    acc_spec = pl.BlockSpec((1, 8, 128), lambda p, c: (p, 0, 0))
    res = _pcall(
        body, name=name, grid=(8, nck), in_specs=[sp for t in specs for sp in t[1] + [t[3], t[2]]],
        out_specs=[sp for t in specs for sp in [t[2]] * 4 + [acc_spec]],
        out_shape=([jax.ShapeDtypeStruct((s, D), F32)] * 4 + [jax.ShapeDtypeStruct((8, 8, 128), F32)]) * nd,
        scratch_shapes=[pltpu.VMEM((CHUNK, 128), F32)] * nd, compiler_params=_cparams(("parallel", "arbitrary")),
    )(*[a for t in dirs for a in _ssd_args(xbc, t) + [t["hs"], dy]])
    return [res[5 * k:5 * k + 5] for k in range(nd)]


def _group_norm_stats(g):
    r = [lax.rsqrt(jnp.mean(g[:, 256 * k:256 * k + 256] ** 2, axis=-1, keepdims=True) + EPS) for k in range(4)]
    grp = _iota(g.shape, 1) // 256
    return jnp.where(grp == 0, r[0], jnp.where(grp == 1, r[1], jnp.where(grp == 2, r[2], r[3])))


def _group_mean(t):
    m = [jnp.mean(t[:, 256 * k:256 * k + 256], axis=-1, keepdims=True) for k in range(4)]
    grp = _iota(t.shape, 1) // 256
    return jnp.where(grp == 0, m[0], jnp.where(grp == 1, m[1], jnp.where(grp == 2, m[2], m[3])))


def _mesh_pos():
    return lax.axis_index("x"), lax.axis_index("y"), lax.axis_index("c")


HBM = pl.BlockSpec(memory_space=pltpu.HBM)
SEM = pl.BlockSpec(memory_space=pltpu.SEMAPHORE)
EFFECT = pltpu.SideEffectType.DATAFLOW_SIDE_EFFECTING


def _hbm(t):
    return pltpu.with_memory_space_constraint(t, pltpu.HBM)


def _other_chips(x, y):
    return [(1 - x, y), (x, 1 - y), (1 - x, 1 - y)]


def _peer(x, y, c, m):
    return x ^ (m >> 2), y ^ ((m >> 1) & 1), c ^ (m & 1)


def gather_start(srcs_a, srcs_b):
    srcs = [_hbm(t) for t in list(srcs_a) + list(srcs_b)]
    n, na = len(srcs), len(srcs_a)
    lands = [_hbm(lax.empty((4,) + t.shape, t.dtype)) for t in srcs]

    def body(*refs):
        src, land = refs[:n], refs[n:2 * n]
        sems = refs[2 * n:2 * n + 4]
        x, y, c = _mesh_pos()
        for k in range(n):
            for j, (px, py) in enumerate(_other_chips(x, y)):
                send, recv, idx = (sems[0], sems[1], 3 * k + j) if k < na else (sems[2], sems[3], 3 * (k - na) + j)
                pltpu.make_async_remote_copy(src_ref=src[k], dst_ref=land[k].at[2 * x + y], send_sem=send.at[idx],
                                             recv_sem=recv.at[idx], device_id=(px, py, c), device_id_type=MESH).start()

    sem_a, sem_b = pltpu.SemaphoreType.DMA((3 * na,)), pltpu.SemaphoreType.DMA((3 * (n - na),))
    res = _pcall(
        body, name="gather_start", in_specs=[HBM] * (2 * n), out_specs=[SEM] * 4 + [HBM] * (2 * n),
        out_shape=[sem_a, sem_a, sem_b, sem_b] + [pltpu.HBM(t.shape, t.dtype) for t in srcs + lands],
        input_output_aliases={i: 4 + i for i in range(2 * n)},
        compiler_params=pltpu.CompilerParams(has_side_effects=EFFECT),
    )(*srcs, *lands)
    thru_src, thru_land = res[4:4 + n], res[4 + n:]
    return ((res[0], res[1], thru_src[:na], thru_land[:na]), (res[2], res[3], thru_src[na:], thru_land[na:]))


def gather_wait(group, name, after=None):
    send, recv, srcs, lands = group
    n = len(srcs)

    def body(*refs):
        src, land, send_ref, recv_ref = refs[:n], refs[n:2 * n], refs[2 * n], refs[2 * n + 1]
        x, y, c = _mesh_pos()
        for j, (px, py) in enumerate(_other_chips(x, y)):
            for k in range(n):
                cp = pltpu.make_async_remote_copy(src_ref=src[k], dst_ref=land[k].at[2 * px + py], send_sem=send_ref.at[3 * k + j],
                                                  recv_sem=recv_ref.at[3 * k + j], device_id=(px, py, c), device_id_type=MESH)
                cp.wait_send()
                cp.wait_recv()

    extra = [] if after is None else [after]
    res = _pcall(
        body, name=name, in_specs=[HBM] * (2 * n) + [SEM, SEM] + [pl.BlockSpec(memory_space=pl.ANY)] * len(extra),
        out_specs=[HBM] * (2 * n), out_shape=[pltpu.HBM(t.shape, t.dtype) for t in list(srcs) + list(lands)],
        input_output_aliases={i: i for i in range(2 * n)}, compiler_params=pltpu.CompilerParams(has_side_effects=EFFECT),
    )(*srcs, *lands, send, recv, *extra)
    return res[:n], res[n:]


def scatter_start(pieces, smalls, name):
    srcs = [_hbm(t) for t in list(pieces) + list(smalls)]
    n, npc = len(srcs), len(pieces)
    lands = [_hbm(lax.empty((8,) + (t.shape[2:] if k < npc else t.shape), t.dtype)) for k, t in enumerate(srcs)]

    def body(*refs):
        src, land, send, recv = refs[:n], refs[n:2 * n], refs[2 * n], refs[2 * n + 1]
        token = refs[-1]
        x, y, c = _mesh_pos()
        for m in range(1, 8):
            px, py, pc = _peer(x, y, c, m)
            for k in range(n):
                s_ref = src[k].at[2 * px + py, pc] if k < npc else src[k]
                d_ref = land[k].at[m] if k < npc else land[k].at[4 * x + 2 * y + c]
                pltpu.make_async_remote_copy(src_ref=s_ref, dst_ref=d_ref, send_sem=send.at[7 * k + m - 1], recv_sem=recv.at[7 * k + m - 1],
                                             device_id=(px, py, pc), device_id_type=MESH).start()
        token[...] = jnp.zeros_like(token)

    sem = pltpu.SemaphoreType.DMA((7 * n,))
    res = _pcall(
        body, name=name, in_specs=[HBM] * (2 * n),
        out_specs=[SEM, SEM] + [HBM] * (2 * n) + [pl.BlockSpec(memory_space=pltpu.VMEM)],
        out_shape=[sem, sem] + [pltpu.HBM(t.shape, t.dtype) for t in srcs + lands] + [jax.ShapeDtypeStruct((8, 128), F32)],
        input_output_aliases={i: 2 + i for i in range(2 * n)},
        compiler_params=pltpu.CompilerParams(has_side_effects=EFFECT),
    )(*srcs, *lands)
    return (res[0], res[1], res[2:2 + n], res[2 + n:2 + 2 * n], npc), res[-1]


def scatter_wait(group, name, after=None):
    send, recv, srcs, lands, npc = group
    n = len(srcs)

    def body(*refs):
        src, land, send_ref, recv_ref = refs[:n], refs[n:2 * n], refs[2 * n], refs[2 * n + 1]
        x, y, c = _mesh_pos()
        for m in range(1, 8):
            px, py, pc = _peer(x, y, c, m)
            for k in range(n):
                s_ref = src[k].at[0, 0] if k < npc else src[k]
                d_ref = land[k].at[m] if k < npc else land[k].at[4 * px + 2 * py + pc]
                cp = pltpu.make_async_remote_copy(src_ref=s_ref, dst_ref=d_ref, send_sem=send_ref.at[7 * k + m - 1],
                                                  recv_sem=recv_ref.at[7 * k + m - 1], device_id=(px, py, pc), device_id_type=MESH)
                cp.wait_send()
                cp.wait_recv()

    extra = [] if after is None else [after]
    res = _pcall(
        body, name=name, in_specs=[HBM] * (2 * n) + [SEM, SEM] + [pl.BlockSpec(memory_space=pl.ANY)] * len(extra),
        out_specs=[HBM] * (2 * n), out_shape=[pltpu.HBM(t.shape, t.dtype) for t in list(srcs) + list(lands)],
        input_output_aliases={i: i for i in range(2 * n)}, compiler_params=pltpu.CompilerParams(has_side_effects=EFFECT),
    )(*srcs, *lands, send, recv, *extra)
    return res[:n], res[n:]


def swap_halves(pieces, name):
    n = len(pieces)
    whole = pl.BlockSpec(memory_space=pltpu.VMEM)

    def body(*refs):
        p_refs, o_refs, send_sems, recv_sems, local_sems = refs[:n], refs[n:2 * n], refs[2 * n], refs[2 * n + 1], refs[2 * n + 2]
        x, y, c = _mesh_pos()
        local = [pltpu.make_async_copy(p_refs[k], o_refs[k].at[c], local_sems.at[k]) for k in range(n)]
        for cp in local:
            cp.start()

        def copy(k, slot):
            return pltpu.make_async_remote_copy(src_ref=p_refs[k], dst_ref=o_refs[k].at[slot], send_sem=send_sems.at[k],
                                                recv_sem=recv_sems.at[k], device_id=(x, y, 1 - c), device_id_type=MESH)

        for k in range(n):
            copy(k, c).start()
        for k in range(n):
            copy(k, 1 - c).wait_recv()
        for k in range(n):
            copy(k, c).wait_send()
        for cp in local:
            cp.wait()

    return _pcall(
        body, name=name, in_specs=[whole] * n, out_specs=[whole] * n,
        out_shape=[jax.ShapeDtypeStruct((2,) + t.shape, t.dtype) for t in pieces],
        scratch_shapes=[pltpu.SemaphoreType.DMA((n,)), pltpu.SemaphoreType.DMA((n,)), pltpu.SemaphoreType.DMA((n,))],
        compiler_params=_cparams(),
    )(*pieces)


def adamw(w, g, m, v, name):
    rows, cols = w.shape
    tm = rows
    for t in (256, 352, 128, 144, 64, 32, 16, 8):
        if rows % t == 0:
            tm = t
            break

    def fn(i, nrow, wv, gv, mv, vv):
        mn = ADAM_B1 * mv + (1.0 - ADAM_B1) * gv
        vn = ADAM_B2 * vv + (1.0 - ADAM_B2) * (gv * gv)
        m_hat = mn / (1.0 - ADAM_B1 ** ADAM_STEP)
        v_hat = vn / (1.0 - ADAM_B2 ** ADAM_STEP)
        delta = -ADAM_LR * (m_hat / (jnp.sqrt(v_hat) + ADAM_EPS) + ADAM_WD * wv)
        return delta, mn, vn

    return ew(fn, name, rows, tm, 1, [(t, "row", cols, 0) for t in (w, g, m, v)], [(cols, F32, cols)] * 3)


REST = ("w_out", "w_up", "w_down")
SMALL = ("norm1_w", "ssm_conv_w", "ssm_conv_b", "a_log_f", "a_log_b", "dt_bias_f", "dt_bias_b", "d_skip",
         "ssm_norm_w", "norm2_w", "ffn_conv_w", "ffn_conv_b", "final_norm_w")
WEIGHTS = ("norm1_w", "w_in", "ssm_conv_w", "ssm_conv_b", "a_log_f", "a_log_b", "dt_bias_f", "dt_bias_b", "d_skip",
           "ssm_norm_w", "w_out", "norm2_w", "w_up", "ffn_conv_w", "ffn_conv_b", "w_down", "final_norm_w")
INPUTS = ("x",) + WEIGHTS + ("loss_target",) + tuple("m_" + n for n in WEIGHTS) + tuple("v_" + n for n in WEIGHTS)


def _flat_rows(parts, width, rows):
    flat = jnp.concatenate([p.reshape(-1) for p in parts])
    return jnp.pad(flat, (0, rows * width - flat.shape[0])).reshape(rows, width)


def _split_flat(flat, shapes):
    out, pos = [], 0
    flat = flat.reshape(-1)
    for shp in shapes:
        n = int(np.prod(shp))
        out.append(flat[pos:pos + n].reshape(shp))
        pos += n
    return out


def _col_shards(t, nshard):
    r, c = t.shape
    return t.reshape(r, nshard, c // nshard).transpose(1, 0, 2)


def _row_shards(t, nshard):
    r, c = t.shape
    return t.reshape(nshard, r // nshard, c)


def kernel(x, norm1_w, w_in, ssm_conv_w, ssm_conv_b, a_log_f, a_log_b, dt_bias_f, dt_bias_b, d_skip, ssm_norm_w, w_out, norm2_w, w_up, ffn_conv_w, ffn_conv_b, w_down, final_norm_w, loss_target, m_norm1_w, m_w_in, m_ssm_conv_w, m_ssm_conv_b, m_a_log_f, m_a_log_b, m_dt_bias_f, m_dt_bias_b, m_d_skip, m_ssm_norm_w, m_w_out, m_norm2_w, m_w_up, m_ffn_conv_w, m_ffn_conv_b, m_w_down, m_final_norm_w, v_norm1_w, v_w_in, v_ssm_conv_w, v_ssm_conv_b, v_a_log_f, v_a_log_b, v_dt_bias_f, v_dt_bias_b, v_d_skip, v_ssm_norm_w, v_w_out, v_norm2_w, v_w_up, v_ffn_conv_w, v_ffn_conv_b, v_w_down, v_final_norm_w):
    p = dict(zip(INPUTS, (x, norm1_w, w_in, ssm_conv_w, ssm_conv_b, a_log_f, a_log_b, dt_bias_f, dt_bias_b, d_skip, ssm_norm_w, w_out, norm2_w, w_up, ffn_conv_w, ffn_conv_b, w_down, final_norm_w, loss_target, m_norm1_w, m_w_in, m_ssm_conv_w, m_ssm_conv_b, m_a_log_f, m_a_log_b, m_dt_bias_f, m_dt_bias_b, m_d_skip, m_ssm_norm_w, m_w_out, m_norm2_w, m_w_up, m_ffn_conv_w, m_ffn_conv_b, m_w_down, m_final_norm_w, v_norm1_w, v_w_in, v_ssm_conv_w, v_ssm_conv_b, v_a_log_f, v_a_log_b, v_dt_bias_f, v_dt_bias_b, v_d_skip, v_ssm_norm_w, v_w_out, v_norm2_w, v_w_up, v_ffn_conv_w, v_ffn_conv_b, v_w_down, v_final_norm_w)))
    x = p["x"][0]
    tgt = p["loss_target"][0]
    s = x.shape[0]
    chip = 2 * lax.axis_index("x") + lax.axis_index("y")

    own_slot = lambda land, mine, slot: lax.dynamic_update_slice_in_dim(land, mine[None], slot, axis=0)
    src_in = p["w_in"][0].astype(BF16)
    src_rest = [p[n][0].astype(BF16) for n in REST]
    small_w = _flat_rows([p["ssm_conv_w"][0], p["ffn_conv_w"][0]], 128, 48)
    gather_in, gather_rest = gather_start([src_in, small_w], src_rest)
    (src_in, small_w), (wg_in, sg) = gather_wait(gather_in, "gather_wait_in")
    w_in = own_slot(wg_in, src_in, chip).transpose(1, 0, 2).reshape(D, -1)
    sg = own_slot(sg, small_w, chip)
    n_in = w_in.shape[1]
    n_main = 6 * D
    w_main = w_in[:, :n_main]
    w_dt = jnp.pad(w_in[:, n_main:], ((0, 0), (0, 128 - (n_in - n_main))))
    sgf = sg.reshape(4, -1)
    n_sc, n_fc = p["ssm_conv_w"].shape[1], p["ffn_conv_w"].shape[1]
    ssm_cw = sgf[:, :n_sc * 3].reshape(-1, 3).T
    ffn_cw = sgf[:, n_sc * 3:(n_sc + n_fc) * 3].reshape(-1, 3).T
    ssm_cb, ffn_cb = p["ssm_conv_b"], p["ffn_conv_b"]
    n1w, n2w, snw, fnw = p["norm1_w"], p["norm2_w"], p["ssm_norm_w"], p["final_norm_w"].reshape(1, D)

    h1, = ew(lambda i, n, xv, w: _rms_fwd(xv, w), "rms1", s, 256, 1,
             [(x, "row", D, 0), (n1w, "const", D, 0)], [(D, BF16, D)])
    proj = matmul(h1, w_main, "nn", "in_proj")
    proj_dt = matmul(h1, w_dt, "nn", "in_proj_dt")
    tabs = _rope_tables(s)
    attn, lse = attn_fwd_all(proj, tabs, "attn_fwd")

    def conv_silu_fn(i, n, xv, xp, xn, w, b):
        return _silu(w[0:1] * _shift_down(xv, xp, i) + w[1:2] * xv + w[2:3] * _shift_up(xv, xn, i, n) + b)

    xbc_act, = ew(conv_silu_fn, "ssm_conv", s, 256, 2,
                  [(proj, "row", D, 4), (proj, "prev", D, 4), (proj, "next", D, 4),
                   (ssm_cw, "const", D, 0), (ssm_cb, "const", D, 0)], [(2 * D, F32, D)])
    dt_bias = jnp.pad(jnp.concatenate([p["dt_bias_f"], p["dt_bias_b"]], axis=1), ((0, 0), (0, 96)))

    lanes_of = np.arange(128)[:, None] == np.arange(D)[None, :] // HD
    spread = [jnp.asarray(np.roll(lanes_of, 16 * k, axis=0), BF16) for k in range(2)]

    def softplus_fn(i, n, r, b, ef, eb):
        t = r + b
        dtv = jnp.maximum(t, 0.0) + jnp.log(1.0 + jnp.exp(-jnp.abs(t)))
        parts = _parts(dtv, 3)
        return dtv, sum(_dot(q, ef) for q in parts), sum(_dot(q, eb) for q in parts)

    dt, dt_exp_f, dt_exp_b = ew(softplus_fn, "dt_softplus", s, 512, 1,
                                [(proj_dt, "row", 128, 0), (dt_bias, "const", 128, 0), (spread[0], "const", D, 0), (spread[1], "const", D, 0)],
                                [(128, F32, 128), (D, F32, D), (D, F32, D)])
    d_exp = jnp.repeat(p["d_skip"], HD, axis=1)
    ssd = []
    for k, (a_log, rev) in enumerate(((p["a_log_f"], False), (p["a_log_b"], True))):
        dt_k = dt[:, 16 * k:16 * k + 16]
        a_head = -jnp.exp(a_log)
        dt_exp = (dt_exp_f, dt_exp_b)[k]
        dtt = jnp.pad(dt_k.T.reshape(8, 2, s), ((0, 0), (0, 6), (0, 0)))
        a_exp = jnp.repeat(a_head, HD, axis=1)
        a_rows = jnp.broadcast_to(jnp.pad(a_head.reshape(8, 2), ((0, 0), (0, 6)))[:, :, None], (8, 8, 128))
        ssd.append(dict(dt_exp=dt_exp, dtt=dtt, a_exp=a_exp, a_rows=a_rows, rev=rev))
    for t, (y_k, hs_k) in zip(ssd, ssd_fwd(xbc_act, ssd, "ssd_fwd")):
        t["y"], t["hs"] = y_k, hs_k

    def gate_fn(i, n, yf, yb, xs, z, dsk, w):
        g = (yf + yb + dsk * xs) * _silu(z)
        return g * _group_norm_stats(g) * w

    ssm_out, = ew(gate_fn, "ssm_gate_norm", s, 256, 1,
                  [(ssd[0]["y"], "row", D, 0), (ssd[1]["y"], "row", D, 0), (xbc_act, "row", D, 0), (proj, "row", D, 3),
                   (d_exp, "const", D, 0), (snw, "const", D, 0)], [(D, F32, D)])
    mix = jnp.concatenate([attn, ssm_out], axis=1).astype(BF16)
    src_rest, wg_rest = gather_wait(gather_rest, "gather_wait_rest", after=mix)
    wg_rest = [own_slot(land, mine, chip) for land, mine in zip(wg_rest, src_rest)]
    w_out = wg_rest[0].reshape(-1, D)
    w_up = wg_rest[1].transpose(1, 0, 2).reshape(D, -1)
    w_down = wg_rest[2].reshape(-1, D)
    mix_w = matmul(mix, w_out, "nn", "out_proj")

    def res_rms_fn(i, n, xv, mw, w):
        x1v = xv + mw
        return x1v, _rms_fwd(x1v, w)

    x1, h2 = ew(res_rms_fn, "res_rms2", s, 256, 1, [(x, "row", D, 0), (mix_w, "row", D, 0), (n2w, "const", D, 0)],
                [(D, F32, D), (D, BF16, D)])
    hw = matmul(h2, w_up, "nn", "ffn_up")
    fw = D_FF // 2
    nfb = D_FF // fw
    ffn_conv_ins = [(hw, "row", fw, 0), (hw, "prev", fw, 0), (hw, "next", fw, 0),
                    (hw, "row", fw, nfb), (hw, "prev", fw, nfb), (hw, "next", fw, nfb),
                    (ffn_cw, "const", fw, 0), (ffn_cw, "const", fw, nfb), (ffn_cb, "const", fw, 0), (ffn_cb, "const", fw, nfb)]

    def ffn_conv(i, n, g, gp, gn, u, up_, un, wg_, wu, bg, bu):
        gs = (_shift_down(g, gp, i), g, _shift_up(g, gn, i, n))
        us = (_shift_down(u, up_, i), u, _shift_up(u, un, i, n))
        gate = wg_[0:1] * gs[0] + wg_[1:2] * gs[1] + wg_[2:3] * gs[2] + bg
        upv = wu[0:1] * us[0] + wu[1:2] * us[1] + wu[2:3] * us[2] + bu
        return gate, upv, gs, us

    def glu_fn(i, n, *blocks):
        gate, upv, _, _ = ffn_conv(i, n, *blocks)
        return _silu(gate) * upv

    act, = ew(glu_fn, "ffn_conv_glu", s, 256, nfb, ffn_conv_ins, [(D_FF, BF16, fw)])
    ffn = matmul(act, w_down, "nn", "ffn_down")

    def head_fn(i, n, x1v, fv, tv, w):
        x2 = x1v + fv
        r = lax.rsqrt(jnp.mean(x2 * x2, axis=-1, keepdims=True) + EPS)
        xh = x2 * r
        diff = xh * w - tv
        loss = 0.5 * jnp.sum(jnp.mean(diff * diff, axis=-1, keepdims=True), axis=0, keepdims=True)
        dout = diff * (1.0 / D)
        dxh = dout * w
        dx2 = r * (dxh - xh * jnp.mean(dxh * xh, axis=-1, keepdims=True))
        return dx2, jnp.broadcast_to(loss, (1, 128)), _colsum(dout * xh)

    dx2, loss_acc, g_fnw = ew(head_fn, "loss_head", s, 256, 1,
                              [(x1, "row", D, 0), (ffn, "row", D, 0), (tgt, "row", D, 0), (fnw, "const", D, 0)],
                              [(D, F32, D)], [(128, 128), (D, D)])
    loss = lax.psum(loss_acc[0, 0], ("x", "y", "c"))

    g_w_down = matmul(act, dx2, "tn", "d_w_down")
    dact = matmul(dx2, w_down, "nt", "d_act")

    res = ew(ffn_conv_bwd_fn, "ffn_conv_glu_bwd", s, 256, nfb,
             ffn_conv_ins + [(dact, "row", fw, 0), (dact, "prev", fw, 0), (dact, "next", fw, 0)],
             [(D_FF, F32, fw)] * 2, [(D_FF, fw)] * 8)
    dhw_g, dhw_u = res[0], res[1]
    g_ffn_cw = jnp.concatenate([jnp.concatenate(res[2:5], axis=0), jnp.concatenate(res[5:8], axis=0)], axis=1).T
    g_ffn_cb = jnp.concatenate([res[8], res[9]], axis=1)

    g_w_up = jnp.concatenate([matmul(h2, dhw_g, "tn", "d_w_up_gate"), matmul(h2, dhw_u, "tn", "d_w_up_up")], axis=1)
    dh2_a = matmul(dhw_g, w_up, "nt", "d_h2_gate")
    dh2_b = matmul(dhw_u, w_up, "nt", "d_h2_up", b_k_off=D_FF // _pick(D_FF, 1408))

    def res_rms_bwd_fn(i, n, dres, da, db, xin, w):
        dx, dw = _rms_bwd(da + db, xin, w)
        return dres + dx, dw

    dx1, g_n2w = ew(res_rms_bwd_fn, "res_rms2_bwd", s, 256, 1,
                    [(dx2, "row", D, 0), (dh2_a, "row", D, 0), (dh2_b, "row", D, 0), (x1, "row", D, 0), (n2w, "const", D, 0)],
                    [(D, F32, D)], [(D, D)])

    g_w_out = matmul(mix, dx1, "tn", "d_w_out")
    to_pieces = lambda t: t.astype(BF16).reshape(4, 2, t.shape[1] // 2, t.shape[2])
    shards_rest = [_row_shards(g_w_out, 4), _col_shards(g_w_up, 4), _row_shards(g_w_down, 4)]
    scatter_rest, token = scatter_start([to_pieces(t) for t in shards_rest], [], "scatter_start_rest")
    dmix = matmul(dx1, w_out, "nt", "d_mix", after=token)
    ii, jj = np.arange(D)[:, None] // HD, np.arange(D)[None, :] // HD
    seg = jnp.asarray(ii == jj, BF16)

    def gate_bwd_fn(i, n, dout, yf, yb, xs, z, dsk, w, segm):
        yt = yf + yb + dsk * xs
        sz = _silu(z)
        g = yt * sz
        r = _group_norm_stats(g)
        gh = g * r
        dn = dout * w
        dg = r * (dn - gh * _group_mean(dn * gh))
        dy = dg * sz
        dsk_lane = jnp.broadcast_to(_colsum(dy * xs), (8, D))
        return dy, dg * yt * _dsilu(z), _colsum(dout * gh), sum(_dot(q, segm) for q in _parts(dsk_lane, 2))[0:1]

    dy, dz, g_snw, g_dskip_l = ew(
        gate_bwd_fn, "ssm_gate_norm_bwd", s, 256, 1,
        [(dmix, "row", D, 1), (ssd[0]["y"], "row", D, 0), (ssd[1]["y"], "row", D, 0), (xbc_act, "row", D, 0),
         (proj, "row", D, 3), (d_exp, "const", D, 0), (snw, "const", D, 0), (seg, "const", D, 0)],
        [(D, F32, D)] * 2, [(D, D)] * 2)
    sb = ssd_bwd(xbc_act, ssd, dy, "ssd_bwd")

    def dxbc_act_fn(i, n, dxf, dxb, dyv, dsk, dbf, dbb, dcf, dcb_):
        db, dc = dbf + dbb, dcf + dcb_
        db = [db[:, 256 * g:256 * g + 128] + db[:, 256 * g + 128:256 * g + 256] for g in range(4)]
        dc = [dc[:, 256 * g:256 * g + 128] + dc[:, 256 * g + 128:256 * g + 256] for g in range(4)]
        return jnp.concatenate([dxf + dxb + dyv * dsk] + db + dc, axis=1)

    dxbc_act, = ew(dxbc_act_fn, "d_xbc_act", s, 256, 1,
                   [(sb[0][0], "row", D, 0), (sb[1][0], "row", D, 0), (dy, "row", D, 0), (d_exp, "const", D, 0),
                    (sb[0][2], "row", D, 0), (sb[1][2], "row", D, 0), (sb[0][3], "row", D, 0), (sb[1][3], "row", D, 0)],
                   [(2 * D, F32, 2 * D)])

    res = ew(silu_conv_bwd_fn, "ssm_conv_bwd", s, 256, 2,
             [(proj, "row", D, 4), (proj, "prev", D, 4), (proj, "next", D, 4), (ssm_cw, "const", D, 0), (ssm_cb, "const", D, 0),
              (dxbc_act, "row", D, 0), (dxbc_act, "prev", D, 0), (dxbc_act, "next", D, 0)], [(2 * D, F32, D)], [(2 * D, D)] * 4)
    dxbc = res[0]
    g_ssm_cw = jnp.concatenate(res[1:4], axis=0).T
    g_ssm_cb = res[4]
    ddt = jnp.pad(jnp.concatenate([sb[0][1][:, ::HD], sb[1][1][:, ::HD]], axis=1), ((0, 0), (0, 96)))

    def dt_bwd_fn(i, n, dd, r, b):
        dr = dd * _sigmoid(r + b)
        return dr, _colsum(dr)

    dproj_dt, g_dt_bias = ew(dt_bwd_fn, "dt_softplus_bwd", s, 512, 1,
                             [(ddt, "row", 128, 0), (proj_dt, "row", 128, 0), (dt_bias, "const", 128, 0)],
                             [(128, F32, 128)], [(128, 128)])
    g_a_log = [t[4][:, 0, ::HD].reshape(1, 16) for t in sb]

    dq, dk, dv = attn_bwd_all(proj, tabs, dmix, attn, lse, "attn_bwd")

    dproj = jnp.concatenate([dq, dk, dv, dz, dxbc], axis=1).astype(BF16)
    g_w_in = jnp.concatenate([matmul(h1, dproj, "tn", "d_w_in"), matmul(h1, dproj_dt, "tn", "d_w_in_dt")[:, :n_in - n_main]], axis=1)
    scatter_in, token = scatter_start([to_pieces(_col_shards(g_w_in, 4))], [], "scatter_start_in")
    dh1_a = matmul(dproj, w_main, "nt", "d_h1", after=token)
    dh1_b = matmul(dproj_dt, w_dt, "nt", "d_h1_dt")
    grad_x, g_n1w = ew(res_rms_bwd_fn, "rms1_bwd", s, 256, 1,
                       [(dx1, "row", D, 0), (dh1_a, "row", D, 0), (dh1_b, "row", D, 0), (x, "row", D, 0), (n1w, "const", D, 0)],
                       [(D, F32, D)], [(D, D)])

    small_g = {"norm1_w": g_n1w, "ssm_conv_w": g_ssm_cw, "ssm_conv_b": g_ssm_cb, "a_log_f": g_a_log[0], "a_log_b": g_a_log[1],
               "dt_bias_f": g_dt_bias[:, :16], "dt_bias_b": g_dt_bias[:, 16:32], "d_skip": g_dskip_l[:, ::HD],
               "ssm_norm_w": g_snw, "norm2_w": g_n2w, "ffn_conv_w": g_ffn_cw, "ffn_conv_b": g_ffn_cb, "final_norm_w": g_fnw}
    small_shapes = [small_g[n].shape for n in SMALL]
    scatter_small, token = scatter_start([], [_flat_rows([small_g[n] for n in SMALL], 128, SMALL_ROWS)], "scatter_start_small")
    core = lax.axis_index("c")

    def sum8_fn(i, n, *v):
        t = v[0].astype(F32)
        for u in v[1:]:
            t = t + u.astype(F32)
        return t

    def sum_pieces(sent, got, name):
        rows, w = got.shape[1:]
        tm = 256 if rows % 256 == 0 else rows
        mine = lax.dynamic_slice(sent, (chip, core, 0, 0), (1, 1, rows, w)).reshape(rows, w)
        ins = [(mine, "row", w, 0)] + [(got.reshape(8 * rows, w), "row", w, 0, k * (rows // tm)) for k in range(1, 8)]
        return ew(sum8_fn, name, rows, tm, 1, ins, [(w, F32, w)])[0]

    grads, delta, new_m, new_v = {}, {}, {}, {}

    def finish(names, sent, got, tag):
        summed = swap_halves([sum_pieces(a, b, "sum_pieces_" + n) for a, b, n in zip(sent, got, names)], "swap_halves_" + tag)
        for n, t in zip(names, summed):
            shp = p[n].shape
            grads[n] = t.reshape(shp)
            r = [u.reshape(shp[1:]) for u in (p[n], grads[n], p["m_" + n], p["v_" + n])]
            delta[n], new_m[n], new_v[n] = [u.reshape(shp) for u in adamw(*r, "adamw_" + n)]

    finish(REST, *scatter_wait(scatter_rest, "scatter_wait_rest", after=token), "rest")
    finish(("w_in",), *scatter_wait(scatter_in, "scatter_wait_in", after=new_v[REST[-1]]), "w_in")
    (sent_small,), (got_small,) = scatter_wait(scatter_small, "scatter_wait_small", after=new_v["w_in"])
    got_small = own_slot(got_small, sent_small, 2 * chip + core)
    small_sum, = ew(sum8_fn, "sum_small", SMALL_ROWS, SMALL_ROWS, 1,
                    [(got_small.reshape(8 * SMALL_ROWS, 128), "row", 128, 0, k) for k in range(8)], [(128, F32, 128)])
    for n, g in zip(SMALL, _split_flat(small_sum, small_shapes)):
        if n in ("ssm_conv_w", "ffn_conv_w"):
            rows = p[n].shape[1]
            g = lax.dynamic_slice_in_dim(g, chip * rows, rows, axis=0)
        grads[n] = g.reshape(p[n].shape)

    shapes = [p[n].shape for n in SMALL]
    total = sum(int(np.prod(sh)) for sh in shapes)
    rows = -(-total // 1024) * 8
    packs = [_flat_rows([t[n] for n in SMALL], 128, rows)
             for t in (p, grads, {n: p["m_" + n] for n in SMALL}, {n: p["v_" + n] for n in SMALL})]
    for dst, t in zip((delta, new_m, new_v), adamw(*packs, "adamw_small")):
        for n, u in zip(SMALL, _split_flat(t, shapes)):
            dst[n] = u
    return (loss, grad_x[None], *[grads[n] for n in WEIGHTS], *[delta[n] for n in WEIGHTS],
            *[new_m[n] for n in WEIGHTS], *[new_v[n] for n in WEIGHTS])
```

```python
import numpy as np
import jax
import jax.numpy as jnp
from jax import lax
from jax.experimental import pallas as pl
from jax.experimental.pallas import tpu as pltpu

F32, BF16 = jnp.float32, jnp.bfloat16
MESH = pl.DeviceIdType.MESH
V7X_VMEM_LIMIT = 56 * 1024 * 1024

D = 1024
HD = 64
EPS = 1e-6
CHUNK = 128
D_FF = 2816
ROPE_DIM = 16
ROPE_THETA = 500000.0
PATTERN_DILATIONS = (1, 4, 16)
BAND = 64
SMALL_ROWS = 280
ADAM_LR, ADAM_B1, ADAM_B2, ADAM_EPS, ADAM_WD, ADAM_STEP = 0.001, 0.9, 0.999, 1e-08, 0.01, 10

NN = (((1,), (0,)), ((), ()))
NT = (((1,), (1,)), ((), ()))
TN = (((0,), (0,)), ((), ()))


def _pcall(body, **kw):
    return pl.pallas_call(body, **kw)


def _cparams(sem=None):
    return pltpu.CompilerParams(dimension_semantics=sem, vmem_limit_bytes=V7X_VMEM_LIMIT)


def _dot(a, b, dims=NN):
    return lax.dot_general(a, b, dims, preferred_element_type=F32)


def _pick(n, cap):
    if n <= cap:
        return n
    best = 0
    for t in range(128, cap + 1, 128):
        if n % t == 0:
            best = t
    assert best, (n, cap)
    return best


def _iota(shape, dim):
    return lax.broadcasted_iota(jnp.int32, shape, dim)


def _parts(x, n):
    out, r = [], x
    for _ in range(n):
        h = r.astype(BF16)
        out.append(h)
        r = r - h.astype(F32)
    return out


def _sigmoid(x):
    return 1.0 / (1.0 + jnp.exp(-x))


def _silu(x):
    return x * _sigmoid(x)


def _dsilu(x):
    s = _sigmoid(x)
    return s * (1.0 + x * (1.0 - s))


def matmul(a, b, mode, name, out_dtype=F32, after=None, b_k_off=0):
    if mode == "nn":
        (m, k), (_, n) = a.shape, b.shape
    elif mode == "nt":
        (m, k), (n, _) = a.shape, b.shape
    else:
        (k, m), (_, n) = a.shape, b.shape
    tm, tn, tk = _pick(m, 1408), _pick(n, 1408), _pick(k, 1408)
    nk = k // tk
    dims = {"nn": NN, "nt": NT, "tn": TN}[mode]
    a_spec = pl.BlockSpec((tk, tm), lambda i, j, kk: (kk, i)) if mode == "tn" else pl.BlockSpec((tm, tk), lambda i, j, kk: (i, kk))
    b_spec = pl.BlockSpec((tn, tk), lambda i, j, kk: (j, kk + b_k_off)) if mode == "nt" else pl.BlockSpec((tk, tn), lambda i, j, kk: (kk, j))
    extra = [] if after is None else [after]

    def body(a_ref, b_ref, *rest):
        o_ref, acc = rest[len(extra)], rest[len(extra) + 1:]
        part = _dot(a_ref[...].astype(BF16), b_ref[...].astype(BF16), dims)
        if nk == 1:
            o_ref[...] = part.astype(o_ref.dtype)
            return
        acc_ref, kk = acc[0], pl.program_id(2)

        @pl.when(kk == 0)
        def _():
            acc_ref[...] = part

        @pl.when((kk > 0) & (kk < nk - 1))
        def _():
            acc_ref[...] += part

        @pl.when(kk == nk - 1)
        def _():
            o_ref[...] = (acc_ref[...] + part).astype(o_ref.dtype)

    return _pcall(
        body, name=name, grid=(m // tm, n // tn, nk), in_specs=[a_spec, b_spec] + [pl.BlockSpec(memory_space=pl.ANY)] * len(extra),
        out_specs=pl.BlockSpec((tm, tn), lambda i, j, kk: (i, j)),
        out_shape=jax.ShapeDtypeStruct((m, n), out_dtype),
        scratch_shapes=[pltpu.VMEM((tm, tn), F32)] if nk > 1 else [],
        compiler_params=_cparams(("parallel", "parallel", "arbitrary")),
    )(a, b, *extra)


def ew(fn, name, rows, tm, ncol, ins, outs, accs=()):
    nrow = rows // tm
    r8 = tm // 8
    in_specs, arrays = [], []
    for ent in ins:
        arr, kind, w, off = ent[:4]
        roff = ent[4] if len(ent) > 4 else 0
        if kind == "row":
            spec = pl.BlockSpec((tm, w), lambda j, i, off=off, roff=roff: (i + roff, j + off))
        elif kind == "const":
            spec = pl.BlockSpec((arr.shape[0], w), lambda j, i, off=off: (0, j + off))
        elif kind == "prev":
            spec = pl.BlockSpec((8, w), lambda j, i, off=off: (jnp.maximum(i * r8 - 1, 0), j + off))
        else:
            spec = pl.BlockSpec((8, w), lambda j, i, off=off: (jnp.minimum((i + 1) * r8, rows // 8 - 1), j + off))
        in_specs.append(spec)
        arrays.append(arr)
    out_specs = [pl.BlockSpec((tm, w), lambda j, i: (i, j)) for (_, _, w) in outs]
    out_shape = [jax.ShapeDtypeStruct((rows, c), dt) for (c, dt, _) in outs]
    out_specs += [pl.BlockSpec((1, w), lambda j, i: (0, j)) for (_, w) in accs]
    out_shape += [jax.ShapeDtypeStruct((1, c), F32) for (c, _) in accs]
    nin, nout = len(ins), len(outs)

    def body(*refs):
        i = pl.program_id(1)
        res = fn(i, nrow, *[r[...] for r in refs[:nin]])
        if not isinstance(res, (tuple, list)):
            res = (res,)
        for r, v in zip(refs[nin:nin + nout], res[:nout]):
            r[...] = v.astype(r.dtype)
        if accs:
            acc_refs = refs[nin + nout:]

            @pl.when(i == 0)
            def _():
                for r in acc_refs:
                    r[...] = jnp.zeros_like(r)

            for r, v in zip(acc_refs, res[nout:]):
                r[...] += v

    res = _pcall(
        body, name=name, grid=(ncol, nrow), in_specs=in_specs, out_specs=out_specs, out_shape=out_shape,
        compiler_params=_cparams(("parallel", "arbitrary")),
    )(*arrays)
    return res


def _shift_down(x, prev8, i):
    first = jnp.where(i == 0, 0.0, prev8[7:8, :])
    return jnp.where(_iota(x.shape, 0) == 0, first, pltpu.roll(x, 1, 0))


def _shift_up(x, next8, i, nrow):
    last = jnp.where(i == nrow - 1, 0.0, next8[0:1, :])
    return jnp.where(_iota(x.shape, 0) == x.shape[0] - 1, last, pltpu.roll(x, x.shape[0] - 1, 0))


def _colsum(x):
    return jnp.sum(x, axis=0, keepdims=True)


def _extend(x, prev8, next8, i, nrow):
    return jnp.concatenate([jnp.where(i == 0, 0.0, prev8), x, jnp.where(i == nrow - 1, 0.0, next8)], axis=0)


def _taps(xe):
    return pltpu.roll(xe, 1, 0), xe, pltpu.roll(xe, xe.shape[0] - 1, 0)


def _mid(xe):
    return xe[8:xe.shape[0] - 8]


def _conv3(w, b, taps):
    return w[0:1] * taps[0] + w[1:2] * taps[1] + w[2:3] * taps[2] + b


def _conv3_t(w, d_ext):
    t = _taps(d_ext)
    return _mid(w[0:1] * t[2] + w[1:2] * t[1] + w[2:3] * t[0])


def ffn_conv_bwd_fn(i, n, g, gp, gn, u, up_, un, wg, wu, bg, bu, da, dap, dan):
    gt, ut = _taps(_extend(g, gp, gn, i, n)), _taps(_extend(u, up_, un, i, n))
    dae = _extend(da, dap, dan, i, n)
    gate, upv = _conv3(wg, bg, gt), _conv3(wu, bu, ut)
    dg, du = dae * upv * _dsilu(gate), dae * _silu(gate)
    dgm, dum = _mid(dg), _mid(du)
    sums = [_colsum(dgm * _mid(t)) for t in gt] + [_colsum(dum * _mid(t)) for t in ut] + [_colsum(dgm), _colsum(dum)]
    return (_conv3_t(wg, dg), _conv3_t(wu, du)) + tuple(sums)


def silu_conv_bwd_fn(i, n, xv, xp, xn, w, b, da, dap, dan):
    xt = _taps(_extend(xv, xp, xn, i, n))
    du = _extend(da, dap, dan, i, n) * _dsilu(_conv3(w, b, xt))
    dum = _mid(du)
    return (_conv3_t(w, du),) + tuple(_colsum(dum * _mid(t)) for t in xt) + (_colsum(dum),)


def _rms_fwd(x, w):
    r = lax.rsqrt(jnp.mean(x * x, axis=-1, keepdims=True) + EPS)
    return x * r * w


def _rms_bwd(dy, x, w):
    r = lax.rsqrt(jnp.mean(x * x, axis=-1, keepdims=True) + EPS)
    xh = x * r
    dxh = dy * w
    dx = r * (dxh - xh * jnp.mean(dxh * xh, axis=-1, keepdims=True))
    return dx, _colsum(dy * xh)


def _rope_tables(s):
    half = ROPE_DIM // 2
    inv_freq = jnp.power(ROPE_THETA, -jnp.arange(half, dtype=F32) * 2.0 / ROPE_DIM)
    ang = jnp.arange(s, dtype=F32)[:, None] * inv_freq[None, :]
    cos, sin = jnp.cos(ang), jnp.sin(ang)
    one, zero = jnp.ones((s, HD - ROPE_DIM), F32), jnp.zeros((s, HD - ROPE_DIM), F32)
    z8 = jnp.zeros((s, half), F32)
    c = jnp.concatenate([cos, cos, one], axis=1)
    sa = jnp.concatenate([-sin, z8, zero], axis=1)
    sb = jnp.concatenate([z8, sin, zero], axis=1)
    return [jnp.tile(t, (1, 2)) for t in (c, sa, sb)]


ATTN_CHUNK = 1024


def _attn_plan(s):
    plan = []
    for d in PATTERN_DILATIONS:
        per_res = ATTN_CHUNK // d
        tq = min(128, per_res)
        plan.append((d, tq, min(s // d, tq + 2 * BAND), per_res // tq, s // d))
    return plan


def _rows(start, size, d):
    return pl.ds(start, size) if d == 1 else pl.ds(start, size, stride=d)


def _for_tiles(chunk, pat, fn):
    d, tq, win, nblk, seq_len = pat
    for b in range(nblk):
        t0 = chunk * (ATTN_CHUNK // d) + b * tq
        kloc = jnp.clip(t0 - BAND, 0, seq_len - win)
        valid = jnp.abs(kloc + _iota((tq, win), 1) - (t0 + _iota((tq, win), 0))) <= BAND
        valid = jnp.concatenate([valid, valid], axis=0)
        if d == 1:
            fn(b * tq, pl.multiple_of(kloc, BAND), valid)
        else:
            def step(r, carry, qoff=d * b * tq, koff=d * kloc, valid=valid):
                fn(qoff + r, koff + r, valid)
                return carry
            lax.fori_loop(0, d, step, 0, unroll=min(d, 8))


def _stack_heads(x, head0):
    zero = jnp.zeros_like(x)
    return jnp.concatenate([jnp.where(head0, x, zero), jnp.where(head0, zero, x)], axis=0)


def _rope_pair(x, c, sa, sb):
    n = x.shape[1]
    return x * c + pltpu.roll(x, n - 8, 1) * sa + pltpu.roll(x, 8, 1) * sb


def _rope_pair_t(dy, c, sa, sb):
    n = dy.shape[1]
    return dy * c + pltpu.roll(dy * sa, 8, 1) + pltpu.roll(dy * sb, n - 8, 1)


def _attn_specs(s):
    whole = lambda off: pl.BlockSpec((s, 128), lambda p, c: (0, off + p))
    table = pl.BlockSpec((s, 128), lambda p, c: (0, 0))
    chunk = pl.BlockSpec((ATTN_CHUNK, 128), lambda p, c: (c, p))
    return whole, table, chunk


def attn_fwd_all(proj, tabs, name):
    s = proj.shape[0]
    plan = _attn_plan(s)
    whole, table, chunk_spec = _attn_specs(s)

    def body(q_ref, k_ref, v_ref, c_ref, sa_ref, sb_ref, o_ref, lse_ref, qs, ks, acc_s, m_s, l_s):
        chunk = pl.program_id(1)

        @pl.when(chunk == 0)
        def _():
            qs[...] = _rope_pair(q_ref[...], c_ref[...], sa_ref[...], sb_ref[...]) * (HD ** -0.5)
            ks[...] = _rope_pair(k_ref[...], c_ref[...], sa_ref[...], sb_ref[...])

        base = pl.multiple_of(chunk * ATTN_CHUNK, ATTN_CHUNK)
        for pi, pat in enumerate(plan):
            d, tq, win = pat[:3]
            head0 = _iota((tq, 128), 1) < HD

            def tile(qrow, krow, valid, pi=pi, d=d, tq=tq, win=win, head0=head0):
                qv = qs[_rows(base + qrow, tq, d), :].astype(BF16)
                kw = ks[_rows(krow, win, d), :].astype(BF16)
                vw = v_ref[_rows(krow, win, d), :].astype(BF16)
                v_ones = jnp.concatenate([vw, jnp.ones_like(vw)], axis=1)
                sc = jnp.where(valid, _dot(_stack_heads(qv, head0), kw, NT), -1e30)
                mh = jnp.max(sc, axis=1, keepdims=True)
                pv = _dot(jnp.exp(sc - mh).astype(BF16), v_ones)
                acc_s[pi, _rows(qrow, tq, d), :] = jnp.where(head0, pv[:tq, :128], pv[tq:, :128])
                m_s[pi, _rows(qrow, tq, d), :] = jnp.where(head0, mh[:tq], mh[tq:])
                l_s[pi, _rows(qrow, tq, d), :] = jnp.where(head0, pv[:tq, 128:], pv[tq:, 128:])

            _for_tiles(chunk, pat, tile)
        m_all = jnp.maximum(jnp.maximum(m_s[0], m_s[1]), m_s[2])
        e = [jnp.exp(m_s[k] - m_all) for k in range(3)]
        den = e[0] * l_s[0] + e[1] * l_s[1] + e[2] * l_s[2]
        o_ref[...] = (e[0] * acc_s[0] + e[1] * acc_s[1] + e[2] * acc_s[2]) / den
        lse_ref[...] = m_all + jnp.log(den)

    stat = pltpu.VMEM((3, ATTN_CHUNK, 128), F32)
    return _pcall(
        body, name=name, grid=(D // 128, s // ATTN_CHUNK),
        in_specs=[whole(0), whole(8), whole(16), table, table, table], out_specs=[chunk_spec, chunk_spec],
        out_shape=[jax.ShapeDtypeStruct((s, D), F32)] * 2,
        scratch_shapes=[pltpu.VMEM((s, 128), F32), pltpu.VMEM((s, 128), F32), stat, stat, stat],
        compiler_params=_cparams(("parallel", "arbitrary")),
    )(proj, proj, proj, *tabs)


def attn_bwd_all(proj, tabs, dmix, o, lse, name):
    s = proj.shape[0]
    plan = _attn_plan(s)
    whole, table, chunk_spec = _attn_specs(s)
    nchunk = s // ATTN_CHUNK

    def body(q_ref, k_ref, v_ref, c_ref, sa_ref, sb_ref, do_ref, o_ref, lse_ref, dq_ref, dk_ref, dv_ref, qs, ks, aug0_s, aug1_s):
        chunk = pl.program_id(1)

        @pl.when(chunk == 0)
        def _():
            qs[...] = _rope_pair(q_ref[...], c_ref[...], sa_ref[...], sb_ref[...]) * (HD ** -0.5)
            ks[...] = _rope_pair(k_ref[...], c_ref[...], sa_ref[...], sb_ref[...])
            dk_ref[...] = jnp.zeros_like(dk_ref)
            dv_ref[...] = jnp.zeros_like(dv_ref)

        base = pl.multiple_of(chunk * ATTN_CHUNK, ATTN_CHUNK)
        prod = do_ref[...] * o_ref[...]
        first = _iota(prod.shape, 1) < HD
        delta = jnp.where(first, jnp.sum(jnp.where(first, prod, 0.0), axis=1, keepdims=True),
                          jnp.sum(jnp.where(first, 0.0, prod), axis=1, keepdims=True))
        lane = _iota(prod.shape, 1)

        def as_lanes(lse_h, delta_h):
            a, b = [u.astype(F32) for u in _parts(lse_h, 3)], [u.astype(F32) for u in _parts(delta_h, 3)]
            out = jnp.zeros_like(lse_h)
            for k, u in enumerate(a + b):
                out = jnp.where(lane == k, u, out)
            return out

        lsev = lse_ref[...]
        aug0_s[...] = as_lanes(lsev, delta)
        aug1_s[...] = as_lanes(pltpu.roll(lsev, HD, 1), pltpu.roll(delta, HD, 1))
        for pi, pat in enumerate(plan):
            d, tq, win = pat[:3]
            head0 = _iota((tq, 128), 1) < HD

            def tile(qrow, krow, valid, pi=pi, d=d, tq=tq, win=win, head0=head0):
                qv = qs[_rows(base + qrow, tq, d), :].astype(BF16)
                kw = ks[_rows(krow, win, d), :].astype(BF16)
                vw = v_ref[_rows(krow, win, d), :].astype(BF16)
                dob = do_ref[_rows(qrow, tq, d), :].astype(BF16)
                aug = jnp.concatenate([aug0_s[_rows(qrow, tq, d), :], aug1_s[_rows(qrow, tq, d), :]], axis=0).astype(BF16)
                klane = _iota((win, 128), 1)
                minus_lse = jnp.where(klane < 3, -1.0, 0.0).astype(BF16)
                minus_delta = jnp.where((klane >= 3) & (klane < 6), -1.0, 0.0).astype(BF16)
                q2, do2 = _stack_heads(qv, head0), _stack_heads(dob, head0)
                s_lse = _dot(jnp.concatenate([q2, aug], axis=1), jnp.concatenate([kw, minus_lse], axis=1), NT)
                dp_delta = _dot(jnp.concatenate([do2, aug], axis=1), jnp.concatenate([vw, minus_delta], axis=1), NT)
                p = jnp.where(valid, jnp.exp(s_lse), 0.0)
                ds = (p * dp_delta).astype(BF16)
                dq2 = _dot(ds, kw)
                dk = _dot(ds, q2, TN)
                dv = _dot(p.astype(BF16), do2, TN)
                dqv = jnp.where(head0, dq2[:tq], dq2[tq:])
                if pi == 0:
                    dq_ref[_rows(qrow, tq, d), :] = dqv
                else:
                    dq_ref[_rows(qrow, tq, d), :] += dqv
                dk_ref[_rows(krow, win, d), :] += dk
                dv_ref[_rows(krow, win, d), :] += dv

            _for_tiles(chunk, pat, tile)
        tab = [t[pl.ds(base, ATTN_CHUNK), :] for t in (c_ref, sa_ref, sb_ref)]
        dq_ref[...] = _rope_pair_t(dq_ref[...] * (HD ** -0.5), *tab)

        @pl.when(chunk == nchunk - 1)
        def _():
            dk_ref[...] = _rope_pair_t(dk_ref[...], c_ref[...], sa_ref[...], sb_ref[...])

    return _pcall(
        body, name=name, grid=(D // 128, nchunk),
        in_specs=[whole(0), whole(8), whole(16), table, table, table, chunk_spec, chunk_spec, chunk_spec],
        out_specs=[chunk_spec, whole(0), whole(0)], out_shape=[jax.ShapeDtypeStruct((s, D), F32)] * 3,
        scratch_shapes=[pltpu.VMEM((s, 128), F32), pltpu.VMEM((s, 128), F32)] + [pltpu.VMEM((ATTN_CHUNK, 128), F32)] * 2,
        compiler_params=_cparams(("parallel", "arbitrary")),
    )(proj, proj, proj, *tabs, dmix, o, lse)


def _ssd_common(x_ref, b_ref, c_ref, dt_ref, dtt_ref, a_ref, ar_ref, rev):
    ii, jj = _iota((CHUNK, CHUNK), 0), _iota((CHUNK, CHUNK), 1)
    low = jj >= ii if rev else jj <= ii
    x, dtx = x_ref[...], dt_ref[...]
    bm, cm = b_ref[...].astype(BF16), c_ref[...].astype(BF16)
    a = dtx * a_ref[...]
    arow = dtt_ref[0] * ar_ref[0]
    lowb = low.astype(BF16)
    cs = _dot(lowb, jnp.concatenate(_parts(a, 3), axis=1))
    cs = cs[:, :128] + cs[:, 128:256] + cs[:, 256:]
    csr = _dot(jnp.concatenate([p.astype(F32) for p in _parts(arow, 3)], axis=0).astype(BF16), lowb, NT)
    csr = csr[0:8] + csr[8:16] + csr[16:24]
    last = 0 if rev else CHUNK - 1
    tot = cs[last:last + 1, :]
    xdt = x * dtx
    cb = _dot(cm, bm, NT)
    lmats = [jnp.exp(jnp.where(low, cs[:, HD * h:HD * h + 1] - csr[h:h + 1, :], -1e30)) for h in range(2)]
    return dict(x=x, dtx=dtx, bm=bm, cm=cm, a=a, cs=cs, tot=tot, xdt=xdt, cb=cb, lmats=lmats, low=low, last=last)


SSD_SUB = 8


def _ssd_specs(s, rev_order):
    nblk, rows = s // (SSD_SUB * CHUNK), SSD_SUB * CHUNK
    ci = (lambda c: nblk - 1 - c) if rev_order else (lambda c: c)
    tile = lambda off, div: pl.BlockSpec((rows, 128), lambda p, c: (ci(c), off + p // div))
    common = [tile(0, 1), tile(8, 2), tile(12, 2), tile(0, 1),
              pl.BlockSpec((1, 8, rows), lambda p, c: (p, 0, ci(c))),
              pl.BlockSpec((1, 128), lambda p, c: (0, p)),
              pl.BlockSpec((1, 8, 128), lambda p, c: (p, 0, 0))]
    hs = pl.BlockSpec((1, SSD_SUB, CHUNK, 128), lambda p, c: (p, ci(c), 0, 0))
    return nblk, common, tile(0, 1), hs


def _chunk_rows(ref, j):
    return ref.at[pl.ds(j * CHUNK, CHUNK), :]


def _ssd_chunk(refs, j):
    return [_chunk_rows(r, j) for r in refs[:4]] + [refs[4].at[:, :, pl.ds(j * CHUNK, CHUNK)], refs[5], refs[6]]


def _ssd_args(xbc, t):
    return [xbc, xbc, xbc, t["dt_exp"], t["dtt"], t["a_exp"], t["a_rows"]]


def ssd_fwd(xbc, dirs, name):
    s = xbc.shape[0]
    nd = len(dirs)
    specs = [_ssd_specs(s, t["rev"]) for t in dirs]
    nck = specs[0][0]

    def one(rev, x_ref, b_ref, c_ref, dt_ref, dtt_ref, a_ref, ar_ref, y_ref, hs_ref, h_scr):
        v = _ssd_common(x_ref, b_ref, c_ref, dt_ref, dtt_ref, a_ref, ar_ref, rev)
        xdtb = v["xdt"].astype(BF16)
        yd = _dot(jnp.concatenate([v["cb"] * v["lmats"][h] for h in range(2)], axis=0).astype(BF16), xdtb)
        h_in = h_scr[...]
        hs_ref[0, 0] = h_in
        y_off = _dot(v["cm"], h_in.astype(BF16)) * jnp.exp(v["cs"])
        y_ref[...] = jnp.where(_iota((CHUNK, 128), 1) < HD, yd[:CHUNK], yd[CHUNK:]) + y_off
        decay = jnp.exp(v["tot"] - v["cs"])
        h_scr[...] = jnp.exp(v["tot"]) * h_in + _dot(v["bm"], (v["xdt"] * decay).astype(BF16), TN)

    def body(*refs):
        @pl.when(pl.program_id(1) == 0)
        def _():
            for k in range(nd):
                refs[9 * nd + k][...] = jnp.zeros((CHUNK, 128), F32)

        for k, t in enumerate(dirs):
            y_ref, hs_ref = refs[7 * nd + 2 * k:7 * nd + 2 * k + 2]
            for j in (range(SSD_SUB)[::-1] if t["rev"] else range(SSD_SUB)):
                one(t["rev"], *_ssd_chunk(refs[7 * k:7 * k + 7], j), _chunk_rows(y_ref, j), hs_ref.at[:, pl.ds(j, 1)], refs[9 * nd + k])

    res = _pcall(
        body, name=name, grid=(8, nck), in_specs=[sp for t in specs for sp in t[1]],
        out_specs=[sp for t in specs for sp in (t[2], t[3])],
        out_shape=[jax.ShapeDtypeStruct((s, D), F32), jax.ShapeDtypeStruct((8, s // CHUNK, CHUNK, 128), F32)] * nd,
        scratch_shapes=[pltpu.VMEM((CHUNK, 128), F32)] * nd, compiler_params=_cparams(("parallel", "arbitrary")),
    )(*[a for t in dirs for a in _ssd_args(xbc, t)])
    return [(res[2 * k], res[2 * k + 1]) for k in range(nd)]


def ssd_bwd(xbc, dirs, dy, name):
    s = xbc.shape[0]
    nd = len(dirs)
    specs = [_ssd_specs(s, not t["rev"]) for t in dirs]
    nck = specs[0][0]

    def one(rev, x_ref, b_ref, c_ref, dt_ref, dtt_ref, a_ref, ar_ref, hs_ref, dy_ref,
            dx_ref, ddt_ref, db_ref, dc_ref, dal_ref, dh_scr):
        v = _ssd_common(x_ref, b_ref, c_ref, dt_ref, dtt_ref, a_ref, ar_ref, rev)
        bm, cm, cs, tot, xdt = v["bm"], v["cm"], v["cs"], v["tot"], v["xdt"]
        h_in, dh = hs_ref[0, 0], dh_scr[...]
        dyv = dy_ref[...]
        dyb = dyv.astype(BF16)
        etot, decay, ecs = jnp.exp(tot), jnp.exp(tot - cs), jnp.exp(cs)
        xdtb = xdt.astype(BF16)
        xdec = xdt * decay
        dch = (dyv * ecs).astype(BF16)
        hb, dhb = h_in.astype(BF16), dh.astype(BF16)
        y_off = _dot(cm, hb) * ecs
        dc = _dot(dch, hb, NT)
        dh_y = _dot(cm, dch, TN)
        dxdec = _dot(bm, dhb)
        db = _dot(xdec.astype(BF16), dhb, NT)
        state_term = xdec * dxdec
        dtot = _colsum(dh * h_in) * etot + _colsum(state_term)
        head0 = _iota((CHUNK, 128), 1) < HD
        ii, jj = _iota((CHUNK, CHUNK), 0), _iota((CHUNK, CHUNK), 1)
        low_t = jj <= ii if rev else jj >= ii
        not_low_t = (~low_t).astype(BF16)
        g = _dot(_stack_heads(dyb, head0), xdtb, NT)
        gl = [g[:CHUNK] * v["lmats"][0], g[CHUNK:] * v["lmats"][1]]
        dcb = gl[0] + gl[1]
        dxd = _dot(jnp.concatenate([v["cb"] * v["lmats"][h] for h in range(2)], axis=1).astype(BF16), dyb, TN)
        dxd = jnp.where(head0, dxd[:CHUNK], dxd[CHUNK:])
        w = _dot(not_low_t, jnp.concatenate([gl[h] * v["cb"] for h in range(2)], axis=0).astype(BF16), NT)
        da_l = [jnp.sum(jnp.where(low_t, w[:, CHUNK * h:CHUNK * h + CHUNK], 0.0), axis=1, keepdims=True) for h in range(2)]
        dxdt = dxdec * decay + dxd
        dcbb = dcb.astype(BF16)
        dc_ref[...] = dc + _dot(dcbb, bm)
        db_ref[...] = db + _dot(dcbb, cm, TN)
        dcs = dyv * y_off - state_term + jnp.where(_iota((CHUNK, 128), 0) == v["last"], dtot, 0.0)
        lowb = v["low"].astype(BF16)
        da = _dot(lowb, jnp.concatenate(_parts(dcs, 2), axis=1), TN)
        da = da[:, :128] + da[:, 128:]
        seg = ((ii < HD) == (jj < HD)).astype(BF16)
        sums = _dot(jnp.concatenate(_parts(da, 2) + _parts(dxdt * v["x"], 2), axis=0), seg)
        da = sums[:CHUNK] + sums[CHUNK:2 * CHUNK] + jnp.where(head0, da_l[0], da_l[1])
        ddt_x = sums[2 * CHUNK:3 * CHUNK] + sums[3 * CHUNK:]
        dx_ref[...] = dxdt * v["dtx"]
        ddt_ref[...] = ddt_x + da * a_ref[...]
        dal_ref[0] += _colsum(da * v["a"])
        dh_scr[...] = etot * dh + dh_y

    def body(*refs):
        @pl.when(pl.program_id(1) == 0)
        def _():
            for k in range(nd):
                refs[14 * nd + k][...] = jnp.zeros((CHUNK, 128), F32)
                refs[9 * nd + 5 * k + 4][...] = jnp.zeros((1, 8, 128), F32)

        for k, t in enumerate(dirs):
            ins, outs = refs[9 * k:9 * k + 9], refs[9 * nd + 5 * k:9 * nd + 5 * k + 5]
            for j in (range(SSD_SUB) if t["rev"] else range(SSD_SUB)[::-1]):
                one(t["rev"], *_ssd_chunk(ins[:7], j), ins[7].at[:, pl.ds(j, 1)], _chunk_rows(ins[8], j),
                    *[_chunk_rows(r, j) for r in outs[:4]], outs[4], refs[14 * nd + k])

    acc_spec = pl.BlockSpec((1, 8, 128), lambda p, c: (p, 0, 0))
    res = _pcall(
        body, name=name, grid=(8, nck), in_specs=[sp for t in specs for sp in t[1] + [t[3], t[2]]],
        out_specs=[sp for t in specs for sp in [t[2]] * 4 + [acc_spec]],
        out_shape=([jax.ShapeDtypeStruct((s, D), F32)] * 4 + [jax.ShapeDtypeStruct((8, 8, 128), F32)]) * nd,
        scratch_shapes=[pltpu.VMEM((CHUNK, 128), F32)] * nd, compiler_params=_cparams(("parallel", "arbitrary")),
    )(*[a for t in dirs for a in _ssd_args(xbc, t) + [t["hs"], dy]])
    return [res[5 * k:5 * k + 5] for k in range(nd)]


def _group_norm_stats(g):
    r = [lax.rsqrt(jnp.mean(g[:, 256 * k:256 * k + 256] ** 2, axis=-1, keepdims=True) + EPS) for k in range(4)]
    grp = _iota(g.shape, 1) // 256
    return jnp.where(grp == 0, r[0], jnp.where(grp == 1, r[1], jnp.where(grp == 2, r[2], r[3])))


def _group_mean(t):
    m = [jnp.mean(t[:, 256 * k:256 * k + 256], axis=-1, keepdims=True) for k in range(4)]
    grp = _iota(t.shape, 1) // 256
    return jnp.where(grp == 0, m[0], jnp.where(grp == 1, m[1], jnp.where(grp == 2, m[2], m[3])))


def _mesh_pos():
    return lax.axis_index("x"), lax.axis_index("y"), lax.axis_index("c")


HBM = pl.BlockSpec(memory_space=pltpu.HBM)
SEM = pl.BlockSpec(memory_space=pltpu.SEMAPHORE)
EFFECT = pltpu.SideEffectType.DATAFLOW_SIDE_EFFECTING


def _hbm(t):
    return pltpu.with_memory_space_constraint(t, pltpu.HBM)


def _other_chips(x, y):
    return [(1 - x, y), (x, 1 - y), (1 - x, 1 - y)]


def _peer(x, y, c, m):
    return x ^ (m >> 2), y ^ ((m >> 1) & 1), c ^ (m & 1)


def gather_start(srcs_a, srcs_b, halved=()):
    srcs = [_hbm(t) for t in list(srcs_a) + list(srcs_b)]
    n, na = len(srcs), len(srcs_a)
    half = [k in halved for k in range(n)]
    lands = [_hbm(lax.empty((4,) + (t.shape[1:] if half[k] else t.shape), t.dtype)) for k, t in enumerate(srcs)]

    def body(*refs):
        src, land = refs[:n], refs[n:2 * n]
        sems = refs[2 * n:2 * n + 4]
        x, y, c = _mesh_pos()
        for k in range(n):
            for j, (px, py) in enumerate(_other_chips(x, y)):
                send, recv, idx = (sems[0], sems[1], 3 * k + j) if k < na else (sems[2], sems[3], 3 * (k - na) + j)
                pltpu.make_async_remote_copy(src_ref=src[k].at[c] if half[k] else src[k], dst_ref=land[k].at[2 * x + y], send_sem=send.at[idx],
                                             recv_sem=recv.at[idx], device_id=(px, py, c), device_id_type=MESH).start()

    sem_a, sem_b = pltpu.SemaphoreType.DMA((3 * na,)), pltpu.SemaphoreType.DMA((3 * (n - na),))
    res = _pcall(
        body, name="gather_start", in_specs=[HBM] * (2 * n), out_specs=[SEM] * 4 + [HBM] * (2 * n),
        out_shape=[sem_a, sem_a, sem_b, sem_b] + [pltpu.HBM(t.shape, t.dtype) for t in srcs + lands],
        input_output_aliases={i: 4 + i for i in range(2 * n)},
        compiler_params=pltpu.CompilerParams(has_side_effects=EFFECT),
    )(*srcs, *lands)
    thru_src, thru_land = res[4:4 + n], res[4 + n:]
    return ((res[0], res[1], thru_src[:na], thru_land[:na], half[:na]), (res[2], res[3], thru_src[na:], thru_land[na:], half[na:]))


def gather_wait(group, name, after=None):
    send, recv, srcs, lands, half = group
    n = len(srcs)

    def body(*refs):
        src, land, send_ref, recv_ref = refs[:n], refs[n:2 * n], refs[2 * n], refs[2 * n + 1]
        x, y, c = _mesh_pos()
        for j, (px, py) in enumerate(_other_chips(x, y)):
            for k in range(n):
                cp = pltpu.make_async_remote_copy(src_ref=src[k].at[0] if half[k] else src[k], dst_ref=land[k].at[2 * px + py], send_sem=send_ref.at[3 * k + j],
                                                  recv_sem=recv_ref.at[3 * k + j], device_id=(px, py, c), device_id_type=MESH)
                cp.wait_send()
                cp.wait_recv()

    extra = [] if after is None else [after]
    res = _pcall(
        body, name=name, in_specs=[HBM] * (2 * n) + [SEM, SEM] + [pl.BlockSpec(memory_space=pl.ANY)] * len(extra),
        out_specs=[HBM] * (2 * n), out_shape=[pltpu.HBM(t.shape, t.dtype) for t in list(srcs) + list(lands)],
        input_output_aliases={i: i for i in range(2 * n)}, compiler_params=pltpu.CompilerParams(has_side_effects=EFFECT),
    )(*srcs, *lands, send, recv, *extra)
    return res[:n], res[n:]


def scatter_start(pieces, smalls, name):
    srcs = [_hbm(t) for t in list(pieces) + list(smalls)]
    n, npc = len(srcs), len(pieces)
    lands = [_hbm(lax.empty((8,) + (t.shape[2:] if k < npc else t.shape), t.dtype)) for k, t in enumerate(srcs)]

    def body(*refs):
        src, land, send, recv = refs[:n], refs[n:2 * n], refs[2 * n], refs[2 * n + 1]
        token = refs[-1]
        x, y, c = _mesh_pos()
        for m in range(1, 8):
            px, py, pc = _peer(x, y, c, m)
            for k in range(n):
                s_ref = src[k].at[2 * px + py, pc] if k < npc else src[k]
                d_ref = land[k].at[m] if k < npc else land[k].at[4 * x + 2 * y + c]
                pltpu.make_async_remote_copy(src_ref=s_ref, dst_ref=d_ref, send_sem=send.at[7 * k + m - 1], recv_sem=recv.at[7 * k + m - 1],
                                             device_id=(px, py, pc), device_id_type=MESH).start()
        token[...] = jnp.zeros_like(token)

    sem = pltpu.SemaphoreType.DMA((7 * n,))
    res = _pcall(
        body, name=name, in_specs=[HBM] * (2 * n),
        out_specs=[SEM, SEM] + [HBM] * (2 * n) + [pl.BlockSpec(memory_space=pltpu.VMEM)],
        out_shape=[sem, sem] + [pltpu.HBM(t.shape, t.dtype) for t in srcs + lands] + [jax.ShapeDtypeStruct((8, 128), F32)],
        input_output_aliases={i: 2 + i for i in range(2 * n)},
        compiler_params=pltpu.CompilerParams(has_side_effects=EFFECT),
    )(*srcs, *lands)
    return (res[0], res[1], res[2:2 + n], res[2 + n:2 + 2 * n], npc), res[-1]


def scatter_wait(group, name, after=None):
    send, recv, srcs, lands, npc = group
    n = len(srcs)

    def body(*refs):
        src, land, send_ref, recv_ref = refs[:n], refs[n:2 * n], refs[2 * n], refs[2 * n + 1]
        x, y, c = _mesh_pos()
        for m in range(1, 8):
            px, py, pc = _peer(x, y, c, m)
            for k in range(n):
                s_ref = src[k].at[0, 0] if k < npc else src[k]
                d_ref = land[k].at[m] if k < npc else land[k].at[4 * px + 2 * py + pc]
                cp = pltpu.make_async_remote_copy(src_ref=s_ref, dst_ref=d_ref, send_sem=send_ref.at[7 * k + m - 1],
                                                  recv_sem=recv_ref.at[7 * k + m - 1], device_id=(px, py, pc), device_id_type=MESH)
                cp.wait_send()
                cp.wait_recv()

    extra = [] if after is None else [after]
    res = _pcall(
        body, name=name, in_specs=[HBM] * (2 * n) + [SEM, SEM] + [pl.BlockSpec(memory_space=pl.ANY)] * len(extra),
        out_specs=[HBM] * (2 * n), out_shape=[pltpu.HBM(t.shape, t.dtype) for t in list(srcs) + list(lands)],
        input_output_aliases={i: i for i in range(2 * n)}, compiler_params=pltpu.CompilerParams(has_side_effects=EFFECT),
    )(*srcs, *lands, send, recv, *extra)
    return res[:n], res[n:]


def swap_halves(pieces, name):
    n = len(pieces)
    whole = pl.BlockSpec(memory_space=pltpu.VMEM)

    def body(*refs):
        p_refs, o_refs, send_sems, recv_sems, local_sems = refs[:n], refs[n:2 * n], refs[2 * n], refs[2 * n + 1], refs[2 * n + 2]
        x, y, c = _mesh_pos()
        local = [pltpu.make_async_copy(p_refs[k], o_refs[k].at[c], local_sems.at[k]) for k in range(n)]
        for cp in local:
            cp.start()

        def copy(k, slot):
            return pltpu.make_async_remote_copy(src_ref=p_refs[k], dst_ref=o_refs[k].at[slot], send_sem=send_sems.at[k],
                                                recv_sem=recv_sems.at[k], device_id=(x, y, 1 - c), device_id_type=MESH)

        for k in range(n):
            copy(k, c).start()
        for k in range(n):
            copy(k, 1 - c).wait_recv()
        for k in range(n):
            copy(k, c).wait_send()
        for cp in local:
            cp.wait()

    return _pcall(
        body, name=name, in_specs=[whole] * n, out_specs=[whole] * n,
        out_shape=[jax.ShapeDtypeStruct((2,) + t.shape, t.dtype) for t in pieces],
        scratch_shapes=[pltpu.SemaphoreType.DMA((n,)), pltpu.SemaphoreType.DMA((n,)), pltpu.SemaphoreType.DMA((n,))],
        compiler_params=_cparams(),
    )(*pieces)


def adamw(w, g, m, v, name):
    rows, cols = w.shape
    tm = rows
    for t in (256, 352, 128, 144, 64, 32, 16, 8):
        if rows % t == 0:
            tm = t
            break

    def fn(i, nrow, wv, gv, mv, vv):
        mn = ADAM_B1 * mv + (1.0 - ADAM_B1) * gv
        vn = ADAM_B2 * vv + (1.0 - ADAM_B2) * (gv * gv)
        m_hat = mn / (1.0 - ADAM_B1 ** ADAM_STEP)
        v_hat = vn / (1.0 - ADAM_B2 ** ADAM_STEP)
        delta = -ADAM_LR * (m_hat / (jnp.sqrt(v_hat) + ADAM_EPS) + ADAM_WD * wv)
        return delta, mn, vn

    return ew(fn, name, rows, tm, 1, [(t, "row", cols, 0) for t in (w, g, m, v)], [(cols, F32, cols)] * 3)


REST = ("w_out", "w_up", "w_down")
SMALL = ("norm1_w", "ssm_conv_w", "ssm_conv_b", "a_log_f", "a_log_b", "dt_bias_f", "dt_bias_b", "d_skip",
         "ssm_norm_w", "norm2_w", "ffn_conv_w", "ffn_conv_b", "final_norm_w")
WEIGHTS = ("norm1_w", "w_in", "ssm_conv_w", "ssm_conv_b", "a_log_f", "a_log_b", "dt_bias_f", "dt_bias_b", "d_skip",
           "ssm_norm_w", "w_out", "norm2_w", "w_up", "ffn_conv_w", "ffn_conv_b", "w_down", "final_norm_w")
INPUTS = ("x",) + WEIGHTS + ("loss_target",) + tuple("m_" + n for n in WEIGHTS) + tuple("v_" + n for n in WEIGHTS)


def _flat_rows(parts, width, rows):
    flat = jnp.concatenate([p.reshape(-1) for p in parts])
    return jnp.pad(flat, (0, rows * width - flat.shape[0])).reshape(rows, width)


def _split_flat(flat, shapes):
    out, pos = [], 0
    flat = flat.reshape(-1)
    for shp in shapes:
        n = int(np.prod(shp))
        out.append(flat[pos:pos + n].reshape(shp))
        pos += n
    return out


def _col_shards(t, nshard):
    r, c = t.shape
    return t.reshape(r, nshard, c // nshard).transpose(1, 0, 2)


def _row_shards(t, nshard):
    r, c = t.shape
    return t.reshape(nshard, r // nshard, c)


def kernel(x, norm1_w, w_in, ssm_conv_w, ssm_conv_b, a_log_f, a_log_b, dt_bias_f, dt_bias_b, d_skip, ssm_norm_w, w_out, norm2_w, w_up, ffn_conv_w, ffn_conv_b, w_down, final_norm_w, loss_target, m_norm1_w, m_w_in, m_ssm_conv_w, m_ssm_conv_b, m_a_log_f, m_a_log_b, m_dt_bias_f, m_dt_bias_b, m_d_skip, m_ssm_norm_w, m_w_out, m_norm2_w, m_w_up, m_ffn_conv_w, m_ffn_conv_b, m_w_down, m_final_norm_w, v_norm1_w, v_w_in, v_ssm_conv_w, v_ssm_conv_b, v_a_log_f, v_a_log_b, v_dt_bias_f, v_dt_bias_b, v_d_skip, v_ssm_norm_w, v_w_out, v_norm2_w, v_w_up, v_ffn_conv_w, v_ffn_conv_b, v_w_down, v_final_norm_w):
    p = dict(zip(INPUTS, (x, norm1_w, w_in, ssm_conv_w, ssm_conv_b, a_log_f, a_log_b, dt_bias_f, dt_bias_b, d_skip, ssm_norm_w, w_out, norm2_w, w_up, ffn_conv_w, ffn_conv_b, w_down, final_norm_w, loss_target, m_norm1_w, m_w_in, m_ssm_conv_w, m_ssm_conv_b, m_a_log_f, m_a_log_b, m_dt_bias_f, m_dt_bias_b, m_d_skip, m_ssm_norm_w, m_w_out, m_norm2_w, m_w_up, m_ffn_conv_w, m_ffn_conv_b, m_w_down, m_final_norm_w, v_norm1_w, v_w_in, v_ssm_conv_w, v_ssm_conv_b, v_a_log_f, v_a_log_b, v_dt_bias_f, v_dt_bias_b, v_d_skip, v_ssm_norm_w, v_w_out, v_norm2_w, v_w_up, v_ffn_conv_w, v_ffn_conv_b, v_w_down, v_final_norm_w)))
    x = p["x"][0]
    tgt = p["loss_target"][0]
    s = x.shape[0]
    chip = 2 * lax.axis_index("x") + lax.axis_index("y")

    own_slot = lambda land, mine, slot: lax.dynamic_update_slice_in_dim(land, mine[None], slot, axis=0)
    core = lax.axis_index("c")
    src_in = p["w_in"][0].astype(BF16).reshape(2, D // 2, -1)
    src_rest = [p[n][0].astype(BF16) for n in REST]
    small_w = _flat_rows([p["ssm_conv_w"][0], p["ffn_conv_w"][0]], 128, 48)
    gather_in, gather_rest = gather_start([src_in, small_w], src_rest, halved=(0,))
    (src_in, small_w), (wg_in, sg) = gather_wait(gather_in, "gather_wait_in")
    wg_in = own_slot(wg_in, lax.dynamic_index_in_dim(src_in, core, 0, keepdims=False), chip)
    wg_in, = swap_halves([wg_in], "swap_w_in_rows")
    w_in = wg_in.transpose(0, 2, 1, 3).reshape(D, -1)
    sg = own_slot(sg, small_w, chip)
    n_in = w_in.shape[1]
    n_main = 6 * D
    w_main = w_in[:, :n_main]
    w_dt = jnp.pad(w_in[:, n_main:], ((0, 0), (0, 128 - (n_in - n_main))))
    sgf = sg.reshape(4, -1)
    n_sc, n_fc = p["ssm_conv_w"].shape[1], p["ffn_conv_w"].shape[1]
    ssm_cw = sgf[:, :n_sc * 3].reshape(-1, 3).T
    ffn_cw = sgf[:, n_sc * 3:(n_sc + n_fc) * 3].reshape(-1, 3).T
    ssm_cb, ffn_cb = p["ssm_conv_b"], p["ffn_conv_b"]
    n1w, n2w, snw, fnw = p["norm1_w"], p["norm2_w"], p["ssm_norm_w"], p["final_norm_w"].reshape(1, D)

    h1, = ew(lambda i, n, xv, w: _rms_fwd(xv, w), "rms1", s, 256, 1,
             [(x, "row", D, 0), (n1w, "const", D, 0)], [(D, BF16, D)])
    proj = matmul(h1, w_main, "nn", "in_proj")
    proj_dt = matmul(h1, w_dt, "nn", "in_proj_dt")
    tabs = _rope_tables(s)
    attn, lse = attn_fwd_all(proj, tabs, "attn_fwd")

    def conv_silu_fn(i, n, xv, xp, xn, w, b):
        return _silu(w[0:1] * _shift_down(xv, xp, i) + w[1:2] * xv + w[2:3] * _shift_up(xv, xn, i, n) + b)

    xbc_act, = ew(conv_silu_fn, "ssm_conv", s, 256, 2,
                  [(proj, "row", D, 4), (proj, "prev", D, 4), (proj, "next", D, 4),
                   (ssm_cw, "const", D, 0), (ssm_cb, "const", D, 0)], [(2 * D, F32, D)])
    dt_bias = jnp.pad(jnp.concatenate([p["dt_bias_f"], p["dt_bias_b"]], axis=1), ((0, 0), (0, 96)))

    lanes_of = np.arange(128)[:, None] == np.arange(D)[None, :] // HD
    spread = [jnp.asarray(np.roll(lanes_of, 16 * k, axis=0), BF16) for k in range(2)]

    def softplus_fn(i, n, r, b, ef, eb):
        t = r + b
        dtv = jnp.maximum(t, 0.0) + jnp.log(1.0 + jnp.exp(-jnp.abs(t)))
        parts = _parts(dtv, 3)
        return dtv, sum(_dot(q, ef) for q in parts), sum(_dot(q, eb) for q in parts)

    dt, dt_exp_f, dt_exp_b = ew(softplus_fn, "dt_softplus", s, 512, 1,
                                [(proj_dt, "row", 128, 0), (dt_bias, "const", 128, 0), (spread[0], "const", D, 0), (spread[1], "const", D, 0)],
                                [(128, F32, 128), (D, F32, D), (D, F32, D)])
    d_exp = jnp.repeat(p["d_skip"], HD, axis=1)
    ssd = []
    for k, (a_log, rev) in enumerate(((p["a_log_f"], False), (p["a_log_b"], True))):
        dt_k = dt[:, 16 * k:16 * k + 16]
        a_head = -jnp.exp(a_log)
        dt_exp = (dt_exp_f, dt_exp_b)[k]
        dtt = jnp.pad(dt_k.T.reshape(8, 2, s), ((0, 0), (0, 6), (0, 0)))
        a_exp = jnp.repeat(a_head, HD, axis=1)
        a_rows = jnp.broadcast_to(jnp.pad(a_head.reshape(8, 2), ((0, 0), (0, 6)))[:, :, None], (8, 8, 128))
        ssd.append(dict(dt_exp=dt_exp, dtt=dtt, a_exp=a_exp, a_rows=a_rows, rev=rev))
    for t, (y_k, hs_k) in zip(ssd, ssd_fwd(xbc_act, ssd, "ssd_fwd")):
        t["y"], t["hs"] = y_k, hs_k

    def gate_fn(i, n, yf, yb, xs, z, dsk, w):
        g = (yf + yb + dsk * xs) * _silu(z)
        return g * _group_norm_stats(g) * w

    ssm_out, = ew(gate_fn, "ssm_gate_norm", s, 256, 1,
                  [(ssd[0]["y"], "row", D, 0), (ssd[1]["y"], "row", D, 0), (xbc_act, "row", D, 0), (proj, "row", D, 3),
                   (d_exp, "const", D, 0), (snw, "const", D, 0)], [(D, F32, D)])
    mix = jnp.concatenate([attn, ssm_out], axis=1).astype(BF16)
    src_rest, wg_rest = gather_wait(gather_rest, "gather_wait_rest", after=mix)
    wg_rest = [own_slot(land, mine, chip) for land, mine in zip(wg_rest, src_rest)]
    w_out = wg_rest[0].reshape(-1, D)
    w_up = wg_rest[1].transpose(1, 0, 2).reshape(D, -1)
    w_down = wg_rest[2].reshape(-1, D)
    mix_w = matmul(mix, w_out, "nn", "out_proj")

    def res_rms_fn(i, n, xv, mw, w):
        x1v = xv + mw
        return x1v, _rms_fwd(x1v, w)

    x1, h2 = ew(res_rms_fn, "res_rms2", s, 256, 1, [(x, "row", D, 0), (mix_w, "row", D, 0), (n2w, "const", D, 0)],
                [(D, F32, D), (D, BF16, D)])
    hw = matmul(h2, w_up, "nn", "ffn_up")
    fw = D_FF // 2
    nfb = D_FF // fw
    ffn_conv_ins = [(hw, "row", fw, 0), (hw, "prev", fw, 0), (hw, "next", fw, 0),
                    (hw, "row", fw, nfb), (hw, "prev", fw, nfb), (hw, "next", fw, nfb),
                    (ffn_cw, "const", fw, 0), (ffn_cw, "const", fw, nfb), (ffn_cb, "const", fw, 0), (ffn_cb, "const", fw, nfb)]

    def ffn_conv(i, n, g, gp, gn, u, up_, un, wg_, wu, bg, bu):
        gs = (_shift_down(g, gp, i), g, _shift_up(g, gn, i, n))
        us = (_shift_down(u, up_, i), u, _shift_up(u, un, i, n))
        gate = wg_[0:1] * gs[0] + wg_[1:2] * gs[1] + wg_[2:3] * gs[2] + bg
        upv = wu[0:1] * us[0] + wu[1:2] * us[1] + wu[2:3] * us[2] + bu
        return gate, upv, gs, us

    def glu_fn(i, n, *blocks):
        gate, upv, _, _ = ffn_conv(i, n, *blocks)
        return _silu(gate) * upv

    act, = ew(glu_fn, "ffn_conv_glu", s, 256, nfb, ffn_conv_ins, [(D_FF, BF16, fw)])
    ffn = matmul(act, w_down, "nn", "ffn_down")

    def head_fn(i, n, x1v, fv, tv, w):
        x2 = x1v + fv
        r = lax.rsqrt(jnp.mean(x2 * x2, axis=-1, keepdims=True) + EPS)
        xh = x2 * r
        diff = xh * w - tv
        loss = 0.5 * jnp.sum(jnp.mean(diff * diff, axis=-1, keepdims=True), axis=0, keepdims=True)
        dout = diff * (1.0 / D)
        dxh = dout * w
        dx2 = r * (dxh - xh * jnp.mean(dxh * xh, axis=-1, keepdims=True))
        return dx2, jnp.broadcast_to(loss, (1, 128)), _colsum(dout * xh)

    dx2, loss_acc, g_fnw = ew(head_fn, "loss_head", s, 256, 1,
                              [(x1, "row", D, 0), (ffn, "row", D, 0), (tgt, "row", D, 0), (fnw, "const", D, 0)],
                              [(D, F32, D)], [(128, 128), (D, D)])
    loss = lax.psum(loss_acc[0, 0], ("x", "y", "c"))

    g_w_down = matmul(act, dx2, "tn", "d_w_down")
    dact = matmul(dx2, w_down, "nt", "d_act")

    res = ew(ffn_conv_bwd_fn, "ffn_conv_glu_bwd", s, 256, nfb,
             ffn_conv_ins + [(dact, "row", fw, 0), (dact, "prev", fw, 0), (dact, "next", fw, 0)],
             [(D_FF, F32, fw)] * 2, [(D_FF, fw)] * 8)
    dhw_g, dhw_u = res[0], res[1]
    g_ffn_cw = jnp.concatenate([jnp.concatenate(res[2:5], axis=0), jnp.concatenate(res[5:8], axis=0)], axis=1).T
    g_ffn_cb = jnp.concatenate([res[8], res[9]], axis=1)

    g_w_up = jnp.concatenate([matmul(h2, dhw_g, "tn", "d_w_up_gate"), matmul(h2, dhw_u, "tn", "d_w_up_up")], axis=1)
    dh2_a = matmul(dhw_g, w_up, "nt", "d_h2_gate")
    dh2_b = matmul(dhw_u, w_up, "nt", "d_h2_up", b_k_off=D_FF // _pick(D_FF, 1408))

    def res_rms_bwd_fn(i, n, dres, da, db, xin, w):
        dx, dw = _rms_bwd(da + db, xin, w)
        return dres + dx, dw

    dx1, g_n2w = ew(res_rms_bwd_fn, "res_rms2_bwd", s, 256, 1,
                    [(dx2, "row", D, 0), (dh2_a, "row", D, 0), (dh2_b, "row", D, 0), (x1, "row", D, 0), (n2w, "const", D, 0)],
                    [(D, F32, D)], [(D, D)])

    g_w_out = matmul(mix, dx1, "tn", "d_w_out")
    to_pieces = lambda t: t.astype(BF16).reshape(4, 2, t.shape[1] // 2, t.shape[2])
    shards_rest = [_row_shards(g_w_out, 4), _col_shards(g_w_up, 4), _row_shards(g_w_down, 4)]
    scatter_rest, token = scatter_start([to_pieces(t) for t in shards_rest], [], "scatter_start_rest")
    dmix = matmul(dx1, w_out, "nt", "d_mix", after=token)
    ii, jj = np.arange(D)[:, None] // HD, np.arange(D)[None, :] // HD
    seg = jnp.asarray(ii == jj, BF16)

    def gate_bwd_fn(i, n, dout, yf, yb, xs, z, dsk, w, segm):
        yt = yf + yb + dsk * xs
        sz = _silu(z)
        g = yt * sz
        r = _group_norm_stats(g)
        gh = g * r
        dn = dout * w
        dg = r * (dn - gh * _group_mean(dn * gh))
        dy = dg * sz
        dsk_lane = jnp.broadcast_to(_colsum(dy * xs), (8, D))
        return dy, dg * yt * _dsilu(z), _colsum(dout * gh), sum(_dot(q, segm) for q in _parts(dsk_lane, 2))[0:1]

    dy, dz, g_snw, g_dskip_l = ew(
        gate_bwd_fn, "ssm_gate_norm_bwd", s, 256, 1,
        [(dmix, "row", D, 1), (ssd[0]["y"], "row", D, 0), (ssd[1]["y"], "row", D, 0), (xbc_act, "row", D, 0),
         (proj, "row", D, 3), (d_exp, "const", D, 0), (snw, "const", D, 0), (seg, "const", D, 0)],
        [(D, F32, D)] * 2, [(D, D)] * 2)
    sb = ssd_bwd(xbc_act, ssd, dy, "ssd_bwd")

    def dxbc_act_fn(i, n, dxf, dxb, dyv, dsk, dbf, dbb, dcf, dcb_):
        db, dc = dbf + dbb, dcf + dcb_
        db = [db[:, 256 * g:256 * g + 128] + db[:, 256 * g + 128:256 * g + 256] for g in range(4)]
        dc = [dc[:, 256 * g:256 * g + 128] + dc[:, 256 * g + 128:256 * g + 256] for g in range(4)]
        return jnp.concatenate([dxf + dxb + dyv * dsk] + db + dc, axis=1)

    dxbc_act, = ew(dxbc_act_fn, "d_xbc_act", s, 256, 1,
                   [(sb[0][0], "row", D, 0), (sb[1][0], "row", D, 0), (dy, "row", D, 0), (d_exp, "const", D, 0),
                    (sb[0][2], "row", D, 0), (sb[1][2], "row", D, 0), (sb[0][3], "row", D, 0), (sb[1][3], "row", D, 0)],
                   [(2 * D, F32, 2 * D)])

    res = ew(silu_conv_bwd_fn, "ssm_conv_bwd", s, 256, 2,
             [(proj, "row", D, 4), (proj, "prev", D, 4), (proj, "next", D, 4), (ssm_cw, "const", D, 0), (ssm_cb, "const", D, 0),
              (dxbc_act, "row", D, 0), (dxbc_act, "prev", D, 0), (dxbc_act, "next", D, 0)], [(2 * D, F32, D)], [(2 * D, D)] * 4)
    dxbc = res[0]
    g_ssm_cw = jnp.concatenate(res[1:4], axis=0).T
    g_ssm_cb = res[4]
    ddt = jnp.pad(jnp.concatenate([sb[0][1][:, ::HD], sb[1][1][:, ::HD]], axis=1), ((0, 0), (0, 96)))

    def dt_bwd_fn(i, n, dd, r, b):
        dr = dd * _sigmoid(r + b)
        return dr, _colsum(dr)

    dproj_dt, g_dt_bias = ew(dt_bwd_fn, "dt_softplus_bwd", s, 512, 1,
                             [(ddt, "row", 128, 0), (proj_dt, "row", 128, 0), (dt_bias, "const", 128, 0)],
                             [(128, F32, 128)], [(128, 128)])
    g_a_log = [t[4][:, 0, ::HD].reshape(1, 16) for t in sb]

    dq, dk, dv = attn_bwd_all(proj, tabs, dmix, attn, lse, "attn_bwd")

    dproj = jnp.concatenate([dq, dk, dv, dz, dxbc], axis=1).astype(BF16)
    g_w_in = jnp.concatenate([matmul(h1, dproj, "tn", "d_w_in"), matmul(h1, dproj_dt, "tn", "d_w_in_dt")[:, :n_in - n_main]], axis=1)
    scatter_in, token = scatter_start([to_pieces(_col_shards(g_w_in, 4))], [], "scatter_start_in")
    dh1_a = matmul(dproj, w_main, "nt", "d_h1", after=token)
    dh1_b = matmul(dproj_dt, w_dt, "nt", "d_h1_dt")
    grad_x, g_n1w = ew(res_rms_bwd_fn, "rms1_bwd", s, 256, 1,
                       [(dx1, "row", D, 0), (dh1_a, "row", D, 0), (dh1_b, "row", D, 0), (x, "row", D, 0), (n1w, "const", D, 0)],
                       [(D, F32, D)], [(D, D)])

    small_g = {"norm1_w": g_n1w, "ssm_conv_w": g_ssm_cw, "ssm_conv_b": g_ssm_cb, "a_log_f": g_a_log[0], "a_log_b": g_a_log[1],
               "dt_bias_f": g_dt_bias[:, :16], "dt_bias_b": g_dt_bias[:, 16:32], "d_skip": g_dskip_l[:, ::HD],
               "ssm_norm_w": g_snw, "norm2_w": g_n2w, "ffn_conv_w": g_ffn_cw, "ffn_conv_b": g_ffn_cb, "final_norm_w": g_fnw}
    small_shapes = [small_g[n].shape for n in SMALL]
    scatter_small, token = scatter_start([], [_flat_rows([small_g[n] for n in SMALL], 128, SMALL_ROWS)], "scatter_start_small")

    def sum8_fn(i, n, *v):
        t = v[0].astype(F32)
        for u in v[1:]:
            t = t + u.astype(F32)
        return t

    def sum_pieces(sent, got, name):
        rows, w = got.shape[1:]
        tm = 256 if rows % 256 == 0 else rows
        mine = lax.dynamic_slice(sent, (chip, core, 0, 0), (1, 1, rows, w)).reshape(rows, w)
        ins = [(mine, "row", w, 0)] + [(got.reshape(8 * rows, w), "row", w, 0, k * (rows // tm)) for k in range(1, 8)]
        return ew(sum8_fn, name, rows, tm, 1, ins, [(w, F32, w)])[0]

    grads, delta, new_m, new_v = {}, {}, {}, {}

    def finish(names, sent, got, tag):
        summed = swap_halves([sum_pieces(a, b, "sum_pieces_" + n) for a, b, n in zip(sent, got, names)], "swap_halves_" + tag)
        for n, t in zip(names, summed):
            shp = p[n].shape
            grads[n] = t.reshape(shp)
            r = [u.reshape(shp[1:]) for u in (p[n], grads[n], p["m_" + n], p["v_" + n])]
            delta[n], new_m[n], new_v[n] = [u.reshape(shp) for u in adamw(*r, "adamw_" + n)]

    finish(REST, *scatter_wait(scatter_rest, "scatter_wait_rest", after=token), "rest")
    finish(("w_in",), *scatter_wait(scatter_in, "scatter_wait_in", after=new_v[REST[-1]]), "w_in")
    (sent_small,), (got_small,) = scatter_wait(scatter_small, "scatter_wait_small", after=new_v["w_in"])
    got_small = own_slot(got_small, sent_small, 2 * chip + core)
    small_sum, = ew(sum8_fn, "sum_small", SMALL_ROWS, SMALL_ROWS, 1,
                    [(got_small.reshape(8 * SMALL_ROWS, 128), "row", 128, 0, k) for k in range(8)], [(128, F32, 128)])
    for n, g in zip(SMALL, _split_flat(small_sum, small_shapes)):
        if n in ("ssm_conv_w", "ffn_conv_w"):
            rows = p[n].shape[1]
            g = lax.dynamic_slice_in_dim(g, chip * rows, rows, axis=0)
        grads[n] = g.reshape(p[n].shape)

    shapes = [p[n].shape for n in SMALL]
    total = sum(int(np.prod(sh)) for sh in shapes)
    rows = -(-total // 1024) * 8
    packs = [_flat_rows([t[n] for n in SMALL], 128, rows)
             for t in (p, grads, {n: p["m_" + n] for n in SMALL}, {n: p["v_" + n] for n in SMALL})]
    for dst, t in zip((delta, new_m, new_v), adamw(*packs, "adamw_small")):
        for n, u in zip(SMALL, _split_flat(t, shapes)):
            dst[n] = u
    return (loss, grad_x[None], *[grads[n] for n in WEIGHTS], *[delta[n] for n in WEIGHTS],
            *[new_m[n] for n in WEIGHTS], *[new_v[n] for n in WEIGHTS])
```

```python
import numpy as np
import jax
import jax.numpy as jnp
from jax import lax
from jax.experimental import pallas as pl
from jax.experimental.pallas import tpu as pltpu

F32, BF16 = jnp.float32, jnp.bfloat16
MESH = pl.DeviceIdType.MESH
V7X_VMEM_LIMIT = 56 * 1024 * 1024

D = 1024
HD = 64
EPS = 1e-6
CHUNK = 128
D_FF = 2816
ROPE_DIM = 16
ROPE_THETA = 500000.0
PATTERN_DILATIONS = (1, 4, 16)
BAND = 64
SMALL_ROWS = 280
ADAM_LR, ADAM_B1, ADAM_B2, ADAM_EPS, ADAM_WD, ADAM_STEP = 0.001, 0.9, 0.999, 1e-08, 0.01, 10

NN = (((1,), (0,)), ((), ()))
NT = (((1,), (1,)), ((), ()))
TN = (((0,), (0,)), ((), ()))


def _pcall(body, **kw):
    return pl.pallas_call(body, **kw)


def _cparams(sem=None):
    return pltpu.CompilerParams(dimension_semantics=sem, vmem_limit_bytes=V7X_VMEM_LIMIT)


def _dot(a, b, dims=NN):
    return lax.dot_general(a, b, dims, preferred_element_type=F32)


def _pick(n, cap):
    if n <= cap:
        return n
    best = 0
    for t in range(128, cap + 1, 128):
        if n % t == 0:
            best = t
    assert best, (n, cap)
    return best


def _iota(shape, dim):
    return lax.broadcasted_iota(jnp.int32, shape, dim)


def _parts(x, n):
    out, r = [], x
    for _ in range(n):
        h = r.astype(BF16)
        out.append(h)
        r = r - h.astype(F32)
    return out


def _sigmoid(x):
    return 1.0 / (1.0 + jnp.exp(-x))


def _silu(x):
    return x * _sigmoid(x)


def _dsilu(x):
    s = _sigmoid(x)
    return s * (1.0 + x * (1.0 - s))


def matmul(a, b, mode, name, out_dtype=F32, after=None, b_k_off=0):
    if mode == "nn":
        (m, k), (_, n) = a.shape, b.shape
    elif mode == "nt":
        (m, k), (n, _) = a.shape, b.shape
    else:
        (k, m), (_, n) = a.shape, b.shape
    tm, tn, tk = _pick(m, 1408), _pick(n, 1408), _pick(k, 1408)
    nk = k // tk
    dims = {"nn": NN, "nt": NT, "tn": TN}[mode]
    a_spec = pl.BlockSpec((tk, tm), lambda i, j, kk: (kk, i)) if mode == "tn" else pl.BlockSpec((tm, tk), lambda i, j, kk: (i, kk))
    b_spec = pl.BlockSpec((tn, tk), lambda i, j, kk: (j, kk + b_k_off)) if mode == "nt" else pl.BlockSpec((tk, tn), lambda i, j, kk: (kk, j))
    extra = [] if after is None else [after]

    def body(a_ref, b_ref, *rest):
        o_ref, acc = rest[len(extra)], rest[len(extra) + 1:]
        part = _dot(a_ref[...].astype(BF16), b_ref[...].astype(BF16), dims)
        if nk == 1:
            o_ref[...] = part.astype(o_ref.dtype)
            return
        acc_ref, kk = acc[0], pl.program_id(2)

        @pl.when(kk == 0)
        def _():
            acc_ref[...] = part

        @pl.when((kk > 0) & (kk < nk - 1))
        def _():
            acc_ref[...] += part

        @pl.when(kk == nk - 1)
        def _():
            o_ref[...] = (acc_ref[...] + part).astype(o_ref.dtype)

    return _pcall(
        body, name=name, grid=(m // tm, n // tn, nk), in_specs=[a_spec, b_spec] + [pl.BlockSpec(memory_space=pl.ANY)] * len(extra),
        out_specs=pl.BlockSpec((tm, tn), lambda i, j, kk: (i, j)),
        out_shape=jax.ShapeDtypeStruct((m, n), out_dtype),
        scratch_shapes=[pltpu.VMEM((tm, tn), F32)] if nk > 1 else [],
        compiler_params=_cparams(("parallel", "parallel", "arbitrary")),
    )(a, b, *extra)


def ew(fn, name, rows, tm, ncol, ins, outs, accs=()):
    nrow = rows // tm
    r8 = tm // 8
    in_specs, arrays = [], []
    for ent in ins:
        arr, kind, w, off = ent[:4]
        roff = ent[4] if len(ent) > 4 else 0
        if kind == "row":
            spec = pl.BlockSpec((tm, w), lambda j, i, off=off, roff=roff: (i + roff, j + off))
        elif kind == "const":
            spec = pl.BlockSpec((arr.shape[0], w), lambda j, i, off=off: (0, j + off))
        elif kind == "prev":
            spec = pl.BlockSpec((8, w), lambda j, i, off=off: (jnp.maximum(i * r8 - 1, 0), j + off))
        else:
            spec = pl.BlockSpec((8, w), lambda j, i, off=off: (jnp.minimum((i + 1) * r8, rows // 8 - 1), j + off))
        in_specs.append(spec)
        arrays.append(arr)
    out_specs = [pl.BlockSpec((tm, w), lambda j, i: (i, j)) for (_, _, w) in outs]
    out_shape = [jax.ShapeDtypeStruct((rows, c), dt) for (c, dt, _) in outs]
    out_specs += [pl.BlockSpec((1, w), lambda j, i: (0, j)) for (_, w) in accs]
    out_shape += [jax.ShapeDtypeStruct((1, c), F32) for (c, _) in accs]
    nin, nout = len(ins), len(outs)

    def body(*refs):
        i = pl.program_id(1)
        res = fn(i, nrow, *[r[...] for r in refs[:nin]])
        if not isinstance(res, (tuple, list)):
            res = (res,)
        for r, v in zip(refs[nin:nin + nout], res[:nout]):
            r[...] = v.astype(r.dtype)
        if accs:
            acc_refs = refs[nin + nout:]

            @pl.when(i == 0)
            def _():
                for r in acc_refs:
                    r[...] = jnp.zeros_like(r)

            for r, v in zip(acc_refs, res[nout:]):
                r[...] += v

    res = _pcall(
        body, name=name, grid=(ncol, nrow), in_specs=in_specs, out_specs=out_specs, out_shape=out_shape,
        compiler_params=_cparams(("parallel", "arbitrary")),
    )(*arrays)
    return res


def _shift_down(x, prev8, i):
    first = jnp.where(i == 0, 0.0, prev8[7:8, :])
    return jnp.where(_iota(x.shape, 0) == 0, first, pltpu.roll(x, 1, 0))


def _shift_up(x, next8, i, nrow):
    last = jnp.where(i == nrow - 1, 0.0, next8[0:1, :])
    return jnp.where(_iota(x.shape, 0) == x.shape[0] - 1, last, pltpu.roll(x, x.shape[0] - 1, 0))


def _colsum(x):
    return jnp.sum(x, axis=0, keepdims=True)


def _extend(x, prev8, next8, i, nrow):
    return jnp.concatenate([jnp.where(i == 0, 0.0, prev8), x, jnp.where(i == nrow - 1, 0.0, next8)], axis=0)


def _taps(xe):
    return pltpu.roll(xe, 1, 0), xe, pltpu.roll(xe, xe.shape[0] - 1, 0)


def _mid(xe):
    return xe[8:xe.shape[0] - 8]


def _conv3(w, b, taps):
    return w[0:1] * taps[0] + w[1:2] * taps[1] + w[2:3] * taps[2] + b


def _conv3_t(w, d_ext):
    t = _taps(d_ext)
    return _mid(w[0:1] * t[2] + w[1:2] * t[1] + w[2:3] * t[0])


def ffn_conv_bwd_fn(i, n, g, gp, gn, u, up_, un, wg, wu, bg, bu, da, dap, dan):
    gt, ut = _taps(_extend(g, gp, gn, i, n)), _taps(_extend(u, up_, un, i, n))
    dae = _extend(da, dap, dan, i, n)
    gate, upv = _conv3(wg, bg, gt), _conv3(wu, bu, ut)
    dg, du = dae * upv * _dsilu(gate), dae * _silu(gate)
    dgm, dum = _mid(dg), _mid(du)
    sums = [_colsum(dgm * _mid(t)) for t in gt] + [_colsum(dum * _mid(t)) for t in ut] + [_colsum(dgm), _colsum(dum)]
    return (_conv3_t(wg, dg), _conv3_t(wu, du)) + tuple(sums)


def silu_conv_bwd_fn(i, n, xv, xp, xn, w, b, da, dap, dan):
    xt = _taps(_extend(xv, xp, xn, i, n))
    du = _extend(da, dap, dan, i, n) * _dsilu(_conv3(w, b, xt))
    dum = _mid(du)
    return (_conv3_t(w, du),) + tuple(_colsum(dum * _mid(t)) for t in xt) + (_colsum(dum),)


def _rms_fwd(x, w):
    r = lax.rsqrt(jnp.mean(x * x, axis=-1, keepdims=True) + EPS)
    return x * r * w


def _rms_bwd(dy, x, w):
    r = lax.rsqrt(jnp.mean(x * x, axis=-1, keepdims=True) + EPS)
    xh = x * r
    dxh = dy * w
    dx = r * (dxh - xh * jnp.mean(dxh * xh, axis=-1, keepdims=True))
    return dx, _colsum(dy * xh)


def _rope_tables(s):
    half = ROPE_DIM // 2
    inv_freq = jnp.power(ROPE_THETA, -jnp.arange(half, dtype=F32) * 2.0 / ROPE_DIM)
    ang = jnp.arange(s, dtype=F32)[:, None] * inv_freq[None, :]
    cos, sin = jnp.cos(ang), jnp.sin(ang)
    one, zero = jnp.ones((s, HD - ROPE_DIM), F32), jnp.zeros((s, HD - ROPE_DIM), F32)
    z8 = jnp.zeros((s, half), F32)
    c = jnp.concatenate([cos, cos, one], axis=1)
    sa = jnp.concatenate([-sin, z8, zero], axis=1)
    sb = jnp.concatenate([z8, sin, zero], axis=1)
    return [jnp.tile(t, (1, 2)) for t in (c, sa, sb)]


ATTN_CHUNK = 2048


def _attn_plan(s):
    plan = []
    for d in PATTERN_DILATIONS:
        per_res = ATTN_CHUNK // d
        tq = min(128, per_res)
        plan.append((d, tq, min(s // d, tq + 2 * BAND), per_res // tq, s // d))
    return plan


def _rows(start, size, d):
    return pl.ds(start, size) if d == 1 else pl.ds(start, size, stride=d)


def _for_tiles(chunk, pat, fn):
    d, tq, win, nblk, seq_len = pat
    for b in range(nblk):
        t0 = chunk * (ATTN_CHUNK // d) + b * tq
        kloc = jnp.clip(t0 - BAND, 0, seq_len - win)
        valid = jnp.abs(kloc + _iota((tq, win), 1) - (t0 + _iota((tq, win), 0))) <= BAND
        valid = jnp.concatenate([valid, valid], axis=0)
        if d == 1:
            fn(b * tq, pl.multiple_of(kloc, BAND), valid)
        else:
            def step(r, carry, qoff=d * b * tq, koff=d * kloc, valid=valid):
                fn(qoff + r, koff + r, valid)
                return carry
            lax.fori_loop(0, d, step, 0, unroll=min(d, 8))


def _stack_heads(x, head0):
    zero = jnp.zeros_like(x)
    return jnp.concatenate([jnp.where(head0, x, zero), jnp.where(head0, zero, x)], axis=0)


def _rope_pair(x, c, sa, sb):
    n = x.shape[1]
    return x * c + pltpu.roll(x, n - 8, 1) * sa + pltpu.roll(x, 8, 1) * sb


def _rope_pair_t(dy, c, sa, sb):
    n = dy.shape[1]
    return dy * c + pltpu.roll(dy * sa, 8, 1) + pltpu.roll(dy * sb, n - 8, 1)


def _attn_specs(s):
    whole = lambda off: pl.BlockSpec((s, 128), lambda p, c: (0, off + p))
    table = pl.BlockSpec((s, 128), lambda p, c: (0, 0))
    chunk = pl.BlockSpec((ATTN_CHUNK, 128), lambda p, c: (c, p))
    return whole, table, chunk


def attn_fwd_all(proj, tabs, name):
    s = proj.shape[0]
    plan = _attn_plan(s)
    whole, table, chunk_spec = _attn_specs(s)

    def body(q_ref, k_ref, v_ref, c_ref, sa_ref, sb_ref, o_ref, lse_ref, qs, ks, acc_s, m_s, l_s):
        chunk = pl.program_id(1)

        @pl.when(chunk == 0)
        def _():
            qs[...] = _rope_pair(q_ref[...], c_ref[...], sa_ref[...], sb_ref[...]) * (HD ** -0.5)
            ks[...] = _rope_pair(k_ref[...], c_ref[...], sa_ref[...], sb_ref[...])

        base = pl.multiple_of(chunk * ATTN_CHUNK, ATTN_CHUNK)
        for pi, pat in enumerate(plan):
            d, tq, win = pat[:3]
            head0 = _iota((tq, 128), 1) < HD

            def tile(qrow, krow, valid, pi=pi, d=d, tq=tq, win=win, head0=head0):
                qv = qs[_rows(base + qrow, tq, d), :].astype(BF16)
                kw = ks[_rows(krow, win, d), :].astype(BF16)
                vw = v_ref[_rows(krow, win, d), :].astype(BF16)
                v_ones = jnp.concatenate([vw, jnp.ones_like(vw)], axis=1)
                sc = jnp.where(valid, _dot(_stack_heads(qv, head0), kw, NT), -1e30)
                mh = jnp.max(sc, axis=1, keepdims=True)
                pv = _dot(jnp.exp(sc - mh).astype(BF16), v_ones)
                acc_s[pi, _rows(qrow, tq, d), :] = jnp.where(head0, pv[:tq, :128], pv[tq:, :128])
                m_s[pi, _rows(qrow, tq, d), :] = jnp.where(head0, mh[:tq], mh[tq:])
                l_s[pi, _rows(qrow, tq, d), :] = jnp.where(head0, pv[:tq, 128:], pv[tq:, 128:])

            _for_tiles(chunk, pat, tile)
        m_all = jnp.maximum(jnp.maximum(m_s[0], m_s[1]), m_s[2])
        e = [jnp.exp(m_s[k] - m_all) for k in range(3)]
        den = e[0] * l_s[0] + e[1] * l_s[1] + e[2] * l_s[2]
        o_ref[...] = (e[0] * acc_s[0] + e[1] * acc_s[1] + e[2] * acc_s[2]) / den
        lse_ref[...] = m_all + jnp.log(den)

    stat = pltpu.VMEM((3, ATTN_CHUNK, 128), F32)
    return _pcall(
        body, name=name, grid=(D // 128, s // ATTN_CHUNK),
        in_specs=[whole(0), whole(8), whole(16), table, table, table], out_specs=[chunk_spec, chunk_spec],
        out_shape=[jax.ShapeDtypeStruct((s, D), F32)] * 2,
        scratch_shapes=[pltpu.VMEM((s, 128), F32), pltpu.VMEM((s, 128), F32), stat, stat, stat],
        compiler_params=_cparams(("parallel", "arbitrary")),
    )(proj, proj, proj, *tabs)


def attn_bwd_all(proj, tabs, dmix, o, lse, name):
    s = proj.shape[0]
    plan = _attn_plan(s)
    whole, table, chunk_spec = _attn_specs(s)
    nchunk = s // ATTN_CHUNK

    def body(q_ref, k_ref, v_ref, c_ref, sa_ref, sb_ref, do_ref, o_ref, lse_ref, dq_ref, dk_ref, dv_ref, qs, ks, aug0_s, aug1_s):
        chunk = pl.program_id(1)

        @pl.when(chunk == 0)
        def _():
            qs[...] = _rope_pair(q_ref[...], c_ref[...], sa_ref[...], sb_ref[...]) * (HD ** -0.5)
            ks[...] = _rope_pair(k_ref[...], c_ref[...], sa_ref[...], sb_ref[...])
            dk_ref[...] = jnp.zeros_like(dk_ref)
            dv_ref[...] = jnp.zeros_like(dv_ref)

        base = pl.multiple_of(chunk * ATTN_CHUNK, ATTN_CHUNK)
        prod = do_ref[...] * o_ref[...]
        first = _iota(prod.shape, 1) < HD
        delta = jnp.where(first, jnp.sum(jnp.where(first, prod, 0.0), axis=1, keepdims=True),
                          jnp.sum(jnp.where(first, 0.0, prod), axis=1, keepdims=True))
        lane = _iota(prod.shape, 1)

        def as_lanes(lse_h, delta_h):
            a, b = [u.astype(F32) for u in _parts(lse_h, 3)], [u.astype(F32) for u in _parts(delta_h, 3)]
            out = jnp.zeros_like(lse_h)
            for k, u in enumerate(a + b):
                out = jnp.where(lane == k, u, out)
            return out

        lsev = lse_ref[...]
        aug0_s[...] = as_lanes(lsev, delta)
        aug1_s[...] = as_lanes(pltpu.roll(lsev, HD, 1), pltpu.roll(delta, HD, 1))
        for pi, pat in enumerate(plan):
            d, tq, win = pat[:3]
            head0 = _iota((tq, 128), 1) < HD

            def tile(qrow, krow, valid, pi=pi, d=d, tq=tq, win=win, head0=head0):
                qv = qs[_rows(base + qrow, tq, d), :].astype(BF16)
                kw = ks[_rows(krow, win, d), :].astype(BF16)
                vw = v_ref[_rows(krow, win, d), :].astype(BF16)
                dob = do_ref[_rows(qrow, tq, d), :].astype(BF16)
                aug = jnp.concatenate([aug0_s[_rows(qrow, tq, d), :], aug1_s[_rows(qrow, tq, d), :]], axis=0).astype(BF16)
                klane = _iota((win, 128), 1)
                minus_lse = jnp.where(klane < 3, -1.0, 0.0).astype(BF16)
                minus_delta = jnp.where((klane >= 3) & (klane < 6), -1.0, 0.0).astype(BF16)
                q2, do2 = _stack_heads(qv, head0), _stack_heads(dob, head0)
                s_lse = _dot(jnp.concatenate([q2, aug], axis=1), jnp.concatenate([kw, minus_lse], axis=1), NT)
                dp_delta = _dot(jnp.concatenate([do2, aug], axis=1), jnp.concatenate([vw, minus_delta], axis=1), NT)
                p = jnp.where(valid, jnp.exp(s_lse), 0.0)
                ds = (p * dp_delta).astype(BF16)
                dq2 = _dot(ds, kw)
                dk = _dot(ds, q2, TN)
                dv = _dot(p.astype(BF16), do2, TN)
                dqv = jnp.where(head0, dq2[:tq], dq2[tq:])
                if pi == 0:
                    dq_ref[_rows(qrow, tq, d), :] = dqv
                else:
                    dq_ref[_rows(qrow, tq, d), :] += dqv
                dk_ref[_rows(krow, win, d), :] += dk
                dv_ref[_rows(krow, win, d), :] += dv

            _for_tiles(chunk, pat, tile)
        tab = [t[pl.ds(base, ATTN_CHUNK), :] for t in (c_ref, sa_ref, sb_ref)]
        dq_ref[...] = _rope_pair_t(dq_ref[...] * (HD ** -0.5), *tab)

        @pl.when(chunk == nchunk - 1)
        def _():
            dk_ref[...] = _rope_pair_t(dk_ref[...], c_ref[...], sa_ref[...], sb_ref[...])

    return _pcall(
        body, name=name, grid=(D // 128, nchunk),
        in_specs=[whole(0), whole(8), whole(16), table, table, table, chunk_spec, chunk_spec, chunk_spec],
        out_specs=[chunk_spec, whole(0), whole(0)], out_shape=[jax.ShapeDtypeStruct((s, D), F32)] * 3,
        scratch_shapes=[pltpu.VMEM((s, 128), F32), pltpu.VMEM((s, 128), F32)] + [pltpu.VMEM((ATTN_CHUNK, 128), F32)] * 2,
        compiler_params=_cparams(("parallel", "arbitrary")),
    )(proj, proj, proj, *tabs, dmix, o, lse)


def _ssd_common(x_ref, b_ref, c_ref, dt_ref, dtt_ref, a_ref, ar_ref, rev):
    ii, jj = _iota((CHUNK, CHUNK), 0), _iota((CHUNK, CHUNK), 1)
    low = jj >= ii if rev else jj <= ii
    x, dtx = x_ref[...], dt_ref[...]
    bm, cm = b_ref[...].astype(BF16), c_ref[...].astype(BF16)
    a = dtx * a_ref[...]
    arow = dtt_ref[0] * ar_ref[0]
    lowb = low.astype(BF16)
    cs = _dot(lowb, jnp.concatenate(_parts(a, 3), axis=1))
    cs = cs[:, :128] + cs[:, 128:256] + cs[:, 256:]
    csr = _dot(jnp.concatenate([p.astype(F32) for p in _parts(arow, 3)], axis=0).astype(BF16), lowb, NT)
    csr = csr[0:8] + csr[8:16] + csr[16:24]
    last = 0 if rev else CHUNK - 1
    tot = cs[last:last + 1, :]
    xdt = x * dtx
    cb = _dot(cm, bm, NT)
    lmats = [jnp.exp(jnp.where(low, cs[:, HD * h:HD * h + 1] - csr[h:h + 1, :], -1e30)) for h in range(2)]
    return dict(x=x, dtx=dtx, bm=bm, cm=cm, a=a, cs=cs, tot=tot, xdt=xdt, cb=cb, lmats=lmats, low=low, last=last)


SSD_SUB = 8


def _ssd_specs(s, rev_order):
    nblk, rows = s // (SSD_SUB * CHUNK), SSD_SUB * CHUNK
    ci = (lambda c: nblk - 1 - c) if rev_order else (lambda c: c)
    tile = lambda off, div: pl.BlockSpec((rows, 128), lambda p, c: (ci(c), off + p // div))
    common = [tile(0, 1), tile(8, 2), tile(12, 2), tile(0, 1),
              pl.BlockSpec((1, 8, rows), lambda p, c: (p, 0, ci(c))),
              pl.BlockSpec((1, 128), lambda p, c: (0, p)),
              pl.BlockSpec((1, 8, 128), lambda p, c: (p, 0, 0))]
    hs = pl.BlockSpec((1, SSD_SUB, CHUNK, 128), lambda p, c: (p, ci(c), 0, 0))
    return nblk, common, tile(0, 1), hs


def _chunk_rows(ref, j):
    return ref.at[pl.ds(j * CHUNK, CHUNK), :]


def _ssd_chunk(refs, j):
    return [_chunk_rows(r, j) for r in refs[:4]] + [refs[4].at[:, :, pl.ds(j * CHUNK, CHUNK)], refs[5], refs[6]]


def _ssd_args(xbc, t):
    return [xbc, xbc, xbc, t["dt_exp"], t["dtt"], t["a_exp"], t["a_rows"]]


def ssd_fwd(xbc, dirs, name):
    s = xbc.shape[0]
    nd = len(dirs)
    specs = [_ssd_specs(s, t["rev"]) for t in dirs]
    nck = specs[0][0]

    def one(rev, x_ref, b_ref, c_ref, dt_ref, dtt_ref, a_ref, ar_ref, y_ref, hs_ref, h_scr):
        v = _ssd_common(x_ref, b_ref, c_ref, dt_ref, dtt_ref, a_ref, ar_ref, rev)
        xdtb = v["xdt"].astype(BF16)
        yd = _dot(jnp.concatenate([v["cb"] * v["lmats"][h] for h in range(2)], axis=0).astype(BF16), xdtb)
        h_in = h_scr[...]
        hs_ref[0, 0] = h_in
        y_off = _dot(v["cm"], h_in.astype(BF16)) * jnp.exp(v["cs"])
        y_ref[...] = jnp.where(_iota((CHUNK, 128), 1) < HD, yd[:CHUNK], yd[CHUNK:]) + y_off
        decay = jnp.exp(v["tot"] - v["cs"])
        h_scr[...] = jnp.exp(v["tot"]) * h_in + _dot(v["bm"], (v["xdt"] * decay).astype(BF16), TN)

    def body(*refs):
        @pl.when(pl.program_id(1) == 0)
        def _():
            for k in range(nd):
                refs[9 * nd + k][...] = jnp.zeros((CHUNK, 128), F32)

        for k, t in enumerate(dirs):
            y_ref, hs_ref = refs[7 * nd + 2 * k:7 * nd + 2 * k + 2]
            for j in (range(SSD_SUB)[::-1] if t["rev"] else range(SSD_SUB)):
                one(t["rev"], *_ssd_chunk(refs[7 * k:7 * k + 7], j), _chunk_rows(y_ref, j), hs_ref.at[:, pl.ds(j, 1)], refs[9 * nd + k])

    res = _pcall(
        body, name=name, grid=(8, nck), in_specs=[sp for t in specs for sp in t[1]],
        out_specs=[sp for t in specs for sp in (t[2], t[3])],
        out_shape=[jax.ShapeDtypeStruct((s, D), F32), jax.ShapeDtypeStruct((8, s // CHUNK, CHUNK, 128), F32)] * nd,
        scratch_shapes=[pltpu.VMEM((CHUNK, 128), F32)] * nd, compiler_params=_cparams(("parallel", "arbitrary")),
    )(*[a for t in dirs for a in _ssd_args(xbc, t)])
    return [(res[2 * k], res[2 * k + 1]) for k in range(nd)]


def ssd_bwd(xbc, dirs, dy, name):
    s = xbc.shape[0]
    nd = len(dirs)
    specs = [_ssd_specs(s, not t["rev"]) for t in dirs]
    nck = specs[0][0]

    def one(rev, x_ref, b_ref, c_ref, dt_ref, dtt_ref, a_ref, ar_ref, hs_ref, dy_ref,
            dx_ref, ddt_ref, db_ref, dc_ref, dal_ref, dh_scr):
        v = _ssd_common(x_ref, b_ref, c_ref, dt_ref, dtt_ref, a_ref, ar_ref, rev)
        bm, cm, cs, tot, xdt = v["bm"], v["cm"], v["cs"], v["tot"], v["xdt"]
        h_in, dh = hs_ref[0, 0], dh_scr[...]
        dyv = dy_ref[...]
        dyb = dyv.astype(BF16)
        etot, decay, ecs = jnp.exp(tot), jnp.exp(tot - cs), jnp.exp(cs)
        xdtb = xdt.astype(BF16)
        xdec = xdt * decay
        dch = (dyv * ecs).astype(BF16)
        hb, dhb = h_in.astype(BF16), dh.astype(BF16)
        y_off = _dot(cm, hb) * ecs
        dc = _dot(dch, hb, NT)
        dh_y = _dot(cm, dch, TN)
        dxdec = _dot(bm, dhb)
        db = _dot(xdec.astype(BF16), dhb, NT)
        state_term = xdec * dxdec
        dtot = _colsum(dh * h_in) * etot + _colsum(state_term)
        head0 = _iota((CHUNK, 128), 1) < HD
        ii, jj = _iota((CHUNK, CHUNK), 0), _iota((CHUNK, CHUNK), 1)
        low_t = jj <= ii if rev else jj >= ii
        not_low_t = (~low_t).astype(BF16)
        g = _dot(_stack_heads(dyb, head0), xdtb, NT)
        gl = [g[:CHUNK] * v["lmats"][0], g[CHUNK:] * v["lmats"][1]]
        dcb = gl[0] + gl[1]
        dxd = _dot(jnp.concatenate([v["cb"] * v["lmats"][h] for h in range(2)], axis=1).astype(BF16), dyb, TN)
        dxd = jnp.where(head0, dxd[:CHUNK], dxd[CHUNK:])
        w = _dot(not_low_t, jnp.concatenate([gl[h] * v["cb"] for h in range(2)], axis=0).astype(BF16), NT)
        da_l = [jnp.sum(jnp.where(low_t, w[:, CHUNK * h:CHUNK * h + CHUNK], 0.0), axis=1, keepdims=True) for h in range(2)]
        dxdt = dxdec * decay + dxd
        dcbb = dcb.astype(BF16)
        dc_ref[...] = dc + _dot(dcbb, bm)
        db_ref[...] = db + _dot(dcbb, cm, TN)
        dcs = dyv * y_off - state_term + jnp.where(_iota((CHUNK, 128), 0) == v["last"], dtot, 0.0)
        lowb = v["low"].astype(BF16)
        da = _dot(lowb, jnp.concatenate(_parts(dcs, 2), axis=1), TN)
        da = da[:, :128] + da[:, 128:]
        seg = ((ii < HD) == (jj < HD)).astype(BF16)
        sums = _dot(jnp.concatenate(_parts(da, 2) + _parts(dxdt * v["x"], 2), axis=0), seg)
        da = sums[:CHUNK] + sums[CHUNK:2 * CHUNK] + jnp.where(head0, da_l[0], da_l[1])
        ddt_x = sums[2 * CHUNK:3 * CHUNK] + sums[3 * CHUNK:]
        dx_ref[...] = dxdt * v["dtx"]
        ddt_ref[...] = ddt_x + da * a_ref[...]
        dal_ref[0] += _colsum(da * v["a"])
        dh_scr[...] = etot * dh + dh_y

    def body(*refs):
        @pl.when(pl.program_id(1) == 0)
        def _():
            for k in range(nd):
                refs[14 * nd + k][...] = jnp.zeros((CHUNK, 128), F32)
                refs[9 * nd + 5 * k + 4][...] = jnp.zeros((1, 8, 128), F32)

        for k, t in enumerate(dirs):
            ins, outs = refs[9 * k:9 * k + 9], refs[9 * nd + 5 * k:9 * nd + 5 * k + 5]
            for j in (range(SSD_SUB) if t["rev"] else range(SSD_SUB)[::-1]):
                one(t["rev"], *_ssd_chunk(ins[:7], j), ins[7].at[:, pl.ds(j, 1)], _chunk_rows(ins[8], j),
                    *[_chunk_rows(r, j) for r in outs[:4]], outs[4], refs[14 * nd + k])

    acc_spec = pl.BlockSpec((1, 8, 128), lambda p, c: (p, 0, 0))
    res = _pcall(
        body, name=name, grid=(8, nck), in_specs=[sp for t in specs for sp in t[1] + [t[3], t[2]]],
        out_specs=[sp for t in specs for sp in [t[2]] * 4 + [acc_spec]],
        out_shape=([jax.ShapeDtypeStruct((s, D), F32)] * 4 + [jax.ShapeDtypeStruct((8, 8, 128), F32)]) * nd,
        scratch_shapes=[pltpu.VMEM((CHUNK, 128), F32)] * nd, compiler_params=_cparams(("parallel", "arbitrary")),
    )(*[a for t in dirs for a in _ssd_args(xbc, t) + [t["hs"], dy]])
    return [res[5 * k:5 * k + 5] for k in range(nd)]


def _group_norm_stats(g):
    r = [lax.rsqrt(jnp.mean(g[:, 256 * k:256 * k + 256] ** 2, axis=-1, keepdims=True) + EPS) for k in range(4)]
    grp = _iota(g.shape, 1) // 256
    return jnp.where(grp == 0, r[0], jnp.where(grp == 1, r[1], jnp.where(grp == 2, r[2], r[3])))


def _group_mean(t):
    m = [jnp.mean(t[:, 256 * k:256 * k + 256], axis=-1, keepdims=True) for k in range(4)]
    grp = _iota(t.shape, 1) // 256
    return jnp.where(grp == 0, m[0], jnp.where(grp == 1, m[1], jnp.where(grp == 2, m[2], m[3])))


def _mesh_pos():
    return lax.axis_index("x"), lax.axis_index("y"), lax.axis_index("c")


HBM = pl.BlockSpec(memory_space=pltpu.HBM)
SEM = pl.BlockSpec(memory_space=pltpu.SEMAPHORE)
EFFECT = pltpu.SideEffectType.DATAFLOW_SIDE_EFFECTING


def _hbm(t):
    return pltpu.with_memory_space_constraint(t, pltpu.HBM)


def _other_chips(x, y):
    return [(1 - x, y), (x, 1 - y), (1 - x, 1 - y)]


def _peer(x, y, c, m):
    return x ^ (m >> 2), y ^ ((m >> 1) & 1), c ^ (m & 1)


def gather_start(srcs_a, srcs_b, halved=()):
    srcs = [_hbm(t) for t in list(srcs_a) + list(srcs_b)]
    n, na = len(srcs), len(srcs_a)
    half = [k in halved for k in range(n)]
    lands = [_hbm(lax.empty((4,) + (t.shape[1:] if half[k] else t.shape), t.dtype)) for k, t in enumerate(srcs)]

    def body(*refs):
        src, land = refs[:n], refs[n:2 * n]
        sems = refs[2 * n:2 * n + 4]
        x, y, c = _mesh_pos()
        for k in range(n):
            for j, (px, py) in enumerate(_other_chips(x, y)):
                send, recv, idx = (sems[0], sems[1], 3 * k + j) if k < na else (sems[2], sems[3], 3 * (k - na) + j)
                pltpu.make_async_remote_copy(src_ref=src[k].at[c] if half[k] else src[k], dst_ref=land[k].at[2 * x + y], send_sem=send.at[idx],
                                             recv_sem=recv.at[idx], device_id=(px, py, c), device_id_type=MESH).start()

    sem_a, sem_b = pltpu.SemaphoreType.DMA((3 * na,)), pltpu.SemaphoreType.DMA((3 * (n - na),))
    res = _pcall(
        body, name="gather_start", in_specs=[HBM] * (2 * n), out_specs=[SEM] * 4 + [HBM] * (2 * n),
        out_shape=[sem_a, sem_a, sem_b, sem_b] + [pltpu.HBM(t.shape, t.dtype) for t in srcs + lands],
        input_output_aliases={i: 4 + i for i in range(2 * n)},
        compiler_params=pltpu.CompilerParams(has_side_effects=EFFECT),
    )(*srcs, *lands)
    thru_src, thru_land = res[4:4 + n], res[4 + n:]
    return ((res[0], res[1], thru_src[:na], thru_land[:na], half[:na]), (res[2], res[3], thru_src[na:], thru_land[na:], half[na:]))


def gather_wait(group, name, after=None):
    send, recv, srcs, lands, half = group
    n = len(srcs)

    def body(*refs):
        src, land, send_ref, recv_ref = refs[:n], refs[n:2 * n], refs[2 * n], refs[2 * n + 1]
        x, y, c = _mesh_pos()
        for j, (px, py) in enumerate(_other_chips(x, y)):
            for k in range(n):
                cp = pltpu.make_async_remote_copy(src_ref=src[k].at[0] if half[k] else src[k], dst_ref=land[k].at[2 * px + py], send_sem=send_ref.at[3 * k + j],
                                                  recv_sem=recv_ref.at[3 * k + j], device_id=(px, py, c), device_id_type=MESH)
                cp.wait_send()
                cp.wait_recv()

    extra = [] if after is None else [after]
    res = _pcall(
        body, name=name, in_specs=[HBM] * (2 * n) + [SEM, SEM] + [pl.BlockSpec(memory_space=pl.ANY)] * len(extra),
        out_specs=[HBM] * (2 * n), out_shape=[pltpu.HBM(t.shape, t.dtype) for t in list(srcs) + list(lands)],
        input_output_aliases={i: i for i in range(2 * n)}, compiler_params=pltpu.CompilerParams(has_side_effects=EFFECT),
    )(*srcs, *lands, send, recv, *extra)
    return res[:n], res[n:]


def scatter_start(pieces, smalls, name):
    srcs = [_hbm(t) for t in list(pieces) + list(smalls)]
    n, npc = len(srcs), len(pieces)
    lands = [_hbm(lax.empty((8,) + (t.shape[2:] if k < npc else t.shape), t.dtype)) for k, t in enumerate(srcs)]

    def body(*refs):
        src, land, send, recv = refs[:n], refs[n:2 * n], refs[2 * n], refs[2 * n + 1]
        token = refs[-1]
        x, y, c = _mesh_pos()
        for m in range(1, 8):
            px, py, pc = _peer(x, y, c, m)
            for k in range(n):
                s_ref = src[k].at[2 * px + py, pc] if k < npc else src[k]
                d_ref = land[k].at[m] if k < npc else land[k].at[4 * x + 2 * y + c]
                pltpu.make_async_remote_copy(src_ref=s_ref, dst_ref=d_ref, send_sem=send.at[7 * k + m - 1], recv_sem=recv.at[7 * k + m - 1],
                                             device_id=(px, py, pc), device_id_type=MESH).start()
        token[...] = jnp.zeros_like(token)

    sem = pltpu.SemaphoreType.DMA((7 * n,))
    res = _pcall(
        body, name=name, in_specs=[HBM] * (2 * n),
        out_specs=[SEM, SEM] + [HBM] * (2 * n) + [pl.BlockSpec(memory_space=pltpu.VMEM)],
        out_shape=[sem, sem] + [pltpu.HBM(t.shape, t.dtype) for t in srcs + lands] + [jax.ShapeDtypeStruct((8, 128), F32)],
        input_output_aliases={i: 2 + i for i in range(2 * n)},
        compiler_params=pltpu.CompilerParams(has_side_effects=EFFECT),
    )(*srcs, *lands)
    return (res[0], res[1], res[2:2 + n], res[2 + n:2 + 2 * n], npc), res[-1]


def scatter_wait(group, name, after=None):
    send, recv, srcs, lands, npc = group
    n = len(srcs)

    def body(*refs):
        src, land, send_ref, recv_ref = refs[:n], refs[n:2 * n], refs[2 * n], refs[2 * n + 1]
        x, y, c = _mesh_pos()
        for m in range(1, 8):
            px, py, pc = _peer(x, y, c, m)
            for k in range(n):
                s_ref = src[k].at[0, 0] if k < npc else src[k]
                d_ref = land[k].at[m] if k < npc else land[k].at[4 * px + 2 * py + pc]
                cp = pltpu.make_async_remote_copy(src_ref=s_ref, dst_ref=d_ref, send_sem=send_ref.at[7 * k + m - 1],
                                                  recv_sem=recv_ref.at[7 * k + m - 1], device_id=(px, py, pc), device_id_type=MESH)
                cp.wait_send()
                cp.wait_recv()

    extra = [] if after is None else [after]
    res = _pcall(
        body, name=name, in_specs=[HBM] * (2 * n) + [SEM, SEM] + [pl.BlockSpec(memory_space=pl.ANY)] * len(extra),
        out_specs=[HBM] * (2 * n), out_shape=[pltpu.HBM(t.shape, t.dtype) for t in list(srcs) + list(lands)],
        input_output_aliases={i: i for i in range(2 * n)}, compiler_params=pltpu.CompilerParams(has_side_effects=EFFECT),
    )(*srcs, *lands, send, recv, *extra)
    return res[:n], res[n:]


def swap_halves(pieces, name):
    n = len(pieces)
    whole = pl.BlockSpec(memory_space=pltpu.VMEM)

    def body(*refs):
        p_refs, o_refs, send_sems, recv_sems, local_sems = refs[:n], refs[n:2 * n], refs[2 * n], refs[2 * n + 1], refs[2 * n + 2]
        x, y, c = _mesh_pos()
        local = [pltpu.make_async_copy(p_refs[k], o_refs[k].at[c], local_sems.at[k]) for k in range(n)]
        for cp in local:
            cp.start()

        def copy(k, slot):
            return pltpu.make_async_remote_copy(src_ref=p_refs[k], dst_ref=o_refs[k].at[slot], send_sem=send_sems.at[k],
                                                recv_sem=recv_sems.at[k], device_id=(x, y, 1 - c), device_id_type=MESH)

        for k in range(n):
            copy(k, c).start()
        for k in range(n):
            copy(k, 1 - c).wait_recv()
        for k in range(n):
            copy(k, c).wait_send()
        for cp in local:
            cp.wait()

    return _pcall(
        body, name=name, in_specs=[whole] * n, out_specs=[whole] * n,
        out_shape=[jax.ShapeDtypeStruct((2,) + t.shape, t.dtype) for t in pieces],
        scratch_shapes=[pltpu.SemaphoreType.DMA((n,)), pltpu.SemaphoreType.DMA((n,)), pltpu.SemaphoreType.DMA((n,))],
        compiler_params=_cparams(),
    )(*pieces)


def adamw(w, g, m, v, name):
    rows, cols = w.shape
    tm = rows
    for t in (256, 352, 128, 144, 64, 32, 16, 8):
        if rows % t == 0:
            tm = t
            break

    def fn(i, nrow, wv, gv, mv, vv):
        mn = ADAM_B1 * mv + (1.0 - ADAM_B1) * gv
        vn = ADAM_B2 * vv + (1.0 - ADAM_B2) * (gv * gv)
        m_hat = mn / (1.0 - ADAM_B1 ** ADAM_STEP)
        v_hat = vn / (1.0 - ADAM_B2 ** ADAM_STEP)
        delta = -ADAM_LR * (m_hat / (jnp.sqrt(v_hat) + ADAM_EPS) + ADAM_WD * wv)
        return delta, mn, vn

    return ew(fn, name, rows, tm, 1, [(t, "row", cols, 0) for t in (w, g, m, v)], [(cols, F32, cols)] * 3)


REST = ("w_out", "w_up", "w_down")
SMALL = ("norm1_w", "ssm_conv_w", "ssm_conv_b", "a_log_f", "a_log_b", "dt_bias_f", "dt_bias_b", "d_skip",
         "ssm_norm_w", "norm2_w", "ffn_conv_w", "ffn_conv_b", "final_norm_w")
WEIGHTS = ("norm1_w", "w_in", "ssm_conv_w", "ssm_conv_b", "a_log_f", "a_log_b", "dt_bias_f", "dt_bias_b", "d_skip",
           "ssm_norm_w", "w_out", "norm2_w", "w_up", "ffn_conv_w", "ffn_conv_b", "w_down", "final_norm_w")
INPUTS = ("x",) + WEIGHTS + ("loss_target",) + tuple("m_" + n for n in WEIGHTS) + tuple("v_" + n for n in WEIGHTS)


def _flat_rows(parts, width, rows):
    flat = jnp.concatenate([p.reshape(-1) for p in parts])
    return jnp.pad(flat, (0, rows * width - flat.shape[0])).reshape(rows, width)


def _split_flat(flat, shapes):
    out, pos = [], 0
    flat = flat.reshape(-1)
    for shp in shapes:
        n = int(np.prod(shp))
        out.append(flat[pos:pos + n].reshape(shp))
        pos += n
    return out


def _col_shards(t, nshard):
    r, c = t.shape
    return t.reshape(r, nshard, c // nshard).transpose(1, 0, 2)


def _row_shards(t, nshard):
    r, c = t.shape
    return t.reshape(nshard, r // nshard, c)


def kernel(x, norm1_w, w_in, ssm_conv_w, ssm_conv_b, a_log_f, a_log_b, dt_bias_f, dt_bias_b, d_skip, ssm_norm_w, w_out, norm2_w, w_up, ffn_conv_w, ffn_conv_b, w_down, final_norm_w, loss_target, m_norm1_w, m_w_in, m_ssm_conv_w, m_ssm_conv_b, m_a_log_f, m_a_log_b, m_dt_bias_f, m_dt_bias_b, m_d_skip, m_ssm_norm_w, m_w_out, m_norm2_w, m_w_up, m_ffn_conv_w, m_ffn_conv_b, m_w_down, m_final_norm_w, v_norm1_w, v_w_in, v_ssm_conv_w, v_ssm_conv_b, v_a_log_f, v_a_log_b, v_dt_bias_f, v_dt_bias_b, v_d_skip, v_ssm_norm_w, v_w_out, v_norm2_w, v_w_up, v_ffn_conv_w, v_ffn_conv_b, v_w_down, v_final_norm_w):
    p = dict(zip(INPUTS, (x, norm1_w, w_in, ssm_conv_w, ssm_conv_b, a_log_f, a_log_b, dt_bias_f, dt_bias_b, d_skip, ssm_norm_w, w_out, norm2_w, w_up, ffn_conv_w, ffn_conv_b, w_down, final_norm_w, loss_target, m_norm1_w, m_w_in, m_ssm_conv_w, m_ssm_conv_b, m_a_log_f, m_a_log_b, m_dt_bias_f, m_dt_bias_b, m_d_skip, m_ssm_norm_w, m_w_out, m_norm2_w, m_w_up, m_ffn_conv_w, m_ffn_conv_b, m_w_down, m_final_norm_w, v_norm1_w, v_w_in, v_ssm_conv_w, v_ssm_conv_b, v_a_log_f, v_a_log_b, v_dt_bias_f, v_dt_bias_b, v_d_skip, v_ssm_norm_w, v_w_out, v_norm2_w, v_w_up, v_ffn_conv_w, v_ffn_conv_b, v_w_down, v_final_norm_w)))
    x = p["x"][0]
    tgt = p["loss_target"][0]
    s = x.shape[0]
    chip = 2 * lax.axis_index("x") + lax.axis_index("y")

    own_slot = lambda land, mine, slot: lax.dynamic_update_slice_in_dim(land, mine[None], slot, axis=0)
    core = lax.axis_index("c")
    src_in = p["w_in"][0].astype(BF16).reshape(2, D // 2, -1)
    src_rest = [p[n][0].astype(BF16) for n in REST]
    small_w = _flat_rows([p["ssm_conv_w"][0], p["ffn_conv_w"][0]], 128, 48)
    gather_in, gather_rest = gather_start([src_in, small_w], src_rest, halved=(0,))
    (src_in, small_w), (wg_in, sg) = gather_wait(gather_in, "gather_wait_in")
    wg_in = own_slot(wg_in, lax.dynamic_index_in_dim(src_in, core, 0, keepdims=False), chip)
    wg_in, = swap_halves([wg_in], "swap_w_in_rows")
    w_in = wg_in.transpose(0, 2, 1, 3).reshape(D, -1)
    sg = own_slot(sg, small_w, chip)
    n_in = w_in.shape[1]
    n_main = 6 * D
    w_main = w_in[:, :n_main]
    w_dt = jnp.pad(w_in[:, n_main:], ((0, 0), (0, 128 - (n_in - n_main))))
    sgf = sg.reshape(4, -1)
    n_sc, n_fc = p["ssm_conv_w"].shape[1], p["ffn_conv_w"].shape[1]
    ssm_cw = sgf[:, :n_sc * 3].reshape(-1, 3).T
    ffn_cw = sgf[:, n_sc * 3:(n_sc + n_fc) * 3].reshape(-1, 3).T
    ssm_cb, ffn_cb = p["ssm_conv_b"], p["ffn_conv_b"]
    n1w, n2w, snw, fnw = p["norm1_w"], p["norm2_w"], p["ssm_norm_w"], p["final_norm_w"].reshape(1, D)

    h1, = ew(lambda i, n, xv, w: _rms_fwd(xv, w), "rms1", s, 256, 1,
             [(x, "row", D, 0), (n1w, "const", D, 0)], [(D, BF16, D)])
    proj = matmul(h1, w_main, "nn", "in_proj")
    proj_dt = matmul(h1, w_dt, "nn", "in_proj_dt")
    tabs = _rope_tables(s)
    attn, lse = attn_fwd_all(proj, tabs, "attn_fwd")

    def conv_silu_fn(i, n, xv, xp, xn, w, b):
        return _silu(w[0:1] * _shift_down(xv, xp, i) + w[1:2] * xv + w[2:3] * _shift_up(xv, xn, i, n) + b)

    xbc_act, = ew(conv_silu_fn, "ssm_conv", s, 256, 2,
                  [(proj, "row", D, 4), (proj, "prev", D, 4), (proj, "next", D, 4),
                   (ssm_cw, "const", D, 0), (ssm_cb, "const", D, 0)], [(2 * D, F32, D)])
    dt_bias = jnp.pad(jnp.concatenate([p["dt_bias_f"], p["dt_bias_b"]], axis=1), ((0, 0), (0, 96)))

    lanes_of = np.arange(128)[:, None] == np.arange(D)[None, :] // HD
    spread = [jnp.asarray(np.roll(lanes_of, 16 * k, axis=0), BF16) for k in range(2)]

    def softplus_fn(i, n, r, b, ef, eb):
        t = r + b
        dtv = jnp.maximum(t, 0.0) + jnp.log(1.0 + jnp.exp(-jnp.abs(t)))
        parts = _parts(dtv, 3)
        return dtv, sum(_dot(q, ef) for q in parts), sum(_dot(q, eb) for q in parts)

    dt, dt_exp_f, dt_exp_b = ew(softplus_fn, "dt_softplus", s, 512, 1,
                                [(proj_dt, "row", 128, 0), (dt_bias, "const", 128, 0), (spread[0], "const", D, 0), (spread[1], "const", D, 0)],
                                [(128, F32, 128), (D, F32, D), (D, F32, D)])
    d_exp = jnp.repeat(p["d_skip"], HD, axis=1)
    ssd = []
    for k, (a_log, rev) in enumerate(((p["a_log_f"], False), (p["a_log_b"], True))):
        dt_k = dt[:, 16 * k:16 * k + 16]
        a_head = -jnp.exp(a_log)
        dt_exp = (dt_exp_f, dt_exp_b)[k]
        dtt = jnp.pad(dt_k.T.reshape(8, 2, s), ((0, 0), (0, 6), (0, 0)))
        a_exp = jnp.repeat(a_head, HD, axis=1)
        a_rows = jnp.broadcast_to(jnp.pad(a_head.reshape(8, 2), ((0, 0), (0, 6)))[:, :, None], (8, 8, 128))
        ssd.append(dict(dt_exp=dt_exp, dtt=dtt, a_exp=a_exp, a_rows=a_rows, rev=rev))
    for t, (y_k, hs_k) in zip(ssd, ssd_fwd(xbc_act, ssd, "ssd_fwd")):
        t["y"], t["hs"] = y_k, hs_k

    def gate_fn(i, n, yf, yb, xs, z, dsk, w):
        g = (yf + yb + dsk * xs) * _silu(z)
        return g * _group_norm_stats(g) * w

    ssm_out, = ew(gate_fn, "ssm_gate_norm", s, 256, 1,
                  [(ssd[0]["y"], "row", D, 0), (ssd[1]["y"], "row", D, 0), (xbc_act, "row", D, 0), (proj, "row", D, 3),
                   (d_exp, "const", D, 0), (snw, "const", D, 0)], [(D, F32, D)])
    mix = jnp.concatenate([attn, ssm_out], axis=1).astype(BF16)
    src_rest, wg_rest = gather_wait(gather_rest, "gather_wait_rest", after=mix)
    wg_rest = [own_slot(land, mine, chip) for land, mine in zip(wg_rest, src_rest)]
    w_out = wg_rest[0].reshape(-1, D)
    w_up = wg_rest[1].transpose(1, 0, 2).reshape(D, -1)
    w_down = wg_rest[2].reshape(-1, D)
    mix_w = matmul(mix, w_out, "nn", "out_proj")

    def res_rms_fn(i, n, xv, mw, w):
        x1v = xv + mw
        return x1v, _rms_fwd(x1v, w)

    x1, h2 = ew(res_rms_fn, "res_rms2", s, 256, 1, [(x, "row", D, 0), (mix_w, "row", D, 0), (n2w, "const", D, 0)],
                [(D, F32, D), (D, BF16, D)])
    hw = matmul(h2, w_up, "nn", "ffn_up")
    fw = D_FF // 2
    nfb = D_FF // fw
    ffn_conv_ins = [(hw, "row", fw, 0), (hw, "prev", fw, 0), (hw, "next", fw, 0),
                    (hw, "row", fw, nfb), (hw, "prev", fw, nfb), (hw, "next", fw, nfb),
                    (ffn_cw, "const", fw, 0), (ffn_cw, "const", fw, nfb), (ffn_cb, "const", fw, 0), (ffn_cb, "const", fw, nfb)]

    def ffn_conv(i, n, g, gp, gn, u, up_, un, wg_, wu, bg, bu):
        gs = (_shift_down(g, gp, i), g, _shift_up(g, gn, i, n))
        us = (_shift_down(u, up_, i), u, _shift_up(u, un, i, n))
        gate = wg_[0:1] * gs[0] + wg_[1:2] * gs[1] + wg_[2:3] * gs[2] + bg
        upv = wu[0:1] * us[0] + wu[1:2] * us[1] + wu[2:3] * us[2] + bu
        return gate, upv, gs, us

    def glu_fn(i, n, *blocks):
        gate, upv, _, _ = ffn_conv(i, n, *blocks)
        return _silu(gate) * upv

    act, = ew(glu_fn, "ffn_conv_glu", s, 256, nfb, ffn_conv_ins, [(D_FF, BF16, fw)])
    ffn = matmul(act, w_down, "nn", "ffn_down")

    def head_fn(i, n, x1v, fv, tv, w):
        x2 = x1v + fv
        r = lax.rsqrt(jnp.mean(x2 * x2, axis=-1, keepdims=True) + EPS)
        xh = x2 * r
        diff = xh * w - tv
        loss = 0.5 * jnp.sum(jnp.mean(diff * diff, axis=-1, keepdims=True), axis=0, keepdims=True)
        dout = diff * (1.0 / D)
        dxh = dout * w
        dx2 = r * (dxh - xh * jnp.mean(dxh * xh, axis=-1, keepdims=True))
        return dx2, jnp.broadcast_to(loss, (1, 128)), _colsum(dout * xh)

    dx2, loss_acc, g_fnw = ew(head_fn, "loss_head", s, 256, 1,
                              [(x1, "row", D, 0), (ffn, "row", D, 0), (tgt, "row", D, 0), (fnw, "const", D, 0)],
                              [(D, F32, D)], [(128, 128), (D, D)])
    loss = lax.psum(loss_acc[0, 0], ("x", "y", "c"))

    g_w_down = matmul(act, dx2, "tn", "d_w_down")
    dact = matmul(dx2, w_down, "nt", "d_act")

    res = ew(ffn_conv_bwd_fn, "ffn_conv_glu_bwd", s, 256, nfb,
             ffn_conv_ins + [(dact, "row", fw, 0), (dact, "prev", fw, 0), (dact, "next", fw, 0)],
             [(D_FF, F32, fw)] * 2, [(D_FF, fw)] * 8)
    dhw_g, dhw_u = res[0], res[1]
    g_ffn_cw = jnp.concatenate([jnp.concatenate(res[2:5], axis=0), jnp.concatenate(res[5:8], axis=0)], axis=1).T
    g_ffn_cb = jnp.concatenate([res[8], res[9]], axis=1)

    g_w_up = jnp.concatenate([matmul(h2, dhw_g, "tn", "d_w_up_gate"), matmul(h2, dhw_u, "tn", "d_w_up_up")], axis=1)
    dh2_a = matmul(dhw_g, w_up, "nt", "d_h2_gate")
    dh2_b = matmul(dhw_u, w_up, "nt", "d_h2_up", b_k_off=D_FF // _pick(D_FF, 1408))

    def res_rms_bwd_fn(i, n, dres, da, db, xin, w):
        dx, dw = _rms_bwd(da + db, xin, w)
        return dres + dx, dw

    dx1, g_n2w = ew(res_rms_bwd_fn, "res_rms2_bwd", s, 256, 1,
                    [(dx2, "row", D, 0), (dh2_a, "row", D, 0), (dh2_b, "row", D, 0), (x1, "row", D, 0), (n2w, "const", D, 0)],
                    [(D, F32, D)], [(D, D)])

    g_w_out = matmul(mix, dx1, "tn", "d_w_out")
    to_pieces = lambda t: t.astype(BF16).reshape(4, 2, t.shape[1] // 2, t.shape[2])
    shards_rest = [_row_shards(g_w_out, 4), _col_shards(g_w_up, 4), _row_shards(g_w_down, 4)]
    scatter_rest, token = scatter_start([to_pieces(t) for t in shards_rest], [], "scatter_start_rest")
    dmix = matmul(dx1, w_out, "nt", "d_mix", after=token)
    ii, jj = np.arange(D)[:, None] // HD, np.arange(D)[None, :] // HD
    seg = jnp.asarray(ii == jj, BF16)

    def gate_bwd_fn(i, n, dout, yf, yb, xs, z, dsk, w, segm):
        yt = yf + yb + dsk * xs
        sz = _silu(z)
        g = yt * sz
        r = _group_norm_stats(g)
        gh = g * r
        dn = dout * w
        dg = r * (dn - gh * _group_mean(dn * gh))
        dy = dg * sz
        dsk_lane = jnp.broadcast_to(_colsum(dy * xs), (8, D))
        return dy, dg * yt * _dsilu(z), _colsum(dout * gh), sum(_dot(q, segm) for q in _parts(dsk_lane, 2))[0:1]

    dy, dz, g_snw, g_dskip_l = ew(
        gate_bwd_fn, "ssm_gate_norm_bwd", s, 256, 1,
        [(dmix, "row", D, 1), (ssd[0]["y"], "row", D, 0), (ssd[1]["y"], "row", D, 0), (xbc_act, "row", D, 0),
         (proj, "row", D, 3), (d_exp, "const", D, 0), (snw, "const", D, 0), (seg, "const", D, 0)],
        [(D, F32, D)] * 2, [(D, D)] * 2)
    sb = ssd_bwd(xbc_act, ssd, dy, "ssd_bwd")

    def dxbc_act_fn(i, n, dxf, dxb, dyv, dsk, dbf, dbb, dcf, dcb_):
        db, dc = dbf + dbb, dcf + dcb_
        db = [db[:, 256 * g:256 * g + 128] + db[:, 256 * g + 128:256 * g + 256] for g in range(4)]
        dc = [dc[:, 256 * g:256 * g + 128] + dc[:, 256 * g + 128:256 * g + 256] for g in range(4)]
        return jnp.concatenate([dxf + dxb + dyv * dsk] + db + dc, axis=1)

    dxbc_act, = ew(dxbc_act_fn, "d_xbc_act", s, 256, 1,
                   [(sb[0][0], "row", D, 0), (sb[1][0], "row", D, 0), (dy, "row", D, 0), (d_exp, "const", D, 0),
                    (sb[0][2], "row", D, 0), (sb[1][2], "row", D, 0), (sb[0][3], "row", D, 0), (sb[1][3], "row", D, 0)],
                   [(2 * D, F32, 2 * D)])

    res = ew(silu_conv_bwd_fn, "ssm_conv_bwd", s, 256, 2,
             [(proj, "row", D, 4), (proj, "prev", D, 4), (proj, "next", D, 4), (ssm_cw, "const", D, 0), (ssm_cb, "const", D, 0),
              (dxbc_act, "row", D, 0), (dxbc_act, "prev", D, 0), (dxbc_act, "next", D, 0)], [(2 * D, F32, D)], [(2 * D, D)] * 4)
    dxbc = res[0]
    g_ssm_cw = jnp.concatenate(res[1:4], axis=0).T
    g_ssm_cb = res[4]
    ddt = jnp.pad(jnp.concatenate([sb[0][1][:, ::HD], sb[1][1][:, ::HD]], axis=1), ((0, 0), (0, 96)))

    def dt_bwd_fn(i, n, dd, r, b):
        dr = dd * _sigmoid(r + b)
        return dr, _colsum(dr)

    dproj_dt, g_dt_bias = ew(dt_bwd_fn, "dt_softplus_bwd", s, 512, 1,
                             [(ddt, "row", 128, 0), (proj_dt, "row", 128, 0), (dt_bias, "const", 128, 0)],
                             [(128, F32, 128)], [(128, 128)])
    g_a_log = [t[4][:, 0, ::HD].reshape(1, 16) for t in sb]

    dq, dk, dv = attn_bwd_all(proj, tabs, dmix, attn, lse, "attn_bwd")

    dproj = jnp.concatenate([dq, dk, dv, dz, dxbc], axis=1).astype(BF16)
    g_w_in = jnp.concatenate([matmul(h1, dproj, "tn", "d_w_in"), matmul(h1, dproj_dt, "tn", "d_w_in_dt")[:, :n_in - n_main]], axis=1)
    scatter_in, token = scatter_start([to_pieces(_col_shards(g_w_in, 4))], [], "scatter_start_in")
    dh1_a = matmul(dproj, w_main, "nt", "d_h1", after=token)
    dh1_b = matmul(dproj_dt, w_dt, "nt", "d_h1_dt")
    grad_x, g_n1w = ew(res_rms_bwd_fn, "rms1_bwd", s, 256, 1,
                       [(dx1, "row", D, 0), (dh1_a, "row", D, 0), (dh1_b, "row", D, 0), (x, "row", D, 0), (n1w, "const", D, 0)],
                       [(D, F32, D)], [(D, D)])

    small_g = {"norm1_w": g_n1w, "ssm_conv_w": g_ssm_cw, "ssm_conv_b": g_ssm_cb, "a_log_f": g_a_log[0], "a_log_b": g_a_log[1],
               "dt_bias_f": g_dt_bias[:, :16], "dt_bias_b": g_dt_bias[:, 16:32], "d_skip": g_dskip_l[:, ::HD],
               "ssm_norm_w": g_snw, "norm2_w": g_n2w, "ffn_conv_w": g_ffn_cw, "ffn_conv_b": g_ffn_cb, "final_norm_w": g_fnw}
    small_shapes = [small_g[n].shape for n in SMALL]
    scatter_small, token = scatter_start([], [_flat_rows([small_g[n] for n in SMALL], 128, SMALL_ROWS)], "scatter_start_small")

    def sum8_fn(i, n, *v):
        t = v[0].astype(F32)
        for u in v[1:]:
            t = t + u.astype(F32)
        return t

    def sum_pieces(sent, got, name):
        rows, w = got.shape[1:]
        tm = 256 if rows % 256 == 0 else rows
        mine = lax.dynamic_slice(sent, (chip, core, 0, 0), (1, 1, rows, w)).reshape(rows, w)
        ins = [(mine, "row", w, 0)] + [(got.reshape(8 * rows, w), "row", w, 0, k * (rows // tm)) for k in range(1, 8)]
        return ew(sum8_fn, name, rows, tm, 1, ins, [(w, F32, w)])[0]

    grads, delta, new_m, new_v = {}, {}, {}, {}

    def finish(names, sent, got, tag):
        summed = swap_halves([sum_pieces(a, b, "sum_pieces_" + n) for a, b, n in zip(sent, got, names)], "swap_halves_" + tag)
        for n, t in zip(names, summed):
            shp = p[n].shape
            grads[n] = t.reshape(shp)
            r = [u.reshape(shp[1:]) for u in (p[n], grads[n], p["m_" + n], p["v_" + n])]
            delta[n], new_m[n], new_v[n] = [u.reshape(shp) for u in adamw(*r, "adamw_" + n)]

    finish(REST, *scatter_wait(scatter_rest, "scatter_wait_rest", after=token), "rest")
    finish(("w_in",), *scatter_wait(scatter_in, "scatter_wait_in", after=new_v[REST[-1]]), "w_in")
    (sent_small,), (got_small,) = scatter_wait(scatter_small, "scatter_wait_small", after=new_v["w_in"])
    got_small = own_slot(got_small, sent_small, 2 * chip + core)
    small_sum, = ew(sum8_fn, "sum_small", SMALL_ROWS, SMALL_ROWS, 1,
                    [(got_small.reshape(8 * SMALL_ROWS, 128), "row", 128, 0, k) for k in range(8)], [(128, F32, 128)])
    for n, g in zip(SMALL, _split_flat(small_sum, small_shapes)):
        if n in ("ssm_conv_w", "ffn_conv_w"):
            rows = p[n].shape[1]
            g = lax.dynamic_slice_in_dim(g, chip * rows, rows, axis=0)
        grads[n] = g.reshape(p[n].shape)

    shapes = [p[n].shape for n in SMALL]
    total = sum(int(np.prod(sh)) for sh in shapes)
    rows = -(-total // 1024) * 8
    packs = [_flat_rows([t[n] for n in SMALL], 128, rows)
             for t in (p, grads, {n: p["m_" + n] for n in SMALL}, {n: p["v_" + n] for n in SMALL})]
    for dst, t in zip((delta, new_m, new_v), adamw(*packs, "adamw_small")):
        for n, u in zip(SMALL, _split_flat(t, shapes)):
            dst[n] = u
    return (loss, grad_x[None], *[grads[n] for n in WEIGHTS], *[delta[n] for n in WEIGHTS],
            *[new_m[n] for n in WEIGHTS], *[new_v[n] for n in WEIGHTS])
```

```python
import numpy as np
import jax
import jax.numpy as jnp
from jax import lax
from jax.experimental import pallas as pl
from jax.experimental.pallas import tpu as pltpu

F32, BF16 = jnp.float32, jnp.bfloat16
MESH = pl.DeviceIdType.MESH
V7X_VMEM_LIMIT = 56 * 1024 * 1024

D = 1024
HD = 64
EPS = 1e-6
CHUNK = 128
D_FF = 2816
ROPE_DIM = 16
ROPE_THETA = 500000.0
PATTERN_DILATIONS = (1, 4, 16)
BAND = 64
SMALL_ROWS = 280
ADAM_LR, ADAM_B1, ADAM_B2, ADAM_EPS, ADAM_WD, ADAM_STEP = 0.001, 0.9, 0.999, 1e-08, 0.01, 10

NN = (((1,), (0,)), ((), ()))
NT = (((1,), (1,)), ((), ()))
TN = (((0,), (0,)), ((), ()))


def _pcall(body, **kw):
    return pl.pallas_call(body, **kw)


def _cparams(sem=None):
    return pltpu.CompilerParams(dimension_semantics=sem, vmem_limit_bytes=V7X_VMEM_LIMIT)


def _dot(a, b, dims=NN):
    return lax.dot_general(a, b, dims, preferred_element_type=F32)


def _pick(n, cap):
    if n <= cap:
        return n
    best = 0
    for t in range(128, cap + 1, 128):
        if n % t == 0:
            best = t
    assert best, (n, cap)
    return best


def _iota(shape, dim):
    return lax.broadcasted_iota(jnp.int32, shape, dim)


def _parts(x, n):
    out, r = [], x
    for _ in range(n):
        h = r.astype(BF16)
        out.append(h)
        r = r - h.astype(F32)
    return out


def _sigmoid(x):
    return 1.0 / (1.0 + jnp.exp(-x))


def _silu(x):
    return x * _sigmoid(x)


def _dsilu(x):
    s = _sigmoid(x)
    return s * (1.0 + x * (1.0 - s))


def matmul(a, b, mode, name, out_dtype=F32, after=None, b_k_off=0, n_cols=None):
    if mode == "nn":
        (m, k), (_, n) = a.shape, (b.shape[0], n_cols or b.shape[1])
    elif mode == "nt":
        (m, k), (n, _) = a.shape, b.shape
    else:
        (k, m), (_, n) = a.shape, b.shape
    tm, tn, tk = _pick(m, 1408), _pick(n, 1408), _pick(k, 1408)
    nk = k // tk
    dims = {"nn": NN, "nt": NT, "tn": TN}[mode]
    a_spec = pl.BlockSpec((tk, tm), lambda i, j, kk: (kk, i)) if mode == "tn" else pl.BlockSpec((tm, tk), lambda i, j, kk: (i, kk))
    b_spec = pl.BlockSpec((tn, tk), lambda i, j, kk: (j, kk + b_k_off)) if mode == "nt" else pl.BlockSpec((tk, tn), lambda i, j, kk: (kk, j))
    extra = [] if after is None else [after]

    def body(a_ref, b_ref, *rest):
        o_ref, acc = rest[len(extra)], rest[len(extra) + 1:]
        part = _dot(a_ref[...].astype(BF16), b_ref[...].astype(BF16), dims)
        if nk == 1:
            o_ref[...] = part.astype(o_ref.dtype)
            return
        acc_ref, kk = acc[0], pl.program_id(2)

        @pl.when(kk == 0)
        def _():
            acc_ref[...] = part

        @pl.when((kk > 0) & (kk < nk - 1))
        def _():
            acc_ref[...] += part

        @pl.when(kk == nk - 1)
        def _():
            o_ref[...] = (acc_ref[...] + part).astype(o_ref.dtype)

    return _pcall(
        body, name=name, grid=(m // tm, n // tn, nk), in_specs=[a_spec, b_spec] + [pl.BlockSpec(memory_space=pl.ANY)] * len(extra),
        out_specs=pl.BlockSpec((tm, tn), lambda i, j, kk: (i, j)),
        out_shape=jax.ShapeDtypeStruct((m, n), out_dtype),
        scratch_shapes=[pltpu.VMEM((tm, tn), F32)] if nk > 1 else [],
        compiler_params=_cparams(("parallel", "parallel", "arbitrary")),
    )(a, b, *extra)


def ew(fn, name, rows, tm, ncol, ins, outs, accs=()):
    nrow = rows // tm
    r8 = tm // 8
    in_specs, arrays = [], []
    for ent in ins:
        arr, kind, w, off = ent[:4]
        roff = ent[4] if len(ent) > 4 else 0
        if kind == "row":
            spec = pl.BlockSpec((tm, w), lambda j, i, off=off, roff=roff: (i + roff, j + off))
        elif kind == "const":
            spec = pl.BlockSpec((arr.shape[0], w), lambda j, i, off=off: (0, j + off))
        elif kind == "prev":
            spec = pl.BlockSpec((8, w), lambda j, i, off=off: (jnp.maximum(i * r8 - 1, 0), j + off))
        else:
            spec = pl.BlockSpec((8, w), lambda j, i, off=off: (jnp.minimum((i + 1) * r8, rows // 8 - 1), j + off))
        in_specs.append(spec)
        arrays.append(arr)
    out_specs = [pl.BlockSpec((tm, w), lambda j, i: (i, j)) for (_, _, w) in outs]
    out_shape = [jax.ShapeDtypeStruct((rows, c), dt) for (c, dt, _) in outs]
    out_specs += [pl.BlockSpec((1, w), lambda j, i: (0, j)) for (_, w) in accs]
    out_shape += [jax.ShapeDtypeStruct((1, c), F32) for (c, _) in accs]
    nin, nout = len(ins), len(outs)

    def body(*refs):
        i = pl.program_id(1)
        res = fn(i, nrow, *[r[...] for r in refs[:nin]])
        if not isinstance(res, (tuple, list)):
            res = (res,)
        for r, v in zip(refs[nin:nin + nout], res[:nout]):
            r[...] = v.astype(r.dtype)
        if accs:
            acc_refs = refs[nin + nout:]

            @pl.when(i == 0)
            def _():
                for r in acc_refs:
                    r[...] = jnp.zeros_like(r)

            for r, v in zip(acc_refs, res[nout:]):
                r[...] += v

    res = _pcall(
        body, name=name, grid=(ncol, nrow), in_specs=in_specs, out_specs=out_specs, out_shape=out_shape,
        compiler_params=_cparams(("parallel", "arbitrary")),
    )(*arrays)
    return res


def _shift_down(x, prev8, i):
    first = jnp.where(i == 0, 0.0, prev8[7:8, :])
    return jnp.where(_iota(x.shape, 0) == 0, first, pltpu.roll(x, 1, 0))


def _shift_up(x, next8, i, nrow):
    last = jnp.where(i == nrow - 1, 0.0, next8[0:1, :])
    return jnp.where(_iota(x.shape, 0) == x.shape[0] - 1, last, pltpu.roll(x, x.shape[0] - 1, 0))


def _colsum(x):
    return jnp.sum(x, axis=0, keepdims=True)


def _extend(x, prev8, next8, i, nrow):
    return jnp.concatenate([jnp.where(i == 0, 0.0, prev8), x, jnp.where(i == nrow - 1, 0.0, next8)], axis=0)


def _taps(xe):
    return pltpu.roll(xe, 1, 0), xe, pltpu.roll(xe, xe.shape[0] - 1, 0)


def _mid(xe):
    return xe[8:xe.shape[0] - 8]


def _conv3(w, b, taps):
    return w[0:1] * taps[0] + w[1:2] * taps[1] + w[2:3] * taps[2] + b


def _conv3_t(w, d_ext):
    t = _taps(d_ext)
    return _mid(w[0:1] * t[2] + w[1:2] * t[1] + w[2:3] * t[0])


def ffn_conv_bwd_fn(i, n, g, gp, gn, u, up_, un, wg, wu, bg, bu, da, dap, dan):
    gt, ut = _taps(_extend(g, gp, gn, i, n)), _taps(_extend(u, up_, un, i, n))
    dae = _extend(da, dap, dan, i, n)
    gate, upv = _conv3(wg, bg, gt), _conv3(wu, bu, ut)
    dg, du = dae * upv * _dsilu(gate), dae * _silu(gate)
    dgm, dum = _mid(dg), _mid(du)
    sums = [_colsum(dgm * _mid(t)) for t in gt] + [_colsum(dum * _mid(t)) for t in ut] + [_colsum(dgm), _colsum(dum)]
    return (_conv3_t(wg, dg), _conv3_t(wu, du)) + tuple(sums)


def silu_conv_bwd_fn(i, n, xv, xp, xn, w, b, da, dap, dan):
    xt = _taps(_extend(xv, xp, xn, i, n))
    du = _extend(da, dap, dan, i, n) * _dsilu(_conv3(w, b, xt))
    dum = _mid(du)
    return (_conv3_t(w, du),) + tuple(_colsum(dum * _mid(t)) for t in xt) + (_colsum(dum),)


def _rms_fwd(x, w):
    r = lax.rsqrt(jnp.mean(x * x, axis=-1, keepdims=True) + EPS)
    return x * r * w


def _rms_bwd(dy, x, w):
    r = lax.rsqrt(jnp.mean(x * x, axis=-1, keepdims=True) + EPS)
    xh = x * r
    dxh = dy * w
    dx = r * (dxh - xh * jnp.mean(dxh * xh, axis=-1, keepdims=True))
    return dx, _colsum(dy * xh)


def _rope_tables(s):
    half = ROPE_DIM // 2
    inv_freq = jnp.power(ROPE_THETA, -jnp.arange(half, dtype=F32) * 2.0 / ROPE_DIM)
    ang = jnp.arange(s, dtype=F32)[:, None] * inv_freq[None, :]
    cos, sin = jnp.cos(ang), jnp.sin(ang)
    one, zero = jnp.ones((s, HD - ROPE_DIM), F32), jnp.zeros((s, HD - ROPE_DIM), F32)
    z8 = jnp.zeros((s, half), F32)
    c = jnp.concatenate([cos, cos, one], axis=1)
    sa = jnp.concatenate([-sin, z8, zero], axis=1)
    sb = jnp.concatenate([z8, sin, zero], axis=1)
    return [jnp.tile(t, (1, 2)) for t in (c, sa, sb)]


ATTN_CHUNK = 2048


def _attn_plan(s):
    plan = []
    for d in PATTERN_DILATIONS:
        per_res = ATTN_CHUNK // d
        tq = min(128, per_res)
        plan.append((d, tq, min(s // d, tq + 2 * BAND), per_res // tq, s // d))
    return plan


def _rows(start, size, d):
    return pl.ds(start, size) if d == 1 else pl.ds(start, size, stride=d)


def _for_tiles(chunk, pat, fn):
    d, tq, win, nblk, seq_len = pat
    for b in range(nblk):
        t0 = chunk * (ATTN_CHUNK // d) + b * tq
        kloc = jnp.clip(t0 - BAND, 0, seq_len - win)
        valid = jnp.abs(kloc + _iota((tq, win), 1) - (t0 + _iota((tq, win), 0))) <= BAND
        valid = jnp.concatenate([valid, valid], axis=0)
        if d == 1:
            fn(b * tq, pl.multiple_of(kloc, BAND), valid)
        else:
            def step(r, carry, qoff=d * b * tq, koff=d * kloc, valid=valid):
                fn(qoff + r, koff + r, valid)
                return carry
            lax.fori_loop(0, d, step, 0, unroll=min(d, 8))


def _stack_heads(x, head0):
    zero = jnp.zeros_like(x)
    return jnp.concatenate([jnp.where(head0, x, zero), jnp.where(head0, zero, x)], axis=0)


def _rope_pair(x, c, sa, sb):
    n = x.shape[1]
    return x * c + pltpu.roll(x, n - 8, 1) * sa + pltpu.roll(x, 8, 1) * sb


def _rope_pair_t(dy, c, sa, sb):
    n = dy.shape[1]
    return dy * c + pltpu.roll(dy * sa, 8, 1) + pltpu.roll(dy * sb, n - 8, 1)


def _attn_specs(s):
    whole = lambda off: pl.BlockSpec((s, 128), lambda p, c: (0, off + p))
    table = pl.BlockSpec((s, 128), lambda p, c: (0, 0))
    chunk = pl.BlockSpec((ATTN_CHUNK, 128), lambda p, c: (c, p))
    return whole, table, chunk


def attn_fwd_all(proj, tabs, name):
    s = proj.shape[0]
    plan = _attn_plan(s)
    whole, table, chunk_spec = _attn_specs(s)

    def body(q_ref, k_ref, v_ref, c_ref, sa_ref, sb_ref, o_ref, lse_ref, qs, ks, acc_s, m_s, l_s):
        chunk = pl.program_id(1)

        @pl.when(chunk == 0)
        def _():
            qs[...] = _rope_pair(q_ref[...], c_ref[...], sa_ref[...], sb_ref[...]) * (HD ** -0.5)
            ks[...] = _rope_pair(k_ref[...], c_ref[...], sa_ref[...], sb_ref[...])

        base = pl.multiple_of(chunk * ATTN_CHUNK, ATTN_CHUNK)
        for pi, pat in enumerate(plan):
            d, tq, win = pat[:3]
            head0 = _iota((tq, 128), 1) < HD

            def tile(qrow, krow, valid, pi=pi, d=d, tq=tq, win=win, head0=head0):
                qv = qs[_rows(base + qrow, tq, d), :].astype(BF16)
                kw = ks[_rows(krow, win, d), :].astype(BF16)
                vw = v_ref[_rows(krow, win, d), :].astype(BF16)
                v_ones = jnp.concatenate([vw, jnp.ones_like(vw)], axis=1)
                sc = jnp.where(valid, _dot(_stack_heads(qv, head0), kw, NT), -1e30)
                mh = jnp.max(sc, axis=1, keepdims=True)
                pv = _dot(jnp.exp(sc - mh).astype(BF16), v_ones)
                acc_s[pi, _rows(qrow, tq, d), :] = jnp.where(head0, pv[:tq, :128], pv[tq:, :128])
                m_s[pi, _rows(qrow, tq, d), :] = jnp.where(head0, mh[:tq], mh[tq:])
                l_s[pi, _rows(qrow, tq, d), :] = jnp.where(head0, pv[:tq, 128:], pv[tq:, 128:])

            _for_tiles(chunk, pat, tile)
        m_all = jnp.maximum(jnp.maximum(m_s[0], m_s[1]), m_s[2])
        e = [jnp.exp(m_s[k] - m_all) for k in range(3)]
        den = e[0] * l_s[0] + e[1] * l_s[1] + e[2] * l_s[2]
        o_ref[...] = (e[0] * acc_s[0] + e[1] * acc_s[1] + e[2] * acc_s[2]) / den
        lse_ref[...] = m_all + jnp.log(den)

    stat = pltpu.VMEM((3, ATTN_CHUNK, 128), F32)
    return _pcall(
        body, name=name, grid=(D // 128, s // ATTN_CHUNK),
        in_specs=[whole(0), whole(8), whole(16), table, table, table], out_specs=[chunk_spec, chunk_spec],
        out_shape=[jax.ShapeDtypeStruct((s, D), F32)] * 2,
        scratch_shapes=[pltpu.VMEM((s, 128), F32), pltpu.VMEM((s, 128), F32), stat, stat, stat],
        compiler_params=_cparams(("parallel", "arbitrary")),
    )(proj, proj, proj, *tabs)


def attn_bwd_all(proj, tabs, dmix, o, lse, name):
    s = proj.shape[0]
    plan = _attn_plan(s)
    whole, table, chunk_spec = _attn_specs(s)
    nchunk = s // ATTN_CHUNK

    def body(q_ref, k_ref, v_ref, c_ref, sa_ref, sb_ref, do_ref, o_ref, lse_ref, dq_ref, dk_ref, dv_ref, qs, ks, aug0_s, aug1_s):
        chunk = pl.program_id(1)

        @pl.when(chunk == 0)
        def _():
            qs[...] = _rope_pair(q_ref[...], c_ref[...], sa_ref[...], sb_ref[...]) * (HD ** -0.5)
            ks[...] = _rope_pair(k_ref[...], c_ref[...], sa_ref[...], sb_ref[...])
            dk_ref[...] = jnp.zeros_like(dk_ref)
            dv_ref[...] = jnp.zeros_like(dv_ref)

        base = pl.multiple_of(chunk * ATTN_CHUNK, ATTN_CHUNK)
        prod = do_ref[...] * o_ref[...]
        first = _iota(prod.shape, 1) < HD
        delta = jnp.where(first, jnp.sum(jnp.where(first, prod, 0.0), axis=1, keepdims=True),
                          jnp.sum(jnp.where(first, 0.0, prod), axis=1, keepdims=True))
        lane = _iota(prod.shape, 1)

        def as_lanes(lse_h, delta_h):
            a, b = [u.astype(F32) for u in _parts(lse_h, 3)], [u.astype(F32) for u in _parts(delta_h, 3)]
            out = jnp.zeros_like(lse_h)
            for k, u in enumerate(a + b):
                out = jnp.where(lane == k, u, out)
            return out

        lsev = lse_ref[...]
        aug0_s[...] = as_lanes(lsev, delta)
        aug1_s[...] = as_lanes(pltpu.roll(lsev, HD, 1), pltpu.roll(delta, HD, 1))
        for pi, pat in enumerate(plan):
            d, tq, win = pat[:3]
            head0 = _iota((tq, 128), 1) < HD

            def tile(qrow, krow, valid, pi=pi, d=d, tq=tq, win=win, head0=head0):
                qv = qs[_rows(base + qrow, tq, d), :].astype(BF16)
                kw = ks[_rows(krow, win, d), :].astype(BF16)
                vw = v_ref[_rows(krow, win, d), :].astype(BF16)
                dob = do_ref[_rows(qrow, tq, d), :].astype(BF16)
                aug = jnp.concatenate([aug0_s[_rows(qrow, tq, d), :], aug1_s[_rows(qrow, tq, d), :]], axis=0).astype(BF16)
                klane = _iota((win, 128), 1)
                minus_lse = jnp.where(klane < 3, -1.0, 0.0).astype(BF16)
                minus_delta = jnp.where((klane >= 3) & (klane < 6), -1.0, 0.0).astype(BF16)
                q2, do2 = _stack_heads(qv, head0), _stack_heads(dob, head0)
                s_lse = _dot(jnp.concatenate([q2, aug], axis=1), jnp.concatenate([kw, minus_lse], axis=1), NT)
                dp_delta = _dot(jnp.concatenate([do2, aug], axis=1), jnp.concatenate([vw, minus_delta], axis=1), NT)
                p = jnp.where(valid, jnp.exp(s_lse), 0.0)
                ds = (p * dp_delta).astype(BF16)
                dq2 = _dot(ds, kw)
                dk = _dot(ds, q2, TN)
                dv = _dot(p.astype(BF16), do2, TN)
                dqv = jnp.where(head0, dq2[:tq], dq2[tq:])
                if pi == 0:
                    dq_ref[_rows(qrow, tq, d), :] = dqv
                else:
                    dq_ref[_rows(qrow, tq, d), :] += dqv
                dk_ref[_rows(krow, win, d), :] += dk
                dv_ref[_rows(krow, win, d), :] += dv

            _for_tiles(chunk, pat, tile)
        tab = [t[pl.ds(base, ATTN_CHUNK), :] for t in (c_ref, sa_ref, sb_ref)]
        dq_ref[...] = _rope_pair_t(dq_ref[...] * (HD ** -0.5), *tab)

        @pl.when(chunk == nchunk - 1)
        def _():
            dk_ref[...] = _rope_pair_t(dk_ref[...], c_ref[...], sa_ref[...], sb_ref[...])

    return _pcall(
        body, name=name, grid=(D // 128, nchunk),
        in_specs=[whole(0), whole(8), whole(16), table, table, table, chunk_spec, chunk_spec, chunk_spec],
        out_specs=[chunk_spec, whole(0), whole(0)], out_shape=[jax.ShapeDtypeStruct((s, D), F32)] * 3,
        scratch_shapes=[pltpu.VMEM((s, 128), F32), pltpu.VMEM((s, 128), F32)] + [pltpu.VMEM((ATTN_CHUNK, 128), F32)] * 2,
        compiler_params=_cparams(("parallel", "arbitrary")),
    )(proj, proj, proj, *tabs, dmix, o, lse)


def _ssd_common(x_ref, b_ref, c_ref, dt_ref, dtt_ref, a_ref, ar_ref, rev):
    ii, jj = _iota((CHUNK, CHUNK), 0), _iota((CHUNK, CHUNK), 1)
    low = jj >= ii if rev else jj <= ii
    x, dtx = x_ref[...], dt_ref[...]
    bm, cm = b_ref[...].astype(BF16), c_ref[...].astype(BF16)
    a = dtx * a_ref[...]
    arow = dtt_ref[0] * ar_ref[0]
    lowb = low.astype(BF16)
    cs = _dot(lowb, jnp.concatenate(_parts(a, 3), axis=1))
    cs = cs[:, :128] + cs[:, 128:256] + cs[:, 256:]
    csr = _dot(jnp.concatenate([p.astype(F32) for p in _parts(arow, 3)], axis=0).astype(BF16), lowb, NT)
    csr = csr[0:8] + csr[8:16] + csr[16:24]
    last = 0 if rev else CHUNK - 1
    tot = cs[last:last + 1, :]
    xdt = x * dtx
    cb = _dot(cm, bm, NT)
    lmats = [jnp.exp(jnp.where(low, cs[:, HD * h:HD * h + 1] - csr[h:h + 1, :], -1e30)) for h in range(2)]
    return dict(x=x, dtx=dtx, bm=bm, cm=cm, a=a, cs=cs, tot=tot, xdt=xdt, cb=cb, lmats=lmats, low=low, last=last)


SSD_SUB = 8


def _ssd_specs(s, rev_order):
    nblk, rows = s // (SSD_SUB * CHUNK), SSD_SUB * CHUNK
    ci = (lambda c: nblk - 1 - c) if rev_order else (lambda c: c)
    tile = lambda off, div: pl.BlockSpec((rows, 128), lambda p, c: (ci(c), off + p // div))
    common = [tile(0, 1), tile(8, 2), tile(12, 2), tile(0, 1),
              pl.BlockSpec((1, 8, rows), lambda p, c: (p, 0, ci(c))),
              pl.BlockSpec((1, 128), lambda p, c: (0, p)),
              pl.BlockSpec((1, 8, 128), lambda p, c: (p, 0, 0))]
    hs = pl.BlockSpec((1, SSD_SUB, CHUNK, 128), lambda p, c: (p, ci(c), 0, 0))
    return nblk, common, tile(0, 1), hs


def _chunk_rows(ref, j):
    return ref.at[pl.ds(j * CHUNK, CHUNK), :]


def _ssd_chunk(refs, j):
    return [_chunk_rows(r, j) for r in refs[:4]] + [refs[4].at[:, :, pl.ds(j * CHUNK, CHUNK)], refs[5], refs[6]]


def _ssd_args(xbc, t):
    return [xbc, xbc, xbc, t["dt_exp"], t["dtt"], t["a_exp"], t["a_rows"]]


def ssd_fwd(xbc, dirs, name):
    s = xbc.shape[0]
    nd = len(dirs)
    specs = [_ssd_specs(s, t["rev"]) for t in dirs]
    nck = specs[0][0]

    def one(rev, x_ref, b_ref, c_ref, dt_ref, dtt_ref, a_ref, ar_ref, y_ref, hs_ref, h_scr):
        v = _ssd_common(x_ref, b_ref, c_ref, dt_ref, dtt_ref, a_ref, ar_ref, rev)
        xdtb = v["xdt"].astype(BF16)
        yd = _dot(jnp.concatenate([v["cb"] * v["lmats"][h] for h in range(2)], axis=0).astype(BF16), xdtb)
        h_in = h_scr[...]
        hs_ref[0, 0] = h_in
        y_off = _dot(v["cm"], h_in.astype(BF16)) * jnp.exp(v["cs"])
        y_ref[...] = jnp.where(_iota((CHUNK, 128), 1) < HD, yd[:CHUNK], yd[CHUNK:]) + y_off
        decay = jnp.exp(v["tot"] - v["cs"])
        h_scr[...] = jnp.exp(v["tot"]) * h_in + _dot(v["bm"], (v["xdt"] * decay).astype(BF16), TN)

    def body(*refs):
        @pl.when(pl.program_id(1) == 0)
        def _():
            for k in range(nd):
                refs[9 * nd + k][...] = jnp.zeros((CHUNK, 128), F32)

        for k, t in enumerate(dirs):
            y_ref, hs_ref = refs[7 * nd + 2 * k:7 * nd + 2 * k + 2]
            for j in (range(SSD_SUB)[::-1] if t["rev"] else range(SSD_SUB)):
                one(t["rev"], *_ssd_chunk(refs[7 * k:7 * k + 7], j), _chunk_rows(y_ref, j), hs_ref.at[:, pl.ds(j, 1)], refs[9 * nd + k])

    res = _pcall(
        body, name=name, grid=(8, nck), in_specs=[sp for t in specs for sp in t[1]],
        out_specs=[sp for t in specs for sp in (t[2], t[3])],
        out_shape=[jax.ShapeDtypeStruct((s, D), F32), jax.ShapeDtypeStruct((8, s // CHUNK, CHUNK, 128), F32)] * nd,
        scratch_shapes=[pltpu.VMEM((CHUNK, 128), F32)] * nd, compiler_params=_cparams(("parallel", "arbitrary")),
    )(*[a for t in dirs for a in _ssd_args(xbc, t)])
    return [(res[2 * k], res[2 * k + 1]) for k in range(nd)]


def ssd_bwd(xbc, dirs, dy, name):
    s = xbc.shape[0]
    nd = len(dirs)
    specs = [_ssd_specs(s, not t["rev"]) for t in dirs]
    nck = specs[0][0]

    def one(rev, x_ref, b_ref, c_ref, dt_ref, dtt_ref, a_ref, ar_ref, hs_ref, dy_ref,
            dx_ref, ddt_ref, db_ref, dc_ref, dal_ref, dh_scr):
        v = _ssd_common(x_ref, b_ref, c_ref, dt_ref, dtt_ref, a_ref, ar_ref, rev)
        bm, cm, cs, tot, xdt = v["bm"], v["cm"], v["cs"], v["tot"], v["xdt"]
        h_in, dh = hs_ref[0, 0], dh_scr[...]
        dyv = dy_ref[...]
        dyb = dyv.astype(BF16)
        etot, decay, ecs = jnp.exp(tot), jnp.exp(tot - cs), jnp.exp(cs)
        xdtb = xdt.astype(BF16)
        xdec = xdt * decay
        dch = (dyv * ecs).astype(BF16)
        hb, dhb = h_in.astype(BF16), dh.astype(BF16)
        y_off = _dot(cm, hb) * ecs
        dc = _dot(dch, hb, NT)
        dh_y = _dot(cm, dch, TN)
        dxdec = _dot(bm, dhb)
        db = _dot(xdec.astype(BF16), dhb, NT)
        state_term = xdec * dxdec
        dtot = _colsum(dh * h_in) * etot + _colsum(state_term)
        head0 = _iota((CHUNK, 128), 1) < HD
        ii, jj = _iota((CHUNK, CHUNK), 0), _iota((CHUNK, CHUNK), 1)
        low_t = jj <= ii if rev else jj >= ii
        not_low_t = (~low_t).astype(BF16)
        g = _dot(_stack_heads(dyb, head0), xdtb, NT)
        gl = [g[:CHUNK] * v["lmats"][0], g[CHUNK:] * v["lmats"][1]]
        dcb = gl[0] + gl[1]
        dxd = _dot(jnp.concatenate([v["cb"] * v["lmats"][h] for h in range(2)], axis=1).astype(BF16), dyb, TN)
        dxd = jnp.where(head0, dxd[:CHUNK], dxd[CHUNK:])
        w = _dot(not_low_t, jnp.concatenate([gl[h] * v["cb"] for h in range(2)], axis=0).astype(BF16), NT)
        da_l = [jnp.sum(jnp.where(low_t, w[:, CHUNK * h:CHUNK * h + CHUNK], 0.0), axis=1, keepdims=True) for h in range(2)]
        dxdt = dxdec * decay + dxd
        dcbb = dcb.astype(BF16)
        dc_ref[...] = dc + _dot(dcbb, bm)
        db_ref[...] = db + _dot(dcbb, cm, TN)
        dcs = dyv * y_off - state_term + jnp.where(_iota((CHUNK, 128), 0) == v["last"], dtot, 0.0)
        lowb = v["low"].astype(BF16)
        da = _dot(lowb, jnp.concatenate(_parts(dcs, 2), axis=1), TN)
        da = da[:, :128] + da[:, 128:]
        seg = ((ii < HD) == (jj < HD)).astype(BF16)
        sums = _dot(jnp.concatenate(_parts(da, 2) + _parts(dxdt * v["x"], 2), axis=0), seg)
        da = sums[:CHUNK] + sums[CHUNK:2 * CHUNK] + jnp.where(head0, da_l[0], da_l[1])
        ddt_x = sums[2 * CHUNK:3 * CHUNK] + sums[3 * CHUNK:]
        dx_ref[...] = dxdt * v["dtx"]
        ddt_ref[...] = ddt_x + da * a_ref[...]
        dal_ref[0] += _colsum(da * v["a"])
        dh_scr[...] = etot * dh + dh_y

    def body(*refs):
        @pl.when(pl.program_id(1) == 0)
        def _():
            for k in range(nd):
                refs[14 * nd + k][...] = jnp.zeros((CHUNK, 128), F32)
                refs[9 * nd + 5 * k + 4][...] = jnp.zeros((1, 8, 128), F32)

        for k, t in enumerate(dirs):
            ins, outs = refs[9 * k:9 * k + 9], refs[9 * nd + 5 * k:9 * nd + 5 * k + 5]
            for j in (range(SSD_SUB) if t["rev"] else range(SSD_SUB)[::-1]):
                one(t["rev"], *_ssd_chunk(ins[:7], j), ins[7].at[:, pl.ds(j, 1)], _chunk_rows(ins[8], j),
                    *[_chunk_rows(r, j) for r in outs[:4]], outs[4], refs[14 * nd + k])

    acc_spec = pl.BlockSpec((1, 8, 128), lambda p, c: (p, 0, 0))
    res = _pcall(
        body, name=name, grid=(8, nck), in_specs=[sp for t in specs for sp in t[1] + [t[3], t[2]]],
        out_specs=[sp for t in specs for sp in [t[2]] * 4 + [acc_spec]],
        out_shape=([jax.ShapeDtypeStruct((s, D), F32)] * 4 + [jax.ShapeDtypeStruct((8, 8, 128), F32)]) * nd,
        scratch_shapes=[pltpu.VMEM((CHUNK, 128), F32)] * nd, compiler_params=_cparams(("parallel", "arbitrary")),
    )(*[a for t in dirs for a in _ssd_args(xbc, t) + [t["hs"], dy]])
    return [res[5 * k:5 * k + 5] for k in range(nd)]


def _group_norm_stats(g):
    r = [lax.rsqrt(jnp.mean(g[:, 256 * k:256 * k + 256] ** 2, axis=-1, keepdims=True) + EPS) for k in range(4)]
    grp = _iota(g.shape, 1) // 256
    return jnp.where(grp == 0, r[0], jnp.where(grp == 1, r[1], jnp.where(grp == 2, r[2], r[3])))


def _group_mean(t):
    m = [jnp.mean(t[:, 256 * k:256 * k + 256], axis=-1, keepdims=True) for k in range(4)]
    grp = _iota(t.shape, 1) // 256
    return jnp.where(grp == 0, m[0], jnp.where(grp == 1, m[1], jnp.where(grp == 2, m[2], m[3])))


def _mesh_pos():
    return lax.axis_index("x"), lax.axis_index("y"), lax.axis_index("c")


HBM = pl.BlockSpec(memory_space=pltpu.HBM)
SEM = pl.BlockSpec(memory_space=pltpu.SEMAPHORE)
EFFECT = pltpu.SideEffectType.DATAFLOW_SIDE_EFFECTING


def _hbm(t):
    return pltpu.with_memory_space_constraint(t, pltpu.HBM)


def _other_chips(x, y):
    return [(1 - x, y), (x, 1 - y), (1 - x, 1 - y)]


def _peer(x, y, c, m):
    return x ^ (m >> 2), y ^ ((m >> 1) & 1), c ^ (m & 1)


def gather_start(srcs_a, srcs_b, halved=()):
    srcs = [_hbm(t) for t in list(srcs_a) + list(srcs_b)]
    n, na = len(srcs), len(srcs_a)
    half = [k in halved for k in range(n)]
    lands = [_hbm(lax.empty((4,) + (t.shape[1:] if half[k] else t.shape), t.dtype)) for k, t in enumerate(srcs)]

    def body(*refs):
        src, land = refs[:n], refs[n:2 * n]
        sems = refs[2 * n:2 * n + 4]
        x, y, c = _mesh_pos()
        for k in range(n):
            for j, (px, py) in enumerate(_other_chips(x, y)):
                send, recv, idx = (sems[0], sems[1], 3 * k + j) if k < na else (sems[2], sems[3], 3 * (k - na) + j)
                pltpu.make_async_remote_copy(src_ref=src[k].at[c] if half[k] else src[k], dst_ref=land[k].at[2 * x + y], send_sem=send.at[idx],
                                             recv_sem=recv.at[idx], device_id=(px, py, c), device_id_type=MESH).start()

    sem_a, sem_b = pltpu.SemaphoreType.DMA((3 * na,)), pltpu.SemaphoreType.DMA((3 * (n - na),))
    res = _pcall(
        body, name="gather_start", in_specs=[HBM] * (2 * n), out_specs=[SEM] * 4 + [HBM] * (2 * n),
        out_shape=[sem_a, sem_a, sem_b, sem_b] + [pltpu.HBM(t.shape, t.dtype) for t in srcs + lands],
        input_output_aliases={i: 4 + i for i in range(2 * n)},
        compiler_params=pltpu.CompilerParams(has_side_effects=EFFECT),
    )(*srcs, *lands)
    thru_src, thru_land = res[4:4 + n], res[4 + n:]
    return ((res[0], res[1], thru_src[:na], thru_land[:na], half[:na]), (res[2], res[3], thru_src[na:], thru_land[na:], half[na:]))


def gather_wait(group, name, after=None):
    send, recv, srcs, lands, half = group
    n = len(srcs)

    def body(*refs):
        src, land, send_ref, recv_ref = refs[:n], refs[n:2 * n], refs[2 * n], refs[2 * n + 1]
        x, y, c = _mesh_pos()
        for j, (px, py) in enumerate(_other_chips(x, y)):
            for k in range(n):
                cp = pltpu.make_async_remote_copy(src_ref=src[k].at[0] if half[k] else src[k], dst_ref=land[k].at[2 * px + py], send_sem=send_ref.at[3 * k + j],
                                                  recv_sem=recv_ref.at[3 * k + j], device_id=(px, py, c), device_id_type=MESH)
                cp.wait_send()
                cp.wait_recv()

    extra = [] if after is None else [after]
    res = _pcall(
        body, name=name, in_specs=[HBM] * (2 * n) + [SEM, SEM] + [pl.BlockSpec(memory_space=pl.ANY)] * len(extra),
        out_specs=[HBM] * (2 * n), out_shape=[pltpu.HBM(t.shape, t.dtype) for t in list(srcs) + list(lands)],
        input_output_aliases={i: i for i in range(2 * n)}, compiler_params=pltpu.CompilerParams(has_side_effects=EFFECT),
    )(*srcs, *lands, send, recv, *extra)
    return res[:n], res[n:]


def scatter_start(pieces, smalls, name):
    srcs = [_hbm(t) for t in list(pieces) + list(smalls)]
    n, npc = len(srcs), len(pieces)
    lands = [_hbm(lax.empty((8,) + (t.shape[2:] if k < npc else t.shape), t.dtype)) for k, t in enumerate(srcs)]

    def body(*refs):
        src, land, send, recv = refs[:n], refs[n:2 * n], refs[2 * n], refs[2 * n + 1]
        token = refs[-1]
        x, y, c = _mesh_pos()
        for m in range(1, 8):
            px, py, pc = _peer(x, y, c, m)
            for k in range(n):
                s_ref = src[k].at[2 * px + py, pc] if k < npc else src[k]
                d_ref = land[k].at[m] if k < npc else land[k].at[4 * x + 2 * y + c]
                pltpu.make_async_remote_copy(src_ref=s_ref, dst_ref=d_ref, send_sem=send.at[7 * k + m - 1], recv_sem=recv.at[7 * k + m - 1],
                                             device_id=(px, py, pc), device_id_type=MESH).start()
        token[...] = jnp.zeros_like(token)

    sem = pltpu.SemaphoreType.DMA((7 * n,))
    res = _pcall(
        body, name=name, in_specs=[HBM] * (2 * n),
        out_specs=[SEM, SEM] + [HBM] * (2 * n) + [pl.BlockSpec(memory_space=pltpu.VMEM)],
        out_shape=[sem, sem] + [pltpu.HBM(t.shape, t.dtype) for t in srcs + lands] + [jax.ShapeDtypeStruct((8, 128), F32)],
        input_output_aliases={i: 2 + i for i in range(2 * n)},
        compiler_params=pltpu.CompilerParams(has_side_effects=EFFECT),
    )(*srcs, *lands)
    return (res[0], res[1], res[2:2 + n], res[2 + n:2 + 2 * n], npc), res[-1]


def scatter_wait(group, name, after=None):
    send, recv, srcs, lands, npc = group
    n = len(srcs)

    def body(*refs):
        src, land, send_ref, recv_ref = refs[:n], refs[n:2 * n], refs[2 * n], refs[2 * n + 1]
        x, y, c = _mesh_pos()
        for m in range(1, 8):
            px, py, pc = _peer(x, y, c, m)
            for k in range(n):
                s_ref = src[k].at[0, 0] if k < npc else src[k]
                d_ref = land[k].at[m] if k < npc else land[k].at[4 * px + 2 * py + pc]
                cp = pltpu.make_async_remote_copy(src_ref=s_ref, dst_ref=d_ref, send_sem=send_ref.at[7 * k + m - 1],
                                                  recv_sem=recv_ref.at[7 * k + m - 1], device_id=(px, py, pc), device_id_type=MESH)
                cp.wait_send()
                cp.wait_recv()

    extra = [] if after is None else [after]
    res = _pcall(
        body, name=name, in_specs=[HBM] * (2 * n) + [SEM, SEM] + [pl.BlockSpec(memory_space=pl.ANY)] * len(extra),
        out_specs=[HBM] * (2 * n), out_shape=[pltpu.HBM(t.shape, t.dtype) for t in list(srcs) + list(lands)],
        input_output_aliases={i: i for i in range(2 * n)}, compiler_params=pltpu.CompilerParams(has_side_effects=EFFECT),
    )(*srcs, *lands, send, recv, *extra)
    return res[:n], res[n:]


def swap_halves(pieces, name):
    n = len(pieces)
    whole = pl.BlockSpec(memory_space=pltpu.VMEM)

    def body(*refs):
        p_refs, o_refs, send_sems, recv_sems, local_sems = refs[:n], refs[n:2 * n], refs[2 * n], refs[2 * n + 1], refs[2 * n + 2]
        x, y, c = _mesh_pos()
        local = [pltpu.make_async_copy(p_refs[k], o_refs[k].at[c], local_sems.at[k]) for k in range(n)]
        for cp in local:
            cp.start()

        def copy(k, slot):
            return pltpu.make_async_remote_copy(src_ref=p_refs[k], dst_ref=o_refs[k].at[slot], send_sem=send_sems.at[k],
                                                recv_sem=recv_sems.at[k], device_id=(x, y, 1 - c), device_id_type=MESH)

        for k in range(n):
            copy(k, c).start()
        for k in range(n):
            copy(k, 1 - c).wait_recv()
        for k in range(n):
            copy(k, c).wait_send()
        for cp in local:
            cp.wait()

    return _pcall(
        body, name=name, in_specs=[whole] * n, out_specs=[whole] * n,
        out_shape=[jax.ShapeDtypeStruct((2,) + t.shape, t.dtype) for t in pieces],
        scratch_shapes=[pltpu.SemaphoreType.DMA((n,)), pltpu.SemaphoreType.DMA((n,)), pltpu.SemaphoreType.DMA((n,))],
        compiler_params=_cparams(),
    )(*pieces)


def adamw(w, g, m, v, name):
    rows, cols = w.shape
    tm = rows
    for t in (256, 352, 128, 144, 64, 32, 16, 8):
        if rows % t == 0:
            tm = t
            break

    def fn(i, nrow, wv, gv, mv, vv):
        mn = ADAM_B1 * mv + (1.0 - ADAM_B1) * gv
        vn = ADAM_B2 * vv + (1.0 - ADAM_B2) * (gv * gv)
        m_hat = mn / (1.0 - ADAM_B1 ** ADAM_STEP)
        v_hat = vn / (1.0 - ADAM_B2 ** ADAM_STEP)
        delta = -ADAM_LR * (m_hat / (jnp.sqrt(v_hat) + ADAM_EPS) + ADAM_WD * wv)
        return delta, mn, vn

    return ew(fn, name, rows, tm, 1, [(t, "row", cols, 0) for t in (w, g, m, v)], [(cols, F32, cols)] * 3)


REST = ("w_out", "w_up", "w_down")
SMALL = ("norm1_w", "ssm_conv_w", "ssm_conv_b", "a_log_f", "a_log_b", "dt_bias_f", "dt_bias_b", "d_skip",
         "ssm_norm_w", "norm2_w", "ffn_conv_w", "ffn_conv_b", "final_norm_w")
WEIGHTS = ("norm1_w", "w_in", "ssm_conv_w", "ssm_conv_b", "a_log_f", "a_log_b", "dt_bias_f", "dt_bias_b", "d_skip",
           "ssm_norm_w", "w_out", "norm2_w", "w_up", "ffn_conv_w", "ffn_conv_b", "w_down", "final_norm_w")
INPUTS = ("x",) + WEIGHTS + ("loss_target",) + tuple("m_" + n for n in WEIGHTS) + tuple("v_" + n for n in WEIGHTS)


def _flat_rows(parts, width, rows):
    flat = jnp.concatenate([p.reshape(-1) for p in parts])
    return jnp.pad(flat, (0, rows * width - flat.shape[0])).reshape(rows, width)


def _split_flat(flat, shapes):
    out, pos = [], 0
    flat = flat.reshape(-1)
    for shp in shapes:
        n = int(np.prod(shp))
        out.append(flat[pos:pos + n].reshape(shp))
        pos += n
    return out


def _col_shards(t, nshard):
    r, c = t.shape
    return t.reshape(r, nshard, c // nshard).transpose(1, 0, 2)


def _row_shards(t, nshard):
    r, c = t.shape
    return t.reshape(nshard, r // nshard, c)


def kernel(x, norm1_w, w_in, ssm_conv_w, ssm_conv_b, a_log_f, a_log_b, dt_bias_f, dt_bias_b, d_skip, ssm_norm_w, w_out, norm2_w, w_up, ffn_conv_w, ffn_conv_b, w_down, final_norm_w, loss_target, m_norm1_w, m_w_in, m_ssm_conv_w, m_ssm_conv_b, m_a_log_f, m_a_log_b, m_dt_bias_f, m_dt_bias_b, m_d_skip, m_ssm_norm_w, m_w_out, m_norm2_w, m_w_up, m_ffn_conv_w, m_ffn_conv_b, m_w_down, m_final_norm_w, v_norm1_w, v_w_in, v_ssm_conv_w, v_ssm_conv_b, v_a_log_f, v_a_log_b, v_dt_bias_f, v_dt_bias_b, v_d_skip, v_ssm_norm_w, v_w_out, v_norm2_w, v_w_up, v_ffn_conv_w, v_ffn_conv_b, v_w_down, v_final_norm_w):
    p = dict(zip(INPUTS, (x, norm1_w, w_in, ssm_conv_w, ssm_conv_b, a_log_f, a_log_b, dt_bias_f, dt_bias_b, d_skip, ssm_norm_w, w_out, norm2_w, w_up, ffn_conv_w, ffn_conv_b, w_down, final_norm_w, loss_target, m_norm1_w, m_w_in, m_ssm_conv_w, m_ssm_conv_b, m_a_log_f, m_a_log_b, m_dt_bias_f, m_dt_bias_b, m_d_skip, m_ssm_norm_w, m_w_out, m_norm2_w, m_w_up, m_ffn_conv_w, m_ffn_conv_b, m_w_down, m_final_norm_w, v_norm1_w, v_w_in, v_ssm_conv_w, v_ssm_conv_b, v_a_log_f, v_a_log_b, v_dt_bias_f, v_dt_bias_b, v_d_skip, v_ssm_norm_w, v_w_out, v_norm2_w, v_w_up, v_ffn_conv_w, v_ffn_conv_b, v_w_down, v_final_norm_w)))
    x = p["x"][0]
    tgt = p["loss_target"][0]
    s = x.shape[0]
    chip = 2 * lax.axis_index("x") + lax.axis_index("y")

    own_slot = lambda land, mine, slot: lax.dynamic_update_slice_in_dim(land, mine[None], slot, axis=0)
    core = lax.axis_index("c")
    src_in = p["w_in"][0].astype(BF16).reshape(2, D // 2, -1)
    src_rest = [p[n][0].astype(BF16) for n in REST]
    small_w = _flat_rows([p["ssm_conv_w"][0], p["ffn_conv_w"][0]], 128, 48)
    gather_in, gather_rest = gather_start([src_in, small_w], src_rest, halved=(0,))
    (src_in, small_w), (wg_in, sg) = gather_wait(gather_in, "gather_wait_in")
    wg_in = own_slot(wg_in, lax.dynamic_index_in_dim(src_in, core, 0, keepdims=False), chip)
    wg_in, = swap_halves([wg_in], "swap_w_in_rows")
    w_in = wg_in.transpose(0, 2, 1, 3).reshape(D, -1)
    sg = own_slot(sg, small_w, chip)
    n_in = w_in.shape[1]
    n_main = 6 * D
    w_dt = jnp.pad(w_in[:, n_main:], ((0, 0), (0, 128 - (n_in - n_main))))
    sgf = sg.reshape(4, -1)
    n_sc, n_fc = p["ssm_conv_w"].shape[1], p["ffn_conv_w"].shape[1]
    ssm_cw = sgf[:, :n_sc * 3].reshape(-1, 3).T
    ffn_cw = sgf[:, n_sc * 3:(n_sc + n_fc) * 3].reshape(-1, 3).T
    ssm_cb, ffn_cb = p["ssm_conv_b"], p["ffn_conv_b"]
    n1w, n2w, snw, fnw = p["norm1_w"], p["norm2_w"], p["ssm_norm_w"], p["final_norm_w"].reshape(1, D)

    h1, = ew(lambda i, n, xv, w: _rms_fwd(xv, w), "rms1", s, 256, 1,
             [(x, "row", D, 0), (n1w, "const", D, 0)], [(D, BF16, D)])
    proj = matmul(h1, w_in, "nn", "in_proj", n_cols=n_main)
    proj_dt = matmul(h1, w_dt, "nn", "in_proj_dt")
    tabs = _rope_tables(s)
    attn, lse = attn_fwd_all(proj, tabs, "attn_fwd")

    def conv_silu_fn(i, n, xv, xp, xn, w, b):
        return _silu(w[0:1] * _shift_down(xv, xp, i) + w[1:2] * xv + w[2:3] * _shift_up(xv, xn, i, n) + b)

    xbc_act, = ew(conv_silu_fn, "ssm_conv", s, 256, 2,
                  [(proj, "row", D, 4), (proj, "prev", D, 4), (proj, "next", D, 4),
                   (ssm_cw, "const", D, 0), (ssm_cb, "const", D, 0)], [(2 * D, F32, D)])
    dt_bias = jnp.pad(jnp.concatenate([p["dt_bias_f"], p["dt_bias_b"]], axis=1), ((0, 0), (0, 96)))

    lanes_of = np.arange(128)[:, None] == np.arange(D)[None, :] // HD
    spread = [jnp.asarray(np.roll(lanes_of, 16 * k, axis=0), BF16) for k in range(2)]

    def softplus_fn(i, n, r, b, ef, eb):
        t = r + b
        dtv = jnp.maximum(t, 0.0) + jnp.log(1.0 + jnp.exp(-jnp.abs(t)))
        parts = _parts(dtv, 3)
        return dtv, sum(_dot(q, ef) for q in parts), sum(_dot(q, eb) for q in parts)

    dt, dt_exp_f, dt_exp_b = ew(softplus_fn, "dt_softplus", s, 512, 1,
                                [(proj_dt, "row", 128, 0), (dt_bias, "const", 128, 0), (spread[0], "const", D, 0), (spread[1], "const", D, 0)],
                                [(128, F32, 128), (D, F32, D), (D, F32, D)])
    d_exp = jnp.repeat(p["d_skip"], HD, axis=1)
    ssd = []
    for k, (a_log, rev) in enumerate(((p["a_log_f"], False), (p["a_log_b"], True))):
        dt_k = dt[:, 16 * k:16 * k + 16]
        a_head = -jnp.exp(a_log)
        dt_exp = (dt_exp_f, dt_exp_b)[k]
        dtt = jnp.pad(dt_k.T.reshape(8, 2, s), ((0, 0), (0, 6), (0, 0)))
        a_exp = jnp.repeat(a_head, HD, axis=1)
        a_rows = jnp.broadcast_to(jnp.pad(a_head.reshape(8, 2), ((0, 0), (0, 6)))[:, :, None], (8, 8, 128))
        ssd.append(dict(dt_exp=dt_exp, dtt=dtt, a_exp=a_exp, a_rows=a_rows, rev=rev))
    for t, (y_k, hs_k) in zip(ssd, ssd_fwd(xbc_act, ssd, "ssd_fwd")):
        t["y"], t["hs"] = y_k, hs_k

    def gate_fn(i, n, yf, yb, xs, z, dsk, w):
        g = (yf + yb + dsk * xs) * _silu(z)
        return g * _group_norm_stats(g) * w

    ssm_out, = ew(gate_fn, "ssm_gate_norm", s, 256, 1,
                  [(ssd[0]["y"], "row", D, 0), (ssd[1]["y"], "row", D, 0), (xbc_act, "row", D, 0), (proj, "row", D, 3),
                   (d_exp, "const", D, 0), (snw, "const", D, 0)], [(D, F32, D)])
    mix = jnp.concatenate([attn, ssm_out], axis=1).astype(BF16)
    src_rest, wg_rest = gather_wait(gather_rest, "gather_wait_rest", after=mix)
    wg_rest = [own_slot(land, mine, chip) for land, mine in zip(wg_rest, src_rest)]
    w_out = wg_rest[0].reshape(-1, D)
    w_up = wg_rest[1].transpose(1, 0, 2).reshape(D, -1)
    w_down = wg_rest[2].reshape(-1, D)
    mix_w = matmul(mix, w_out, "nn", "out_proj")

    def res_rms_fn(i, n, xv, mw, w):
        x1v = xv + mw
        return x1v, _rms_fwd(x1v, w)

    x1, h2 = ew(res_rms_fn, "res_rms2", s, 256, 1, [(x, "row", D, 0), (mix_w, "row", D, 0), (n2w, "const", D, 0)],
                [(D, F32, D), (D, BF16, D)])
    hw = matmul(h2, w_up, "nn", "ffn_up")
    fw = D_FF // 2
    nfb = D_FF // fw
    ffn_conv_ins = [(hw, "row", fw, 0), (hw, "prev", fw, 0), (hw, "next", fw, 0),
                    (hw, "row", fw, nfb), (hw, "prev", fw, nfb), (hw, "next", fw, nfb),
                    (ffn_cw, "const", fw, 0), (ffn_cw, "const", fw, nfb), (ffn_cb, "const", fw, 0), (ffn_cb, "const", fw, nfb)]

    def ffn_conv(i, n, g, gp, gn, u, up_, un, wg_, wu, bg, bu):
        gs = (_shift_down(g, gp, i), g, _shift_up(g, gn, i, n))
        us = (_shift_down(u, up_, i), u, _shift_up(u, un, i, n))
        gate = wg_[0:1] * gs[0] + wg_[1:2] * gs[1] + wg_[2:3] * gs[2] + bg
        upv = wu[0:1] * us[0] + wu[1:2] * us[1] + wu[2:3] * us[2] + bu
        return gate, upv, gs, us

    def glu_fn(i, n, *blocks):
        gate, upv, _, _ = ffn_conv(i, n, *blocks)
        return _silu(gate) * upv

    act, = ew(glu_fn, "ffn_conv_glu", s, 256, nfb, ffn_conv_ins, [(D_FF, BF16, fw)])
    ffn = matmul(act, w_down, "nn", "ffn_down")

    def head_fn(i, n, x1v, fv, tv, w):
        x2 = x1v + fv
        r = lax.rsqrt(jnp.mean(x2 * x2, axis=-1, keepdims=True) + EPS)
        xh = x2 * r
        diff = xh * w - tv
        loss = 0.5 * jnp.sum(jnp.mean(diff * diff, axis=-1, keepdims=True), axis=0, keepdims=True)
        dout = diff * (1.0 / D)
        dxh = dout * w
        dx2 = r * (dxh - xh * jnp.mean(dxh * xh, axis=-1, keepdims=True))
        return dx2, jnp.broadcast_to(loss, (1, 128)), _colsum(dout * xh)

    dx2, loss_acc, g_fnw = ew(head_fn, "loss_head", s, 256, 1,
                              [(x1, "row", D, 0), (ffn, "row", D, 0), (tgt, "row", D, 0), (fnw, "const", D, 0)],
                              [(D, F32, D)], [(128, 128), (D, D)])
    loss = lax.psum(loss_acc[0, 0], ("x", "y", "c"))

    g_w_down = matmul(act, dx2, "tn", "d_w_down")
    dact = matmul(dx2, w_down, "nt", "d_act")

    res = ew(ffn_conv_bwd_fn, "ffn_conv_glu_bwd", s, 256, nfb,
             ffn_conv_ins + [(dact, "row", fw, 0), (dact, "prev", fw, 0), (dact, "next", fw, 0)],
             [(D_FF, F32, fw)] * 2, [(D_FF, fw)] * 8)
    dhw_g, dhw_u = res[0], res[1]
    g_ffn_cw = jnp.concatenate([jnp.concatenate(res[2:5], axis=0), jnp.concatenate(res[5:8], axis=0)], axis=1).T
    g_ffn_cb = jnp.concatenate([res[8], res[9]], axis=1)

    g_w_up = jnp.concatenate([matmul(h2, dhw_g, "tn", "d_w_up_gate"), matmul(h2, dhw_u, "tn", "d_w_up_up")], axis=1)
    dh2_a = matmul(dhw_g, w_up, "nt", "d_h2_gate")
    dh2_b = matmul(dhw_u, w_up, "nt", "d_h2_up", b_k_off=D_FF // _pick(D_FF, 1408))

    def res_rms_bwd_fn(i, n, dres, da, db, xin, w):
        dx, dw = _rms_bwd(da + db, xin, w)
        return dres + dx, dw

    dx1, g_n2w = ew(res_rms_bwd_fn, "res_rms2_bwd", s, 256, 1,
                    [(dx2, "row", D, 0), (dh2_a, "row", D, 0), (dh2_b, "row", D, 0), (x1, "row", D, 0), (n2w, "const", D, 0)],
                    [(D, F32, D)], [(D, D)])

    g_w_out = matmul(mix, dx1, "tn", "d_w_out")
    to_pieces = lambda t: t.astype(BF16).reshape(4, 2, t.shape[1] // 2, t.shape[2])
    shards_rest = [_row_shards(g_w_out, 4), _col_shards(g_w_up, 4), _row_shards(g_w_down, 4)]
    scatter_rest, token = scatter_start([to_pieces(t) for t in shards_rest], [], "scatter_start_rest")
    dmix = matmul(dx1, w_out, "nt", "d_mix", after=token)
    ii, jj = np.arange(D)[:, None] // HD, np.arange(D)[None, :] // HD
    seg = jnp.asarray(ii == jj, BF16)

    def gate_bwd_fn(i, n, dout, yf, yb, xs, z, dsk, w, segm):
        yt = yf + yb + dsk * xs
        sz = _silu(z)
        g = yt * sz
        r = _group_norm_stats(g)
        gh = g * r
        dn = dout * w
        dg = r * (dn - gh * _group_mean(dn * gh))
        dy = dg * sz
        dsk_lane = jnp.broadcast_to(_colsum(dy * xs), (8, D))
        return dy, dg * yt * _dsilu(z), _colsum(dout * gh), sum(_dot(q, segm) for q in _parts(dsk_lane, 2))[0:1]

    dy, dz, g_snw, g_dskip_l = ew(
        gate_bwd_fn, "ssm_gate_norm_bwd", s, 256, 1,
        [(dmix, "row", D, 1), (ssd[0]["y"], "row", D, 0), (ssd[1]["y"], "row", D, 0), (xbc_act, "row", D, 0),
         (proj, "row", D, 3), (d_exp, "const", D, 0), (snw, "const", D, 0), (seg, "const", D, 0)],
        [(D, F32, D)] * 2, [(D, D)] * 2)
    sb = ssd_bwd(xbc_act, ssd, dy, "ssd_bwd")

    def dxbc_act_fn(i, n, dxf, dxb, dyv, dsk, dbf, dbb, dcf, dcb_):
        db, dc = dbf + dbb, dcf + dcb_
        db = [db[:, 256 * g:256 * g + 128] + db[:, 256 * g + 128:256 * g + 256] for g in range(4)]
        dc = [dc[:, 256 * g:256 * g + 128] + dc[:, 256 * g + 128:256 * g + 256] for g in range(4)]
        return jnp.concatenate([dxf + dxb + dyv * dsk] + db + dc, axis=1)

    dxbc_act, = ew(dxbc_act_fn, "d_xbc_act", s, 256, 1,
                   [(sb[0][0], "row", D, 0), (sb[1][0], "row", D, 0), (dy, "row", D, 0), (d_exp, "const", D, 0),
                    (sb[0][2], "row", D, 0), (sb[1][2], "row", D, 0), (sb[0][3], "row", D, 0), (sb[1][3], "row", D, 0)],
                   [(2 * D, F32, 2 * D)])

    res = ew(silu_conv_bwd_fn, "ssm_conv_bwd", s, 256, 2,
             [(proj, "row", D, 4), (proj, "prev", D, 4), (proj, "next", D, 4), (ssm_cw, "const", D, 0), (ssm_cb, "const", D, 0),
              (dxbc_act, "row", D, 0), (dxbc_act, "prev", D, 0), (dxbc_act, "next", D, 0)], [(2 * D, F32, D)], [(2 * D, D)] * 4)
    dxbc = res[0]
    g_ssm_cw = jnp.concatenate(res[1:4], axis=0).T
    g_ssm_cb = res[4]
    ddt = jnp.pad(jnp.concatenate([sb[0][1][:, ::HD], sb[1][1][:, ::HD]], axis=1), ((0, 0), (0, 96)))

    def dt_bwd_fn(i, n, dd, r, b):
        dr = dd * _sigmoid(r + b)
        return dr, _colsum(dr)

    dproj_dt, g_dt_bias = ew(dt_bwd_fn, "dt_softplus_bwd", s, 512, 1,
                             [(ddt, "row", 128, 0), (proj_dt, "row", 128, 0), (dt_bias, "const", 128, 0)],
                             [(128, F32, 128)], [(128, 128)])
    g_a_log = [t[4][:, 0, ::HD].reshape(1, 16) for t in sb]

    dq, dk, dv = attn_bwd_all(proj, tabs, dmix, attn, lse, "attn_bwd")

    dproj = jnp.concatenate([dq, dk, dv, dz, dxbc], axis=1).astype(BF16)
    g_w_in = jnp.concatenate([matmul(h1, dproj, "tn", "d_w_in"), matmul(h1, dproj_dt, "tn", "d_w_in_dt")[:, :n_in - n_main]], axis=1)
    scatter_in, token = scatter_start([to_pieces(_col_shards(g_w_in, 4))], [], "scatter_start_in")
    dh1_a = matmul(dproj, w_in, "nt", "d_h1", after=token)
    dh1_b = matmul(dproj_dt, w_dt, "nt", "d_h1_dt")
    grad_x, g_n1w = ew(res_rms_bwd_fn, "rms1_bwd", s, 256, 1,
                       [(dx1, "row", D, 0), (dh1_a, "row", D, 0), (dh1_b, "row", D, 0), (x, "row", D, 0), (n1w, "const", D, 0)],
                       [(D, F32, D)], [(D, D)])

    small_g = {"norm1_w": g_n1w, "ssm_conv_w": g_ssm_cw, "ssm_conv_b": g_ssm_cb, "a_log_f": g_a_log[0], "a_log_b": g_a_log[1],
               "dt_bias_f": g_dt_bias[:, :16], "dt_bias_b": g_dt_bias[:, 16:32], "d_skip": g_dskip_l[:, ::HD],
               "ssm_norm_w": g_snw, "norm2_w": g_n2w, "ffn_conv_w": g_ffn_cw, "ffn_conv_b": g_ffn_cb, "final_norm_w": g_fnw}
    small_shapes = [small_g[n].shape for n in SMALL]
    scatter_small, token = scatter_start([], [_flat_rows([small_g[n] for n in SMALL], 128, SMALL_ROWS)], "scatter_start_small")

    def sum8_fn(i, n, *v):
        t = v[0].astype(F32)
        for u in v[1:]:
            t = t + u.astype(F32)
        return t

    def sum_pieces(sent, got, name):
        rows, w = got.shape[1:]
        tm = 256 if rows % 256 == 0 else rows
        mine = lax.dynamic_slice(sent, (chip, core, 0, 0), (1, 1, rows, w)).reshape(rows, w)
        ins = [(mine, "row", w, 0)] + [(got.reshape(8 * rows, w), "row", w, 0, k * (rows // tm)) for k in range(1, 8)]
        return ew(sum8_fn, name, rows, tm, 1, ins, [(w, F32, w)])[0]

    grads, delta, new_m, new_v = {}, {}, {}, {}

    def finish(names, sent, got, tag):
        summed = swap_halves([sum_pieces(a, b, "sum_pieces_" + n) for a, b, n in zip(sent, got, names)], "swap_halves_" + tag)
        for n, t in zip(names, summed):
            shp = p[n].shape
            grads[n] = t.reshape(shp)
            r = [u.reshape(shp[1:]) for u in (p[n], grads[n], p["m_" + n], p["v_" + n])]
            delta[n], new_m[n], new_v[n] = [u.reshape(shp) for u in adamw(*r, "adamw_" + n)]

    finish(REST, *scatter_wait(scatter_rest, "scatter_wait_rest", after=token), "rest")
    finish(("w_in",), *scatter_wait(scatter_in, "scatter_wait_in", after=new_v[REST[-1]]), "w_in")
    (sent_small,), (got_small,) = scatter_wait(scatter_small, "scatter_wait_small", after=new_v["w_in"])
    got_small = own_slot(got_small, sent_small, 2 * chip + core)
    small_sum, = ew(sum8_fn, "sum_small", SMALL_ROWS, SMALL_ROWS, 1,
                    [(got_small.reshape(8 * SMALL_ROWS, 128), "row", 128, 0, k) for k in range(8)], [(128, F32, 128)])
    for n, g in zip(SMALL, _split_flat(small_sum, small_shapes)):
        if n in ("ssm_conv_w", "ffn_conv_w"):
            rows = p[n].shape[1]
            g = lax.dynamic_slice_in_dim(g, chip * rows, rows, axis=0)
        grads[n] = g.reshape(p[n].shape)

    shapes = [p[n].shape for n in SMALL]
    total = sum(int(np.prod(sh)) for sh in shapes)
    rows = -(-total // 1024) * 8
    packs = [_flat_rows([t[n] for n in SMALL], 128, rows)
             for t in (p, grads, {n: p["m_" + n] for n in SMALL}, {n: p["v_" + n] for n in SMALL})]
    for dst, t in zip((delta, new_m, new_v), adamw(*packs, "adamw_small")):
        for n, u in zip(SMALL, _split_flat(t, shapes)):
            dst[n] = u
    return (loss, grad_x[None], *[grads[n] for n in WEIGHTS], *[delta[n] for n in WEIGHTS],
            *[new_m[n] for n in WEIGHTS], *[new_v[n] for n in WEIGHTS])
```

```python
import numpy as np
import jax
import jax.numpy as jnp
from jax import lax
from jax.experimental import pallas as pl
from jax.experimental.pallas import tpu as pltpu

F32, BF16 = jnp.float32, jnp.bfloat16
MESH = pl.DeviceIdType.MESH
V7X_VMEM_LIMIT = 56 * 1024 * 1024

D = 1024
HD = 64
EPS = 1e-6
CHUNK = 128
D_FF = 2816
ROPE_DIM = 16
ROPE_THETA = 500000.0
PATTERN_DILATIONS = (1, 4, 16)
BAND = 64
SMALL_ROWS = 280
ADAM_LR, ADAM_B1, ADAM_B2, ADAM_EPS, ADAM_WD, ADAM_STEP = 0.001, 0.9, 0.999, 1e-08, 0.01, 10

NN = (((1,), (0,)), ((), ()))
NT = (((1,), (1,)), ((), ()))
TN = (((0,), (0,)), ((), ()))


def _pcall(body, **kw):
    return pl.pallas_call(body, **kw)


def _cparams(sem=None):
    return pltpu.CompilerParams(dimension_semantics=sem, vmem_limit_bytes=V7X_VMEM_LIMIT)


def _dot(a, b, dims=NN):
    return lax.dot_general(a, b, dims, preferred_element_type=F32)


def _pick(n, cap):
    if n <= cap:
        return n
    best = 0
    for t in range(128, cap + 1, 128):
        if n % t == 0:
            best = t
    assert best, (n, cap)
    return best


def _iota(shape, dim):
    return lax.broadcasted_iota(jnp.int32, shape, dim)


def _parts(x, n):
    out, r = [], x
    for _ in range(n):
        h = r.astype(BF16)
        out.append(h)
        r = r - h.astype(F32)
    return out


def _sigmoid(x):
    return 1.0 / (1.0 + jnp.exp(-x))


def _silu(x):
    return x * _sigmoid(x)


def _dsilu(x):
    s = _sigmoid(x)
    return s * (1.0 + x * (1.0 - s))


def matmul(a, b, mode, name, out_dtype=F32, after=None, b_k_off=0, n_cols=None):
    if mode == "nn":
        (m, k), (_, n) = a.shape, (b.shape[0], n_cols or b.shape[1])
    elif mode == "nt":
        (m, k), (n, _) = a.shape, b.shape
    else:
        (k, m), (_, n) = a.shape, b.shape
    tm, tn, tk = _pick(m, 1408), _pick(n, 1408), _pick(k, 1408)
    nk = k // tk
    dims = {"nn": NN, "nt": NT, "tn": TN}[mode]
    a_spec = pl.BlockSpec((tk, tm), lambda i, j, kk: (kk, i)) if mode == "tn" else pl.BlockSpec((tm, tk), lambda i, j, kk: (i, kk))
    b_spec = pl.BlockSpec((tn, tk), lambda i, j, kk: (j, kk + b_k_off)) if mode == "nt" else pl.BlockSpec((tk, tn), lambda i, j, kk: (kk, j))
    extra = [] if after is None else [after]

    def body(a_ref, b_ref, *rest):
        o_ref, acc = rest[len(extra)], rest[len(extra) + 1:]
        part = _dot(a_ref[...].astype(BF16), b_ref[...].astype(BF16), dims)
        if nk == 1:
            o_ref[...] = part.astype(o_ref.dtype)
            return
        acc_ref, kk = acc[0], pl.program_id(2)

        @pl.when(kk == 0)
        def _():
            acc_ref[...] = part

        @pl.when((kk > 0) & (kk < nk - 1))
        def _():
            acc_ref[...] += part

        @pl.when(kk == nk - 1)
        def _():
            o_ref[...] = (acc_ref[...] + part).astype(o_ref.dtype)

    return _pcall(
        body, name=name, grid=(m // tm, n // tn, nk), in_specs=[a_spec, b_spec] + [pl.BlockSpec(memory_space=pl.ANY)] * len(extra),
        out_specs=pl.BlockSpec((tm, tn), lambda i, j, kk: (i, j)),
        out_shape=jax.ShapeDtypeStruct((m, n), out_dtype),
        scratch_shapes=[pltpu.VMEM((tm, tn), F32)] if nk > 1 else [],
        compiler_params=_cparams(("parallel", "parallel", "arbitrary")),
    )(a, b, *extra)


def ew(fn, name, rows, tm, ncol, ins, outs, accs=()):
    nrow = rows // tm
    r8 = tm // 8
    in_specs, arrays = [], []
    for ent in ins:
        arr, kind, w, off = ent[:4]
        roff = ent[4] if len(ent) > 4 else 0
        if kind == "row":
            spec = pl.BlockSpec((tm, w), lambda j, i, off=off, roff=roff: (i + roff, j + off))
        elif kind == "const":
            spec = pl.BlockSpec((arr.shape[0], w), lambda j, i, off=off: (0, j + off))
        elif kind == "prev":
            spec = pl.BlockSpec((8, w), lambda j, i, off=off: (jnp.maximum(i * r8 - 1, 0), j + off))
        else:
            spec = pl.BlockSpec((8, w), lambda j, i, off=off: (jnp.minimum((i + 1) * r8, rows // 8 - 1), j + off))
        in_specs.append(spec)
        arrays.append(arr)
    out_specs = [pl.BlockSpec((tm, w), lambda j, i: (i, j)) for (_, _, w) in outs]
    out_shape = [jax.ShapeDtypeStruct((rows, c), dt) for (c, dt, _) in outs]
    out_specs += [pl.BlockSpec((1, w), lambda j, i: (0, j)) for (_, w) in accs]
    out_shape += [jax.ShapeDtypeStruct((1, c), F32) for (c, _) in accs]
    nin, nout = len(ins), len(outs)

    def body(*refs):
        i = pl.program_id(1)
        res = fn(i, nrow, *[r[...] for r in refs[:nin]])
        if not isinstance(res, (tuple, list)):
            res = (res,)
        for r, v in zip(refs[nin:nin + nout], res[:nout]):
            r[...] = v.astype(r.dtype)
        if accs:
            acc_refs = refs[nin + nout:]

            @pl.when(i == 0)
            def _():
                for r in acc_refs:
                    r[...] = jnp.zeros_like(r)

            for r, v in zip(acc_refs, res[nout:]):
                r[...] += v

    res = _pcall(
        body, name=name, grid=(ncol, nrow), in_specs=in_specs, out_specs=out_specs, out_shape=out_shape,
        compiler_params=_cparams(("parallel", "arbitrary")),
    )(*arrays)
    return res


def _shift_down(x, prev8, i):
    first = jnp.where(i == 0, 0.0, prev8[7:8, :])
    return jnp.where(_iota(x.shape, 0) == 0, first, pltpu.roll(x, 1, 0))


def _shift_up(x, next8, i, nrow):
    last = jnp.where(i == nrow - 1, 0.0, next8[0:1, :])
    return jnp.where(_iota(x.shape, 0) == x.shape[0] - 1, last, pltpu.roll(x, x.shape[0] - 1, 0))


def _colsum(x):
    return jnp.sum(x, axis=0, keepdims=True)


def _extend(x, prev8, next8, i, nrow):
    return jnp.concatenate([jnp.where(i == 0, 0.0, prev8), x, jnp.where(i == nrow - 1, 0.0, next8)], axis=0)


def _taps(xe):
    return pltpu.roll(xe, 1, 0), xe, pltpu.roll(xe, xe.shape[0] - 1, 0)


def _mid(xe):
    return xe[8:xe.shape[0] - 8]


def _conv3(w, b, taps):
    return w[0:1] * taps[0] + w[1:2] * taps[1] + w[2:3] * taps[2] + b


def _conv3_t(w, d_ext):
    t = _taps(d_ext)
    return _mid(w[0:1] * t[2] + w[1:2] * t[1] + w[2:3] * t[0])


def ffn_conv_bwd_fn(i, n, g, gp, gn, u, up_, un, wg, wu, bg, bu, da, dap, dan):
    gt, ut = _taps(_extend(g, gp, gn, i, n)), _taps(_extend(u, up_, un, i, n))
    dae = _extend(da, dap, dan, i, n)
    gate, upv = _conv3(wg, bg, gt), _conv3(wu, bu, ut)
    dg, du = dae * upv * _dsilu(gate), dae * _silu(gate)
    dgm, dum = _mid(dg), _mid(du)
    sums = [_colsum(dgm * _mid(t)) for t in gt] + [_colsum(dum * _mid(t)) for t in ut] + [_colsum(dgm), _colsum(dum)]
    return (_conv3_t(wg, dg), _conv3_t(wu, du)) + tuple(sums)


def silu_conv_bwd_fn(i, n, xv, xp, xn, w, b, da, dap, dan):
    xt = _taps(_extend(xv, xp, xn, i, n))
    du = _extend(da, dap, dan, i, n) * _dsilu(_conv3(w, b, xt))
    dum = _mid(du)
    return (_conv3_t(w, du),) + tuple(_colsum(dum * _mid(t)) for t in xt) + (_colsum(dum),)


def _rms_fwd(x, w):
    r = lax.rsqrt(jnp.mean(x * x, axis=-1, keepdims=True) + EPS)
    return x * r * w


def _rms_bwd(dy, x, w):
    r = lax.rsqrt(jnp.mean(x * x, axis=-1, keepdims=True) + EPS)
    xh = x * r
    dxh = dy * w
    dx = r * (dxh - xh * jnp.mean(dxh * xh, axis=-1, keepdims=True))
    return dx, _colsum(dy * xh)


def _rope_tables(s):
    half = ROPE_DIM // 2
    inv_freq = jnp.power(ROPE_THETA, -jnp.arange(half, dtype=F32) * 2.0 / ROPE_DIM)
    ang = jnp.arange(s, dtype=F32)[:, None] * inv_freq[None, :]
    cos, sin = jnp.cos(ang), jnp.sin(ang)
    one, zero = jnp.ones((s, HD - ROPE_DIM), F32), jnp.zeros((s, HD - ROPE_DIM), F32)
    z8 = jnp.zeros((s, half), F32)
    c = jnp.concatenate([cos, cos, one], axis=1)
    sa = jnp.concatenate([-sin, z8, zero], axis=1)
    sb = jnp.concatenate([z8, sin, zero], axis=1)
    return [jnp.tile(t, (1, 2)) for t in (c, sa, sb)]


ATTN_CHUNK = 2048


def _attn_plan(s):
    plan = []
    for d in PATTERN_DILATIONS:
        per_res = ATTN_CHUNK // d
        tq = min(128, per_res)
        plan.append((d, tq, min(s // d, tq + 2 * BAND), per_res // tq, s // d))
    return plan


def _rows(start, size, d):
    return pl.ds(start, size) if d == 1 else pl.ds(start, size, stride=d)


def _for_tiles(chunk, pat, fn):
    d, tq, win, nblk, seq_len = pat
    for b in range(nblk):
        t0 = chunk * (ATTN_CHUNK // d) + b * tq
        kloc = jnp.clip(t0 - BAND, 0, seq_len - win)
        valid = jnp.abs(kloc + _iota((tq, win), 1) - (t0 + _iota((tq, win), 0))) <= BAND
        valid = jnp.concatenate([valid, valid], axis=0)
        if d == 1:
            fn(b * tq, pl.multiple_of(kloc, BAND), valid)
        else:
            def step(r, carry, qoff=d * b * tq, koff=d * kloc, valid=valid):
                fn(qoff + r, koff + r, valid)
                return carry
            lax.fori_loop(0, d, step, 0, unroll=min(d, 8))


def _stack_heads(x, head0):
    zero = jnp.zeros_like(x)
    return jnp.concatenate([jnp.where(head0, x, zero), jnp.where(head0, zero, x)], axis=0)


def _rope_pair(x, c, sa, sb):
    n = x.shape[1]
    return x * c + pltpu.roll(x, n - 8, 1) * sa + pltpu.roll(x, 8, 1) * sb


def _rope_pair_t(dy, c, sa, sb):
    n = dy.shape[1]
    return dy * c + pltpu.roll(dy * sa, 8, 1) + pltpu.roll(dy * sb, n - 8, 1)


def _attn_specs(s):
    whole = lambda off: pl.BlockSpec((s, 128), lambda p, c: (0, off + p))
    table = pl.BlockSpec((s, 128), lambda p, c: (0, 0))
    chunk = pl.BlockSpec((ATTN_CHUNK, 128), lambda p, c: (c, p))
    return whole, table, chunk


def attn_fwd_all(proj, tabs, name):
    s = proj.shape[0]
    plan = _attn_plan(s)
    whole, table, chunk_spec = _attn_specs(s)

    def body(q_ref, k_ref, v_ref, c_ref, sa_ref, sb_ref, o_ref, lse_ref, qs, ks, acc_s, m_s, l_s):
        chunk = pl.program_id(1)

        @pl.when(chunk == 0)
        def _():
            qs[...] = _rope_pair(q_ref[...], c_ref[...], sa_ref[...], sb_ref[...]) * (HD ** -0.5)
            ks[...] = _rope_pair(k_ref[...], c_ref[...], sa_ref[...], sb_ref[...])

        base = pl.multiple_of(chunk * ATTN_CHUNK, ATTN_CHUNK)
        for pi, pat in enumerate(plan):
            d, tq, win = pat[:3]
            head0 = _iota((tq, 128), 1) < HD

            def tile(qrow, krow, valid, pi=pi, d=d, tq=tq, win=win, head0=head0):
                qv = qs[_rows(base + qrow, tq, d), :].astype(BF16)
                kw = ks[_rows(krow, win, d), :].astype(BF16)
                vw = v_ref[_rows(krow, win, d), :].astype(BF16)
                v_ones = jnp.concatenate([vw, jnp.ones_like(vw)], axis=1)
                sc = jnp.where(valid, _dot(_stack_heads(qv, head0), kw, NT), -1e30)
                mh = jnp.max(sc, axis=1, keepdims=True)
                pv = _dot(jnp.exp(sc - mh).astype(BF16), v_ones)
                acc_s[pi, _rows(qrow, tq, d), :] = jnp.where(head0, pv[:tq, :128], pv[tq:, :128])
                m_s[pi, _rows(qrow, tq, d), :] = jnp.where(head0, mh[:tq], mh[tq:])
                l_s[pi, _rows(qrow, tq, d), :] = jnp.where(head0, pv[:tq, 128:], pv[tq:, 128:])

            _for_tiles(chunk, pat, tile)
        m_all = jnp.maximum(jnp.maximum(m_s[0], m_s[1]), m_s[2])
        e = [jnp.exp(m_s[k] - m_all) for k in range(3)]
        den = e[0] * l_s[0] + e[1] * l_s[1] + e[2] * l_s[2]
        o_ref[...] = (e[0] * acc_s[0] + e[1] * acc_s[1] + e[2] * acc_s[2]) / den
        lse_ref[...] = m_all + jnp.log(den)

    stat = pltpu.VMEM((3, ATTN_CHUNK, 128), F32)
    return _pcall(
        body, name=name, grid=(D // 128, s // ATTN_CHUNK),
        in_specs=[whole(0), whole(8), whole(16), table, table, table], out_specs=[chunk_spec, chunk_spec],
        out_shape=[jax.ShapeDtypeStruct((s, D), F32)] * 2,
        scratch_shapes=[pltpu.VMEM((s, 128), F32), pltpu.VMEM((s, 128), F32), stat, stat, stat],
        compiler_params=_cparams(("parallel", "arbitrary")),
    )(proj, proj, proj, *tabs)


def attn_bwd_all(proj, tabs, dmix, o, lse, name):
    s = proj.shape[0]
    plan = _attn_plan(s)
    whole, table, chunk_spec = _attn_specs(s)
    nchunk = s // ATTN_CHUNK

    def body(q_ref, k_ref, v_ref, c_ref, sa_ref, sb_ref, do_ref, o_ref, lse_ref, dq_ref, dk_ref, dv_ref, qs, ks, aug0_s, aug1_s):
        chunk = pl.program_id(1)

        @pl.when(chunk == 0)
        def _():
            qs[...] = _rope_pair(q_ref[...], c_ref[...], sa_ref[...], sb_ref[...]) * (HD ** -0.5)
            ks[...] = _rope_pair(k_ref[...], c_ref[...], sa_ref[...], sb_ref[...])
            dk_ref[...] = jnp.zeros_like(dk_ref)
            dv_ref[...] = jnp.zeros_like(dv_ref)

        base = pl.multiple_of(chunk * ATTN_CHUNK, ATTN_CHUNK)
        prod = do_ref[...] * o_ref[...]
        first = _iota(prod.shape, 1) < HD
        delta = jnp.where(first, jnp.sum(jnp.where(first, prod, 0.0), axis=1, keepdims=True),
                          jnp.sum(jnp.where(first, 0.0, prod), axis=1, keepdims=True))
        lane = _iota(prod.shape, 1)

        def as_lanes(lse_h, delta_h):
            a, b = [u.astype(F32) for u in _parts(lse_h, 3)], [u.astype(F32) for u in _parts(delta_h, 3)]
            out = jnp.zeros_like(lse_h)
            for k, u in enumerate(a + b):
                out = jnp.where(lane == k, u, out)
            return out

        lsev = lse_ref[...]
        aug0_s[...] = as_lanes(lsev, delta)
        aug1_s[...] = as_lanes(pltpu.roll(lsev, HD, 1), pltpu.roll(delta, HD, 1))
        for pi, pat in enumerate(plan):
            d, tq, win = pat[:3]
            head0 = _iota((tq, 128), 1) < HD

            def tile(qrow, krow, valid, pi=pi, d=d, tq=tq, win=win, head0=head0):
                qv = qs[_rows(base + qrow, tq, d), :].astype(BF16)
                kw = ks[_rows(krow, win, d), :].astype(BF16)
                vw = v_ref[_rows(krow, win, d), :].astype(BF16)
                dob = do_ref[_rows(qrow, tq, d), :].astype(BF16)
                aug = jnp.concatenate([aug0_s[_rows(qrow, tq, d), :], aug1_s[_rows(qrow, tq, d), :]], axis=0).astype(BF16)
                klane = _iota((win, 128), 1)
                minus_lse = jnp.where(klane < 3, -1.0, 0.0).astype(BF16)
                minus_delta = jnp.where((klane >= 3) & (klane < 6), -1.0, 0.0).astype(BF16)
                q2, do2 = _stack_heads(qv, head0), _stack_heads(dob, head0)
                s_lse = _dot(jnp.concatenate([q2, aug], axis=1), jnp.concatenate([kw, minus_lse], axis=1), NT)
                dp_delta = _dot(jnp.concatenate([do2, aug], axis=1), jnp.concatenate([vw, minus_delta], axis=1), NT)
                p = jnp.where(valid, jnp.exp(s_lse), 0.0)
                ds = (p * dp_delta).astype(BF16)
                dq2 = _dot(ds, kw)
                dk = _dot(ds, q2, TN)
                dv = _dot(p.astype(BF16), do2, TN)
                dqv = jnp.where(head0, dq2[:tq], dq2[tq:])
                if pi == 0:
                    dq_ref[_rows(qrow, tq, d), :] = dqv
                else:
                    dq_ref[_rows(qrow, tq, d), :] += dqv
                dk_ref[_rows(krow, win, d), :] += dk
                dv_ref[_rows(krow, win, d), :] += dv

            _for_tiles(chunk, pat, tile)
        tab = [t[pl.ds(base, ATTN_CHUNK), :] for t in (c_ref, sa_ref, sb_ref)]
        dq_ref[...] = _rope_pair_t(dq_ref[...] * (HD ** -0.5), *tab)

        @pl.when(chunk == nchunk - 1)
        def _():
            dk_ref[...] = _rope_pair_t(dk_ref[...], c_ref[...], sa_ref[...], sb_ref[...])

    return _pcall(
        body, name=name, grid=(D // 128, nchunk),
        in_specs=[whole(0), whole(8), whole(16), table, table, table, chunk_spec, chunk_spec, chunk_spec],
        out_specs=[chunk_spec, whole(0), whole(0)], out_shape=[jax.ShapeDtypeStruct((s, D), F32)] * 3,
        scratch_shapes=[pltpu.VMEM((s, 128), F32), pltpu.VMEM((s, 128), F32)] + [pltpu.VMEM((ATTN_CHUNK, 128), F32)] * 2,
        compiler_params=_cparams(("parallel", "arbitrary")),
    )(proj, proj, proj, *tabs, dmix, o, lse)


def _ssd_common(x_ref, b_ref, c_ref, dt_ref, dtt_ref, a_ref, ar_ref, rev):
    ii, jj = _iota((CHUNK, CHUNK), 0), _iota((CHUNK, CHUNK), 1)
    low = jj >= ii if rev else jj <= ii
    x, dtx = x_ref[...], dt_ref[...]
    bm, cm = b_ref[...].astype(BF16), c_ref[...].astype(BF16)
    a = dtx * a_ref[...]
    arow = dtt_ref[0] * ar_ref[0]
    lowb = low.astype(BF16)
    cs = _dot(lowb, jnp.concatenate(_parts(a, 3), axis=1))
    cs = cs[:, :128] + cs[:, 128:256] + cs[:, 256:]
    csr = _dot(jnp.concatenate([p.astype(F32) for p in _parts(arow, 3)], axis=0).astype(BF16), lowb, NT)
    csr = csr[0:8] + csr[8:16] + csr[16:24]
    last = 0 if rev else CHUNK - 1
    tot = cs[last:last + 1, :]
    xdt = x * dtx
    cb = _dot(cm, bm, NT)
    lmats = [jnp.exp(jnp.where(low, cs[:, HD * h:HD * h + 1] - csr[h:h + 1, :], -1e30)) for h in range(2)]
    return dict(x=x, dtx=dtx, bm=bm, cm=cm, a=a, cs=cs, tot=tot, xdt=xdt, cb=cb, lmats=lmats, low=low, last=last)


SSD_SUB = 8


def _ssd_specs(s, rev_order):
    nblk, rows = s // (SSD_SUB * CHUNK), SSD_SUB * CHUNK
    ci = (lambda c: nblk - 1 - c) if rev_order else (lambda c: c)
    tile = lambda off, div: pl.BlockSpec((rows, 128), lambda p, c: (ci(c), off + p // div))
    common = [tile(0, 1), tile(8, 2), tile(12, 2), tile(0, 1),
              pl.BlockSpec((1, 8, rows), lambda p, c: (p, 0, ci(c))),
              pl.BlockSpec((1, 128), lambda p, c: (0, p)),
              pl.BlockSpec((1, 8, 128), lambda p, c: (p, 0, 0))]
    hs = pl.BlockSpec((1, SSD_SUB, CHUNK, 128), lambda p, c: (p, ci(c), 0, 0))
    return nblk, common, tile(0, 1), hs


def _chunk_rows(ref, j):
    return ref.at[pl.ds(j * CHUNK, CHUNK), :]


def _ssd_chunk(refs, j):
    return [_chunk_rows(r, j) for r in refs[:4]] + [refs[4].at[:, :, pl.ds(j * CHUNK, CHUNK)], refs[5], refs[6]]


def _ssd_args(xbc, t):
    return [xbc, xbc, xbc, t["dt_exp"], t["dtt"], t["a_exp"], t["a_rows"]]


def ssd_fwd(xbc, dirs, name):
    s = xbc.shape[0]
    nd = len(dirs)
    specs = [_ssd_specs(s, t["rev"]) for t in dirs]
    nck = specs[0][0]

    def one(rev, x_ref, b_ref, c_ref, dt_ref, dtt_ref, a_ref, ar_ref, y_ref, hs_ref, h_scr):
        v = _ssd_common(x_ref, b_ref, c_ref, dt_ref, dtt_ref, a_ref, ar_ref, rev)
        xdtb = v["xdt"].astype(BF16)
        yd = _dot(jnp.concatenate([v["cb"] * v["lmats"][h] for h in range(2)], axis=0).astype(BF16), xdtb)
        h_in = h_scr[...]
        hs_ref[0, 0] = h_in
        y_off = _dot(v["cm"], h_in.astype(BF16)) * jnp.exp(v["cs"])
        y_ref[...] = jnp.where(_iota((CHUNK, 128), 1) < HD, yd[:CHUNK], yd[CHUNK:]) + y_off
        decay = jnp.exp(v["tot"] - v["cs"])
        h_scr[...] = jnp.exp(v["tot"]) * h_in + _dot(v["bm"], (v["xdt"] * decay).astype(BF16), TN)

    def body(*refs):
        @pl.when(pl.program_id(1) == 0)
        def _():
            for k in range(nd):
                refs[9 * nd + k][...] = jnp.zeros((CHUNK, 128), F32)

        for k, t in enumerate(dirs):
            y_ref, hs_ref = refs[7 * nd + 2 * k:7 * nd + 2 * k + 2]
            for j in (range(SSD_SUB)[::-1] if t["rev"] else range(SSD_SUB)):
                one(t["rev"], *_ssd_chunk(refs[7 * k:7 * k + 7], j), _chunk_rows(y_ref, j), hs_ref.at[:, pl.ds(j, 1)], refs[9 * nd + k])

    res = _pcall(
        body, name=name, grid=(8, nck), in_specs=[sp for t in specs for sp in t[1]],
        out_specs=[sp for t in specs for sp in (t[2], t[3])],
        out_shape=[jax.ShapeDtypeStruct((s, D), F32), jax.ShapeDtypeStruct((8, s // CHUNK, CHUNK, 128), F32)] * nd,
        scratch_shapes=[pltpu.VMEM((CHUNK, 128), F32)] * nd, compiler_params=_cparams(("parallel", "arbitrary")),
    )(*[a for t in dirs for a in _ssd_args(xbc, t)])
    return [(res[2 * k], res[2 * k + 1]) for k in range(nd)]


def ssd_bwd(xbc, dirs, dy, name):
    s = xbc.shape[0]
    nd = len(dirs)
    specs = [_ssd_specs(s, not t["rev"]) for t in dirs]
    nck = specs[0][0]

    def one(rev, x_ref, b_ref, c_ref, dt_ref, dtt_ref, a_ref, ar_ref, hs_ref, dy_ref,
            dx_ref, ddt_ref, db_ref, dc_ref, dal_ref, dh_scr):
        v = _ssd_common(x_ref, b_ref, c_ref, dt_ref, dtt_ref, a_ref, ar_ref, rev)
        bm, cm, cs, tot, xdt = v["bm"], v["cm"], v["cs"], v["tot"], v["xdt"]
        h_in, dh = hs_ref[0, 0], dh_scr[...]
        dyv = dy_ref[...]
        dyb = dyv.astype(BF16)
        etot, decay, ecs = jnp.exp(tot), jnp.exp(tot - cs), jnp.exp(cs)
        xdtb = xdt.astype(BF16)
        xdec = xdt * decay
        dch = (dyv * ecs).astype(BF16)
        hb, dhb = h_in.astype(BF16), dh.astype(BF16)
        y_off = _dot(cm, hb) * ecs
        dc = _dot(dch, hb, NT)
        dh_y = _dot(cm, dch, TN)
        dxdec = _dot(bm, dhb)
        db = _dot(xdec.astype(BF16), dhb, NT)
        state_term = xdec * dxdec
        dtot = _colsum(dh * h_in) * etot + _colsum(state_term)
        head0 = _iota((CHUNK, 128), 1) < HD
        ii, jj = _iota((CHUNK, CHUNK), 0), _iota((CHUNK, CHUNK), 1)
        low_t = jj <= ii if rev else jj >= ii
        not_low_t = (~low_t).astype(BF16)
        g = _dot(_stack_heads(dyb, head0), xdtb, NT)
        gl = [g[:CHUNK] * v["lmats"][0], g[CHUNK:] * v["lmats"][1]]
        dcb = gl[0] + gl[1]
        dxd = _dot(jnp.concatenate([v["cb"] * v["lmats"][h] for h in range(2)], axis=1).astype(BF16), dyb, TN)
        dxd = jnp.where(head0, dxd[:CHUNK], dxd[CHUNK:])
        w = _dot(not_low_t, jnp.concatenate([gl[h] * v["cb"] for h in range(2)], axis=0).astype(BF16), NT)
        da_l = [jnp.sum(jnp.where(low_t, w[:, CHUNK * h:CHUNK * h + CHUNK], 0.0), axis=1, keepdims=True) for h in range(2)]
        dxdt = dxdec * decay + dxd
        dcbb = dcb.astype(BF16)
        dc_ref[...] = dc + _dot(dcbb, bm)
        db_ref[...] = db + _dot(dcbb, cm, TN)
        dcs = dyv * y_off - state_term + jnp.where(_iota((CHUNK, 128), 0) == v["last"], dtot, 0.0)
        lowb = v["low"].astype(BF16)
        da = _dot(lowb, jnp.concatenate(_parts(dcs, 2), axis=1), TN)
        da = da[:, :128] + da[:, 128:]
        seg = ((ii < HD) == (jj < HD)).astype(BF16)
        sums = _dot(jnp.concatenate(_parts(da, 2) + _parts(dxdt * v["x"], 2), axis=0), seg)
        da = sums[:CHUNK] + sums[CHUNK:2 * CHUNK] + jnp.where(head0, da_l[0], da_l[1])
        ddt_x = sums[2 * CHUNK:3 * CHUNK] + sums[3 * CHUNK:]
        dx_ref[...] = dxdt * v["dtx"]
        ddt_ref[...] = ddt_x + da * a_ref[...]
        dal_ref[0] += _colsum(da * v["a"])
        dh_scr[...] = etot * dh + dh_y

    def body(*refs):
        @pl.when(pl.program_id(1) == 0)
        def _():
            for k in range(nd):
                refs[14 * nd + k][...] = jnp.zeros((CHUNK, 128), F32)
                refs[9 * nd + 5 * k + 4][...] = jnp.zeros((1, 8, 128), F32)

        for k, t in enumerate(dirs):
            ins, outs = refs[9 * k:9 * k + 9], refs[9 * nd + 5 * k:9 * nd + 5 * k + 5]
            for j in (range(SSD_SUB) if t["rev"] else range(SSD_SUB)[::-1]):
                one(t["rev"], *_ssd_chunk(ins[:7], j), ins[7].at[:, pl.ds(j, 1)], _chunk_rows(ins[8], j),
                    *[_chunk_rows(r, j) for r in outs[:4]], outs[4], refs[14 * nd + k])

    acc_spec = pl.BlockSpec((1, 8, 128), lambda p, c: (p, 0, 0))
    res = _pcall(
        body, name=name, grid=(8, nck), in_specs=[sp for t in specs for sp in t[1] + [t[3], t[2]]],
        out_specs=[sp for t in specs for sp in [t[2]] * 4 + [acc_spec]],
        out_shape=([jax.ShapeDtypeStruct((s, D), F32)] * 4 + [jax.ShapeDtypeStruct((8, 8, 128), F32)]) * nd,
        scratch_shapes=[pltpu.VMEM((CHUNK, 128), F32)] * nd, compiler_params=_cparams(("parallel", "arbitrary")),
    )(*[a for t in dirs for a in _ssd_args(xbc, t) + [t["hs"], dy]])
    return [res[5 * k:5 * k + 5] for k in range(nd)]


def _group_norm_stats(g):
    r = [lax.rsqrt(jnp.mean(g[:, 256 * k:256 * k + 256] ** 2, axis=-1, keepdims=True) + EPS) for k in range(4)]
    grp = _iota(g.shape, 1) // 256
    return jnp.where(grp == 0, r[0], jnp.where(grp == 1, r[1], jnp.where(grp == 2, r[2], r[3])))


def _group_mean(t):
    m = [jnp.mean(t[:, 256 * k:256 * k + 256], axis=-1, keepdims=True) for k in range(4)]
    grp = _iota(t.shape, 1) // 256
    return jnp.where(grp == 0, m[0], jnp.where(grp == 1, m[1], jnp.where(grp == 2, m[2], m[3])))


def _mesh_pos():
    return lax.axis_index("x"), lax.axis_index("y"), lax.axis_index("c")


HBM = pl.BlockSpec(memory_space=pltpu.HBM)
SEM = pl.BlockSpec(memory_space=pltpu.SEMAPHORE)
EFFECT = pltpu.SideEffectType.DATAFLOW_SIDE_EFFECTING


def _hbm(t):
    return pltpu.with_memory_space_constraint(t, pltpu.HBM)


def _other_chips(x, y):
    return [(1 - x, y), (x, 1 - y), (1 - x, 1 - y)]


def _peer(x, y, c, m):
    return x ^ (m >> 2), y ^ ((m >> 1) & 1), c ^ (m & 1)


def gather_start(srcs_a, srcs_b, halved=()):
    srcs = [_hbm(t) for t in list(srcs_a) + list(srcs_b)]
    n, na = len(srcs), len(srcs_a)
    half = [k in halved for k in range(n)]
    lands = [_hbm(lax.empty((4,) + (t.shape[1:] if half[k] else t.shape), t.dtype)) for k, t in enumerate(srcs)]

    def body(*refs):
        src, land = refs[:n], refs[n:2 * n]
        sems = refs[2 * n:2 * n + 4]
        x, y, c = _mesh_pos()
        for k in range(n):
            for j, (px, py) in enumerate(_other_chips(x, y)):
                send, recv, idx = (sems[0], sems[1], 3 * k + j) if k < na else (sems[2], sems[3], 3 * (k - na) + j)
                pltpu.make_async_remote_copy(src_ref=src[k].at[c] if half[k] else src[k], dst_ref=land[k].at[2 * x + y], send_sem=send.at[idx],
                                             recv_sem=recv.at[idx], device_id=(px, py, c), device_id_type=MESH).start()

    sem_a, sem_b = pltpu.SemaphoreType.DMA((3 * na,)), pltpu.SemaphoreType.DMA((3 * (n - na),))
    res = _pcall(
        body, name="gather_start", in_specs=[HBM] * (2 * n), out_specs=[SEM] * 4 + [HBM] * (2 * n),
        out_shape=[sem_a, sem_a, sem_b, sem_b] + [pltpu.HBM(t.shape, t.dtype) for t in srcs + lands],
        input_output_aliases={i: 4 + i for i in range(2 * n)},
        compiler_params=pltpu.CompilerParams(has_side_effects=EFFECT),
    )(*srcs, *lands)
    thru_src, thru_land = res[4:4 + n], res[4 + n:]
    return ((res[0], res[1], thru_src[:na], thru_land[:na], half[:na]), (res[2], res[3], thru_src[na:], thru_land[na:], half[na:]))


def gather_wait(group, name, after=None):
    send, recv, srcs, lands, half = group
    n = len(srcs)

    def body(*refs):
        src, land, send_ref, recv_ref = refs[:n], refs[n:2 * n], refs[2 * n], refs[2 * n + 1]
        x, y, c = _mesh_pos()
        for j, (px, py) in enumerate(_other_chips(x, y)):
            for k in range(n):
                cp = pltpu.make_async_remote_copy(src_ref=src[k].at[0] if half[k] else src[k], dst_ref=land[k].at[2 * px + py], send_sem=send_ref.at[3 * k + j],
                                                  recv_sem=recv_ref.at[3 * k + j], device_id=(px, py, c), device_id_type=MESH)
                cp.wait_send()
                cp.wait_recv()

    extra = [] if after is None else [after]
    res = _pcall(
        body, name=name, in_specs=[HBM] * (2 * n) + [SEM, SEM] + [pl.BlockSpec(memory_space=pl.ANY)] * len(extra),
        out_specs=[HBM] * (2 * n), out_shape=[pltpu.HBM(t.shape, t.dtype) for t in list(srcs) + list(lands)],
        input_output_aliases={i: i for i in range(2 * n)}, compiler_params=pltpu.CompilerParams(has_side_effects=EFFECT),
    )(*srcs, *lands, send, recv, *extra)
    return res[:n], res[n:]


def scatter_start(pieces, smalls, name):
    srcs = [_hbm(t) for t in list(pieces) + list(smalls)]
    n, npc = len(srcs), len(pieces)
    lands = [_hbm(lax.empty((8,) + (t.shape[2:] if k < npc else t.shape), t.dtype)) for k, t in enumerate(srcs)]

    def body(*refs):
        src, land, send, recv = refs[:n], refs[n:2 * n], refs[2 * n], refs[2 * n + 1]
        token = refs[-1]
        x, y, c = _mesh_pos()
        for m in range(1, 8):
            px, py, pc = _peer(x, y, c, m)
            for k in range(n):
                s_ref = src[k].at[2 * px + py, pc] if k < npc else src[k]
                d_ref = land[k].at[m] if k < npc else land[k].at[4 * x + 2 * y + c]
                pltpu.make_async_remote_copy(src_ref=s_ref, dst_ref=d_ref, send_sem=send.at[7 * k + m - 1], recv_sem=recv.at[7 * k + m - 1],
                                             device_id=(px, py, pc), device_id_type=MESH).start()
        token[...] = jnp.zeros_like(token)

    sem = pltpu.SemaphoreType.DMA((7 * n,))
    res = _pcall(
        body, name=name, in_specs=[HBM] * (2 * n),
        out_specs=[SEM, SEM] + [HBM] * (2 * n) + [pl.BlockSpec(memory_space=pltpu.VMEM)],
        out_shape=[sem, sem] + [pltpu.HBM(t.shape, t.dtype) for t in srcs + lands] + [jax.ShapeDtypeStruct((8, 128), F32)],
        input_output_aliases={i: 2 + i for i in range(2 * n)},
        compiler_params=pltpu.CompilerParams(has_side_effects=EFFECT),
    )(*srcs, *lands)
    return (res[0], res[1], res[2:2 + n], res[2 + n:2 + 2 * n], npc), res[-1]


def scatter_wait(group, name, after=None):
    send, recv, srcs, lands, npc = group
    n = len(srcs)

    def body(*refs):
        src, land, send_ref, recv_ref = refs[:n], refs[n:2 * n], refs[2 * n], refs[2 * n + 1]
        x, y, c = _mesh_pos()
        for m in range(1, 8):
            px, py, pc = _peer(x, y, c, m)
            for k in range(n):
                s_ref = src[k].at[0, 0] if k < npc else src[k]
                d_ref = land[k].at[m] if k < npc else land[k].at[4 * px + 2 * py + pc]
                cp = pltpu.make_async_remote_copy(src_ref=s_ref, dst_ref=d_ref, send_sem=send_ref.at[7 * k + m - 1],
                                                  recv_sem=recv_ref.at[7 * k + m - 1], device_id=(px, py, pc), device_id_type=MESH)
                cp.wait_send()
                cp.wait_recv()

    extra = [] if after is None else [after]
    res = _pcall(
        body, name=name, in_specs=[HBM] * (2 * n) + [SEM, SEM] + [pl.BlockSpec(memory_space=pl.ANY)] * len(extra),
        out_specs=[HBM] * (2 * n), out_shape=[pltpu.HBM(t.shape, t.dtype) for t in list(srcs) + list(lands)],
        input_output_aliases={i: i for i in range(2 * n)}, compiler_params=pltpu.CompilerParams(has_side_effects=EFFECT),
    )(*srcs, *lands, send, recv, *extra)
    return res[:n], res[n:]


def swap_halves(pieces, name):
    n = len(pieces)
    whole = pl.BlockSpec(memory_space=pltpu.VMEM)

    def body(*refs):
        p_refs, o_refs, send_sems, recv_sems, local_sems = refs[:n], refs[n:2 * n], refs[2 * n], refs[2 * n + 1], refs[2 * n + 2]
        x, y, c = _mesh_pos()
        local = [pltpu.make_async_copy(p_refs[k], o_refs[k].at[c], local_sems.at[k]) for k in range(n)]
        for cp in local:
            cp.start()

        def copy(k, slot):
            return pltpu.make_async_remote_copy(src_ref=p_refs[k], dst_ref=o_refs[k].at[slot], send_sem=send_sems.at[k],
                                                recv_sem=recv_sems.at[k], device_id=(x, y, 1 - c), device_id_type=MESH)

        for k in range(n):
            copy(k, c).start()
        for k in range(n):
            copy(k, 1 - c).wait_recv()
        for k in range(n):
            copy(k, c).wait_send()
        for cp in local:
            cp.wait()

    return _pcall(
        body, name=name, in_specs=[whole] * n, out_specs=[whole] * n,
        out_shape=[jax.ShapeDtypeStruct((2,) + t.shape, t.dtype) for t in pieces],
        scratch_shapes=[pltpu.SemaphoreType.DMA((n,)), pltpu.SemaphoreType.DMA((n,)), pltpu.SemaphoreType.DMA((n,))],
        compiler_params=_cparams(),
    )(*pieces)


def adamw(w, g, m, v, name):
    rows, cols = w.shape
    tm = rows
    for t in (256, 352, 128, 144, 64, 32, 16, 8):
        if rows % t == 0:
            tm = t
            break

    def fn(i, nrow, wv, gv, mv, vv):
        mn = ADAM_B1 * mv + (1.0 - ADAM_B1) * gv
        vn = ADAM_B2 * vv + (1.0 - ADAM_B2) * (gv * gv)
        m_hat = mn / (1.0 - ADAM_B1 ** ADAM_STEP)
        v_hat = vn / (1.0 - ADAM_B2 ** ADAM_STEP)
        delta = -ADAM_LR * (m_hat / (jnp.sqrt(v_hat) + ADAM_EPS) + ADAM_WD * wv)
        return delta, mn, vn

    return ew(fn, name, rows, tm, 1, [(t, "row", cols, 0) for t in (w, g, m, v)], [(cols, F32, cols)] * 3)


REST = ("w_out", "w_up", "w_down")
SMALL = ("norm1_w", "ssm_conv_w", "ssm_conv_b", "a_log_f", "a_log_b", "dt_bias_f", "dt_bias_b", "d_skip",
         "ssm_norm_w", "norm2_w", "ffn_conv_w", "ffn_conv_b", "final_norm_w")
WEIGHTS = ("norm1_w", "w_in", "ssm_conv_w", "ssm_conv_b", "a_log_f", "a_log_b", "dt_bias_f", "dt_bias_b", "d_skip",
           "ssm_norm_w", "w_out", "norm2_w", "w_up", "ffn_conv_w", "ffn_conv_b", "w_down", "final_norm_w")
INPUTS = ("x",) + WEIGHTS + ("loss_target",) + tuple("m_" + n for n in WEIGHTS) + tuple("v_" + n for n in WEIGHTS)


def _flat_rows(parts, width, rows):
    flat = jnp.concatenate([p.reshape(-1) for p in parts])
    return jnp.pad(flat, (0, rows * width - flat.shape[0])).reshape(rows, width)


def _split_flat(flat, shapes):
    out, pos = [], 0
    flat = flat.reshape(-1)
    for shp in shapes:
        n = int(np.prod(shp))
        out.append(flat[pos:pos + n].reshape(shp))
        pos += n
    return out


def _col_shards(t, nshard):
    r, c = t.shape
    return t.reshape(r, nshard, c // nshard).transpose(1, 0, 2)


def _row_shards(t, nshard):
    r, c = t.shape
    return t.reshape(nshard, r // nshard, c)


def kernel(x, norm1_w, w_in, ssm_conv_w, ssm_conv_b, a_log_f, a_log_b, dt_bias_f, dt_bias_b, d_skip, ssm_norm_w, w_out, norm2_w, w_up, ffn_conv_w, ffn_conv_b, w_down, final_norm_w, loss_target, m_norm1_w, m_w_in, m_ssm_conv_w, m_ssm_conv_b, m_a_log_f, m_a_log_b, m_dt_bias_f, m_dt_bias_b, m_d_skip, m_ssm_norm_w, m_w_out, m_norm2_w, m_w_up, m_ffn_conv_w, m_ffn_conv_b, m_w_down, m_final_norm_w, v_norm1_w, v_w_in, v_ssm_conv_w, v_ssm_conv_b, v_a_log_f, v_a_log_b, v_dt_bias_f, v_dt_bias_b, v_d_skip, v_ssm_norm_w, v_w_out, v_norm2_w, v_w_up, v_ffn_conv_w, v_ffn_conv_b, v_w_down, v_final_norm_w):
    p = dict(zip(INPUTS, (x, norm1_w, w_in, ssm_conv_w, ssm_conv_b, a_log_f, a_log_b, dt_bias_f, dt_bias_b, d_skip, ssm_norm_w, w_out, norm2_w, w_up, ffn_conv_w, ffn_conv_b, w_down, final_norm_w, loss_target, m_norm1_w, m_w_in, m_ssm_conv_w, m_ssm_conv_b, m_a_log_f, m_a_log_b, m_dt_bias_f, m_dt_bias_b, m_d_skip, m_ssm_norm_w, m_w_out, m_norm2_w, m_w_up, m_ffn_conv_w, m_ffn_conv_b, m_w_down, m_final_norm_w, v_norm1_w, v_w_in, v_ssm_conv_w, v_ssm_conv_b, v_a_log_f, v_a_log_b, v_dt_bias_f, v_dt_bias_b, v_d_skip, v_ssm_norm_w, v_w_out, v_norm2_w, v_w_up, v_ffn_conv_w, v_ffn_conv_b, v_w_down, v_final_norm_w)))
    x = p["x"][0]
    tgt = p["loss_target"][0]
    s = x.shape[0]
    chip = 2 * lax.axis_index("x") + lax.axis_index("y")

    own_slot = lambda land, mine, slot: lax.dynamic_update_slice_in_dim(land, mine[None], slot, axis=0)
    core = lax.axis_index("c")
    src_in = p["w_in"][0].astype(BF16).reshape(2, D // 2, -1)
    src_rest = [p[n][0].astype(BF16) for n in REST]
    small_w = _flat_rows([p["ssm_conv_w"][0], p["ffn_conv_w"][0]], 128, 48)
    gather_in, gather_rest = gather_start([src_in, small_w], src_rest, halved=(0,))
    (src_in, small_w), (wg_in, sg) = gather_wait(gather_in, "gather_wait_in")
    wg_in = own_slot(wg_in, lax.dynamic_index_in_dim(src_in, core, 0, keepdims=False), chip)
    wg_in, = swap_halves([wg_in], "swap_w_in_rows")
    w_in = wg_in.transpose(0, 2, 1, 3).reshape(D, -1)
    sg = own_slot(sg, small_w, chip)
    n_in = w_in.shape[1]
    n_main = 6 * D
    w_dt = jnp.pad(w_in[:, n_main:], ((0, 0), (0, 128 - (n_in - n_main))))
    sgf = sg.reshape(4, -1)
    n_sc, n_fc = p["ssm_conv_w"].shape[1], p["ffn_conv_w"].shape[1]
    ssm_cw = sgf[:, :n_sc * 3].reshape(-1, 3).T
    ffn_cw = sgf[:, n_sc * 3:(n_sc + n_fc) * 3].reshape(-1, 3).T
    ssm_cb, ffn_cb = p["ssm_conv_b"], p["ffn_conv_b"]
    n1w, n2w, snw, fnw = p["norm1_w"], p["norm2_w"], p["ssm_norm_w"], p["final_norm_w"].reshape(1, D)

    h1, = ew(lambda i, n, xv, w: _rms_fwd(xv, w), "rms1", s, 256, 1,
             [(x, "row", D, 0), (n1w, "const", D, 0)], [(D, BF16, D)])
    proj = matmul(h1, w_in, "nn", "in_proj", n_cols=n_main)
    proj_dt = matmul(h1, w_dt, "nn", "in_proj_dt")
    tabs = _rope_tables(s)
    attn, lse = attn_fwd_all(proj, tabs, "attn_fwd")

    def conv_silu_fn(i, n, xv, xp, xn, w, b):
        return _silu(w[0:1] * _shift_down(xv, xp, i) + w[1:2] * xv + w[2:3] * _shift_up(xv, xn, i, n) + b)

    xbc_act, = ew(conv_silu_fn, "ssm_conv", s, 256, 2,
                  [(proj, "row", D, 4), (proj, "prev", D, 4), (proj, "next", D, 4),
                   (ssm_cw, "const", D, 0), (ssm_cb, "const", D, 0)], [(2 * D, F32, D)])
    dt_bias = jnp.pad(jnp.concatenate([p["dt_bias_f"], p["dt_bias_b"]], axis=1), ((0, 0), (0, 96)))

    lanes_of = np.arange(128)[:, None] == np.arange(D)[None, :] // HD
    spread = [jnp.asarray(np.roll(lanes_of, 16 * k, axis=0), BF16) for k in range(2)]

    def softplus_fn(i, n, r, b, ef, eb):
        t = r + b
        dtv = jnp.maximum(t, 0.0) + jnp.log(1.0 + jnp.exp(-jnp.abs(t)))
        parts = _parts(dtv, 3)
        return dtv, sum(_dot(q, ef) for q in parts), sum(_dot(q, eb) for q in parts)

    dt, dt_exp_f, dt_exp_b = ew(softplus_fn, "dt_softplus", s, 512, 1,
                                [(proj_dt, "row", 128, 0), (dt_bias, "const", 128, 0), (spread[0], "const", D, 0), (spread[1], "const", D, 0)],
                                [(128, F32, 128), (D, F32, D), (D, F32, D)])
    d_exp = jnp.repeat(p["d_skip"], HD, axis=1)
    ssd = []
    for k, (a_log, rev) in enumerate(((p["a_log_f"], False), (p["a_log_b"], True))):
        dt_k = dt[:, 16 * k:16 * k + 16]
        a_head = -jnp.exp(a_log)
        dt_exp = (dt_exp_f, dt_exp_b)[k]
        dtt = jnp.pad(dt_k.T.reshape(8, 2, s), ((0, 0), (0, 6), (0, 0)))
        a_exp = jnp.repeat(a_head, HD, axis=1)
        a_rows = jnp.broadcast_to(jnp.pad(a_head.reshape(8, 2), ((0, 0), (0, 6)))[:, :, None], (8, 8, 128))
        ssd.append(dict(dt_exp=dt_exp, dtt=dtt, a_exp=a_exp, a_rows=a_rows, rev=rev))
    for t, (y_k, hs_k) in zip(ssd, ssd_fwd(xbc_act, ssd, "ssd_fwd")):
        t["y"], t["hs"] = y_k, hs_k

    def gate_fn(i, n, yf, yb, xs, z, dsk, w):
        g = (yf + yb + dsk * xs) * _silu(z)
        return g * _group_norm_stats(g) * w

    ssm_out, = ew(gate_fn, "ssm_gate_norm", s, 256, 1,
                  [(ssd[0]["y"], "row", D, 0), (ssd[1]["y"], "row", D, 0), (xbc_act, "row", D, 0), (proj, "row", D, 3),
                   (d_exp, "const", D, 0), (snw, "const", D, 0)], [(D, F32, D)])
    mix = jnp.concatenate([attn, ssm_out], axis=1).astype(BF16)
    src_rest, wg_rest = gather_wait(gather_rest, "gather_wait_rest", after=mix)
    wg_rest = [own_slot(land, mine, chip) for land, mine in zip(wg_rest, src_rest)]
    w_out = wg_rest[0].reshape(-1, D)
    w_up = wg_rest[1].transpose(1, 0, 2).reshape(D, -1)
    w_down = wg_rest[2].reshape(-1, D)
    mix_w = matmul(mix, w_out, "nn", "out_proj")

    def res_rms_fn(i, n, xv, mw, w):
        x1v = xv + mw
        return x1v, _rms_fwd(x1v, w)

    x1, h2 = ew(res_rms_fn, "res_rms2", s, 256, 1, [(x, "row", D, 0), (mix_w, "row", D, 0), (n2w, "const", D, 0)],
                [(D, F32, D), (D, BF16, D)])
    hw = matmul(h2, w_up, "nn", "ffn_up")
    fw = D_FF // 2
    nfb = D_FF // fw
    ffn_conv_ins = [(hw, "row", fw, 0), (hw, "prev", fw, 0), (hw, "next", fw, 0),
                    (hw, "row", fw, nfb), (hw, "prev", fw, nfb), (hw, "next", fw, nfb),
                    (ffn_cw, "const", fw, 0), (ffn_cw, "const", fw, nfb), (ffn_cb, "const", fw, 0), (ffn_cb, "const", fw, nfb)]

    def ffn_conv(i, n, g, gp, gn, u, up_, un, wg_, wu, bg, bu):
        gs = (_shift_down(g, gp, i), g, _shift_up(g, gn, i, n))
        us = (_shift_down(u, up_, i), u, _shift_up(u, un, i, n))
        gate = wg_[0:1] * gs[0] + wg_[1:2] * gs[1] + wg_[2:3] * gs[2] + bg
        upv = wu[0:1] * us[0] + wu[1:2] * us[1] + wu[2:3] * us[2] + bu
        return gate, upv, gs, us

    def glu_fn(i, n, *blocks):
        gate, upv, _, _ = ffn_conv(i, n, *blocks)
        return _silu(gate) * upv

    act, = ew(glu_fn, "ffn_conv_glu", s, 256, nfb, ffn_conv_ins, [(D_FF, BF16, fw)])
    ffn = matmul(act, w_down, "nn", "ffn_down")

    def head_fn(i, n, x1v, fv, tv, w):
        x2 = x1v + fv
        r = lax.rsqrt(jnp.mean(x2 * x2, axis=-1, keepdims=True) + EPS)
        xh = x2 * r
        diff = xh * w - tv
        loss = 0.5 * jnp.sum(jnp.mean(diff * diff, axis=-1, keepdims=True), axis=0, keepdims=True)
        dout = diff * (1.0 / D)
        dxh = dout * w
        dx2 = r * (dxh - xh * jnp.mean(dxh * xh, axis=-1, keepdims=True))
        return dx2, jnp.broadcast_to(loss, (1, 128)), _colsum(dout * xh)

    dx2, loss_acc, g_fnw = ew(head_fn, "loss_head", s, 256, 1,
                              [(x1, "row", D, 0), (ffn, "row", D, 0), (tgt, "row", D, 0), (fnw, "const", D, 0)],
                              [(D, F32, D)], [(128, 128), (D, D)])
    loss = lax.psum(loss_acc[0, 0], ("x", "y", "c"))

    g_w_down = matmul(act, dx2, "tn", "d_w_down")
    dact = matmul(dx2, w_down, "nt", "d_act")

    res = ew(ffn_conv_bwd_fn, "ffn_conv_glu_bwd", s, 256, nfb,
             ffn_conv_ins + [(dact, "row", fw, 0), (dact, "prev", fw, 0), (dact, "next", fw, 0)],
             [(D_FF, F32, fw)] * 2, [(D_FF, fw)] * 8)
    dhw_g, dhw_u = res[0], res[1]
    g_ffn_cw = jnp.concatenate([jnp.concatenate(res[2:5], axis=0), jnp.concatenate(res[5:8], axis=0)], axis=1).T
    g_ffn_cb = jnp.concatenate([res[8], res[9]], axis=1)

    g_w_up = jnp.concatenate([matmul(h2, dhw_g, "tn", "d_w_up_gate"), matmul(h2, dhw_u, "tn", "d_w_up_up")], axis=1)
    dh2_a = matmul(dhw_g, w_up, "nt", "d_h2_gate")
    dh2_b = matmul(dhw_u, w_up, "nt", "d_h2_up", b_k_off=D_FF // _pick(D_FF, 1408))

    def res_rms_bwd_fn(i, n, dres, da, db, xin, w):
        dx, dw = _rms_bwd(da + db, xin, w)
        return dres + dx, dw

    dx1, g_n2w = ew(res_rms_bwd_fn, "res_rms2_bwd", s, 256, 1,
                    [(dx2, "row", D, 0), (dh2_a, "row", D, 0), (dh2_b, "row", D, 0), (x1, "row", D, 0), (n2w, "const", D, 0)],
                    [(D, F32, D)], [(D, D)])

    g_w_out = matmul(mix, dx1, "tn", "d_w_out")
    to_pieces = lambda t: t.astype(BF16).reshape(4, 2, t.shape[1] // 2, t.shape[2])
    shards_rest = [_row_shards(g_w_out, 4), _col_shards(g_w_up, 4), _row_shards(g_w_down, 4)]
    scatter_rest, token = scatter_start([to_pieces(t) for t in shards_rest], [], "scatter_start_rest")
    dmix = matmul(dx1, w_out, "nt", "d_mix", after=token)
    ii, jj = np.arange(D)[:, None] // HD, np.arange(D)[None, :] // HD
    seg = jnp.asarray(ii == jj, BF16)

    def gate_bwd_fn(i, n, dout, yf, yb, xs, z, dsk, w, segm):
        yt = yf + yb + dsk * xs
        sz = _silu(z)
        g = yt * sz
        r = _group_norm_stats(g)
        gh = g * r
        dn = dout * w
        dg = r * (dn - gh * _group_mean(dn * gh))
        dy = dg * sz
        dsk_lane = jnp.broadcast_to(_colsum(dy * xs), (8, D))
        return dy, dg * yt * _dsilu(z), _colsum(dout * gh), sum(_dot(q, segm) for q in _parts(dsk_lane, 2))[0:1]

    dy, dz, g_snw, g_dskip_l = ew(
        gate_bwd_fn, "ssm_gate_norm_bwd", s, 256, 1,
        [(dmix, "row", D, 1), (ssd[0]["y"], "row", D, 0), (ssd[1]["y"], "row", D, 0), (xbc_act, "row", D, 0),
         (proj, "row", D, 3), (d_exp, "const", D, 0), (snw, "const", D, 0), (seg, "const", D, 0)],
        [(D, F32, D), (D, BF16, D)], [(D, D)] * 2)
    sb = ssd_bwd(xbc_act, ssd, dy, "ssd_bwd")

    def dxbc_act_fn(i, n, dxf, dxb, dyv, dsk, dbf, dbb, dcf, dcb_):
        db, dc = dbf + dbb, dcf + dcb_
        db = [db[:, 256 * g:256 * g + 128] + db[:, 256 * g + 128:256 * g + 256] for g in range(4)]
        dc = [dc[:, 256 * g:256 * g + 128] + dc[:, 256 * g + 128:256 * g + 256] for g in range(4)]
        return jnp.concatenate([dxf + dxb + dyv * dsk] + db + dc, axis=1)

    dxbc_act, = ew(dxbc_act_fn, "d_xbc_act", s, 256, 1,
                   [(sb[0][0], "row", D, 0), (sb[1][0], "row", D, 0), (dy, "row", D, 0), (d_exp, "const", D, 0),
                    (sb[0][2], "row", D, 0), (sb[1][2], "row", D, 0), (sb[0][3], "row", D, 0), (sb[1][3], "row", D, 0)],
                   [(2 * D, F32, 2 * D)])

    res = ew(silu_conv_bwd_fn, "ssm_conv_bwd", s, 256, 2,
             [(proj, "row", D, 4), (proj, "prev", D, 4), (proj, "next", D, 4), (ssm_cw, "const", D, 0), (ssm_cb, "const", D, 0),
              (dxbc_act, "row", D, 0), (dxbc_act, "prev", D, 0), (dxbc_act, "next", D, 0)], [(2 * D, BF16, D)], [(2 * D, D)] * 4)
    dxbc = res[0]
    g_ssm_cw = jnp.concatenate(res[1:4], axis=0).T
    g_ssm_cb = res[4]
    pick = np.zeros((D, 128), np.float32)
    pick[np.arange(16) * HD, np.arange(16)] = 1.0
    picks = [jnp.asarray(np.roll(pick, 16 * k, axis=1), BF16) for k in range(2)]

    def dt_bwd_fn(i, n, ddf, ddb, r, b, pf, pb):
        dd = sum(_dot(q, pf) for q in _parts(ddf, 3)) + sum(_dot(q, pb) for q in _parts(ddb, 3))
        dr = dd * _sigmoid(r + b)
        return dr, _colsum(dr)

    dproj_dt, g_dt_bias = ew(dt_bwd_fn, "dt_softplus_bwd", s, 512, 1,
                             [(sb[0][1], "row", D, 0), (sb[1][1], "row", D, 0), (proj_dt, "row", 128, 0), (dt_bias, "const", 128, 0),
                              (picks[0], "const", 128, 0), (picks[1], "const", 128, 0)],
                             [(128, F32, 128)], [(128, 128)])
    g_a_log = [t[4][:, 0, ::HD].reshape(1, 16) for t in sb]

    dq, dk, dv = attn_bwd_all(proj, tabs, dmix, attn, lse, "attn_bwd")

    dproj = jnp.concatenate([t.astype(BF16) for t in (dq, dk, dv, dz, dxbc)], axis=1)
    g_w_in = jnp.concatenate([matmul(h1, dproj, "tn", "d_w_in"), matmul(h1, dproj_dt, "tn", "d_w_in_dt")[:, :n_in - n_main]], axis=1)
    scatter_in, token = scatter_start([to_pieces(_col_shards(g_w_in, 4))], [], "scatter_start_in")
    dh1_a = matmul(dproj, w_in, "nt", "d_h1", after=token)
    dh1_b = matmul(dproj_dt, w_dt, "nt", "d_h1_dt")
    grad_x, g_n1w = ew(res_rms_bwd_fn, "rms1_bwd", s, 256, 1,
                       [(dx1, "row", D, 0), (dh1_a, "row", D, 0), (dh1_b, "row", D, 0), (x, "row", D, 0), (n1w, "const", D, 0)],
                       [(D, F32, D)], [(D, D)])

    small_g = {"norm1_w": g_n1w, "ssm_conv_w": g_ssm_cw, "ssm_conv_b": g_ssm_cb, "a_log_f": g_a_log[0], "a_log_b": g_a_log[1],
               "dt_bias_f": g_dt_bias[:, :16], "dt_bias_b": g_dt_bias[:, 16:32], "d_skip": g_dskip_l[:, ::HD],
               "ssm_norm_w": g_snw, "norm2_w": g_n2w, "ffn_conv_w": g_ffn_cw, "ffn_conv_b": g_ffn_cb, "final_norm_w": g_fnw}
    small_shapes = [small_g[n].shape for n in SMALL]
    scatter_small, token = scatter_start([], [_flat_rows([small_g[n] for n in SMALL], 128, SMALL_ROWS)], "scatter_start_small")

    def sum8_fn(i, n, *v):
        t = v[0].astype(F32)
        for u in v[1:]:
            t = t + u.astype(F32)
        return t

    def sum_pieces(sent, got, name):
        rows, w = got.shape[1:]
        tm = 256 if rows % 256 == 0 else rows
        mine = lax.dynamic_slice(sent, (chip, core, 0, 0), (1, 1, rows, w)).reshape(rows, w)
        ins = [(mine, "row", w, 0)] + [(got.reshape(8 * rows, w), "row", w, 0, k * (rows // tm)) for k in range(1, 8)]
        return ew(sum8_fn, name, rows, tm, 1, ins, [(w, F32, w)])[0]

    grads, delta, new_m, new_v = {}, {}, {}, {}

    def finish(names, sent, got, tag):
        summed = swap_halves([sum_pieces(a, b, "sum_pieces_" + n) for a, b, n in zip(sent, got, names)], "swap_halves_" + tag)
        for n, t in zip(names, summed):
            shp = p[n].shape
            grads[n] = t.reshape(shp)
            r = [u.reshape(shp[1:]) for u in (p[n], grads[n], p["m_" + n], p["v_" + n])]
            delta[n], new_m[n], new_v[n] = [u.reshape(shp) for u in adamw(*r, "adamw_" + n)]

    finish(REST, *scatter_wait(scatter_rest, "scatter_wait_rest", after=token), "rest")
    finish(("w_in",), *scatter_wait(scatter_in, "scatter_wait_in", after=new_v[REST[-1]]), "w_in")
    (sent_small,), (got_small,) = scatter_wait(scatter_small, "scatter_wait_small", after=new_v["w_in"])
    got_small = own_slot(got_small, sent_small, 2 * chip + core)
    small_sum, = ew(sum8_fn, "sum_small", SMALL_ROWS, SMALL_ROWS, 1,
                    [(got_small.reshape(8 * SMALL_ROWS, 128), "row", 128, 0, k) for k in range(8)], [(128, F32, 128)])
    for n, g in zip(SMALL, _split_flat(small_sum, small_shapes)):
        if n in ("ssm_conv_w", "ffn_conv_w"):
            rows = p[n].shape[1]
            g = lax.dynamic_slice_in_dim(g, chip * rows, rows, axis=0)
        grads[n] = g.reshape(p[n].shape)

    shapes = [p[n].shape for n in SMALL]
    total = sum(int(np.prod(sh)) for sh in shapes)
    rows = -(-total // 1024) * 8
    packs = [_flat_rows([t[n] for n in SMALL], 128, rows)
             for t in (p, grads, {n: p["m_" + n] for n in SMALL}, {n: p["v_" + n] for n in SMALL})]
    for dst, t in zip((delta, new_m, new_v), adamw(*packs, "adamw_small")):
        for n, u in zip(SMALL, _split_flat(t, shapes)):
            dst[n] = u
    return (loss, grad_x[None], *[grads[n] for n in WEIGHTS], *[delta[n] for n in WEIGHTS],
            *[new_m[n] for n in WEIGHTS], *[new_v[n] for n in WEIGHTS])
```

```python
import numpy as np
import jax
import jax.numpy as jnp
from jax import lax
from jax.experimental import pallas as pl
from jax.experimental.pallas import tpu as pltpu

F32, BF16 = jnp.float32, jnp.bfloat16
MESH = pl.DeviceIdType.MESH
V7X_VMEM_LIMIT = 56 * 1024 * 1024

D = 1024
HD = 64
EPS = 1e-6
CHUNK = 128
D_FF = 2816
ROPE_DIM = 16
ROPE_THETA = 500000.0
PATTERN_DILATIONS = (1, 4, 16)
BAND = 64
SMALL_ROWS = 280
ADAM_LR, ADAM_B1, ADAM_B2, ADAM_EPS, ADAM_WD, ADAM_STEP = 0.001, 0.9, 0.999, 1e-08, 0.01, 10

NN = (((1,), (0,)), ((), ()))
NT = (((1,), (1,)), ((), ()))
TN = (((0,), (0,)), ((), ()))


def _pcall(body, **kw):
    return pl.pallas_call(body, **kw)


def _cparams(sem=None):
    return pltpu.CompilerParams(dimension_semantics=sem, vmem_limit_bytes=V7X_VMEM_LIMIT)


def _dot(a, b, dims=NN):
    return lax.dot_general(a, b, dims, preferred_element_type=F32)


def _pick(n, cap):
    if n <= cap:
        return n
    best = 0
    for t in range(128, cap + 1, 128):
        if n % t == 0:
            best = t
    assert best, (n, cap)
    return best


def _iota(shape, dim):
    return lax.broadcasted_iota(jnp.int32, shape, dim)


def _parts(x, n):
    out, r = [], x
    for _ in range(n):
        h = r.astype(BF16)
        out.append(h)
        r = r - h.astype(F32)
    return out


def _sigmoid(x):
    return 1.0 / (1.0 + jnp.exp(-x))


def _silu(x):
    return x * _sigmoid(x)


def _dsilu(x):
    s = _sigmoid(x)
    return s * (1.0 + x * (1.0 - s))


def matmul(a, b, mode, name, out_dtype=F32, after=None, b_k_off=0, n_cols=None):
    if mode == "nn":
        (m, k), (_, n) = a.shape, (b.shape[0], n_cols or b.shape[1])
    elif mode == "nt":
        (m, k), (n, _) = a.shape, b.shape
    else:
        (k, m), (_, n) = a.shape, b.shape
    tm, tn, tk = _pick(m, 1408), _pick(n, 1408), _pick(k, 1408)
    nk = k // tk
    dims = {"nn": NN, "nt": NT, "tn": TN}[mode]
    a_spec = pl.BlockSpec((tk, tm), lambda i, j, kk: (kk, i)) if mode == "tn" else pl.BlockSpec((tm, tk), lambda i, j, kk: (i, kk))
    b_spec = pl.BlockSpec((tn, tk), lambda i, j, kk: (j, kk + b_k_off)) if mode == "nt" else pl.BlockSpec((tk, tn), lambda i, j, kk: (kk, j))
    extra = [] if after is None else [after]

    def body(a_ref, b_ref, *rest):
        o_ref, acc = rest[len(extra)], rest[len(extra) + 1:]
        part = _dot(a_ref[...].astype(BF16), b_ref[...].astype(BF16), dims)
        if nk == 1:
            o_ref[...] = part.astype(o_ref.dtype)
            return
        acc_ref, kk = acc[0], pl.program_id(2)

        @pl.when(kk == 0)
        def _():
            acc_ref[...] = part

        @pl.when((kk > 0) & (kk < nk - 1))
        def _():
            acc_ref[...] += part

        @pl.when(kk == nk - 1)
        def _():
            o_ref[...] = (acc_ref[...] + part).astype(o_ref.dtype)

    return _pcall(
        body, name=name, grid=(m // tm, n // tn, nk), in_specs=[a_spec, b_spec] + [pl.BlockSpec(memory_space=pl.ANY)] * len(extra),
        out_specs=pl.BlockSpec((tm, tn), lambda i, j, kk: (i, j)),
        out_shape=jax.ShapeDtypeStruct((m, n), out_dtype),
        scratch_shapes=[pltpu.VMEM((tm, tn), F32)] if nk > 1 else [],
        compiler_params=_cparams(("parallel", "parallel", "arbitrary")),
    )(a, b, *extra)


def ew(fn, name, rows, tm, ncol, ins, outs, accs=()):
    nrow = rows // tm
    r8 = tm // 8
    in_specs, arrays = [], []
    for ent in ins:
        arr, kind, w, off = ent[:4]
        roff = ent[4] if len(ent) > 4 else 0
        if kind == "row":
            spec = pl.BlockSpec((tm, w), lambda j, i, off=off, roff=roff: (i + roff, j + off))
        elif kind == "const":
            spec = pl.BlockSpec((arr.shape[0], w), lambda j, i, off=off: (0, j + off))
        elif kind == "prev":
            spec = pl.BlockSpec((8, w), lambda j, i, off=off: (jnp.maximum(i * r8 - 1, 0), j + off))
        else:
            spec = pl.BlockSpec((8, w), lambda j, i, off=off: (jnp.minimum((i + 1) * r8, rows // 8 - 1), j + off))
        in_specs.append(spec)
        arrays.append(arr)
    out_specs = [pl.BlockSpec((tm, w), lambda j, i: (i, j)) for (_, _, w) in outs]
    out_shape = [jax.ShapeDtypeStruct((rows, c), dt) for (c, dt, _) in outs]
    out_specs += [pl.BlockSpec((1, w), lambda j, i: (0, j)) for (_, w) in accs]
    out_shape += [jax.ShapeDtypeStruct((1, c), F32) for (c, _) in accs]
    nin, nout = len(ins), len(outs)

    def body(*refs):
        i = pl.program_id(1)
        res = fn(i, nrow, *[r[...] for r in refs[:nin]])
        if not isinstance(res, (tuple, list)):
            res = (res,)
        for r, v in zip(refs[nin:nin + nout], res[:nout]):
            r[...] = v.astype(r.dtype)
        if accs:
            acc_refs = refs[nin + nout:]

            @pl.when(i == 0)
            def _():
                for r in acc_refs:
                    r[...] = jnp.zeros_like(r)

            for r, v in zip(acc_refs, res[nout:]):
                r[...] += v

    res = _pcall(
        body, name=name, grid=(ncol, nrow), in_specs=in_specs, out_specs=out_specs, out_shape=out_shape,
        compiler_params=_cparams(("parallel", "arbitrary")),
    )(*arrays)
    return res


def _shift_down(x, prev8, i):
    first = jnp.where(i == 0, 0.0, prev8[7:8, :])
    return jnp.where(_iota(x.shape, 0) == 0, first, pltpu.roll(x, 1, 0))


def _shift_up(x, next8, i, nrow):
    last = jnp.where(i == nrow - 1, 0.0, next8[0:1, :])
    return jnp.where(_iota(x.shape, 0) == x.shape[0] - 1, last, pltpu.roll(x, x.shape[0] - 1, 0))


def _colsum(x):
    return jnp.sum(x, axis=0, keepdims=True)


def _extend(x, prev8, next8, i, nrow):
    return jnp.concatenate([jnp.where(i == 0, 0.0, prev8), x, jnp.where(i == nrow - 1, 0.0, next8)], axis=0)


def _taps(xe):
    return pltpu.roll(xe, 1, 0), xe, pltpu.roll(xe, xe.shape[0] - 1, 0)


def _mid(xe):
    return xe[8:xe.shape[0] - 8]


def _conv3(w, b, taps):
    return w[0:1] * taps[0] + w[1:2] * taps[1] + w[2:3] * taps[2] + b


def _conv3_t(w, d_ext):
    t = _taps(d_ext)
    return _mid(w[0:1] * t[2] + w[1:2] * t[1] + w[2:3] * t[0])


def ffn_conv_bwd_fn(i, n, g, gp, gn, u, up_, un, wg, wu, bg, bu, da, dap, dan):
    gt, ut = _taps(_extend(g, gp, gn, i, n)), _taps(_extend(u, up_, un, i, n))
    dae = _extend(da, dap, dan, i, n)
    gate, upv = _conv3(wg, bg, gt), _conv3(wu, bu, ut)
    dg, du = dae * upv * _dsilu(gate), dae * _silu(gate)
    dgm, dum = _mid(dg), _mid(du)
    sums = [_colsum(dgm * _mid(t)) for t in gt] + [_colsum(dum * _mid(t)) for t in ut] + [_colsum(dgm), _colsum(dum)]
    return (_conv3_t(wg, dg), _conv3_t(wu, du)) + tuple(sums)


def silu_conv_bwd_fn(i, n, xv, xp, xn, w, b, da, dap, dan):
    xt = _taps(_extend(xv, xp, xn, i, n))
    du = _extend(da, dap, dan, i, n) * _dsilu(_conv3(w, b, xt))
    dum = _mid(du)
    return (_conv3_t(w, du),) + tuple(_colsum(dum * _mid(t)) for t in xt) + (_colsum(dum),)


def _rms_fwd(x, w):
    r = lax.rsqrt(jnp.mean(x * x, axis=-1, keepdims=True) + EPS)
    return x * r * w


def _rms_bwd(dy, x, w):
    r = lax.rsqrt(jnp.mean(x * x, axis=-1, keepdims=True) + EPS)
    xh = x * r
    dxh = dy * w
    dx = r * (dxh - xh * jnp.mean(dxh * xh, axis=-1, keepdims=True))
    return dx, _colsum(dy * xh)


def _rope_tables(s):
    half = ROPE_DIM // 2
    inv_freq = jnp.power(ROPE_THETA, -jnp.arange(half, dtype=F32) * 2.0 / ROPE_DIM)
    ang = jnp.arange(s, dtype=F32)[:, None] * inv_freq[None, :]
    cos, sin = jnp.cos(ang), jnp.sin(ang)
    one, zero = jnp.ones((s, HD - ROPE_DIM), F32), jnp.zeros((s, HD - ROPE_DIM), F32)
    z8 = jnp.zeros((s, half), F32)
    c = jnp.concatenate([cos, cos, one], axis=1)
    sa = jnp.concatenate([-sin, z8, zero], axis=1)
    sb = jnp.concatenate([z8, sin, zero], axis=1)
    return [jnp.tile(t, (1, 2)) for t in (c, sa, sb)]


ATTN_CHUNK = 2048


def _attn_plan(s):
    plan = []
    for d in PATTERN_DILATIONS:
        per_res = ATTN_CHUNK // d
        tq = min(128, per_res)
        plan.append((d, tq, min(s // d, tq + 2 * BAND), per_res // tq, s // d))
    return plan


def _rows(start, size, d):
    return pl.ds(start, size) if d == 1 else pl.ds(start, size, stride=d)


def _for_tiles(chunk, pat, fn):
    d, tq, win, nblk, seq_len = pat
    for b in range(nblk):
        t0 = chunk * (ATTN_CHUNK // d) + b * tq
        kloc = jnp.clip(t0 - BAND, 0, seq_len - win)
        valid = jnp.abs(kloc + _iota((tq, win), 1) - (t0 + _iota((tq, win), 0))) <= BAND
        valid = jnp.concatenate([valid, valid], axis=0)
        if d == 1:
            fn(b * tq, pl.multiple_of(kloc, BAND), valid)
        else:
            def step(r, carry, qoff=d * b * tq, koff=d * kloc, valid=valid):
                fn(qoff + r, koff + r, valid)
                return carry
            lax.fori_loop(0, d, step, 0, unroll=min(d, 8))


def _stack_heads(x, head0):
    zero = jnp.zeros_like(x)
    return jnp.concatenate([jnp.where(head0, x, zero), jnp.where(head0, zero, x)], axis=0)


def _rope_pair(x, c, sa, sb):
    n = x.shape[1]
    return x * c + pltpu.roll(x, n - 8, 1) * sa + pltpu.roll(x, 8, 1) * sb


def _rope_pair_t(dy, c, sa, sb):
    n = dy.shape[1]
    return dy * c + pltpu.roll(dy * sa, 8, 1) + pltpu.roll(dy * sb, n - 8, 1)


def _attn_specs(s):
    whole = lambda off: pl.BlockSpec((s, 128), lambda p, c: (0, off + p))
    table = pl.BlockSpec((s, 128), lambda p, c: (0, 0))
    chunk = pl.BlockSpec((ATTN_CHUNK, 128), lambda p, c: (c, p))
    return whole, table, chunk


def attn_fwd_all(proj, tabs, name):
    s = proj.shape[0]
    plan = _attn_plan(s)
    whole, table, chunk_spec = _attn_specs(s)

    def body(q_ref, k_ref, v_ref, c_ref, sa_ref, sb_ref, o_ref, lse_ref, qs, ks, acc_s, m_s, l_s):
        chunk = pl.program_id(1)

        @pl.when(chunk == 0)
        def _():
            qs[...] = _rope_pair(q_ref[...], c_ref[...], sa_ref[...], sb_ref[...]) * (HD ** -0.5)
            ks[...] = _rope_pair(k_ref[...], c_ref[...], sa_ref[...], sb_ref[...])

        base = pl.multiple_of(chunk * ATTN_CHUNK, ATTN_CHUNK)
        for pi, pat in enumerate(plan):
            d, tq, win = pat[:3]
            head0 = _iota((tq, 128), 1) < HD

            def tile(qrow, krow, valid, pi=pi, d=d, tq=tq, win=win, head0=head0):
                qv = qs[_rows(base + qrow, tq, d), :].astype(BF16)
                kw = ks[_rows(krow, win, d), :].astype(BF16)
                vw = v_ref[_rows(krow, win, d), :].astype(BF16)
                v_ones = jnp.concatenate([vw, jnp.ones_like(vw)], axis=1)
                sc = jnp.where(valid, _dot(_stack_heads(qv, head0), kw, NT), -1e30)
                mh = jnp.max(sc, axis=1, keepdims=True)
                pv = _dot(jnp.exp(sc - mh).astype(BF16), v_ones)
                acc_s[pi, _rows(qrow, tq, d), :] = jnp.where(head0, pv[:tq, :128], pv[tq:, :128])
                m_s[pi, _rows(qrow, tq, d), :] = jnp.where(head0, mh[:tq], mh[tq:])
                l_s[pi, _rows(qrow, tq, d), :] = jnp.where(head0, pv[:tq, 128:], pv[tq:, 128:])

            _for_tiles(chunk, pat, tile)
        m_all = jnp.maximum(jnp.maximum(m_s[0], m_s[1]), m_s[2])
        e = [jnp.exp(m_s[k] - m_all) for k in range(3)]
        den = e[0] * l_s[0] + e[1] * l_s[1] + e[2] * l_s[2]
        o_ref[...] = (e[0] * acc_s[0] + e[1] * acc_s[1] + e[2] * acc_s[2]) / den
        lse_ref[...] = m_all + jnp.log(den)

    stat = pltpu.VMEM((3, ATTN_CHUNK, 128), F32)
    return _pcall(
        body, name=name, grid=(D // 128, s // ATTN_CHUNK),
        in_specs=[whole(0), whole(8), whole(16), table, table, table], out_specs=[chunk_spec, chunk_spec],
        out_shape=[jax.ShapeDtypeStruct((s, D), F32)] * 2,
        scratch_shapes=[pltpu.VMEM((s, 128), F32), pltpu.VMEM((s, 128), F32), stat, stat, stat],
        compiler_params=_cparams(("parallel", "arbitrary")),
    )(proj, proj, proj, *tabs)


def attn_bwd_all(proj, tabs, dmix, o, lse, name):
    s = proj.shape[0]
    plan = _attn_plan(s)
    whole, table, chunk_spec = _attn_specs(s)
    nchunk = s // ATTN_CHUNK

    def body(q_ref, k_ref, v_ref, c_ref, sa_ref, sb_ref, do_ref, o_ref, lse_ref, dq_ref, dk_ref, dv_ref, qs, ks, aug0_s, aug1_s):
        chunk = pl.program_id(1)

        @pl.when(chunk == 0)
        def _():
            qs[...] = _rope_pair(q_ref[...], c_ref[...], sa_ref[...], sb_ref[...]) * (HD ** -0.5)
            ks[...] = _rope_pair(k_ref[...], c_ref[...], sa_ref[...], sb_ref[...])
            dk_ref[...] = jnp.zeros_like(dk_ref)
            dv_ref[...] = jnp.zeros_like(dv_ref)

        base = pl.multiple_of(chunk * ATTN_CHUNK, ATTN_CHUNK)
        prod = do_ref[...] * o_ref[...]
        first = _iota(prod.shape, 1) < HD
        delta = jnp.where(first, jnp.sum(jnp.where(first, prod, 0.0), axis=1, keepdims=True),
                          jnp.sum(jnp.where(first, 0.0, prod), axis=1, keepdims=True))
        lane = _iota(prod.shape, 1)

        def as_lanes(lse_h, delta_h):
            a, b = [u.astype(F32) for u in _parts(lse_h, 3)], [u.astype(F32) for u in _parts(delta_h, 3)]
            out = jnp.zeros_like(lse_h)
            for k, u in enumerate(a + b):
                out = jnp.where(lane == k, u, out)
            return out

        lsev = lse_ref[...]
        aug0_s[...] = as_lanes(lsev, delta)
        aug1_s[...] = as_lanes(pltpu.roll(lsev, HD, 1), pltpu.roll(delta, HD, 1))
        for pi, pat in enumerate(plan):
            d, tq, win = pat[:3]
            head0 = _iota((tq, 128), 1) < HD

            def tile(qrow, krow, valid, pi=pi, d=d, tq=tq, win=win, head0=head0):
                qv = qs[_rows(base + qrow, tq, d), :].astype(BF16)
                kw = ks[_rows(krow, win, d), :].astype(BF16)
                vw = v_ref[_rows(krow, win, d), :].astype(BF16)
                dob = do_ref[_rows(qrow, tq, d), :].astype(BF16)
                aug = jnp.concatenate([aug0_s[_rows(qrow, tq, d), :], aug1_s[_rows(qrow, tq, d), :]], axis=0).astype(BF16)
                klane = _iota((win, 128), 1)
                minus_lse = jnp.where(klane < 3, -1.0, 0.0).astype(BF16)
                minus_delta = jnp.where((klane >= 3) & (klane < 6), -1.0, 0.0).astype(BF16)
                q2, do2 = _stack_heads(qv, head0), _stack_heads(dob, head0)
                s_lse = _dot(jnp.concatenate([q2, aug], axis=1), jnp.concatenate([kw, minus_lse], axis=1), NT)
                dp_delta = _dot(jnp.concatenate([do2, aug], axis=1), jnp.concatenate([vw, minus_delta], axis=1), NT)
                p = jnp.where(valid, jnp.exp(s_lse), 0.0)
                ds = (p * dp_delta).astype(BF16)
                dq2 = _dot(ds, kw)
                dk = _dot(ds, q2, TN)
                dv = _dot(p.astype(BF16), do2, TN)
                dqv = jnp.where(head0, dq2[:tq], dq2[tq:])
                if pi == 0:
                    dq_ref[_rows(qrow, tq, d), :] = dqv
                else:
                    dq_ref[_rows(qrow, tq, d), :] += dqv
                dk_ref[_rows(krow, win, d), :] += dk
                dv_ref[_rows(krow, win, d), :] += dv

            _for_tiles(chunk, pat, tile)
        tab = [t[pl.ds(base, ATTN_CHUNK), :] for t in (c_ref, sa_ref, sb_ref)]
        dq_ref[...] = _rope_pair_t(dq_ref[...] * (HD ** -0.5), *tab)

        @pl.when(chunk == nchunk - 1)
        def _():
            dk_ref[...] = _rope_pair_t(dk_ref[...], c_ref[...], sa_ref[...], sb_ref[...])

    return _pcall(
        body, name=name, grid=(D // 128, nchunk),
        in_specs=[whole(0), whole(8), whole(16), table, table, table, chunk_spec, chunk_spec, chunk_spec],
        out_specs=[chunk_spec, whole(0), whole(0)], out_shape=[jax.ShapeDtypeStruct((s, D), F32)] * 3,
        scratch_shapes=[pltpu.VMEM((s, 128), F32), pltpu.VMEM((s, 128), F32)] + [pltpu.VMEM((ATTN_CHUNK, 128), F32)] * 2,
        compiler_params=_cparams(("parallel", "arbitrary")),
    )(proj, proj, proj, *tabs, dmix, o, lse)


def _ssd_common(x_ref, b_ref, c_ref, dt_ref, dtt_ref, a_ref, ar_ref, rev):
    ii, jj = _iota((CHUNK, CHUNK), 0), _iota((CHUNK, CHUNK), 1)
    low = jj >= ii if rev else jj <= ii
    x, dtx = x_ref[...], dt_ref[...]
    bm, cm = b_ref[...].astype(BF16), c_ref[...].astype(BF16)
    a = dtx * a_ref[...]
    arow = dtt_ref[0] * ar_ref[0]
    lowb = low.astype(BF16)
    cs = _dot(lowb, jnp.concatenate(_parts(a, 3), axis=1))
    cs = cs[:, :128] + cs[:, 128:256] + cs[:, 256:]
    csr = _dot(jnp.concatenate([p.astype(F32) for p in _parts(arow, 3)], axis=0).astype(BF16), lowb, NT)
    csr = csr[0:8] + csr[8:16] + csr[16:24]
    last = 0 if rev else CHUNK - 1
    tot = cs[last:last + 1, :]
    xdt = x * dtx
    cb = _dot(cm, bm, NT)
    lmats = [jnp.exp(jnp.where(low, cs[:, HD * h:HD * h + 1] - csr[h:h + 1, :], -1e30)) for h in range(2)]
    return dict(x=x, dtx=dtx, bm=bm, cm=cm, a=a, cs=cs, tot=tot, xdt=xdt, cb=cb, lmats=lmats, low=low, last=last)


SSD_SUB = 8


def _ssd_specs(s, rev_order):
    nblk, rows = s // (SSD_SUB * CHUNK), SSD_SUB * CHUNK
    ci = (lambda c: nblk - 1 - c) if rev_order else (lambda c: c)
    tile = lambda off, div: pl.BlockSpec((rows, 128), lambda p, c: (ci(c), off + p // div))
    common = [tile(0, 1), tile(8, 2), tile(12, 2), tile(0, 1),
              pl.BlockSpec((1, 8, rows), lambda p, c: (p, 0, ci(c))),
              pl.BlockSpec((1, 128), lambda p, c: (0, p)),
              pl.BlockSpec((1, 8, 128), lambda p, c: (p, 0, 0))]
    hs = pl.BlockSpec((1, SSD_SUB, CHUNK, 128), lambda p, c: (p, ci(c), 0, 0))
    return nblk, common, tile(0, 1), hs


def _chunk_rows(ref, j):
    return ref.at[pl.ds(j * CHUNK, CHUNK), :]


def _ssd_chunk(refs, j):
    return [_chunk_rows(r, j) for r in refs[:4]] + [refs[4].at[:, :, pl.ds(j * CHUNK, CHUNK)], refs[5], refs[6]]


def _ssd_args(xbc, t):
    return [xbc, xbc, xbc, t["dt_exp"], t["dtt"], t["a_exp"], t["a_rows"]]


def ssd_fwd(xbc, dirs, name):
    s = xbc.shape[0]
    nd = len(dirs)
    specs = [_ssd_specs(s, t["rev"]) for t in dirs]
    nck = specs[0][0]

    def one(rev, x_ref, b_ref, c_ref, dt_ref, dtt_ref, a_ref, ar_ref, y_ref, hs_ref, h_scr):
        v = _ssd_common(x_ref, b_ref, c_ref, dt_ref, dtt_ref, a_ref, ar_ref, rev)
        xdtb = v["xdt"].astype(BF16)
        yd = _dot(jnp.concatenate([v["cb"] * v["lmats"][h] for h in range(2)], axis=0).astype(BF16), xdtb)
        h_in = h_scr[...]
        hs_ref[0, 0] = h_in
        y_off = _dot(v["cm"], h_in.astype(BF16)) * jnp.exp(v["cs"])
        y_ref[...] = jnp.where(_iota((CHUNK, 128), 1) < HD, yd[:CHUNK], yd[CHUNK:]) + y_off
        decay = jnp.exp(v["tot"] - v["cs"])
        h_scr[...] = jnp.exp(v["tot"]) * h_in + _dot(v["bm"], (v["xdt"] * decay).astype(BF16), TN)

    def body(*refs):
        @pl.when(pl.program_id(1) == 0)
        def _():
            for k in range(nd):
                refs[9 * nd + k][...] = jnp.zeros((CHUNK, 128), F32)

        for k, t in enumerate(dirs):
            y_ref, hs_ref = refs[7 * nd + 2 * k:7 * nd + 2 * k + 2]
            for j in (range(SSD_SUB)[::-1] if t["rev"] else range(SSD_SUB)):
                one(t["rev"], *_ssd_chunk(refs[7 * k:7 * k + 7], j), _chunk_rows(y_ref, j), hs_ref.at[:, pl.ds(j, 1)], refs[9 * nd + k])

    res = _pcall(
        body, name=name, grid=(8, nck), in_specs=[sp for t in specs for sp in t[1]],
        out_specs=[sp for t in specs for sp in (t[2], t[3])],
        out_shape=[jax.ShapeDtypeStruct((s, D), F32), jax.ShapeDtypeStruct((8, s // CHUNK, CHUNK, 128), F32)] * nd,
        scratch_shapes=[pltpu.VMEM((CHUNK, 128), F32)] * nd, compiler_params=_cparams(("parallel", "arbitrary")),
    )(*[a for t in dirs for a in _ssd_args(xbc, t)])
    return [(res[2 * k], res[2 * k + 1]) for k in range(nd)]


def ssd_bwd(xbc, dirs, dy, name):
    s = xbc.shape[0]
    nd = len(dirs)
    specs = [_ssd_specs(s, not t["rev"]) for t in dirs]
    nck = specs[0][0]

    def one(rev, x_ref, b_ref, c_ref, dt_ref, dtt_ref, a_ref, ar_ref, hs_ref, dy_ref,
            dx_ref, ddt_ref, db_ref, dc_ref, dal_ref, dh_scr):
        v = _ssd_common(x_ref, b_ref, c_ref, dt_ref, dtt_ref, a_ref, ar_ref, rev)
        bm, cm, cs, tot, xdt = v["bm"], v["cm"], v["cs"], v["tot"], v["xdt"]
        h_in, dh = hs_ref[0, 0], dh_scr[...]
        dyv = dy_ref[...]
        dyb = dyv.astype(BF16)
        etot, decay, ecs = jnp.exp(tot), jnp.exp(tot - cs), jnp.exp(cs)
        xdtb = xdt.astype(BF16)
        xdec = xdt * decay
        dch = (dyv * ecs).astype(BF16)
        hb, dhb = h_in.astype(BF16), dh.astype(BF16)
        y_off = _dot(cm, hb) * ecs
        dc = _dot(dch, hb, NT)
        dh_y = _dot(cm, dch, TN)
        dxdec = _dot(bm, dhb)
        db = _dot(xdec.astype(BF16), dhb, NT)
        state_term = xdec * dxdec
        dtot = _colsum(dh * h_in) * etot + _colsum(state_term)
        head0 = _iota((CHUNK, 128), 1) < HD
        ii, jj = _iota((CHUNK, CHUNK), 0), _iota((CHUNK, CHUNK), 1)
        low_t = jj <= ii if rev else jj >= ii
        not_low_t = (~low_t).astype(BF16)
        g = _dot(_stack_heads(dyb, head0), xdtb, NT)
        gl = [g[:CHUNK] * v["lmats"][0], g[CHUNK:] * v["lmats"][1]]
        dcb = gl[0] + gl[1]
        dxd = _dot(jnp.concatenate([v["cb"] * v["lmats"][h] for h in range(2)], axis=1).astype(BF16), dyb, TN)
        dxd = jnp.where(head0, dxd[:CHUNK], dxd[CHUNK:])
        w = _dot(not_low_t, jnp.concatenate([gl[h] * v["cb"] for h in range(2)], axis=0).astype(BF16), NT)
        da_l = [jnp.sum(jnp.where(low_t, w[:, CHUNK * h:CHUNK * h + CHUNK], 0.0), axis=1, keepdims=True) for h in range(2)]
        dxdt = dxdec * decay + dxd
        dcbb = dcb.astype(BF16)
        dc_ref[...] = dc + _dot(dcbb, bm)
        db_ref[...] = db + _dot(dcbb, cm, TN)
        dcs = dyv * y_off - state_term + jnp.where(_iota((CHUNK, 128), 0) == v["last"], dtot, 0.0)
        lowb = v["low"].astype(BF16)
        da = _dot(lowb, jnp.concatenate(_parts(dcs, 2), axis=1), TN)
        da = da[:, :128] + da[:, 128:]
        seg = ((ii < HD) == (jj < HD)).astype(BF16)
        sums = _dot(jnp.concatenate(_parts(da, 2) + _parts(dxdt * v["x"], 2), axis=0), seg)
        da = sums[:CHUNK] + sums[CHUNK:2 * CHUNK] + jnp.where(head0, da_l[0], da_l[1])
        ddt_x = sums[2 * CHUNK:3 * CHUNK] + sums[3 * CHUNK:]
        dx_ref[...] = dxdt * v["dtx"]
        ddt_ref[...] = ddt_x + da * a_ref[...]
        dal_ref[0] += _colsum(da * v["a"])
        dh_scr[...] = etot * dh + dh_y

    def body(*refs):
        @pl.when(pl.program_id(1) == 0)
        def _():
            for k in range(nd):
                refs[14 * nd + k][...] = jnp.zeros((CHUNK, 128), F32)
                refs[9 * nd + 5 * k + 4][...] = jnp.zeros((1, 8, 128), F32)

        for k, t in enumerate(dirs):
            ins, outs = refs[9 * k:9 * k + 9], refs[9 * nd + 5 * k:9 * nd + 5 * k + 5]
            for j in (range(SSD_SUB) if t["rev"] else range(SSD_SUB)[::-1]):
                one(t["rev"], *_ssd_chunk(ins[:7], j), ins[7].at[:, pl.ds(j, 1)], _chunk_rows(ins[8], j),
                    *[_chunk_rows(r, j) for r in outs[:4]], outs[4], refs[14 * nd + k])

    acc_spec = pl.BlockSpec((1, 8, 128), lambda p, c: (p, 0, 0))
    res = _pcall(
        body, name=name, grid=(8, nck), in_specs=[sp for t in specs for sp in t[1] + [t[3], t[2]]],
        out_specs=[sp for t in specs for sp in [t[2]] * 4 + [acc_spec]],
        out_shape=([jax.ShapeDtypeStruct((s, D), F32)] * 4 + [jax.ShapeDtypeStruct((8, 8, 128), F32)]) * nd,
        scratch_shapes=[pltpu.VMEM((CHUNK, 128), F32)] * nd, compiler_params=_cparams(("parallel", "arbitrary")),
    )(*[a for t in dirs for a in _ssd_args(xbc, t) + [t["hs"], dy]])
    return [res[5 * k:5 * k + 5] for k in range(nd)]


def _group_norm_stats(g):
    r = [lax.rsqrt(jnp.mean(g[:, 256 * k:256 * k + 256] ** 2, axis=-1, keepdims=True) + EPS) for k in range(4)]
    grp = _iota(g.shape, 1) // 256
    return jnp.where(grp == 0, r[0], jnp.where(grp == 1, r[1], jnp.where(grp == 2, r[2], r[3])))


def _group_mean(t):
    m = [jnp.mean(t[:, 256 * k:256 * k + 256], axis=-1, keepdims=True) for k in range(4)]
    grp = _iota(t.shape, 1) // 256
    return jnp.where(grp == 0, m[0], jnp.where(grp == 1, m[1], jnp.where(grp == 2, m[2], m[3])))


def _mesh_pos():
    return lax.axis_index("x"), lax.axis_index("y"), lax.axis_index("c")


HBM = pl.BlockSpec(memory_space=pltpu.HBM)
SEM = pl.BlockSpec(memory_space=pltpu.SEMAPHORE)
EFFECT = pltpu.SideEffectType.DATAFLOW_SIDE_EFFECTING


def _hbm(t):
    return pltpu.with_memory_space_constraint(t, pltpu.HBM)


def _other_chips(x, y):
    return [(1 - x, y), (x, 1 - y), (1 - x, 1 - y)]


def _peer(x, y, c, m):
    return x ^ (m >> 2), y ^ ((m >> 1) & 1), c ^ (m & 1)


def gather_start(srcs_a, srcs_b, halved=()):
    srcs = [_hbm(t) for t in list(srcs_a) + list(srcs_b)]
    n, na = len(srcs), len(srcs_a)
    half = [k in halved for k in range(n)]
    lands = [_hbm(lax.empty((4,) + (t.shape[1:] if half[k] else t.shape), t.dtype)) for k, t in enumerate(srcs)]

    def body(*refs):
        src, land = refs[:n], refs[n:2 * n]
        sems = refs[2 * n:2 * n + 4]
        x, y, c = _mesh_pos()
        for k in range(n):
            for j, (px, py) in enumerate(_other_chips(x, y)):
                send, recv, idx = (sems[0], sems[1], 3 * k + j) if k < na else (sems[2], sems[3], 3 * (k - na) + j)
                pltpu.make_async_remote_copy(src_ref=src[k].at[c] if half[k] else src[k], dst_ref=land[k].at[2 * x + y], send_sem=send.at[idx],
                                             recv_sem=recv.at[idx], device_id=(px, py, c), device_id_type=MESH).start()

    sem_a, sem_b = pltpu.SemaphoreType.DMA((3 * na,)), pltpu.SemaphoreType.DMA((3 * (n - na),))
    res = _pcall(
        body, name="gather_start", in_specs=[HBM] * (2 * n), out_specs=[SEM] * 4 + [HBM] * (2 * n),
        out_shape=[sem_a, sem_a, sem_b, sem_b] + [pltpu.HBM(t.shape, t.dtype) for t in srcs + lands],
        input_output_aliases={i: 4 + i for i in range(2 * n)},
        compiler_params=pltpu.CompilerParams(has_side_effects=EFFECT),
    )(*srcs, *lands)
    thru_src, thru_land = res[4:4 + n], res[4 + n:]
    return ((res[0], res[1], thru_src[:na], thru_land[:na], half[:na]), (res[2], res[3], thru_src[na:], thru_land[na:], half[na:]))


def gather_wait(group, name, after=None):
    send, recv, srcs, lands, half = group
    n = len(srcs)

    def body(*refs):
        src, land, send_ref, recv_ref = refs[:n], refs[n:2 * n], refs[2 * n], refs[2 * n + 1]
        x, y, c = _mesh_pos()
        for j, (px, py) in enumerate(_other_chips(x, y)):
            for k in range(n):
                cp = pltpu.make_async_remote_copy(src_ref=src[k].at[0] if half[k] else src[k], dst_ref=land[k].at[2 * px + py], send_sem=send_ref.at[3 * k + j],
                                                  recv_sem=recv_ref.at[3 * k + j], device_id=(px, py, c), device_id_type=MESH)
                cp.wait_send()
                cp.wait_recv()

    extra = [] if after is None else [after]
    res = _pcall(
        body, name=name, in_specs=[HBM] * (2 * n) + [SEM, SEM] + [pl.BlockSpec(memory_space=pl.ANY)] * len(extra),
        out_specs=[HBM] * (2 * n), out_shape=[pltpu.HBM(t.shape, t.dtype) for t in list(srcs) + list(lands)],
        input_output_aliases={i: i for i in range(2 * n)}, compiler_params=pltpu.CompilerParams(has_side_effects=EFFECT),
    )(*srcs, *lands, send, recv, *extra)
    return res[:n], res[n:]


def scatter_start(pieces, smalls, name):
    srcs = [_hbm(t) for t in list(pieces) + list(smalls)]
    n, npc = len(srcs), len(pieces)
    lands = [_hbm(lax.empty((8,) + (t.shape[2:] if k < npc else t.shape), t.dtype)) for k, t in enumerate(srcs)]

    def body(*refs):
        src, land, send, recv = refs[:n], refs[n:2 * n], refs[2 * n], refs[2 * n + 1]
        token = refs[-1]
        x, y, c = _mesh_pos()
        for m in range(1, 8):
            px, py, pc = _peer(x, y, c, m)
            for k in range(n):
                s_ref = src[k].at[2 * px + py, pc] if k < npc else src[k]
                d_ref = land[k].at[m] if k < npc else land[k].at[4 * x + 2 * y + c]
                pltpu.make_async_remote_copy(src_ref=s_ref, dst_ref=d_ref, send_sem=send.at[7 * k + m - 1], recv_sem=recv.at[7 * k + m - 1],
                                             device_id=(px, py, pc), device_id_type=MESH).start()
        token[...] = jnp.zeros_like(token)

    sem = pltpu.SemaphoreType.DMA((7 * n,))
    res = _pcall(
        body, name=name, in_specs=[HBM] * (2 * n),
        out_specs=[SEM, SEM] + [HBM] * (2 * n) + [pl.BlockSpec(memory_space=pltpu.VMEM)],
        out_shape=[sem, sem] + [pltpu.HBM(t.shape, t.dtype) for t in srcs + lands] + [jax.ShapeDtypeStruct((8, 128), F32)],
        input_output_aliases={i: 2 + i for i in range(2 * n)},
        compiler_params=pltpu.CompilerParams(has_side_effects=EFFECT),
    )(*srcs, *lands)
    return (res[0], res[1], res[2:2 + n], res[2 + n:2 + 2 * n], npc), res[-1]


def scatter_wait(group, name, after=None):
    send, recv, srcs, lands, npc = group
    n = len(srcs)

    def body(*refs):
        src, land, send_ref, recv_ref = refs[:n], refs[n:2 * n], refs[2 * n], refs[2 * n + 1]
        x, y, c = _mesh_pos()
        for m in range(1, 8):
            px, py, pc = _peer(x, y, c, m)
            for k in range(n):
                s_ref = src[k].at[0, 0] if k < npc else src[k]
                d_ref = land[k].at[m] if k < npc else land[k].at[4 * px + 2 * py + pc]
                cp = pltpu.make_async_remote_copy(src_ref=s_ref, dst_ref=d_ref, send_sem=send_ref.at[7 * k + m - 1],
                                                  recv_sem=recv_ref.at[7 * k + m - 1], device_id=(px, py, pc), device_id_type=MESH)
                cp.wait_send()
                cp.wait_recv()

    extra = [] if after is None else [after]
    res = _pcall(
        body, name=name, in_specs=[HBM] * (2 * n) + [SEM, SEM] + [pl.BlockSpec(memory_space=pl.ANY)] * len(extra),
        out_specs=[HBM] * (2 * n), out_shape=[pltpu.HBM(t.shape, t.dtype) for t in list(srcs) + list(lands)],
        input_output_aliases={i: i for i in range(2 * n)}, compiler_params=pltpu.CompilerParams(has_side_effects=EFFECT),
    )(*srcs, *lands, send, recv, *extra)
    return res[:n], res[n:]


def swap_halves(pieces, name):
    n = len(pieces)
    whole = pl.BlockSpec(memory_space=pltpu.VMEM)

    def body(*refs):
        p_refs, o_refs, send_sems, recv_sems, local_sems = refs[:n], refs[n:2 * n], refs[2 * n], refs[2 * n + 1], refs[2 * n + 2]
        x, y, c = _mesh_pos()
        local = [pltpu.make_async_copy(p_refs[k], o_refs[k].at[c], local_sems.at[k]) for k in range(n)]
        for cp in local:
            cp.start()

        def copy(k, slot):
            return pltpu.make_async_remote_copy(src_ref=p_refs[k], dst_ref=o_refs[k].at[slot], send_sem=send_sems.at[k],
                                                recv_sem=recv_sems.at[k], device_id=(x, y, 1 - c), device_id_type=MESH)

        for k in range(n):
            copy(k, c).start()
        for k in range(n):
            copy(k, 1 - c).wait_recv()
        for k in range(n):
            copy(k, c).wait_send()
        for cp in local:
            cp.wait()

    return _pcall(
        body, name=name, in_specs=[whole] * n, out_specs=[whole] * n,
        out_shape=[jax.ShapeDtypeStruct((2,) + t.shape, t.dtype) for t in pieces],
        scratch_shapes=[pltpu.SemaphoreType.DMA((n,)), pltpu.SemaphoreType.DMA((n,)), pltpu.SemaphoreType.DMA((n,))],
        compiler_params=_cparams(),
    )(*pieces)


def adamw(w, g, m, v, name):
    rows, cols = w.shape
    tm = rows
    for t in (256, 352, 128, 144, 64, 32, 16, 8):
        if rows % t == 0:
            tm = t
            break

    def fn(i, nrow, wv, gv, mv, vv):
        mn = ADAM_B1 * mv + (1.0 - ADAM_B1) * gv
        vn = ADAM_B2 * vv + (1.0 - ADAM_B2) * (gv * gv)
        m_hat = mn / (1.0 - ADAM_B1 ** ADAM_STEP)
        v_hat = vn / (1.0 - ADAM_B2 ** ADAM_STEP)
        delta = -ADAM_LR * (m_hat / (jnp.sqrt(v_hat) + ADAM_EPS) + ADAM_WD * wv)
        return delta, mn, vn

    return ew(fn, name, rows, tm, 1, [(t, "row", cols, 0) for t in (w, g, m, v)], [(cols, F32, cols)] * 3)


REST = ("w_out", "w_up", "w_down")
SMALL = ("norm1_w", "ssm_conv_w", "ssm_conv_b", "a_log_f", "a_log_b", "dt_bias_f", "dt_bias_b", "d_skip",
         "ssm_norm_w", "norm2_w", "ffn_conv_w", "ffn_conv_b", "final_norm_w")
WEIGHTS = ("norm1_w", "w_in", "ssm_conv_w", "ssm_conv_b", "a_log_f", "a_log_b", "dt_bias_f", "dt_bias_b", "d_skip",
           "ssm_norm_w", "w_out", "norm2_w", "w_up", "ffn_conv_w", "ffn_conv_b", "w_down", "final_norm_w")
INPUTS = ("x",) + WEIGHTS + ("loss_target",) + tuple("m_" + n for n in WEIGHTS) + tuple("v_" + n for n in WEIGHTS)


def _flat_rows(parts, width, rows):
    flat = jnp.concatenate([p.reshape(-1) for p in parts])
    return jnp.pad(flat, (0, rows * width - flat.shape[0])).reshape(rows, width)


def _split_flat(flat, shapes):
    out, pos = [], 0
    flat = flat.reshape(-1)
    for shp in shapes:
        n = int(np.prod(shp))
        out.append(flat[pos:pos + n].reshape(shp))
        pos += n
    return out


def _col_shards(t, nshard):
    r, c = t.shape
    return t.reshape(r, nshard, c // nshard).transpose(1, 0, 2)


def _row_shards(t, nshard):
    r, c = t.shape
    return t.reshape(nshard, r // nshard, c)


def kernel(x, norm1_w, w_in, ssm_conv_w, ssm_conv_b, a_log_f, a_log_b, dt_bias_f, dt_bias_b, d_skip, ssm_norm_w, w_out, norm2_w, w_up, ffn_conv_w, ffn_conv_b, w_down, final_norm_w, loss_target, m_norm1_w, m_w_in, m_ssm_conv_w, m_ssm_conv_b, m_a_log_f, m_a_log_b, m_dt_bias_f, m_dt_bias_b, m_d_skip, m_ssm_norm_w, m_w_out, m_norm2_w, m_w_up, m_ffn_conv_w, m_ffn_conv_b, m_w_down, m_final_norm_w, v_norm1_w, v_w_in, v_ssm_conv_w, v_ssm_conv_b, v_a_log_f, v_a_log_b, v_dt_bias_f, v_dt_bias_b, v_d_skip, v_ssm_norm_w, v_w_out, v_norm2_w, v_w_up, v_ffn_conv_w, v_ffn_conv_b, v_w_down, v_final_norm_w):
    p = dict(zip(INPUTS, (x, norm1_w, w_in, ssm_conv_w, ssm_conv_b, a_log_f, a_log_b, dt_bias_f, dt_bias_b, d_skip, ssm_norm_w, w_out, norm2_w, w_up, ffn_conv_w, ffn_conv_b, w_down, final_norm_w, loss_target, m_norm1_w, m_w_in, m_ssm_conv_w, m_ssm_conv_b, m_a_log_f, m_a_log_b, m_dt_bias_f, m_dt_bias_b, m_d_skip, m_ssm_norm_w, m_w_out, m_norm2_w, m_w_up, m_ffn_conv_w, m_ffn_conv_b, m_w_down, m_final_norm_w, v_norm1_w, v_w_in, v_ssm_conv_w, v_ssm_conv_b, v_a_log_f, v_a_log_b, v_dt_bias_f, v_dt_bias_b, v_d_skip, v_ssm_norm_w, v_w_out, v_norm2_w, v_w_up, v_ffn_conv_w, v_ffn_conv_b, v_w_down, v_final_norm_w)))
    x = p["x"][0]
    tgt = p["loss_target"][0]
    s = x.shape[0]
    chip = 2 * lax.axis_index("x") + lax.axis_index("y")

    own_slot = lambda land, mine, slot: lax.dynamic_update_slice_in_dim(land, mine[None], slot, axis=0)
    core = lax.axis_index("c")
    src_in = p["w_in"][0].astype(BF16).reshape(2, D // 2, -1)
    src_rest = [p[n][0].astype(BF16) for n in REST]
    small_w = _flat_rows([p["ssm_conv_w"][0], p["ffn_conv_w"][0]], 128, 48)
    gather_in, gather_rest = gather_start([src_in, small_w], src_rest, halved=(0,))
    (src_in, small_w), (wg_in, sg) = gather_wait(gather_in, "gather_wait_in")
    wg_in = own_slot(wg_in, lax.dynamic_index_in_dim(src_in, core, 0, keepdims=False), chip)
    wg_in, = swap_halves([wg_in], "swap_w_in_rows")
    w_in = wg_in.transpose(0, 2, 1, 3).reshape(D, -1)
    sg = own_slot(sg, small_w, chip)
    n_in = w_in.shape[1]
    n_main = 6 * D
    w_dt = jnp.pad(w_in[:, n_main:], ((0, 0), (0, 128 - (n_in - n_main))))
    sgf = sg.reshape(4, -1)
    n_sc, n_fc = p["ssm_conv_w"].shape[1], p["ffn_conv_w"].shape[1]
    ssm_cw = sgf[:, :n_sc * 3].reshape(-1, 3).T
    ffn_cw = sgf[:, n_sc * 3:(n_sc + n_fc) * 3].reshape(-1, 3).T
    ssm_cb, ffn_cb = p["ssm_conv_b"], p["ffn_conv_b"]
    n1w, n2w, snw, fnw = p["norm1_w"], p["norm2_w"], p["ssm_norm_w"], p["final_norm_w"].reshape(1, D)

    h1, = ew(lambda i, n, xv, w: _rms_fwd(xv, w), "rms1", s, 256, 1,
             [(x, "row", D, 0), (n1w, "const", D, 0)], [(D, BF16, D)])
    proj = matmul(h1, w_in, "nn", "in_proj", n_cols=n_main)
    proj_dt = matmul(h1, w_dt, "nn", "in_proj_dt")
    tabs = _rope_tables(s)
    attn, lse = attn_fwd_all(proj, tabs, "attn_fwd")

    def conv_silu_fn(i, n, xv, xp, xn, w, b):
        return _silu(w[0:1] * _shift_down(xv, xp, i) + w[1:2] * xv + w[2:3] * _shift_up(xv, xn, i, n) + b)

    xbc_act, = ew(conv_silu_fn, "ssm_conv", s, 256, 2,
                  [(proj, "row", D, 4), (proj, "prev", D, 4), (proj, "next", D, 4),
                   (ssm_cw, "const", D, 0), (ssm_cb, "const", D, 0)], [(2 * D, F32, D)])
    dt_bias = jnp.pad(jnp.concatenate([p["dt_bias_f"], p["dt_bias_b"]], axis=1), ((0, 0), (0, 96)))

    lanes_of = np.arange(128)[:, None] == np.arange(D)[None, :] // HD
    spread = [jnp.asarray(np.roll(lanes_of, 16 * k, axis=0), BF16) for k in range(2)]

    def softplus_fn(i, n, r, b, ef, eb):
        t = r + b
        dtv = jnp.maximum(t, 0.0) + jnp.log(1.0 + jnp.exp(-jnp.abs(t)))
        parts = _parts(dtv, 3)
        return dtv, sum(_dot(q, ef) for q in parts), sum(_dot(q, eb) for q in parts)

    dt, dt_exp_f, dt_exp_b = ew(softplus_fn, "dt_softplus", s, 512, 1,
                                [(proj_dt, "row", 128, 0), (dt_bias, "const", 128, 0), (spread[0], "const", D, 0), (spread[1], "const", D, 0)],
                                [(128, F32, 128), (D, F32, D), (D, F32, D)])
    d_exp = jnp.repeat(p["d_skip"], HD, axis=1)
    ssd = []
    for k, (a_log, rev) in enumerate(((p["a_log_f"], False), (p["a_log_b"], True))):
        dt_k = dt[:, 16 * k:16 * k + 16]
        a_head = -jnp.exp(a_log)
        dt_exp = (dt_exp_f, dt_exp_b)[k]
        dtt = jnp.pad(dt_k.T.reshape(8, 2, s), ((0, 0), (0, 6), (0, 0)))
        a_exp = jnp.repeat(a_head, HD, axis=1)
        a_rows = jnp.broadcast_to(jnp.pad(a_head.reshape(8, 2), ((0, 0), (0, 6)))[:, :, None], (8, 8, 128))
        ssd.append(dict(dt_exp=dt_exp, dtt=dtt, a_exp=a_exp, a_rows=a_rows, rev=rev))
    for t, (y_k, hs_k) in zip(ssd, ssd_fwd(xbc_act, ssd, "ssd_fwd")):
        t["y"], t["hs"] = y_k, hs_k

    def gate_fn(i, n, yf, yb, xs, z, dsk, w):
        g = (yf + yb + dsk * xs) * _silu(z)
        return g * _group_norm_stats(g) * w

    ssm_out, = ew(gate_fn, "ssm_gate_norm", s, 256, 1,
                  [(ssd[0]["y"], "row", D, 0), (ssd[1]["y"], "row", D, 0), (xbc_act, "row", D, 0), (proj, "row", D, 3),
                   (d_exp, "const", D, 0), (snw, "const", D, 0)], [(D, F32, D)])
    mix = jnp.concatenate([attn, ssm_out], axis=1).astype(BF16)
    src_rest, wg_rest = gather_wait(gather_rest, "gather_wait_rest", after=mix)
    wg_rest = [own_slot(land, mine, chip) for land, mine in zip(wg_rest, src_rest)]
    w_out = wg_rest[0].reshape(-1, D)
    w_up = wg_rest[1].transpose(1, 0, 2).reshape(D, -1)
    w_down = wg_rest[2].reshape(-1, D)
    mix_w = matmul(mix, w_out, "nn", "out_proj")

    def res_rms_fn(i, n, xv, mw, w):
        x1v = xv + mw
        return x1v, _rms_fwd(x1v, w)

    x1, h2 = ew(res_rms_fn, "res_rms2", s, 256, 1, [(x, "row", D, 0), (mix_w, "row", D, 0), (n2w, "const", D, 0)],
                [(D, F32, D), (D, BF16, D)])
    hw = matmul(h2, w_up, "nn", "ffn_up")
    fw = D_FF // 2
    nfb = D_FF // fw
    ffn_conv_ins = [(hw, "row", fw, 0), (hw, "prev", fw, 0), (hw, "next", fw, 0),
                    (hw, "row", fw, nfb), (hw, "prev", fw, nfb), (hw, "next", fw, nfb),
                    (ffn_cw, "const", fw, 0), (ffn_cw, "const", fw, nfb), (ffn_cb, "const", fw, 0), (ffn_cb, "const", fw, nfb)]

    def ffn_conv(i, n, g, gp, gn, u, up_, un, wg_, wu, bg, bu):
        gs = (_shift_down(g, gp, i), g, _shift_up(g, gn, i, n))
        us = (_shift_down(u, up_, i), u, _shift_up(u, un, i, n))
        gate = wg_[0:1] * gs[0] + wg_[1:2] * gs[1] + wg_[2:3] * gs[2] + bg
        upv = wu[0:1] * us[0] + wu[1:2] * us[1] + wu[2:3] * us[2] + bu
        return gate, upv, gs, us

    def glu_fn(i, n, *blocks):
        gate, upv, _, _ = ffn_conv(i, n, *blocks)
        return _silu(gate) * upv

    act, = ew(glu_fn, "ffn_conv_glu", s, 256, nfb, ffn_conv_ins, [(D_FF, BF16, fw)])
    ffn = matmul(act, w_down, "nn", "ffn_down")

    def head_fn(i, n, x1v, fv, tv, w):
        x2 = x1v + fv
        r = lax.rsqrt(jnp.mean(x2 * x2, axis=-1, keepdims=True) + EPS)
        xh = x2 * r
        diff = xh * w - tv
        loss = 0.5 * jnp.sum(jnp.mean(diff * diff, axis=-1, keepdims=True), axis=0, keepdims=True)
        dout = diff * (1.0 / D)
        dxh = dout * w
        dx2 = r * (dxh - xh * jnp.mean(dxh * xh, axis=-1, keepdims=True))
        return dx2, jnp.broadcast_to(loss, (1, 128)), _colsum(dout * xh)

    dx2, loss_acc, g_fnw = ew(head_fn, "loss_head", s, 256, 1,
                              [(x1, "row", D, 0), (ffn, "row", D, 0), (tgt, "row", D, 0), (fnw, "const", D, 0)],
                              [(D, F32, D)], [(128, 128), (D, D)])
    loss = lax.psum(loss_acc[0, 0], ("x", "y", "c"))

    g_w_down = matmul(act, dx2, "tn", "d_w_down")
    dact = matmul(dx2, w_down, "nt", "d_act")

    res = ew(ffn_conv_bwd_fn, "ffn_conv_glu_bwd", s, 256, nfb,
             ffn_conv_ins + [(dact, "row", fw, 0), (dact, "prev", fw, 0), (dact, "next", fw, 0)],
             [(D_FF, F32, fw)] * 2, [(D_FF, fw)] * 8)
    dhw_g, dhw_u = res[0], res[1]
    g_ffn_cw = jnp.concatenate([jnp.concatenate(res[2:5], axis=0), jnp.concatenate(res[5:8], axis=0)], axis=1).T
    g_ffn_cb = jnp.concatenate([res[8], res[9]], axis=1)

    g_w_up = jnp.concatenate([matmul(h2, dhw_g, "tn", "d_w_up_gate"), matmul(h2, dhw_u, "tn", "d_w_up_up")], axis=1)
    dh2_a = matmul(dhw_g, w_up, "nt", "d_h2_gate")
    dh2_b = matmul(dhw_u, w_up, "nt", "d_h2_up", b_k_off=D_FF // _pick(D_FF, 1408))

    def res_rms_bwd_fn(i, n, dres, da, db, xin, w):
        dx, dw = _rms_bwd(da + db, xin, w)
        return dres + dx, dw

    dx1, g_n2w = ew(res_rms_bwd_fn, "res_rms2_bwd", s, 256, 1,
                    [(dx2, "row", D, 0), (dh2_a, "row", D, 0), (dh2_b, "row", D, 0), (x1, "row", D, 0), (n2w, "const", D, 0)],
                    [(D, F32, D)], [(D, D)])

    g_w_out = matmul(mix, dx1, "tn", "d_w_out")
    to_pieces = lambda t: t.astype(BF16).reshape(4, 2, t.shape[1] // 2, t.shape[2])
    shards_rest = [_row_shards(g_w_out, 4), _col_shards(g_w_up, 4), _row_shards(g_w_down, 4)]
    scatter_rest, token = scatter_start([to_pieces(t) for t in shards_rest], [], "scatter_start_rest")
    dmix = matmul(dx1, w_out, "nt", "d_mix", after=token)
    ii, jj = np.arange(128)[:, None] // HD, np.arange(128)[None, :] // HD
    seg = jnp.asarray(ii == jj, BF16)

    def gate_bwd_fn(i, n, dout, yf, yb, xs, z, dsk, w, segm):
        yt = yf + yb + dsk * xs
        sz = _silu(z)
        g = yt * sz
        r = _group_norm_stats(g)
        gh = g * r
        dn = dout * w
        dg = r * (dn - gh * _group_mean(dn * gh))
        dy = dg * sz
        parts = _parts(jnp.broadcast_to(_colsum(dy * xs), (8, D)), 2)
        dsk_heads = jnp.concatenate([sum(_dot(q[:, 128 * k:128 * k + 128], segm) for q in parts) for k in range(D // 128)], axis=1)
        return dy, dg * yt * _dsilu(z), _colsum(dout * gh), dsk_heads[0:1]

    dy, dz, g_snw, g_dskip_l = ew(
        gate_bwd_fn, "ssm_gate_norm_bwd", s, 256, 1,
        [(dmix, "row", D, 1), (ssd[0]["y"], "row", D, 0), (ssd[1]["y"], "row", D, 0), (xbc_act, "row", D, 0),
         (proj, "row", D, 3), (d_exp, "const", D, 0), (snw, "const", D, 0), (seg, "const", 128, 0)],
        [(D, F32, D), (D, BF16, D)], [(D, D)] * 2)
    sb = ssd_bwd(xbc_act, ssd, dy, "ssd_bwd")

    def dxbc_act_fn(i, n, dxf, dxb, dyv, dsk, dbf, dbb, dcf, dcb_):
        db, dc = dbf + dbb, dcf + dcb_
        db = [db[:, 256 * g:256 * g + 128] + db[:, 256 * g + 128:256 * g + 256] for g in range(4)]
        dc = [dc[:, 256 * g:256 * g + 128] + dc[:, 256 * g + 128:256 * g + 256] for g in range(4)]
        return jnp.concatenate([dxf + dxb + dyv * dsk] + db + dc, axis=1)

    dxbc_act, = ew(dxbc_act_fn, "d_xbc_act", s, 256, 1,
                   [(sb[0][0], "row", D, 0), (sb[1][0], "row", D, 0), (dy, "row", D, 0), (d_exp, "const", D, 0),
                    (sb[0][2], "row", D, 0), (sb[1][2], "row", D, 0), (sb[0][3], "row", D, 0), (sb[1][3], "row", D, 0)],
                   [(2 * D, F32, 2 * D)])

    res = ew(silu_conv_bwd_fn, "ssm_conv_bwd", s, 256, 2,
             [(proj, "row", D, 4), (proj, "prev", D, 4), (proj, "next", D, 4), (ssm_cw, "const", D, 0), (ssm_cb, "const", D, 0),
              (dxbc_act, "row", D, 0), (dxbc_act, "prev", D, 0), (dxbc_act, "next", D, 0)], [(2 * D, BF16, D)], [(2 * D, D)] * 4)
    dxbc = res[0]
    g_ssm_cw = jnp.concatenate(res[1:4], axis=0).T
    g_ssm_cb = res[4]
    pick = np.zeros((D, 128), np.float32)
    pick[np.arange(16) * HD, np.arange(16)] = 1.0
    picks = [jnp.asarray(np.roll(pick, 16 * k, axis=1), BF16) for k in range(2)]

    def dt_bwd_fn(i, n, ddf, ddb, r, b, pf, pb):
        dd = sum(_dot(q, pf) for q in _parts(ddf, 3)) + sum(_dot(q, pb) for q in _parts(ddb, 3))
        dr = dd * _sigmoid(r + b)
        return dr, _colsum(dr)

    dproj_dt, g_dt_bias = ew(dt_bwd_fn, "dt_softplus_bwd", s, 512, 1,
                             [(sb[0][1], "row", D, 0), (sb[1][1], "row", D, 0), (proj_dt, "row", 128, 0), (dt_bias, "const", 128, 0),
                              (picks[0], "const", 128, 0), (picks[1], "const", 128, 0)],
                             [(128, F32, 128)], [(128, 128)])
    g_a_log = [t[4][:, 0, ::HD].reshape(1, 16) for t in sb]

    dq, dk, dv = attn_bwd_all(proj, tabs, dmix, attn, lse, "attn_bwd")

    dproj = jnp.concatenate([t.astype(BF16) for t in (dq, dk, dv, dz, dxbc)], axis=1)
    g_w_in = jnp.concatenate([matmul(h1, dproj, "tn", "d_w_in"), matmul(h1, dproj_dt, "tn", "d_w_in_dt")[:, :n_in - n_main]], axis=1)
    scatter_in, token = scatter_start([to_pieces(_col_shards(g_w_in, 4))], [], "scatter_start_in")
    dh1_a = matmul(dproj, w_in, "nt", "d_h1", after=token)
    dh1_b = matmul(dproj_dt, w_dt, "nt", "d_h1_dt")
    grad_x, g_n1w = ew(res_rms_bwd_fn, "rms1_bwd", s, 256, 1,
                       [(dx1, "row", D, 0), (dh1_a, "row", D, 0), (dh1_b, "row", D, 0), (x, "row", D, 0), (n1w, "const", D, 0)],
                       [(D, F32, D)], [(D, D)])

    small_g = {"norm1_w": g_n1w, "ssm_conv_w": g_ssm_cw, "ssm_conv_b": g_ssm_cb, "a_log_f": g_a_log[0], "a_log_b": g_a_log[1],
               "dt_bias_f": g_dt_bias[:, :16], "dt_bias_b": g_dt_bias[:, 16:32], "d_skip": g_dskip_l[:, ::HD],
               "ssm_norm_w": g_snw, "norm2_w": g_n2w, "ffn_conv_w": g_ffn_cw, "ffn_conv_b": g_ffn_cb, "final_norm_w": g_fnw}
    small_shapes = [small_g[n].shape for n in SMALL]
    scatter_small, token = scatter_start([], [_flat_rows([small_g[n] for n in SMALL], 128, SMALL_ROWS)], "scatter_start_small")

    def sum8_fn(i, n, *v):
        t = v[0].astype(F32)
        for u in v[1:]:
            t = t + u.astype(F32)
        return t

    def sum_pieces(sent, got, name):
        rows, w = got.shape[1:]
        tm = 256 if rows % 256 == 0 else rows
        mine = lax.dynamic_slice(sent, (chip, core, 0, 0), (1, 1, rows, w)).reshape(rows, w)
        ins = [(mine, "row", w, 0)] + [(got.reshape(8 * rows, w), "row", w, 0, k * (rows // tm)) for k in range(1, 8)]
        return ew(sum8_fn, name, rows, tm, 1, ins, [(w, F32, w)])[0]

    grads, delta, new_m, new_v = {}, {}, {}, {}

    def finish(names, sent, got, tag):
        summed = swap_halves([sum_pieces(a, b, "sum_pieces_" + n) for a, b, n in zip(sent, got, names)], "swap_halves_" + tag)
        for n, t in zip(names, summed):
            shp = p[n].shape
            grads[n] = t.reshape(shp)
            r = [u.reshape(shp[1:]) for u in (p[n], grads[n], p["m_" + n], p["v_" + n])]
            delta[n], new_m[n], new_v[n] = [u.reshape(shp) for u in adamw(*r, "adamw_" + n)]

    finish(REST, *scatter_wait(scatter_rest, "scatter_wait_rest", after=token), "rest")
    finish(("w_in",), *scatter_wait(scatter_in, "scatter_wait_in", after=new_v[REST[-1]]), "w_in")
    (sent_small,), (got_small,) = scatter_wait(scatter_small, "scatter_wait_small", after=new_v["w_in"])
    got_small = own_slot(got_small, sent_small, 2 * chip + core)
    small_sum, = ew(sum8_fn, "sum_small", SMALL_ROWS, SMALL_ROWS, 1,
                    [(got_small.reshape(8 * SMALL_ROWS, 128), "row", 128, 0, k) for k in range(8)], [(128, F32, 128)])
    for n, g in zip(SMALL, _split_flat(small_sum, small_shapes)):
        if n in ("ssm_conv_w", "ffn_conv_w"):
            rows = p[n].shape[1]
            g = lax.dynamic_slice_in_dim(g, chip * rows, rows, axis=0)
        grads[n] = g.reshape(p[n].shape)

    shapes = [p[n].shape for n in SMALL]
    total = sum(int(np.prod(sh)) for sh in shapes)
    rows = -(-total // 1024) * 8
    packs = [_flat_rows([t[n] for n in SMALL], 128, rows)
             for t in (p, grads, {n: p["m_" + n] for n in SMALL}, {n: p["v_" + n] for n in SMALL})]
    for dst, t in zip((delta, new_m, new_v), adamw(*packs, "adamw_small")):
        for n, u in zip(SMALL, _split_flat(t, shapes)):
            dst[n] = u
    return (loss, grad_x[None], *[grads[n] for n in WEIGHTS], *[delta[n] for n in WEIGHTS],
            *[new_m[n] for n in WEIGHTS], *[new_v[n] for n in WEIGHTS])
```

```python
import numpy as np
import jax
import jax.numpy as jnp
from jax import lax
from jax.experimental import pallas as pl
from jax.experimental.pallas import tpu as pltpu

F32, BF16 = jnp.float32, jnp.bfloat16
MESH = pl.DeviceIdType.MESH
V7X_VMEM_LIMIT = 56 * 1024 * 1024

D = 1024
HD = 64
EPS = 1e-6
CHUNK = 128
D_FF = 2816
ROPE_DIM = 16
ROPE_THETA = 500000.0
PATTERN_DILATIONS = (1, 4, 16)
BAND = 64
SMALL_ROWS = 280
ADAM_LR, ADAM_B1, ADAM_B2, ADAM_EPS, ADAM_WD, ADAM_STEP = 0.001, 0.9, 0.999, 1e-08, 0.01, 10

NN = (((1,), (0,)), ((), ()))
NT = (((1,), (1,)), ((), ()))
TN = (((0,), (0,)), ((), ()))


def _pcall(body, **kw):
    return pl.pallas_call(body, **kw)


def _cparams(sem=None):
    return pltpu.CompilerParams(dimension_semantics=sem, vmem_limit_bytes=V7X_VMEM_LIMIT)


def _dot(a, b, dims=NN):
    return lax.dot_general(a, b, dims, preferred_element_type=F32)


def _pick(n, cap):
    if n <= cap:
        return n
    best = 0
    for t in range(128, cap + 1, 128):
        if n % t == 0:
            best = t
    assert best, (n, cap)
    return best


def _iota(shape, dim):
    return lax.broadcasted_iota(jnp.int32, shape, dim)


def _parts(x, n):
    out, r = [], x
    for _ in range(n):
        h = r.astype(BF16)
        out.append(h)
        r = r - h.astype(F32)
    return out


def _sigmoid(x):
    return 1.0 / (1.0 + jnp.exp(-x))


def _silu(x):
    return x * _sigmoid(x)


def _dsilu(x):
    s = _sigmoid(x)
    return s * (1.0 + x * (1.0 - s))


def matmul(a, b, mode, name, out_dtype=F32, after=None, b_k_off=0, n_cols=None):
    if mode == "nn":
        (m, k), (_, n) = a.shape, (b.shape[0], n_cols or b.shape[1])
    elif mode == "nt":
        (m, k), (n, _) = a.shape, b.shape
    else:
        (k, m), (_, n) = a.shape, b.shape
    tm, tn, tk = _pick(m, 1408), _pick(n, 1408), _pick(k, 1408)
    nk = k // tk
    dims = {"nn": NN, "nt": NT, "tn": TN}[mode]
    a_spec = pl.BlockSpec((tk, tm), lambda i, j, kk: (kk, i)) if mode == "tn" else pl.BlockSpec((tm, tk), lambda i, j, kk: (i, kk))
    b_spec = pl.BlockSpec((tn, tk), lambda i, j, kk: (j, kk + b_k_off)) if mode == "nt" else pl.BlockSpec((tk, tn), lambda i, j, kk: (kk, j))
    extra = [] if after is None else [after]

    def body(a_ref, b_ref, *rest):
        o_ref, acc = rest[len(extra)], rest[len(extra) + 1:]
        part = _dot(a_ref[...].astype(BF16), b_ref[...].astype(BF16), dims)
        if nk == 1:
            o_ref[...] = part.astype(o_ref.dtype)
            return
        acc_ref, kk = acc[0], pl.program_id(2)

        @pl.when(kk == 0)
        def _():
            acc_ref[...] = part

        @pl.when((kk > 0) & (kk < nk - 1))
        def _():
            acc_ref[...] += part

        @pl.when(kk == nk - 1)
        def _():
            o_ref[...] = (acc_ref[...] + part).astype(o_ref.dtype)

    return _pcall(
        body, name=name, grid=(m // tm, n // tn, nk), in_specs=[a_spec, b_spec] + [pl.BlockSpec(memory_space=pl.ANY)] * len(extra),
        out_specs=pl.BlockSpec((tm, tn), lambda i, j, kk: (i, j)),
        out_shape=jax.ShapeDtypeStruct((m, n), out_dtype),
        scratch_shapes=[pltpu.VMEM((tm, tn), F32)] if nk > 1 else [],
        compiler_params=_cparams(("parallel", "parallel", "arbitrary")),
    )(a, b, *extra)


def ew(fn, name, rows, tm, ncol, ins, outs, accs=()):
    nrow = rows // tm
    r8 = tm // 8
    in_specs, arrays = [], []
    for ent in ins:
        arr, kind, w, off = ent[:4]
        roff = ent[4] if len(ent) > 4 else 0
        if kind == "row":
            spec = pl.BlockSpec((tm, w), lambda j, i, off=off, roff=roff: (i + roff, j + off))
        elif kind == "const":
            spec = pl.BlockSpec((arr.shape[0], w), lambda j, i, off=off: (0, j + off))
        elif kind == "prev":
            spec = pl.BlockSpec((8, w), lambda j, i, off=off: (jnp.maximum(i * r8 - 1, 0), j + off))
        else:
            spec = pl.BlockSpec((8, w), lambda j, i, off=off: (jnp.minimum((i + 1) * r8, rows // 8 - 1), j + off))
        in_specs.append(spec)
        arrays.append(arr)
    out_specs = [pl.BlockSpec((tm, w), lambda j, i: (i, j)) for (_, _, w) in outs]
    out_shape = [jax.ShapeDtypeStruct((rows, c), dt) for (c, dt, _) in outs]
    out_specs += [pl.BlockSpec((1, w), lambda j, i: (0, j)) for (_, w) in accs]
    out_shape += [jax.ShapeDtypeStruct((1, c), F32) for (c, _) in accs]
    nin, nout = len(ins), len(outs)

    def body(*refs):
        i = pl.program_id(1)
        res = fn(i, nrow, *[r[...] for r in refs[:nin]])
        if not isinstance(res, (tuple, list)):
            res = (res,)
        for r, v in zip(refs[nin:nin + nout], res[:nout]):
            r[...] = v.astype(r.dtype)
        if accs:
            acc_refs = refs[nin + nout:]

            @pl.when(i == 0)
            def _():
                for r in acc_refs:
                    r[...] = jnp.zeros_like(r)

            for r, v in zip(acc_refs, res[nout:]):
                r[...] += v

    res = _pcall(
        body, name=name, grid=(ncol, nrow), in_specs=in_specs, out_specs=out_specs, out_shape=out_shape,
        compiler_params=_cparams(("parallel", "arbitrary")),
    )(*arrays)
    return res


def _shift_down(x, prev8, i):
    first = jnp.where(i == 0, 0.0, prev8[7:8, :])
    return jnp.where(_iota(x.shape, 0) == 0, first, pltpu.roll(x, 1, 0))


def _shift_up(x, next8, i, nrow):
    last = jnp.where(i == nrow - 1, 0.0, next8[0:1, :])
    return jnp.where(_iota(x.shape, 0) == x.shape[0] - 1, last, pltpu.roll(x, x.shape[0] - 1, 0))


def _colsum(x):
    return jnp.sum(x, axis=0, keepdims=True)


def _extend(x, prev8, next8, i, nrow):
    return jnp.concatenate([jnp.where(i == 0, 0.0, prev8), x, jnp.where(i == nrow - 1, 0.0, next8)], axis=0)


def _taps(xe):
    return pltpu.roll(xe, 1, 0), xe, pltpu.roll(xe, xe.shape[0] - 1, 0)


def _mid(xe):
    return xe[8:xe.shape[0] - 8]


def _conv3(w, b, taps):
    return w[0:1] * taps[0] + w[1:2] * taps[1] + w[2:3] * taps[2] + b


def _conv3_t(w, d_ext):
    t = _taps(d_ext)
    return _mid(w[0:1] * t[2] + w[1:2] * t[1] + w[2:3] * t[0])


def ffn_conv_bwd_fn(i, n, g, gp, gn, u, up_, un, wg, wu, bg, bu, da, dap, dan):
    gt, ut = _taps(_extend(g, gp, gn, i, n)), _taps(_extend(u, up_, un, i, n))
    dae = _extend(da, dap, dan, i, n)
    gate, upv = _conv3(wg, bg, gt), _conv3(wu, bu, ut)
    dg, du = dae * upv * _dsilu(gate), dae * _silu(gate)
    dgm, dum = _mid(dg), _mid(du)
    sums = [_colsum(dgm * _mid(t)) for t in gt] + [_colsum(dum * _mid(t)) for t in ut] + [_colsum(dgm), _colsum(dum)]
    return (_conv3_t(wg, dg), _conv3_t(wu, du)) + tuple(sums)


def silu_conv_bwd_fn(i, n, xv, xp, xn, w, b, da, dap, dan):
    xt = _taps(_extend(xv, xp, xn, i, n))
    du = _extend(da, dap, dan, i, n) * _dsilu(_conv3(w, b, xt))
    dum = _mid(du)
    return (_conv3_t(w, du),) + tuple(_colsum(dum * _mid(t)) for t in xt) + (_colsum(dum),)


def _rms_fwd(x, w):
    r = lax.rsqrt(jnp.mean(x * x, axis=-1, keepdims=True) + EPS)
    return x * r * w


def _rms_bwd(dy, x, w):
    r = lax.rsqrt(jnp.mean(x * x, axis=-1, keepdims=True) + EPS)
    xh = x * r
    dxh = dy * w
    dx = r * (dxh - xh * jnp.mean(dxh * xh, axis=-1, keepdims=True))
    return dx, _colsum(dy * xh)


def _rope_tables(s):
    half = ROPE_DIM // 2
    inv_freq = jnp.power(ROPE_THETA, -jnp.arange(half, dtype=F32) * 2.0 / ROPE_DIM)
    ang = jnp.arange(s, dtype=F32)[:, None] * inv_freq[None, :]
    cos, sin = jnp.cos(ang), jnp.sin(ang)
    one, zero = jnp.ones((s, HD - ROPE_DIM), F32), jnp.zeros((s, HD - ROPE_DIM), F32)
    z8 = jnp.zeros((s, half), F32)
    c = jnp.concatenate([cos, cos, one], axis=1)
    sa = jnp.concatenate([-sin, z8, zero], axis=1)
    sb = jnp.concatenate([z8, sin, zero], axis=1)
    return [jnp.tile(t, (1, 2)) for t in (c, sa, sb)]


ATTN_CHUNK = 2048


def _attn_plan(s):
    plan = []
    for d in PATTERN_DILATIONS:
        per_res = ATTN_CHUNK // d
        tq = min(128, per_res)
        plan.append((d, tq, min(s // d, tq + 2 * BAND), per_res // tq, s // d))
    return plan


def _rows(start, size, d):
    return pl.ds(start, size) if d == 1 else pl.ds(start, size, stride=d)


def _for_tiles(chunk, pat, fn):
    d, tq, win, nblk, seq_len = pat
    for b in range(nblk):
        t0 = chunk * (ATTN_CHUNK // d) + b * tq
        kloc = jnp.clip(t0 - BAND, 0, seq_len - win)
        valid = jnp.abs(kloc + _iota((tq, win), 1) - (t0 + _iota((tq, win), 0))) <= BAND
        valid = jnp.concatenate([valid, valid], axis=0)
        if d == 1:
            fn(b * tq, pl.multiple_of(kloc, BAND), valid)
        else:
            def step(r, carry, qoff=d * b * tq, koff=d * kloc, valid=valid):
                fn(qoff + r, koff + r, valid)
                return carry
            lax.fori_loop(0, d, step, 0, unroll=min(d, 8))


def _stack_heads(x, head0):
    zero = jnp.zeros_like(x)
    return jnp.concatenate([jnp.where(head0, x, zero), jnp.where(head0, zero, x)], axis=0)


def _rope_pair(x, c, sa, sb):
    n = x.shape[1]
    return x * c + pltpu.roll(x, n - 8, 1) * sa + pltpu.roll(x, 8, 1) * sb


def _rope_pair_t(dy, c, sa, sb):
    n = dy.shape[1]
    return dy * c + pltpu.roll(dy * sa, 8, 1) + pltpu.roll(dy * sb, n - 8, 1)


def _attn_specs(s):
    whole = lambda off: pl.BlockSpec((s, 128), lambda p, c: (0, off + p))
    table = pl.BlockSpec((s, 128), lambda p, c: (0, 0))
    chunk = pl.BlockSpec((ATTN_CHUNK, 128), lambda p, c: (c, p))
    return whole, table, chunk


def attn_fwd_all(proj, tabs, name):
    s = proj.shape[0]
    plan = _attn_plan(s)
    whole, table, chunk_spec = _attn_specs(s)

    def body(q_ref, k_ref, v_ref, c_ref, sa_ref, sb_ref, o_ref, lse_ref, qs, ks, acc_s, m_s, l_s):
        chunk = pl.program_id(1)

        @pl.when(chunk == 0)
        def _():
            qs[...] = _rope_pair(q_ref[...], c_ref[...], sa_ref[...], sb_ref[...]) * (HD ** -0.5)
            ks[...] = _rope_pair(k_ref[...], c_ref[...], sa_ref[...], sb_ref[...])

        base = pl.multiple_of(chunk * ATTN_CHUNK, ATTN_CHUNK)
        for pi, pat in enumerate(plan):
            d, tq, win = pat[:3]
            head0 = _iota((tq, 128), 1) < HD

            def tile(qrow, krow, valid, pi=pi, d=d, tq=tq, win=win, head0=head0):
                qv = qs[_rows(base + qrow, tq, d), :].astype(BF16)
                kw = ks[_rows(krow, win, d), :].astype(BF16)
                vw = v_ref[_rows(krow, win, d), :].astype(BF16)
                v_ones = jnp.concatenate([vw, jnp.ones_like(vw)], axis=1)
                sc = jnp.where(valid, _dot(_stack_heads(qv, head0), kw, NT), -1e30)
                mh = jnp.max(sc, axis=1, keepdims=True)
                pv = _dot(jnp.exp(sc - mh).astype(BF16), v_ones)
                acc_s[pi, _rows(qrow, tq, d), :] = jnp.where(head0, pv[:tq, :128], pv[tq:, :128])
                m_s[pi, _rows(qrow, tq, d), :] = jnp.where(head0, mh[:tq], mh[tq:])
                l_s[pi, _rows(qrow, tq, d), :] = jnp.where(head0, pv[:tq, 128:], pv[tq:, 128:])

            _for_tiles(chunk, pat, tile)
        m_all = jnp.maximum(jnp.maximum(m_s[0], m_s[1]), m_s[2])
        e = [jnp.exp(m_s[k] - m_all) for k in range(3)]
        den = e[0] * l_s[0] + e[1] * l_s[1] + e[2] * l_s[2]
        o_ref[...] = (e[0] * acc_s[0] + e[1] * acc_s[1] + e[2] * acc_s[2]) / den
        lse_ref[...] = m_all + jnp.log(den)

    stat = pltpu.VMEM((3, ATTN_CHUNK, 128), F32)
    return _pcall(
        body, name=name, grid=(D // 128, s // ATTN_CHUNK),
        in_specs=[whole(0), whole(8), whole(16), table, table, table], out_specs=[chunk_spec, chunk_spec],
        out_shape=[jax.ShapeDtypeStruct((s, D), F32)] * 2,
        scratch_shapes=[pltpu.VMEM((s, 128), F32), pltpu.VMEM((s, 128), F32), stat, stat, stat],
        compiler_params=_cparams(("parallel", "arbitrary")),
    )(proj, proj, proj, *tabs)


def attn_bwd_all(proj, tabs, dmix, o, lse, name):
    s = proj.shape[0]
    plan = _attn_plan(s)
    whole, table, chunk_spec = _attn_specs(s)
    nchunk = s // ATTN_CHUNK

    def body(q_ref, k_ref, v_ref, c_ref, sa_ref, sb_ref, do_ref, o_ref, lse_ref, dq_ref, dk_ref, dv_ref, qs, ks, aug0_s, aug1_s):
        chunk = pl.program_id(1)

        @pl.when(chunk == 0)
        def _():
            qs[...] = _rope_pair(q_ref[...], c_ref[...], sa_ref[...], sb_ref[...]) * (HD ** -0.5)
            ks[...] = _rope_pair(k_ref[...], c_ref[...], sa_ref[...], sb_ref[...])
            dk_ref[...] = jnp.zeros_like(dk_ref)
            dv_ref[...] = jnp.zeros_like(dv_ref)

        base = pl.multiple_of(chunk * ATTN_CHUNK, ATTN_CHUNK)
        prod = do_ref[...] * o_ref[...]
        first = _iota(prod.shape, 1) < HD
        delta = jnp.where(first, jnp.sum(jnp.where(first, prod, 0.0), axis=1, keepdims=True),
                          jnp.sum(jnp.where(first, 0.0, prod), axis=1, keepdims=True))
        lane = _iota(prod.shape, 1)

        def as_lanes(lse_h, delta_h):
            a, b = [u.astype(F32) for u in _parts(lse_h, 3)], [u.astype(F32) for u in _parts(delta_h, 3)]
            out = jnp.zeros_like(lse_h)
            for k, u in enumerate(a + b):
                out = jnp.where(lane == k, u, out)
            return out

        lsev = lse_ref[...]
        aug0_s[...] = as_lanes(lsev, delta)
        aug1_s[...] = as_lanes(pltpu.roll(lsev, HD, 1), pltpu.roll(delta, HD, 1))
        for pi, pat in enumerate(plan):
            d, tq, win = pat[:3]
            head0 = _iota((tq, 128), 1) < HD

            def tile(qrow, krow, valid, pi=pi, d=d, tq=tq, win=win, head0=head0):
                qv = qs[_rows(base + qrow, tq, d), :].astype(BF16)
                kw = ks[_rows(krow, win, d), :].astype(BF16)
                vw = v_ref[_rows(krow, win, d), :].astype(BF16)
                dob = do_ref[_rows(qrow, tq, d), :].astype(BF16)
                aug = jnp.concatenate([aug0_s[_rows(qrow, tq, d), :], aug1_s[_rows(qrow, tq, d), :]], axis=0).astype(BF16)
                klane = _iota((win, 128), 1)
                minus_lse = jnp.where(klane < 3, -1.0, 0.0).astype(BF16)
                minus_delta = jnp.where((klane >= 3) & (klane < 6), -1.0, 0.0).astype(BF16)
                q2, do2 = _stack_heads(qv, head0), _stack_heads(dob, head0)
                s_lse = _dot(jnp.concatenate([q2, aug], axis=1), jnp.concatenate([kw, minus_lse], axis=1), NT)
                dp_delta = _dot(jnp.concatenate([do2, aug], axis=1), jnp.concatenate([vw, minus_delta], axis=1), NT)
                p = jnp.where(valid, jnp.exp(s_lse), 0.0)
                ds = (p * dp_delta).astype(BF16)
                dq2 = _dot(ds, kw)
                dk = _dot(ds, q2, TN)
                dv = _dot(p.astype(BF16), do2, TN)
                dqv = jnp.where(head0, dq2[:tq], dq2[tq:])
                if pi == 0:
                    dq_ref[_rows(qrow, tq, d), :] = dqv
                else:
                    dq_ref[_rows(qrow, tq, d), :] += dqv
                dk_ref[_rows(krow, win, d), :] += dk
                dv_ref[_rows(krow, win, d), :] += dv

            _for_tiles(chunk, pat, tile)
        tab = [t[pl.ds(base, ATTN_CHUNK), :] for t in (c_ref, sa_ref, sb_ref)]
        dq_ref[...] = _rope_pair_t(dq_ref[...] * (HD ** -0.5), *tab)

        @pl.when(chunk == nchunk - 1)
        def _():
            dk_ref[...] = _rope_pair_t(dk_ref[...], c_ref[...], sa_ref[...], sb_ref[...])

    return _pcall(
        body, name=name, grid=(D // 128, nchunk),
        in_specs=[whole(0), whole(8), whole(16), table, table, table, chunk_spec, chunk_spec, chunk_spec],
        out_specs=[chunk_spec, whole(0), whole(0)], out_shape=[jax.ShapeDtypeStruct((s, D), F32)] * 3,
        scratch_shapes=[pltpu.VMEM((s, 128), F32), pltpu.VMEM((s, 128), F32)] + [pltpu.VMEM((ATTN_CHUNK, 128), F32)] * 2,
        compiler_params=_cparams(("parallel", "arbitrary")),
    )(proj, proj, proj, *tabs, dmix, o, lse)


def _ssd_common(x_ref, b_ref, c_ref, dt_ref, dtt_ref, a_ref, ar_ref, rev):
    ii, jj = _iota((CHUNK, CHUNK), 0), _iota((CHUNK, CHUNK), 1)
    low = jj >= ii if rev else jj <= ii
    x, dtx = x_ref[...], dt_ref[...]
    bm, cm = b_ref[...].astype(BF16), c_ref[...].astype(BF16)
    a = dtx * a_ref[...]
    arow = dtt_ref[0] * ar_ref[0]
    lowb = low.astype(BF16)
    cs = _dot(lowb, jnp.concatenate(_parts(a, 3), axis=1))
    cs = cs[:, :128] + cs[:, 128:256] + cs[:, 256:]
    csr = _dot(jnp.concatenate([p.astype(F32) for p in _parts(arow, 3)], axis=0).astype(BF16), lowb, NT)
    csr = csr[0:8] + csr[8:16] + csr[16:24]
    last = 0 if rev else CHUNK - 1
    tot = cs[last:last + 1, :]
    xdt = x * dtx
    cb = _dot(cm, bm, NT)
    lmats = [jnp.exp(jnp.where(low, cs[:, HD * h:HD * h + 1] - csr[h:h + 1, :], -1e30)) for h in range(2)]
    return dict(x=x, dtx=dtx, bm=bm, cm=cm, a=a, cs=cs, tot=tot, xdt=xdt, cb=cb, lmats=lmats, low=low, last=last)


SSD_SUB = 8


def _ssd_specs(s, rev_order):
    nblk, rows = s // (SSD_SUB * CHUNK), SSD_SUB * CHUNK
    ci = (lambda c: nblk - 1 - c) if rev_order else (lambda c: c)
    tile = lambda off, div: pl.BlockSpec((rows, 128), lambda p, c: (ci(c), off + p // div))
    common = [tile(0, 1), tile(8, 2), tile(12, 2), tile(0, 1),
              pl.BlockSpec((1, 8, rows), lambda p, c: (p, 0, ci(c))),
              pl.BlockSpec((1, 128), lambda p, c: (0, p)),
              pl.BlockSpec((1, 8, 128), lambda p, c: (p, 0, 0))]
    hs = pl.BlockSpec((1, SSD_SUB, CHUNK, 128), lambda p, c: (p, ci(c), 0, 0))
    return nblk, common, tile(0, 1), hs


def _chunk_rows(ref, j):
    return ref.at[pl.ds(j * CHUNK, CHUNK), :]


def _ssd_chunk(refs, j):
    return [_chunk_rows(r, j) for r in refs[:4]] + [refs[4].at[:, :, pl.ds(j * CHUNK, CHUNK)], refs[5], refs[6]]


def _ssd_args(xbc, t):
    return [xbc, xbc, xbc, t["dt_exp"], t["dtt"], t["a_exp"], t["a_rows"]]


def ssd_fwd(xbc, dirs, name):
    s = xbc.shape[0]
    nd = len(dirs)
    specs = [_ssd_specs(s, t["rev"]) for t in dirs]
    nck = specs[0][0]

    def one(rev, x_ref, b_ref, c_ref, dt_ref, dtt_ref, a_ref, ar_ref, y_ref, hs_ref, h_scr):
        v = _ssd_common(x_ref, b_ref, c_ref, dt_ref, dtt_ref, a_ref, ar_ref, rev)
        xdtb = v["xdt"].astype(BF16)
        yd = _dot(jnp.concatenate([v["cb"] * v["lmats"][h] for h in range(2)], axis=0).astype(BF16), xdtb)
        h_in = h_scr[...]
        hs_ref[0, 0] = h_in
        y_off = _dot(v["cm"], h_in.astype(BF16)) * jnp.exp(v["cs"])
        y_ref[...] = jnp.where(_iota((CHUNK, 128), 1) < HD, yd[:CHUNK], yd[CHUNK:]) + y_off
        decay = jnp.exp(v["tot"] - v["cs"])
        h_scr[...] = jnp.exp(v["tot"]) * h_in + _dot(v["bm"], (v["xdt"] * decay).astype(BF16), TN)

    def body(*refs):
        @pl.when(pl.program_id(1) == 0)
        def _():
            for k in range(nd):
                refs[9 * nd + k][...] = jnp.zeros((CHUNK, 128), F32)

        for k, t in enumerate(dirs):
            y_ref, hs_ref = refs[7 * nd + 2 * k:7 * nd + 2 * k + 2]
            for j in (range(SSD_SUB)[::-1] if t["rev"] else range(SSD_SUB)):
                one(t["rev"], *_ssd_chunk(refs[7 * k:7 * k + 7], j), _chunk_rows(y_ref, j), hs_ref.at[:, pl.ds(j, 1)], refs[9 * nd + k])

    res = _pcall(
        body, name=name, grid=(8, nck), in_specs=[sp for t in specs for sp in t[1]],
        out_specs=[sp for t in specs for sp in (t[2], t[3])],
        out_shape=[jax.ShapeDtypeStruct((s, D), F32), jax.ShapeDtypeStruct((8, s // CHUNK, CHUNK, 128), F32)] * nd,
        scratch_shapes=[pltpu.VMEM((CHUNK, 128), F32)] * nd, compiler_params=_cparams(("parallel", "arbitrary")),
    )(*[a for t in dirs for a in _ssd_args(xbc, t)])
    return [(res[2 * k], res[2 * k + 1]) for k in range(nd)]


def ssd_bwd(xbc, dirs, dy, name):
    s = xbc.shape[0]
    nd = len(dirs)
    specs = [_ssd_specs(s, not t["rev"]) for t in dirs]
    nck = specs[0][0]

    def one(rev, x_ref, b_ref, c_ref, dt_ref, dtt_ref, a_ref, ar_ref, hs_ref, dy_ref,
            dx_ref, ddt_ref, db_ref, dc_ref, dal_ref, dh_scr):
        v = _ssd_common(x_ref, b_ref, c_ref, dt_ref, dtt_ref, a_ref, ar_ref, rev)
        bm, cm, cs, tot, xdt = v["bm"], v["cm"], v["cs"], v["tot"], v["xdt"]
        h_in, dh = hs_ref[0, 0], dh_scr[...]
        dyv = dy_ref[...]
        dyb = dyv.astype(BF16)
        etot, decay, ecs = jnp.exp(tot), jnp.exp(tot - cs), jnp.exp(cs)
        xdtb = xdt.astype(BF16)
        xdec = xdt * decay
        dch = (dyv * ecs).astype(BF16)
        hb, dhb = h_in.astype(BF16), dh.astype(BF16)
        y_off = _dot(cm, hb) * ecs
        dc = _dot(dch, hb, NT)
        dh_y = _dot(cm, dch, TN)
        dxdec = _dot(bm, dhb)
        db = _dot(xdec.astype(BF16), dhb, NT)
        state_term = xdec * dxdec
        dtot = _colsum(dh * h_in) * etot + _colsum(state_term)
        head0 = _iota((CHUNK, 128), 1) < HD
        ii, jj = _iota((CHUNK, CHUNK), 0), _iota((CHUNK, CHUNK), 1)
        low_t = jj <= ii if rev else jj >= ii
        not_low_t = (~low_t).astype(BF16)
        g = _dot(_stack_heads(dyb, head0), xdtb, NT)
        gl = [g[:CHUNK] * v["lmats"][0], g[CHUNK:] * v["lmats"][1]]
        dcb = gl[0] + gl[1]
        dxd = _dot(jnp.concatenate([v["cb"] * v["lmats"][h] for h in range(2)], axis=1).astype(BF16), dyb, TN)
        dxd = jnp.where(head0, dxd[:CHUNK], dxd[CHUNK:])
        w = _dot(not_low_t, jnp.concatenate([gl[h] * v["cb"] for h in range(2)], axis=0).astype(BF16), NT)
        da_l = [jnp.sum(jnp.where(low_t, w[:, CHUNK * h:CHUNK * h + CHUNK], 0.0), axis=1, keepdims=True) for h in range(2)]
        dxdt = dxdec * decay + dxd
        dcbb = dcb.astype(BF16)
        dc_ref[...] = dc + _dot(dcbb, bm)
        db_ref[...] = db + _dot(dcbb, cm, TN)
        dcs = dyv * y_off - state_term + jnp.where(_iota((CHUNK, 128), 0) == v["last"], dtot, 0.0)
        lowb = v["low"].astype(BF16)
        da = _dot(lowb, jnp.concatenate(_parts(dcs, 2), axis=1), TN)
        da = da[:, :128] + da[:, 128:]
        seg = ((ii < HD) == (jj < HD)).astype(BF16)
        sums = _dot(jnp.concatenate(_parts(da, 2) + _parts(dxdt * v["x"], 2), axis=0), seg)
        da = sums[:CHUNK] + sums[CHUNK:2 * CHUNK] + jnp.where(head0, da_l[0], da_l[1])
        ddt_x = sums[2 * CHUNK:3 * CHUNK] + sums[3 * CHUNK:]
        dx_ref[...] = dxdt * v["dtx"]
        ddt_ref[...] = ddt_x + da * a_ref[...]
        dal_ref[0] += _colsum(da * v["a"])
        dh_scr[...] = etot * dh + dh_y

    def body(*refs):
        @pl.when(pl.program_id(1) == 0)
        def _():
            for k in range(nd):
                refs[14 * nd + k][...] = jnp.zeros((CHUNK, 128), F32)
                refs[9 * nd + 5 * k + 4][...] = jnp.zeros((1, 8, 128), F32)

        for k, t in enumerate(dirs):
            ins, outs = refs[9 * k:9 * k + 9], refs[9 * nd + 5 * k:9 * nd + 5 * k + 5]
            for j in (range(SSD_SUB) if t["rev"] else range(SSD_SUB)[::-1]):
                one(t["rev"], *_ssd_chunk(ins[:7], j), ins[7].at[:, pl.ds(j, 1)], _chunk_rows(ins[8], j),
                    *[_chunk_rows(r, j) for r in outs[:4]], outs[4], refs[14 * nd + k])

    acc_spec = pl.BlockSpec((1, 8, 128), lambda p, c: (p, 0, 0))
    res = _pcall(
        body, name=name, grid=(8, nck), in_specs=[sp for t in specs for sp in t[1] + [t[3], t[2]]],
        out_specs=[sp for t in specs for sp in [t[2]] * 4 + [acc_spec]],
        out_shape=([jax.ShapeDtypeStruct((s, D), F32)] * 4 + [jax.ShapeDtypeStruct((8, 8, 128), F32)]) * nd,
        scratch_shapes=[pltpu.VMEM((CHUNK, 128), F32)] * nd, compiler_params=_cparams(("parallel", "arbitrary")),
    )(*[a for t in dirs for a in _ssd_args(xbc, t) + [t["hs"], dy]])
    return [res[5 * k:5 * k + 5] for k in range(nd)]


def _group_norm_stats(g):
    r = [lax.rsqrt(jnp.mean(g[:, 256 * k:256 * k + 256] ** 2, axis=-1, keepdims=True) + EPS) for k in range(4)]
    grp = _iota(g.shape, 1) // 256
    return jnp.where(grp == 0, r[0], jnp.where(grp == 1, r[1], jnp.where(grp == 2, r[2], r[3])))


def _group_mean(t):
    m = [jnp.mean(t[:, 256 * k:256 * k + 256], axis=-1, keepdims=True) for k in range(4)]
    grp = _iota(t.shape, 1) // 256
    return jnp.where(grp == 0, m[0], jnp.where(grp == 1, m[1], jnp.where(grp == 2, m[2], m[3])))


def _mesh_pos():
    return lax.axis_index("x"), lax.axis_index("y"), lax.axis_index("c")


HBM = pl.BlockSpec(memory_space=pltpu.HBM)
SEM = pl.BlockSpec(memory_space=pltpu.SEMAPHORE)
EFFECT = pltpu.SideEffectType.DATAFLOW_SIDE_EFFECTING


def _hbm(t):
    return pltpu.with_memory_space_constraint(t, pltpu.HBM)


def _other_chips(x, y):
    return [(1 - x, y), (x, 1 - y), (1 - x, 1 - y)]


def _peer(x, y, c, m):
    return x ^ (m >> 2), y ^ ((m >> 1) & 1), c ^ (m & 1)


def gather_start(srcs_a, srcs_b, halved=()):
    srcs = [_hbm(t) for t in list(srcs_a) + list(srcs_b)]
    n, na = len(srcs), len(srcs_a)
    half = [k in halved for k in range(n)]
    lands = [_hbm(lax.empty((4,) + (t.shape[1:] if half[k] else t.shape), t.dtype)) for k, t in enumerate(srcs)]

    def body(*refs):
        src, land = refs[:n], refs[n:2 * n]
        sems = refs[2 * n:2 * n + 4]
        x, y, c = _mesh_pos()
        for k in range(n):
            for j, (px, py) in enumerate(_other_chips(x, y)):
                send, recv, idx = (sems[0], sems[1], 3 * k + j) if k < na else (sems[2], sems[3], 3 * (k - na) + j)
                pltpu.make_async_remote_copy(src_ref=src[k].at[c] if half[k] else src[k], dst_ref=land[k].at[2 * x + y], send_sem=send.at[idx],
                                             recv_sem=recv.at[idx], device_id=(px, py, c), device_id_type=MESH).start()

    sem_a, sem_b = pltpu.SemaphoreType.DMA((3 * na,)), pltpu.SemaphoreType.DMA((3 * (n - na),))
    res = _pcall(
        body, name="gather_start", in_specs=[HBM] * (2 * n), out_specs=[SEM] * 4 + [HBM] * (2 * n),
        out_shape=[sem_a, sem_a, sem_b, sem_b] + [pltpu.HBM(t.shape, t.dtype) for t in srcs + lands],
        input_output_aliases={i: 4 + i for i in range(2 * n)},
        compiler_params=pltpu.CompilerParams(has_side_effects=EFFECT),
    )(*srcs, *lands)
    thru_src, thru_land = res[4:4 + n], res[4 + n:]
    return ((res[0], res[1], thru_src[:na], thru_land[:na], half[:na]), (res[2], res[3], thru_src[na:], thru_land[na:], half[na:]))


def gather_wait(group, name, after=None):
    send, recv, srcs, lands, half = group
    n = len(srcs)

    def body(*refs):
        src, land, send_ref, recv_ref = refs[:n], refs[n:2 * n], refs[2 * n], refs[2 * n + 1]
        x, y, c = _mesh_pos()
        for j, (px, py) in enumerate(_other_chips(x, y)):
            for k in range(n):
                cp = pltpu.make_async_remote_copy(src_ref=src[k].at[0] if half[k] else src[k], dst_ref=land[k].at[2 * px + py], send_sem=send_ref.at[3 * k + j],
                                                  recv_sem=recv_ref.at[3 * k + j], device_id=(px, py, c), device_id_type=MESH)
                cp.wait_send()
                cp.wait_recv()

    extra = [] if after is None else [after]
    res = _pcall(
        body, name=name, in_specs=[HBM] * (2 * n) + [SEM, SEM] + [pl.BlockSpec(memory_space=pl.ANY)] * len(extra),
        out_specs=[HBM] * (2 * n), out_shape=[pltpu.HBM(t.shape, t.dtype) for t in list(srcs) + list(lands)],
        input_output_aliases={i: i for i in range(2 * n)}, compiler_params=pltpu.CompilerParams(has_side_effects=EFFECT),
    )(*srcs, *lands, send, recv, *extra)
    return res[:n], res[n:]


def scatter_start(pieces, smalls, name):
    srcs = [_hbm(t) for t in list(pieces) + list(smalls)]
    n, npc = len(srcs), len(pieces)
    lands = [_hbm(lax.empty((8,) + (t.shape[2:] if k < npc else t.shape), t.dtype)) for k, t in enumerate(srcs)]

    def body(*refs):
        src, land, send, recv = refs[:n], refs[n:2 * n], refs[2 * n], refs[2 * n + 1]
        token = refs[-1]
        x, y, c = _mesh_pos()
        for m in range(1, 8):
            px, py, pc = _peer(x, y, c, m)
            for k in range(n):
                s_ref = src[k].at[2 * px + py, pc] if k < npc else src[k]
                d_ref = land[k].at[m] if k < npc else land[k].at[4 * x + 2 * y + c]
                pltpu.make_async_remote_copy(src_ref=s_ref, dst_ref=d_ref, send_sem=send.at[7 * k + m - 1], recv_sem=recv.at[7 * k + m - 1],
                                             device_id=(px, py, pc), device_id_type=MESH).start()
        token[...] = jnp.zeros_like(token)

    sem = pltpu.SemaphoreType.DMA((7 * n,))
    res = _pcall(
        body, name=name, in_specs=[HBM] * (2 * n),
        out_specs=[SEM, SEM] + [HBM] * (2 * n) + [pl.BlockSpec(memory_space=pltpu.VMEM)],
        out_shape=[sem, sem] + [pltpu.HBM(t.shape, t.dtype) for t in srcs + lands] + [jax.ShapeDtypeStruct((8, 128), F32)],
        input_output_aliases={i: 2 + i for i in range(2 * n)},
        compiler_params=pltpu.CompilerParams(has_side_effects=EFFECT),
    )(*srcs, *lands)
    return (res[0], res[1], res[2:2 + n], res[2 + n:2 + 2 * n], npc), res[-1]


def scatter_wait(group, name, after=None):
    send, recv, srcs, lands, npc = group
    n = len(srcs)

    def body(*refs):
        src, land, send_ref, recv_ref = refs[:n], refs[n:2 * n], refs[2 * n], refs[2 * n + 1]
        x, y, c = _mesh_pos()
        for m in range(1, 8):
            px, py, pc = _peer(x, y, c, m)
            for k in range(n):
                s_ref = src[k].at[0, 0] if k < npc else src[k]
                d_ref = land[k].at[m] if k < npc else land[k].at[4 * px + 2 * py + pc]
                cp = pltpu.make_async_remote_copy(src_ref=s_ref, dst_ref=d_ref, send_sem=send_ref.at[7 * k + m - 1],
                                                  recv_sem=recv_ref.at[7 * k + m - 1], device_id=(px, py, pc), device_id_type=MESH)
                cp.wait_send()
                cp.wait_recv()

    extra = [] if after is None else [after]
    res = _pcall(
        body, name=name, in_specs=[HBM] * (2 * n) + [SEM, SEM] + [pl.BlockSpec(memory_space=pl.ANY)] * len(extra),
        out_specs=[HBM] * (2 * n), out_shape=[pltpu.HBM(t.shape, t.dtype) for t in list(srcs) + list(lands)],
        input_output_aliases={i: i for i in range(2 * n)}, compiler_params=pltpu.CompilerParams(has_side_effects=EFFECT),
    )(*srcs, *lands, send, recv, *extra)
    return res[:n], res[n:]


def swap_halves(pieces, name):
    n = len(pieces)
    whole = pl.BlockSpec(memory_space=pltpu.VMEM)

    def body(*refs):
        p_refs, o_refs, send_sems, recv_sems, local_sems = refs[:n], refs[n:2 * n], refs[2 * n], refs[2 * n + 1], refs[2 * n + 2]
        x, y, c = _mesh_pos()
        local = [pltpu.make_async_copy(p_refs[k], o_refs[k].at[c], local_sems.at[k]) for k in range(n)]
        for cp in local:
            cp.start()

        def copy(k, slot):
            return pltpu.make_async_remote_copy(src_ref=p_refs[k], dst_ref=o_refs[k].at[slot], send_sem=send_sems.at[k],
                                                recv_sem=recv_sems.at[k], device_id=(x, y, 1 - c), device_id_type=MESH)

        for k in range(n):
            copy(k, c).start()
        for k in range(n):
            copy(k, 1 - c).wait_recv()
        for k in range(n):
            copy(k, c).wait_send()
        for cp in local:
            cp.wait()

    return _pcall(
        body, name=name, in_specs=[whole] * n, out_specs=[whole] * n,
        out_shape=[jax.ShapeDtypeStruct((2,) + t.shape, t.dtype) for t in pieces],
        scratch_shapes=[pltpu.SemaphoreType.DMA((n,)), pltpu.SemaphoreType.DMA((n,)), pltpu.SemaphoreType.DMA((n,))],
        compiler_params=_cparams(),
    )(*pieces)


def adamw(w, g, m, v, name):
    rows, cols = w.shape
    tm = rows
    for t in (256, 352, 128, 144, 64, 32, 16, 8):
        if rows % t == 0:
            tm = t
            break

    def fn(i, nrow, wv, gv, mv, vv):
        mn = ADAM_B1 * mv + (1.0 - ADAM_B1) * gv
        vn = ADAM_B2 * vv + (1.0 - ADAM_B2) * (gv * gv)
        m_hat = mn / (1.0 - ADAM_B1 ** ADAM_STEP)
        v_hat = vn / (1.0 - ADAM_B2 ** ADAM_STEP)
        delta = -ADAM_LR * (m_hat / (jnp.sqrt(v_hat) + ADAM_EPS) + ADAM_WD * wv)
        return delta, mn, vn

    return ew(fn, name, rows, tm, 1, [(t, "row", cols, 0) for t in (w, g, m, v)], [(cols, F32, cols)] * 3)


REST = ("w_out", "w_up", "w_down")
SMALL = ("norm1_w", "ssm_conv_w", "ssm_conv_b", "a_log_f", "a_log_b", "dt_bias_f", "dt_bias_b", "d_skip",
         "ssm_norm_w", "norm2_w", "ffn_conv_w", "ffn_conv_b", "final_norm_w")
WEIGHTS = ("norm1_w", "w_in", "ssm_conv_w", "ssm_conv_b", "a_log_f", "a_log_b", "dt_bias_f", "dt_bias_b", "d_skip",
           "ssm_norm_w", "w_out", "norm2_w", "w_up", "ffn_conv_w", "ffn_conv_b", "w_down", "final_norm_w")
INPUTS = ("x",) + WEIGHTS + ("loss_target",) + tuple("m_" + n for n in WEIGHTS) + tuple("v_" + n for n in WEIGHTS)


def _flat_rows(parts, width, rows):
    flat = jnp.concatenate([p.reshape(-1) for p in parts])
    return jnp.pad(flat, (0, rows * width - flat.shape[0])).reshape(rows, width)


def _split_flat(flat, shapes):
    out, pos = [], 0
    flat = flat.reshape(-1)
    for shp in shapes:
        n = int(np.prod(shp))
        out.append(flat[pos:pos + n].reshape(shp))
        pos += n
    return out


def _col_shards(t, nshard):
    r, c = t.shape
    return t.reshape(r, nshard, c // nshard).transpose(1, 0, 2)


def _row_shards(t, nshard):
    r, c = t.shape
    return t.reshape(nshard, r // nshard, c)


def kernel(x, norm1_w, w_in, ssm_conv_w, ssm_conv_b, a_log_f, a_log_b, dt_bias_f, dt_bias_b, d_skip, ssm_norm_w, w_out, norm2_w, w_up, ffn_conv_w, ffn_conv_b, w_down, final_norm_w, loss_target, m_norm1_w, m_w_in, m_ssm_conv_w, m_ssm_conv_b, m_a_log_f, m_a_log_b, m_dt_bias_f, m_dt_bias_b, m_d_skip, m_ssm_norm_w, m_w_out, m_norm2_w, m_w_up, m_ffn_conv_w, m_ffn_conv_b, m_w_down, m_final_norm_w, v_norm1_w, v_w_in, v_ssm_conv_w, v_ssm_conv_b, v_a_log_f, v_a_log_b, v_dt_bias_f, v_dt_bias_b, v_d_skip, v_ssm_norm_w, v_w_out, v_norm2_w, v_w_up, v_ffn_conv_w, v_ffn_conv_b, v_w_down, v_final_norm_w):
    p = dict(zip(INPUTS, (x, norm1_w, w_in, ssm_conv_w, ssm_conv_b, a_log_f, a_log_b, dt_bias_f, dt_bias_b, d_skip, ssm_norm_w, w_out, norm2_w, w_up, ffn_conv_w, ffn_conv_b, w_down, final_norm_w, loss_target, m_norm1_w, m_w_in, m_ssm_conv_w, m_ssm_conv_b, m_a_log_f, m_a_log_b, m_dt_bias_f, m_dt_bias_b, m_d_skip, m_ssm_norm_w, m_w_out, m_norm2_w, m_w_up, m_ffn_conv_w, m_ffn_conv_b, m_w_down, m_final_norm_w, v_norm1_w, v_w_in, v_ssm_conv_w, v_ssm_conv_b, v_a_log_f, v_a_log_b, v_dt_bias_f, v_dt_bias_b, v_d_skip, v_ssm_norm_w, v_w_out, v_norm2_w, v_w_up, v_ffn_conv_w, v_ffn_conv_b, v_w_down, v_final_norm_w)))
    x = p["x"][0]
    tgt = p["loss_target"][0]
    s = x.shape[0]
    chip = 2 * lax.axis_index("x") + lax.axis_index("y")

    own_slot = lambda land, mine, slot: lax.dynamic_update_slice_in_dim(land, mine[None], slot, axis=0)
    core = lax.axis_index("c")
    src_in = p["w_in"][0].astype(BF16).reshape(2, D // 2, -1)
    src_rest = [p[n][0].astype(BF16) for n in REST]
    small_w = _flat_rows([p["ssm_conv_w"][0], p["ffn_conv_w"][0]], 128, 48)
    gather_in, gather_rest = gather_start([src_in, small_w], src_rest, halved=(0,))
    (src_in, small_w), (wg_in, sg) = gather_wait(gather_in, "gather_wait_in")
    wg_in = own_slot(wg_in, lax.dynamic_index_in_dim(src_in, core, 0, keepdims=False), chip)
    wg_in, = swap_halves([wg_in], "swap_w_in_rows")
    w_in = wg_in.transpose(0, 2, 1, 3).reshape(D, -1)
    sg = own_slot(sg, small_w, chip)
    n_in = w_in.shape[1]
    n_main = 6 * D
    w_dt = jnp.pad(w_in[:, n_main:], ((0, 0), (0, 128 - (n_in - n_main))))
    sgf = sg.reshape(4, -1)
    n_sc, n_fc = p["ssm_conv_w"].shape[1], p["ffn_conv_w"].shape[1]
    ssm_cw = sgf[:, :n_sc * 3].reshape(-1, 3).T
    ffn_cw = sgf[:, n_sc * 3:(n_sc + n_fc) * 3].reshape(-1, 3).T
    ssm_cb, ffn_cb = p["ssm_conv_b"], p["ffn_conv_b"]
    n1w, n2w, snw, fnw = p["norm1_w"], p["norm2_w"], p["ssm_norm_w"], p["final_norm_w"].reshape(1, D)

    h1, = ew(lambda i, n, xv, w: _rms_fwd(xv, w), "rms1", s, 512, 1,
             [(x, "row", D, 0), (n1w, "const", D, 0)], [(D, BF16, D)])
    proj = matmul(h1, w_in, "nn", "in_proj", n_cols=n_main)
    proj_dt = matmul(h1, w_dt, "nn", "in_proj_dt")
    tabs = _rope_tables(s)
    attn, lse = attn_fwd_all(proj, tabs, "attn_fwd")

    def conv_silu_fn(i, n, xv, xp, xn, w, b):
        return _silu(w[0:1] * _shift_down(xv, xp, i) + w[1:2] * xv + w[2:3] * _shift_up(xv, xn, i, n) + b)

    xbc_act, = ew(conv_silu_fn, "ssm_conv", s, 256, 2,
                  [(proj, "row", D, 4), (proj, "prev", D, 4), (proj, "next", D, 4),
                   (ssm_cw, "const", D, 0), (ssm_cb, "const", D, 0)], [(2 * D, F32, D)])
    dt_bias = jnp.pad(jnp.concatenate([p["dt_bias_f"], p["dt_bias_b"]], axis=1), ((0, 0), (0, 96)))

    lanes_of = np.arange(128)[:, None] == np.arange(D)[None, :] // HD
    spread = [jnp.asarray(np.roll(lanes_of, 16 * k, axis=0), BF16) for k in range(2)]

    def softplus_fn(i, n, r, b, ef, eb):
        t = r + b
        dtv = jnp.maximum(t, 0.0) + jnp.log(1.0 + jnp.exp(-jnp.abs(t)))
        parts = _parts(dtv, 3)
        return dtv, sum(_dot(q, ef) for q in parts), sum(_dot(q, eb) for q in parts)

    dt, dt_exp_f, dt_exp_b = ew(softplus_fn, "dt_softplus", s, 512, 1,
                                [(proj_dt, "row", 128, 0), (dt_bias, "const", 128, 0), (spread[0], "const", D, 0), (spread[1], "const", D, 0)],
                                [(128, F32, 128), (D, F32, D), (D, F32, D)])
    d_exp = jnp.repeat(p["d_skip"], HD, axis=1)
    ssd = []
    for k, (a_log, rev) in enumerate(((p["a_log_f"], False), (p["a_log_b"], True))):
        dt_k = dt[:, 16 * k:16 * k + 16]
        a_head = -jnp.exp(a_log)
        dt_exp = (dt_exp_f, dt_exp_b)[k]
        dtt = jnp.pad(dt_k.T.reshape(8, 2, s), ((0, 0), (0, 6), (0, 0)))
        a_exp = jnp.repeat(a_head, HD, axis=1)
        a_rows = jnp.broadcast_to(jnp.pad(a_head.reshape(8, 2), ((0, 0), (0, 6)))[:, :, None], (8, 8, 128))
        ssd.append(dict(dt_exp=dt_exp, dtt=dtt, a_exp=a_exp, a_rows=a_rows, rev=rev))
    for t, (y_k, hs_k) in zip(ssd, ssd_fwd(xbc_act, ssd, "ssd_fwd")):
        t["y"], t["hs"] = y_k, hs_k

    def gate_fn(i, n, yf, yb, xs, z, dsk, w):
        g = (yf + yb + dsk * xs) * _silu(z)
        return g * _group_norm_stats(g) * w

    ssm_out, = ew(gate_fn, "ssm_gate_norm", s, 256, 1,
                  [(ssd[0]["y"], "row", D, 0), (ssd[1]["y"], "row", D, 0), (xbc_act, "row", D, 0), (proj, "row", D, 3),
                   (d_exp, "const", D, 0), (snw, "const", D, 0)], [(D, F32, D)])
    mix = jnp.concatenate([attn, ssm_out], axis=1).astype(BF16)
    src_rest, wg_rest = gather_wait(gather_rest, "gather_wait_rest", after=mix)
    wg_rest = [own_slot(land, mine, chip) for land, mine in zip(wg_rest, src_rest)]
    w_out = wg_rest[0].reshape(-1, D)
    w_up = wg_rest[1].transpose(1, 0, 2).reshape(D, -1)
    w_down = wg_rest[2].reshape(-1, D)
    mix_w = matmul(mix, w_out, "nn", "out_proj")

    def res_rms_fn(i, n, xv, mw, w):
        x1v = xv + mw
        return x1v, _rms_fwd(x1v, w)

    x1, h2 = ew(res_rms_fn, "res_rms2", s, 512, 1, [(x, "row", D, 0), (mix_w, "row", D, 0), (n2w, "const", D, 0)],
                [(D, F32, D), (D, BF16, D)])
    hw = matmul(h2, w_up, "nn", "ffn_up")
    fw = D_FF // 2
    nfb = D_FF // fw
    ffn_conv_ins = [(hw, "row", fw, 0), (hw, "prev", fw, 0), (hw, "next", fw, 0),
                    (hw, "row", fw, nfb), (hw, "prev", fw, nfb), (hw, "next", fw, nfb),
                    (ffn_cw, "const", fw, 0), (ffn_cw, "const", fw, nfb), (ffn_cb, "const", fw, 0), (ffn_cb, "const", fw, nfb)]

    def ffn_conv(i, n, g, gp, gn, u, up_, un, wg_, wu, bg, bu):
        gs = (_shift_down(g, gp, i), g, _shift_up(g, gn, i, n))
        us = (_shift_down(u, up_, i), u, _shift_up(u, un, i, n))
        gate = wg_[0:1] * gs[0] + wg_[1:2] * gs[1] + wg_[2:3] * gs[2] + bg
        upv = wu[0:1] * us[0] + wu[1:2] * us[1] + wu[2:3] * us[2] + bu
        return gate, upv, gs, us

    def glu_fn(i, n, *blocks):
        gate, upv, _, _ = ffn_conv(i, n, *blocks)
        return _silu(gate) * upv

    act, = ew(glu_fn, "ffn_conv_glu", s, 256, nfb, ffn_conv_ins, [(D_FF, BF16, fw)])
    ffn = matmul(act, w_down, "nn", "ffn_down")

    def head_fn(i, n, x1v, fv, tv, w):
        x2 = x1v + fv
        r = lax.rsqrt(jnp.mean(x2 * x2, axis=-1, keepdims=True) + EPS)
        xh = x2 * r
        diff = xh * w - tv
        loss = 0.5 * jnp.sum(jnp.mean(diff * diff, axis=-1, keepdims=True), axis=0, keepdims=True)
        dout = diff * (1.0 / D)
        dxh = dout * w
        dx2 = r * (dxh - xh * jnp.mean(dxh * xh, axis=-1, keepdims=True))
        return dx2, jnp.broadcast_to(loss, (1, 128)), _colsum(dout * xh)

    dx2, loss_acc, g_fnw = ew(head_fn, "loss_head", s, 512, 1,
                              [(x1, "row", D, 0), (ffn, "row", D, 0), (tgt, "row", D, 0), (fnw, "const", D, 0)],
                              [(D, F32, D)], [(128, 128), (D, D)])
    loss = lax.psum(loss_acc[0, 0], ("x", "y", "c"))

    g_w_down = matmul(act, dx2, "tn", "d_w_down")
    dact = matmul(dx2, w_down, "nt", "d_act")

    res = ew(ffn_conv_bwd_fn, "ffn_conv_glu_bwd", s, 256, nfb,
             ffn_conv_ins + [(dact, "row", fw, 0), (dact, "prev", fw, 0), (dact, "next", fw, 0)],
             [(D_FF, F32, fw)] * 2, [(D_FF, fw)] * 8)
    dhw_g, dhw_u = res[0], res[1]
    g_ffn_cw = jnp.concatenate([jnp.concatenate(res[2:5], axis=0), jnp.concatenate(res[5:8], axis=0)], axis=1).T
    g_ffn_cb = jnp.concatenate([res[8], res[9]], axis=1)

    g_w_up = jnp.concatenate([matmul(h2, dhw_g, "tn", "d_w_up_gate"), matmul(h2, dhw_u, "tn", "d_w_up_up")], axis=1)
    dh2_a = matmul(dhw_g, w_up, "nt", "d_h2_gate")
    dh2_b = matmul(dhw_u, w_up, "nt", "d_h2_up", b_k_off=D_FF // _pick(D_FF, 1408))

    def res_rms_bwd_fn(i, n, dres, da, db, xin, w):
        dx, dw = _rms_bwd(da + db, xin, w)
        return dres + dx, dw

    dx1, g_n2w = ew(res_rms_bwd_fn, "res_rms2_bwd", s, 512, 1,
                    [(dx2, "row", D, 0), (dh2_a, "row", D, 0), (dh2_b, "row", D, 0), (x1, "row", D, 0), (n2w, "const", D, 0)],
                    [(D, F32, D)], [(D, D)])

    g_w_out = matmul(mix, dx1, "tn", "d_w_out")
    to_pieces = lambda t: t.astype(BF16).reshape(4, 2, t.shape[1] // 2, t.shape[2])
    shards_rest = [_row_shards(g_w_out, 4), _col_shards(g_w_up, 4), _row_shards(g_w_down, 4)]
    scatter_rest, token = scatter_start([to_pieces(t) for t in shards_rest], [], "scatter_start_rest")
    dmix = matmul(dx1, w_out, "nt", "d_mix", after=token)
    ii, jj = np.arange(D)[:, None] // HD, np.arange(D)[None, :] // HD
    seg = jnp.asarray(ii == jj, BF16)

    def gate_bwd_fn(i, n, dout, yf, yb, xs, z, dsk, w, segm):
        yt = yf + yb + dsk * xs
        sz = _silu(z)
        g = yt * sz
        r = _group_norm_stats(g)
        gh = g * r
        dn = dout * w
        dg = r * (dn - gh * _group_mean(dn * gh))
        dy = dg * sz
        dsk_lane = jnp.broadcast_to(_colsum(dy * xs), (8, D))
        return dy, dg * yt * _dsilu(z), _colsum(dout * gh), sum(_dot(q, segm) for q in _parts(dsk_lane, 2))[0:1]

    dy, dz, g_snw, g_dskip_l = ew(
        gate_bwd_fn, "ssm_gate_norm_bwd", s, 256, 1,
        [(dmix, "row", D, 1), (ssd[0]["y"], "row", D, 0), (ssd[1]["y"], "row", D, 0), (xbc_act, "row", D, 0),
         (proj, "row", D, 3), (d_exp, "const", D, 0), (snw, "const", D, 0), (seg, "const", D, 0)],
        [(D, F32, D), (D, BF16, D)], [(D, D)] * 2)
    sb = ssd_bwd(xbc_act, ssd, dy, "ssd_bwd")

    def dxbc_act_fn(i, n, dxf, dxb, dyv, dsk, dbf, dbb, dcf, dcb_):
        db, dc = dbf + dbb, dcf + dcb_
        db = [db[:, 256 * g:256 * g + 128] + db[:, 256 * g + 128:256 * g + 256] for g in range(4)]
        dc = [dc[:, 256 * g:256 * g + 128] + dc[:, 256 * g + 128:256 * g + 256] for g in range(4)]
        return jnp.concatenate([dxf + dxb + dyv * dsk] + db + dc, axis=1)

    dxbc_act, = ew(dxbc_act_fn, "d_xbc_act", s, 256, 1,
                   [(sb[0][0], "row", D, 0), (sb[1][0], "row", D, 0), (dy, "row", D, 0), (d_exp, "const", D, 0),
                    (sb[0][2], "row", D, 0), (sb[1][2], "row", D, 0), (sb[0][3], "row", D, 0), (sb[1][3], "row", D, 0)],
                   [(2 * D, F32, 2 * D)])

    res = ew(silu_conv_bwd_fn, "ssm_conv_bwd", s, 256, 2,
             [(proj, "row", D, 4), (proj, "prev", D, 4), (proj, "next", D, 4), (ssm_cw, "const", D, 0), (ssm_cb, "const", D, 0),
              (dxbc_act, "row", D, 0), (dxbc_act, "prev", D, 0), (dxbc_act, "next", D, 0)], [(2 * D, BF16, D)], [(2 * D, D)] * 4)
    dxbc = res[0]
    g_ssm_cw = jnp.concatenate(res[1:4], axis=0).T
    g_ssm_cb = res[4]
    pick = np.zeros((D, 128), np.float32)
    pick[np.arange(16) * HD, np.arange(16)] = 1.0
    picks = [jnp.asarray(np.roll(pick, 16 * k, axis=1), BF16) for k in range(2)]

    def dt_bwd_fn(i, n, ddf, ddb, r, b, pf, pb):
        dd = sum(_dot(q, pf) for q in _parts(ddf, 3)) + sum(_dot(q, pb) for q in _parts(ddb, 3))
        dr = dd * _sigmoid(r + b)
        return dr, _colsum(dr)

    dproj_dt, g_dt_bias = ew(dt_bwd_fn, "dt_softplus_bwd", s, 512, 1,
                             [(sb[0][1], "row", D, 0), (sb[1][1], "row", D, 0), (proj_dt, "row", 128, 0), (dt_bias, "const", 128, 0),
                              (picks[0], "const", 128, 0), (picks[1], "const", 128, 0)],
                             [(128, F32, 128)], [(128, 128)])
    g_a_log = [t[4][:, 0, ::HD].reshape(1, 16) for t in sb]

    dq, dk, dv = attn_bwd_all(proj, tabs, dmix, attn, lse, "attn_bwd")

    dproj = jnp.concatenate([t.astype(BF16) for t in (dq, dk, dv, dz, dxbc)], axis=1)
    g_w_in = jnp.concatenate([matmul(h1, dproj, "tn", "d_w_in"), matmul(h1, dproj_dt, "tn", "d_w_in_dt")[:, :n_in - n_main]], axis=1)
    scatter_in, token = scatter_start([to_pieces(_col_shards(g_w_in, 4))], [], "scatter_start_in")
    dh1_a = matmul(dproj, w_in, "nt", "d_h1", after=token)
    dh1_b = matmul(dproj_dt, w_dt, "nt", "d_h1_dt")
    grad_x, g_n1w = ew(res_rms_bwd_fn, "rms1_bwd", s, 512, 1,
                       [(dx1, "row", D, 0), (dh1_a, "row", D, 0), (dh1_b, "row", D, 0), (x, "row", D, 0), (n1w, "const", D, 0)],
                       [(D, F32, D)], [(D, D)])

    small_g = {"norm1_w": g_n1w, "ssm_conv_w": g_ssm_cw, "ssm_conv_b": g_ssm_cb, "a_log_f": g_a_log[0], "a_log_b": g_a_log[1],
               "dt_bias_f": g_dt_bias[:, :16], "dt_bias_b": g_dt_bias[:, 16:32], "d_skip": g_dskip_l[:, ::HD],
               "ssm_norm_w": g_snw, "norm2_w": g_n2w, "ffn_conv_w": g_ffn_cw, "ffn_conv_b": g_ffn_cb, "final_norm_w": g_fnw}
    small_shapes = [small_g[n].shape for n in SMALL]
    scatter_small, token = scatter_start([], [_flat_rows([small_g[n] for n in SMALL], 128, SMALL_ROWS)], "scatter_start_small")

    def sum8_fn(i, n, *v):
        t = v[0].astype(F32)
        for u in v[1:]:
            t = t + u.astype(F32)
        return t

    def sum_pieces(sent, got, name):
        rows, w = got.shape[1:]
        tm = 256 if rows % 256 == 0 else rows
        mine = lax.dynamic_slice(sent, (chip, core, 0, 0), (1, 1, rows, w)).reshape(rows, w)
        ins = [(mine, "row", w, 0)] + [(got.reshape(8 * rows, w), "row", w, 0, k * (rows // tm)) for k in range(1, 8)]
        return ew(sum8_fn, name, rows, tm, 1, ins, [(w, F32, w)])[0]

    grads, delta, new_m, new_v = {}, {}, {}, {}

    def finish(names, sent, got, tag):
        summed = swap_halves([sum_pieces(a, b, "sum_pieces_" + n) for a, b, n in zip(sent, got, names)], "swap_halves_" + tag)
        for n, t in zip(names, summed):
            shp = p[n].shape
            grads[n] = t.reshape(shp)
            r = [u.reshape(shp[1:]) for u in (p[n], grads[n], p["m_" + n], p["v_" + n])]
            delta[n], new_m[n], new_v[n] = [u.reshape(shp) for u in adamw(*r, "adamw_" + n)]

    finish(REST, *scatter_wait(scatter_rest, "scatter_wait_rest", after=token), "rest")
    finish(("w_in",), *scatter_wait(scatter_in, "scatter_wait_in", after=new_v[REST[-1]]), "w_in")
    (sent_small,), (got_small,) = scatter_wait(scatter_small, "scatter_wait_small", after=new_v["w_in"])
    got_small = own_slot(got_small, sent_small, 2 * chip + core)
    small_sum, = ew(sum8_fn, "sum_small", SMALL_ROWS, SMALL_ROWS, 1,
                    [(got_small.reshape(8 * SMALL_ROWS, 128), "row", 128, 0, k) for k in range(8)], [(128, F32, 128)])
    for n, g in zip(SMALL, _split_flat(small_sum, small_shapes)):
        if n in ("ssm_conv_w", "ffn_conv_w"):
            rows = p[n].shape[1]
            g = lax.dynamic_slice_in_dim(g, chip * rows, rows, axis=0)
        grads[n] = g.reshape(p[n].shape)

    shapes = [p[n].shape for n in SMALL]
    total = sum(int(np.prod(sh)) for sh in shapes)
    rows = -(-total // 1024) * 8
    packs = [_flat_rows([t[n] for n in SMALL], 128, rows)
             for t in (p, grads, {n: p["m_" + n] for n in SMALL}, {n: p["v_" + n] for n in SMALL})]
    for dst, t in zip((delta, new_m, new_v), adamw(*packs, "adamw_small")):
        for n, u in zip(SMALL, _split_flat(t, shapes)):
            dst[n] = u
    return (loss, grad_x[None], *[grads[n] for n in WEIGHTS], *[delta[n] for n in WEIGHTS],
            *[new_m[n] for n in WEIGHTS], *[new_v[n] for n in WEIGHTS])
```
